```python
import jax, jax.numpy as jnp
from jax import lax
import numpy as np

D_MODEL = 1024
BATCH = 8
SEQ = 2048
DEPTH = 2
DEC_BATCH = 128
DEC_SEQ = 4
PAST_LEN = 16384
PAGE_SIZE = 128

D_MIX = D_MODEL
D_A = D_MIX // 4
D_B = D_MIX // 4
D_C = D_MIX // 4
D_D = D_MIX - D_A - D_B - D_C
CONV_A_W = 31
CONV_C_W = 3
CHUNK = 128
N_HEADS_B = 4
HD_B = D_B // N_HEADS_B
POOL_WINDOWS = (2, 4, 8, 16)
N_POOL_GROUPS = 4
POOL_GC = D_D // N_POOL_GROUPS
POOL_PAST = 15
D_IN = 2 * D_A + 2 * D_B + 3 * D_C + D_D
SPLIT_POINTS = (D_A, 2 * D_A, 2 * D_A + D_B, 2 * D_A + 2 * D_B, 2 * D_A + 2 * D_B + D_C,
                2 * D_A + 2 * D_B + 2 * D_C, 2 * D_A + 2 * D_B + 3 * D_C)
D_FF = (7 * D_MODEL) // 2
N_EXPERTS = 8
TOP_K = 2
MOE_BLOCK = 128
N_DENSE = (DEPTH + 1) // 2
N_MOE = DEPTH // 2
ALPHA = (2.0 * DEPTH) ** 0.25
BETA = (8.0 * DEPTH) ** -0.25
LN_EPS = 1e-5

kernel_name = "hymba_conv_gmlp_pool_deepnorm_decode"


def layer_norm(x, g, b):
    xf = x.astype(jnp.float32)
    mu = jnp.mean(xf, axis=-1, keepdims=True)
    var = jnp.mean(jnp.square(xf - mu), axis=-1, keepdims=True)
    return ((xf - mu) * lax.rsqrt(var + LN_EPS) * g + b).astype(x.dtype)


def causal_depthwise_conv(past, cur, w):
    width, ch = w.shape
    ext = jnp.concatenate([past, cur], axis=1)
    y = lax.conv_general_dilated(ext, w[:, None, :].astype(ext.dtype), window_strides=(1,),
                                 padding='VALID', dimension_numbers=('NWC', 'WIO', 'NWC'),
                                 feature_group_count=ch)
    return y, ext[:, ext.shape[1] - (width - 1):]


def chunk_gmlp(u, v, w_s, b_s):
    bsz, L, _ = v.shape
    n_chunks = -(-L // CHUNK)
    pad = n_chunks * CHUNK - L
    vp = jnp.pad(v, ((0, 0), (0, pad), (0, 0))).reshape(bsz, n_chunks, CHUNK, N_HEADS_B, HD_B)
    mask = jnp.tril(jnp.ones((CHUNK, CHUNK), dtype=bool))
    ws = jnp.where(mask[None], w_s, 0)
    s = jnp.einsum('hij,bcjhd->bcihd', ws, vp) + jnp.transpose(b_s)[None, None, :, :, None]
    s = s.reshape(bsz, n_chunks * CHUNK, D_B)[:, :L]
    return u * s


def pool_mixer(past, cur, w_pool, scale, start_pos):
    bsz, L, _ = cur.shape
    ext = jnp.concatenate([past, cur], axis=1)
    cs = jnp.pad(jnp.cumsum(ext.astype(jnp.float32), axis=1), ((0, 0), (1, 0), (0, 0)))
    end = cs[:, POOL_PAST + 1:]
    pos = (start_pos + jnp.arange(L)).astype(jnp.float32)
    means = []
    for g, w in enumerate(POOL_WINDOWS):
        sl = slice(g * POOL_GC, (g + 1) * POOL_GC)
        start = cs[:, POOL_PAST + 1 - w:POOL_PAST + 1 - w + L, sl]
        count = jnp.minimum(pos + 1.0, float(w))[None, :, None]
        means.append((end[..., sl] - start) / count)
    mean = jnp.concatenate(means, axis=-1)
    d = (mean - cur.astype(jnp.float32)).astype(cur.dtype).reshape(bsz, L, N_POOL_GROUPS, POOL_GC)
    y = jnp.einsum('blgc,gcd->blgd', d, w_pool).reshape(bsz, L, D_D) * scale
    return y, ext[:, ext.shape[1] - POOL_PAST:]


def dense_swiglu(h, w_gate, w_up, w_down):
    return (jax.nn.silu(h @ w_gate) * (h @ w_up)) @ w_down


def moe_swiglu(h, w_router, w_gate, w_up, w_down):
    T, D = h.shape
    logits = (h @ w_router).astype(jnp.float32)
    top_logit, top_e = lax.top_k(logits, TOP_K)
    gates = jax.nn.softmax(top_logit, axis=-1).astype(h.dtype)
    n_assign = T * TOP_K
    flat_e = top_e.reshape(-1)
    order = jnp.argsort(flat_e)
    sorted_e = flat_e[order]
    sorted_tok = (order // TOP_K).astype(jnp.int32)
    counts = jnp.bincount(flat_e, length=N_EXPERTS).astype(jnp.int32)
    padded = (counts + MOE_BLOCK - 1) // MOE_BLOCK * MOE_BLOCK
    pad_end = jnp.cumsum(padded)
    pad_start = pad_end - padded
    start = jnp.cumsum(counts) - counts
    dest = pad_start[sorted_e] + (jnp.arange(n_assign, dtype=jnp.int32) - start[sorted_e])
    cap = (-(-n_assign // MOE_BLOCK) + N_EXPERTS) * MOE_BLOCK
    n_blocks = cap // MOE_BLOCK
    slot_tok = jnp.full((cap,), T, dtype=jnp.int32).at[dest].set(sorted_tok)
    block_e = jnp.minimum(
        jnp.searchsorted(pad_end, jnp.arange(n_blocks, dtype=jnp.int32) * MOE_BLOCK, side='right'),
        N_EXPERTS - 1)
    h_pad = jnp.concatenate([h, jnp.zeros((1, D), h.dtype)], axis=0)

    def expert_block(args):
        tok, e = args
        xb = h_pad[tok]
        return (jax.nn.silu(xb @ w_gate[e]) * (xb @ w_up[e])) @ w_down[e]

    y_slots = lax.map(expert_block, (slot_tok.reshape(n_blocks, MOE_BLOCK), block_e))
    y_assign = y_slots.reshape(cap, D)[dest]
    w_assign = gates.reshape(-1)[order]
    return jnp.zeros((T, D), h.dtype).at[sorted_tok].add(y_assign * w_assign[:, None])


def trunk(x, past_a, past_c, past_d, start_pos, p):
    bsz, L, _ = x.shape
    h = layer_norm(x, p['ln_in_g'], p['ln_in_b'])
    end_pos = start_pos + L
    n_cur = end_pos - ((end_pos - 1) // CHUNK) * CHUNK
    outs_a, outs_c, outs_d, outs_v = [], [], [], []
    for i in range(DEPTH):
        z = h @ p['w_in'][i]
        a_val, a_gate, b_u, b_v, c_x, c_b, c_c, d_in = jnp.split(z, SPLIT_POINTS, axis=-1)
        a_glu = a_val * jax.nn.sigmoid(a_gate)
        a_conv, buf_a = causal_depthwise_conv(past_a[i], a_glu, p['conv_a_w'][i])
        y_a = jax.nn.silu(layer_norm(a_conv + p['conv_a_b'][i], p['ln_a_g'][i], p['ln_a_b'][i]))
        u = jax.nn.gelu(b_u, approximate=False)
        v = layer_norm(jax.nn.gelu(b_v, approximate=False), p['ln_v_g'][i], p['ln_v_b'][i])
        y_b = chunk_gmlp(u, v, p['w_s'][i], p['b_s'][i])
        c_conv, buf_c = causal_depthwise_conv(past_c[i], c_c * c_x, p['conv_c_w'][i])
        y_c = c_b * c_conv
        y_d, buf_d = pool_mixer(past_d[i], d_in, p['w_pool'][i], p['pool_scale'][i], start_pos)
        mix = jnp.concatenate([y_a, y_b, y_c, y_d], axis=-1) @ p['w_out'][i]
        h = layer_norm(ALPHA * h + mix, p['ln_mix_g'][i], p['ln_mix_b'][i])
        flat = h.reshape(bsz * L, D_MODEL)
        j = i // 2
        if i % 2 == 0:
            f = dense_swiglu(flat, p['w_ff_gate'][j], p['w_ff_up'][j], p['w_ff_down'][j])
        else:
            f = moe_swiglu(flat, p['w_router'][j], p['w_e_gate'][j], p['w_e_up'][j], p['w_e_down'][j])
        h = layer_norm(ALPHA * h + f.reshape(bsz, L, D_MODEL), p['ln_ffn_g'][i], p['ln_ffn_b'][i])
        outs_a.append(buf_a)
        outs_c.append(buf_c)
        outs_d.append(buf_d)
        outs_v.append(v[:, L - n_cur:])
    return h, jnp.stack(outs_a), jnp.stack(outs_c), jnp.stack(outs_d), jnp.stack(outs_v)


def setup_inputs(seed: int = 0) -> dict:
    key = jax.random.key(seed)
    ks = list(jax.random.split(key, 40))

    def nrm(idx, shape, scale):
        return jax.random.normal(ks[idx], shape, jnp.float32) * scale

    return {
        'x_prompt': nrm(0, (BATCH, SEQ, D_MODEL), 1.0),
        'x_sample': nrm(1, (DEC_BATCH, DEC_SEQ, D_MODEL), 1.0),
        'state_conv_a': nrm(2, (DEPTH, DEC_BATCH, CONV_A_W - 1, D_A), 0.5),
        'state_conv_c': nrm(3, (DEPTH, DEC_BATCH, CONV_C_W - 1, D_C), 1.0),
        'state_pool_d': nrm(4, (DEPTH, DEC_BATCH, POOL_PAST, D_D), 1.0),
        'ln_in_g': 1.0 + nrm(5, (D_MODEL,), 0.05),
        'ln_in_b': nrm(6, (D_MODEL,), 0.02),
        'w_in': nrm(7, (DEPTH, D_MODEL, D_IN), D_MODEL ** -0.5),
        'conv_a_w': nrm(8, (DEPTH, CONV_A_W, D_A), CONV_A_W ** -0.5),
        'conv_a_b': nrm(9, (DEPTH, D_A), 0.02),
        'ln_a_g': 1.0 + nrm(10, (DEPTH, D_A), 0.05),
        'ln_a_b': nrm(11, (DEPTH, D_A), 0.02),
        'ln_v_g': 1.0 + nrm(12, (DEPTH, D_B), 0.05),
        'ln_v_b': nrm(13, (DEPTH, D_B), 0.02),
        'w_s': nrm(14, (DEPTH, N_HEADS_B, CHUNK, CHUNK), CHUNK ** -0.5),
        'b_s': 1.0 + nrm(15, (DEPTH, N_HEADS_B, CHUNK), 0.05),
        'conv_c_w': nrm(16, (DEPTH, CONV_C_W, D_C), CONV_C_W ** -0.5),
        'w_pool': nrm(17, (DEPTH, N_POOL_GROUPS, POOL_GC, POOL_GC), POOL_GC ** -0.5),
        'pool_scale': 1.0 + nrm(18, (DEPTH, D_D), 0.05),
        'w_out': nrm(19, (DEPTH, D_MIX, D_MODEL), BETA * D_MIX ** -0.5),
        'ln_mix_g': 1.0 + nrm(20, (DEPTH, D_MODEL), 0.05),
        'ln_mix_b': nrm(21, (DEPTH, D_MODEL), 0.02),
        'w_ff_gate': nrm(22, (N_DENSE, D_MODEL, D_FF), D_MODEL ** -0.5),
        'w_ff_up': nrm(23, (N_DENSE, D_MODEL, D_FF), D_MODEL ** -0.5),
        'w_ff_down': nrm(24, (N_DENSE, D_FF, D_MODEL), BETA * D_FF ** -0.5),
        'w_router': nrm(25, (N_MOE, D_MODEL, N_EXPERTS), D_MODEL ** -0.5),
        'w_e_gate': nrm(26, (N_MOE, N_EXPERTS, D_MODEL, D_FF), D_MODEL ** -0.5),
        'w_e_up': nrm(27, (N_MOE, N_EXPERTS, D_MODEL, D_FF), D_MODEL ** -0.5),
        'w_e_down': nrm(28, (N_MOE, N_EXPERTS, D_FF, D_MODEL), BETA * D_FF ** -0.5),
        'ln_ffn_g': 1.0 + nrm(29, (DEPTH, D_MODEL), 0.05),
        'ln_ffn_b': nrm(30, (DEPTH, D_MODEL), 0.02),
    }


def reference(x_prompt, x_sample, state_conv_a, state_conv_c, state_pool_d,
              ln_in_g, ln_in_b, w_in, conv_a_w, conv_a_b, ln_a_g, ln_a_b, ln_v_g, ln_v_b,
              w_s, b_s, conv_c_w, w_pool, pool_scale, w_out, ln_mix_g, ln_mix_b,
              w_ff_gate, w_ff_up, w_ff_down, w_router, w_e_gate, w_e_up, w_e_down,
              ln_ffn_g, ln_ffn_b):
    p = dict(ln_in_g=ln_in_g, ln_in_b=ln_in_b, w_in=w_in, conv_a_w=conv_a_w, conv_a_b=conv_a_b,
             ln_a_g=ln_a_g, ln_a_b=ln_a_b, ln_v_g=ln_v_g, ln_v_b=ln_v_b, w_s=w_s, b_s=b_s,
             conv_c_w=conv_c_w, w_pool=w_pool, pool_scale=pool_scale, w_out=w_out,
             ln_mix_g=ln_mix_g, ln_mix_b=ln_mix_b, w_ff_gate=w_ff_gate, w_ff_up=w_ff_up,
             w_ff_down=w_ff_down, w_router=w_router, w_e_gate=w_e_gate, w_e_up=w_e_up,
             w_e_down=w_e_down, ln_ffn_g=ln_ffn_g, ln_ffn_b=ln_ffn_b)
    bsz = x_prompt.shape[0]
    dt = x_prompt.dtype
    zero_a = jnp.zeros((DEPTH, bsz, CONV_A_W - 1, D_A), dt)
    zero_c = jnp.zeros((DEPTH, bsz, CONV_C_W - 1, D_C), dt)
    zero_d = jnp.zeros((DEPTH, bsz, POOL_PAST, D_D), dt)
    y_prompt, conv_a_prompt, conv_c_prompt, pool_d_prompt, chunk_v_prompt = trunk(
        x_prompt, zero_a, zero_c, zero_d, 0, p)
    y_sample, conv_a_sample, conv_c_sample, pool_d_sample, chunk_v_sample = trunk(
        x_sample, state_conv_a, state_conv_c, state_pool_d, PAST_LEN, p)
    return (y_prompt, y_sample, conv_a_prompt, conv_c_prompt, pool_d_prompt, chunk_v_prompt,
            conv_a_sample, conv_c_sample, pool_d_sample, chunk_v_sample)
```

```python
import functools

import numpy as np
import jax
import jax.numpy as jnp
from jax import lax
from jax.experimental import pallas as pl
from jax.experimental.pallas import tpu as pltpu

F32 = jnp.float32
BF16 = jnp.bfloat16
I32 = jnp.int32

PAST_LEN = 16384
CHUNK = 128
N_HEADS_B = 4
POOL_WINDOWS = (2, 4, 8, 16)
POOL_PAST = 15
CONV_A_W = 31
CONV_C_W = 3
N_EXPERTS = 8
TOP_K = 2
LN_EPS = 1e-5
SQRT_HALF = float(np.sqrt(0.5).astype(np.float32))

LANES = 128
SUBLANES = 8
VMEM_LIMIT_BYTES = 56 * 1024 * 1024

HIST_A = 32
HIST_C = 8
HIST_D = 24

TL_PROMPT = 512
TM_DENSE = 1024
TF = 512
TM_ROUTE = 512
TM_MOVE = 256


def _ln(x, g, b):
    mu = jnp.mean(x, axis=-1, keepdims=True)
    xc = x - mu
    var = jnp.mean(xc * xc, axis=-1, keepdims=True)
    return xc * lax.rsqrt(var + LN_EPS) * g + b


def _gelu(x):
    return 0.5 * x * (1.0 + lax.erf(x * SQRT_HALF))


def _silu(x):
    return x * jax.nn.sigmoid(x)


def _dot(a, b):
    return jnp.dot(a, b, preferred_element_type=F32)


def _cparams(sem):
    return pltpu.CompilerParams(dimension_semantics=sem, vmem_limit_bytes=VMEM_LIMIT_BYTES)


def _mixer_prompt_kernel(first_layer, alpha, tl, d_mix,
                         x_ref, lng_ref, lnb_ref, win_ref, caw_ref, cab_ref, lag_ref, lab_ref,
                         lvg_ref, lvb_ref, ws_ref, bsm_ref, ccw_ref, wpool_ref, pscale_ref,
                         wout_ref, lmg_ref, lmb_ref,
                         h_out, sa_out, sc_out, sd_out, v_out,
                         ext_a, ext_c, ext_d, cat_ref):
    dq = d_mix // 4
    j = pl.program_id(1)
    nj = pl.num_programs(1)

    @pl.when(j == 0)
    def _():
        ext_a[0:HIST_A, :] = jnp.zeros((HIST_A, dq), F32)
        ext_c[0:HIST_C, :] = jnp.zeros((HIST_C, dq), F32)
        ext_d[0:HIST_D, :] = jnp.zeros((HIST_D, dq), F32)

    x = x_ref[0]
    h = _ln(x, lng_ref[...], lnb_ref[...]) if first_layer else x
    hb = h.astype(BF16)

    def proj(i):
        return _dot(hb, win_ref[:, i * dq:(i + 1) * dq])

    a_glu = proj(0) * jax.nn.sigmoid(proj(1))
    ext_a[HIST_A:HIST_A + tl, :] = a_glu
    rb = 64
    off_a = HIST_A - (CONV_A_W - 1)
    for r0 in range(0, tl, rb):
        acc = jnp.zeros((rb, dq), F32)
        for k in range(CONV_A_W):
            acc = acc + ext_a[off_a + r0 + k:off_a + r0 + k + rb, :] * caw_ref[k:k + 1, :]
        y = _silu(_ln(acc + cab_ref[...], lag_ref[...], lab_ref[...]))
        cat_ref[r0:r0 + rb, 0:dq] = y.astype(BF16)

    u = _gelu(proj(2))
    v = _ln(_gelu(proj(3)), lvg_ref[...], lvb_ref[...])
    hd = dq // N_HEADS_B
    lane_head = lax.broadcasted_iota(I32, (CHUNK, dq), 1) // hd
    for c in range(tl // CHUNK):
        vc = v[c * CHUNK:(c + 1) * CHUNK, :]
        s = bsm_ref[...]
        for hh in range(N_HEADS_B):
            vm = jnp.where(lane_head == hh, vc, 0.0).astype(BF16)
            s = s + _dot(ws_ref[hh], vm)
        cat_ref[c * CHUNK:(c + 1) * CHUNK, dq:2 * dq] = (u[c * CHUNK:(c + 1) * CHUNK, :] * s).astype(BF16)

    c_x = proj(4)
    c_b = proj(5)
    c_c = proj(6)
    gx = c_c * c_x
    ext_c[HIST_C:HIST_C + tl, :] = gx
    off_c = HIST_C - (CONV_C_W - 1)
    conv_c = gx * ccw_ref[CONV_C_W - 1:CONV_C_W, :]
    for k in range(CONV_C_W - 1):
        conv_c = conv_c + ext_c[off_c + k:off_c + k + tl, :] * ccw_ref[k:k + 1, :]
    cat_ref[:, 2 * dq:3 * dq] = (c_b * conv_c).astype(BF16)

    d_in = proj(7)
    ext_d[HIST_D:HIST_D + tl, :] = d_in
    e = ext_d[...]
    s2 = e + pltpu.roll(e, 1, axis=0)
    s4 = s2 + pltpu.roll(s2, 2, axis=0)
    s8 = s4 + pltpu.roll(s4, 4, axis=0)
    s16 = s8 + pltpu.roll(s8, 8, axis=0)
    sums = (s2, s4, s8, s16)
    pos = (j * tl + lax.broadcasted_iota(I32, (tl, 1), 0)).astype(F32)
    gc = dq // len(POOL_WINDOWS)
    lane_grp = lax.broadcasted_iota(I32, (tl, dq), 1) // gc
    mean = jnp.zeros((tl, dq), F32)
    for g, w in enumerate(POOL_WINDOWS):
        inv = 1.0 / jnp.minimum(pos + 1.0, float(w))
        mean = jnp.where(lane_grp == g, sums[g][HIST_D:HIST_D + tl, :] * inv, mean)
    dd = (mean - d_in).astype(BF16)
    cat_ref[:, 3 * dq:4 * dq] = (_dot(dd, wpool_ref[...]) * pscale_ref[...]).astype(BF16)

    mix = _dot(cat_ref[...], wout_ref[...])
    h_out[0] = _ln(alpha * h + mix, lmg_ref[...], lmb_ref[...])

    @pl.when(j == nj - 1)
    def _():
        sa_out[0] = ext_a[HIST_A + tl - (CONV_A_W - 1):HIST_A + tl, :]
        sc_out[0] = ext_c[HIST_C + tl - (CONV_C_W - 1):HIST_C + tl, :]
        sd_out[0] = ext_d[HIST_D + tl - POOL_PAST:HIST_D + tl, :]
        v_out[0] = v[tl - CHUNK:tl, :]

    ext_a[0:HIST_A, :] = ext_a[tl:tl + HIST_A, :]
    ext_c[0:HIST_C, :] = ext_c[tl:tl + HIST_C, :]
    ext_d[0:HIST_D, :] = ext_d[tl:tl + HIST_D, :]


def _full_spec(arr):
    nd = arr.ndim
    return pl.BlockSpec(arr.shape, lambda *_: (0,) * nd)


def _mixer_prompt(x, lw, first_layer, alpha):
    bsz, seq, d = x.shape
    d_mix = lw['w_out'].shape[0]
    dq = d_mix // 4
    tl = TL_PROMPT
    assert seq % tl == 0 and tl % CHUNK == 0 and seq >= CHUNK
    params = [lw['ln_in_g'], lw['ln_in_b'], lw['w_in'], lw['conv_a_w'], lw['conv_a_b'], lw['ln_a_g'],
              lw['ln_a_b'], lw['ln_v_g'], lw['ln_v_b'], lw['ws_tril'], lw['bs_mat'], lw['conv_c_w'],
              lw['w_pool_bd'], lw['pool_scale'], lw['w_out'], lw['ln_mix_g'], lw['ln_mix_b']]
    out_shape = (
        jax.ShapeDtypeStruct((bsz, seq, d), F32),
        jax.ShapeDtypeStruct((bsz, CONV_A_W - 1, dq), F32),
        jax.ShapeDtypeStruct((bsz, CONV_C_W - 1, dq), F32),
        jax.ShapeDtypeStruct((bsz, POOL_PAST, dq), F32),
        jax.ShapeDtypeStruct((bsz, CHUNK, dq), F32),
    )
    state_spec = lambda rows: pl.BlockSpec((1, rows, dq), lambda b, j: (b, 0, 0))
    return pl.pallas_call(
        functools.partial(_mixer_prompt_kernel, first_layer, alpha, tl, d_mix),
        grid=(bsz, seq // tl),
        in_specs=[pl.BlockSpec((1, tl, d), lambda b, j: (b, j, 0))] + [_full_spec(p) for p in params],
        out_specs=(pl.BlockSpec((1, tl, d), lambda b, j: (b, j, 0)),
                   state_spec(CONV_A_W - 1), state_spec(CONV_C_W - 1), state_spec(POOL_PAST),
                   state_spec(CHUNK)),
        out_shape=out_shape,
        scratch_shapes=[pltpu.VMEM((HIST_A + tl, dq), F32), pltpu.VMEM((HIST_C + tl, dq), F32),
                        pltpu.VMEM((HIST_D + tl, dq), F32), pltpu.VMEM((tl, d_mix), BF16)],
        compiler_params=_cparams(("arbitrary", "arbitrary")),
        name="mixer_prompt",
    )(x, *params)


def _mixer_sample_kernel(first_layer, alpha, n_seq, n_step, d_mix,
                         x_ref, sa_ref, sc_ref, sd_ref, lng_ref, lnb_ref, win_ref, caw_ref, cab_ref,
                         lag_ref, lab_ref, lvg_ref, lvb_ref, wsv_ref, bsv_ref, ccw_ref, wpool_ref,
                         pscale_ref, wout_ref, lmg_ref, lmb_ref,
                         h_out, sa_out, sc_out, sd_out, v_out, cat_ref):
    dq = d_mix // 4
    x = x_ref[...]
    h = _ln(x, lng_ref[...], lnb_ref[...]) if first_layer else x
    hb = h.astype(BF16)

    def proj(i):
        return _dot(hb, win_ref[:, i * dq:(i + 1) * dq])

    def slab(val, i):
        return val[i * n_seq:(i + 1) * n_seq, :]

    def ext_slabs(state_ref, n_past, cur):
        return ([state_ref[i * n_seq:(i + 1) * n_seq, :] for i in range(n_past)]
                + [slab(cur, i) for i in range(n_step)])

    def store_state(out_ref, slabs, n_keep):
        for i, sl in enumerate(slabs[len(slabs) - n_keep:]):
            out_ref[i * n_seq:(i + 1) * n_seq, :] = sl

    a_glu = proj(0) * jax.nn.sigmoid(proj(1))
    ea = ext_slabs(sa_ref, CONV_A_W - 1, a_glu)
    for l in range(n_step):
        acc = jnp.zeros((n_seq, dq), F32)
        for k in range(CONV_A_W):
            acc = acc + ea[l + k] * caw_ref[k:k + 1, :]
        y = _silu(_ln(acc + cab_ref[...], lag_ref[...], lab_ref[...]))
        cat_ref[l * n_seq:(l + 1) * n_seq, 0:dq] = y.astype(BF16)
    store_state(sa_out, ea, CONV_A_W - 1)

    u = _gelu(proj(2))
    v = _ln(_gelu(proj(3)), lvg_ref[...], lvb_ref[...])
    v_out[...] = v
    for l in range(n_step):
        s = jnp.zeros((n_seq, dq), F32) + bsv_ref[l:l + 1, :]
        for m in range(l + 1):
            s = s + slab(v, m) * wsv_ref[l * n_step + m:l * n_step + m + 1, :]
        cat_ref[l * n_seq:(l + 1) * n_seq, dq:2 * dq] = (slab(u, l) * s).astype(BF16)

    c_x = proj(4)
    c_b = proj(5)
    c_c = proj(6)
    gx = c_c * c_x
    ec = ext_slabs(sc_ref, CONV_C_W - 1, gx)
    for l in range(n_step):
        acc = jnp.zeros((n_seq, dq), F32)
        for k in range(CONV_C_W):
            acc = acc + ec[l + k] * ccw_ref[k:k + 1, :]
        cat_ref[l * n_seq:(l + 1) * n_seq, 2 * dq:3 * dq] = (slab(c_b, l) * acc).astype(BF16)
    store_state(sc_out, ec, CONV_C_W - 1)

    d_in = proj(7)
    ed = ext_slabs(sd_ref, POOL_PAST, d_in)
    memo = {}

    def wsum(i, w):
        if i < 0:
            return None
        if w == 1:
            return ed[i]
        if (i, w) not in memo:
            a, b = wsum(i, w // 2), wsum(i - w // 2, w // 2)
            memo[(i, w)] = a if b is None else a + b
        return memo[(i, w)]

    gc = dq // len(POOL_WINDOWS)
    lane_grp = lax.broadcasted_iota(I32, (n_seq, dq), 1) // gc
    for l in range(n_step):
        mean = jnp.zeros((n_seq, dq), F32)
        for g, w in enumerate(POOL_WINDOWS):
            count = min(PAST_LEN + l + 1, w)
            mean = jnp.where(lane_grp == g, wsum(POOL_PAST + l, w) * (1.0 / count), mean)
        dd = (mean - slab(d_in, l)).astype(BF16)
        cat_ref[l * n_seq:(l + 1) * n_seq, 3 * dq:4 * dq] = (
            _dot(dd, wpool_ref[...]) * pscale_ref[...]).astype(BF16)
    store_state(sd_out, ed, POOL_PAST)

    mix = _dot(cat_ref[...], wout_ref[...])
    h_out[...] = _ln(alpha * h + mix, lmg_ref[...], lmb_ref[...])


def _mixer_sample(x_tm, sa, sc, sd, lw, first_layer, alpha, n_seq, n_step):
    rows, d = x_tm.shape
    d_mix = lw['w_out'].shape[0]
    dq = d_mix // 4
    assert n_seq % SUBLANES == 0 and n_step <= CHUNK and PAST_LEN % CHUNK == 0
    ins = [x_tm, sa, sc, sd, lw['ln_in_g'], lw['ln_in_b'], lw['w_in'], lw['conv_a_w'], lw['conv_a_b'],
           lw['ln_a_g'], lw['ln_a_b'], lw['ln_v_g'], lw['ln_v_b'], lw['ws_vec'], lw['bs_vec'],
           lw['conv_c_w'], lw['w_pool_bd'], lw['pool_scale'], lw['w_out'], lw['ln_mix_g'], lw['ln_mix_b']]
    out_shape = (
        jax.ShapeDtypeStruct((rows, d), F32),
        jax.ShapeDtypeStruct(((CONV_A_W - 1) * n_seq, dq), F32),
        jax.ShapeDtypeStruct(((CONV_C_W - 1) * n_seq, dq), F32),
        jax.ShapeDtypeStruct((POOL_PAST * n_seq, dq), F32),
        jax.ShapeDtypeStruct((rows, dq), F32),
    )
    return pl.pallas_call(
        functools.partial(_mixer_sample_kernel, first_layer, alpha, n_seq, n_step, d_mix),
        grid=(1,),
        in_specs=[_full_spec(a) for a in ins],
        out_specs=tuple(pl.BlockSpec(s.shape, lambda i: (0, 0)) for s in out_shape),
        out_shape=out_shape,
        scratch_shapes=[pltpu.VMEM((rows, d_mix), BF16)],
        compiler_params=_cparams(("arbitrary",)),
        name="mixer_sample",
    )(*ins)


def _ffn_dense_kernel(alpha, x_ref, wg_ref, wu_ref, wd_ref, g_ref, b_ref, o_ref, acc_ref):
    f = pl.program_id(1)

    @pl.when(f == 0)
    def _():
        acc_ref[...] = jnp.zeros_like(acc_ref)

    xb = x_ref[...].astype(BF16)
    mid = _silu(_dot(xb, wg_ref[...])) * _dot(xb, wu_ref[...])
    acc_ref[...] += _dot(mid.astype(BF16), wd_ref[...])

    @pl.when(f == pl.num_programs(1) - 1)
    def _():
        o_ref[...] = _ln(alpha * x_ref[...] + acc_ref[...], g_ref[...], b_ref[...])


def _ffn_dense(x, wg, wu, wd, g, b, alpha, tm):
    t, d = x.shape
    dff = wg.shape[1]
    assert t % tm == 0 and dff % TF == 0
    return pl.pallas_call(
        functools.partial(_ffn_dense_kernel, alpha),
        grid=(t // tm, dff // TF),
        in_specs=[pl.BlockSpec((tm, d), lambda i, f: (i, 0)),
                  pl.BlockSpec((d, TF), lambda i, f: (0, f)),
                  pl.BlockSpec((d, TF), lambda i, f: (0, f)),
                  pl.BlockSpec((TF, d), lambda i, f: (f, 0)),
                  pl.BlockSpec((1, d), lambda i, f: (0, 0)),
                  pl.BlockSpec((1, d), lambda i, f: (0, 0))],
        out_specs=pl.BlockSpec((tm, d), lambda i, f: (i, 0)),
        out_shape=jax.ShapeDtypeStruct((t, d), F32),
        scratch_shapes=[pltpu.VMEM((tm, d), F32)],
        compiler_params=_cparams(("arbitrary", "arbitrary")),
        name="ffn_dense",
    )(x, wg, wu, wd, g, b)


def _router_kernel(tm, h_ref, wr_ref, mi_ref, mf_ref, cnt_ref, carry_ref):
    i = pl.program_id(0)

    @pl.when(i == 0)
    def _():
        carry_ref[...] = jnp.zeros_like(carry_ref)

    lane = lax.broadcasted_iota(I32, (tm, LANES), 1).astype(F32)
    logits = jnp.dot(h_ref[...], wr_ref[...], preferred_element_type=F32,
                     precision=lax.Precision.HIGHEST)
    logits = jnp.where(lane < N_EXPERTS, logits, -jnp.inf)
    m1 = jnp.max(logits, axis=-1, keepdims=True)
    i1 = jnp.min(jnp.where(logits == m1, lane, float(LANES)), axis=-1, keepdims=True)
    sel1 = lane == i1
    rest = jnp.where(sel1, -jnp.inf, logits)
    m2 = jnp.max(rest, axis=-1, keepdims=True)
    i2 = jnp.min(jnp.where(rest == m2, lane, float(LANES)), axis=-1, keepdims=True)
    sel2 = lane == i2
    e2 = jnp.exp(m2 - m1)
    den = 1.0 + e2
    g1 = 1.0 / den
    g2 = e2 / den

    sel = jnp.where(sel1 | sel2, 1.0, 0.0)
    row = lax.broadcasted_iota(I32, (tm, tm), 0)
    col = lax.broadcasted_iota(I32, (tm, tm), 1)
    below = jnp.where(col < row, 1.0, 0.0).astype(BF16)
    base = _dot(below, sel.astype(BF16)) + carry_ref[0:1, :]
    r1 = jnp.sum(jnp.where(sel1, base, 0.0), axis=-1, keepdims=True)
    r2 = jnp.sum(jnp.where(sel2, base, 0.0), axis=-1, keepdims=True)
    carry_ref[...] = carry_ref[...] + jnp.sum(sel, axis=0, keepdims=True)

    mi = jnp.where(lane == 0, i1, jnp.where(lane == 1, i2, jnp.where(lane == 2, r1, r2)))
    mi_ref[...] = mi[:, 0:SUBLANES].astype(I32)
    mf = jnp.where(lane == 0, g1, g2)
    mf_ref[...] = mf[:, 0:SUBLANES]
    cnt_ref[...] = carry_ref[...].astype(I32)


def _router(h, wr_pad):
    t, d = h.shape
    tm = min(TM_ROUTE, t)
    assert t % tm == 0
    return pl.pallas_call(
        functools.partial(_router_kernel, tm),
        grid=(t // tm,),
        in_specs=[pl.BlockSpec((tm, d), lambda i: (i, 0)), _full_spec(wr_pad)],
        out_specs=(pl.BlockSpec((tm, SUBLANES), lambda i: (i, 0)),
                   pl.BlockSpec((tm, SUBLANES), lambda i: (i, 0)),
                   pl.BlockSpec((SUBLANES, LANES), lambda i: (0, 0))),
        out_shape=(jax.ShapeDtypeStruct((t, SUBLANES), I32),
                   jax.ShapeDtypeStruct((t, SUBLANES), F32),
                   jax.ShapeDtypeStruct((SUBLANES, LANES), I32)),
        scratch_shapes=[pltpu.VMEM((SUBLANES, LANES), F32)],
        compiler_params=_cparams(("arbitrary",)),
        name="router",
    )(h, wr_pad)


def _row_copy(src, src_row, dst, dst_row, sem):
    return pltpu.make_async_copy(src.at[pl.ds(src_row, 1), :], dst.at[pl.ds(dst_row, 1), :], sem)


def _dispatch_kernel(tm, ps_ref, meta_ref, x_ref, init_ref, xs_ref, sem):
    del init_ref

    def issue(t, carry):
        for k in range(TOP_K):
            dst = ps_ref[meta_ref[0, k, t]] + meta_ref[0, TOP_K + k, t]
            _row_copy(x_ref, t, xs_ref, dst, sem).start()
        return carry

    lax.fori_loop(0, tm, issue, 0)

    def drain(t, carry):
        for k in range(TOP_K):
            _row_copy(x_ref, t, xs_ref, 0, sem).wait()
        return carry

    lax.fori_loop(0, tm, drain, 0)


def _dispatch(h, meta_blk, pad_start, cap):
    t, d = h.shape
    tm = min(TM_MOVE, t)
    init = jnp.zeros((cap, d), F32)
    return pl.pallas_call(
        functools.partial(_dispatch_kernel, tm),
        grid_spec=pltpu.PrefetchScalarGridSpec(
            num_scalar_prefetch=1,
            grid=(t // tm,),
            in_specs=[pl.BlockSpec((1, 2 * TOP_K, tm), lambda i, ps: (i, 0, 0), memory_space=pltpu.SMEM),
                      pl.BlockSpec((tm, d), lambda i, ps: (i, 0)),
                      pl.BlockSpec(memory_space=pl.ANY)],
            out_specs=pl.BlockSpec(memory_space=pl.ANY),
            scratch_shapes=[pltpu.SemaphoreType.DMA(())],
        ),
        out_shape=jax.ShapeDtypeStruct((cap, d), F32),
        input_output_aliases={3: 0},
        compiler_params=_cparams(("arbitrary",)),
        name="moe_dispatch",
    )(pad_start, meta_blk, h, init)


def _ffn_moe_kernel(be_ref, nu_ref, x_ref, wg_ref, wu_ref, wd_ref, y_ref, acc_ref):
    b = pl.program_id(0)
    f = pl.program_id(1)

    @pl.when(b < nu_ref[0])
    def _():
        @pl.when(f == 0)
        def _():
            acc_ref[...] = jnp.zeros_like(acc_ref)

        xb = x_ref[...].astype(BF16)
        mid = _silu(_dot(xb, wg_ref[0])) * _dot(xb, wu_ref[0])
        acc_ref[...] += _dot(mid.astype(BF16), wd_ref[0])

        @pl.when(f == pl.num_programs(1) - 1)
        def _():
            y_ref[...] = acc_ref[...]

    @pl.when((b >= nu_ref[0]) & (f == pl.num_programs(1) - 1))
    def _():
        y_ref[...] = jnp.zeros_like(y_ref)


def _ffn_moe(xs, wg, wu, wd, block_e, n_used, tm):
    cap, d = xs.shape
    dff = wg.shape[2]
    nf = dff // TF
    assert cap % tm == 0 and dff % TF == 0

    def blk(b, nu):
        return jnp.minimum(b, nu[0] - 1)

    def fcol(b, f, nu):
        return jnp.where(b < nu[0], f, nf - 1)

    return pl.pallas_call(
        _ffn_moe_kernel,
        grid_spec=pltpu.PrefetchScalarGridSpec(
            num_scalar_prefetch=2,
            grid=(cap // tm, nf),
            in_specs=[pl.BlockSpec((tm, d), lambda b, f, be, nu: (blk(b, nu), 0)),
                      pl.BlockSpec((1, d, TF), lambda b, f, be, nu: (be[b], 0, fcol(b, f, nu))),
                      pl.BlockSpec((1, d, TF), lambda b, f, be, nu: (be[b], 0, fcol(b, f, nu))),
                      pl.BlockSpec((1, TF, d), lambda b, f, be, nu: (be[b], fcol(b, f, nu), 0))],
            out_specs=pl.BlockSpec((tm, d), lambda b, f, be, nu: (b, 0)),
            scratch_shapes=[pltpu.VMEM((tm, d), F32)],
        ),
        out_shape=jax.ShapeDtypeStruct((cap, d), F32),
        compiler_params=_cparams(("arbitrary", "arbitrary")),
        name="ffn_moe",
    )(block_e, n_used, xs, wg, wu, wd)


def _combine_kernel(alpha, tm, ps_ref, meta_ref, h_ref, gate_ref, g_ref, b_ref, ys_ref, o_ref, ybuf, sem):
    def issue(t, carry):
        for k in range(TOP_K):
            src = ps_ref[meta_ref[0, k, t]] + meta_ref[0, TOP_K + k, t]
            _row_copy(ys_ref, src, ybuf.at[k], t, sem).start()
        return carry

    lax.fori_loop(0, tm, issue, 0)

    def drain(t, carry):
        for k in range(TOP_K):
            _row_copy(ys_ref, 0, ybuf.at[k], t, sem).wait()
        return carry

    lax.fori_loop(0, tm, drain, 0)

    gates = gate_ref[...]
    ffn = gates[:, 0:1] * ybuf[0] + gates[:, 1:2] * ybuf[1]
    o_ref[...] = _ln(alpha * h_ref[...] + ffn, g_ref[...], b_ref[...])


def _combine(h, meta_blk, gates, pad_start, ys, g, b, alpha):
    t, d = h.shape
    tm = min(TM_MOVE, t)
    return pl.pallas_call(
        functools.partial(_combine_kernel, alpha, tm),
        grid_spec=pltpu.PrefetchScalarGridSpec(
            num_scalar_prefetch=1,
            grid=(t // tm,),
            in_specs=[pl.BlockSpec((1, 2 * TOP_K, tm), lambda i, ps: (i, 0, 0), memory_space=pltpu.SMEM),
                      pl.BlockSpec((tm, d), lambda i, ps: (i, 0)),
                      pl.BlockSpec((tm, SUBLANES), lambda i, ps: (i, 0)),
                      pl.BlockSpec((1, d), lambda i, ps: (0, 0)),
                      pl.BlockSpec((1, d), lambda i, ps: (0, 0)),
                      pl.BlockSpec(memory_space=pl.ANY)],
            out_specs=pl.BlockSpec((tm, d), lambda i, ps: (i, 0)),
            scratch_shapes=[pltpu.VMEM((TOP_K, tm, d), F32), pltpu.SemaphoreType.DMA(())],
        ),
        out_shape=jax.ShapeDtypeStruct((t, d), F32),
        compiler_params=_cparams(("arbitrary",)),
        name="moe_combine",
    )(pad_start, meta_blk, h, gates, g, b, ys)


def _ffn_routed(h, mw, g, b, alpha, tm_expert):
    t, d = h.shape
    meta_i, gates, cnt = _router(h, mw['w_router_pad'])
    counts = cnt[0, :N_EXPERTS]
    padded = (counts + tm_expert - 1) // tm_expert * tm_expert
    pad_end = jnp.cumsum(padded)
    pad_start = (pad_end - padded).astype(I32)
    n_blocks = -(-(t * TOP_K) // tm_expert) + N_EXPERTS
    cap = n_blocks * tm_expert
    n_used = (pad_end[-1] // tm_expert).astype(I32)
    blk_start = jnp.minimum(jnp.arange(n_blocks, dtype=I32), n_used - 1) * tm_expert
    block_e = jnp.minimum(jnp.searchsorted(pad_end, blk_start, side='right'), N_EXPERTS - 1).astype(I32)
    tm_mv = min(TM_MOVE, t)
    meta_blk = meta_i[:, :2 * TOP_K].reshape(t // tm_mv, tm_mv, 2 * TOP_K).transpose(0, 2, 1)
    xs = _dispatch(h, meta_blk, pad_start, cap)
    ys = _ffn_moe(xs, mw['w_e_gate'], mw['w_e_up'], mw['w_e_down'], block_e, n_used.reshape(1), tm_expert)
    return _combine(h, meta_blk, gates, pad_start, ys, g, b, alpha)


def _prep_layer(i, p, n_step):
    dq = p['conv_a_w'].shape[2]
    hd = dq // N_HEADS_B
    tril = jnp.tril(jnp.ones((CHUNK, CHUNK), dtype=bool))
    w_s = p['w_s'][i]
    b_s = p['b_s'][i]
    wp = p['w_pool'][i]
    ng, gc = wp.shape[0], wp.shape[1]
    w_pool_bd = jnp.zeros((dq, dq), F32)
    for g in range(ng):
        w_pool_bd = w_pool_bd.at[g * gc:(g + 1) * gc, g * gc:(g + 1) * gc].set(wp[g])
    row = lambda a: a.reshape(1, -1)
    return dict(
        ln_in_g=row(p['ln_in_g']), ln_in_b=row(p['ln_in_b']),
        w_in=p['w_in'][i].astype(BF16),
        conv_a_w=p['conv_a_w'][i], conv_a_b=row(p['conv_a_b'][i]),
        ln_a_g=row(p['ln_a_g'][i]), ln_a_b=row(p['ln_a_b'][i]),
        ln_v_g=row(p['ln_v_g'][i]), ln_v_b=row(p['ln_v_b'][i]),
        ws_tril=jnp.where(tril[None], w_s, 0).astype(BF16),
        bs_mat=jnp.repeat(b_s.T, hd, axis=1),
        ws_vec=jnp.repeat(jnp.where(tril[None], w_s, 0)[:, :n_step, :n_step].transpose(1, 2, 0)
                          .reshape(n_step * n_step, N_HEADS_B), hd, axis=1),
        bs_vec=jnp.repeat(b_s[:, :n_step].T, hd, axis=1),
        conv_c_w=p['conv_c_w'][i],
        w_pool_bd=w_pool_bd.astype(BF16), pool_scale=row(p['pool_scale'][i]),
        w_out=p['w_out'][i].astype(BF16),
        ln_mix_g=row(p['ln_mix_g'][i]), ln_mix_b=row(p['ln_mix_b'][i]),
    )


def kernel(x_prompt, x_sample, state_conv_a, state_conv_c, state_pool_d, ln_in_g, ln_in_b, w_in, conv_a_w,
           conv_a_b, ln_a_g, ln_a_b, ln_v_g, ln_v_b, w_s, b_s, conv_c_w, w_pool, pool_scale, w_out,
           ln_mix_g, ln_mix_b, w_ff_gate, w_ff_up, w_ff_down, w_router, w_e_gate, w_e_up, w_e_down,
           ln_ffn_g, ln_ffn_b):
    p = dict(ln_in_g=ln_in_g, ln_in_b=ln_in_b, w_in=w_in, conv_a_w=conv_a_w, conv_a_b=conv_a_b,
             ln_a_g=ln_a_g, ln_a_b=ln_a_b, ln_v_g=ln_v_g, ln_v_b=ln_v_b, w_s=w_s, b_s=b_s,
             conv_c_w=conv_c_w, w_pool=w_pool, pool_scale=pool_scale, w_out=w_out,
             ln_mix_g=ln_mix_g, ln_mix_b=ln_mix_b)
    depth = w_in.shape[0]
    bsz, seq, d = x_prompt.shape
    n_seq, n_step, _ = x_sample.shape
    dq = conv_a_w.shape[2]
    alpha = float((2.0 * depth) ** 0.25)

    hp = x_prompt
    hs = x_sample.transpose(1, 0, 2).reshape(n_step * n_seq, d)
    outs = {k: [] for k in ('pa', 'pc', 'pd', 'pv', 'sa', 'sc', 'sd', 'sv')}
    for i in range(depth):
        lw = _prep_layer(i, p, n_step)
        hp, pa, pc, pd, pv = _mixer_prompt(hp, lw, i == 0, alpha)
        tm_state = lambda s: s.transpose(1, 0, 2).reshape(-1, dq)
        hs, sa, sc, sd, sv = _mixer_sample(hs, tm_state(state_conv_a[i]), tm_state(state_conv_c[i]),
                                           tm_state(state_pool_d[i]), lw, i == 0, alpha, n_seq, n_step)
        lg, lb = ln_ffn_g[i].reshape(1, -1), ln_ffn_b[i].reshape(1, -1)
        j = i // 2
        if i % 2 == 0:
            wg, wu, wd = w_ff_gate[j].astype(BF16), w_ff_up[j].astype(BF16), w_ff_down[j].astype(BF16)
            hp = _ffn_dense(hp.reshape(bsz * seq, d), wg, wu, wd, lg, lb, alpha, TM_DENSE).reshape(bsz, seq, d)
            hs = _ffn_dense(hs, wg, wu, wd, lg, lb, alpha, hs.shape[0])
        else:
            mw = dict(w_router_pad=jnp.pad(w_router[j], ((0, 0), (0, LANES - N_EXPERTS))),
                      w_e_gate=w_e_gate[j].astype(BF16), w_e_up=w_e_up[j].astype(BF16),
                      w_e_down=w_e_down[j].astype(BF16))
            hp = _ffn_routed(hp.reshape(bsz * seq, d), mw, lg, lb, alpha, 512).reshape(bsz, seq, d)
            hs = _ffn_routed(hs, mw, lg, lb, alpha, 256)
        back = lambda a: a.reshape(-1, n_seq, dq).transpose(1, 0, 2)
        for k, val in zip(('pa', 'pc', 'pd', 'pv', 'sa', 'sc', 'sd', 'sv'),
                          (pa, pc, pd, pv, back(sa), back(sc), back(sd), back(sv))):
            outs[k].append(val)
    y_sample = hs.reshape(n_step, n_seq, d).transpose(1, 0, 2)
    st = lambda k: jnp.stack(outs[k])
    return (hp, y_sample, st('pa'), st('pc'), st('pd'), st('pv'), st('sa'), st('sc'), st('sd'), st('sv'))
```

```python
import functools

import numpy as np
import jax
import jax.numpy as jnp
from jax import lax
from jax.experimental import pallas as pl
from jax.experimental.pallas import tpu as pltpu

F32 = jnp.float32
BF16 = jnp.bfloat16
I32 = jnp.int32

PAST_LEN = 16384
CHUNK = 128
N_HEADS_B = 4
POOL_WINDOWS = (2, 4, 8, 16)
POOL_PAST = 15
CONV_A_W = 31
CONV_C_W = 3
N_EXPERTS = 8
TOP_K = 2
LN_EPS = 1e-5
SQRT_HALF = float(np.sqrt(0.5).astype(np.float32))

LANES = 128
SUBLANES = 8
VMEM_LIMIT_BYTES = 56 * 1024 * 1024

HIST_A = 32
HIST_C = 8
HIST_D = 24

TL_PROMPT = 512
TM_DENSE = 1024
TF = 512
TM_ROUTE = 512
TM_MOVE = 256
TM_EXPERT = 512


def _ln(x, g, b):
    mu = jnp.mean(x, axis=-1, keepdims=True)
    xc = x - mu
    var = jnp.mean(xc * xc, axis=-1, keepdims=True)
    return xc * lax.rsqrt(var + LN_EPS) * g + b


def _gelu(x):
    return 0.5 * x * (1.0 + lax.erf(x * SQRT_HALF))


def _silu(x):
    return x * jax.nn.sigmoid(x)


def _dot(a, b):
    return jnp.dot(a, b, preferred_element_type=F32)


def _cparams(sem):
    return pltpu.CompilerParams(dimension_semantics=sem, vmem_limit_bytes=VMEM_LIMIT_BYTES)


def _mixer_prompt_kernel(first_layer, alpha, tl, d_mix,
                         x_ref, lng_ref, lnb_ref, win_ref, caw_ref, cab_ref, lag_ref, lab_ref,
                         lvg_ref, lvb_ref, ws_ref, bsm_ref, ccw_ref, wpool_ref, pscale_ref,
                         wout_ref, lmg_ref, lmb_ref,
                         h_out, sa_out, sc_out, sd_out, v_out,
                         ext_a, ext_c, ext_d, cat_ref):
    dq = d_mix // 4
    j = pl.program_id(1)
    nj = pl.num_programs(1)

    @pl.when(j == 0)
    def _():
        ext_a[0, 0:HIST_A, :] = jnp.zeros((HIST_A, dq), F32)
        ext_c[0:HIST_C, :] = jnp.zeros((HIST_C, dq), F32)
        ext_d[0:HIST_D, :] = jnp.zeros((HIST_D, dq), F32)

    x = x_ref[0]
    h = _ln(x, lng_ref[...], lnb_ref[...]) if first_layer else x
    hb = h.astype(BF16)

    def proj(i):
        return _dot(hb, win_ref[:, i * dq:(i + 1) * dq])

    a_glu = proj(0) * jax.nn.sigmoid(proj(1))
    ext_a[0, HIST_A:HIST_A + tl, :] = a_glu
    n_ext = HIST_A + tl
    full = ext_a[0]
    for s in range(1, SUBLANES):
        ext_a[s, 0:n_ext - SUBLANES, :] = pltpu.roll(full, n_ext - s, axis=0)[0:n_ext - SUBLANES, :]
    rb = 64
    off_a = HIST_A - (CONV_A_W - 1)
    for r0 in range(0, tl, rb):
        acc = jnp.zeros((rb, dq), F32)
        for k in range(CONV_A_W):
            s = (off_a + k) % SUBLANES
            row = off_a + k - s + r0
            acc = acc + ext_a[s, row:row + rb, :] * caw_ref[k:k + 1, :]
        y = _silu(_ln(acc + cab_ref[...], lag_ref[...], lab_ref[...]))
        cat_ref[r0:r0 + rb, 0:dq] = y.astype(BF16)

    u = _gelu(proj(2))
    v = _ln(_gelu(proj(3)), lvg_ref[...], lvb_ref[...])
    hd = dq // N_HEADS_B
    lane_head = lax.broadcasted_iota(I32, (CHUNK, dq), 1) // hd
    for c in range(tl // CHUNK):
        vc = v[c * CHUNK:(c + 1) * CHUNK, :]
        s = bsm_ref[...]
        for hh in range(N_HEADS_B):
            vm = jnp.where(lane_head == hh, vc, 0.0).astype(BF16)
            s = s + _dot(ws_ref[hh], vm)
        cat_ref[c * CHUNK:(c + 1) * CHUNK, dq:2 * dq] = (u[c * CHUNK:(c + 1) * CHUNK, :] * s).astype(BF16)

    c_x = proj(4)
    c_b = proj(5)
    c_c = proj(6)
    gx = c_c * c_x
    ext_c[HIST_C:HIST_C + tl, :] = gx
    off_c = HIST_C - (CONV_C_W - 1)
    conv_c = gx * ccw_ref[CONV_C_W - 1:CONV_C_W, :]
    for k in range(CONV_C_W - 1):
        conv_c = conv_c + ext_c[off_c + k:off_c + k + tl, :] * ccw_ref[k:k + 1, :]
    cat_ref[:, 2 * dq:3 * dq] = (c_b * conv_c).astype(BF16)

    d_in = proj(7)
    ext_d[HIST_D:HIST_D + tl, :] = d_in
    e = ext_d[...]
    s2 = e + pltpu.roll(e, 1, axis=0)
    s4 = s2 + pltpu.roll(s2, 2, axis=0)
    s8 = s4 + pltpu.roll(s4, 4, axis=0)
    s16 = s8 + pltpu.roll(s8, 8, axis=0)
    sums = (s2, s4, s8, s16)
    pos = (j * tl + lax.broadcasted_iota(I32, (tl, 1), 0)).astype(F32)
    gc = dq // len(POOL_WINDOWS)
    lane_grp = lax.broadcasted_iota(I32, (tl, dq), 1) // gc
    mean = jnp.zeros((tl, dq), F32)
    for g, w in enumerate(POOL_WINDOWS):
        inv = 1.0 / jnp.minimum(pos + 1.0, float(w))
        mean = jnp.where(lane_grp == g, sums[g][HIST_D:HIST_D + tl, :] * inv, mean)
    dd = (mean - d_in).astype(BF16)
    cat_ref[:, 3 * dq:4 * dq] = (_dot(dd, wpool_ref[...]) * pscale_ref[...]).astype(BF16)

    mix = _dot(cat_ref[...], wout_ref[...])
    h_out[0] = _ln(alpha * h + mix, lmg_ref[...], lmb_ref[...])

    @pl.when(j == nj - 1)
    def _():
        sa_out[0] = ext_a[0, HIST_A + tl - (CONV_A_W - 1):HIST_A + tl, :]
        sc_out[0] = ext_c[HIST_C + tl - (CONV_C_W - 1):HIST_C + tl, :]
        sd_out[0] = ext_d[HIST_D + tl - POOL_PAST:HIST_D + tl, :]
        v_out[0] = v[tl - CHUNK:tl, :]

    ext_a[0, 0:HIST_A, :] = ext_a[0, tl:tl + HIST_A, :]
    ext_c[0:HIST_C, :] = ext_c[tl:tl + HIST_C, :]
    ext_d[0:HIST_D, :] = ext_d[tl:tl + HIST_D, :]


def _full_spec(arr):
    nd = arr.ndim
    return pl.BlockSpec(arr.shape, lambda *_: (0,) * nd)


def _mixer_prompt(x, lw, first_layer, alpha):
    bsz, seq, d = x.shape
    d_mix = lw['w_out'].shape[0]
    dq = d_mix // 4
    tl = TL_PROMPT
    assert seq % tl == 0 and tl % CHUNK == 0 and seq >= CHUNK
    params = [lw['ln_in_g'], lw['ln_in_b'], lw['w_in'], lw['conv_a_w'], lw['conv_a_b'], lw['ln_a_g'],
              lw['ln_a_b'], lw['ln_v_g'], lw['ln_v_b'], lw['ws_tril'], lw['bs_mat'], lw['conv_c_w'],
              lw['w_pool_bd'], lw['pool_scale'], lw['w_out'], lw['ln_mix_g'], lw['ln_mix_b']]
    out_shape = (
        jax.ShapeDtypeStruct((bsz, seq, d), F32),
        jax.ShapeDtypeStruct((bsz, CONV_A_W - 1, dq), F32),
        jax.ShapeDtypeStruct((bsz, CONV_C_W - 1, dq), F32),
        jax.ShapeDtypeStruct((bsz, POOL_PAST, dq), F32),
        jax.ShapeDtypeStruct((bsz, CHUNK, dq), F32),
    )
    state_spec = lambda rows: pl.BlockSpec((1, rows, dq), lambda b, j: (b, 0, 0))
    return pl.pallas_call(
        functools.partial(_mixer_prompt_kernel, first_layer, alpha, tl, d_mix),
        grid=(bsz, seq // tl),
        in_specs=[pl.BlockSpec((1, tl, d), lambda b, j: (b, j, 0))] + [_full_spec(p) for p in params],
        out_specs=(pl.BlockSpec((1, tl, d), lambda b, j: (b, j, 0)),
                   state_spec(CONV_A_W - 1), state_spec(CONV_C_W - 1), state_spec(POOL_PAST),
                   state_spec(CHUNK)),
        out_shape=out_shape,
        scratch_shapes=[pltpu.VMEM((SUBLANES, HIST_A + tl, dq), F32), pltpu.VMEM((HIST_C + tl, dq), F32),
                        pltpu.VMEM((HIST_D + tl, dq), F32), pltpu.VMEM((tl, d_mix), BF16)],
        compiler_params=_cparams(("arbitrary", "arbitrary")),
        name="mixer_prompt",
    )(x, *params)


def _mixer_sample_kernel(first_layer, alpha, n_seq, n_step, d_mix,
                         x_ref, sa_ref, sc_ref, sd_ref, lng_ref, lnb_ref, win_ref, caw_ref, cab_ref,
                         lag_ref, lab_ref, lvg_ref, lvb_ref, wsv_ref, bsv_ref, ccw_ref, wpool_ref,
                         pscale_ref, wout_ref, lmg_ref, lmb_ref,
                         h_out, sa_out, sc_out, sd_out, v_out, cat_ref):
    dq = d_mix // 4
    x = x_ref[...]
    h = _ln(x, lng_ref[...], lnb_ref[...]) if first_layer else x
    hb = h.astype(BF16)

    def proj(i):
        return _dot(hb, win_ref[:, i * dq:(i + 1) * dq])

    def slab(val, i):
        return val[i * n_seq:(i + 1) * n_seq, :]

    def ext_slabs(state_ref, n_past, cur):
        return ([state_ref[i * n_seq:(i + 1) * n_seq, :] for i in range(n_past)]
                + [slab(cur, i) for i in range(n_step)])

    def store_state(out_ref, slabs, n_keep):
        for i, sl in enumerate(slabs[len(slabs) - n_keep:]):
            out_ref[i * n_seq:(i + 1) * n_seq, :] = sl

    a_glu = proj(0) * jax.nn.sigmoid(proj(1))
    ea = ext_slabs(sa_ref, CONV_A_W - 1, a_glu)
    for l in range(n_step):
        acc = jnp.zeros((n_seq, dq), F32)
        for k in range(CONV_A_W):
            acc = acc + ea[l + k] * caw_ref[k:k + 1, :]
        y = _silu(_ln(acc + cab_ref[...], lag_ref[...], lab_ref[...]))
        cat_ref[l * n_seq:(l + 1) * n_seq, 0:dq] = y.astype(BF16)
    store_state(sa_out, ea, CONV_A_W - 1)

    u = _gelu(proj(2))
    v = _ln(_gelu(proj(3)), lvg_ref[...], lvb_ref[...])
    v_out[...] = v
    for l in range(n_step):
        s = jnp.zeros((n_seq, dq), F32) + bsv_ref[l:l + 1, :]
        for m in range(l + 1):
            s = s + slab(v, m) * wsv_ref[l * n_step + m:l * n_step + m + 1, :]
        cat_ref[l * n_seq:(l + 1) * n_seq, dq:2 * dq] = (slab(u, l) * s).astype(BF16)

    c_x = proj(4)
    c_b = proj(5)
    c_c = proj(6)
    gx = c_c * c_x
    ec = ext_slabs(sc_ref, CONV_C_W - 1, gx)
    for l in range(n_step):
        acc = jnp.zeros((n_seq, dq), F32)
        for k in range(CONV_C_W):
            acc = acc + ec[l + k] * ccw_ref[k:k + 1, :]
        cat_ref[l * n_seq:(l + 1) * n_seq, 2 * dq:3 * dq] = (slab(c_b, l) * acc).astype(BF16)
    store_state(sc_out, ec, CONV_C_W - 1)

    d_in = proj(7)
    ed = ext_slabs(sd_ref, POOL_PAST, d_in)
    memo = {}

    def wsum(i, w):
        if i < 0:
            return None
        if w == 1:
            return ed[i]
        if (i, w) not in memo:
            a, b = wsum(i, w // 2), wsum(i - w // 2, w // 2)
            memo[(i, w)] = a if b is None else a + b
        return memo[(i, w)]

    gc = dq // len(POOL_WINDOWS)
    lane_grp = lax.broadcasted_iota(I32, (n_seq, dq), 1) // gc
    for l in range(n_step):
        mean = jnp.zeros((n_seq, dq), F32)
        for g, w in enumerate(POOL_WINDOWS):
            count = min(PAST_LEN + l + 1, w)
            mean = jnp.where(lane_grp == g, wsum(POOL_PAST + l, w) * (1.0 / count), mean)
        dd = (mean - slab(d_in, l)).astype(BF16)
        cat_ref[l * n_seq:(l + 1) * n_seq, 3 * dq:4 * dq] = (
            _dot(dd, wpool_ref[...]) * pscale_ref[...]).astype(BF16)
    store_state(sd_out, ed, POOL_PAST)

    mix = _dot(cat_ref[...], wout_ref[...])
    h_out[...] = _ln(alpha * h + mix, lmg_ref[...], lmb_ref[...])


def _mixer_sample(x_tm, sa, sc, sd, lw, first_layer, alpha, n_seq, n_step):
    rows, d = x_tm.shape
    d_mix = lw['w_out'].shape[0]
    dq = d_mix // 4
    assert n_seq % SUBLANES == 0 and n_step <= CHUNK and PAST_LEN % CHUNK == 0
    ins = [x_tm, sa, sc, sd, lw['ln_in_g'], lw['ln_in_b'], lw['w_in'], lw['conv_a_w'], lw['conv_a_b'],
           lw['ln_a_g'], lw['ln_a_b'], lw['ln_v_g'], lw['ln_v_b'], lw['ws_vec'], lw['bs_vec'],
           lw['conv_c_w'], lw['w_pool_bd'], lw['pool_scale'], lw['w_out'], lw['ln_mix_g'], lw['ln_mix_b']]
    out_shape = (
        jax.ShapeDtypeStruct((rows, d), F32),
        jax.ShapeDtypeStruct(((CONV_A_W - 1) * n_seq, dq), F32),
        jax.ShapeDtypeStruct(((CONV_C_W - 1) * n_seq, dq), F32),
        jax.ShapeDtypeStruct((POOL_PAST * n_seq, dq), F32),
        jax.ShapeDtypeStruct((rows, dq), F32),
    )
    return pl.pallas_call(
        functools.partial(_mixer_sample_kernel, first_layer, alpha, n_seq, n_step, d_mix),
        grid=(1,),
        in_specs=[_full_spec(a) for a in ins],
        out_specs=tuple(pl.BlockSpec(s.shape, lambda i: (0, 0)) for s in out_shape),
        out_shape=out_shape,
        scratch_shapes=[pltpu.VMEM((rows, d_mix), BF16)],
        compiler_params=_cparams(("arbitrary",)),
        name="mixer_sample",
    )(*ins)


def _ffn_dense_kernel(alpha, x_ref, wg_ref, wu_ref, wd_ref, g_ref, b_ref, o_ref, acc_ref):
    f = pl.program_id(1)

    @pl.when(f == 0)
    def _():
        acc_ref[...] = jnp.zeros_like(acc_ref)

    xb = x_ref[...].astype(BF16)
    mid = _silu(_dot(xb, wg_ref[...])) * _dot(xb, wu_ref[...])
    acc_ref[...] += _dot(mid.astype(BF16), wd_ref[...])

    @pl.when(f == pl.num_programs(1) - 1)
    def _():
        o_ref[...] = _ln(alpha * x_ref[...] + acc_ref[...], g_ref[...], b_ref[...])


def _ffn_dense(x, wg, wu, wd, g, b, alpha, tm):
    t, d = x.shape
    dff = wg.shape[1]
    assert t % tm == 0 and dff % TF == 0
    return pl.pallas_call(
        functools.partial(_ffn_dense_kernel, alpha),
        grid=(t // tm, dff // TF),
        in_specs=[pl.BlockSpec((tm, d), lambda i, f: (i, 0)),
                  pl.BlockSpec((d, TF), lambda i, f: (0, f)),
                  pl.BlockSpec((d, TF), lambda i, f: (0, f)),
                  pl.BlockSpec((TF, d), lambda i, f: (f, 0)),
                  pl.BlockSpec((1, d), lambda i, f: (0, 0)),
                  pl.BlockSpec((1, d), lambda i, f: (0, 0))],
        out_specs=pl.BlockSpec((tm, d), lambda i, f: (i, 0)),
        out_shape=jax.ShapeDtypeStruct((t, d), F32),
        scratch_shapes=[pltpu.VMEM((tm, d), F32)],
        compiler_params=_cparams(("arbitrary", "arbitrary")),
        name="ffn_dense",
    )(x, wg, wu, wd, g, b)


def _router_kernel(tm, h_ref, whi_ref, wlo_ref, cin_ref, mi_ref, mf_ref, cnt_ref, carry_ref, below_ref):
    i = pl.program_id(0)

    @pl.when(i == 0)
    def _():
        carry_ref[...] = cin_ref[...].astype(F32)
        row = lax.broadcasted_iota(I32, (tm, tm), 0)
        col = lax.broadcasted_iota(I32, (tm, tm), 1)
        below_ref[...] = jnp.where(col < row, 1.0, 0.0).astype(BF16)

    lane = lax.broadcasted_iota(I32, (tm, LANES), 1).astype(F32)
    h = h_ref[...]
    h_hi = h.astype(BF16)
    h_lo = (h - h_hi.astype(F32)).astype(BF16)
    logits = (_dot(h_hi, whi_ref[...]) + (_dot(h_lo, whi_ref[...]) + _dot(h_hi, wlo_ref[...]))
              + _dot(h_lo, wlo_ref[...]))
    logits = jnp.where(lane < N_EXPERTS, logits, -jnp.inf)
    m1 = jnp.max(logits, axis=-1, keepdims=True)
    i1 = jnp.min(jnp.where(logits == m1, lane, float(LANES)), axis=-1, keepdims=True)
    sel1 = lane == i1
    rest = jnp.where(sel1, -jnp.inf, logits)
    m2 = jnp.max(rest, axis=-1, keepdims=True)
    i2 = jnp.min(jnp.where(rest == m2, lane, float(LANES)), axis=-1, keepdims=True)
    sel2 = lane == i2
    e2 = jnp.exp(m2 - m1)
    den = 1.0 + e2
    g1 = 1.0 / den
    g2 = e2 / den

    sel = jnp.where(sel1 | sel2, 1.0, 0.0)
    base = _dot(below_ref[...], sel.astype(BF16)) + carry_ref[0:1, :]
    r1 = jnp.sum(jnp.where(sel1, base, 0.0), axis=-1, keepdims=True)
    r2 = jnp.sum(jnp.where(sel2, base, 0.0), axis=-1, keepdims=True)
    carry_ref[...] = carry_ref[...] + jnp.sum(sel, axis=0, keepdims=True)

    mi = jnp.where(lane == 0, i1, jnp.where(lane == 1, i2, jnp.where(lane == 2, r1, r2)))
    mi_ref[...] = mi[:, 0:SUBLANES].astype(I32)
    mf = jnp.where(lane == 0, g1, g2)
    mf_ref[...] = mf[:, 0:SUBLANES]
    cnt_ref[...] = carry_ref[...].astype(I32)


def _router(h, wr_hi, wr_lo, counts_in):
    t, d = h.shape
    tm = min(TM_ROUTE, t)
    assert t % tm == 0
    return pl.pallas_call(
        functools.partial(_router_kernel, tm),
        grid=(t // tm,),
        in_specs=[pl.BlockSpec((tm, d), lambda i: (i, 0)), _full_spec(wr_hi), _full_spec(wr_lo),
                  _full_spec(counts_in)],
        out_specs=(pl.BlockSpec((tm, SUBLANES), lambda i: (i, 0)),
                   pl.BlockSpec((tm, SUBLANES), lambda i: (i, 0)),
                   pl.BlockSpec((SUBLANES, LANES), lambda i: (0, 0))),
        out_shape=(jax.ShapeDtypeStruct((t, SUBLANES), I32),
                   jax.ShapeDtypeStruct((t, SUBLANES), F32),
                   jax.ShapeDtypeStruct((SUBLANES, LANES), I32)),
        scratch_shapes=[pltpu.VMEM((SUBLANES, LANES), F32), pltpu.VMEM((tm, tm), BF16)],
        compiler_params=_cparams(("arbitrary",)),
        name="router",
    )(h, wr_hi, wr_lo, counts_in)


def _row_copy(src, src_row, dst, dst_row, sem):
    return pltpu.make_async_copy(src.at[pl.ds(src_row, 1), :], dst.at[pl.ds(dst_row, 1), :], sem)


def _rows_wait(src, dst, n_rows, sem):
    pltpu.make_async_copy(src.at[pl.ds(0, n_rows), :], dst.at[pl.ds(0, n_rows), :], sem).wait()


def _dispatch_kernel(tm, tm_expert, n_blocks, blk_ranges, zi_ref, dest_ref, *rest):
    n_src = len(blk_ranges)
    h_refs = rest[:n_src]
    xs_ref, zblk, sem, zsem = rest[n_src:]
    i = pl.program_id(0)

    for h_ref, (b0, b1) in zip(h_refs, blk_ranges):
        @pl.when((i >= b0) & (i < b1))
        def _(h_ref=h_ref, b0=b0):
            base = (i - b0) * tm

            def issue(t, carry):
                for k in range(TOP_K):
                    _row_copy(h_ref, base + t, xs_ref, dest_ref[0, k, t], sem).start(priority=k)
                return carry

            lax.fori_loop(0, tm, issue, 0, unroll=8)

    @pl.when(i == 0)
    def _():
        zblk[...] = jnp.zeros_like(zblk)
        for e in range(N_EXPERTS):
            lo = zi_ref[e] + zi_ref[N_EXPERTS + e]
            hi = zi_ref[e] + zi_ref[2 * N_EXPERTS + e]

            def zissue(r, carry):
                _row_copy(zblk, 0, xs_ref, r, zsem).start()
                return carry

            def zdrain(r, carry):
                _row_copy(zblk, 0, xs_ref, r, zsem).wait()
                return carry

            lax.fori_loop(lo, hi, zissue, 0)
            lax.fori_loop(lo, hi, zdrain, 0)

        def bcopy(b):
            return pltpu.make_async_copy(zblk, xs_ref.at[pl.ds(b * tm_expert, tm_expert), :], zsem)

        def bissue(b, carry):
            bcopy(b).start()
            return carry

        def bdrain(b, carry):
            bcopy(b).wait()
            return carry

        lax.fori_loop(zi_ref[3 * N_EXPERTS], n_blocks, bissue, 0)
        lax.fori_loop(zi_ref[3 * N_EXPERTS], n_blocks, bdrain, 0)

    @pl.when(i > 0)
    def _():
        for _k in range(TOP_K):
            _rows_wait(h_refs[0], xs_ref, tm, sem)

    @pl.when(i == pl.num_programs(0) - 1)
    def _():
        for _k in range(TOP_K):
            _rows_wait(h_refs[0], xs_ref, tm, sem)


def _dispatch(hs_list, dest_blk, zinfo, n_blocks, tm_expert):
    d = hs_list[0].shape[1]
    nb, _, tm = dest_blk.shape
    blk_ranges, b0 = [], 0
    for h in hs_list:
        assert h.shape[0] % tm == 0 and h.shape[0] >= tm
        blk_ranges.append((b0, b0 + h.shape[0] // tm))
        b0 = blk_ranges[-1][1]
    assert b0 == nb
    any_spec = pl.BlockSpec(memory_space=pl.ANY)
    return pl.pallas_call(
        functools.partial(_dispatch_kernel, tm, tm_expert, n_blocks, tuple(blk_ranges)),
        grid_spec=pltpu.PrefetchScalarGridSpec(
            num_scalar_prefetch=1,
            grid=(nb,),
            in_specs=[pl.BlockSpec((1, TOP_K, tm), lambda i, zi: (i, 0, 0), memory_space=pltpu.SMEM)]
            + [any_spec] * len(hs_list),
            out_specs=any_spec,
            scratch_shapes=[pltpu.VMEM((tm_expert, d), F32), pltpu.SemaphoreType.DMA(()),
                            pltpu.SemaphoreType.DMA(())],
        ),
        out_shape=jax.ShapeDtypeStruct((n_blocks * tm_expert, d), F32),
        compiler_params=_cparams(("arbitrary",)),
        name="moe_dispatch",
    )(zinfo, dest_blk, *hs_list)


def _ffn_moe_kernel(be_ref, nu_ref, x_ref, wg_ref, wu_ref, wd_ref, y_ref, acc_ref):
    b = pl.program_id(0)
    f = pl.program_id(1)

    @pl.when(b < nu_ref[0])
    def _():
        @pl.when(f == 0)
        def _():
            acc_ref[...] = jnp.zeros_like(acc_ref)

        xb = x_ref[...].astype(BF16)
        mid = _silu(_dot(xb, wg_ref[0])) * _dot(xb, wu_ref[0])
        acc_ref[...] += _dot(mid.astype(BF16), wd_ref[0])

        @pl.when(f == pl.num_programs(1) - 1)
        def _():
            y_ref[...] = acc_ref[...]

    @pl.when((b >= nu_ref[0]) & (f == pl.num_programs(1) - 1))
    def _():
        y_ref[...] = jnp.zeros_like(y_ref)


def _ffn_moe(xs, wg, wu, wd, block_e, n_used, tm):
    cap, d = xs.shape
    dff = wg.shape[2]
    nf = dff // TF
    assert cap % tm == 0 and dff % TF == 0

    def blk(b, nu):
        return jnp.minimum(b, nu[0] - 1)

    def fcol(b, f, nu):
        return jnp.where(b < nu[0], f, nf - 1)

    return pl.pallas_call(
        _ffn_moe_kernel,
        grid_spec=pltpu.PrefetchScalarGridSpec(
            num_scalar_prefetch=2,
            grid=(cap // tm, nf),
            in_specs=[pl.BlockSpec((tm, d), lambda b, f, be, nu: (blk(b, nu), 0)),
                      pl.BlockSpec((1, d, TF), lambda b, f, be, nu: (be[b], 0, fcol(b, f, nu))),
                      pl.BlockSpec((1, d, TF), lambda b, f, be, nu: (be[b], 0, fcol(b, f, nu))),
                      pl.BlockSpec((1, TF, d), lambda b, f, be, nu: (be[b], fcol(b, f, nu), 0))],
            out_specs=pl.BlockSpec((tm, d), lambda b, f, be, nu: (b, 0)),
            scratch_shapes=[pltpu.VMEM((tm, d), F32)],
        ),
        out_shape=jax.ShapeDtypeStruct((cap, d), F32),
        compiler_params=_cparams(("arbitrary", "arbitrary")),
        name="ffn_moe",
    )(block_e, n_used, xs, wg, wu, wd)


def _combine_kernel(alpha, tm, dcur_ref, dnext_ref, h_ref, gate_ref, g_ref, b_ref, ys_ref, o_ref, ybuf, sems):
    i = pl.program_id(0)
    n = pl.num_programs(0)

    def issue(dref, slot):
        def body(t, carry):
            for k in range(TOP_K):
                _row_copy(ys_ref, dref[0, k, t], ybuf.at[slot, k], t, sems.at[slot]).start(priority=k)
            return carry

        lax.fori_loop(0, tm, body, 0, unroll=8)

    @pl.when(i == 0)
    def _():
        issue(dcur_ref, 0)

    @pl.when(i + 1 < n)
    def _():
        issue(dnext_ref, (i + 1) % 2)

    slot = i % 2
    for k in range(TOP_K):
        _rows_wait(ys_ref, ybuf.at[slot, k], tm, sems.at[slot])

    gates = gate_ref[...]
    ffn = gates[:, 0:1] * ybuf[slot, 0] + gates[:, 1:2] * ybuf[slot, 1]
    o_ref[...] = _ln(alpha * h_ref[...] + ffn, g_ref[...], b_ref[...])


def _combine(h, dest_blk, gates, ys, g, b, alpha):
    t, d = h.shape
    nb, _, tm = dest_blk.shape
    dest_spec = lambda imap: pl.BlockSpec((1, TOP_K, tm), imap, memory_space=pltpu.SMEM)
    return pl.pallas_call(
        functools.partial(_combine_kernel, alpha, tm),
        grid=(nb,),
        in_specs=[dest_spec(lambda i: (i, 0, 0)),
                  dest_spec(lambda i: (jnp.minimum(i + 1, nb - 1), 0, 0)),
                  pl.BlockSpec((tm, d), lambda i: (i, 0)),
                  pl.BlockSpec((tm, SUBLANES), lambda i: (i, 0)),
                  pl.BlockSpec((1, d), lambda i: (0, 0)),
                  pl.BlockSpec((1, d), lambda i: (0, 0)),
                  pl.BlockSpec(memory_space=pl.ANY)],
        out_specs=pl.BlockSpec((tm, d), lambda i: (i, 0)),
        out_shape=jax.ShapeDtypeStruct((t, d), F32),
        scratch_shapes=[pltpu.VMEM((2, TOP_K, tm, d), F32), pltpu.SemaphoreType.DMA((2,))],
        compiler_params=_cparams(("arbitrary",)),
        name="moe_combine",
    )(dest_blk, dest_blk, h, gates, g, b, ys)


def _ffn_routed(hs_list, mw, g, b, alpha, tm_expert):
    d = hs_list[0].shape[1]
    counts_in = jnp.zeros((SUBLANES, LANES), I32)
    routed = []
    for h in hs_list:
        meta_i, gates, counts_in = _router(h, mw['w_router_hi'], mw['w_router_lo'], counts_in)
        routed.append((meta_i, gates))
    counts = counts_in[0, :N_EXPERTS]
    padded = (counts + tm_expert - 1) // tm_expert * tm_expert
    pad_end = jnp.cumsum(padded)
    pad_start = (pad_end - padded).astype(I32)
    n_assign = sum(h.shape[0] for h in hs_list) * TOP_K
    n_blocks = -(-n_assign // tm_expert) + N_EXPERTS
    cap = n_blocks * tm_expert
    n_used = (pad_end[-1] // tm_expert).astype(I32)
    blk_start = jnp.minimum(jnp.arange(n_blocks, dtype=I32), n_used - 1) * tm_expert
    block_e = jnp.minimum(jnp.searchsorted(pad_end, blk_start, side='right'), N_EXPERTS - 1).astype(I32)
    zinfo = jnp.concatenate([pad_start, counts, padded, n_used.reshape(1)]).astype(I32)

    dests = []
    for h, (meta_i, _) in zip(hs_list, routed):
        dest = pad_start[meta_i[:, 0:TOP_K]] + meta_i[:, TOP_K:2 * TOP_K]
        dests.append(dest.reshape(h.shape[0] // TM_MOVE, TM_MOVE, TOP_K).transpose(0, 2, 1))
    xs = _dispatch(hs_list, jnp.concatenate(dests, axis=0), zinfo, n_blocks, tm_expert)
    ys = _ffn_moe(xs, mw['w_e_gate'], mw['w_e_up'], mw['w_e_down'], block_e, n_used.reshape(1), tm_expert)
    return [_combine(h, dest_blk, gates, ys, g, b, alpha)
            for h, dest_blk, (_, gates) in zip(hs_list, dests, routed)]


def _prep_layer(i, p, n_step):
    dq = p['conv_a_w'].shape[2]
    hd = dq // N_HEADS_B
    tril = jnp.tril(jnp.ones((CHUNK, CHUNK), dtype=bool))
    w_s = p['w_s'][i]
    b_s = p['b_s'][i]
    wp = p['w_pool'][i]
    ng, gc = wp.shape[0], wp.shape[1]
    w_pool_bd = jnp.zeros((dq, dq), F32)
    for g in range(ng):
        w_pool_bd = w_pool_bd.at[g * gc:(g + 1) * gc, g * gc:(g + 1) * gc].set(wp[g])
    row = lambda a: a.reshape(1, -1)
    return dict(
        ln_in_g=row(p['ln_in_g']), ln_in_b=row(p['ln_in_b']),
        w_in=p['w_in'][i].astype(BF16),
        conv_a_w=p['conv_a_w'][i], conv_a_b=row(p['conv_a_b'][i]),
        ln_a_g=row(p['ln_a_g'][i]), ln_a_b=row(p['ln_a_b'][i]),
        ln_v_g=row(p['ln_v_g'][i]), ln_v_b=row(p['ln_v_b'][i]),
        ws_tril=jnp.where(tril[None], w_s, 0).astype(BF16),
        bs_mat=jnp.repeat(b_s.T, hd, axis=1),
        ws_vec=jnp.repeat(jnp.where(tril[None], w_s, 0)[:, :n_step, :n_step].transpose(1, 2, 0)
                          .reshape(n_step * n_step, N_HEADS_B), hd, axis=1),
        bs_vec=jnp.repeat(b_s[:, :n_step].T, hd, axis=1),
        conv_c_w=p['conv_c_w'][i],
        w_pool_bd=w_pool_bd.astype(BF16), pool_scale=row(p['pool_scale'][i]),
        w_out=p['w_out'][i].astype(BF16),
        ln_mix_g=row(p['ln_mix_g'][i]), ln_mix_b=row(p['ln_mix_b'][i]),
    )


def kernel(x_prompt, x_sample, state_conv_a, state_conv_c, state_pool_d, ln_in_g, ln_in_b, w_in, conv_a_w,
           conv_a_b, ln_a_g, ln_a_b, ln_v_g, ln_v_b, w_s, b_s, conv_c_w, w_pool, pool_scale, w_out,
           ln_mix_g, ln_mix_b, w_ff_gate, w_ff_up, w_ff_down, w_router, w_e_gate, w_e_up, w_e_down,
           ln_ffn_g, ln_ffn_b):
    p = dict(ln_in_g=ln_in_g, ln_in_b=ln_in_b, w_in=w_in, conv_a_w=conv_a_w, conv_a_b=conv_a_b,
             ln_a_g=ln_a_g, ln_a_b=ln_a_b, ln_v_g=ln_v_g, ln_v_b=ln_v_b, w_s=w_s, b_s=b_s,
             conv_c_w=conv_c_w, w_pool=w_pool, pool_scale=pool_scale, w_out=w_out,
             ln_mix_g=ln_mix_g, ln_mix_b=ln_mix_b)
    depth = w_in.shape[0]
    bsz, seq, d = x_prompt.shape
    n_seq, n_step, _ = x_sample.shape
    dq = conv_a_w.shape[2]
    alpha = float((2.0 * depth) ** 0.25)

    hp = x_prompt
    hs = x_sample.transpose(1, 0, 2).reshape(n_step * n_seq, d)
    outs = {k: [] for k in ('pa', 'pc', 'pd', 'pv', 'sa', 'sc', 'sd', 'sv')}
    for i in range(depth):
        lw = _prep_layer(i, p, n_step)
        hp, pa, pc, pd, pv = _mixer_prompt(hp, lw, i == 0, alpha)
        tm_state = lambda s: s.transpose(1, 0, 2).reshape(-1, dq)
        hs, sa, sc, sd, sv = _mixer_sample(hs, tm_state(state_conv_a[i]), tm_state(state_conv_c[i]),
                                           tm_state(state_pool_d[i]), lw, i == 0, alpha, n_seq, n_step)
        lg, lb = ln_ffn_g[i].reshape(1, -1), ln_ffn_b[i].reshape(1, -1)
        j = i // 2
        if i % 2 == 0:
            wg, wu, wd = w_ff_gate[j].astype(BF16), w_ff_up[j].astype(BF16), w_ff_down[j].astype(BF16)
            hp = _ffn_dense(hp.reshape(bsz * seq, d), wg, wu, wd, lg, lb, alpha, TM_DENSE).reshape(bsz, seq, d)
            hs = _ffn_dense(hs, wg, wu, wd, lg, lb, alpha, hs.shape[0])
        else:
            wr = jnp.pad(w_router[j], ((0, 0), (0, LANES - N_EXPERTS)))
            wr_hi = wr.astype(BF16)
            mw = dict(w_router_hi=wr_hi, w_router_lo=(wr - wr_hi.astype(F32)).astype(BF16),
                      w_e_gate=w_e_gate[j].astype(BF16), w_e_up=w_e_up[j].astype(BF16),
                      w_e_down=w_e_down[j].astype(BF16))
            hp, hs = _ffn_routed([hp.reshape(bsz * seq, d), hs], mw, lg, lb, alpha, TM_EXPERT)
            hp = hp.reshape(bsz, seq, d)
        back = lambda a: a.reshape(-1, n_seq, dq).transpose(1, 0, 2)
        for k, val in zip(('pa', 'pc', 'pd', 'pv', 'sa', 'sc', 'sd', 'sv'),
                          (pa, pc, pd, pv, back(sa), back(sc), back(sd), back(sv))):
            outs[k].append(val)
    y_sample = hs.reshape(n_step, n_seq, d).transpose(1, 0, 2)
    st = lambda k: jnp.stack(outs[k])
    return (hp, y_sample, st('pa'), st('pc'), st('pd'), st('pv'), st('sa'), st('sc'), st('sd'), st('sv'))
```

```python
import functools

import numpy as np
import jax
import jax.numpy as jnp
from jax import lax
from jax.experimental import pallas as pl
from jax.experimental.pallas import tpu as pltpu

F32 = jnp.float32
BF16 = jnp.bfloat16
I32 = jnp.int32

PAST_LEN = 16384
CHUNK = 128
N_HEADS_B = 4
POOL_WINDOWS = (2, 4, 8, 16)
POOL_PAST = 15
CONV_A_W = 31
CONV_C_W = 3
N_EXPERTS = 8
TOP_K = 2
LN_EPS = 1e-5
SQRT_HALF = float(np.sqrt(0.5).astype(np.float32))

LANES = 128
SUBLANES = 8
VMEM_LIMIT_BYTES = 56 * 1024 * 1024

HIST_A = 32
HIST_C = 8
HIST_D = 24

TL_PROMPT = 512
TM_DENSE = 1024
TF = 512
TM_ROUTE = 512
TM_MOVE = 256
TM_EXPERT = 512
N_STAGE = 3


def _ln(x, g, b):
    mu = jnp.mean(x, axis=-1, keepdims=True)
    xc = x - mu
    var = jnp.mean(xc * xc, axis=-1, keepdims=True)
    return xc * lax.rsqrt(var + LN_EPS) * g + b


def _gelu(x):
    return 0.5 * x * (1.0 + lax.erf(x * SQRT_HALF))


def _silu(x):
    return x * jax.nn.sigmoid(x)


def _dot(a, b):
    return jnp.dot(a, b, preferred_element_type=F32)


def _cparams(sem):
    return pltpu.CompilerParams(dimension_semantics=sem, vmem_limit_bytes=VMEM_LIMIT_BYTES)


def _mixer_prompt_kernel(first_layer, alpha, tl, d_mix,
                         x_ref, lng_ref, lnb_ref, win_ref, caw_ref, cab_ref, lag_ref, lab_ref,
                         lvg_ref, lvb_ref, ws_ref, bsm_ref, ccw_ref, wpool_ref, pscale_ref,
                         wout_ref, lmg_ref, lmb_ref,
                         h_out, sa_out, sc_out, sd_out, v_out,
                         ext_a, ext_c, ext_d, cat_ref):
    dq = d_mix // 4
    j = pl.program_id(1)
    nj = pl.num_programs(1)

    @pl.when(j == 0)
    def _():
        ext_a[0, 0:HIST_A, :] = jnp.zeros((HIST_A, dq), F32)
        ext_c[0:HIST_C, :] = jnp.zeros((HIST_C, dq), F32)
        ext_d[0:HIST_D, :] = jnp.zeros((HIST_D, dq), F32)

    x = x_ref[0]
    h = _ln(x, lng_ref[...], lnb_ref[...]) if first_layer else x
    hb = h.astype(BF16)

    def proj(i):
        return _dot(hb, win_ref[:, i * dq:(i + 1) * dq])

    a_glu = proj(0) * jax.nn.sigmoid(proj(1))
    ext_a[0, HIST_A:HIST_A + tl, :] = a_glu
    n_ext = HIST_A + tl
    full = ext_a[0]
    for s in range(1, SUBLANES):
        ext_a[s, 0:n_ext - SUBLANES, :] = pltpu.roll(full, n_ext - s, axis=0)[0:n_ext - SUBLANES, :]
    rb = 64
    off_a = HIST_A - (CONV_A_W - 1)
    for r0 in range(0, tl, rb):
        acc = jnp.zeros((rb, dq), F32)
        for k in range(CONV_A_W):
            s = (off_a + k) % SUBLANES
            row = off_a + k - s + r0
            acc = acc + ext_a[s, row:row + rb, :] * caw_ref[k:k + 1, :]
        y = _silu(_ln(acc + cab_ref[...], lag_ref[...], lab_ref[...]))
        cat_ref[r0:r0 + rb, 0:dq] = y.astype(BF16)

    u = _gelu(proj(2))
    v = _ln(_gelu(proj(3)), lvg_ref[...], lvb_ref[...])
    hd = dq // N_HEADS_B
    lane_head = lax.broadcasted_iota(I32, (CHUNK, dq), 1) // hd
    for c in range(tl // CHUNK):
        vc = v[c * CHUNK:(c + 1) * CHUNK, :]
        s = bsm_ref[...]
        for hh in range(N_HEADS_B):
            vm = jnp.where(lane_head == hh, vc, 0.0).astype(BF16)
            s = s + _dot(ws_ref[hh], vm)
        cat_ref[c * CHUNK:(c + 1) * CHUNK, dq:2 * dq] = (u[c * CHUNK:(c + 1) * CHUNK, :] * s).astype(BF16)

    c_x = proj(4)
    c_b = proj(5)
    c_c = proj(6)
    gx = c_c * c_x
    ext_c[HIST_C:HIST_C + tl, :] = gx
    off_c = HIST_C - (CONV_C_W - 1)
    conv_c = gx * ccw_ref[CONV_C_W - 1:CONV_C_W, :]
    for k in range(CONV_C_W - 1):
        conv_c = conv_c + ext_c[off_c + k:off_c + k + tl, :] * ccw_ref[k:k + 1, :]
    cat_ref[:, 2 * dq:3 * dq] = (c_b * conv_c).astype(BF16)

    d_in = proj(7)
    ext_d[HIST_D:HIST_D + tl, :] = d_in
    e = ext_d[...]
    s2 = e + pltpu.roll(e, 1, axis=0)
    s4 = s2 + pltpu.roll(s2, 2, axis=0)
    s8 = s4 + pltpu.roll(s4, 4, axis=0)
    s16 = s8 + pltpu.roll(s8, 8, axis=0)
    sums = (s2, s4, s8, s16)
    pos = (j * tl + lax.broadcasted_iota(I32, (tl, 1), 0)).astype(F32)
    gc = dq // len(POOL_WINDOWS)
    lane_grp = lax.broadcasted_iota(I32, (tl, dq), 1) // gc
    mean = jnp.zeros((tl, dq), F32)
    for g, w in enumerate(POOL_WINDOWS):
        inv = 1.0 / jnp.minimum(pos + 1.0, float(w))
        mean = jnp.where(lane_grp == g, sums[g][HIST_D:HIST_D + tl, :] * inv, mean)
    dd = (mean - d_in).astype(BF16)
    cat_ref[:, 3 * dq:4 * dq] = (_dot(dd, wpool_ref[...]) * pscale_ref[...]).astype(BF16)

    mix = _dot(cat_ref[...], wout_ref[...])
    h_out[0] = _ln(alpha * h + mix, lmg_ref[...], lmb_ref[...])

    @pl.when(j == nj - 1)
    def _():
        sa_out[0] = ext_a[0, HIST_A + tl - (CONV_A_W - 1):HIST_A + tl, :]
        sc_out[0] = ext_c[HIST_C + tl - (CONV_C_W - 1):HIST_C + tl, :]
        sd_out[0] = ext_d[HIST_D + tl - POOL_PAST:HIST_D + tl, :]
        v_out[0] = v[tl - CHUNK:tl, :]

    ext_a[0, 0:HIST_A, :] = ext_a[0, tl:tl + HIST_A, :]
    ext_c[0:HIST_C, :] = ext_c[tl:tl + HIST_C, :]
    ext_d[0:HIST_D, :] = ext_d[tl:tl + HIST_D, :]


def _full_spec(arr):
    nd = arr.ndim
    return pl.BlockSpec(arr.shape, lambda *_: (0,) * nd)


def _mixer_prompt(x, lw, first_layer, alpha):
    bsz, seq, d = x.shape
    d_mix = lw['w_out'].shape[0]
    dq = d_mix // 4
    tl = TL_PROMPT
    assert seq % tl == 0 and tl % CHUNK == 0 and seq >= CHUNK
    params = [lw['ln_in_g'], lw['ln_in_b'], lw['w_in'], lw['conv_a_w'], lw['conv_a_b'], lw['ln_a_g'],
              lw['ln_a_b'], lw['ln_v_g'], lw['ln_v_b'], lw['ws_tril'], lw['bs_mat'], lw['conv_c_w'],
              lw['w_pool_bd'], lw['pool_scale'], lw['w_out'], lw['ln_mix_g'], lw['ln_mix_b']]
    out_shape = (
        jax.ShapeDtypeStruct((bsz, seq, d), F32),
        jax.ShapeDtypeStruct((bsz, CONV_A_W - 1, dq), F32),
        jax.ShapeDtypeStruct((bsz, CONV_C_W - 1, dq), F32),
        jax.ShapeDtypeStruct((bsz, POOL_PAST, dq), F32),
        jax.ShapeDtypeStruct((bsz, CHUNK, dq), F32),
    )
    state_spec = lambda rows: pl.BlockSpec((1, rows, dq), lambda b, j: (b, 0, 0))
    return pl.pallas_call(
        functools.partial(_mixer_prompt_kernel, first_layer, alpha, tl, d_mix),
        grid=(bsz, seq // tl),
        in_specs=[pl.BlockSpec((1, tl, d), lambda b, j: (b, j, 0))] + [_full_spec(p) for p in params],
        out_specs=(pl.BlockSpec((1, tl, d), lambda b, j: (b, j, 0)),
                   state_spec(CONV_A_W - 1), state_spec(CONV_C_W - 1), state_spec(POOL_PAST),
                   state_spec(CHUNK)),
        out_shape=out_shape,
        scratch_shapes=[pltpu.VMEM((SUBLANES, HIST_A + tl, dq), F32), pltpu.VMEM((HIST_C + tl, dq), F32),
                        pltpu.VMEM((HIST_D + tl, dq), F32), pltpu.VMEM((tl, d_mix), BF16)],
        compiler_params=_cparams(("arbitrary", "arbitrary")),
        name="mixer_prompt",
    )(x, *params)


def _mixer_sample_kernel(first_layer, alpha, n_seq, n_step, d_mix,
                         x_ref, sa_ref, sc_ref, sd_ref, lng_ref, lnb_ref, win_ref, caw_ref, cab_ref,
                         lag_ref, lab_ref, lvg_ref, lvb_ref, wsv_ref, bsv_ref, ccw_ref, wpool_ref,
                         pscale_ref, wout_ref, lmg_ref, lmb_ref,
                         h_out, sa_out, sc_out, sd_out, v_out, cat_ref):
    dq = d_mix // 4
    x = x_ref[...]
    h = _ln(x, lng_ref[...], lnb_ref[...]) if first_layer else x
    hb = h.astype(BF16)

    def proj(i):
        return _dot(hb, win_ref[:, i * dq:(i + 1) * dq])

    def slab(val, i):
        return val[i * n_seq:(i + 1) * n_seq, :]

    def ext_slabs(state_ref, n_past, cur):
        return ([state_ref[i * n_seq:(i + 1) * n_seq, :] for i in range(n_past)]
                + [slab(cur, i) for i in range(n_step)])

    def store_state(out_ref, slabs, n_keep):
        for i, sl in enumerate(slabs[len(slabs) - n_keep:]):
            out_ref[i * n_seq:(i + 1) * n_seq, :] = sl

    a_glu = proj(0) * jax.nn.sigmoid(proj(1))
    ea = ext_slabs(sa_ref, CONV_A_W - 1, a_glu)
    for l in range(n_step):
        acc = jnp.zeros((n_seq, dq), F32)
        for k in range(CONV_A_W):
            acc = acc + ea[l + k] * caw_ref[k:k + 1, :]
        y = _silu(_ln(acc + cab_ref[...], lag_ref[...], lab_ref[...]))
        cat_ref[l * n_seq:(l + 1) * n_seq, 0:dq] = y.astype(BF16)
    store_state(sa_out, ea, CONV_A_W - 1)

    u = _gelu(proj(2))
    v = _ln(_gelu(proj(3)), lvg_ref[...], lvb_ref[...])
    v_out[...] = v
    for l in range(n_step):
        s = jnp.zeros((n_seq, dq), F32) + bsv_ref[l:l + 1, :]
        for m in range(l + 1):
            s = s + slab(v, m) * wsv_ref[l * n_step + m:l * n_step + m + 1, :]
        cat_ref[l * n_seq:(l + 1) * n_seq, dq:2 * dq] = (slab(u, l) * s).astype(BF16)

    c_x = proj(4)
    c_b = proj(5)
    c_c = proj(6)
    gx = c_c * c_x
    ec = ext_slabs(sc_ref, CONV_C_W - 1, gx)
    for l in range(n_step):
        acc = jnp.zeros((n_seq, dq), F32)
        for k in range(CONV_C_W):
            acc = acc + ec[l + k] * ccw_ref[k:k + 1, :]
        cat_ref[l * n_seq:(l + 1) * n_seq, 2 * dq:3 * dq] = (slab(c_b, l) * acc).astype(BF16)
    store_state(sc_out, ec, CONV_C_W - 1)

    d_in = proj(7)
    ed = ext_slabs(sd_ref, POOL_PAST, d_in)
    memo = {}

    def wsum(i, w):
        if i < 0:
            return None
        if w == 1:
            return ed[i]
        if (i, w) not in memo:
            a, b = wsum(i, w // 2), wsum(i - w // 2, w // 2)
            memo[(i, w)] = a if b is None else a + b
        return memo[(i, w)]

    gc = dq // len(POOL_WINDOWS)
    lane_grp = lax.broadcasted_iota(I32, (n_seq, dq), 1) // gc
    for l in range(n_step):
        mean = jnp.zeros((n_seq, dq), F32)
        for g, w in enumerate(POOL_WINDOWS):
            count = min(PAST_LEN + l + 1, w)
            mean = jnp.where(lane_grp == g, wsum(POOL_PAST + l, w) * (1.0 / count), mean)
        dd = (mean - slab(d_in, l)).astype(BF16)
        cat_ref[l * n_seq:(l + 1) * n_seq, 3 * dq:4 * dq] = (
            _dot(dd, wpool_ref[...]) * pscale_ref[...]).astype(BF16)
    store_state(sd_out, ed, POOL_PAST)

    mix = _dot(cat_ref[...], wout_ref[...])
    h_out[...] = _ln(alpha * h + mix, lmg_ref[...], lmb_ref[...])


def _mixer_sample(x_tm, sa, sc, sd, lw, first_layer, alpha, n_seq, n_step):
    rows, d = x_tm.shape
    d_mix = lw['w_out'].shape[0]
    dq = d_mix // 4
    assert n_seq % SUBLANES == 0 and n_step <= CHUNK and PAST_LEN % CHUNK == 0
    ins = [x_tm, sa, sc, sd, lw['ln_in_g'], lw['ln_in_b'], lw['w_in'], lw['conv_a_w'], lw['conv_a_b'],
           lw['ln_a_g'], lw['ln_a_b'], lw['ln_v_g'], lw['ln_v_b'], lw['ws_vec'], lw['bs_vec'],
           lw['conv_c_w'], lw['w_pool_bd'], lw['pool_scale'], lw['w_out'], lw['ln_mix_g'], lw['ln_mix_b']]
    out_shape = (
        jax.ShapeDtypeStruct((rows, d), F32),
        jax.ShapeDtypeStruct(((CONV_A_W - 1) * n_seq, dq), F32),
        jax.ShapeDtypeStruct(((CONV_C_W - 1) * n_seq, dq), F32),
        jax.ShapeDtypeStruct((POOL_PAST * n_seq, dq), F32),
        jax.ShapeDtypeStruct((rows, dq), F32),
    )
    return pl.pallas_call(
        functools.partial(_mixer_sample_kernel, first_layer, alpha, n_seq, n_step, d_mix),
        grid=(1,),
        in_specs=[_full_spec(a) for a in ins],
        out_specs=tuple(pl.BlockSpec(s.shape, lambda i: (0, 0)) for s in out_shape),
        out_shape=out_shape,
        scratch_shapes=[pltpu.VMEM((rows, d_mix), BF16)],
        compiler_params=_cparams(("arbitrary",)),
        name="mixer_sample",
    )(*ins)


def _ffn_dense_kernel(alpha, x_ref, wg_ref, wu_ref, wd_ref, g_ref, b_ref, o_ref, acc_ref):
    f = pl.program_id(1)

    @pl.when(f == 0)
    def _():
        acc_ref[...] = jnp.zeros_like(acc_ref)

    xb = x_ref[...].astype(BF16)
    mid = _silu(_dot(xb, wg_ref[...])) * _dot(xb, wu_ref[...])
    acc_ref[...] += _dot(mid.astype(BF16), wd_ref[...])

    @pl.when(f == pl.num_programs(1) - 1)
    def _():
        o_ref[...] = _ln(alpha * x_ref[...] + acc_ref[...], g_ref[...], b_ref[...])


def _ffn_dense(x, wg, wu, wd, g, b, alpha, tm):
    t, d = x.shape
    dff = wg.shape[1]
    assert t % tm == 0 and dff % TF == 0
    return pl.pallas_call(
        functools.partial(_ffn_dense_kernel, alpha),
        grid=(t // tm, dff // TF),
        in_specs=[pl.BlockSpec((tm, d), lambda i, f: (i, 0)),
                  pl.BlockSpec((d, TF), lambda i, f: (0, f)),
                  pl.BlockSpec((d, TF), lambda i, f: (0, f)),
                  pl.BlockSpec((TF, d), lambda i, f: (f, 0)),
                  pl.BlockSpec((1, d), lambda i, f: (0, 0)),
                  pl.BlockSpec((1, d), lambda i, f: (0, 0))],
        out_specs=pl.BlockSpec((tm, d), lambda i, f: (i, 0)),
        out_shape=jax.ShapeDtypeStruct((t, d), F32),
        scratch_shapes=[pltpu.VMEM((tm, d), F32)],
        compiler_params=_cparams(("arbitrary", "arbitrary")),
        name="ffn_dense",
    )(x, wg, wu, wd, g, b)


def _router_kernel(tm, h_ref, whi_ref, wlo_ref, cin_ref, mi_ref, mf_ref, cnt_ref, carry_ref, below_ref):
    i = pl.program_id(0)

    @pl.when(i == 0)
    def _():
        carry_ref[...] = cin_ref[...].astype(F32)
        row = lax.broadcasted_iota(I32, (tm, tm), 0)
        col = lax.broadcasted_iota(I32, (tm, tm), 1)
        below_ref[...] = jnp.where(col < row, 1.0, 0.0).astype(BF16)

    lane = lax.broadcasted_iota(I32, (tm, LANES), 1).astype(F32)
    h = h_ref[...]
    h_hi = h.astype(BF16)
    h_lo = (h - h_hi.astype(F32)).astype(BF16)
    logits = (_dot(h_hi, whi_ref[...]) + (_dot(h_lo, whi_ref[...]) + _dot(h_hi, wlo_ref[...]))
              + _dot(h_lo, wlo_ref[...]))
    logits = jnp.where(lane < N_EXPERTS, logits, -jnp.inf)
    m1 = jnp.max(logits, axis=-1, keepdims=True)
    i1 = jnp.min(jnp.where(logits == m1, lane, float(LANES)), axis=-1, keepdims=True)
    sel1 = lane == i1
    rest = jnp.where(sel1, -jnp.inf, logits)
    m2 = jnp.max(rest, axis=-1, keepdims=True)
    i2 = jnp.min(jnp.where(rest == m2, lane, float(LANES)), axis=-1, keepdims=True)
    sel2 = lane == i2
    e2 = jnp.exp(m2 - m1)
    den = 1.0 + e2
    g1 = 1.0 / den
    g2 = e2 / den

    sel = jnp.where(sel1 | sel2, 1.0, 0.0)
    base = _dot(below_ref[...], sel.astype(BF16)) + carry_ref[0:1, :]
    r1 = jnp.sum(jnp.where(sel1, base, 0.0), axis=-1, keepdims=True)
    r2 = jnp.sum(jnp.where(sel2, base, 0.0), axis=-1, keepdims=True)
    carry_ref[...] = carry_ref[...] + jnp.sum(sel, axis=0, keepdims=True)

    mi = jnp.where(lane == 0, i1, jnp.where(lane == 1, i2, jnp.where(lane == 2, r1, r2)))
    mi_ref[...] = mi[:, 0:SUBLANES].astype(I32)
    mf = jnp.where(lane == 0, g1, g2)
    mf_ref[...] = mf[:, 0:SUBLANES]
    cnt_ref[...] = carry_ref[...].astype(I32)


def _router(h, wr_hi, wr_lo, counts_in):
    t, d = h.shape
    tm = min(TM_ROUTE, t)
    assert t % tm == 0
    return pl.pallas_call(
        functools.partial(_router_kernel, tm),
        grid=(t // tm,),
        in_specs=[pl.BlockSpec((tm, d), lambda i: (i, 0)), _full_spec(wr_hi), _full_spec(wr_lo),
                  _full_spec(counts_in)],
        out_specs=(pl.BlockSpec((tm, SUBLANES), lambda i: (i, 0)),
                   pl.BlockSpec((tm, SUBLANES), lambda i: (i, 0)),
                   pl.BlockSpec((SUBLANES, LANES), lambda i: (0, 0))),
        out_shape=(jax.ShapeDtypeStruct((t, SUBLANES), I32),
                   jax.ShapeDtypeStruct((t, SUBLANES), F32),
                   jax.ShapeDtypeStruct((SUBLANES, LANES), I32)),
        scratch_shapes=[pltpu.VMEM((SUBLANES, LANES), F32), pltpu.VMEM((tm, tm), BF16)],
        compiler_params=_cparams(("arbitrary",)),
        name="router",
    )(h, wr_hi, wr_lo, counts_in)


def _row_copy(src, src_row, dst, dst_row, sem):
    return pltpu.make_async_copy(src.at[pl.ds(src_row, 1), :], dst.at[pl.ds(dst_row, 1), :], sem)


def _rows_wait(src, dst, n_rows, sem):
    pltpu.make_async_copy(src.at[pl.ds(0, n_rows), :], dst.at[pl.ds(0, n_rows), :], sem).wait()


def _dispatch_kernel(tm, tm_expert, n_blocks, blk_ranges, zi_ref, dest_ref, *rest):
    n_src = len(blk_ranges)
    h_refs = rest[:n_src]
    xs_ref, xbuf, zblk, lsems, rsems, zsem = rest[n_src:]
    i = pl.program_id(0)
    n = pl.num_programs(0)

    def block_load(g, start):
        slot = lax.rem(g, N_STAGE)
        for h_ref, (b0, b1) in zip(h_refs, blk_ranges):
            @pl.when((g >= b0) & (g < b1))
            def _(h_ref=h_ref, b0=b0):
                cp = pltpu.make_async_copy(h_ref.at[pl.ds((g - b0) * tm, tm), :], xbuf.at[slot],
                                           lsems.at[slot])
                if start:
                    cp.start()
                else:
                    cp.wait()

    def rows_wait(g):
        slot = lax.rem(g, N_STAGE)
        for _k in range(TOP_K):
            _rows_wait(xbuf.at[slot], xs_ref, tm, rsems.at[slot])

    @pl.when(i == 0)
    def _():
        block_load(i, True)

    @pl.when(i >= N_STAGE - 1)
    def _():
        rows_wait(i - (N_STAGE - 1))

    @pl.when(i + 1 < n)
    def _():
        block_load(i + 1, True)

    block_load(i, False)
    cur = lax.rem(i, N_STAGE)

    def issue(t, carry):
        for k in range(TOP_K):
            _row_copy(xbuf.at[cur], t, xs_ref, dest_ref[0, k, t], rsems.at[cur]).start(priority=k)
        return carry

    lax.fori_loop(0, tm, issue, 0, unroll=8)

    @pl.when(i == n - 1)
    def _():
        for back in range(N_STAGE - 2, -1, -1):
            @pl.when(i >= back)
            def _(back=back):
                rows_wait(i - back)

    @pl.when(i == 0)
    def _():
        zblk[...] = jnp.zeros_like(zblk)
        for e in range(N_EXPERTS):
            lo = zi_ref[e] + zi_ref[N_EXPERTS + e]
            hi = zi_ref[e] + zi_ref[2 * N_EXPERTS + e]

            def zissue(r, carry):
                _row_copy(zblk, 0, xs_ref, r, zsem).start()
                return carry

            def zdrain(r, carry):
                _row_copy(zblk, 0, xs_ref, r, zsem).wait()
                return carry

            lax.fori_loop(lo, hi, zissue, 0)
            lax.fori_loop(lo, hi, zdrain, 0)

        def bcopy(b):
            return pltpu.make_async_copy(zblk, xs_ref.at[pl.ds(b * tm_expert, tm_expert), :], zsem)

        def bissue(b, carry):
            bcopy(b).start()
            return carry

        def bdrain(b, carry):
            bcopy(b).wait()
            return carry

        lax.fori_loop(zi_ref[3 * N_EXPERTS], n_blocks, bissue, 0)
        lax.fori_loop(zi_ref[3 * N_EXPERTS], n_blocks, bdrain, 0)


def _dispatch(hs_list, dest_blk, zinfo, n_blocks, tm_expert):
    d = hs_list[0].shape[1]
    nb, _, tm = dest_blk.shape
    blk_ranges, b0 = [], 0
    for h in hs_list:
        assert h.shape[0] % tm == 0 and h.shape[0] >= tm
        blk_ranges.append((b0, b0 + h.shape[0] // tm))
        b0 = blk_ranges[-1][1]
    assert b0 == nb
    any_spec = pl.BlockSpec(memory_space=pl.ANY)
    return pl.pallas_call(
        functools.partial(_dispatch_kernel, tm, tm_expert, n_blocks, tuple(blk_ranges)),
        grid_spec=pltpu.PrefetchScalarGridSpec(
            num_scalar_prefetch=1,
            grid=(nb,),
            in_specs=[pl.BlockSpec((1, TOP_K, tm), lambda i, zi: (i, 0, 0), memory_space=pltpu.SMEM)]
            + [any_spec] * len(hs_list),
            out_specs=any_spec,
            scratch_shapes=[pltpu.VMEM((N_STAGE, tm, d), F32), pltpu.VMEM((tm_expert, d), F32),
                            pltpu.SemaphoreType.DMA((N_STAGE,)), pltpu.SemaphoreType.DMA((N_STAGE,)),
                            pltpu.SemaphoreType.DMA(())],
        ),
        out_shape=jax.ShapeDtypeStruct((n_blocks * tm_expert, d), F32),
        compiler_params=_cparams(("arbitrary",)),
        name="moe_dispatch",
    )(zinfo, dest_blk, *hs_list)


def _ffn_moe_kernel(be_ref, nu_ref, x_ref, wg_ref, wu_ref, wd_ref, y_ref, acc_ref):
    b = pl.program_id(0)
    f = pl.program_id(1)

    @pl.when(b < nu_ref[0])
    def _():
        @pl.when(f == 0)
        def _():
            acc_ref[...] = jnp.zeros_like(acc_ref)

        xb = x_ref[...].astype(BF16)
        mid = _silu(_dot(xb, wg_ref[0])) * _dot(xb, wu_ref[0])
        acc_ref[...] += _dot(mid.astype(BF16), wd_ref[0])

        @pl.when(f == pl.num_programs(1) - 1)
        def _():
            y_ref[...] = acc_ref[...]

    @pl.when((b >= nu_ref[0]) & (f == pl.num_programs(1) - 1))
    def _():
        y_ref[...] = jnp.zeros_like(y_ref)


def _ffn_moe(xs, wg, wu, wd, block_e, n_used, tm):
    cap, d = xs.shape
    dff = wg.shape[2]
    nf = dff // TF
    assert cap % tm == 0 and dff % TF == 0

    def blk(b, nu):
        return jnp.minimum(b, nu[0] - 1)

    def fcol(b, f, nu):
        return jnp.where(b < nu[0], f, nf - 1)

    return pl.pallas_call(
        _ffn_moe_kernel,
        grid_spec=pltpu.PrefetchScalarGridSpec(
            num_scalar_prefetch=2,
            grid=(cap // tm, nf),
            in_specs=[pl.BlockSpec((tm, d), lambda b, f, be, nu: (blk(b, nu), 0)),
                      pl.BlockSpec((1, d, TF), lambda b, f, be, nu: (be[b], 0, fcol(b, f, nu))),
                      pl.BlockSpec((1, d, TF), lambda b, f, be, nu: (be[b], 0, fcol(b, f, nu))),
                      pl.BlockSpec((1, TF, d), lambda b, f, be, nu: (be[b], fcol(b, f, nu), 0))],
            out_specs=pl.BlockSpec((tm, d), lambda b, f, be, nu: (b, 0)),
            scratch_shapes=[pltpu.VMEM((tm, d), F32)],
        ),
        out_shape=jax.ShapeDtypeStruct((cap, d), F32),
        compiler_params=_cparams(("arbitrary", "arbitrary")),
        name="ffn_moe",
    )(block_e, n_used, xs, wg, wu, wd)


def _combine_kernel(alpha, tm, dcur_ref, dnext_ref, h_ref, gate_ref, g_ref, b_ref, ys_ref, o_ref, ybuf, sems):
    i = pl.program_id(0)
    n = pl.num_programs(0)

    def issue(dref, slot):
        def body(t, carry):
            for k in range(TOP_K):
                _row_copy(ys_ref, dref[0, k, t], ybuf.at[slot, k], t, sems.at[slot]).start(priority=k)
            return carry

        lax.fori_loop(0, tm, body, 0, unroll=8)

    @pl.when(i == 0)
    def _():
        issue(dcur_ref, 0)

    @pl.when(i + 1 < n)
    def _():
        issue(dnext_ref, (i + 1) % 2)

    slot = i % 2
    for k in range(TOP_K):
        _rows_wait(ys_ref, ybuf.at[slot, k], tm, sems.at[slot])

    gates = gate_ref[...]
    ffn = gates[:, 0:1] * ybuf[slot, 0] + gates[:, 1:2] * ybuf[slot, 1]
    o_ref[...] = _ln(alpha * h_ref[...] + ffn, g_ref[...], b_ref[...])


def _combine(h, dest_blk, gates, ys, g, b, alpha):
    t, d = h.shape
    nb, _, tm = dest_blk.shape
    dest_spec = lambda imap: pl.BlockSpec((1, TOP_K, tm), imap, memory_space=pltpu.SMEM)
    return pl.pallas_call(
        functools.partial(_combine_kernel, alpha, tm),
        grid=(nb,),
        in_specs=[dest_spec(lambda i: (i, 0, 0)),
                  dest_spec(lambda i: (jnp.minimum(i + 1, nb - 1), 0, 0)),
                  pl.BlockSpec((tm, d), lambda i: (i, 0)),
                  pl.BlockSpec((tm, SUBLANES), lambda i: (i, 0)),
                  pl.BlockSpec((1, d), lambda i: (0, 0)),
                  pl.BlockSpec((1, d), lambda i: (0, 0)),
                  pl.BlockSpec(memory_space=pl.ANY)],
        out_specs=pl.BlockSpec((tm, d), lambda i: (i, 0)),
        out_shape=jax.ShapeDtypeStruct((t, d), F32),
        scratch_shapes=[pltpu.VMEM((2, TOP_K, tm, d), F32), pltpu.SemaphoreType.DMA((2,))],
        compiler_params=_cparams(("arbitrary",)),
        name="moe_combine",
    )(dest_blk, dest_blk, h, gates, g, b, ys)


def _ffn_routed(hs_list, mw, g, b, alpha, tm_expert):
    d = hs_list[0].shape[1]
    counts_in = jnp.zeros((SUBLANES, LANES), I32)
    routed = []
    for h in hs_list:
        meta_i, gates, counts_in = _router(h, mw['w_router_hi'], mw['w_router_lo'], counts_in)
        routed.append((meta_i, gates))
    counts = counts_in[0, :N_EXPERTS]
    padded = (counts + tm_expert - 1) // tm_expert * tm_expert
    pad_end = jnp.cumsum(padded)
    pad_start = (pad_end - padded).astype(I32)
    n_assign = sum(h.shape[0] for h in hs_list) * TOP_K
    n_blocks = -(-n_assign // tm_expert) + N_EXPERTS
    cap = n_blocks * tm_expert
    n_used = (pad_end[-1] // tm_expert).astype(I32)
    blk_start = jnp.minimum(jnp.arange(n_blocks, dtype=I32), n_used - 1) * tm_expert
    block_e = jnp.minimum(jnp.searchsorted(pad_end, blk_start, side='right'), N_EXPERTS - 1).astype(I32)
    zinfo = jnp.concatenate([pad_start, counts, padded, n_used.reshape(1)]).astype(I32)

    dests = []
    for h, (meta_i, _) in zip(hs_list, routed):
        dest = pad_start[meta_i[:, 0:TOP_K]] + meta_i[:, TOP_K:2 * TOP_K]
        dests.append(dest.reshape(h.shape[0] // TM_MOVE, TM_MOVE, TOP_K).transpose(0, 2, 1))
    xs = _dispatch(hs_list, jnp.concatenate(dests, axis=0), zinfo, n_blocks, tm_expert)
    ys = _ffn_moe(xs, mw['w_e_gate'], mw['w_e_up'], mw['w_e_down'], block_e, n_used.reshape(1), tm_expert)
    return [_combine(h, dest_blk, gates, ys, g, b, alpha)
            for h, dest_blk, (_, gates) in zip(hs_list, dests, routed)]


def _prep_layer(i, p, n_step):
    dq = p['conv_a_w'].shape[2]
    hd = dq // N_HEADS_B
    tril = jnp.tril(jnp.ones((CHUNK, CHUNK), dtype=bool))
    w_s = p['w_s'][i]
    b_s = p['b_s'][i]
    wp = p['w_pool'][i]
    ng, gc = wp.shape[0], wp.shape[1]
    w_pool_bd = jnp.zeros((dq, dq), F32)
    for g in range(ng):
        w_pool_bd = w_pool_bd.at[g * gc:(g + 1) * gc, g * gc:(g + 1) * gc].set(wp[g])
    row = lambda a: a.reshape(1, -1)
    return dict(
        ln_in_g=row(p['ln_in_g']), ln_in_b=row(p['ln_in_b']),
        w_in=p['w_in'][i].astype(BF16),
        conv_a_w=p['conv_a_w'][i], conv_a_b=row(p['conv_a_b'][i]),
        ln_a_g=row(p['ln_a_g'][i]), ln_a_b=row(p['ln_a_b'][i]),
        ln_v_g=row(p['ln_v_g'][i]), ln_v_b=row(p['ln_v_b'][i]),
        ws_tril=jnp.where(tril[None], w_s, 0).astype(BF16),
        bs_mat=jnp.repeat(b_s.T, hd, axis=1),
        ws_vec=jnp.repeat(jnp.where(tril[None], w_s, 0)[:, :n_step, :n_step].transpose(1, 2, 0)
                          .reshape(n_step * n_step, N_HEADS_B), hd, axis=1),
        bs_vec=jnp.repeat(b_s[:, :n_step].T, hd, axis=1),
        conv_c_w=p['conv_c_w'][i],
        w_pool_bd=w_pool_bd.astype(BF16), pool_scale=row(p['pool_scale'][i]),
        w_out=p['w_out'][i].astype(BF16),
        ln_mix_g=row(p['ln_mix_g'][i]), ln_mix_b=row(p['ln_mix_b'][i]),
    )


def kernel(x_prompt, x_sample, state_conv_a, state_conv_c, state_pool_d, ln_in_g, ln_in_b, w_in, conv_a_w,
           conv_a_b, ln_a_g, ln_a_b, ln_v_g, ln_v_b, w_s, b_s, conv_c_w, w_pool, pool_scale, w_out,
           ln_mix_g, ln_mix_b, w_ff_gate, w_ff_up, w_ff_down, w_router, w_e_gate, w_e_up, w_e_down,
           ln_ffn_g, ln_ffn_b):
    p = dict(ln_in_g=ln_in_g, ln_in_b=ln_in_b, w_in=w_in, conv_a_w=conv_a_w, conv_a_b=conv_a_b,
             ln_a_g=ln_a_g, ln_a_b=ln_a_b, ln_v_g=ln_v_g, ln_v_b=ln_v_b, w_s=w_s, b_s=b_s,
             conv_c_w=conv_c_w, w_pool=w_pool, pool_scale=pool_scale, w_out=w_out,
             ln_mix_g=ln_mix_g, ln_mix_b=ln_mix_b)
    depth = w_in.shape[0]
    bsz, seq, d = x_prompt.shape
    n_seq, n_step, _ = x_sample.shape
    dq = conv_a_w.shape[2]
    alpha = float((2.0 * depth) ** 0.25)

    hp = x_prompt
    hs = x_sample.transpose(1, 0, 2).reshape(n_step * n_seq, d)
    outs = {k: [] for k in ('pa', 'pc', 'pd', 'pv', 'sa', 'sc', 'sd', 'sv')}
    for i in range(depth):
        lw = _prep_layer(i, p, n_step)
        hp, pa, pc, pd, pv = _mixer_prompt(hp, lw, i == 0, alpha)
        tm_state = lambda s: s.transpose(1, 0, 2).reshape(-1, dq)
        hs, sa, sc, sd, sv = _mixer_sample(hs, tm_state(state_conv_a[i]), tm_state(state_conv_c[i]),
                                           tm_state(state_pool_d[i]), lw, i == 0, alpha, n_seq, n_step)
        lg, lb = ln_ffn_g[i].reshape(1, -1), ln_ffn_b[i].reshape(1, -1)
        j = i // 2
        if i % 2 == 0:
            wg, wu, wd = w_ff_gate[j].astype(BF16), w_ff_up[j].astype(BF16), w_ff_down[j].astype(BF16)
            hp = _ffn_dense(hp.reshape(bsz * seq, d), wg, wu, wd, lg, lb, alpha, TM_DENSE).reshape(bsz, seq, d)
            hs = _ffn_dense(hs, wg, wu, wd, lg, lb, alpha, hs.shape[0])
        else:
            wr = jnp.pad(w_router[j], ((0, 0), (0, LANES - N_EXPERTS)))
            wr_hi = wr.astype(BF16)
            mw = dict(w_router_hi=wr_hi, w_router_lo=(wr - wr_hi.astype(F32)).astype(BF16),
                      w_e_gate=w_e_gate[j].astype(BF16), w_e_up=w_e_up[j].astype(BF16),
                      w_e_down=w_e_down[j].astype(BF16))
            hp, hs = _ffn_routed([hp.reshape(bsz * seq, d), hs], mw, lg, lb, alpha, TM_EXPERT)
            hp = hp.reshape(bsz, seq, d)
        back = lambda a: a.reshape(-1, n_seq, dq).transpose(1, 0, 2)
        for k, val in zip(('pa', 'pc', 'pd', 'pv', 'sa', 'sc', 'sd', 'sv'),
                          (pa, pc, pd, pv, back(sa), back(sc), back(sd), back(sv))):
            outs[k].append(val)
    y_sample = hs.reshape(n_step, n_seq, d).transpose(1, 0, 2)
    st = lambda k: jnp.stack(outs[k])
    return (hp, y_sample, st('pa'), st('pc'), st('pd'), st('pv'), st('sa'), st('sc'), st('sd'), st('sv'))
```

```python
import functools

import numpy as np
import jax
import jax.numpy as jnp
from jax import lax
from jax.experimental import pallas as pl
from jax.experimental.pallas import tpu as pltpu

F32 = jnp.float32
BF16 = jnp.bfloat16
I32 = jnp.int32

PAST_LEN = 16384
CHUNK = 128
N_HEADS_B = 4
POOL_WINDOWS = (2, 4, 8, 16)
POOL_PAST = 15
CONV_A_W = 31
CONV_C_W = 3
N_EXPERTS = 8
TOP_K = 2
LN_EPS = 1e-5
SQRT_HALF = float(np.sqrt(0.5).astype(np.float32))

LANES = 128
SUBLANES = 8
VMEM_LIMIT_BYTES = 56 * 1024 * 1024

HIST_A = 32
HIST_C = 8
HIST_D = 24

TL_PROMPT = 512
TM_DENSE = 1024
TF_DENSE = 896
TF_EXPERT = 1792
TM_ROUTE = 512
TM_MOVE = 256
TM_EXPERT = 512
N_STAGE = 3


def _ln(x, g, b):
    mu = jnp.mean(x, axis=-1, keepdims=True)
    xc = x - mu
    var = jnp.mean(xc * xc, axis=-1, keepdims=True)
    return xc * lax.rsqrt(var + LN_EPS) * g + b


def _gelu(x):
    return 0.5 * x * (1.0 + lax.erf(x * SQRT_HALF))


def _silu(x):
    return x * jax.nn.sigmoid(x)


def _dot(a, b):
    return jnp.dot(a, b, preferred_element_type=F32)


def _cparams(sem):
    return pltpu.CompilerParams(dimension_semantics=sem, vmem_limit_bytes=VMEM_LIMIT_BYTES)


def _mixer_prompt_kernel(first_layer, alpha, tl, d_mix,
                         x_ref, lng_ref, lnb_ref, win_ref, caw_ref, cab_ref, lag_ref, lab_ref,
                         lvg_ref, lvb_ref, ws_ref, bsm_ref, ccw_ref, wpool_ref, pscale_ref,
                         wout_ref, lmg_ref, lmb_ref,
                         h_out, sa_out, sc_out, sd_out, v_out,
                         ext_a, ext_c, ext_d, cat_ref):
    dq = d_mix // 4
    j = pl.program_id(1)
    nj = pl.num_programs(1)

    @pl.when(j == 0)
    def _():
        ext_a[0, 0:HIST_A, :] = jnp.zeros((HIST_A, dq), F32)
        ext_c[0:HIST_C, :] = jnp.zeros((HIST_C, dq), F32)
        ext_d[0:HIST_D, :] = jnp.zeros((HIST_D, dq), F32)

    x = x_ref[0]
    h = _ln(x, lng_ref[...], lnb_ref[...]) if first_layer else x
    hb = h.astype(BF16)

    def proj(i):
        return _dot(hb, win_ref[:, i * dq:(i + 1) * dq])

    a_glu = proj(0) * jax.nn.sigmoid(proj(1))
    ext_a[0, HIST_A:HIST_A + tl, :] = a_glu
    n_ext = HIST_A + tl
    full = ext_a[0]
    for s in range(1, SUBLANES):
        ext_a[s, 0:n_ext - SUBLANES, :] = pltpu.roll(full, n_ext - s, axis=0)[0:n_ext - SUBLANES, :]
    rb = 64
    off_a = HIST_A - (CONV_A_W - 1)
    for r0 in range(0, tl, rb):
        acc = jnp.zeros((rb, dq), F32)
        for k in range(CONV_A_W):
            s = (off_a + k) % SUBLANES
            row = off_a + k - s + r0
            acc = acc + ext_a[s, row:row + rb, :] * caw_ref[k:k + 1, :]
        y = _silu(_ln(acc + cab_ref[...], lag_ref[...], lab_ref[...]))
        cat_ref[r0:r0 + rb, 0:dq] = y.astype(BF16)

    u = _gelu(proj(2))
    v = _ln(_gelu(proj(3)), lvg_ref[...], lvb_ref[...])
    hd = dq // N_HEADS_B
    lane_head = lax.broadcasted_iota(I32, (CHUNK, dq), 1) // hd
    for c in range(tl // CHUNK):
        vc = v[c * CHUNK:(c + 1) * CHUNK, :]
        s = bsm_ref[...]
        for hh in range(N_HEADS_B):
            vm = jnp.where(lane_head == hh, vc, 0.0).astype(BF16)
            s = s + _dot(ws_ref[hh], vm)
        cat_ref[c * CHUNK:(c + 1) * CHUNK, dq:2 * dq] = (u[c * CHUNK:(c + 1) * CHUNK, :] * s).astype(BF16)

    c_x = proj(4)
    c_b = proj(5)
    c_c = proj(6)
    gx = c_c * c_x
    ext_c[HIST_C:HIST_C + tl, :] = gx
    off_c = HIST_C - (CONV_C_W - 1)
    conv_c = gx * ccw_ref[CONV_C_W - 1:CONV_C_W, :]
    for k in range(CONV_C_W - 1):
        conv_c = conv_c + ext_c[off_c + k:off_c + k + tl, :] * ccw_ref[k:k + 1, :]
    cat_ref[:, 2 * dq:3 * dq] = (c_b * conv_c).astype(BF16)

    d_in = proj(7)
    ext_d[HIST_D:HIST_D + tl, :] = d_in
    e = ext_d[...]
    s2 = e + pltpu.roll(e, 1, axis=0)
    s4 = s2 + pltpu.roll(s2, 2, axis=0)
    s8 = s4 + pltpu.roll(s4, 4, axis=0)
    s16 = s8 + pltpu.roll(s8, 8, axis=0)
    sums = (s2, s4, s8, s16)
    pos = (j * tl + lax.broadcasted_iota(I32, (tl, 1), 0)).astype(F32)
    gc = dq // len(POOL_WINDOWS)
    lane_grp = lax.broadcasted_iota(I32, (tl, dq), 1) // gc
    mean = jnp.zeros((tl, dq), F32)
    for g, w in enumerate(POOL_WINDOWS):
        inv = 1.0 / jnp.minimum(pos + 1.0, float(w))
        mean = jnp.where(lane_grp == g, sums[g][HIST_D:HIST_D + tl, :] * inv, mean)
    dd = (mean - d_in).astype(BF16)
    cat_ref[:, 3 * dq:4 * dq] = (_dot(dd, wpool_ref[...]) * pscale_ref[...]).astype(BF16)

    mix = _dot(cat_ref[...], wout_ref[...])
    h_out[0] = _ln(alpha * h + mix, lmg_ref[...], lmb_ref[...])

    @pl.when(j == nj - 1)
    def _():
        sa_out[0] = ext_a[0, HIST_A + tl - (CONV_A_W - 1):HIST_A + tl, :]
        sc_out[0] = ext_c[HIST_C + tl - (CONV_C_W - 1):HIST_C + tl, :]
        sd_out[0] = ext_d[HIST_D + tl - POOL_PAST:HIST_D + tl, :]
        v_out[0] = v[tl - CHUNK:tl, :]

    ext_a[0, 0:HIST_A, :] = ext_a[0, tl:tl + HIST_A, :]
    ext_c[0:HIST_C, :] = ext_c[tl:tl + HIST_C, :]
    ext_d[0:HIST_D, :] = ext_d[tl:tl + HIST_D, :]


def _full_spec(arr):
    nd = arr.ndim
    return pl.BlockSpec(arr.shape, lambda *_: (0,) * nd)


def _mixer_prompt(x, lw, first_layer, alpha):
    bsz, seq, d = x.shape
    d_mix = lw['w_out'].shape[0]
    dq = d_mix // 4
    tl = TL_PROMPT
    assert seq % tl == 0 and tl % CHUNK == 0 and seq >= CHUNK
    params = [lw['ln_in_g'], lw['ln_in_b'], lw['w_in'], lw['conv_a_w'], lw['conv_a_b'], lw['ln_a_g'],
              lw['ln_a_b'], lw['ln_v_g'], lw['ln_v_b'], lw['ws_tril'], lw['bs_mat'], lw['conv_c_w'],
              lw['w_pool_bd'], lw['pool_scale'], lw['w_out'], lw['ln_mix_g'], lw['ln_mix_b']]
    out_shape = (
        jax.ShapeDtypeStruct((bsz, seq, d), F32),
        jax.ShapeDtypeStruct((bsz, CONV_A_W - 1, dq), F32),
        jax.ShapeDtypeStruct((bsz, CONV_C_W - 1, dq), F32),
        jax.ShapeDtypeStruct((bsz, POOL_PAST, dq), F32),
        jax.ShapeDtypeStruct((bsz, CHUNK, dq), F32),
    )
    state_spec = lambda rows: pl.BlockSpec((1, rows, dq), lambda b, j: (b, 0, 0))
    return pl.pallas_call(
        functools.partial(_mixer_prompt_kernel, first_layer, alpha, tl, d_mix),
        grid=(bsz, seq // tl),
        in_specs=[pl.BlockSpec((1, tl, d), lambda b, j: (b, j, 0))] + [_full_spec(p) for p in params],
        out_specs=(pl.BlockSpec((1, tl, d), lambda b, j: (b, j, 0)),
                   state_spec(CONV_A_W - 1), state_spec(CONV_C_W - 1), state_spec(POOL_PAST),
                   state_spec(CHUNK)),
        out_shape=out_shape,
        scratch_shapes=[pltpu.VMEM((SUBLANES, HIST_A + tl, dq), F32), pltpu.VMEM((HIST_C + tl, dq), F32),
                        pltpu.VMEM((HIST_D + tl, dq), F32), pltpu.VMEM((tl, d_mix), BF16)],
        compiler_params=_cparams(("arbitrary", "arbitrary")),
        name="mixer_prompt",
    )(x, *params)


def _mixer_sample_kernel(first_layer, alpha, n_seq, n_step, d_mix,
                         x_ref, sa_ref, sc_ref, sd_ref, lng_ref, lnb_ref, win_ref, caw_ref, cab_ref,
                         lag_ref, lab_ref, lvg_ref, lvb_ref, wsv_ref, bsv_ref, ccw_ref, wpool_ref,
                         pscale_ref, wout_ref, lmg_ref, lmb_ref,
                         h_out, sa_out, sc_out, sd_out, v_out, cat_ref):
    dq = d_mix // 4
    x = x_ref[...]
    h = _ln(x, lng_ref[...], lnb_ref[...]) if first_layer else x
    hb = h.astype(BF16)

    def proj(i):
        return _dot(hb, win_ref[:, i * dq:(i + 1) * dq])

    def slab(val, i):
        return val[i * n_seq:(i + 1) * n_seq, :]

    def ext_slabs(state_ref, n_past, cur):
        return ([state_ref[i * n_seq:(i + 1) * n_seq, :] for i in range(n_past)]
                + [slab(cur, i) for i in range(n_step)])

    def store_state(out_ref, slabs, n_keep):
        for i, sl in enumerate(slabs[len(slabs) - n_keep:]):
            out_ref[i * n_seq:(i + 1) * n_seq, :] = sl

    a_glu = proj(0) * jax.nn.sigmoid(proj(1))
    ea = ext_slabs(sa_ref, CONV_A_W - 1, a_glu)
    for l in range(n_step):
        acc = jnp.zeros((n_seq, dq), F32)
        for k in range(CONV_A_W):
            acc = acc + ea[l + k] * caw_ref[k:k + 1, :]
        y = _silu(_ln(acc + cab_ref[...], lag_ref[...], lab_ref[...]))
        cat_ref[l * n_seq:(l + 1) * n_seq, 0:dq] = y.astype(BF16)
    store_state(sa_out, ea, CONV_A_W - 1)

    u = _gelu(proj(2))
    v = _ln(_gelu(proj(3)), lvg_ref[...], lvb_ref[...])
    v_out[...] = v
    for l in range(n_step):
        s = jnp.zeros((n_seq, dq), F32) + bsv_ref[l:l + 1, :]
        for m in range(l + 1):
            s = s + slab(v, m) * wsv_ref[l * n_step + m:l * n_step + m + 1, :]
        cat_ref[l * n_seq:(l + 1) * n_seq, dq:2 * dq] = (slab(u, l) * s).astype(BF16)

    c_x = proj(4)
    c_b = proj(5)
    c_c = proj(6)
    gx = c_c * c_x
    ec = ext_slabs(sc_ref, CONV_C_W - 1, gx)
    for l in range(n_step):
        acc = jnp.zeros((n_seq, dq), F32)
        for k in range(CONV_C_W):
            acc = acc + ec[l + k] * ccw_ref[k:k + 1, :]
        cat_ref[l * n_seq:(l + 1) * n_seq, 2 * dq:3 * dq] = (slab(c_b, l) * acc).astype(BF16)
    store_state(sc_out, ec, CONV_C_W - 1)

    d_in = proj(7)
    ed = ext_slabs(sd_ref, POOL_PAST, d_in)
    memo = {}

    def wsum(i, w):
        if i < 0:
            return None
        if w == 1:
            return ed[i]
        if (i, w) not in memo:
            a, b = wsum(i, w // 2), wsum(i - w // 2, w // 2)
            memo[(i, w)] = a if b is None else a + b
        return memo[(i, w)]

    gc = dq // len(POOL_WINDOWS)
    lane_grp = lax.broadcasted_iota(I32, (n_seq, dq), 1) // gc
    for l in range(n_step):
        mean = jnp.zeros((n_seq, dq), F32)
        for g, w in enumerate(POOL_WINDOWS):
            count = min(PAST_LEN + l + 1, w)
            mean = jnp.where(lane_grp == g, wsum(POOL_PAST + l, w) * (1.0 / count), mean)
        dd = (mean - slab(d_in, l)).astype(BF16)
        cat_ref[l * n_seq:(l + 1) * n_seq, 3 * dq:4 * dq] = (
            _dot(dd, wpool_ref[...]) * pscale_ref[...]).astype(BF16)
    store_state(sd_out, ed, POOL_PAST)

    mix = _dot(cat_ref[...], wout_ref[...])
    h_out[...] = _ln(alpha * h + mix, lmg_ref[...], lmb_ref[...])


def _mixer_sample(x_tm, sa, sc, sd, lw, first_layer, alpha, n_seq, n_step):
    rows, d = x_tm.shape
    d_mix = lw['w_out'].shape[0]
    dq = d_mix // 4
    assert n_seq % SUBLANES == 0 and n_step <= CHUNK and PAST_LEN % CHUNK == 0
    ins = [x_tm, sa, sc, sd, lw['ln_in_g'], lw['ln_in_b'], lw['w_in'], lw['conv_a_w'], lw['conv_a_b'],
           lw['ln_a_g'], lw['ln_a_b'], lw['ln_v_g'], lw['ln_v_b'], lw['ws_vec'], lw['bs_vec'],
           lw['conv_c_w'], lw['w_pool_bd'], lw['pool_scale'], lw['w_out'], lw['ln_mix_g'], lw['ln_mix_b']]
    out_shape = (
        jax.ShapeDtypeStruct((rows, d), F32),
        jax.ShapeDtypeStruct(((CONV_A_W - 1) * n_seq, dq), F32),
        jax.ShapeDtypeStruct(((CONV_C_W - 1) * n_seq, dq), F32),
        jax.ShapeDtypeStruct((POOL_PAST * n_seq, dq), F32),
        jax.ShapeDtypeStruct((rows, dq), F32),
    )
    return pl.pallas_call(
        functools.partial(_mixer_sample_kernel, first_layer, alpha, n_seq, n_step, d_mix),
        grid=(1,),
        in_specs=[_full_spec(a) for a in ins],
        out_specs=tuple(pl.BlockSpec(s.shape, lambda i: (0, 0)) for s in out_shape),
        out_shape=out_shape,
        scratch_shapes=[pltpu.VMEM((rows, d_mix), BF16)],
        compiler_params=_cparams(("arbitrary",)),
        name="mixer_sample",
    )(*ins)


def _ffn_dense_kernel(alpha, x_ref, wg_ref, wu_ref, wd_ref, g_ref, b_ref, o_ref, acc_ref):
    f = pl.program_id(1)

    @pl.when(f == 0)
    def _():
        acc_ref[...] = jnp.zeros_like(acc_ref)

    xb = x_ref[...].astype(BF16)
    mid = _silu(_dot(xb, wg_ref[...])) * _dot(xb, wu_ref[...])
    acc_ref[...] += _dot(mid.astype(BF16), wd_ref[...])

    @pl.when(f == pl.num_programs(1) - 1)
    def _():
        o_ref[...] = _ln(alpha * x_ref[...] + acc_ref[...], g_ref[...], b_ref[...])


def _ffn_dense(x, wg, wu, wd, g, b, alpha, tm, tf):
    t, d = x.shape
    dff = wg.shape[1]
    assert t % tm == 0 and dff % tf == 0
    return pl.pallas_call(
        functools.partial(_ffn_dense_kernel, alpha),
        grid=(t // tm, dff // tf),
        in_specs=[pl.BlockSpec((tm, d), lambda i, f: (i, 0)),
                  pl.BlockSpec((d, tf), lambda i, f: (0, f)),
                  pl.BlockSpec((d, tf), lambda i, f: (0, f)),
                  pl.BlockSpec((tf, d), lambda i, f: (f, 0)),
                  pl.BlockSpec((1, d), lambda i, f: (0, 0)),
                  pl.BlockSpec((1, d), lambda i, f: (0, 0))],
        out_specs=pl.BlockSpec((tm, d), lambda i, f: (i, 0)),
        out_shape=jax.ShapeDtypeStruct((t, d), F32),
        scratch_shapes=[pltpu.VMEM((tm, d), F32)],
        compiler_params=_cparams(("arbitrary", "arbitrary")),
        name="ffn_dense",
    )(x, wg, wu, wd, g, b)


def _router_kernel(tm, h_ref, whi_ref, wlo_ref, cin_ref, mi_ref, mf_ref, cnt_ref, carry_ref, below_ref):
    i = pl.program_id(0)

    @pl.when(i == 0)
    def _():
        carry_ref[...] = cin_ref[...].astype(F32)
        row = lax.broadcasted_iota(I32, (tm, tm), 0)
        col = lax.broadcasted_iota(I32, (tm, tm), 1)
        below_ref[...] = jnp.where(col < row, 1.0, 0.0).astype(BF16)

    lane = lax.broadcasted_iota(I32, (tm, LANES), 1).astype(F32)
    h = h_ref[...]
    h_hi = h.astype(BF16)
    h_lo = (h - h_hi.astype(F32)).astype(BF16)
    logits = (_dot(h_hi, whi_ref[...]) + (_dot(h_lo, whi_ref[...]) + _dot(h_hi, wlo_ref[...]))
              + _dot(h_lo, wlo_ref[...]))
    logits = jnp.where(lane < N_EXPERTS, logits, -jnp.inf)
    m1 = jnp.max(logits, axis=-1, keepdims=True)
    i1 = jnp.min(jnp.where(logits == m1, lane, float(LANES)), axis=-1, keepdims=True)
    sel1 = lane == i1
    rest = jnp.where(sel1, -jnp.inf, logits)
    m2 = jnp.max(rest, axis=-1, keepdims=True)
    i2 = jnp.min(jnp.where(rest == m2, lane, float(LANES)), axis=-1, keepdims=True)
    sel2 = lane == i2
    e2 = jnp.exp(m2 - m1)
    den = 1.0 + e2
    g1 = 1.0 / den
    g2 = e2 / den

    sel = jnp.where(sel1 | sel2, 1.0, 0.0)
    base = _dot(below_ref[...], sel.astype(BF16)) + carry_ref[0:1, :]
    r1 = jnp.sum(jnp.where(sel1, base, 0.0), axis=-1, keepdims=True)
    r2 = jnp.sum(jnp.where(sel2, base, 0.0), axis=-1, keepdims=True)
    carry_ref[...] = carry_ref[...] + jnp.sum(sel, axis=0, keepdims=True)

    mi = jnp.where(lane == 0, i1, jnp.where(lane == 1, i2, jnp.where(lane == 2, r1, r2)))
    mi_ref[...] = mi[:, 0:SUBLANES].astype(I32)
    mf = jnp.where(lane == 0, g1, g2)
    mf_ref[...] = mf[:, 0:SUBLANES]
    cnt_ref[...] = carry_ref[...].astype(I32)


def _router(h, wr_hi, wr_lo, counts_in):
    t, d = h.shape
    tm = min(TM_ROUTE, t)
    assert t % tm == 0
    return pl.pallas_call(
        functools.partial(_router_kernel, tm),
        grid=(t // tm,),
        in_specs=[pl.BlockSpec((tm, d), lambda i: (i, 0)), _full_spec(wr_hi), _full_spec(wr_lo),
                  _full_spec(counts_in)],
        out_specs=(pl.BlockSpec((tm, SUBLANES), lambda i: (i, 0)),
                   pl.BlockSpec((tm, SUBLANES), lambda i: (i, 0)),
                   pl.BlockSpec((SUBLANES, LANES), lambda i: (0, 0))),
        out_shape=(jax.ShapeDtypeStruct((t, SUBLANES), I32),
                   jax.ShapeDtypeStruct((t, SUBLANES), F32),
                   jax.ShapeDtypeStruct((SUBLANES, LANES), I32)),
        scratch_shapes=[pltpu.VMEM((SUBLANES, LANES), F32), pltpu.VMEM((tm, tm), BF16)],
        compiler_params=_cparams(("arbitrary",)),
        name="router",
    )(h, wr_hi, wr_lo, counts_in)


def _row_copy(src, src_row, dst, dst_row, sem):
    return pltpu.make_async_copy(src.at[pl.ds(src_row, 1), :], dst.at[pl.ds(dst_row, 1), :], sem)


def _rows_wait(src, dst, n_rows, sem):
    pltpu.make_async_copy(src.at[pl.ds(0, n_rows), :], dst.at[pl.ds(0, n_rows), :], sem).wait()


def _dispatch_kernel(tm, tm_expert, n_blocks, blk_ranges, zi_ref, dest_ref, *rest):
    n_src = len(blk_ranges)
    h_refs = rest[:n_src]
    xs_ref, xbuf, zblk, lsems, rsems, zsem = rest[n_src:]
    i = pl.program_id(0)
    n = pl.num_programs(0)

    def block_load(g, start):
        slot = lax.rem(g, N_STAGE)
        for h_ref, (b0, b1) in zip(h_refs, blk_ranges):
            @pl.when((g >= b0) & (g < b1))
            def _(h_ref=h_ref, b0=b0):
                cp = pltpu.make_async_copy(h_ref.at[pl.ds((g - b0) * tm, tm), :], xbuf.at[slot],
                                           lsems.at[slot])
                if start:
                    cp.start()
                else:
                    cp.wait()

    def rows_wait(g):
        slot = lax.rem(g, N_STAGE)
        for _k in range(TOP_K):
            _rows_wait(xbuf.at[slot], xs_ref, tm, rsems.at[slot])

    @pl.when(i == 0)
    def _():
        block_load(i, True)

    @pl.when(i >= N_STAGE - 1)
    def _():
        rows_wait(i - (N_STAGE - 1))

    @pl.when(i + 1 < n)
    def _():
        block_load(i + 1, True)

    block_load(i, False)
    cur = lax.rem(i, N_STAGE)

    def issue(j, carry):
        r0 = pl.multiple_of(j * SUBLANES, SUBLANES)
        for u in range(SUBLANES):
            for k in range(TOP_K):
                dst = dest_ref[0, 0, j * (SUBLANES * TOP_K) + (u * TOP_K + k)]
                _row_copy(xbuf.at[cur], r0 + u, xs_ref, dst, rsems.at[cur]).start(priority=k)
        return carry

    lax.fori_loop(0, tm // SUBLANES, issue, 0)

    @pl.when(i == n - 1)
    def _():
        for back in range(N_STAGE - 2, -1, -1):
            @pl.when(i >= back)
            def _(back=back):
                rows_wait(i - back)

    @pl.when(i == 0)
    def _():
        zblk[...] = jnp.zeros_like(zblk)
        for e in range(N_EXPERTS):
            lo = zi_ref[e] + zi_ref[N_EXPERTS + e]
            hi = zi_ref[e] + zi_ref[2 * N_EXPERTS + e]

            def zissue(r, carry):
                _row_copy(zblk, 0, xs_ref, r, zsem).start()
                return carry

            def zdrain(r, carry):
                _row_copy(zblk, 0, xs_ref, r, zsem).wait()
                return carry

            lax.fori_loop(lo, hi, zissue, 0)
            lax.fori_loop(lo, hi, zdrain, 0)

        def bcopy(b):
            return pltpu.make_async_copy(zblk, xs_ref.at[pl.ds(b * tm_expert, tm_expert), :], zsem)

        def bissue(b, carry):
            bcopy(b).start()
            return carry

        def bdrain(b, carry):
            bcopy(b).wait()
            return carry

        lax.fori_loop(zi_ref[3 * N_EXPERTS], n_blocks, bissue, 0)
        lax.fori_loop(zi_ref[3 * N_EXPERTS], n_blocks, bdrain, 0)


def _dispatch(hs_list, dest_blk, zinfo, n_blocks, tm_expert):
    d = hs_list[0].shape[1]
    nb = dest_blk.shape[0]
    tm = dest_blk.shape[2] // TOP_K
    blk_ranges, b0 = [], 0
    for h in hs_list:
        assert h.shape[0] % tm == 0 and h.shape[0] >= tm
        blk_ranges.append((b0, b0 + h.shape[0] // tm))
        b0 = blk_ranges[-1][1]
    assert b0 == nb
    any_spec = pl.BlockSpec(memory_space=pl.ANY)
    return pl.pallas_call(
        functools.partial(_dispatch_kernel, tm, tm_expert, n_blocks, tuple(blk_ranges)),
        grid_spec=pltpu.PrefetchScalarGridSpec(
            num_scalar_prefetch=1,
            grid=(nb,),
            in_specs=[pl.BlockSpec((1, 1, TOP_K * tm), lambda i, zi: (i, 0, 0), memory_space=pltpu.SMEM)]
            + [any_spec] * len(hs_list),
            out_specs=any_spec,
            scratch_shapes=[pltpu.VMEM((N_STAGE, tm, d), F32), pltpu.VMEM((tm_expert, d), F32),
                            pltpu.SemaphoreType.DMA((N_STAGE,)), pltpu.SemaphoreType.DMA((N_STAGE,)),
                            pltpu.SemaphoreType.DMA(())],
        ),
        out_shape=jax.ShapeDtypeStruct((n_blocks * tm_expert, d), F32),
        compiler_params=_cparams(("arbitrary",)),
        name="moe_dispatch",
    )(zinfo, dest_blk, *hs_list)


def _ffn_moe_kernel(be_ref, nu_ref, x_ref, wg_ref, wu_ref, wd_ref, y_ref, acc_ref):
    b = pl.program_id(0)
    f = pl.program_id(1)

    @pl.when(b < nu_ref[0])
    def _():
        @pl.when(f == 0)
        def _():
            acc_ref[...] = jnp.zeros_like(acc_ref)

        xb = x_ref[...].astype(BF16)
        mid = _silu(_dot(xb, wg_ref[0])) * _dot(xb, wu_ref[0])
        acc_ref[...] += _dot(mid.astype(BF16), wd_ref[0])

        @pl.when(f == pl.num_programs(1) - 1)
        def _():
            y_ref[...] = acc_ref[...]

    @pl.when((b >= nu_ref[0]) & (f == pl.num_programs(1) - 1))
    def _():
        y_ref[...] = jnp.zeros_like(y_ref)


def _ffn_moe(xs, wg, wu, wd, block_e, n_used, tm, tf):
    cap, d = xs.shape
    dff = wg.shape[2]
    nf = dff // tf
    assert cap % tm == 0 and dff % tf == 0

    def blk(b, nu):
        return jnp.minimum(b, nu[0] - 1)

    def fcol(b, f, nu):
        return jnp.where(b < nu[0], f, nf - 1)

    return pl.pallas_call(
        _ffn_moe_kernel,
        grid_spec=pltpu.PrefetchScalarGridSpec(
            num_scalar_prefetch=2,
            grid=(cap // tm, nf),
            in_specs=[pl.BlockSpec((tm, d), lambda b, f, be, nu: (blk(b, nu), 0)),
                      pl.BlockSpec((1, d, tf), lambda b, f, be, nu: (be[b], 0, fcol(b, f, nu))),
                      pl.BlockSpec((1, d, tf), lambda b, f, be, nu: (be[b], 0, fcol(b, f, nu))),
                      pl.BlockSpec((1, tf, d), lambda b, f, be, nu: (be[b], fcol(b, f, nu), 0))],
            out_specs=pl.BlockSpec((tm, d), lambda b, f, be, nu: (b, 0)),
            scratch_shapes=[pltpu.VMEM((tm, d), F32)],
        ),
        out_shape=jax.ShapeDtypeStruct((cap, d), F32),
        compiler_params=_cparams(("arbitrary", "arbitrary")),
        name="ffn_moe",
    )(block_e, n_used, xs, wg, wu, wd)


def _combine_kernel(alpha, tm, dcur_ref, dnext_ref, h_ref, gate_ref, g_ref, b_ref, ys_ref, o_ref, ybuf, sems):
    i = pl.program_id(0)
    n = pl.num_programs(0)

    def issue(dref, slot):
        def body(j, carry):
            r0 = pl.multiple_of(j * SUBLANES, SUBLANES)
            for u in range(SUBLANES):
                for k in range(TOP_K):
                    src = dref[0, 0, j * (SUBLANES * TOP_K) + (u * TOP_K + k)]
                    _row_copy(ys_ref, src, ybuf.at[slot, k], r0 + u, sems.at[slot]).start(priority=k)
            return carry

        lax.fori_loop(0, tm // SUBLANES, body, 0)

    @pl.when(i == 0)
    def _():
        issue(dcur_ref, 0)

    @pl.when(i + 1 < n)
    def _():
        issue(dnext_ref, (i + 1) % 2)

    slot = i % 2
    for k in range(TOP_K):
        _rows_wait(ys_ref, ybuf.at[slot, k], tm, sems.at[slot])

    gates = gate_ref[...]
    ffn = gates[:, 0:1] * ybuf[slot, 0] + gates[:, 1:2] * ybuf[slot, 1]
    o_ref[...] = _ln(alpha * h_ref[...] + ffn, g_ref[...], b_ref[...])


def _combine(h, dest_blk, gates, ys, g, b, alpha):
    t, d = h.shape
    nb = dest_blk.shape[0]
    tm = dest_blk.shape[2] // TOP_K
    dest_spec = lambda imap: pl.BlockSpec((1, 1, TOP_K * tm), imap, memory_space=pltpu.SMEM)
    return pl.pallas_call(
        functools.partial(_combine_kernel, alpha, tm),
        grid=(nb,),
        in_specs=[dest_spec(lambda i: (i, 0, 0)),
                  dest_spec(lambda i: (jnp.minimum(i + 1, nb - 1), 0, 0)),
                  pl.BlockSpec((tm, d), lambda i: (i, 0)),
                  pl.BlockSpec((tm, SUBLANES), lambda i: (i, 0)),
                  pl.BlockSpec((1, d), lambda i: (0, 0)),
                  pl.BlockSpec((1, d), lambda i: (0, 0)),
                  pl.BlockSpec(memory_space=pl.ANY)],
        out_specs=pl.BlockSpec((tm, d), lambda i: (i, 0)),
        out_shape=jax.ShapeDtypeStruct((t, d), F32),
        scratch_shapes=[pltpu.VMEM((2, TOP_K, tm, d), F32), pltpu.SemaphoreType.DMA((2,))],
        compiler_params=_cparams(("arbitrary",)),
        name="moe_combine",
    )(dest_blk, dest_blk, h, gates, g, b, ys)


def _ffn_routed(hs_list, mw, g, b, alpha, tm_expert):
    d = hs_list[0].shape[1]
    counts_in = jnp.zeros((SUBLANES, LANES), I32)
    routed = []
    for h in hs_list:
        meta_i, gates, counts_in = _router(h, mw['w_router_hi'], mw['w_router_lo'], counts_in)
        routed.append((meta_i, gates))
    counts = counts_in[0, :N_EXPERTS]
    padded = (counts + tm_expert - 1) // tm_expert * tm_expert
    pad_end = jnp.cumsum(padded)
    pad_start = (pad_end - padded).astype(I32)
    n_assign = sum(h.shape[0] for h in hs_list) * TOP_K
    n_blocks = -(-n_assign // tm_expert) + N_EXPERTS
    cap = n_blocks * tm_expert
    n_used = (pad_end[-1] // tm_expert).astype(I32)
    blk_start = jnp.minimum(jnp.arange(n_blocks, dtype=I32), n_used - 1) * tm_expert
    block_e = jnp.minimum(jnp.sum(blk_start[:, None] >= pad_end[None, :], axis=1), N_EXPERTS - 1).astype(I32)
    zinfo = jnp.concatenate([pad_start, counts, padded, n_used.reshape(1)]).astype(I32)

    dests = []
    for h, (meta_i, _) in zip(hs_list, routed):
        dest = pad_start[meta_i[:, 0:TOP_K]] + meta_i[:, TOP_K:2 * TOP_K]
        dests.append(dest.reshape(h.shape[0] // TM_MOVE, 1, TM_MOVE * TOP_K))
    xs = _dispatch(hs_list, jnp.concatenate(dests, axis=0), zinfo, n_blocks, tm_expert)
    ys = _ffn_moe(xs, mw['w_e_gate'], mw['w_e_up'], mw['w_e_down'], block_e, n_used.reshape(1), tm_expert,
                  TF_EXPERT)
    return [_combine(h, dest_blk, gates, ys, g, b, alpha)
            for h, dest_blk, (_, gates) in zip(hs_list, dests, routed)]


def _prep_layer(i, p, n_step):
    dq = p['conv_a_w'].shape[2]
    hd = dq // N_HEADS_B
    tril = jnp.tril(jnp.ones((CHUNK, CHUNK), dtype=bool))
    w_s = p['w_s'][i]
    b_s = p['b_s'][i]
    wp = p['w_pool'][i]
    ng, gc = wp.shape[0], wp.shape[1]
    w_pool_bd = (jnp.eye(ng, dtype=F32)[:, None, :, None] * wp[:, :, None, :]).reshape(ng * gc, ng * gc)
    row = lambda a: a.reshape(1, -1)
    return dict(
        ln_in_g=row(p['ln_in_g']), ln_in_b=row(p['ln_in_b']),
        w_in=p['w_in'][i].astype(BF16),
        conv_a_w=p['conv_a_w'][i], conv_a_b=row(p['conv_a_b'][i]),
        ln_a_g=row(p['ln_a_g'][i]), ln_a_b=row(p['ln_a_b'][i]),
        ln_v_g=row(p['ln_v_g'][i]), ln_v_b=row(p['ln_v_b'][i]),
        ws_tril=jnp.where(tril[None], w_s, 0).astype(BF16),
        bs_mat=jnp.repeat(b_s.T, hd, axis=1),
        ws_vec=jnp.repeat(jnp.where(tril[None], w_s, 0)[:, :n_step, :n_step].transpose(1, 2, 0)
                          .reshape(n_step * n_step, N_HEADS_B), hd, axis=1),
        bs_vec=jnp.repeat(b_s[:, :n_step].T, hd, axis=1),
        conv_c_w=p['conv_c_w'][i],
        w_pool_bd=w_pool_bd.astype(BF16), pool_scale=row(p['pool_scale'][i]),
        w_out=p['w_out'][i].astype(BF16),
        ln_mix_g=row(p['ln_mix_g'][i]), ln_mix_b=row(p['ln_mix_b'][i]),
    )


def kernel(x_prompt, x_sample, state_conv_a, state_conv_c, state_pool_d, ln_in_g, ln_in_b, w_in, conv_a_w,
           conv_a_b, ln_a_g, ln_a_b, ln_v_g, ln_v_b, w_s, b_s, conv_c_w, w_pool, pool_scale, w_out,
           ln_mix_g, ln_mix_b, w_ff_gate, w_ff_up, w_ff_down, w_router, w_e_gate, w_e_up, w_e_down,
           ln_ffn_g, ln_ffn_b):
    p = dict(ln_in_g=ln_in_g, ln_in_b=ln_in_b, w_in=w_in, conv_a_w=conv_a_w, conv_a_b=conv_a_b,
             ln_a_g=ln_a_g, ln_a_b=ln_a_b, ln_v_g=ln_v_g, ln_v_b=ln_v_b, w_s=w_s, b_s=b_s,
             conv_c_w=conv_c_w, w_pool=w_pool, pool_scale=pool_scale, w_out=w_out,
             ln_mix_g=ln_mix_g, ln_mix_b=ln_mix_b)
    depth = w_in.shape[0]
    bsz, seq, d = x_prompt.shape
    n_seq, n_step, _ = x_sample.shape
    dq = conv_a_w.shape[2]
    alpha = float((2.0 * depth) ** 0.25)

    hp = x_prompt
    hs = x_sample.transpose(1, 0, 2).reshape(n_step * n_seq, d)
    outs = {k: [] for k in ('pa', 'pc', 'pd', 'pv', 'sa', 'sc', 'sd', 'sv')}
    for i in range(depth):
        lw = _prep_layer(i, p, n_step)
        hp, pa, pc, pd, pv = _mixer_prompt(hp, lw, i == 0, alpha)
        tm_state = lambda s: s.transpose(1, 0, 2).reshape(-1, dq)
        hs, sa, sc, sd, sv = _mixer_sample(hs, tm_state(state_conv_a[i]), tm_state(state_conv_c[i]),
                                           tm_state(state_pool_d[i]), lw, i == 0, alpha, n_seq, n_step)
        lg, lb = ln_ffn_g[i].reshape(1, -1), ln_ffn_b[i].reshape(1, -1)
        j = i // 2
        if i % 2 == 0:
            wg, wu, wd = w_ff_gate[j].astype(BF16), w_ff_up[j].astype(BF16), w_ff_down[j].astype(BF16)
            hp = _ffn_dense(hp.reshape(bsz * seq, d), wg, wu, wd, lg, lb, alpha, TM_DENSE,
                            TF_DENSE).reshape(bsz, seq, d)
            hs = _ffn_dense(hs, wg, wu, wd, lg, lb, alpha, hs.shape[0], TF_DENSE)
        else:
            wr = jnp.pad(w_router[j], ((0, 0), (0, LANES - N_EXPERTS)))
            wr_hi = wr.astype(BF16)
            mw = dict(w_router_hi=wr_hi, w_router_lo=(wr - wr_hi.astype(F32)).astype(BF16),
                      w_e_gate=w_e_gate[j].astype(BF16), w_e_up=w_e_up[j].astype(BF16),
                      w_e_down=w_e_down[j].astype(BF16))
            hp, hs = _ffn_routed([hp.reshape(bsz * seq, d), hs], mw, lg, lb, alpha, TM_EXPERT)
            hp = hp.reshape(bsz, seq, d)
        back = lambda a: a.reshape(-1, n_seq, dq).transpose(1, 0, 2)
        for k, val in zip(('pa', 'pc', 'pd', 'pv', 'sa', 'sc', 'sd', 'sv'),
                          (pa, pc, pd, pv, back(sa), back(sc), back(sd), back(sv))):
            outs[k].append(val)
    y_sample = hs.reshape(n_step, n_seq, d).transpose(1, 0, 2)
    st = lambda k: jnp.stack(outs[k])
    return (hp, y_sample, st('pa'), st('pc'), st('pd'), st('pv'), st('sa'), st('sc'), st('sd'), st('sv'))
```

```python
import functools

import numpy as np
import jax
import jax.numpy as jnp
from jax import lax
from jax.experimental import pallas as pl
from jax.experimental.pallas import tpu as pltpu

F32 = jnp.float32
BF16 = jnp.bfloat16
I32 = jnp.int32

PAST_LEN = 16384
CHUNK = 128
N_HEADS_B = 4
POOL_WINDOWS = (2, 4, 8, 16)
POOL_PAST = 15
CONV_A_W = 31
CONV_C_W = 3
N_EXPERTS = 8
TOP_K = 2
LN_EPS = 1e-5
SQRT_HALF = float(np.sqrt(0.5).astype(np.float32))

LANES = 128
SUBLANES = 8
VMEM_LIMIT_BYTES = 56 * 1024 * 1024

HIST_A = 32
HIST_C = 8
HIST_D = 24

TL_PROMPT = 512
TM_DENSE = 512
TF_DENSE = 1792
TF_EXPERT = 1792
TM_ROUTE = 512
TM_MOVE = 256
TM_EXPERT = 512
N_STAGE = 3


def _ln(x, g, b):
    mu = jnp.mean(x, axis=-1, keepdims=True)
    xc = x - mu
    var = jnp.mean(xc * xc, axis=-1, keepdims=True)
    return xc * lax.rsqrt(var + LN_EPS) * g + b


def _gelu(x):
    return 0.5 * x * (1.0 + lax.erf(x * SQRT_HALF))


def _silu(x):
    return x * jax.nn.sigmoid(x)


def _dot(a, b):
    return jnp.dot(a, b, preferred_element_type=F32)


def _cparams(sem):
    return pltpu.CompilerParams(dimension_semantics=sem, vmem_limit_bytes=VMEM_LIMIT_BYTES)


def _mixer_prompt_kernel(first_layer, alpha, tl, d_mix,
                         x_ref, lng_ref, lnb_ref, win_ref, caw_ref, cab_ref, lag_ref, lab_ref,
                         lvg_ref, lvb_ref, ws_ref, bsm_ref, ccw_ref, wpool_ref, pscale_ref,
                         wout_ref, lmg_ref, lmb_ref,
                         h_out, sa_out, sc_out, sd_out, v_out,
                         ext_a, ext_c, ext_d, cat_ref):
    dq = d_mix // 4
    j = pl.program_id(1)
    nj = pl.num_programs(1)

    @pl.when(j == 0)
    def _():
        ext_a[0, 0:HIST_A, :] = jnp.zeros((HIST_A, dq), F32)
        ext_c[0:HIST_C, :] = jnp.zeros((HIST_C, dq), F32)
        ext_d[0:HIST_D, :] = jnp.zeros((HIST_D, dq), F32)

    x = x_ref[0]
    h = _ln(x, lng_ref[...], lnb_ref[...]) if first_layer else x
    hb = h.astype(BF16)

    def proj(i):
        return _dot(hb, win_ref[:, i * dq:(i + 1) * dq])

    a_glu = proj(0) * jax.nn.sigmoid(proj(1))
    ext_a[0, HIST_A:HIST_A + tl, :] = a_glu
    n_ext = HIST_A + tl
    full = ext_a[0]
    for s in range(1, SUBLANES):
        ext_a[s, 0:n_ext - SUBLANES, :] = pltpu.roll(full, n_ext - s, axis=0)[0:n_ext - SUBLANES, :]
    rb = 64
    off_a = HIST_A - (CONV_A_W - 1)
    for r0 in range(0, tl, rb):
        acc = jnp.zeros((rb, dq), F32)
        for k in range(CONV_A_W):
            s = (off_a + k) % SUBLANES
            row = off_a + k - s + r0
            acc = acc + ext_a[s, row:row + rb, :] * caw_ref[k:k + 1, :]
        y = _silu(_ln(acc + cab_ref[...], lag_ref[...], lab_ref[...]))
        cat_ref[r0:r0 + rb, 0:dq] = y.astype(BF16)

    u = _gelu(proj(2))
    v = _ln(_gelu(proj(3)), lvg_ref[...], lvb_ref[...])
    hd = dq // N_HEADS_B
    lane_head = lax.broadcasted_iota(I32, (CHUNK, dq), 1) // hd
    for c in range(tl // CHUNK):
        vc = v[c * CHUNK:(c + 1) * CHUNK, :]
        s = bsm_ref[...]
        for hh in range(N_HEADS_B):
            vm = jnp.where(lane_head == hh, vc, 0.0).astype(BF16)
            s = s + _dot(ws_ref[hh], vm)
        cat_ref[c * CHUNK:(c + 1) * CHUNK, dq:2 * dq] = (u[c * CHUNK:(c + 1) * CHUNK, :] * s).astype(BF16)

    c_x = proj(4)
    c_b = proj(5)
    c_c = proj(6)
    gx = c_c * c_x
    ext_c[HIST_C:HIST_C + tl, :] = gx
    off_c = HIST_C - (CONV_C_W - 1)
    conv_c = gx * ccw_ref[CONV_C_W - 1:CONV_C_W, :]
    for k in range(CONV_C_W - 1):
        conv_c = conv_c + ext_c[off_c + k:off_c + k + tl, :] * ccw_ref[k:k + 1, :]
    cat_ref[:, 2 * dq:3 * dq] = (c_b * conv_c).astype(BF16)

    d_in = proj(7)
    ext_d[HIST_D:HIST_D + tl, :] = d_in
    e = ext_d[...]
    s2 = e + pltpu.roll(e, 1, axis=0)
    s4 = s2 + pltpu.roll(s2, 2, axis=0)
    s8 = s4 + pltpu.roll(s4, 4, axis=0)
    s16 = s8 + pltpu.roll(s8, 8, axis=0)
    sums = (s2, s4, s8, s16)
    pos = (j * tl + lax.broadcasted_iota(I32, (tl, 1), 0)).astype(F32)
    gc = dq // len(POOL_WINDOWS)
    lane_grp = lax.broadcasted_iota(I32, (tl, dq), 1) // gc
    mean = jnp.zeros((tl, dq), F32)
    for g, w in enumerate(POOL_WINDOWS):
        inv = 1.0 / jnp.minimum(pos + 1.0, float(w))
        mean = jnp.where(lane_grp == g, sums[g][HIST_D:HIST_D + tl, :] * inv, mean)
    dd = (mean - d_in).astype(BF16)
    cat_ref[:, 3 * dq:4 * dq] = (_dot(dd, wpool_ref[...]) * pscale_ref[...]).astype(BF16)

    mix = _dot(cat_ref[...], wout_ref[...])
    h_out[0] = _ln(alpha * h + mix, lmg_ref[...], lmb_ref[...])

    @pl.when(j == nj - 1)
    def _():
        sa_out[0] = ext_a[0, HIST_A + tl - (CONV_A_W - 1):HIST_A + tl, :]
        sc_out[0] = ext_c[HIST_C + tl - (CONV_C_W - 1):HIST_C + tl, :]
        sd_out[0] = ext_d[HIST_D + tl - POOL_PAST:HIST_D + tl, :]
        v_out[0] = v[tl - CHUNK:tl, :]

    ext_a[0, 0:HIST_A, :] = ext_a[0, tl:tl + HIST_A, :]
    ext_c[0:HIST_C, :] = ext_c[tl:tl + HIST_C, :]
    ext_d[0:HIST_D, :] = ext_d[tl:tl + HIST_D, :]


def _full_spec(arr):
    nd = arr.ndim
    return pl.BlockSpec(arr.shape, lambda *_: (0,) * nd)


def _mixer_prompt(x, lw, first_layer, alpha):
    bsz, seq, d = x.shape
    d_mix = lw['w_out'].shape[0]
    dq = d_mix // 4
    tl = TL_PROMPT
    assert seq % tl == 0 and tl % CHUNK == 0 and seq >= CHUNK
    params = [lw['ln_in_g'], lw['ln_in_b'], lw['w_in'], lw['conv_a_w'], lw['conv_a_b'], lw['ln_a_g'],
              lw['ln_a_b'], lw['ln_v_g'], lw['ln_v_b'], lw['ws_tril'], lw['bs_mat'], lw['conv_c_w'],
              lw['w_pool_bd'], lw['pool_scale'], lw['w_out'], lw['ln_mix_g'], lw['ln_mix_b']]
    out_shape = (
        jax.ShapeDtypeStruct((bsz, seq, d), F32),
        jax.ShapeDtypeStruct((bsz, CONV_A_W - 1, dq), F32),
        jax.ShapeDtypeStruct((bsz, CONV_C_W - 1, dq), F32),
        jax.ShapeDtypeStruct((bsz, POOL_PAST, dq), F32),
        jax.ShapeDtypeStruct((bsz, CHUNK, dq), F32),
    )
    state_spec = lambda rows: pl.BlockSpec((1, rows, dq), lambda b, j: (b, 0, 0))
    return pl.pallas_call(
        functools.partial(_mixer_prompt_kernel, first_layer, alpha, tl, d_mix),
        grid=(bsz, seq // tl),
        in_specs=[pl.BlockSpec((1, tl, d), lambda b, j: (b, j, 0))] + [_full_spec(p) for p in params],
        out_specs=(pl.BlockSpec((1, tl, d), lambda b, j: (b, j, 0)),
                   state_spec(CONV_A_W - 1), state_spec(CONV_C_W - 1), state_spec(POOL_PAST),
                   state_spec(CHUNK)),
        out_shape=out_shape,
        scratch_shapes=[pltpu.VMEM((SUBLANES, HIST_A + tl, dq), F32), pltpu.VMEM((HIST_C + tl, dq), F32),
                        pltpu.VMEM((HIST_D + tl, dq), F32), pltpu.VMEM((tl, d_mix), BF16)],
        compiler_params=_cparams(("arbitrary", "arbitrary")),
        name="mixer_prompt",
    )(x, *params)


def _mixer_sample_kernel(first_layer, alpha, n_seq, n_step, d_mix,
                         x_ref, sa_ref, sc_ref, sd_ref, lng_ref, lnb_ref, win_ref, caw_ref, cab_ref,
                         lag_ref, lab_ref, lvg_ref, lvb_ref, wsv_ref, bsv_ref, ccw_ref, wpool_ref,
                         pscale_ref, wout_ref, lmg_ref, lmb_ref,
                         h_out, sa_out, sc_out, sd_out, v_out, cat_ref):
    dq = d_mix // 4
    x = x_ref[...]
    h = _ln(x, lng_ref[...], lnb_ref[...]) if first_layer else x
    hb = h.astype(BF16)

    def proj(i):
        return _dot(hb, win_ref[:, i * dq:(i + 1) * dq])

    def slab(val, i):
        return val[i * n_seq:(i + 1) * n_seq, :]

    def ext_slabs(state_ref, n_past, cur):
        return ([state_ref[i * n_seq:(i + 1) * n_seq, :] for i in range(n_past)]
                + [slab(cur, i) for i in range(n_step)])

    def store_state(out_ref, slabs, n_keep):
        for i, sl in enumerate(slabs[len(slabs) - n_keep:]):
            out_ref[i * n_seq:(i + 1) * n_seq, :] = sl

    a_glu = proj(0) * jax.nn.sigmoid(proj(1))
    ea = ext_slabs(sa_ref, CONV_A_W - 1, a_glu)
    for l in range(n_step):
        acc = jnp.zeros((n_seq, dq), F32)
        for k in range(CONV_A_W):
            acc = acc + ea[l + k] * caw_ref[k:k + 1, :]
        y = _silu(_ln(acc + cab_ref[...], lag_ref[...], lab_ref[...]))
        cat_ref[l * n_seq:(l + 1) * n_seq, 0:dq] = y.astype(BF16)
    store_state(sa_out, ea, CONV_A_W - 1)

    u = _gelu(proj(2))
    v = _ln(_gelu(proj(3)), lvg_ref[...], lvb_ref[...])
    v_out[...] = v
    for l in range(n_step):
        s = jnp.zeros((n_seq, dq), F32) + bsv_ref[l:l + 1, :]
        for m in range(l + 1):
            s = s + slab(v, m) * wsv_ref[l * n_step + m:l * n_step + m + 1, :]
        cat_ref[l * n_seq:(l + 1) * n_seq, dq:2 * dq] = (slab(u, l) * s).astype(BF16)

    c_x = proj(4)
    c_b = proj(5)
    c_c = proj(6)
    gx = c_c * c_x
    ec = ext_slabs(sc_ref, CONV_C_W - 1, gx)
    for l in range(n_step):
        acc = jnp.zeros((n_seq, dq), F32)
        for k in range(CONV_C_W):
            acc = acc + ec[l + k] * ccw_ref[k:k + 1, :]
        cat_ref[l * n_seq:(l + 1) * n_seq, 2 * dq:3 * dq] = (slab(c_b, l) * acc).astype(BF16)
    store_state(sc_out, ec, CONV_C_W - 1)

    d_in = proj(7)
    ed = ext_slabs(sd_ref, POOL_PAST, d_in)
    memo = {}

    def wsum(i, w):
        if i < 0:
            return None
        if w == 1:
            return ed[i]
        if (i, w) not in memo:
            a, b = wsum(i, w // 2), wsum(i - w // 2, w // 2)
            memo[(i, w)] = a if b is None else a + b
        return memo[(i, w)]

    gc = dq // len(POOL_WINDOWS)
    lane_grp = lax.broadcasted_iota(I32, (n_seq, dq), 1) // gc
    for l in range(n_step):
        mean = jnp.zeros((n_seq, dq), F32)
        for g, w in enumerate(POOL_WINDOWS):
            count = min(PAST_LEN + l + 1, w)
            mean = jnp.where(lane_grp == g, wsum(POOL_PAST + l, w) * (1.0 / count), mean)
        dd = (mean - slab(d_in, l)).astype(BF16)
        cat_ref[l * n_seq:(l + 1) * n_seq, 3 * dq:4 * dq] = (
            _dot(dd, wpool_ref[...]) * pscale_ref[...]).astype(BF16)
    store_state(sd_out, ed, POOL_PAST)

    mix = _dot(cat_ref[...], wout_ref[...])
    h_out[...] = _ln(alpha * h + mix, lmg_ref[...], lmb_ref[...])


def _mixer_sample(x_tm, sa, sc, sd, lw, first_layer, alpha, n_seq, n_step):
    rows, d = x_tm.shape
    d_mix = lw['w_out'].shape[0]
    dq = d_mix // 4
    assert n_seq % SUBLANES == 0 and n_step <= CHUNK and PAST_LEN % CHUNK == 0
    ins = [x_tm, sa, sc, sd, lw['ln_in_g'], lw['ln_in_b'], lw['w_in'], lw['conv_a_w'], lw['conv_a_b'],
           lw['ln_a_g'], lw['ln_a_b'], lw['ln_v_g'], lw['ln_v_b'], lw['ws_vec'], lw['bs_vec'],
           lw['conv_c_w'], lw['w_pool_bd'], lw['pool_scale'], lw['w_out'], lw['ln_mix_g'], lw['ln_mix_b']]
    out_shape = (
        jax.ShapeDtypeStruct((rows, d), F32),
        jax.ShapeDtypeStruct(((CONV_A_W - 1) * n_seq, dq), F32),
        jax.ShapeDtypeStruct(((CONV_C_W - 1) * n_seq, dq), F32),
        jax.ShapeDtypeStruct((POOL_PAST * n_seq, dq), F32),
        jax.ShapeDtypeStruct((rows, dq), F32),
    )
    return pl.pallas_call(
        functools.partial(_mixer_sample_kernel, first_layer, alpha, n_seq, n_step, d_mix),
        grid=(1,),
        in_specs=[_full_spec(a) for a in ins],
        out_specs=tuple(pl.BlockSpec(s.shape, lambda i: (0, 0)) for s in out_shape),
        out_shape=out_shape,
        scratch_shapes=[pltpu.VMEM((rows, d_mix), BF16)],
        compiler_params=_cparams(("arbitrary",)),
        name="mixer_sample",
    )(*ins)


def _ffn_dense_kernel(alpha, x_ref, wg_ref, wu_ref, wd_ref, g_ref, b_ref, o_ref, acc_ref):
    f = pl.program_id(1)

    @pl.when(f == 0)
    def _():
        acc_ref[...] = jnp.zeros_like(acc_ref)

    xb = x_ref[...].astype(BF16)
    mid = _silu(_dot(xb, wg_ref[...])) * _dot(xb, wu_ref[...])
    acc_ref[...] += _dot(mid.astype(BF16), wd_ref[...])

    @pl.when(f == pl.num_programs(1) - 1)
    def _():
        o_ref[...] = _ln(alpha * x_ref[...] + acc_ref[...], g_ref[...], b_ref[...])


def _ffn_dense(x, wg, wu, wd, g, b, alpha, tm, tf):
    t, d = x.shape
    dff = wg.shape[1]
    assert t % tm == 0 and dff % tf == 0
    return pl.pallas_call(
        functools.partial(_ffn_dense_kernel, alpha),
        grid=(t // tm, dff // tf),
        in_specs=[pl.BlockSpec((tm, d), lambda i, f: (i, 0)),
                  pl.BlockSpec((d, tf), lambda i, f: (0, f)),
                  pl.BlockSpec((d, tf), lambda i, f: (0, f)),
                  pl.BlockSpec((tf, d), lambda i, f: (f, 0)),
                  pl.BlockSpec((1, d), lambda i, f: (0, 0)),
                  pl.BlockSpec((1, d), lambda i, f: (0, 0))],
        out_specs=pl.BlockSpec((tm, d), lambda i, f: (i, 0)),
        out_shape=jax.ShapeDtypeStruct((t, d), F32),
        scratch_shapes=[pltpu.VMEM((tm, d), F32)],
        compiler_params=_cparams(("arbitrary", "arbitrary")),
        name="ffn_dense",
    )(x, wg, wu, wd, g, b)


def _router_kernel(tm, h_ref, wcat_ref, cin_ref, mi_ref, mf_ref, cnt_ref, carry_ref, before_ref):
    i = pl.program_id(0)

    @pl.when(i == 0)
    def _():
        carry_ref[...] = cin_ref[...].astype(F32)
        row = lax.broadcasted_iota(I32, (tm, tm), 0)
        col = lax.broadcasted_iota(I32, (tm, tm), 1)
        before_ref[...] = jnp.where(row < col, 1.0, 0.0).astype(BF16)

    h = h_ref[...]
    h_hi = h.astype(BF16)
    h_lo = (h - h_hi.astype(F32)).astype(BF16)
    p_hi = _dot(h_hi, wcat_ref[...])
    p_lo = _dot(h_lo, wcat_ref[...])
    logits = (p_hi[:, 0:LANES] + (p_lo[:, 0:LANES] + p_hi[:, LANES:2 * LANES])) + p_lo[:, LANES:2 * LANES]
    lg = logits.T[0:N_EXPERTS, :]
    ex = lax.broadcasted_iota(I32, (N_EXPERTS, tm), 0).astype(F32)
    m1 = jnp.max(lg, axis=0, keepdims=True)
    i1 = jnp.min(jnp.where(lg == m1, ex, float(N_EXPERTS)), axis=0, keepdims=True)
    sel1 = ex == i1
    rest = jnp.where(sel1, -jnp.inf, lg)
    m2 = jnp.max(rest, axis=0, keepdims=True)
    i2 = jnp.min(jnp.where(rest == m2, ex, float(N_EXPERTS)), axis=0, keepdims=True)
    sel2 = ex == i2
    e2 = jnp.exp(m2 - m1)
    den = 1.0 + e2
    g1 = 1.0 / den
    g2 = e2 / den

    sel = jnp.where(sel1 | sel2, 1.0, 0.0)
    sel_pad = jnp.concatenate([sel, jnp.zeros_like(sel)], axis=0).astype(BF16)
    base = _dot(sel_pad, before_ref[...])[0:N_EXPERTS, :] + carry_ref[:, 0:1]
    r1 = jnp.sum(jnp.where(sel1, base, 0.0), axis=0, keepdims=True)
    r2 = jnp.sum(jnp.where(sel2, base, 0.0), axis=0, keepdims=True)
    carry_ref[...] = carry_ref[...] + jnp.sum(sel, axis=1, keepdims=True)

    meta = jnp.where(ex == 0, i1, jnp.where(ex == 1, i2, jnp.where(ex == 2, r1, r2)))
    mi_ref[...] = meta.astype(I32)
    gt = jnp.where(ex == 0, g1, jnp.where(ex == 1, g2, 0.0))
    gt = jnp.concatenate([gt, jnp.zeros((LANES - N_EXPERTS, tm), F32)], axis=0)
    mf_ref[...] = gt.T[:, 0:SUBLANES]
    cnt_ref[...] = carry_ref[...].astype(I32)


def _router(h, wr_cat, counts_in):
    t, d = h.shape
    tm = min(TM_ROUTE, t)
    assert t % tm == 0 and N_EXPERTS == SUBLANES
    return pl.pallas_call(
        functools.partial(_router_kernel, tm),
        grid=(t // tm,),
        in_specs=[pl.BlockSpec((tm, d), lambda i: (i, 0)), _full_spec(wr_cat), _full_spec(counts_in)],
        out_specs=(pl.BlockSpec((SUBLANES, tm), lambda i: (0, i)),
                   pl.BlockSpec((tm, SUBLANES), lambda i: (i, 0)),
                   pl.BlockSpec((SUBLANES, LANES), lambda i: (0, 0))),
        out_shape=(jax.ShapeDtypeStruct((SUBLANES, t), I32),
                   jax.ShapeDtypeStruct((t, SUBLANES), F32),
                   jax.ShapeDtypeStruct((SUBLANES, LANES), I32)),
        scratch_shapes=[pltpu.VMEM((SUBLANES, LANES), F32), pltpu.VMEM((tm, tm), BF16)],
        compiler_params=_cparams(("arbitrary",)),
        name="router",
    )(h, wr_cat, counts_in)


def _row_copy(src, src_row, dst, dst_row, sem):
    return pltpu.make_async_copy(src.at[pl.ds(src_row, 1), :], dst.at[pl.ds(dst_row, 1), :], sem)


def _rows_wait(src, dst, n_rows, sem):
    pltpu.make_async_copy(src.at[pl.ds(0, n_rows), :], dst.at[pl.ds(0, n_rows), :], sem).wait()


def _dispatch_kernel(tm, tm_expert, n_blocks, blk_ranges, zi_ref, dest_ref, *rest):
    n_src = len(blk_ranges)
    h_refs = rest[:n_src]
    xs_ref, xbuf, zblk, lsems, rsems, zsem = rest[n_src:]
    i = pl.program_id(0)
    n = pl.num_programs(0)

    def block_load(g, start):
        slot = lax.rem(g, N_STAGE)
        for h_ref, (b0, b1) in zip(h_refs, blk_ranges):
            @pl.when((g >= b0) & (g < b1))
            def _(h_ref=h_ref, b0=b0):
                cp = pltpu.make_async_copy(h_ref.at[pl.ds((g - b0) * tm, tm), :], xbuf.at[slot],
                                           lsems.at[slot])
                if start:
                    cp.start()
                else:
                    cp.wait()

    def rows_wait(g):
        slot = lax.rem(g, N_STAGE)
        for _k in range(TOP_K):
            _rows_wait(xbuf.at[slot], xs_ref, tm, rsems.at[slot])

    @pl.when(i == 0)
    def _():
        block_load(i, True)

    @pl.when(i >= N_STAGE - 1)
    def _():
        rows_wait(i - (N_STAGE - 1))

    @pl.when(i + 1 < n)
    def _():
        block_load(i + 1, True)

    block_load(i, False)
    cur = lax.rem(i, N_STAGE)

    def issue(j, carry):
        r0 = pl.multiple_of(j * SUBLANES, SUBLANES)
        for u in range(SUBLANES):
            for k in range(TOP_K):
                dst = dest_ref[0, 0, j * SUBLANES + (k * tm + u)]
                _row_copy(xbuf.at[cur], r0 + u, xs_ref, dst, rsems.at[cur]).start(priority=k)
        return carry

    lax.fori_loop(0, tm // SUBLANES, issue, 0)

    @pl.when(i == n - 1)
    def _():
        for back in range(N_STAGE - 2, -1, -1):
            @pl.when(i >= back)
            def _(back=back):
                rows_wait(i - back)

    @pl.when(i == 0)
    def _():
        zblk[...] = jnp.zeros_like(zblk)
        for e in range(N_EXPERTS):
            lo = zi_ref[e] + zi_ref[N_EXPERTS + e]
            hi = zi_ref[e] + zi_ref[2 * N_EXPERTS + e]

            def zissue(r, carry):
                _row_copy(zblk, 0, xs_ref, r, zsem).start()
                return carry

            def zdrain(r, carry):
                _row_copy(zblk, 0, xs_ref, r, zsem).wait()
                return carry

            lax.fori_loop(lo, hi, zissue, 0)
            lax.fori_loop(lo, hi, zdrain, 0)

        def bcopy(b):
            return pltpu.make_async_copy(zblk, xs_ref.at[pl.ds(b * tm_expert, tm_expert), :], zsem)

        def bissue(b, carry):
            bcopy(b).start()
            return carry

        def bdrain(b, carry):
            bcopy(b).wait()
            return carry

        lax.fori_loop(zi_ref[3 * N_EXPERTS], n_blocks, bissue, 0)
        lax.fori_loop(zi_ref[3 * N_EXPERTS], n_blocks, bdrain, 0)


def _dispatch(hs_list, dest_blk, zinfo, n_blocks, tm_expert):
    d = hs_list[0].shape[1]
    nb = dest_blk.shape[0]
    tm = dest_blk.shape[2] // TOP_K
    blk_ranges, b0 = [], 0
    for h in hs_list:
        assert h.shape[0] % tm == 0 and h.shape[0] >= tm
        blk_ranges.append((b0, b0 + h.shape[0] // tm))
        b0 = blk_ranges[-1][1]
    assert b0 == nb
    any_spec = pl.BlockSpec(memory_space=pl.ANY)
    return pl.pallas_call(
        functools.partial(_dispatch_kernel, tm, tm_expert, n_blocks, tuple(blk_ranges)),
        grid_spec=pltpu.PrefetchScalarGridSpec(
            num_scalar_prefetch=1,
            grid=(nb,),
            in_specs=[pl.BlockSpec((1, 1, TOP_K * tm), lambda i, zi: (i, 0, 0), memory_space=pltpu.SMEM)]
            + [any_spec] * len(hs_list),
            out_specs=any_spec,
            scratch_shapes=[pltpu.VMEM((N_STAGE, tm, d), F32), pltpu.VMEM((tm_expert, d), F32),
                            pltpu.SemaphoreType.DMA((N_STAGE,)), pltpu.SemaphoreType.DMA((N_STAGE,)),
                            pltpu.SemaphoreType.DMA(())],
        ),
        out_shape=jax.ShapeDtypeStruct((n_blocks * tm_expert, d), F32),
        compiler_params=_cparams(("arbitrary",)),
        name="moe_dispatch",
    )(zinfo, dest_blk, *hs_list)


def _ffn_moe_kernel(be_ref, nu_ref, x_ref, wg_ref, wu_ref, wd_ref, y_ref, acc_ref):
    b = pl.program_id(0)
    f = pl.program_id(1)

    @pl.when(b < nu_ref[0])
    def _():
        @pl.when(f == 0)
        def _():
            acc_ref[...] = jnp.zeros_like(acc_ref)

        xb = x_ref[...].astype(BF16)
        mid = _silu(_dot(xb, wg_ref[0])) * _dot(xb, wu_ref[0])
        acc_ref[...] += _dot(mid.astype(BF16), wd_ref[0])

        @pl.when(f == pl.num_programs(1) - 1)
        def _():
            y_ref[...] = acc_ref[...]

    @pl.when((b >= nu_ref[0]) & (f == pl.num_programs(1) - 1))
    def _():
        y_ref[...] = jnp.zeros_like(y_ref)


def _ffn_moe(xs, wg, wu, wd, block_e, n_used, tm, tf):
    cap, d = xs.shape
    dff = wg.shape[2]
    nf = dff // tf
    assert cap % tm == 0 and dff % tf == 0

    def blk(b, nu):
        return jnp.minimum(b, nu[0] - 1)

    def fcol(b, f, nu):
        return jnp.where(b < nu[0], f, nf - 1)

    return pl.pallas_call(
        _ffn_moe_kernel,
        grid_spec=pltpu.PrefetchScalarGridSpec(
            num_scalar_prefetch=2,
            grid=(cap // tm, nf),
            in_specs=[pl.BlockSpec((tm, d), lambda b, f, be, nu: (blk(b, nu), 0)),
                      pl.BlockSpec((1, d, tf), lambda b, f, be, nu: (be[b], 0, fcol(b, f, nu))),
                      pl.BlockSpec((1, d, tf), lambda b, f, be, nu: (be[b], 0, fcol(b, f, nu))),
                      pl.BlockSpec((1, tf, d), lambda b, f, be, nu: (be[b], fcol(b, f, nu), 0))],
            out_specs=pl.BlockSpec((tm, d), lambda b, f, be, nu: (b, 0)),
            scratch_shapes=[pltpu.VMEM((tm, d), F32)],
        ),
        out_shape=jax.ShapeDtypeStruct((cap, d), F32),
        compiler_params=_cparams(("arbitrary", "arbitrary")),
        name="ffn_moe",
    )(block_e, n_used, xs, wg, wu, wd)


def _combine_kernel(alpha, tm, dcur_ref, dnext_ref, h_ref, gate_ref, g_ref, b_ref, ys_ref, o_ref, ybuf, sems):
    i = pl.program_id(0)
    n = pl.num_programs(0)

    def issue(dref, slot):
        def body(j, carry):
            r0 = pl.multiple_of(j * SUBLANES, SUBLANES)
            for u in range(SUBLANES):
                for k in range(TOP_K):
                    src = dref[0, 0, j * SUBLANES + (k * tm + u)]
                    _row_copy(ys_ref, src, ybuf.at[slot, k], r0 + u, sems.at[slot]).start(priority=k)
            return carry

        lax.fori_loop(0, tm // SUBLANES, body, 0)

    @pl.when(i == 0)
    def _():
        issue(dcur_ref, 0)

    @pl.when(i + 1 < n)
    def _():
        issue(dnext_ref, (i + 1) % 2)

    slot = i % 2
    for k in range(TOP_K):
        _rows_wait(ys_ref, ybuf.at[slot, k], tm, sems.at[slot])

    gates = gate_ref[...]
    ffn = gates[:, 0:1] * ybuf[slot, 0] + gates[:, 1:2] * ybuf[slot, 1]
    o_ref[...] = _ln(alpha * h_ref[...] + ffn, g_ref[...], b_ref[...])


def _combine(h, dest_blk, gates, ys, g, b, alpha):
    t, d = h.shape
    nb = dest_blk.shape[0]
    tm = dest_blk.shape[2] // TOP_K
    dest_spec = lambda imap: pl.BlockSpec((1, 1, TOP_K * tm), imap, memory_space=pltpu.SMEM)
    return pl.pallas_call(
        functools.partial(_combine_kernel, alpha, tm),
        grid=(nb,),
        in_specs=[dest_spec(lambda i: (i, 0, 0)),
                  dest_spec(lambda i: (jnp.minimum(i + 1, nb - 1), 0, 0)),
                  pl.BlockSpec((tm, d), lambda i: (i, 0)),
                  pl.BlockSpec((tm, SUBLANES), lambda i: (i, 0)),
                  pl.BlockSpec((1, d), lambda i: (0, 0)),
                  pl.BlockSpec((1, d), lambda i: (0, 0)),
                  pl.BlockSpec(memory_space=pl.ANY)],
        out_specs=pl.BlockSpec((tm, d), lambda i: (i, 0)),
        out_shape=jax.ShapeDtypeStruct((t, d), F32),
        scratch_shapes=[pltpu.VMEM((2, TOP_K, tm, d), F32), pltpu.SemaphoreType.DMA((2,))],
        compiler_params=_cparams(("arbitrary",)),
        name="moe_combine",
    )(dest_blk, dest_blk, h, gates, g, b, ys)


def _ffn_routed(hs_list, mw, g, b, alpha, tm_expert):
    d = hs_list[0].shape[1]
    counts_in = jnp.zeros((SUBLANES, LANES), I32)
    routed = []
    for h in hs_list:
        meta_i, gates, counts_in = _router(h, mw['w_router_cat'], counts_in)
        routed.append((meta_i, gates))
    counts = counts_in[:N_EXPERTS, 0]
    padded = (counts + tm_expert - 1) // tm_expert * tm_expert
    pad_end = jnp.cumsum(padded)
    pad_start = (pad_end - padded).astype(I32)
    n_assign = sum(h.shape[0] for h in hs_list) * TOP_K
    n_blocks = -(-n_assign // tm_expert) + N_EXPERTS
    cap = n_blocks * tm_expert
    n_used = (pad_end[-1] // tm_expert).astype(I32)
    blk_start = jnp.minimum(jnp.arange(n_blocks, dtype=I32), n_used - 1) * tm_expert
    block_e = jnp.minimum(jnp.sum(blk_start[:, None] >= pad_end[None, :], axis=1), N_EXPERTS - 1).astype(I32)
    zinfo = jnp.concatenate([pad_start, counts, padded, n_used.reshape(1)]).astype(I32)

    dests = []
    for h, (meta_i, _) in zip(hs_list, routed):
        dest = pad_start[meta_i[0:TOP_K, :]] + meta_i[TOP_K:2 * TOP_K, :]
        nb = h.shape[0] // TM_MOVE
        dests.append(dest.reshape(TOP_K, nb, TM_MOVE).transpose(1, 0, 2).reshape(nb, 1, TOP_K * TM_MOVE))
    xs = _dispatch(hs_list, jnp.concatenate(dests, axis=0), zinfo, n_blocks, tm_expert)
    ys = _ffn_moe(xs, mw['w_e_gate'], mw['w_e_up'], mw['w_e_down'], block_e, n_used.reshape(1), tm_expert,
                  TF_EXPERT)
    return [_combine(h, dest_blk, gates, ys, g, b, alpha)
            for h, dest_blk, (_, gates) in zip(hs_list, dests, routed)]


def _prep_layer(i, p, n_step):
    dq = p['conv_a_w'].shape[2]
    hd = dq // N_HEADS_B
    tril = jnp.tril(jnp.ones((CHUNK, CHUNK), dtype=bool))
    w_s = p['w_s'][i]
    b_s = p['b_s'][i]
    wp = p['w_pool'][i]
    ng, gc = wp.shape[0], wp.shape[1]
    w_pool_bd = (jnp.eye(ng, dtype=F32)[:, None, :, None] * wp[:, :, None, :]).reshape(ng * gc, ng * gc)
    row = lambda a: a.reshape(1, -1)
    return dict(
        ln_in_g=row(p['ln_in_g']), ln_in_b=row(p['ln_in_b']),
        w_in=p['w_in'][i].astype(BF16),
        conv_a_w=p['conv_a_w'][i], conv_a_b=row(p['conv_a_b'][i]),
        ln_a_g=row(p['ln_a_g'][i]), ln_a_b=row(p['ln_a_b'][i]),
        ln_v_g=row(p['ln_v_g'][i]), ln_v_b=row(p['ln_v_b'][i]),
        ws_tril=jnp.where(tril[None], w_s, 0).astype(BF16),
        bs_mat=jnp.repeat(b_s.T, hd, axis=1),
        ws_vec=jnp.repeat(jnp.where(tril[None], w_s, 0)[:, :n_step, :n_step].transpose(1, 2, 0)
                          .reshape(n_step * n_step, N_HEADS_B), hd, axis=1),
        bs_vec=jnp.repeat(b_s[:, :n_step].T, hd, axis=1),
        conv_c_w=p['conv_c_w'][i],
        w_pool_bd=w_pool_bd.astype(BF16), pool_scale=row(p['pool_scale'][i]),
        w_out=p['w_out'][i].astype(BF16),
        ln_mix_g=row(p['ln_mix_g'][i]), ln_mix_b=row(p['ln_mix_b'][i]),
    )


def kernel(x_prompt, x_sample, state_conv_a, state_conv_c, state_pool_d, ln_in_g, ln_in_b, w_in, conv_a_w,
           conv_a_b, ln_a_g, ln_a_b, ln_v_g, ln_v_b, w_s, b_s, conv_c_w, w_pool, pool_scale, w_out,
           ln_mix_g, ln_mix_b, w_ff_gate, w_ff_up, w_ff_down, w_router, w_e_gate, w_e_up, w_e_down,
           ln_ffn_g, ln_ffn_b):
    p = dict(ln_in_g=ln_in_g, ln_in_b=ln_in_b, w_in=w_in, conv_a_w=conv_a_w, conv_a_b=conv_a_b,
             ln_a_g=ln_a_g, ln_a_b=ln_a_b, ln_v_g=ln_v_g, ln_v_b=ln_v_b, w_s=w_s, b_s=b_s,
             conv_c_w=conv_c_w, w_pool=w_pool, pool_scale=pool_scale, w_out=w_out,
             ln_mix_g=ln_mix_g, ln_mix_b=ln_mix_b)
    depth = w_in.shape[0]
    bsz, seq, d = x_prompt.shape
    n_seq, n_step, _ = x_sample.shape
    dq = conv_a_w.shape[2]
    alpha = float((2.0 * depth) ** 0.25)

    hp = x_prompt
    hs = x_sample.transpose(1, 0, 2).reshape(n_step * n_seq, d)
    outs = {k: [] for k in ('pa', 'pc', 'pd', 'pv', 'sa', 'sc', 'sd', 'sv')}
    for i in range(depth):
        lw = _prep_layer(i, p, n_step)
        hp, pa, pc, pd, pv = _mixer_prompt(hp, lw, i == 0, alpha)
        tm_state = lambda s: s.transpose(1, 0, 2).reshape(-1, dq)
        hs, sa, sc, sd, sv = _mixer_sample(hs, tm_state(state_conv_a[i]), tm_state(state_conv_c[i]),
                                           tm_state(state_pool_d[i]), lw, i == 0, alpha, n_seq, n_step)
        lg, lb = ln_ffn_g[i].reshape(1, -1), ln_ffn_b[i].reshape(1, -1)
        j = i // 2
        if i % 2 == 0:
            wg, wu, wd = w_ff_gate[j].astype(BF16), w_ff_up[j].astype(BF16), w_ff_down[j].astype(BF16)
            hp = _ffn_dense(hp.reshape(bsz * seq, d), wg, wu, wd, lg, lb, alpha, TM_DENSE,
                            TF_DENSE).reshape(bsz, seq, d)
            hs = _ffn_dense(hs, wg, wu, wd, lg, lb, alpha, hs.shape[0], TF_DENSE)
        else:
            wr = jnp.pad(w_router[j], ((0, 0), (0, LANES - N_EXPERTS)))
            wr_hi = wr.astype(BF16)
            wr_lo = (wr - wr_hi.astype(F32)).astype(BF16)
            mw = dict(w_router_cat=jnp.concatenate([wr_hi, wr_lo], axis=1),
                      w_e_gate=w_e_gate[j].astype(BF16), w_e_up=w_e_up[j].astype(BF16),
                      w_e_down=w_e_down[j].astype(BF16))
            hp, hs = _ffn_routed([hp.reshape(bsz * seq, d), hs], mw, lg, lb, alpha, TM_EXPERT)
            hp = hp.reshape(bsz, seq, d)
        back = lambda a: a.reshape(-1, n_seq, dq).transpose(1, 0, 2)
        for k, val in zip(('pa', 'pc', 'pd', 'pv', 'sa', 'sc', 'sd', 'sv'),
                          (pa, pc, pd, pv, back(sa), back(sc), back(sd), back(sv))):
            outs[k].append(val)
    y_sample = hs.reshape(n_step, n_seq, d).transpose(1, 0, 2)
    st = lambda k: jnp.stack(outs[k])
    return (hp, y_sample, st('pa'), st('pc'), st('pd'), st('pv'), st('sa'), st('sc'), st('sd'), st('sv'))
```

```python
import functools

import numpy as np
import jax
import jax.numpy as jnp
from jax import lax
from jax.experimental import pallas as pl
from jax.experimental.pallas import tpu as pltpu

F32 = jnp.float32
BF16 = jnp.bfloat16
I32 = jnp.int32

PAST_LEN = 16384
CHUNK = 128
N_HEADS_B = 4
POOL_WINDOWS = (2, 4, 8, 16)
POOL_PAST = 15
CONV_A_W = 31
CONV_C_W = 3
N_EXPERTS = 8
TOP_K = 2
LN_EPS = 1e-5
SQRT_HALF = float(np.sqrt(0.5).astype(np.float32))

LANES = 128
SUBLANES = 8
VMEM_LIMIT_BYTES = 56 * 1024 * 1024

HIST_A = 32
HIST_C = 8
HIST_D = 24

TL_PROMPT = 512
TM_DENSE = 512
TF_DENSE = 1792
TF_EXPERT = 1792
TM_ROUTE = 512
TM_MOVE = 256
TM_EXPERT = 512
N_STAGE = 3


def _ln(x, g, b):
    mu = jnp.mean(x, axis=-1, keepdims=True)
    xc = x - mu
    var = jnp.mean(xc * xc, axis=-1, keepdims=True)
    return xc * lax.rsqrt(var + LN_EPS) * g + b


def _gelu(x):
    return 0.5 * x * (1.0 + lax.erf(x * SQRT_HALF))


def _silu(x):
    return x * jax.nn.sigmoid(x)


def _dot(a, b):
    return jnp.dot(a, b, preferred_element_type=F32)


def _cparams(sem):
    return pltpu.CompilerParams(dimension_semantics=sem, vmem_limit_bytes=VMEM_LIMIT_BYTES)


def _mixer_prompt_kernel(first_layer, alpha, tl, d_mix,
                         x_ref, lng_ref, lnb_ref, win_ref, caw_ref, cab_ref, lag_ref, lab_ref,
                         lvg_ref, lvb_ref, ws_ref, bsm_ref, ccw_ref, wpool_ref, pscale_ref,
                         wout_ref, lmg_ref, lmb_ref,
                         h_out, sa_out, sc_out, sd_out, v_out,
                         ext_a, ext_c, ext_d, cat_ref, z_ref, hb_ref, mix_ref):
    dq = d_mix // 4
    j = pl.program_id(1)
    nj = pl.num_programs(1)

    @pl.when(j == 0)
    def _():
        ext_a[0, 0:HIST_A, :] = jnp.zeros((HIST_A, dq), F32)
        ext_c[0:HIST_C, :] = jnp.zeros((HIST_C, dq), F32)
        ext_d[0:HIST_D, :] = jnp.zeros((HIST_D, dq), F32)

    d = x_ref.shape[2]
    rb = CHUNK

    def resid(r0, n):
        return h_out[0, r0:r0 + n, :] if first_layer else x_ref[0, r0:r0 + n, :]

    for r0 in range(0, tl, rb):
        x = x_ref[0, r0:r0 + rb, :]
        if first_layer:
            x = _ln(x, lng_ref[...], lnb_ref[...])
            h_out[0, r0:r0 + rb, :] = x
        hb_ref[r0:r0 + rb, :] = x.astype(BF16)

    z_ref[...] = _dot(hb_ref[...], win_ref[...])

    def zcol(r0, i, n=rb):
        return z_ref[r0:r0 + n, i * dq:(i + 1) * dq]

    chunk_rows = range(0, tl, CHUNK)

    for r0 in chunk_rows:
        ext_a[0, HIST_A + r0:HIST_A + r0 + rb, :] = zcol(r0, 0) * jax.nn.sigmoid(zcol(r0, 1))
    n_ext = HIST_A + tl
    for r0 in range(0, n_ext - SUBLANES, rb):
        n = min(rb, n_ext - SUBLANES - r0)
        blk = ext_a[0, r0:r0 + n + SUBLANES, :]
        for s in range(1, SUBLANES):
            ext_a[s, r0:r0 + n, :] = pltpu.roll(blk, n + SUBLANES - s, axis=0)[0:n, :]
    off_a = HIST_A - (CONV_A_W - 1)
    for q0 in range(0, tl, 64):
        acc = jnp.zeros((64, dq), F32)
        for k in range(CONV_A_W):
            s = (off_a + k) % SUBLANES
            row = off_a + k - s + q0
            acc = acc + ext_a[s, row:row + 64, :] * caw_ref[k:k + 1, :]
        y = _silu(_ln(acc + cab_ref[...], lag_ref[...], lab_ref[...]))
        cat_ref[q0:q0 + 64, 0:dq] = y.astype(BF16)

    hd = dq // N_HEADS_B
    lane_head = lax.broadcasted_iota(I32, (CHUNK, dq), 1) // hd
    for r0 in chunk_rows:
        v = _ln(_gelu(zcol(r0, 3)), lvg_ref[...], lvb_ref[...])
        sm = bsm_ref[...]
        for hh in range(N_HEADS_B):
            vm = jnp.where(lane_head == hh, v, 0.0).astype(BF16)
            sm = sm + _dot(ws_ref[hh], vm)
        cat_ref[r0:r0 + CHUNK, dq:2 * dq] = (_gelu(zcol(r0, 2)) * sm).astype(BF16)
        if r0 == tl - CHUNK:
            @pl.when(j == nj - 1)
            def _(v=v):
                v_out[0] = v

    off_c = HIST_C - (CONV_C_W - 1)
    for r0 in chunk_rows:
        ext_c[HIST_C + r0:HIST_C + r0 + rb, :] = zcol(r0, 6) * zcol(r0, 4)
    for r0 in chunk_rows:
        conv_c = jnp.zeros((CHUNK, dq), F32)
        for k in range(CONV_C_W):
            conv_c = conv_c + ext_c[off_c + r0 + k:off_c + r0 + k + CHUNK, :] * ccw_ref[k:k + 1, :]
        cat_ref[r0:r0 + CHUNK, 2 * dq:3 * dq] = (zcol(r0, 5) * conv_c).astype(BF16)

    gc = dq // len(POOL_WINDOWS)
    lane_grp = lax.broadcasted_iota(I32, (CHUNK, dq), 1) // gc
    lead = 2 * SUBLANES
    for r0 in chunk_rows:
        ext_d[HIST_D + r0:HIST_D + r0 + rb, :] = zcol(r0, 7)
    for r0 in chunk_rows:
        e = ext_d[HIST_D + r0 - lead:HIST_D + r0 + CHUNK, :]
        s2 = e + pltpu.roll(e, 1, axis=0)
        s4 = s2 + pltpu.roll(s2, 2, axis=0)
        s8 = s4 + pltpu.roll(s4, 4, axis=0)
        s16 = s8 + pltpu.roll(s8, 8, axis=0)
        sums = (s2, s4, s8, s16)
        pos = (j * tl + r0 + lax.broadcasted_iota(I32, (CHUNK, 1), 0)).astype(F32)
        mean = jnp.zeros((CHUNK, dq), F32)
        for g, w in enumerate(POOL_WINDOWS):
            inv = 1.0 / jnp.minimum(pos + 1.0, float(w))
            mean = jnp.where(lane_grp == g, sums[g][lead:lead + CHUNK, :] * inv, mean)
        dd = (mean - e[lead:lead + CHUNK, :]).astype(BF16)
        cat_ref[r0:r0 + CHUNK, 3 * dq:4 * dq] = (_dot(dd, wpool_ref[...]) * pscale_ref[...]).astype(BF16)

    mix_ref[...] = _dot(cat_ref[...], wout_ref[...])
    for r0 in range(0, tl, 64):
        h_out[0, r0:r0 + 64, :] = _ln(alpha * resid(r0, 64) + mix_ref[r0:r0 + 64, :], lmg_ref[...], lmb_ref[...])

    @pl.when(j == nj - 1)
    def _():
        sa_out[0] = ext_a[0, HIST_A + tl - (CONV_A_W - 1):HIST_A + tl, :]
        sc_out[0] = ext_c[HIST_C + tl - (CONV_C_W - 1):HIST_C + tl, :]
        sd_out[0] = ext_d[HIST_D + tl - POOL_PAST:HIST_D + tl, :]

    ext_a[0, 0:HIST_A, :] = ext_a[0, tl:tl + HIST_A, :]
    ext_c[0:HIST_C, :] = ext_c[tl:tl + HIST_C, :]
    ext_d[0:HIST_D, :] = ext_d[tl:tl + HIST_D, :]


def _full_spec(arr):
    nd = arr.ndim
    return pl.BlockSpec(arr.shape, lambda *_: (0,) * nd)


def _mixer_prompt(x, lw, first_layer, alpha):
    bsz, seq, d = x.shape
    d_mix = lw['w_out'].shape[0]
    dq = d_mix // 4
    tl = TL_PROMPT
    assert seq % tl == 0 and tl % CHUNK == 0 and seq >= CHUNK
    params = [lw['ln_in_g'], lw['ln_in_b'], lw['w_in'], lw['conv_a_w'], lw['conv_a_b'], lw['ln_a_g'],
              lw['ln_a_b'], lw['ln_v_g'], lw['ln_v_b'], lw['ws_tril'], lw['bs_mat'], lw['conv_c_w'],
              lw['w_pool_bd'], lw['pool_scale'], lw['w_out'], lw['ln_mix_g'], lw['ln_mix_b']]
    out_shape = (
        jax.ShapeDtypeStruct((bsz, seq, d), F32),
        jax.ShapeDtypeStruct((bsz, CONV_A_W - 1, dq), F32),
        jax.ShapeDtypeStruct((bsz, CONV_C_W - 1, dq), F32),
        jax.ShapeDtypeStruct((bsz, POOL_PAST, dq), F32),
        jax.ShapeDtypeStruct((bsz, CHUNK, dq), F32),
    )
    state_spec = lambda rows: pl.BlockSpec((1, rows, dq), lambda b, j: (b, 0, 0))
    return pl.pallas_call(
        functools.partial(_mixer_prompt_kernel, first_layer, alpha, tl, d_mix),
        grid=(bsz, seq // tl),
        in_specs=[pl.BlockSpec((1, tl, d), lambda b, j: (b, j, 0))] + [_full_spec(p) for p in params],
        out_specs=(pl.BlockSpec((1, tl, d), lambda b, j: (b, j, 0)),
                   state_spec(CONV_A_W - 1), state_spec(CONV_C_W - 1), state_spec(POOL_PAST),
                   state_spec(CHUNK)),
        out_shape=out_shape,
        scratch_shapes=[pltpu.VMEM((SUBLANES, HIST_A + tl, dq), F32), pltpu.VMEM((HIST_C + tl, dq), F32),
                        pltpu.VMEM((HIST_D + tl, dq), F32), pltpu.VMEM((tl, d_mix), BF16),
                        pltpu.VMEM((tl, lw['w_in'].shape[1]), F32), pltpu.VMEM((tl, d), BF16),
                        pltpu.VMEM((tl, d), F32)],
        compiler_params=_cparams(("arbitrary", "arbitrary")),
        name="mixer_prompt",
    )(x, *params)


def _mixer_sample_kernel(first_layer, alpha, n_seq, n_step, d_mix,
                         x_ref, sa_ref, sc_ref, sd_ref, lng_ref, lnb_ref, win_ref, caw_ref, cab_ref,
                         lag_ref, lab_ref, lvg_ref, lvb_ref, wsv_ref, bsv_ref, ccw_ref, wpool_ref,
                         pscale_ref, wout_ref, lmg_ref, lmb_ref,
                         h_out, sa_out, sc_out, sd_out, v_out, cat_ref):
    dq = d_mix // 4
    x = x_ref[...]
    h = _ln(x, lng_ref[...], lnb_ref[...]) if first_layer else x
    hb = h.astype(BF16)

    def proj(i):
        return _dot(hb, win_ref[:, i * dq:(i + 1) * dq])

    def slab(val, i):
        return val[i * n_seq:(i + 1) * n_seq, :]

    def ext_slabs(state_ref, n_past, cur):
        return ([state_ref[i * n_seq:(i + 1) * n_seq, :] for i in range(n_past)]
                + [slab(cur, i) for i in range(n_step)])

    def store_state(out_ref, slabs, n_keep):
        for i, sl in enumerate(slabs[len(slabs) - n_keep:]):
            out_ref[i * n_seq:(i + 1) * n_seq, :] = sl

    a_glu = proj(0) * jax.nn.sigmoid(proj(1))
    ea = ext_slabs(sa_ref, CONV_A_W - 1, a_glu)
    for l in range(n_step):
        acc = jnp.zeros((n_seq, dq), F32)
        for k in range(CONV_A_W):
            acc = acc + ea[l + k] * caw_ref[k:k + 1, :]
        y = _silu(_ln(acc + cab_ref[...], lag_ref[...], lab_ref[...]))
        cat_ref[l * n_seq:(l + 1) * n_seq, 0:dq] = y.astype(BF16)
    store_state(sa_out, ea, CONV_A_W - 1)

    u = _gelu(proj(2))
    v = _ln(_gelu(proj(3)), lvg_ref[...], lvb_ref[...])
    v_out[...] = v
    for l in range(n_step):
        s = jnp.zeros((n_seq, dq), F32) + bsv_ref[l:l + 1, :]
        for m in range(l + 1):
            s = s + slab(v, m) * wsv_ref[l * n_step + m:l * n_step + m + 1, :]
        cat_ref[l * n_seq:(l + 1) * n_seq, dq:2 * dq] = (slab(u, l) * s).astype(BF16)

    c_x = proj(4)
    c_b = proj(5)
    c_c = proj(6)
    gx = c_c * c_x
    ec = ext_slabs(sc_ref, CONV_C_W - 1, gx)
    for l in range(n_step):
        acc = jnp.zeros((n_seq, dq), F32)
        for k in range(CONV_C_W):
            acc = acc + ec[l + k] * ccw_ref[k:k + 1, :]
        cat_ref[l * n_seq:(l + 1) * n_seq, 2 * dq:3 * dq] = (slab(c_b, l) * acc).astype(BF16)
    store_state(sc_out, ec, CONV_C_W - 1)

    d_in = proj(7)
    ed = ext_slabs(sd_ref, POOL_PAST, d_in)
    memo = {}

    def wsum(i, w):
        if i < 0:
            return None
        if w == 1:
            return ed[i]
        if (i, w) not in memo:
            a, b = wsum(i, w // 2), wsum(i - w // 2, w // 2)
            memo[(i, w)] = a if b is None else a + b
        return memo[(i, w)]

    gc = dq // len(POOL_WINDOWS)
    lane_grp = lax.broadcasted_iota(I32, (n_seq, dq), 1) // gc
    for l in range(n_step):
        mean = jnp.zeros((n_seq, dq), F32)
        for g, w in enumerate(POOL_WINDOWS):
            count = min(PAST_LEN + l + 1, w)
            mean = jnp.where(lane_grp == g, wsum(POOL_PAST + l, w) * (1.0 / count), mean)
        dd = (mean - slab(d_in, l)).astype(BF16)
        cat_ref[l * n_seq:(l + 1) * n_seq, 3 * dq:4 * dq] = (
            _dot(dd, wpool_ref[...]) * pscale_ref[...]).astype(BF16)
    store_state(sd_out, ed, POOL_PAST)

    mix = _dot(cat_ref[...], wout_ref[...])
    h_out[...] = _ln(alpha * h + mix, lmg_ref[...], lmb_ref[...])


def _mixer_sample(x_tm, sa, sc, sd, lw, first_layer, alpha, n_seq, n_step):
    rows, d = x_tm.shape
    d_mix = lw['w_out'].shape[0]
    dq = d_mix // 4
    assert n_seq % SUBLANES == 0 and n_step <= CHUNK and PAST_LEN % CHUNK == 0
    ins = [x_tm, sa, sc, sd, lw['ln_in_g'], lw['ln_in_b'], lw['w_in'], lw['conv_a_w'], lw['conv_a_b'],
           lw['ln_a_g'], lw['ln_a_b'], lw['ln_v_g'], lw['ln_v_b'], lw['ws_vec'], lw['bs_vec'],
           lw['conv_c_w'], lw['w_pool_bd'], lw['pool_scale'], lw['w_out'], lw['ln_mix_g'], lw['ln_mix_b']]
    out_shape = (
        jax.ShapeDtypeStruct((rows, d), F32),
        jax.ShapeDtypeStruct(((CONV_A_W - 1) * n_seq, dq), F32),
        jax.ShapeDtypeStruct(((CONV_C_W - 1) * n_seq, dq), F32),
        jax.ShapeDtypeStruct((POOL_PAST * n_seq, dq), F32),
        jax.ShapeDtypeStruct((rows, dq), F32),
    )
    return pl.pallas_call(
        functools.partial(_mixer_sample_kernel, first_layer, alpha, n_seq, n_step, d_mix),
        grid=(1,),
        in_specs=[_full_spec(a) for a in ins],
        out_specs=tuple(pl.BlockSpec(s.shape, lambda i: (0, 0)) for s in out_shape),
        out_shape=out_shape,
        scratch_shapes=[pltpu.VMEM((rows, d_mix), BF16)],
        compiler_params=_cparams(("arbitrary",)),
        name="mixer_sample",
    )(*ins)


def _ffn_dense_kernel(alpha, x_ref, wg_ref, wu_ref, wd_ref, g_ref, b_ref, o_ref, acc_ref):
    f = pl.program_id(1)

    @pl.when(f == 0)
    def _():
        acc_ref[...] = jnp.zeros_like(acc_ref)

    xb = x_ref[...].astype(BF16)
    mid = _silu(_dot(xb, wg_ref[...])) * _dot(xb, wu_ref[...])
    acc_ref[...] += _dot(mid.astype(BF16), wd_ref[...])

    @pl.when(f == pl.num_programs(1) - 1)
    def _():
        o_ref[...] = _ln(alpha * x_ref[...] + acc_ref[...], g_ref[...], b_ref[...])


def _ffn_dense(x, wg, wu, wd, g, b, alpha, tm, tf):
    t, d = x.shape
    dff = wg.shape[1]
    assert t % tm == 0 and dff % tf == 0
    return pl.pallas_call(
        functools.partial(_ffn_dense_kernel, alpha),
        grid=(t // tm, dff // tf),
        in_specs=[pl.BlockSpec((tm, d), lambda i, f: (i, 0)),
                  pl.BlockSpec((d, tf), lambda i, f: (0, f)),
                  pl.BlockSpec((d, tf), lambda i, f: (0, f)),
                  pl.BlockSpec((tf, d), lambda i, f: (f, 0)),
                  pl.BlockSpec((1, d), lambda i, f: (0, 0)),
                  pl.BlockSpec((1, d), lambda i, f: (0, 0))],
        out_specs=pl.BlockSpec((tm, d), lambda i, f: (i, 0)),
        out_shape=jax.ShapeDtypeStruct((t, d), F32),
        scratch_shapes=[pltpu.VMEM((tm, d), F32)],
        compiler_params=_cparams(("arbitrary", "arbitrary")),
        name="ffn_dense",
    )(x, wg, wu, wd, g, b)


def _router_kernel(tm, h_ref, wcat_ref, cin_ref, mi_ref, mf_ref, cnt_ref, carry_ref, before_ref):
    i = pl.program_id(0)

    @pl.when(i == 0)
    def _():
        carry_ref[...] = cin_ref[...].astype(F32)
        row = lax.broadcasted_iota(I32, (tm, tm), 0)
        col = lax.broadcasted_iota(I32, (tm, tm), 1)
        before_ref[...] = jnp.where(row < col, 1.0, 0.0).astype(BF16)

    h = h_ref[...]
    h_hi = h.astype(BF16)
    h_lo = (h - h_hi.astype(F32)).astype(BF16)
    p_hi = _dot(h_hi, wcat_ref[...])
    p_lo = _dot(h_lo, wcat_ref[...])
    logits = (p_hi[:, 0:LANES] + (p_lo[:, 0:LANES] + p_hi[:, LANES:2 * LANES])) + p_lo[:, LANES:2 * LANES]
    lg = logits.T[0:N_EXPERTS, :]
    ex = lax.broadcasted_iota(I32, (N_EXPERTS, tm), 0).astype(F32)
    m1 = jnp.max(lg, axis=0, keepdims=True)
    i1 = jnp.min(jnp.where(lg == m1, ex, float(N_EXPERTS)), axis=0, keepdims=True)
    sel1 = ex == i1
    rest = jnp.where(sel1, -jnp.inf, lg)
    m2 = jnp.max(rest, axis=0, keepdims=True)
    i2 = jnp.min(jnp.where(rest == m2, ex, float(N_EXPERTS)), axis=0, keepdims=True)
    sel2 = ex == i2
    e2 = jnp.exp(m2 - m1)
    den = 1.0 + e2
    g1 = 1.0 / den
    g2 = e2 / den

    sel = jnp.where(sel1 | sel2, 1.0, 0.0)
    sel_pad = jnp.concatenate([sel, jnp.zeros_like(sel)], axis=0).astype(BF16)
    base = _dot(sel_pad, before_ref[...])[0:N_EXPERTS, :] + carry_ref[:, 0:1]
    r1 = jnp.sum(jnp.where(sel1, base, 0.0), axis=0, keepdims=True)
    r2 = jnp.sum(jnp.where(sel2, base, 0.0), axis=0, keepdims=True)
    carry_ref[...] = carry_ref[...] + jnp.sum(sel, axis=1, keepdims=True)

    meta = jnp.where(ex == 0, i1, jnp.where(ex == 1, i2, jnp.where(ex == 2, r1, r2)))
    mi_ref[...] = meta.astype(I32)
    gt = jnp.where(ex == 0, g1, jnp.where(ex == 1, g2, 0.0))
    gt = jnp.concatenate([gt, jnp.zeros((LANES - N_EXPERTS, tm), F32)], axis=0)
    mf_ref[...] = gt.T[:, 0:SUBLANES]
    cnt_ref[...] = carry_ref[...].astype(I32)


def _router(h, wr_cat, counts_in):
    t, d = h.shape
    tm = min(TM_ROUTE, t)
    assert t % tm == 0 and N_EXPERTS == SUBLANES
    return pl.pallas_call(
        functools.partial(_router_kernel, tm),
        grid=(t // tm,),
        in_specs=[pl.BlockSpec((tm, d), lambda i: (i, 0)), _full_spec(wr_cat), _full_spec(counts_in)],
        out_specs=(pl.BlockSpec((SUBLANES, tm), lambda i: (0, i)),
                   pl.BlockSpec((tm, SUBLANES), lambda i: (i, 0)),
                   pl.BlockSpec((SUBLANES, LANES), lambda i: (0, 0))),
        out_shape=(jax.ShapeDtypeStruct((SUBLANES, t), I32),
                   jax.ShapeDtypeStruct((t, SUBLANES), F32),
                   jax.ShapeDtypeStruct((SUBLANES, LANES), I32)),
        scratch_shapes=[pltpu.VMEM((SUBLANES, LANES), F32), pltpu.VMEM((tm, tm), BF16)],
        compiler_params=_cparams(("arbitrary",)),
        name="router",
    )(h, wr_cat, counts_in)


def _row_copy(src, src_row, dst, dst_row, sem):
    return pltpu.make_async_copy(src.at[pl.ds(src_row, 1), :], dst.at[pl.ds(dst_row, 1), :], sem)


def _rows_wait(src, dst, n_rows, sem):
    pltpu.make_async_copy(src.at[pl.ds(0, n_rows), :], dst.at[pl.ds(0, n_rows), :], sem).wait()


def _dispatch_kernel(tm, tm_expert, n_blocks, blk_ranges, zi_ref, dest_ref, *rest):
    n_src = len(blk_ranges)
    h_refs = rest[:n_src]
    xs_ref, xbuf, zblk, lsems, rsems, zsem = rest[n_src:]
    i = pl.program_id(0)
    n = pl.num_programs(0)

    def block_load(g, start):
        slot = lax.rem(g, N_STAGE)
        for h_ref, (b0, b1) in zip(h_refs, blk_ranges):
            @pl.when((g >= b0) & (g < b1))
            def _(h_ref=h_ref, b0=b0):
                cp = pltpu.make_async_copy(h_ref.at[pl.ds((g - b0) * tm, tm), :], xbuf.at[slot],
                                           lsems.at[slot])
                if start:
                    cp.start()
                else:
                    cp.wait()

    def rows_wait(g):
        slot = lax.rem(g, N_STAGE)
        for _k in range(TOP_K):
            _rows_wait(xbuf.at[slot], xs_ref, tm, rsems.at[slot])

    @pl.when(i == 0)
    def _():
        block_load(i, True)

    @pl.when(i >= N_STAGE - 1)
    def _():
        rows_wait(i - (N_STAGE - 1))

    @pl.when(i + 1 < n)
    def _():
        block_load(i + 1, True)

    block_load(i, False)
    cur = lax.rem(i, N_STAGE)

    def issue(j, carry):
        r0 = pl.multiple_of(j * SUBLANES, SUBLANES)
        for u in range(SUBLANES):
            for k in range(TOP_K):
                dst = dest_ref[0, 0, j * SUBLANES + (k * tm + u)]
                _row_copy(xbuf.at[cur], r0 + u, xs_ref, dst, rsems.at[cur]).start(priority=k)
        return carry

    lax.fori_loop(0, tm // SUBLANES, issue, 0)

    @pl.when(i == n - 1)
    def _():
        for back in range(N_STAGE - 2, -1, -1):
            @pl.when(i >= back)
            def _(back=back):
                rows_wait(i - back)

    @pl.when(i == 0)
    def _():
        zblk[...] = jnp.zeros_like(zblk)
        for e in range(N_EXPERTS):
            lo = zi_ref[e] + zi_ref[N_EXPERTS + e]
            hi = zi_ref[e] + zi_ref[2 * N_EXPERTS + e]

            def zissue(r, carry):
                _row_copy(zblk, 0, xs_ref, r, zsem).start()
                return carry

            def zdrain(r, carry):
                _row_copy(zblk, 0, xs_ref, r, zsem).wait()
                return carry

            lax.fori_loop(lo, hi, zissue, 0)
            lax.fori_loop(lo, hi, zdrain, 0)

        def bcopy(b):
            return pltpu.make_async_copy(zblk, xs_ref.at[pl.ds(b * tm_expert, tm_expert), :], zsem)

        def bissue(b, carry):
            bcopy(b).start()
            return carry

        def bdrain(b, carry):
            bcopy(b).wait()
            return carry

        lax.fori_loop(zi_ref[3 * N_EXPERTS], n_blocks, bissue, 0)
        lax.fori_loop(zi_ref[3 * N_EXPERTS], n_blocks, bdrain, 0)


def _dispatch(hs_list, dest_blk, zinfo, n_blocks, tm_expert):
    d = hs_list[0].shape[1]
    nb = dest_blk.shape[0]
    tm = dest_blk.shape[2] // TOP_K
    blk_ranges, b0 = [], 0
    for h in hs_list:
        assert h.shape[0] % tm == 0 and h.shape[0] >= tm
        blk_ranges.append((b0, b0 + h.shape[0] // tm))
        b0 = blk_ranges[-1][1]
    assert b0 == nb
    any_spec = pl.BlockSpec(memory_space=pl.ANY)
    return pl.pallas_call(
        functools.partial(_dispatch_kernel, tm, tm_expert, n_blocks, tuple(blk_ranges)),
        grid_spec=pltpu.PrefetchScalarGridSpec(
            num_scalar_prefetch=1,
            grid=(nb,),
            in_specs=[pl.BlockSpec((1, 1, TOP_K * tm), lambda i, zi: (i, 0, 0), memory_space=pltpu.SMEM)]
            + [any_spec] * len(hs_list),
            out_specs=any_spec,
            scratch_shapes=[pltpu.VMEM((N_STAGE, tm, d), F32), pltpu.VMEM((tm_expert, d), F32),
                            pltpu.SemaphoreType.DMA((N_STAGE,)), pltpu.SemaphoreType.DMA((N_STAGE,)),
                            pltpu.SemaphoreType.DMA(())],
        ),
        out_shape=jax.ShapeDtypeStruct((n_blocks * tm_expert, d), F32),
        compiler_params=_cparams(("arbitrary",)),
        name="moe_dispatch",
    )(zinfo, dest_blk, *hs_list)


def _ffn_moe_kernel(be_ref, nu_ref, x_ref, wg_ref, wu_ref, wd_ref, y_ref, acc_ref):
    b = pl.program_id(0)
    f = pl.program_id(1)

    @pl.when(b < nu_ref[0])
    def _():
        @pl.when(f == 0)
        def _():
            acc_ref[...] = jnp.zeros_like(acc_ref)

        xb = x_ref[...].astype(BF16)
        mid = _silu(_dot(xb, wg_ref[0])) * _dot(xb, wu_ref[0])
        acc_ref[...] += _dot(mid.astype(BF16), wd_ref[0])

        @pl.when(f == pl.num_programs(1) - 1)
        def _():
            y_ref[...] = acc_ref[...]

    @pl.when((b >= nu_ref[0]) & (f == pl.num_programs(1) - 1))
    def _():
        y_ref[...] = jnp.zeros_like(y_ref)


def _ffn_moe(xs, wg, wu, wd, block_e, n_used, tm, tf):
    cap, d = xs.shape
    dff = wg.shape[2]
    nf = dff // tf
    assert cap % tm == 0 and dff % tf == 0

    def blk(b, nu):
        return jnp.minimum(b, nu[0] - 1)

    def fcol(b, f, nu):
        return jnp.where(b < nu[0], f, nf - 1)

    return pl.pallas_call(
        _ffn_moe_kernel,
        grid_spec=pltpu.PrefetchScalarGridSpec(
            num_scalar_prefetch=2,
            grid=(cap // tm, nf),
            in_specs=[pl.BlockSpec((tm, d), lambda b, f, be, nu: (blk(b, nu), 0)),
                      pl.BlockSpec((1, d, tf), lambda b, f, be, nu: (be[b], 0, fcol(b, f, nu))),
                      pl.BlockSpec((1, d, tf), lambda b, f, be, nu: (be[b], 0, fcol(b, f, nu))),
                      pl.BlockSpec((1, tf, d), lambda b, f, be, nu: (be[b], fcol(b, f, nu), 0))],
            out_specs=pl.BlockSpec((tm, d), lambda b, f, be, nu: (b, 0)),
            scratch_shapes=[pltpu.VMEM((tm, d), F32)],
        ),
        out_shape=jax.ShapeDtypeStruct((cap, d), F32),
        compiler_params=_cparams(("arbitrary", "arbitrary")),
        name="ffn_moe",
    )(block_e, n_used, xs, wg, wu, wd)


def _combine_kernel(alpha, tm, dcur_ref, dnext_ref, h_ref, gate_ref, g_ref, b_ref, ys_ref, o_ref, ybuf, sems):
    i = pl.program_id(0)
    n = pl.num_programs(0)

    def issue(dref, slot):
        def body(j, carry):
            r0 = pl.multiple_of(j * SUBLANES, SUBLANES)
            for u in range(SUBLANES):
                for k in range(TOP_K):
                    src = dref[0, 0, j * SUBLANES + (k * tm + u)]
                    _row_copy(ys_ref, src, ybuf.at[slot, k], r0 + u, sems.at[slot]).start(priority=k)
            return carry

        lax.fori_loop(0, tm // SUBLANES, body, 0)

    @pl.when(i == 0)
    def _():
        issue(dcur_ref, 0)

    @pl.when(i + 1 < n)
    def _():
        issue(dnext_ref, (i + 1) % 2)

    slot = i % 2
    for k in range(TOP_K):
        _rows_wait(ys_ref, ybuf.at[slot, k], tm, sems.at[slot])

    gates = gate_ref[...]
    ffn = gates[:, 0:1] * ybuf[slot, 0] + gates[:, 1:2] * ybuf[slot, 1]
    o_ref[...] = _ln(alpha * h_ref[...] + ffn, g_ref[...], b_ref[...])


def _combine(h, dest_blk, gates, ys, g, b, alpha):
    t, d = h.shape
    nb = dest_blk.shape[0]
    tm = dest_blk.shape[2] // TOP_K
    dest_spec = lambda imap: pl.BlockSpec((1, 1, TOP_K * tm), imap, memory_space=pltpu.SMEM)
    return pl.pallas_call(
        functools.partial(_combine_kernel, alpha, tm),
        grid=(nb,),
        in_specs=[dest_spec(lambda i: (i, 0, 0)),
                  dest_spec(lambda i: (jnp.minimum(i + 1, nb - 1), 0, 0)),
                  pl.BlockSpec((tm, d), lambda i: (i, 0)),
                  pl.BlockSpec((tm, SUBLANES), lambda i: (i, 0)),
                  pl.BlockSpec((1, d), lambda i: (0, 0)),
                  pl.BlockSpec((1, d), lambda i: (0, 0)),
                  pl.BlockSpec(memory_space=pl.ANY)],
        out_specs=pl.BlockSpec((tm, d), lambda i: (i, 0)),
        out_shape=jax.ShapeDtypeStruct((t, d), F32),
        scratch_shapes=[pltpu.VMEM((2, TOP_K, tm, d), F32), pltpu.SemaphoreType.DMA((2,))],
        compiler_params=_cparams(("arbitrary",)),
        name="moe_combine",
    )(dest_blk, dest_blk, h, gates, g, b, ys)


def _ffn_routed(hs_list, mw, g, b, alpha, tm_expert):
    d = hs_list[0].shape[1]
    counts_in = jnp.zeros((SUBLANES, LANES), I32)
    routed = []
    for h in hs_list:
        meta_i, gates, counts_in = _router(h, mw['w_router_cat'], counts_in)
        routed.append((meta_i, gates))
    counts = counts_in[:N_EXPERTS, 0]
    padded = (counts + tm_expert - 1) // tm_expert * tm_expert
    pad_end = jnp.cumsum(padded)
    pad_start = (pad_end - padded).astype(I32)
    n_assign = sum(h.shape[0] for h in hs_list) * TOP_K
    n_blocks = -(-n_assign // tm_expert) + N_EXPERTS
    cap = n_blocks * tm_expert
    n_used = (pad_end[-1] // tm_expert).astype(I32)
    blk_start = jnp.minimum(jnp.arange(n_blocks, dtype=I32), n_used - 1) * tm_expert
    block_e = jnp.minimum(jnp.sum(blk_start[:, None] >= pad_end[None, :], axis=1), N_EXPERTS - 1).astype(I32)
    zinfo = jnp.concatenate([pad_start, counts, padded, n_used.reshape(1)]).astype(I32)

    dests = []
    for h, (meta_i, _) in zip(hs_list, routed):
        dest = meta_i[TOP_K:2 * TOP_K, :]
        for e in range(N_EXPERTS):
            dest = dest + jnp.where(meta_i[0:TOP_K, :] == e, pad_start[e], 0)
        nb = h.shape[0] // TM_MOVE
        dests.append(dest.reshape(TOP_K, nb, TM_MOVE).transpose(1, 0, 2).reshape(nb, 1, TOP_K * TM_MOVE))
    xs = _dispatch(hs_list, jnp.concatenate(dests, axis=0), zinfo, n_blocks, tm_expert)
    ys = _ffn_moe(xs, mw['w_e_gate'], mw['w_e_up'], mw['w_e_down'], block_e, n_used.reshape(1), tm_expert,
                  TF_EXPERT)
    return [_combine(h, dest_blk, gates, ys, g, b, alpha)
            for h, dest_blk, (_, gates) in zip(hs_list, dests, routed)]


def _prep_layer(i, p, n_step):
    dq = p['conv_a_w'].shape[2]
    hd = dq // N_HEADS_B
    tril = jnp.tril(jnp.ones((CHUNK, CHUNK), dtype=bool))
    w_s = p['w_s'][i]
    b_s = p['b_s'][i]
    wp = p['w_pool'][i]
    ng, gc = wp.shape[0], wp.shape[1]
    w_pool_bd = (jnp.eye(ng, dtype=F32)[:, None, :, None] * wp[:, :, None, :]).reshape(ng * gc, ng * gc)
    row = lambda a: a.reshape(1, -1)
    return dict(
        ln_in_g=row(p['ln_in_g']), ln_in_b=row(p['ln_in_b']),
        w_in=p['w_in'][i].astype(BF16),
        conv_a_w=p['conv_a_w'][i], conv_a_b=row(p['conv_a_b'][i]),
        ln_a_g=row(p['ln_a_g'][i]), ln_a_b=row(p['ln_a_b'][i]),
        ln_v_g=row(p['ln_v_g'][i]), ln_v_b=row(p['ln_v_b'][i]),
        ws_tril=jnp.where(tril[None], w_s, 0).astype(BF16),
        bs_mat=jnp.repeat(b_s.T, hd, axis=1),
        ws_vec=jnp.repeat(jnp.where(tril[None], w_s, 0)[:, :n_step, :n_step].transpose(1, 2, 0)
                          .reshape(n_step * n_step, N_HEADS_B), hd, axis=1),
        bs_vec=jnp.repeat(b_s[:, :n_step].T, hd, axis=1),
        conv_c_w=p['conv_c_w'][i],
        w_pool_bd=w_pool_bd.astype(BF16), pool_scale=row(p['pool_scale'][i]),
        w_out=p['w_out'][i].astype(BF16),
        ln_mix_g=row(p['ln_mix_g'][i]), ln_mix_b=row(p['ln_mix_b'][i]),
    )


def kernel(x_prompt, x_sample, state_conv_a, state_conv_c, state_pool_d, ln_in_g, ln_in_b, w_in, conv_a_w,
           conv_a_b, ln_a_g, ln_a_b, ln_v_g, ln_v_b, w_s, b_s, conv_c_w, w_pool, pool_scale, w_out,
           ln_mix_g, ln_mix_b, w_ff_gate, w_ff_up, w_ff_down, w_router, w_e_gate, w_e_up, w_e_down,
           ln_ffn_g, ln_ffn_b):
    p = dict(ln_in_g=ln_in_g, ln_in_b=ln_in_b, w_in=w_in, conv_a_w=conv_a_w, conv_a_b=conv_a_b,
             ln_a_g=ln_a_g, ln_a_b=ln_a_b, ln_v_g=ln_v_g, ln_v_b=ln_v_b, w_s=w_s, b_s=b_s,
             conv_c_w=conv_c_w, w_pool=w_pool, pool_scale=pool_scale, w_out=w_out,
             ln_mix_g=ln_mix_g, ln_mix_b=ln_mix_b)
    depth = w_in.shape[0]
    bsz, seq, d = x_prompt.shape
    n_seq, n_step, _ = x_sample.shape
    dq = conv_a_w.shape[2]
    alpha = float((2.0 * depth) ** 0.25)

    hp = x_prompt
    hs = x_sample.transpose(1, 0, 2).reshape(n_step * n_seq, d)
    outs = {k: [] for k in ('pa', 'pc', 'pd', 'pv', 'sa', 'sc', 'sd', 'sv')}
    for i in range(depth):
        lw = _prep_layer(i, p, n_step)
        hp, pa, pc, pd, pv = _mixer_prompt(hp, lw, i == 0, alpha)
        tm_state = lambda s: s.transpose(1, 0, 2).reshape(-1, dq)
        hs, sa, sc, sd, sv = _mixer_sample(hs, tm_state(state_conv_a[i]), tm_state(state_conv_c[i]),
                                           tm_state(state_pool_d[i]), lw, i == 0, alpha, n_seq, n_step)
        lg, lb = ln_ffn_g[i].reshape(1, -1), ln_ffn_b[i].reshape(1, -1)
        j = i // 2
        if i % 2 == 0:
            wg, wu, wd = w_ff_gate[j].astype(BF16), w_ff_up[j].astype(BF16), w_ff_down[j].astype(BF16)
            hp = _ffn_dense(hp.reshape(bsz * seq, d), wg, wu, wd, lg, lb, alpha, TM_DENSE,
                            TF_DENSE).reshape(bsz, seq, d)
            hs = _ffn_dense(hs, wg, wu, wd, lg, lb, alpha, hs.shape[0], TF_DENSE)
        else:
            wr = jnp.pad(w_router[j], ((0, 0), (0, LANES - N_EXPERTS)))
            wr_hi = wr.astype(BF16)
            wr_lo = (wr - wr_hi.astype(F32)).astype(BF16)
            mw = dict(w_router_cat=jnp.concatenate([wr_hi, wr_lo], axis=1),
                      w_e_gate=w_e_gate[j].astype(BF16), w_e_up=w_e_up[j].astype(BF16),
                      w_e_down=w_e_down[j].astype(BF16))
            hp, hs = _ffn_routed([hp.reshape(bsz * seq, d), hs], mw, lg, lb, alpha, TM_EXPERT)
            hp = hp.reshape(bsz, seq, d)
        back = lambda a: a.reshape(-1, n_seq, dq).transpose(1, 0, 2)
        for k, val in zip(('pa', 'pc', 'pd', 'pv', 'sa', 'sc', 'sd', 'sv'),
                          (pa, pc, pd, pv, back(sa), back(sc), back(sd), back(sv))):
            outs[k].append(val)
    y_sample = hs.reshape(n_step, n_seq, d).transpose(1, 0, 2)
    st = lambda k: jnp.stack(outs[k])
    return (hp, y_sample, st('pa'), st('pc'), st('pd'), st('pv'), st('sa'), st('sc'), st('sd'), st('sv'))
```

```python
import functools

import numpy as np
import jax
import jax.numpy as jnp
from jax import lax
from jax.experimental import pallas as pl
from jax.experimental.pallas import tpu as pltpu

F32 = jnp.float32
BF16 = jnp.bfloat16
I32 = jnp.int32

PAST_LEN = 16384
CHUNK = 128
N_HEADS_B = 4
POOL_WINDOWS = (2, 4, 8, 16)
POOL_PAST = 15
CONV_A_W = 31
CONV_C_W = 3
N_EXPERTS = 8
TOP_K = 2
LN_EPS = 1e-5
SQRT_HALF = float(np.sqrt(0.5).astype(np.float32))

LANES = 128
SUBLANES = 8
VMEM_LIMIT_BYTES = 56 * 1024 * 1024

HIST_A = 32
HIST_C = 8
HIST_D = 24

TL_PROMPT = 512
TM_DENSE = 512
TF_DENSE = 1792
TF_EXPERT = 1792
TM_ROUTE = 512
TM_MOVE = 256
TM_EXPERT = 512
N_STAGE = 3


def _ln(x, g, b):
    mu = jnp.mean(x, axis=-1, keepdims=True)
    xc = x - mu
    var = jnp.mean(xc * xc, axis=-1, keepdims=True)
    return xc * lax.rsqrt(var + LN_EPS) * g + b


def _gelu(x):
    return 0.5 * x * (1.0 + lax.erf(x * SQRT_HALF))


def _silu(x):
    return x * jax.nn.sigmoid(x)


def _dot(a, b):
    return jnp.dot(a, b, preferred_element_type=F32)


def _cparams(sem):
    return pltpu.CompilerParams(dimension_semantics=sem, vmem_limit_bytes=VMEM_LIMIT_BYTES)


def _mixer_prompt_kernel(first_layer, alpha, tl, d_mix,
                         x_ref, lng_ref, lnb_ref, win_ref, caw_ref, cab_ref, lag_ref, lab_ref,
                         lvg_ref, lvb_ref, ws_ref, bsm_ref, ccw_ref, wpool_ref, pscale_ref,
                         wout_ref, lmg_ref, lmb_ref,
                         h_out, sa_out, sc_out, sd_out, v_out,
                         ext_a, ext_c, ext_d, cat_ref, z_ref, hb_ref, mix_ref):
    dq = d_mix // 4
    j = pl.program_id(1)
    nj = pl.num_programs(1)

    @pl.when(j == 0)
    def _():
        ext_a[0, 0:HIST_A, :] = jnp.zeros((HIST_A, dq), F32)
        ext_c[0:HIST_C, :] = jnp.zeros((HIST_C, dq), F32)
        ext_d[0:HIST_D, :] = jnp.zeros((HIST_D, dq), F32)

    d = x_ref.shape[2]
    rb = CHUNK

    def resid(r0, n):
        return h_out[0, r0:r0 + n, :] if first_layer else x_ref[0, r0:r0 + n, :]

    for r0 in range(0, tl, rb):
        x = x_ref[0, r0:r0 + rb, :]
        if first_layer:
            x = _ln(x, lng_ref[...], lnb_ref[...])
            h_out[0, r0:r0 + rb, :] = x
        hb_ref[r0:r0 + rb, :] = x.astype(BF16)

    z_ref[...] = _dot(hb_ref[...], win_ref[...])

    def zcol(r0, i, n=rb):
        return z_ref[r0:r0 + n, i * dq:(i + 1) * dq]

    chunk_rows = range(0, tl, CHUNK)

    for r0 in chunk_rows:
        ext_a[0, HIST_A + r0:HIST_A + r0 + rb, :] = zcol(r0, 0) * jax.nn.sigmoid(zcol(r0, 1))
    n_ext = HIST_A + tl
    for r0 in range(0, n_ext - SUBLANES, rb):
        n = min(rb, n_ext - SUBLANES - r0)
        blk = ext_a[0, r0:r0 + n + SUBLANES, :]
        for s in range(1, SUBLANES):
            ext_a[s, r0:r0 + n, :] = pltpu.roll(blk, n + SUBLANES - s, axis=0)[0:n, :]
    off_a = HIST_A - (CONV_A_W - 1)
    for q0 in range(0, tl, 64):
        acc = jnp.zeros((64, dq), F32)
        for k in range(CONV_A_W):
            s = (off_a + k) % SUBLANES
            row = off_a + k - s + q0
            acc = acc + ext_a[s, row:row + 64, :] * caw_ref[k:k + 1, :]
        y = _silu(_ln(acc + cab_ref[...], lag_ref[...], lab_ref[...]))
        cat_ref[q0:q0 + 64, 0:dq] = y.astype(BF16)

    hd = dq // N_HEADS_B
    lane_head = lax.broadcasted_iota(I32, (CHUNK, dq), 1) // hd
    for r0 in chunk_rows:
        v = _ln(_gelu(zcol(r0, 3)), lvg_ref[...], lvb_ref[...])
        sm = bsm_ref[...]
        for hh in range(N_HEADS_B):
            vm = jnp.where(lane_head == hh, v, 0.0).astype(BF16)
            sm = sm + _dot(ws_ref[hh], vm)
        cat_ref[r0:r0 + CHUNK, dq:2 * dq] = (_gelu(zcol(r0, 2)) * sm).astype(BF16)
        if r0 == tl - CHUNK:
            @pl.when(j == nj - 1)
            def _(v=v):
                v_out[0] = v

    off_c = HIST_C - (CONV_C_W - 1)
    for r0 in chunk_rows:
        ext_c[HIST_C + r0:HIST_C + r0 + rb, :] = zcol(r0, 6) * zcol(r0, 4)
    for r0 in chunk_rows:
        conv_c = jnp.zeros((CHUNK, dq), F32)
        for k in range(CONV_C_W):
            conv_c = conv_c + ext_c[off_c + r0 + k:off_c + r0 + k + CHUNK, :] * ccw_ref[k:k + 1, :]
        cat_ref[r0:r0 + CHUNK, 2 * dq:3 * dq] = (zcol(r0, 5) * conv_c).astype(BF16)

    gc = dq // len(POOL_WINDOWS)
    lane_grp = lax.broadcasted_iota(I32, (CHUNK, dq), 1) // gc
    lead = 2 * SUBLANES
    for r0 in chunk_rows:
        ext_d[HIST_D + r0:HIST_D + r0 + rb, :] = zcol(r0, 7)
    for r0 in chunk_rows:
        e = ext_d[HIST_D + r0 - lead:HIST_D + r0 + CHUNK, :]
        s2 = e + pltpu.roll(e, 1, axis=0)
        s4 = s2 + pltpu.roll(s2, 2, axis=0)
        s8 = s4 + pltpu.roll(s4, 4, axis=0)
        s16 = s8 + pltpu.roll(s8, 8, axis=0)
        sums = (s2, s4, s8, s16)
        pos = (j * tl + r0 + lax.broadcasted_iota(I32, (CHUNK, 1), 0)).astype(F32)
        mean = jnp.zeros((CHUNK, dq), F32)
        for g, w in enumerate(POOL_WINDOWS):
            inv = 1.0 / jnp.minimum(pos + 1.0, float(w))
            mean = jnp.where(lane_grp == g, sums[g][lead:lead + CHUNK, :] * inv, mean)
        dd = (mean - e[lead:lead + CHUNK, :]).astype(BF16)
        cat_ref[r0:r0 + CHUNK, 3 * dq:4 * dq] = (_dot(dd, wpool_ref[...]) * pscale_ref[...]).astype(BF16)

    mix_ref[...] = _dot(cat_ref[...], wout_ref[...])
    for r0 in range(0, tl, 64):
        h_out[0, r0:r0 + 64, :] = _ln(alpha * resid(r0, 64) + mix_ref[r0:r0 + 64, :], lmg_ref[...], lmb_ref[...])

    @pl.when(j == nj - 1)
    def _():
        sa_out[0] = ext_a[0, HIST_A + tl - (CONV_A_W - 1):HIST_A + tl, :]
        sc_out[0] = ext_c[HIST_C + tl - (CONV_C_W - 1):HIST_C + tl, :]
        sd_out[0] = ext_d[HIST_D + tl - POOL_PAST:HIST_D + tl, :]

    ext_a[0, 0:HIST_A, :] = ext_a[0, tl:tl + HIST_A, :]
    ext_c[0:HIST_C, :] = ext_c[tl:tl + HIST_C, :]
    ext_d[0:HIST_D, :] = ext_d[tl:tl + HIST_D, :]


def _full_spec(arr):
    nd = arr.ndim
    return pl.BlockSpec(arr.shape, lambda *_: (0,) * nd)


def _mixer_prompt(x, lw, first_layer, alpha):
    bsz, seq, d = x.shape
    d_mix = lw['w_out'].shape[0]
    dq = d_mix // 4
    tl = TL_PROMPT
    assert seq % tl == 0 and tl % CHUNK == 0 and seq >= CHUNK
    params = [lw['ln_in_g'], lw['ln_in_b'], lw['w_in'], lw['conv_a_w'], lw['conv_a_b'], lw['ln_a_g'],
              lw['ln_a_b'], lw['ln_v_g'], lw['ln_v_b'], lw['ws_tril'], lw['bs_mat'], lw['conv_c_w'],
              lw['w_pool_bd'], lw['pool_scale'], lw['w_out'], lw['ln_mix_g'], lw['ln_mix_b']]
    out_shape = (
        jax.ShapeDtypeStruct((bsz, seq, d), F32),
        jax.ShapeDtypeStruct((bsz, CONV_A_W - 1, dq), F32),
        jax.ShapeDtypeStruct((bsz, CONV_C_W - 1, dq), F32),
        jax.ShapeDtypeStruct((bsz, POOL_PAST, dq), F32),
        jax.ShapeDtypeStruct((bsz, CHUNK, dq), F32),
    )
    state_spec = lambda rows: pl.BlockSpec((1, rows, dq), lambda b, j: (b, 0, 0))
    return pl.pallas_call(
        functools.partial(_mixer_prompt_kernel, first_layer, alpha, tl, d_mix),
        grid=(bsz, seq // tl),
        in_specs=[pl.BlockSpec((1, tl, d), lambda b, j: (b, j, 0))] + [_full_spec(p) for p in params],
        out_specs=(pl.BlockSpec((1, tl, d), lambda b, j: (b, j, 0)),
                   state_spec(CONV_A_W - 1), state_spec(CONV_C_W - 1), state_spec(POOL_PAST),
                   state_spec(CHUNK)),
        out_shape=out_shape,
        scratch_shapes=[pltpu.VMEM((SUBLANES, HIST_A + tl, dq), F32), pltpu.VMEM((HIST_C + tl, dq), F32),
                        pltpu.VMEM((HIST_D + tl, dq), F32), pltpu.VMEM((tl, d_mix), BF16),
                        pltpu.VMEM((tl, lw['w_in'].shape[1]), F32), pltpu.VMEM((tl, d), BF16),
                        pltpu.VMEM((tl, d), F32)],
        compiler_params=_cparams(("arbitrary", "arbitrary")),
        name="mixer_prompt",
    )(x, *params)


def _mixer_sample_kernel(first_layer, alpha, n_seq, n_step, d_mix,
                         x_ref, sa_ref, sc_ref, sd_ref, lng_ref, lnb_ref, win_ref, caw_ref, cab_ref,
                         lag_ref, lab_ref, lvg_ref, lvb_ref, wsv_ref, bsv_ref, ccw_ref, wpool_ref,
                         pscale_ref, wout_ref, lmg_ref, lmb_ref,
                         h_out, sa_out, sc_out, sd_out, v_out, cat_ref):
    dq = d_mix // 4
    x = x_ref[...]
    h = _ln(x, lng_ref[...], lnb_ref[...]) if first_layer else x
    hb = h.astype(BF16)

    def proj(i):
        return _dot(hb, win_ref[:, i * dq:(i + 1) * dq])

    def slab(val, i):
        return val[i * n_seq:(i + 1) * n_seq, :]

    def ext_slabs(state_ref, n_past, cur):
        return ([state_ref[i * n_seq:(i + 1) * n_seq, :] for i in range(n_past)]
                + [slab(cur, i) for i in range(n_step)])

    def store_state(out_ref, slabs, n_keep):
        for i, sl in enumerate(slabs[len(slabs) - n_keep:]):
            out_ref[i * n_seq:(i + 1) * n_seq, :] = sl

    a_glu = proj(0) * jax.nn.sigmoid(proj(1))
    ea = ext_slabs(sa_ref, CONV_A_W - 1, a_glu)
    for l in range(n_step):
        acc = jnp.zeros((n_seq, dq), F32)
        for k in range(CONV_A_W):
            acc = acc + ea[l + k] * caw_ref[k:k + 1, :]
        y = _silu(_ln(acc + cab_ref[...], lag_ref[...], lab_ref[...]))
        cat_ref[l * n_seq:(l + 1) * n_seq, 0:dq] = y.astype(BF16)
    store_state(sa_out, ea, CONV_A_W - 1)

    u = _gelu(proj(2))
    v = _ln(_gelu(proj(3)), lvg_ref[...], lvb_ref[...])
    v_out[...] = v
    for l in range(n_step):
        s = jnp.zeros((n_seq, dq), F32) + bsv_ref[l:l + 1, :]
        for m in range(l + 1):
            s = s + slab(v, m) * wsv_ref[l * n_step + m:l * n_step + m + 1, :]
        cat_ref[l * n_seq:(l + 1) * n_seq, dq:2 * dq] = (slab(u, l) * s).astype(BF16)

    c_x = proj(4)
    c_b = proj(5)
    c_c = proj(6)
    gx = c_c * c_x
    ec = ext_slabs(sc_ref, CONV_C_W - 1, gx)
    for l in range(n_step):
        acc = jnp.zeros((n_seq, dq), F32)
        for k in range(CONV_C_W):
            acc = acc + ec[l + k] * ccw_ref[k:k + 1, :]
        cat_ref[l * n_seq:(l + 1) * n_seq, 2 * dq:3 * dq] = (slab(c_b, l) * acc).astype(BF16)
    store_state(sc_out, ec, CONV_C_W - 1)

    d_in = proj(7)
    ed = ext_slabs(sd_ref, POOL_PAST, d_in)
    memo = {}

    def wsum(i, w):
        if i < 0:
            return None
        if w == 1:
            return ed[i]
        if (i, w) not in memo:
            a, b = wsum(i, w // 2), wsum(i - w // 2, w // 2)
            memo[(i, w)] = a if b is None else a + b
        return memo[(i, w)]

    gc = dq // len(POOL_WINDOWS)
    lane_grp = lax.broadcasted_iota(I32, (n_seq, dq), 1) // gc
    for l in range(n_step):
        mean = jnp.zeros((n_seq, dq), F32)
        for g, w in enumerate(POOL_WINDOWS):
            count = min(PAST_LEN + l + 1, w)
            mean = jnp.where(lane_grp == g, wsum(POOL_PAST + l, w) * (1.0 / count), mean)
        dd = (mean - slab(d_in, l)).astype(BF16)
        cat_ref[l * n_seq:(l + 1) * n_seq, 3 * dq:4 * dq] = (
            _dot(dd, wpool_ref[...]) * pscale_ref[...]).astype(BF16)
    store_state(sd_out, ed, POOL_PAST)

    mix = _dot(cat_ref[...], wout_ref[...])
    h_out[...] = _ln(alpha * h + mix, lmg_ref[...], lmb_ref[...])


def _mixer_sample(x_tm, sa, sc, sd, lw, first_layer, alpha, n_seq, n_step):
    rows, d = x_tm.shape
    d_mix = lw['w_out'].shape[0]
    dq = d_mix // 4
    assert n_seq % SUBLANES == 0 and n_step <= CHUNK and PAST_LEN % CHUNK == 0
    ins = [x_tm, sa, sc, sd, lw['ln_in_g'], lw['ln_in_b'], lw['w_in'], lw['conv_a_w'], lw['conv_a_b'],
           lw['ln_a_g'], lw['ln_a_b'], lw['ln_v_g'], lw['ln_v_b'], lw['ws_vec'], lw['bs_vec'],
           lw['conv_c_w'], lw['w_pool_bd'], lw['pool_scale'], lw['w_out'], lw['ln_mix_g'], lw['ln_mix_b']]
    out_shape = (
        jax.ShapeDtypeStruct((rows, d), F32),
        jax.ShapeDtypeStruct(((CONV_A_W - 1) * n_seq, dq), F32),
        jax.ShapeDtypeStruct(((CONV_C_W - 1) * n_seq, dq), F32),
        jax.ShapeDtypeStruct((POOL_PAST * n_seq, dq), F32),
        jax.ShapeDtypeStruct((rows, dq), F32),
    )
    return pl.pallas_call(
        functools.partial(_mixer_sample_kernel, first_layer, alpha, n_seq, n_step, d_mix),
        grid=(1,),
        in_specs=[_full_spec(a) for a in ins],
        out_specs=tuple(pl.BlockSpec(s.shape, lambda i: (0, 0)) for s in out_shape),
        out_shape=out_shape,
        scratch_shapes=[pltpu.VMEM((rows, d_mix), BF16)],
        compiler_params=_cparams(("arbitrary",)),
        name="mixer_sample",
    )(*ins)


def _ffn_dense_kernel(alpha, tf, x_ref, wg_hbm, wu_hbm, wd_hbm, g_ref, b_ref, o_ref, wg_ref, wu_ref, wd_ref, sem):
    @pl.when(pl.program_id(0) == 0)
    def _():
        copies = [pltpu.make_async_copy(src, dst, sem.at[n])
                  for n, (src, dst) in enumerate(((wg_hbm, wg_ref), (wu_hbm, wu_ref), (wd_hbm, wd_ref)))]
        for cp in copies:
            cp.start()
        for cp in copies:
            cp.wait()

    x = x_ref[...]
    xb = x.astype(BF16)
    ffn = None
    for c0 in range(0, wg_ref.shape[1], tf):
        mid = _silu(_dot(xb, wg_ref[:, c0:c0 + tf])) * _dot(xb, wu_ref[:, c0:c0 + tf])
        part = _dot(mid.astype(BF16), wd_ref[c0:c0 + tf, :])
        ffn = part if ffn is None else ffn + part
    o_ref[...] = _ln(alpha * x + ffn, g_ref[...], b_ref[...])


def _ffn_dense(x, wg, wu, wd, g, b, alpha, tm, tf):
    t, d = x.shape
    dff = wg.shape[1]
    assert t % tm == 0 and dff % tf == 0
    any_spec = pl.BlockSpec(memory_space=pl.ANY)
    return pl.pallas_call(
        functools.partial(_ffn_dense_kernel, alpha, tf),
        grid=(t // tm,),
        in_specs=[pl.BlockSpec((tm, d), lambda i: (i, 0)), any_spec, any_spec, any_spec,
                  pl.BlockSpec((1, d), lambda i: (0, 0)),
                  pl.BlockSpec((1, d), lambda i: (0, 0))],
        out_specs=pl.BlockSpec((tm, d), lambda i: (i, 0)),
        out_shape=jax.ShapeDtypeStruct((t, d), F32),
        scratch_shapes=[pltpu.VMEM(wg.shape, BF16), pltpu.VMEM(wu.shape, BF16), pltpu.VMEM(wd.shape, BF16),
                        pltpu.SemaphoreType.DMA((3,))],
        compiler_params=_cparams(("arbitrary",)),
        name="ffn_dense",
    )(x, wg, wu, wd, g, b)


def _router_kernel(tm, h_ref, wcat_ref, cin_ref, mi_ref, mf_ref, cnt_ref, carry_ref, before_ref):
    i = pl.program_id(0)

    @pl.when(i == 0)
    def _():
        carry_ref[...] = cin_ref[...].astype(F32)
        row = lax.broadcasted_iota(I32, (tm, tm), 0)
        col = lax.broadcasted_iota(I32, (tm, tm), 1)
        before_ref[...] = jnp.where(row < col, 1.0, 0.0).astype(BF16)

    h = h_ref[...]
    h_hi = h.astype(BF16)
    h_lo = (h - h_hi.astype(F32)).astype(BF16)
    p_hi = _dot(h_hi, wcat_ref[...])
    p_lo = _dot(h_lo, wcat_ref[...])
    logits = (p_hi[:, 0:LANES] + (p_lo[:, 0:LANES] + p_hi[:, LANES:2 * LANES])) + p_lo[:, LANES:2 * LANES]
    lg = logits.T[0:N_EXPERTS, :]
    ex = lax.broadcasted_iota(I32, (N_EXPERTS, tm), 0).astype(F32)
    m1 = jnp.max(lg, axis=0, keepdims=True)
    i1 = jnp.min(jnp.where(lg == m1, ex, float(N_EXPERTS)), axis=0, keepdims=True)
    sel1 = ex == i1
    rest = jnp.where(sel1, -jnp.inf, lg)
    m2 = jnp.max(rest, axis=0, keepdims=True)
    i2 = jnp.min(jnp.where(rest == m2, ex, float(N_EXPERTS)), axis=0, keepdims=True)
    sel2 = ex == i2
    e2 = jnp.exp(m2 - m1)
    den = 1.0 + e2
    g1 = 1.0 / den
    g2 = e2 / den

    sel = jnp.where(sel1 | sel2, 1.0, 0.0)
    sel_pad = jnp.concatenate([sel, jnp.zeros_like(sel)], axis=0).astype(BF16)
    base = _dot(sel_pad, before_ref[...])[0:N_EXPERTS, :] + carry_ref[:, 0:1]
    r1 = jnp.sum(jnp.where(sel1, base, 0.0), axis=0, keepdims=True)
    r2 = jnp.sum(jnp.where(sel2, base, 0.0), axis=0, keepdims=True)
    carry_ref[...] = carry_ref[...] + jnp.sum(sel, axis=1, keepdims=True)

    meta = jnp.where(ex == 0, i1, jnp.where(ex == 1, i2, jnp.where(ex == 2, r1, r2)))
    mi_ref[...] = meta.astype(I32)
    gt = jnp.where(ex == 0, g1, jnp.where(ex == 1, g2, 0.0))
    gt = jnp.concatenate([gt, jnp.zeros((LANES - N_EXPERTS, tm), F32)], axis=0)
    mf_ref[...] = gt.T[:, 0:SUBLANES]
    cnt_ref[...] = carry_ref[...].astype(I32)


def _router(h, wr_cat, counts_in):
    t, d = h.shape
    tm = min(TM_ROUTE, t)
    assert t % tm == 0 and N_EXPERTS == SUBLANES
    return pl.pallas_call(
        functools.partial(_router_kernel, tm),
        grid=(t // tm,),
        in_specs=[pl.BlockSpec((tm, d), lambda i: (i, 0)), _full_spec(wr_cat), _full_spec(counts_in)],
        out_specs=(pl.BlockSpec((SUBLANES, tm), lambda i: (0, i)),
                   pl.BlockSpec((tm, SUBLANES), lambda i: (i, 0)),
                   pl.BlockSpec((SUBLANES, LANES), lambda i: (0, 0))),
        out_shape=(jax.ShapeDtypeStruct((SUBLANES, t), I32),
                   jax.ShapeDtypeStruct((t, SUBLANES), F32),
                   jax.ShapeDtypeStruct((SUBLANES, LANES), I32)),
        scratch_shapes=[pltpu.VMEM((SUBLANES, LANES), F32), pltpu.VMEM((tm, tm), BF16)],
        compiler_params=_cparams(("arbitrary",)),
        name="router",
    )(h, wr_cat, counts_in)


def _row_copy(src, src_row, dst, dst_row, sem):
    return pltpu.make_async_copy(src.at[pl.ds(src_row, 1), :], dst.at[pl.ds(dst_row, 1), :], sem)


def _rows_wait(src, dst, n_rows, sem):
    pltpu.make_async_copy(src.at[pl.ds(0, n_rows), :], dst.at[pl.ds(0, n_rows), :], sem).wait()


def _dispatch_kernel(tm, tm_expert, n_blocks, blk_ranges, zi_ref, dest_ref, *rest):
    n_src = len(blk_ranges)
    h_refs = rest[:n_src]
    xs_ref, xbuf, zblk, lsems, rsems, zsem = rest[n_src:]
    i = pl.program_id(0)
    n = pl.num_programs(0)

    def block_load(g, start):
        slot = lax.rem(g, N_STAGE)
        for h_ref, (b0, b1) in zip(h_refs, blk_ranges):
            @pl.when((g >= b0) & (g < b1))
            def _(h_ref=h_ref, b0=b0):
                cp = pltpu.make_async_copy(h_ref.at[pl.ds((g - b0) * tm, tm), :], xbuf.at[slot],
                                           lsems.at[slot])
                if start:
                    cp.start()
                else:
                    cp.wait()

    def rows_wait(g):
        slot = lax.rem(g, N_STAGE)
        for _k in range(TOP_K):
            _rows_wait(xbuf.at[slot], xs_ref, tm, rsems.at[slot])

    @pl.when(i == 0)
    def _():
        block_load(i, True)

    @pl.when(i >= N_STAGE - 1)
    def _():
        rows_wait(i - (N_STAGE - 1))

    @pl.when(i + 1 < n)
    def _():
        block_load(i + 1, True)

    block_load(i, False)
    cur = lax.rem(i, N_STAGE)

    def issue(j, carry):
        r0 = pl.multiple_of(j * SUBLANES, SUBLANES)
        for u in range(SUBLANES):
            for k in range(TOP_K):
                dst = dest_ref[0, 0, j * SUBLANES + (k * tm + u)]
                _row_copy(xbuf.at[cur], r0 + u, xs_ref, dst, rsems.at[cur]).start(priority=k)
        return carry

    lax.fori_loop(0, tm // SUBLANES, issue, 0)

    @pl.when(i == n - 1)
    def _():
        for back in range(N_STAGE - 2, -1, -1):
            @pl.when(i >= back)
            def _(back=back):
                rows_wait(i - back)

    @pl.when(i == 0)
    def _():
        zblk[...] = jnp.zeros_like(zblk)
        for e in range(N_EXPERTS):
            lo = zi_ref[e] + zi_ref[N_EXPERTS + e]
            hi = zi_ref[e] + zi_ref[2 * N_EXPERTS + e]

            def zissue(r, carry):
                _row_copy(zblk, 0, xs_ref, r, zsem).start()
                return carry

            def zdrain(r, carry):
                _row_copy(zblk, 0, xs_ref, r, zsem).wait()
                return carry

            lax.fori_loop(lo, hi, zissue, 0)
            lax.fori_loop(lo, hi, zdrain, 0)

        def bcopy(b):
            return pltpu.make_async_copy(zblk, xs_ref.at[pl.ds(b * tm_expert, tm_expert), :], zsem)

        def bissue(b, carry):
            bcopy(b).start()
            return carry

        def bdrain(b, carry):
            bcopy(b).wait()
            return carry

        lax.fori_loop(zi_ref[3 * N_EXPERTS], n_blocks, bissue, 0)
        lax.fori_loop(zi_ref[3 * N_EXPERTS], n_blocks, bdrain, 0)


def _dispatch(hs_list, dest_blk, zinfo, n_blocks, tm_expert):
    d = hs_list[0].shape[1]
    nb = dest_blk.shape[0]
    tm = dest_blk.shape[2] // TOP_K
    blk_ranges, b0 = [], 0
    for h in hs_list:
        assert h.shape[0] % tm == 0 and h.shape[0] >= tm
        blk_ranges.append((b0, b0 + h.shape[0] // tm))
        b0 = blk_ranges[-1][1]
    assert b0 == nb
    any_spec = pl.BlockSpec(memory_space=pl.ANY)
    return pl.pallas_call(
        functools.partial(_dispatch_kernel, tm, tm_expert, n_blocks, tuple(blk_ranges)),
        grid_spec=pltpu.PrefetchScalarGridSpec(
            num_scalar_prefetch=1,
            grid=(nb,),
            in_specs=[pl.BlockSpec((1, 1, TOP_K * tm), lambda i, zi: (i, 0, 0), memory_space=pltpu.SMEM)]
            + [any_spec] * len(hs_list),
            out_specs=any_spec,
            scratch_shapes=[pltpu.VMEM((N_STAGE, tm, d), F32), pltpu.VMEM((tm_expert, d), F32),
                            pltpu.SemaphoreType.DMA((N_STAGE,)), pltpu.SemaphoreType.DMA((N_STAGE,)),
                            pltpu.SemaphoreType.DMA(())],
        ),
        out_shape=jax.ShapeDtypeStruct((n_blocks * tm_expert, d), F32),
        compiler_params=_cparams(("arbitrary",)),
        name="moe_dispatch",
    )(zinfo, dest_blk, *hs_list)


def _ffn_moe_kernel(be_ref, nu_ref, x_ref, wg_ref, wu_ref, wd_ref, y_ref, acc_ref):
    b = pl.program_id(0)
    f = pl.program_id(1)

    @pl.when(b < nu_ref[0])
    def _():
        @pl.when(f == 0)
        def _():
            acc_ref[...] = jnp.zeros_like(acc_ref)

        xb = x_ref[...].astype(BF16)
        mid = _silu(_dot(xb, wg_ref[0])) * _dot(xb, wu_ref[0])
        acc_ref[...] += _dot(mid.astype(BF16), wd_ref[0])

        @pl.when(f == pl.num_programs(1) - 1)
        def _():
            y_ref[...] = acc_ref[...]

    @pl.when((b >= nu_ref[0]) & (f == pl.num_programs(1) - 1))
    def _():
        y_ref[...] = jnp.zeros_like(y_ref)


def _ffn_moe(xs, wg, wu, wd, block_e, n_used, tm, tf):
    cap, d = xs.shape
    dff = wg.shape[2]
    nf = dff // tf
    assert cap % tm == 0 and dff % tf == 0

    def blk(b, nu):
        return jnp.minimum(b, nu[0] - 1)

    def fcol(b, f, nu):
        return jnp.where(b < nu[0], f, nf - 1)

    return pl.pallas_call(
        _ffn_moe_kernel,
        grid_spec=pltpu.PrefetchScalarGridSpec(
            num_scalar_prefetch=2,
            grid=(cap // tm, nf),
            in_specs=[pl.BlockSpec((tm, d), lambda b, f, be, nu: (blk(b, nu), 0)),
                      pl.BlockSpec((1, d, tf), lambda b, f, be, nu: (be[b], 0, fcol(b, f, nu))),
                      pl.BlockSpec((1, d, tf), lambda b, f, be, nu: (be[b], 0, fcol(b, f, nu))),
                      pl.BlockSpec((1, tf, d), lambda b, f, be, nu: (be[b], fcol(b, f, nu), 0))],
            out_specs=pl.BlockSpec((tm, d), lambda b, f, be, nu: (b, 0)),
            scratch_shapes=[pltpu.VMEM((tm, d), F32)],
        ),
        out_shape=jax.ShapeDtypeStruct((cap, d), F32),
        compiler_params=_cparams(("arbitrary", "arbitrary")),
        name="ffn_moe",
    )(block_e, n_used, xs, wg, wu, wd)


def _combine_kernel(alpha, tm, dcur_ref, dnext_ref, h_ref, gate_ref, g_ref, b_ref, ys_ref, o_ref, ybuf, sems):
    i = pl.program_id(0)
    n = pl.num_programs(0)

    def issue(dref, slot):
        def body(j, carry):
            r0 = pl.multiple_of(j * SUBLANES, SUBLANES)
            for u in range(SUBLANES):
                for k in range(TOP_K):
                    src = dref[0, 0, j * SUBLANES + (k * tm + u)]
                    _row_copy(ys_ref, src, ybuf.at[slot, k], r0 + u, sems.at[slot]).start(priority=k)
            return carry

        lax.fori_loop(0, tm // SUBLANES, body, 0)

    @pl.when(i == 0)
    def _():
        issue(dcur_ref, 0)

    @pl.when(i + 1 < n)
    def _():
        issue(dnext_ref, (i + 1) % 2)

    slot = i % 2
    for k in range(TOP_K):
        _rows_wait(ys_ref, ybuf.at[slot, k], tm, sems.at[slot])

    gates = gate_ref[...]
    ffn = gates[:, 0:1] * ybuf[slot, 0] + gates[:, 1:2] * ybuf[slot, 1]
    o_ref[...] = _ln(alpha * h_ref[...] + ffn, g_ref[...], b_ref[...])


def _combine(h, dest_blk, gates, ys, g, b, alpha):
    t, d = h.shape
    nb = dest_blk.shape[0]
    tm = dest_blk.shape[2] // TOP_K
    dest_spec = lambda imap: pl.BlockSpec((1, 1, TOP_K * tm), imap, memory_space=pltpu.SMEM)
    return pl.pallas_call(
        functools.partial(_combine_kernel, alpha, tm),
        grid=(nb,),
        in_specs=[dest_spec(lambda i: (i, 0, 0)),
                  dest_spec(lambda i: (jnp.minimum(i + 1, nb - 1), 0, 0)),
                  pl.BlockSpec((tm, d), lambda i: (i, 0)),
                  pl.BlockSpec((tm, SUBLANES), lambda i: (i, 0)),
                  pl.BlockSpec((1, d), lambda i: (0, 0)),
                  pl.BlockSpec((1, d), lambda i: (0, 0)),
                  pl.BlockSpec(memory_space=pl.ANY)],
        out_specs=pl.BlockSpec((tm, d), lambda i: (i, 0)),
        out_shape=jax.ShapeDtypeStruct((t, d), F32),
        scratch_shapes=[pltpu.VMEM((2, TOP_K, tm, d), F32), pltpu.SemaphoreType.DMA((2,))],
        compiler_params=_cparams(("arbitrary",)),
        name="moe_combine",
    )(dest_blk, dest_blk, h, gates, g, b, ys)


def _ffn_routed(hs_list, mw, g, b, alpha, tm_expert):
    d = hs_list[0].shape[1]
    counts_in = jnp.zeros((SUBLANES, LANES), I32)
    routed = []
    for h in hs_list:
        meta_i, gates, counts_in = _router(h, mw['w_router_cat'], counts_in)
        routed.append((meta_i, gates))
    counts = counts_in[:N_EXPERTS, 0]
    padded = (counts + tm_expert - 1) // tm_expert * tm_expert
    pad_end = jnp.cumsum(padded)
    pad_start = (pad_end - padded).astype(I32)
    n_assign = sum(h.shape[0] for h in hs_list) * TOP_K
    n_blocks = -(-n_assign // tm_expert) + N_EXPERTS
    cap = n_blocks * tm_expert
    n_used = (pad_end[-1] // tm_expert).astype(I32)
    blk_start = jnp.minimum(jnp.arange(n_blocks, dtype=I32), n_used - 1) * tm_expert
    block_e = jnp.minimum(jnp.sum(blk_start[:, None] >= pad_end[None, :], axis=1), N_EXPERTS - 1).astype(I32)
    zinfo = jnp.concatenate([pad_start, counts, padded, n_used.reshape(1)]).astype(I32)

    dests = []
    for h, (meta_i, _) in zip(hs_list, routed):
        dest = meta_i[TOP_K:2 * TOP_K, :]
        for e in range(N_EXPERTS):
            dest = dest + jnp.where(meta_i[0:TOP_K, :] == e, pad_start[e], 0)
        nb = h.shape[0] // TM_MOVE
        dests.append(dest.reshape(TOP_K, nb, TM_MOVE).transpose(1, 0, 2).reshape(nb, 1, TOP_K * TM_MOVE))
    xs = _dispatch(hs_list, jnp.concatenate(dests, axis=0), zinfo, n_blocks, tm_expert)
    ys = _ffn_moe(xs, mw['w_e_gate'], mw['w_e_up'], mw['w_e_down'], block_e, n_used.reshape(1), tm_expert,
                  TF_EXPERT)
    return [_combine(h, dest_blk, gates, ys, g, b, alpha)
            for h, dest_blk, (_, gates) in zip(hs_list, dests, routed)]


def _prep_layer(i, p, n_step):
    dq = p['conv_a_w'].shape[2]
    hd = dq // N_HEADS_B
    tril = jnp.tril(jnp.ones((CHUNK, CHUNK), dtype=bool))
    w_s = p['w_s'][i]
    b_s = p['b_s'][i]
    wp = p['w_pool'][i]
    ng, gc = wp.shape[0], wp.shape[1]
    w_pool_bd = (jnp.eye(ng, dtype=F32)[:, None, :, None] * wp[:, :, None, :]).reshape(ng * gc, ng * gc)
    row = lambda a: a.reshape(1, -1)
    return dict(
        ln_in_g=row(p['ln_in_g']), ln_in_b=row(p['ln_in_b']),
        w_in=p['w_in'][i].astype(BF16),
        conv_a_w=p['conv_a_w'][i], conv_a_b=row(p['conv_a_b'][i]),
        ln_a_g=row(p['ln_a_g'][i]), ln_a_b=row(p['ln_a_b'][i]),
        ln_v_g=row(p['ln_v_g'][i]), ln_v_b=row(p['ln_v_b'][i]),
        ws_tril=jnp.where(tril[None], w_s, 0).astype(BF16),
        bs_mat=jnp.repeat(b_s.T, hd, axis=1),
        ws_vec=jnp.repeat(jnp.where(tril[None], w_s, 0)[:, :n_step, :n_step].transpose(1, 2, 0)
                          .reshape(n_step * n_step, N_HEADS_B), hd, axis=1),
        bs_vec=jnp.repeat(b_s[:, :n_step].T, hd, axis=1),
        conv_c_w=p['conv_c_w'][i],
        w_pool_bd=w_pool_bd.astype(BF16), pool_scale=row(p['pool_scale'][i]),
        w_out=p['w_out'][i].astype(BF16),
        ln_mix_g=row(p['ln_mix_g'][i]), ln_mix_b=row(p['ln_mix_b'][i]),
    )


def kernel(x_prompt, x_sample, state_conv_a, state_conv_c, state_pool_d, ln_in_g, ln_in_b, w_in, conv_a_w,
           conv_a_b, ln_a_g, ln_a_b, ln_v_g, ln_v_b, w_s, b_s, conv_c_w, w_pool, pool_scale, w_out,
           ln_mix_g, ln_mix_b, w_ff_gate, w_ff_up, w_ff_down, w_router, w_e_gate, w_e_up, w_e_down,
           ln_ffn_g, ln_ffn_b):
    p = dict(ln_in_g=ln_in_g, ln_in_b=ln_in_b, w_in=w_in, conv_a_w=conv_a_w, conv_a_b=conv_a_b,
             ln_a_g=ln_a_g, ln_a_b=ln_a_b, ln_v_g=ln_v_g, ln_v_b=ln_v_b, w_s=w_s, b_s=b_s,
             conv_c_w=conv_c_w, w_pool=w_pool, pool_scale=pool_scale, w_out=w_out,
             ln_mix_g=ln_mix_g, ln_mix_b=ln_mix_b)
    depth = w_in.shape[0]
    bsz, seq, d = x_prompt.shape
    n_seq, n_step, _ = x_sample.shape
    dq = conv_a_w.shape[2]
    alpha = float((2.0 * depth) ** 0.25)

    hp = x_prompt
    hs = x_sample.transpose(1, 0, 2).reshape(n_step * n_seq, d)
    outs = {k: [] for k in ('pa', 'pc', 'pd', 'pv', 'sa', 'sc', 'sd', 'sv')}
    for i in range(depth):
        lw = _prep_layer(i, p, n_step)
        hp, pa, pc, pd, pv = _mixer_prompt(hp, lw, i == 0, alpha)
        tm_state = lambda s: s.transpose(1, 0, 2).reshape(-1, dq)
        hs, sa, sc, sd, sv = _mixer_sample(hs, tm_state(state_conv_a[i]), tm_state(state_conv_c[i]),
                                           tm_state(state_pool_d[i]), lw, i == 0, alpha, n_seq, n_step)
        lg, lb = ln_ffn_g[i].reshape(1, -1), ln_ffn_b[i].reshape(1, -1)
        j = i // 2
        if i % 2 == 0:
            wg, wu, wd = w_ff_gate[j].astype(BF16), w_ff_up[j].astype(BF16), w_ff_down[j].astype(BF16)
            hp = _ffn_dense(hp.reshape(bsz * seq, d), wg, wu, wd, lg, lb, alpha, TM_DENSE,
                            TF_DENSE).reshape(bsz, seq, d)
            hs = _ffn_dense(hs, wg, wu, wd, lg, lb, alpha, hs.shape[0], TF_DENSE)
        else:
            wr = jnp.pad(w_router[j], ((0, 0), (0, LANES - N_EXPERTS)))
            wr_hi = wr.astype(BF16)
            wr_lo = (wr - wr_hi.astype(F32)).astype(BF16)
            mw = dict(w_router_cat=jnp.concatenate([wr_hi, wr_lo], axis=1),
                      w_e_gate=w_e_gate[j].astype(BF16), w_e_up=w_e_up[j].astype(BF16),
                      w_e_down=w_e_down[j].astype(BF16))
            hp, hs = _ffn_routed([hp.reshape(bsz * seq, d), hs], mw, lg, lb, alpha, TM_EXPERT)
            hp = hp.reshape(bsz, seq, d)
        back = lambda a: a.reshape(-1, n_seq, dq).transpose(1, 0, 2)
        for k, val in zip(('pa', 'pc', 'pd', 'pv', 'sa', 'sc', 'sd', 'sv'),
                          (pa, pc, pd, pv, back(sa), back(sc), back(sd), back(sv))):
            outs[k].append(val)
    y_sample = hs.reshape(n_step, n_seq, d).transpose(1, 0, 2)
    st = lambda k: jnp.stack(outs[k])
    return (hp, y_sample, st('pa'), st('pc'), st('pd'), st('pv'), st('sa'), st('sc'), st('sd'), st('sv'))
```

```python
import functools

import numpy as np
import jax
import jax.numpy as jnp
from jax import lax
from jax.experimental import pallas as pl
from jax.experimental.pallas import tpu as pltpu

F32 = jnp.float32
BF16 = jnp.bfloat16
I32 = jnp.int32

PAST_LEN = 16384
CHUNK = 128
N_HEADS_B = 4
POOL_WINDOWS = (2, 4, 8, 16)
POOL_PAST = 15
CONV_A_W = 31
CONV_C_W = 3
N_EXPERTS = 8
TOP_K = 2
LN_EPS = 1e-5
SQRT_HALF = float(np.sqrt(0.5).astype(np.float32))

LANES = 128
SUBLANES = 8
VMEM_LIMIT_BYTES = 56 * 1024 * 1024

HIST_A = 32
HIST_C = 8
HIST_D = 24

TL_PROMPT = 512
TM_DENSE = 512
TF_DENSE = 1792
TF_EXPERT = 1792
TM_ROUTE = 512
TM_MOVE = 256
TM_EXPERT = 512
N_STAGE = 3


def _ln(x, g, b):
    mu = jnp.mean(x, axis=-1, keepdims=True)
    xc = x - mu
    var = jnp.mean(xc * xc, axis=-1, keepdims=True)
    return xc * lax.rsqrt(var + LN_EPS) * g + b


def _gelu(x):
    return 0.5 * x * (1.0 + lax.erf(x * SQRT_HALF))


def _silu(x):
    return x * jax.nn.sigmoid(x)


def _dot(a, b):
    return jnp.dot(a, b, preferred_element_type=F32)


def _cparams(sem):
    return pltpu.CompilerParams(dimension_semantics=sem, vmem_limit_bytes=VMEM_LIMIT_BYTES)


def _mixer_prompt_kernel(first_layer, alpha, tl, d_mix,
                         x_ref, lng_ref, lnb_ref, win_ref, caw_ref, cab_ref, lag_ref, lab_ref,
                         lvg_ref, lvb_ref, ws_ref, bsm_ref, ccw_ref, wpool_ref, pscale_ref,
                         wout_ref, lmg_ref, lmb_ref,
                         h_out, sa_out, sc_out, sd_out, v_out,
                         ext_a, ext_c, ext_d, cat_ref, z_ref, hb_ref, mix_ref):
    dq = d_mix // 4
    j = pl.program_id(1)
    nj = pl.num_programs(1)

    @pl.when(j == 0)
    def _():
        ext_a[0, 0:HIST_A, :] = jnp.zeros((HIST_A, dq), F32)
        ext_c[0:HIST_C, :] = jnp.zeros((HIST_C, dq), F32)
        ext_d[0:HIST_D, :] = jnp.zeros((HIST_D, dq), F32)

    d = x_ref.shape[2]
    rb = CHUNK

    def resid(r0, n):
        return h_out[0, r0:r0 + n, :] if first_layer else x_ref[0, r0:r0 + n, :]

    for r0 in range(0, tl, rb):
        x = x_ref[0, r0:r0 + rb, :]
        if first_layer:
            x = _ln(x, lng_ref[...], lnb_ref[...])
            h_out[0, r0:r0 + rb, :] = x
        hb_ref[r0:r0 + rb, :] = x.astype(BF16)

    z_ref[...] = _dot(hb_ref[...], win_ref[...])

    def zcol(r0, i, n=rb):
        return z_ref[r0:r0 + n, i * dq:(i + 1) * dq]

    chunk_rows = range(0, tl, CHUNK)

    for r0 in chunk_rows:
        ext_a[0, HIST_A + r0:HIST_A + r0 + rb, :] = zcol(r0, 0) * jax.nn.sigmoid(zcol(r0, 1))
    n_ext = HIST_A + tl
    for r0 in range(0, n_ext - SUBLANES, rb):
        n = min(rb, n_ext - SUBLANES - r0)
        blk = ext_a[0, r0:r0 + n + SUBLANES, :]
        for s in range(1, SUBLANES):
            ext_a[s, r0:r0 + n, :] = pltpu.roll(blk, n + SUBLANES - s, axis=0)[0:n, :]
    off_a = HIST_A - (CONV_A_W - 1)
    for q0 in range(0, tl, 64):
        acc = jnp.zeros((64, dq), F32)
        for k in range(CONV_A_W):
            s = (off_a + k) % SUBLANES
            row = off_a + k - s + q0
            acc = acc + ext_a[s, row:row + 64, :] * caw_ref[k:k + 1, :]
        y = _silu(_ln(acc + cab_ref[...], lag_ref[...], lab_ref[...]))
        cat_ref[q0:q0 + 64, 0:dq] = y.astype(BF16)

    hd = dq // N_HEADS_B
    lane_head = lax.broadcasted_iota(I32, (CHUNK, dq), 1) // hd
    for r0 in chunk_rows:
        v = _ln(_gelu(zcol(r0, 3)), lvg_ref[...], lvb_ref[...])
        sm = bsm_ref[...]
        for hh in range(N_HEADS_B):
            vm = jnp.where(lane_head == hh, v, 0.0).astype(BF16)
            sm = sm + _dot(ws_ref[hh], vm)
        cat_ref[r0:r0 + CHUNK, dq:2 * dq] = (_gelu(zcol(r0, 2)) * sm).astype(BF16)
        if r0 == tl - CHUNK:
            @pl.when(j == nj - 1)
            def _(v=v):
                v_out[0] = v

    off_c = HIST_C - (CONV_C_W - 1)
    for r0 in chunk_rows:
        ext_c[HIST_C + r0:HIST_C + r0 + rb, :] = zcol(r0, 6) * zcol(r0, 4)
    for r0 in chunk_rows:
        conv_c = jnp.zeros((CHUNK, dq), F32)
        for k in range(CONV_C_W):
            conv_c = conv_c + ext_c[off_c + r0 + k:off_c + r0 + k + CHUNK, :] * ccw_ref[k:k + 1, :]
        cat_ref[r0:r0 + CHUNK, 2 * dq:3 * dq] = (zcol(r0, 5) * conv_c).astype(BF16)

    gc = dq // len(POOL_WINDOWS)
    lane_grp = lax.broadcasted_iota(I32, (CHUNK, dq), 1) // gc
    lead = 2 * SUBLANES
    for r0 in chunk_rows:
        ext_d[HIST_D + r0:HIST_D + r0 + rb, :] = zcol(r0, 7)
    for r0 in chunk_rows:
        e = ext_d[HIST_D + r0 - lead:HIST_D + r0 + CHUNK, :]
        s2 = e + pltpu.roll(e, 1, axis=0)
        s4 = s2 + pltpu.roll(s2, 2, axis=0)
        s8 = s4 + pltpu.roll(s4, 4, axis=0)
        s16 = s8 + pltpu.roll(s8, 8, axis=0)
        sums = (s2, s4, s8, s16)
        pos = (j * tl + r0 + lax.broadcasted_iota(I32, (CHUNK, 1), 0)).astype(F32)
        mean = jnp.zeros((CHUNK, dq), F32)
        for g, w in enumerate(POOL_WINDOWS):
            inv = 1.0 / jnp.minimum(pos + 1.0, float(w))
            mean = jnp.where(lane_grp == g, sums[g][lead:lead + CHUNK, :] * inv, mean)
        dd = (mean - e[lead:lead + CHUNK, :]).astype(BF16)
        cat_ref[r0:r0 + CHUNK, 3 * dq:4 * dq] = (_dot(dd, wpool_ref[...]) * pscale_ref[...]).astype(BF16)

    mix_ref[...] = _dot(cat_ref[...], wout_ref[...])
    for r0 in range(0, tl, 64):
        h_out[0, r0:r0 + 64, :] = _ln(alpha * resid(r0, 64) + mix_ref[r0:r0 + 64, :], lmg_ref[...], lmb_ref[...])

    @pl.when(j == nj - 1)
    def _():
        sa_out[0] = ext_a[0, HIST_A + tl - (CONV_A_W - 1):HIST_A + tl, :]
        sc_out[0] = ext_c[HIST_C + tl - (CONV_C_W - 1):HIST_C + tl, :]
        sd_out[0] = ext_d[HIST_D + tl - POOL_PAST:HIST_D + tl, :]

    ext_a[0, 0:HIST_A, :] = ext_a[0, tl:tl + HIST_A, :]
    ext_c[0:HIST_C, :] = ext_c[tl:tl + HIST_C, :]
    ext_d[0:HIST_D, :] = ext_d[tl:tl + HIST_D, :]


def _full_spec(arr):
    nd = arr.ndim
    return pl.BlockSpec(arr.shape, lambda *_: (0,) * nd)


def _mixer_prompt(x, lw, first_layer, alpha):
    bsz, seq, d = x.shape
    d_mix = lw['w_out'].shape[0]
    dq = d_mix // 4
    tl = TL_PROMPT
    assert seq % tl == 0 and tl % CHUNK == 0 and seq >= CHUNK
    params = [lw['ln_in_g'], lw['ln_in_b'], lw['w_in'], lw['conv_a_w'], lw['conv_a_b'], lw['ln_a_g'],
              lw['ln_a_b'], lw['ln_v_g'], lw['ln_v_b'], lw['ws_tril'], lw['bs_mat'], lw['conv_c_w'],
              lw['w_pool_bd'], lw['pool_scale'], lw['w_out'], lw['ln_mix_g'], lw['ln_mix_b']]
    out_shape = (
        jax.ShapeDtypeStruct((bsz, seq, d), F32),
        jax.ShapeDtypeStruct((bsz, CONV_A_W - 1, dq), F32),
        jax.ShapeDtypeStruct((bsz, CONV_C_W - 1, dq), F32),
        jax.ShapeDtypeStruct((bsz, POOL_PAST, dq), F32),
        jax.ShapeDtypeStruct((bsz, CHUNK, dq), F32),
    )
    state_spec = lambda rows: pl.BlockSpec((1, rows, dq), lambda b, j: (b, 0, 0))
    return pl.pallas_call(
        functools.partial(_mixer_prompt_kernel, first_layer, alpha, tl, d_mix),
        grid=(bsz, seq // tl),
        in_specs=[pl.BlockSpec((1, tl, d), lambda b, j: (b, j, 0))] + [_full_spec(p) for p in params],
        out_specs=(pl.BlockSpec((1, tl, d), lambda b, j: (b, j, 0)),
                   state_spec(CONV_A_W - 1), state_spec(CONV_C_W - 1), state_spec(POOL_PAST),
                   state_spec(CHUNK)),
        out_shape=out_shape,
        scratch_shapes=[pltpu.VMEM((SUBLANES, HIST_A + tl, dq), F32), pltpu.VMEM((HIST_C + tl, dq), F32),
                        pltpu.VMEM((HIST_D + tl, dq), F32), pltpu.VMEM((tl, d_mix), BF16),
                        pltpu.VMEM((tl, lw['w_in'].shape[1]), F32), pltpu.VMEM((tl, d), BF16),
                        pltpu.VMEM((tl, d), F32)],
        compiler_params=_cparams(("arbitrary", "arbitrary")),
        name="mixer_prompt",
    )(x, *params)


def _mixer_sample_kernel(first_layer, alpha, n_seq, n_step, d_mix,
                         x_ref, sa_ref, sc_ref, sd_ref, lng_ref, lnb_ref, win_ref, caw_ref, cab_ref,
                         lag_ref, lab_ref, lvg_ref, lvb_ref, wsv_ref, bsv_ref, ccw_ref, wpool_ref,
                         pscale_ref, wout_ref, lmg_ref, lmb_ref,
                         h_out, sa_out, sc_out, sd_out, v_out, cat_ref):
    dq = d_mix // 4
    x = x_ref[...]
    h = _ln(x, lng_ref[...], lnb_ref[...]) if first_layer else x
    hb = h.astype(BF16)

    def proj(i):
        return _dot(hb, win_ref[:, i * dq:(i + 1) * dq])

    def slab(val, i):
        return val[i * n_seq:(i + 1) * n_seq, :]

    def ext_slabs(state_ref, n_past, cur):
        return ([state_ref[i * n_seq:(i + 1) * n_seq, :] for i in range(n_past)]
                + [slab(cur, i) for i in range(n_step)])

    def store_state(out_ref, slabs, n_keep):
        for i, sl in enumerate(slabs[len(slabs) - n_keep:]):
            out_ref[i * n_seq:(i + 1) * n_seq, :] = sl

    a_glu = proj(0) * jax.nn.sigmoid(proj(1))
    ea = ext_slabs(sa_ref, CONV_A_W - 1, a_glu)
    for l in range(n_step):
        acc = jnp.zeros((n_seq, dq), F32)
        for k in range(CONV_A_W):
            acc = acc + ea[l + k] * caw_ref[k:k + 1, :]
        y = _silu(_ln(acc + cab_ref[...], lag_ref[...], lab_ref[...]))
        cat_ref[l * n_seq:(l + 1) * n_seq, 0:dq] = y.astype(BF16)
    store_state(sa_out, ea, CONV_A_W - 1)

    u = _gelu(proj(2))
    v = _ln(_gelu(proj(3)), lvg_ref[...], lvb_ref[...])
    v_out[...] = v
    for l in range(n_step):
        s = jnp.zeros((n_seq, dq), F32) + bsv_ref[l:l + 1, :]
        for m in range(l + 1):
            s = s + slab(v, m) * wsv_ref[l * n_step + m:l * n_step + m + 1, :]
        cat_ref[l * n_seq:(l + 1) * n_seq, dq:2 * dq] = (slab(u, l) * s).astype(BF16)

    c_x = proj(4)
    c_b = proj(5)
    c_c = proj(6)
    gx = c_c * c_x
    ec = ext_slabs(sc_ref, CONV_C_W - 1, gx)
    for l in range(n_step):
        acc = jnp.zeros((n_seq, dq), F32)
        for k in range(CONV_C_W):
            acc = acc + ec[l + k] * ccw_ref[k:k + 1, :]
        cat_ref[l * n_seq:(l + 1) * n_seq, 2 * dq:3 * dq] = (slab(c_b, l) * acc).astype(BF16)
    store_state(sc_out, ec, CONV_C_W - 1)

    d_in = proj(7)
    ed = ext_slabs(sd_ref, POOL_PAST, d_in)
    memo = {}

    def wsum(i, w):
        if i < 0:
            return None
        if w == 1:
            return ed[i]
        if (i, w) not in memo:
            a, b = wsum(i, w // 2), wsum(i - w // 2, w // 2)
            memo[(i, w)] = a if b is None else a + b
        return memo[(i, w)]

    gc = dq // len(POOL_WINDOWS)
    lane_grp = lax.broadcasted_iota(I32, (n_seq, dq), 1) // gc
    for l in range(n_step):
        mean = jnp.zeros((n_seq, dq), F32)
        for g, w in enumerate(POOL_WINDOWS):
            count = min(PAST_LEN + l + 1, w)
            mean = jnp.where(lane_grp == g, wsum(POOL_PAST + l, w) * (1.0 / count), mean)
        dd = (mean - slab(d_in, l)).astype(BF16)
        cat_ref[l * n_seq:(l + 1) * n_seq, 3 * dq:4 * dq] = (
            _dot(dd, wpool_ref[...]) * pscale_ref[...]).astype(BF16)
    store_state(sd_out, ed, POOL_PAST)

    mix = _dot(cat_ref[...], wout_ref[...])
    h_out[...] = _ln(alpha * h + mix, lmg_ref[...], lmb_ref[...])


def _mixer_sample(x_tm, sa, sc, sd, lw, first_layer, alpha, n_seq, n_step):
    rows, d = x_tm.shape
    d_mix = lw['w_out'].shape[0]
    dq = d_mix // 4
    assert n_seq % SUBLANES == 0 and n_step <= CHUNK and PAST_LEN % CHUNK == 0
    ins = [x_tm, sa, sc, sd, lw['ln_in_g'], lw['ln_in_b'], lw['w_in'], lw['conv_a_w'], lw['conv_a_b'],
           lw['ln_a_g'], lw['ln_a_b'], lw['ln_v_g'], lw['ln_v_b'], lw['ws_vec'], lw['bs_vec'],
           lw['conv_c_w'], lw['w_pool_bd'], lw['pool_scale'], lw['w_out'], lw['ln_mix_g'], lw['ln_mix_b']]
    out_shape = (
        jax.ShapeDtypeStruct((rows, d), F32),
        jax.ShapeDtypeStruct(((CONV_A_W - 1) * n_seq, dq), F32),
        jax.ShapeDtypeStruct(((CONV_C_W - 1) * n_seq, dq), F32),
        jax.ShapeDtypeStruct((POOL_PAST * n_seq, dq), F32),
        jax.ShapeDtypeStruct((rows, dq), F32),
    )
    return pl.pallas_call(
        functools.partial(_mixer_sample_kernel, first_layer, alpha, n_seq, n_step, d_mix),
        grid=(1,),
        in_specs=[_full_spec(a) for a in ins],
        out_specs=tuple(pl.BlockSpec(s.shape, lambda i: (0, 0)) for s in out_shape),
        out_shape=out_shape,
        scratch_shapes=[pltpu.VMEM((rows, d_mix), BF16)],
        compiler_params=_cparams(("arbitrary",)),
        name="mixer_sample",
    )(*ins)


def _ffn_dense_kernel(alpha, tf, x_ref, wg_hbm, wu_hbm, wd_hbm, g_ref, b_ref, o_ref, wg_ref, wu_ref, wd_ref, sem):
    @pl.when(pl.program_id(0) == 0)
    def _():
        copies = [pltpu.make_async_copy(src, dst, sem.at[n])
                  for n, (src, dst) in enumerate(((wg_hbm, wg_ref), (wu_hbm, wu_ref), (wd_hbm, wd_ref)))]
        for cp in copies:
            cp.start()
        for cp in copies:
            cp.wait()

    x = x_ref[...]
    xb = x.astype(BF16)
    ffn = None
    for c0 in range(0, wg_ref.shape[1], tf):
        mid = _silu(_dot(xb, wg_ref[:, c0:c0 + tf])) * _dot(xb, wu_ref[:, c0:c0 + tf])
        part = _dot(mid.astype(BF16), wd_ref[c0:c0 + tf, :])
        ffn = part if ffn is None else ffn + part
    o_ref[...] = _ln(alpha * x + ffn, g_ref[...], b_ref[...])


def _ffn_dense(x, wg, wu, wd, g, b, alpha, tm, tf):
    t, d = x.shape
    dff = wg.shape[1]
    assert t % tm == 0 and dff % tf == 0
    any_spec = pl.BlockSpec(memory_space=pl.ANY)
    return pl.pallas_call(
        functools.partial(_ffn_dense_kernel, alpha, tf),
        grid=(t // tm,),
        in_specs=[pl.BlockSpec((tm, d), lambda i: (i, 0)), any_spec, any_spec, any_spec,
                  pl.BlockSpec((1, d), lambda i: (0, 0)),
                  pl.BlockSpec((1, d), lambda i: (0, 0))],
        out_specs=pl.BlockSpec((tm, d), lambda i: (i, 0)),
        out_shape=jax.ShapeDtypeStruct((t, d), F32),
        scratch_shapes=[pltpu.VMEM(wg.shape, BF16), pltpu.VMEM(wu.shape, BF16), pltpu.VMEM(wd.shape, BF16),
                        pltpu.SemaphoreType.DMA((3,))],
        compiler_params=_cparams(("arbitrary",)),
        name="ffn_dense",
    )(x, wg, wu, wd, g, b)


def _router_kernel(tm, h_ref, wcat_ref, cin_ref, mi_ref, mf_ref, cnt_ref, carry_ref, before_ref):
    i = pl.program_id(0)

    @pl.when(i == 0)
    def _():
        carry_ref[...] = cin_ref[...].astype(F32)
        row = lax.broadcasted_iota(I32, (tm, tm), 0)
        col = lax.broadcasted_iota(I32, (tm, tm), 1)
        before_ref[...] = jnp.where(row < col, 1.0, 0.0).astype(BF16)

    h = h_ref[...]
    h_hi = h.astype(BF16)
    h_lo = (h - h_hi.astype(F32)).astype(BF16)
    p_hi = _dot(h_hi, wcat_ref[...])
    p_lo = _dot(h_lo, wcat_ref[...])
    logits = (p_hi[:, 0:LANES] + (p_lo[:, 0:LANES] + p_hi[:, LANES:2 * LANES])) + p_lo[:, LANES:2 * LANES]
    lg = logits.T[0:N_EXPERTS, :]
    ex = lax.broadcasted_iota(I32, (N_EXPERTS, tm), 0).astype(F32)
    m1 = jnp.max(lg, axis=0, keepdims=True)
    i1 = jnp.min(jnp.where(lg == m1, ex, float(N_EXPERTS)), axis=0, keepdims=True)
    sel1 = ex == i1
    rest = jnp.where(sel1, -jnp.inf, lg)
    m2 = jnp.max(rest, axis=0, keepdims=True)
    i2 = jnp.min(jnp.where(rest == m2, ex, float(N_EXPERTS)), axis=0, keepdims=True)
    sel2 = ex == i2
    e2 = jnp.exp(m2 - m1)
    den = 1.0 + e2
    g1 = 1.0 / den
    g2 = e2 / den

    sel = jnp.where(sel1 | sel2, 1.0, 0.0)
    sel_pad = jnp.concatenate([sel, jnp.zeros_like(sel)], axis=0).astype(BF16)
    base = _dot(sel_pad, before_ref[...])[0:N_EXPERTS, :] + carry_ref[:, 0:1]
    r1 = jnp.sum(jnp.where(sel1, base, 0.0), axis=0, keepdims=True)
    r2 = jnp.sum(jnp.where(sel2, base, 0.0), axis=0, keepdims=True)
    carry_ref[...] = carry_ref[...] + jnp.sum(sel, axis=1, keepdims=True)

    meta = jnp.where(ex == 0, i1, jnp.where(ex == 1, i2, jnp.where(ex == 2, r1, r2)))
    mi_ref[...] = meta.astype(I32)
    gt = jnp.where(ex == 0, g1, jnp.where(ex == 1, g2, 0.0))
    gt = jnp.concatenate([gt, jnp.zeros((LANES - N_EXPERTS, tm), F32)], axis=0)
    mf_ref[...] = gt.T[:, 0:SUBLANES]
    cnt_ref[...] = carry_ref[...].astype(I32)


def _router(h, wr_cat, counts_in):
    t, d = h.shape
    tm = min(TM_ROUTE, t)
    assert t % tm == 0 and N_EXPERTS == SUBLANES
    return pl.pallas_call(
        functools.partial(_router_kernel, tm),
        grid=(t // tm,),
        in_specs=[pl.BlockSpec((tm, d), lambda i: (i, 0)), _full_spec(wr_cat), _full_spec(counts_in)],
        out_specs=(pl.BlockSpec((SUBLANES, tm), lambda i: (0, i)),
                   pl.BlockSpec((tm, SUBLANES), lambda i: (i, 0)),
                   pl.BlockSpec((SUBLANES, LANES), lambda i: (0, 0))),
        out_shape=(jax.ShapeDtypeStruct((SUBLANES, t), I32),
                   jax.ShapeDtypeStruct((t, SUBLANES), F32),
                   jax.ShapeDtypeStruct((SUBLANES, LANES), I32)),
        scratch_shapes=[pltpu.VMEM((SUBLANES, LANES), F32), pltpu.VMEM((tm, tm), BF16)],
        compiler_params=_cparams(("arbitrary",)),
        name="router",
    )(h, wr_cat, counts_in)


def _row_copy(src, src_row, dst, dst_row, sem):
    return pltpu.make_async_copy(src.at[pl.ds(src_row, 1), :], dst.at[pl.ds(dst_row, 1), :], sem)


def _rows_wait(src, dst, n_rows, sem):
    pltpu.make_async_copy(src.at[pl.ds(0, n_rows), :], dst.at[pl.ds(0, n_rows), :], sem).wait()


def _dispatch_kernel(tm, tm_expert, n_blocks, blk_ranges, zi_ref, dest_ref, *rest):
    n_src = len(blk_ranges)
    h_refs = rest[:n_src]
    xs_ref, xbuf, zblk, lsems, rsems, zsem = rest[n_src:]
    i = pl.program_id(0)
    n = pl.num_programs(0)

    def block_load(g, start):
        slot = lax.rem(g, N_STAGE)
        for h_ref, (b0, b1) in zip(h_refs, blk_ranges):
            @pl.when((g >= b0) & (g < b1))
            def _(h_ref=h_ref, b0=b0):
                cp = pltpu.make_async_copy(h_ref.at[pl.ds((g - b0) * tm, tm), :], xbuf.at[slot],
                                           lsems.at[slot])
                if start:
                    cp.start()
                else:
                    cp.wait()

    def rows_wait(g):
        slot = lax.rem(g, N_STAGE)
        for _k in range(TOP_K):
            _rows_wait(xbuf.at[slot], xs_ref, tm, rsems.at[slot])

    @pl.when(i == 0)
    def _():
        block_load(i, True)

    @pl.when(i >= N_STAGE - 1)
    def _():
        rows_wait(i - (N_STAGE - 1))

    @pl.when(i + 1 < n)
    def _():
        block_load(i + 1, True)

    block_load(i, False)
    cur = lax.rem(i, N_STAGE)

    def issue(j, carry):
        r0 = pl.multiple_of(j * SUBLANES, SUBLANES)
        for u in range(SUBLANES):
            for k in range(TOP_K):
                dst = dest_ref[0, 0, j * SUBLANES + (k * tm + u)]
                _row_copy(xbuf.at[cur], r0 + u, xs_ref, dst, rsems.at[cur]).start(priority=k)
        return carry

    lax.fori_loop(0, tm // SUBLANES, issue, 0)

    @pl.when(i == n - 1)
    def _():
        for back in range(N_STAGE - 2, -1, -1):
            @pl.when(i >= back)
            def _(back=back):
                rows_wait(i - back)

    @pl.when(i == 0)
    def _():
        zblk[...] = jnp.zeros_like(zblk)
        for e in range(N_EXPERTS):
            lo = zi_ref[e] + zi_ref[N_EXPERTS + e]
            hi = zi_ref[e] + zi_ref[2 * N_EXPERTS + e]

            def zissue(r, carry):
                _row_copy(zblk, 0, xs_ref, r, zsem).start()
                return carry

            def zdrain(r, carry):
                _row_copy(zblk, 0, xs_ref, r, zsem).wait()
                return carry

            lax.fori_loop(lo, hi, zissue, 0)
            lax.fori_loop(lo, hi, zdrain, 0)

        def bcopy(b):
            return pltpu.make_async_copy(zblk, xs_ref.at[pl.ds(b * tm_expert, tm_expert), :], zsem)

        def bissue(b, carry):
            bcopy(b).start()
            return carry

        def bdrain(b, carry):
            bcopy(b).wait()
            return carry

        lax.fori_loop(zi_ref[3 * N_EXPERTS], n_blocks, bissue, 0)
        lax.fori_loop(zi_ref[3 * N_EXPERTS], n_blocks, bdrain, 0)


def _dispatch(hs_list, dest_blk, zinfo, n_blocks, tm_expert):
    d = hs_list[0].shape[1]
    nb = dest_blk.shape[0]
    tm = dest_blk.shape[2] // TOP_K
    blk_ranges, b0 = [], 0
    for h in hs_list:
        assert h.shape[0] % tm == 0 and h.shape[0] >= tm
        blk_ranges.append((b0, b0 + h.shape[0] // tm))
        b0 = blk_ranges[-1][1]
    assert b0 == nb
    any_spec = pl.BlockSpec(memory_space=pl.ANY)
    return pl.pallas_call(
        functools.partial(_dispatch_kernel, tm, tm_expert, n_blocks, tuple(blk_ranges)),
        grid_spec=pltpu.PrefetchScalarGridSpec(
            num_scalar_prefetch=1,
            grid=(nb,),
            in_specs=[pl.BlockSpec((1, 1, TOP_K * tm), lambda i, zi: (i, 0, 0), memory_space=pltpu.SMEM)]
            + [any_spec] * len(hs_list),
            out_specs=any_spec,
            scratch_shapes=[pltpu.VMEM((N_STAGE, tm, d), F32), pltpu.VMEM((tm_expert, d), F32),
                            pltpu.SemaphoreType.DMA((N_STAGE,)), pltpu.SemaphoreType.DMA((N_STAGE,)),
                            pltpu.SemaphoreType.DMA(())],
        ),
        out_shape=jax.ShapeDtypeStruct((n_blocks * tm_expert, d), F32),
        compiler_params=_cparams(("arbitrary",)),
        name="moe_dispatch",
    )(zinfo, dest_blk, *hs_list)


def _ffn_moe_kernel(tf, ch, be_ref, nu_ref, x_ref, wg_hbm, wu_hbm, wd_hbm, y_ref,
                    wg_ref, wu_ref, wd_ref, stage_in, stage_out, sems):
    b = pl.program_id(0)
    dff = wg_ref.shape[1]
    n_ch = dff // ch
    e = be_ref[b]
    fresh = (b == 0) | (e != be_ref[jnp.maximum(b - 1, 0)])

    @pl.when((b < nu_ref[0]) & fresh)
    def _():
        def chunk(n):
            slot = n % 2
            kind, c = divmod(n, n_ch)
            cols = pl.ds(c * ch, ch)
            if kind == 0:
                return (pltpu.make_async_copy(wg_hbm.at[e, :, cols], stage_in.at[slot], sems.at[slot]),
                        stage_in.at[slot], wg_ref.at[:, cols])
            if kind == 1:
                return (pltpu.make_async_copy(wu_hbm.at[e, :, cols], stage_in.at[slot], sems.at[slot]),
                        stage_in.at[slot], wu_ref.at[:, cols])
            return (pltpu.make_async_copy(wd_hbm.at[e, cols, :], stage_out.at[slot], sems.at[slot]),
                    stage_out.at[slot], wd_ref.at[cols, :])

        chunk(0)[0].start()
        for n in range(3 * n_ch):
            if n + 1 < 3 * n_ch:
                chunk(n + 1)[0].start()
            copy, staged, dst = chunk(n)
            copy.wait()
            dst[...] = staged[...].astype(BF16)

    @pl.when(b < nu_ref[0])
    def _():
        xb = x_ref[...].astype(BF16)
        ffn = None
        for c0 in range(0, dff, tf):
            mid = _silu(_dot(xb, wg_ref[:, c0:c0 + tf])) * _dot(xb, wu_ref[:, c0:c0 + tf])
            part = _dot(mid.astype(BF16), wd_ref[c0:c0 + tf, :])
            ffn = part if ffn is None else ffn + part
        y_ref[...] = ffn

    @pl.when(b >= nu_ref[0])
    def _():
        y_ref[...] = jnp.zeros_like(y_ref)


def _ffn_moe(xs, wg, wu, wd, block_e, n_used, tm, tf):
    cap, d = xs.shape
    dff = wg.shape[2]
    ch = 256
    assert cap % tm == 0 and dff % tf == 0 and dff % ch == 0
    any_spec = pl.BlockSpec(memory_space=pl.ANY)
    return pl.pallas_call(
        functools.partial(_ffn_moe_kernel, tf, ch),
        grid_spec=pltpu.PrefetchScalarGridSpec(
            num_scalar_prefetch=2,
            grid=(cap // tm,),
            in_specs=[pl.BlockSpec((tm, d), lambda b, be, nu: (jnp.minimum(b, nu[0] - 1), 0)),
                      any_spec, any_spec, any_spec],
            out_specs=pl.BlockSpec((tm, d), lambda b, be, nu: (b, 0)),
            scratch_shapes=[pltpu.VMEM((d, dff), BF16), pltpu.VMEM((d, dff), BF16), pltpu.VMEM((dff, d), BF16),
                            pltpu.VMEM((2, d, ch), F32), pltpu.VMEM((2, ch, d), F32),
                            pltpu.SemaphoreType.DMA((2,))],
        ),
        out_shape=jax.ShapeDtypeStruct((cap, d), F32),
        compiler_params=_cparams(("arbitrary",)),
        name="ffn_moe",
    )(block_e, n_used, xs, wg, wu, wd)


def _combine_kernel(alpha, tm, dcur_ref, dnext_ref, h_ref, gate_ref, g_ref, b_ref, ys_ref, o_ref, ybuf, sems):
    i = pl.program_id(0)
    n = pl.num_programs(0)

    def issue(dref, slot):
        def body(j, carry):
            r0 = pl.multiple_of(j * SUBLANES, SUBLANES)
            for u in range(SUBLANES):
                for k in range(TOP_K):
                    src = dref[0, 0, j * SUBLANES + (k * tm + u)]
                    _row_copy(ys_ref, src, ybuf.at[slot, k], r0 + u, sems.at[slot]).start(priority=k)
            return carry

        lax.fori_loop(0, tm // SUBLANES, body, 0)

    @pl.when(i == 0)
    def _():
        issue(dcur_ref, 0)

    @pl.when(i + 1 < n)
    def _():
        issue(dnext_ref, (i + 1) % 2)

    slot = i % 2
    for k in range(TOP_K):
        _rows_wait(ys_ref, ybuf.at[slot, k], tm, sems.at[slot])

    gates = gate_ref[...]
    ffn = gates[:, 0:1] * ybuf[slot, 0] + gates[:, 1:2] * ybuf[slot, 1]
    o_ref[...] = _ln(alpha * h_ref[...] + ffn, g_ref[...], b_ref[...])


def _combine(h, dest_blk, gates, ys, g, b, alpha):
    t, d = h.shape
    nb = dest_blk.shape[0]
    tm = dest_blk.shape[2] // TOP_K
    dest_spec = lambda imap: pl.BlockSpec((1, 1, TOP_K * tm), imap, memory_space=pltpu.SMEM)
    return pl.pallas_call(
        functools.partial(_combine_kernel, alpha, tm),
        grid=(nb,),
        in_specs=[dest_spec(lambda i: (i, 0, 0)),
                  dest_spec(lambda i: (jnp.minimum(i + 1, nb - 1), 0, 0)),
                  pl.BlockSpec((tm, d), lambda i: (i, 0)),
                  pl.BlockSpec((tm, SUBLANES), lambda i: (i, 0)),
                  pl.BlockSpec((1, d), lambda i: (0, 0)),
                  pl.BlockSpec((1, d), lambda i: (0, 0)),
                  pl.BlockSpec(memory_space=pl.ANY)],
        out_specs=pl.BlockSpec((tm, d), lambda i: (i, 0)),
        out_shape=jax.ShapeDtypeStruct((t, d), F32),
        scratch_shapes=[pltpu.VMEM((2, TOP_K, tm, d), F32), pltpu.SemaphoreType.DMA((2,))],
        compiler_params=_cparams(("arbitrary",)),
        name="moe_combine",
    )(dest_blk, dest_blk, h, gates, g, b, ys)


def _ffn_routed(hs_list, mw, g, b, alpha, tm_expert):
    d = hs_list[0].shape[1]
    counts_in = jnp.zeros((SUBLANES, LANES), I32)
    routed = []
    for h in hs_list:
        meta_i, gates, counts_in = _router(h, mw['w_router_cat'], counts_in)
        routed.append((meta_i, gates))
    counts = counts_in[:N_EXPERTS, 0]
    padded = (counts + tm_expert - 1) // tm_expert * tm_expert
    pad_end = jnp.cumsum(padded)
    pad_start = (pad_end - padded).astype(I32)
    n_assign = sum(h.shape[0] for h in hs_list) * TOP_K
    n_blocks = -(-n_assign // tm_expert) + N_EXPERTS
    cap = n_blocks * tm_expert
    n_used = (pad_end[-1] // tm_expert).astype(I32)
    blk_start = jnp.minimum(jnp.arange(n_blocks, dtype=I32), n_used - 1) * tm_expert
    block_e = jnp.minimum(jnp.sum(blk_start[:, None] >= pad_end[None, :], axis=1), N_EXPERTS - 1).astype(I32)
    zinfo = jnp.concatenate([pad_start, counts, padded, n_used.reshape(1)]).astype(I32)

    dests = []
    for h, (meta_i, _) in zip(hs_list, routed):
        dest = meta_i[TOP_K:2 * TOP_K, :]
        for e in range(N_EXPERTS):
            dest = dest + jnp.where(meta_i[0:TOP_K, :] == e, pad_start[e], 0)
        nb = h.shape[0] // TM_MOVE
        dests.append(dest.reshape(TOP_K, nb, TM_MOVE).transpose(1, 0, 2).reshape(nb, 1, TOP_K * TM_MOVE))
    xs = _dispatch(hs_list, jnp.concatenate(dests, axis=0), zinfo, n_blocks, tm_expert)
    ys = _ffn_moe(xs, mw['w_e_gate'], mw['w_e_up'], mw['w_e_down'], block_e, n_used.reshape(1), tm_expert,
                  TF_EXPERT)
    return [_combine(h, dest_blk, gates, ys, g, b, alpha)
            for h, dest_blk, (_, gates) in zip(hs_list, dests, routed)]


def _prep_layer(i, p, n_step):
    dq = p['conv_a_w'].shape[2]
    hd = dq // N_HEADS_B
    tril = jnp.tril(jnp.ones((CHUNK, CHUNK), dtype=bool))
    w_s = p['w_s'][i]
    b_s = p['b_s'][i]
    wp = p['w_pool'][i]
    ng, gc = wp.shape[0], wp.shape[1]
    w_pool_bd = (jnp.eye(ng, dtype=F32)[:, None, :, None] * wp[:, :, None, :]).reshape(ng * gc, ng * gc)
    row = lambda a: a.reshape(1, -1)
    return dict(
        ln_in_g=row(p['ln_in_g']), ln_in_b=row(p['ln_in_b']),
        w_in=p['w_in'][i].astype(BF16),
        conv_a_w=p['conv_a_w'][i], conv_a_b=row(p['conv_a_b'][i]),
        ln_a_g=row(p['ln_a_g'][i]), ln_a_b=row(p['ln_a_b'][i]),
        ln_v_g=row(p['ln_v_g'][i]), ln_v_b=row(p['ln_v_b'][i]),
        ws_tril=jnp.where(tril[None], w_s, 0).astype(BF16),
        bs_mat=jnp.repeat(b_s.T, hd, axis=1),
        ws_vec=jnp.repeat(jnp.where(tril[None], w_s, 0)[:, :n_step, :n_step].transpose(1, 2, 0)
                          .reshape(n_step * n_step, N_HEADS_B), hd, axis=1),
        bs_vec=jnp.repeat(b_s[:, :n_step].T, hd, axis=1),
        conv_c_w=p['conv_c_w'][i],
        w_pool_bd=w_pool_bd.astype(BF16), pool_scale=row(p['pool_scale'][i]),
        w_out=p['w_out'][i].astype(BF16),
        ln_mix_g=row(p['ln_mix_g'][i]), ln_mix_b=row(p['ln_mix_b'][i]),
    )


def kernel(x_prompt, x_sample, state_conv_a, state_conv_c, state_pool_d, ln_in_g, ln_in_b, w_in, conv_a_w,
           conv_a_b, ln_a_g, ln_a_b, ln_v_g, ln_v_b, w_s, b_s, conv_c_w, w_pool, pool_scale, w_out,
           ln_mix_g, ln_mix_b, w_ff_gate, w_ff_up, w_ff_down, w_router, w_e_gate, w_e_up, w_e_down,
           ln_ffn_g, ln_ffn_b):
    p = dict(ln_in_g=ln_in_g, ln_in_b=ln_in_b, w_in=w_in, conv_a_w=conv_a_w, conv_a_b=conv_a_b,
             ln_a_g=ln_a_g, ln_a_b=ln_a_b, ln_v_g=ln_v_g, ln_v_b=ln_v_b, w_s=w_s, b_s=b_s,
             conv_c_w=conv_c_w, w_pool=w_pool, pool_scale=pool_scale, w_out=w_out,
             ln_mix_g=ln_mix_g, ln_mix_b=ln_mix_b)
    depth = w_in.shape[0]
    bsz, seq, d = x_prompt.shape
    n_seq, n_step, _ = x_sample.shape
    dq = conv_a_w.shape[2]
    alpha = float((2.0 * depth) ** 0.25)

    hp = x_prompt
    hs = x_sample.transpose(1, 0, 2).reshape(n_step * n_seq, d)
    outs = {k: [] for k in ('pa', 'pc', 'pd', 'pv', 'sa', 'sc', 'sd', 'sv')}
    for i in range(depth):
        lw = _prep_layer(i, p, n_step)
        hp, pa, pc, pd, pv = _mixer_prompt(hp, lw, i == 0, alpha)
        tm_state = lambda s: s.transpose(1, 0, 2).reshape(-1, dq)
        hs, sa, sc, sd, sv = _mixer_sample(hs, tm_state(state_conv_a[i]), tm_state(state_conv_c[i]),
                                           tm_state(state_pool_d[i]), lw, i == 0, alpha, n_seq, n_step)
        lg, lb = ln_ffn_g[i].reshape(1, -1), ln_ffn_b[i].reshape(1, -1)
        j = i // 2
        if i % 2 == 0:
            wg, wu, wd = w_ff_gate[j].astype(BF16), w_ff_up[j].astype(BF16), w_ff_down[j].astype(BF16)
            hp = _ffn_dense(hp.reshape(bsz * seq, d), wg, wu, wd, lg, lb, alpha, TM_DENSE,
                            TF_DENSE).reshape(bsz, seq, d)
            hs = _ffn_dense(hs, wg, wu, wd, lg, lb, alpha, hs.shape[0], TF_DENSE)
        else:
            wr = jnp.pad(w_router[j], ((0, 0), (0, LANES - N_EXPERTS)))
            wr_hi = wr.astype(BF16)
            wr_lo = (wr - wr_hi.astype(F32)).astype(BF16)
            mw = dict(w_router_cat=jnp.concatenate([wr_hi, wr_lo], axis=1),
                      w_e_gate=w_e_gate[j], w_e_up=w_e_up[j], w_e_down=w_e_down[j])
            hp, hs = _ffn_routed([hp.reshape(bsz * seq, d), hs], mw, lg, lb, alpha, TM_EXPERT)
            hp = hp.reshape(bsz, seq, d)
        back = lambda a: a.reshape(-1, n_seq, dq).transpose(1, 0, 2)
        for k, val in zip(('pa', 'pc', 'pd', 'pv', 'sa', 'sc', 'sd', 'sv'),
                          (pa, pc, pd, pv, back(sa), back(sc), back(sd), back(sv))):
            outs[k].append(val)
    y_sample = hs.reshape(n_step, n_seq, d).transpose(1, 0, 2)
    st = lambda k: jnp.stack(outs[k])
    return (hp, y_sample, st('pa'), st('pc'), st('pd'), st('pv'), st('sa'), st('sc'), st('sd'), st('sv'))
```

```python
import functools

import numpy as np
import jax
import jax.numpy as jnp
from jax import lax
from jax.experimental import pallas as pl
from jax.experimental.pallas import tpu as pltpu

F32 = jnp.float32
BF16 = jnp.bfloat16
I32 = jnp.int32

PAST_LEN = 16384
CHUNK = 128
N_HEADS_B = 4
POOL_WINDOWS = (2, 4, 8, 16)
POOL_PAST = 15
CONV_A_W = 31
CONV_C_W = 3
N_EXPERTS = 8
TOP_K = 2
LN_EPS = 1e-5
SQRT_HALF = float(np.sqrt(0.5).astype(np.float32))

LANES = 128
SUBLANES = 8
VMEM_LIMIT_BYTES = 56 * 1024 * 1024

HIST_A = 32
HIST_C = 8
HIST_D = 24

TL_PROMPT = 512
TM_DENSE = 512
TF_DENSE = 1792
TF_EXPERT = 1792
TM_ROUTE = 512
TM_MOVE = 256
TM_EXPERT = 512
N_STAGE = 3


def _ln(x, g, b):
    mu = jnp.mean(x, axis=-1, keepdims=True)
    xc = x - mu
    var = jnp.mean(xc * xc, axis=-1, keepdims=True)
    return xc * lax.rsqrt(var + LN_EPS) * g + b


def _gelu(x):
    return 0.5 * x * (1.0 + lax.erf(x * SQRT_HALF))


def _silu(x):
    return x * jax.nn.sigmoid(x)


def _dot(a, b):
    return jnp.dot(a, b, preferred_element_type=F32)


def _cparams(sem):
    return pltpu.CompilerParams(dimension_semantics=sem, vmem_limit_bytes=VMEM_LIMIT_BYTES)


def _mixer_prompt_kernel(first_layer, alpha, tl, d_mix,
                         x_ref, lng_ref, lnb_ref, win_ref, caw_ref, cab_ref, lag_ref, lab_ref,
                         lvg_ref, lvb_ref, ws_ref, bsm_ref, ccw_ref, wpool_ref, pscale_ref,
                         wout_ref, lmg_ref, lmb_ref,
                         h_out, sa_out, sc_out, sd_out, v_out,
                         ext_a, ext_c, ext_d, cat_ref, z_ref, hb_ref, mix_ref):
    dq = d_mix // 4
    j = pl.program_id(1)
    nj = pl.num_programs(1)

    @pl.when(j == 0)
    def _():
        ext_a[0, 0:HIST_A, :] = jnp.zeros((HIST_A, dq), F32)
        ext_c[0:HIST_C, :] = jnp.zeros((HIST_C, dq), F32)
        ext_d[0:HIST_D, :] = jnp.zeros((HIST_D, dq), F32)

    d = x_ref.shape[2]
    rb = CHUNK

    def resid(r0, n):
        return h_out[0, r0:r0 + n, :] if first_layer else x_ref[0, r0:r0 + n, :]

    for r0 in range(0, tl, rb):
        x = x_ref[0, r0:r0 + rb, :]
        if first_layer:
            x = _ln(x, lng_ref[...], lnb_ref[...])
            h_out[0, r0:r0 + rb, :] = x
        hb_ref[r0:r0 + rb, :] = x.astype(BF16)

    z_ref[...] = _dot(hb_ref[...], win_ref[...])

    def zcol(r0, i, n=rb):
        return z_ref[r0:r0 + n, i * dq:(i + 1) * dq]

    chunk_rows = range(0, tl, CHUNK)

    for r0 in chunk_rows:
        ext_a[0, HIST_A + r0:HIST_A + r0 + rb, :] = zcol(r0, 0) * jax.nn.sigmoid(zcol(r0, 1))
    n_ext = HIST_A + tl
    for r0 in range(0, n_ext - SUBLANES, rb):
        n = min(rb, n_ext - SUBLANES - r0)
        blk = ext_a[0, r0:r0 + n + SUBLANES, :]
        for s in range(1, SUBLANES):
            ext_a[s, r0:r0 + n, :] = pltpu.roll(blk, n + SUBLANES - s, axis=0)[0:n, :]
    off_a = HIST_A - (CONV_A_W - 1)
    for q0 in range(0, tl, 64):
        acc = jnp.zeros((64, dq), F32)
        for k in range(CONV_A_W):
            s = (off_a + k) % SUBLANES
            row = off_a + k - s + q0
            acc = acc + ext_a[s, row:row + 64, :] * caw_ref[k:k + 1, :]
        y = _silu(_ln(acc + cab_ref[...], lag_ref[...], lab_ref[...]))
        cat_ref[q0:q0 + 64, 0:dq] = y.astype(BF16)

    hd = dq // N_HEADS_B
    lane_head = lax.broadcasted_iota(I32, (CHUNK, dq), 1) // hd
    for r0 in chunk_rows:
        v = _ln(_gelu(zcol(r0, 3)), lvg_ref[...], lvb_ref[...])
        sm = bsm_ref[...]
        for hh in range(N_HEADS_B):
            vm = jnp.where(lane_head == hh, v, 0.0).astype(BF16)
            sm = sm + _dot(ws_ref[hh], vm)
        cat_ref[r0:r0 + CHUNK, dq:2 * dq] = (_gelu(zcol(r0, 2)) * sm).astype(BF16)
        if r0 == tl - CHUNK:
            @pl.when(j == nj - 1)
            def _(v=v):
                v_out[0] = v

    off_c = HIST_C - (CONV_C_W - 1)
    for r0 in chunk_rows:
        ext_c[HIST_C + r0:HIST_C + r0 + rb, :] = zcol(r0, 6) * zcol(r0, 4)
    for r0 in chunk_rows:
        conv_c = jnp.zeros((CHUNK, dq), F32)
        for k in range(CONV_C_W):
            conv_c = conv_c + ext_c[off_c + r0 + k:off_c + r0 + k + CHUNK, :] * ccw_ref[k:k + 1, :]
        cat_ref[r0:r0 + CHUNK, 2 * dq:3 * dq] = (zcol(r0, 5) * conv_c).astype(BF16)

    gc = dq // len(POOL_WINDOWS)
    lane_grp = lax.broadcasted_iota(I32, (CHUNK, dq), 1) // gc
    lead = 2 * SUBLANES
    for r0 in chunk_rows:
        ext_d[HIST_D + r0:HIST_D + r0 + rb, :] = zcol(r0, 7)
    for r0 in chunk_rows:
        e = ext_d[HIST_D + r0 - lead:HIST_D + r0 + CHUNK, :]
        s2 = e + pltpu.roll(e, 1, axis=0)
        s4 = s2 + pltpu.roll(s2, 2, axis=0)
        s8 = s4 + pltpu.roll(s4, 4, axis=0)
        s16 = s8 + pltpu.roll(s8, 8, axis=0)
        sums = (s2, s4, s8, s16)
        pos = (j * tl + r0 + lax.broadcasted_iota(I32, (CHUNK, 1), 0)).astype(F32)
        mean = jnp.zeros((CHUNK, dq), F32)
        for g, w in enumerate(POOL_WINDOWS):
            inv = 1.0 / jnp.minimum(pos + 1.0, float(w))
            mean = jnp.where(lane_grp == g, sums[g][lead:lead + CHUNK, :] * inv, mean)
        dd = (mean - e[lead:lead + CHUNK, :]).astype(BF16)
        cat_ref[r0:r0 + CHUNK, 3 * dq:4 * dq] = (_dot(dd, wpool_ref[...]) * pscale_ref[...]).astype(BF16)

    mix_ref[...] = _dot(cat_ref[...], wout_ref[...])
    for r0 in range(0, tl, 64):
        h_out[0, r0:r0 + 64, :] = _ln(alpha * resid(r0, 64) + mix_ref[r0:r0 + 64, :], lmg_ref[...], lmb_ref[...])

    @pl.when(j == nj - 1)
    def _():
        sa_out[0] = ext_a[0, HIST_A + tl - (CONV_A_W - 1):HIST_A + tl, :]
        sc_out[0] = ext_c[HIST_C + tl - (CONV_C_W - 1):HIST_C + tl, :]
        sd_out[0] = ext_d[HIST_D + tl - POOL_PAST:HIST_D + tl, :]

    ext_a[0, 0:HIST_A, :] = ext_a[0, tl:tl + HIST_A, :]
    ext_c[0:HIST_C, :] = ext_c[tl:tl + HIST_C, :]
    ext_d[0:HIST_D, :] = ext_d[tl:tl + HIST_D, :]


def _full_spec(arr):
    nd = arr.ndim
    return pl.BlockSpec(arr.shape, lambda *_: (0,) * nd)


def _mixer_prompt(x, lw, first_layer, alpha):
    bsz, seq, d = x.shape
    d_mix = lw['w_out'].shape[0]
    dq = d_mix // 4
    tl = TL_PROMPT
    assert seq % tl == 0 and tl % CHUNK == 0 and seq >= CHUNK
    params = [lw['ln_in_g'], lw['ln_in_b'], lw['w_in'], lw['conv_a_w'], lw['conv_a_b'], lw['ln_a_g'],
              lw['ln_a_b'], lw['ln_v_g'], lw['ln_v_b'], lw['ws_tril'], lw['bs_mat'], lw['conv_c_w'],
              lw['w_pool_bd'], lw['pool_scale'], lw['w_out'], lw['ln_mix_g'], lw['ln_mix_b']]
    out_shape = (
        jax.ShapeDtypeStruct((bsz, seq, d), F32),
        jax.ShapeDtypeStruct((bsz, CONV_A_W - 1, dq), F32),
        jax.ShapeDtypeStruct((bsz, CONV_C_W - 1, dq), F32),
        jax.ShapeDtypeStruct((bsz, POOL_PAST, dq), F32),
        jax.ShapeDtypeStruct((bsz, CHUNK, dq), F32),
    )
    state_spec = lambda rows: pl.BlockSpec((1, rows, dq), lambda b, j: (b, 0, 0))
    return pl.pallas_call(
        functools.partial(_mixer_prompt_kernel, first_layer, alpha, tl, d_mix),
        grid=(bsz, seq // tl),
        in_specs=[pl.BlockSpec((1, tl, d), lambda b, j: (b, j, 0))] + [_full_spec(p) for p in params],
        out_specs=(pl.BlockSpec((1, tl, d), lambda b, j: (b, j, 0)),
                   state_spec(CONV_A_W - 1), state_spec(CONV_C_W - 1), state_spec(POOL_PAST),
                   state_spec(CHUNK)),
        out_shape=out_shape,
        scratch_shapes=[pltpu.VMEM((SUBLANES, HIST_A + tl, dq), F32), pltpu.VMEM((HIST_C + tl, dq), F32),
                        pltpu.VMEM((HIST_D + tl, dq), F32), pltpu.VMEM((tl, d_mix), BF16),
                        pltpu.VMEM((tl, lw['w_in'].shape[1]), F32), pltpu.VMEM((tl, d), BF16),
                        pltpu.VMEM((tl, d), F32)],
        compiler_params=_cparams(("arbitrary", "arbitrary")),
        name="mixer_prompt",
    )(x, *params)


def _mixer_sample_kernel(first_layer, alpha, n_seq, n_step, d_mix,
                         x_ref, sa_ref, sc_ref, sd_ref, lng_ref, lnb_ref, win_ref, caw_ref, cab_ref,
                         lag_ref, lab_ref, lvg_ref, lvb_ref, wsv_ref, bsv_ref, ccw_ref, wpool_ref,
                         pscale_ref, wout_ref, lmg_ref, lmb_ref,
                         h_out, sa_out, sc_out, sd_out, v_out, cat_ref):
    dq = d_mix // 4
    x = x_ref[...]
    h = _ln(x, lng_ref[...], lnb_ref[...]) if first_layer else x
    hb = h.astype(BF16)

    def proj(i):
        return _dot(hb, win_ref[:, i * dq:(i + 1) * dq])

    def slab(val, i):
        return val[i * n_seq:(i + 1) * n_seq, :]

    def ext_slabs(state_ref, n_past, cur):
        return ([state_ref[i * n_seq:(i + 1) * n_seq, :] for i in range(n_past)]
                + [slab(cur, i) for i in range(n_step)])

    def store_state(out_ref, slabs, n_keep):
        for i, sl in enumerate(slabs[len(slabs) - n_keep:]):
            out_ref[i * n_seq:(i + 1) * n_seq, :] = sl

    a_glu = proj(0) * jax.nn.sigmoid(proj(1))
    ea = ext_slabs(sa_ref, CONV_A_W - 1, a_glu)
    for l in range(n_step):
        acc = jnp.zeros((n_seq, dq), F32)
        for k in range(CONV_A_W):
            acc = acc + ea[l + k] * caw_ref[k:k + 1, :]
        y = _silu(_ln(acc + cab_ref[...], lag_ref[...], lab_ref[...]))
        cat_ref[l * n_seq:(l + 1) * n_seq, 0:dq] = y.astype(BF16)
    store_state(sa_out, ea, CONV_A_W - 1)

    u = _gelu(proj(2))
    v = _ln(_gelu(proj(3)), lvg_ref[...], lvb_ref[...])
    v_out[...] = v
    for l in range(n_step):
        s = jnp.zeros((n_seq, dq), F32) + bsv_ref[l:l + 1, :]
        for m in range(l + 1):
            s = s + slab(v, m) * wsv_ref[l * n_step + m:l * n_step + m + 1, :]
        cat_ref[l * n_seq:(l + 1) * n_seq, dq:2 * dq] = (slab(u, l) * s).astype(BF16)

    c_x = proj(4)
    c_b = proj(5)
    c_c = proj(6)
    gx = c_c * c_x
    ec = ext_slabs(sc_ref, CONV_C_W - 1, gx)
    for l in range(n_step):
        acc = jnp.zeros((n_seq, dq), F32)
        for k in range(CONV_C_W):
            acc = acc + ec[l + k] * ccw_ref[k:k + 1, :]
        cat_ref[l * n_seq:(l + 1) * n_seq, 2 * dq:3 * dq] = (slab(c_b, l) * acc).astype(BF16)
    store_state(sc_out, ec, CONV_C_W - 1)

    d_in = proj(7)
    ed = ext_slabs(sd_ref, POOL_PAST, d_in)
    memo = {}

    def wsum(i, w):
        if i < 0:
            return None
        if w == 1:
            return ed[i]
        if (i, w) not in memo:
            a, b = wsum(i, w // 2), wsum(i - w // 2, w // 2)
            memo[(i, w)] = a if b is None else a + b
        return memo[(i, w)]

    gc = dq // len(POOL_WINDOWS)
    lane_grp = lax.broadcasted_iota(I32, (n_seq, dq), 1) // gc
    for l in range(n_step):
        mean = jnp.zeros((n_seq, dq), F32)
        for g, w in enumerate(POOL_WINDOWS):
            count = min(PAST_LEN + l + 1, w)
            mean = jnp.where(lane_grp == g, wsum(POOL_PAST + l, w) * (1.0 / count), mean)
        dd = (mean - slab(d_in, l)).astype(BF16)
        cat_ref[l * n_seq:(l + 1) * n_seq, 3 * dq:4 * dq] = (
            _dot(dd, wpool_ref[...]) * pscale_ref[...]).astype(BF16)
    store_state(sd_out, ed, POOL_PAST)

    mix = _dot(cat_ref[...], wout_ref[...])
    h_out[...] = _ln(alpha * h + mix, lmg_ref[...], lmb_ref[...])


def _mixer_sample(x_tm, sa, sc, sd, lw, first_layer, alpha, n_seq, n_step):
    rows, d = x_tm.shape
    d_mix = lw['w_out'].shape[0]
    dq = d_mix // 4
    assert n_seq % SUBLANES == 0 and n_step <= CHUNK and PAST_LEN % CHUNK == 0
    ins = [x_tm, sa, sc, sd, lw['ln_in_g'], lw['ln_in_b'], lw['w_in'], lw['conv_a_w'], lw['conv_a_b'],
           lw['ln_a_g'], lw['ln_a_b'], lw['ln_v_g'], lw['ln_v_b'], lw['ws_vec'], lw['bs_vec'],
           lw['conv_c_w'], lw['w_pool_bd'], lw['pool_scale'], lw['w_out'], lw['ln_mix_g'], lw['ln_mix_b']]
    out_shape = (
        jax.ShapeDtypeStruct((rows, d), F32),
        jax.ShapeDtypeStruct(((CONV_A_W - 1) * n_seq, dq), F32),
        jax.ShapeDtypeStruct(((CONV_C_W - 1) * n_seq, dq), F32),
        jax.ShapeDtypeStruct((POOL_PAST * n_seq, dq), F32),
        jax.ShapeDtypeStruct((rows, dq), F32),
    )
    return pl.pallas_call(
        functools.partial(_mixer_sample_kernel, first_layer, alpha, n_seq, n_step, d_mix),
        grid=(1,),
        in_specs=[_full_spec(a) for a in ins],
        out_specs=tuple(pl.BlockSpec(s.shape, lambda i: (0, 0)) for s in out_shape),
        out_shape=out_shape,
        scratch_shapes=[pltpu.VMEM((rows, d_mix), BF16)],
        compiler_params=_cparams(("arbitrary",)),
        name="mixer_sample",
    )(*ins)


def _ffn_dense_kernel(alpha, tf, x_ref, wg_hbm, wu_hbm, wd_hbm, g_ref, b_ref, o_ref, wg_ref, wu_ref, wd_ref, sem):
    @pl.when(pl.program_id(0) == 0)
    def _():
        copies = [pltpu.make_async_copy(src, dst, sem.at[n])
                  for n, (src, dst) in enumerate(((wg_hbm, wg_ref), (wu_hbm, wu_ref), (wd_hbm, wd_ref)))]
        for cp in copies:
            cp.start()
        for cp in copies:
            cp.wait()

    x = x_ref[...]
    xb = x.astype(BF16)
    ffn = None
    for c0 in range(0, wg_ref.shape[1], tf):
        mid = _silu(_dot(xb, wg_ref[:, c0:c0 + tf])) * _dot(xb, wu_ref[:, c0:c0 + tf])
        part = _dot(mid.astype(BF16), wd_ref[c0:c0 + tf, :])
        ffn = part if ffn is None else ffn + part
    o_ref[...] = _ln(alpha * x + ffn, g_ref[...], b_ref[...])


def _ffn_dense(x, wg, wu, wd, g, b, alpha, tm, tf):
    t, d = x.shape
    dff = wg.shape[1]
    assert t % tm == 0 and dff % tf == 0
    any_spec = pl.BlockSpec(memory_space=pl.ANY)
    return pl.pallas_call(
        functools.partial(_ffn_dense_kernel, alpha, tf),
        grid=(t // tm,),
        in_specs=[pl.BlockSpec((tm, d), lambda i: (i, 0)), any_spec, any_spec, any_spec,
                  pl.BlockSpec((1, d), lambda i: (0, 0)),
                  pl.BlockSpec((1, d), lambda i: (0, 0))],
        out_specs=pl.BlockSpec((tm, d), lambda i: (i, 0)),
        out_shape=jax.ShapeDtypeStruct((t, d), F32),
        scratch_shapes=[pltpu.VMEM(wg.shape, BF16), pltpu.VMEM(wu.shape, BF16), pltpu.VMEM(wd.shape, BF16),
                        pltpu.SemaphoreType.DMA((3,))],
        compiler_params=_cparams(("arbitrary",)),
        name="ffn_dense",
    )(x, wg, wu, wd, g, b)


def _router_kernel(tm, h_ref, wcat_ref, cin_ref, mi_ref, mf_ref, cnt_ref, carry_ref, before_ref):
    i = pl.program_id(0)

    @pl.when(i == 0)
    def _():
        carry_ref[...] = cin_ref[...].astype(F32)
        row = lax.broadcasted_iota(I32, (tm, tm), 0)
        col = lax.broadcasted_iota(I32, (tm, tm), 1)
        before_ref[...] = jnp.where(row < col, 1.0, 0.0).astype(BF16)

    h = h_ref[...]
    h_hi = h.astype(BF16)
    h_lo = (h - h_hi.astype(F32)).astype(BF16)
    p_hi = _dot(h_hi, wcat_ref[...])
    p_lo = _dot(h_lo, wcat_ref[...])
    logits = (p_hi[:, 0:LANES] + (p_lo[:, 0:LANES] + p_hi[:, LANES:2 * LANES])) + p_lo[:, LANES:2 * LANES]
    lg = logits.T[0:N_EXPERTS, :]
    ex = lax.broadcasted_iota(I32, (N_EXPERTS, tm), 0).astype(F32)
    m1 = jnp.max(lg, axis=0, keepdims=True)
    i1 = jnp.min(jnp.where(lg == m1, ex, float(N_EXPERTS)), axis=0, keepdims=True)
    sel1 = ex == i1
    rest = jnp.where(sel1, -jnp.inf, lg)
    m2 = jnp.max(rest, axis=0, keepdims=True)
    i2 = jnp.min(jnp.where(rest == m2, ex, float(N_EXPERTS)), axis=0, keepdims=True)
    sel2 = ex == i2
    e2 = jnp.exp(m2 - m1)
    den = 1.0 + e2
    g1 = 1.0 / den
    g2 = e2 / den

    sel = jnp.where(sel1 | sel2, 1.0, 0.0)
    sel_pad = jnp.concatenate([sel, jnp.zeros_like(sel)], axis=0).astype(BF16)
    base = _dot(sel_pad, before_ref[...])[0:N_EXPERTS, :] + carry_ref[:, 0:1]
    r1 = jnp.sum(jnp.where(sel1, base, 0.0), axis=0, keepdims=True)
    r2 = jnp.sum(jnp.where(sel2, base, 0.0), axis=0, keepdims=True)
    carry_ref[...] = carry_ref[...] + jnp.sum(sel, axis=1, keepdims=True)

    meta = jnp.where(ex == 0, i1, jnp.where(ex == 1, i2, jnp.where(ex == 2, r1, r2)))
    mi_ref[...] = meta.astype(I32)
    gt = jnp.where(ex == 0, g1, jnp.where(ex == 1, g2, 0.0))
    gt = jnp.concatenate([gt, jnp.zeros((LANES - N_EXPERTS, tm), F32)], axis=0)
    mf_ref[...] = gt.T[:, 0:SUBLANES]
    cnt_ref[...] = carry_ref[...].astype(I32)


def _router(h, wr_cat, counts_in):
    t, d = h.shape
    tm = min(TM_ROUTE, t)
    assert t % tm == 0 and N_EXPERTS == SUBLANES
    return pl.pallas_call(
        functools.partial(_router_kernel, tm),
        grid=(t // tm,),
        in_specs=[pl.BlockSpec((tm, d), lambda i: (i, 0)), _full_spec(wr_cat), _full_spec(counts_in)],
        out_specs=(pl.BlockSpec((SUBLANES, tm), lambda i: (0, i)),
                   pl.BlockSpec((tm, SUBLANES), lambda i: (i, 0)),
                   pl.BlockSpec((SUBLANES, LANES), lambda i: (0, 0))),
        out_shape=(jax.ShapeDtypeStruct((SUBLANES, t), I32),
                   jax.ShapeDtypeStruct((t, SUBLANES), F32),
                   jax.ShapeDtypeStruct((SUBLANES, LANES), I32)),
        scratch_shapes=[pltpu.VMEM((SUBLANES, LANES), F32), pltpu.VMEM((tm, tm), BF16)],
        compiler_params=_cparams(("arbitrary",)),
        name="router",
    )(h, wr_cat, counts_in)


def _row_copy(src, src_row, dst, dst_row, sem):
    return pltpu.make_async_copy(src.at[pl.ds(src_row, 1), :], dst.at[pl.ds(dst_row, 1), :], sem)


def _rows_wait(src, dst, n_rows, sem):
    pltpu.make_async_copy(src.at[pl.ds(0, n_rows), :], dst.at[pl.ds(0, n_rows), :], sem).wait()


def _dispatch_kernel(tm, tm_expert, n_blocks, blk_ranges, zi_ref, dest_ref, *rest):
    n_src = len(blk_ranges)
    h_refs = rest[:n_src]
    xs_ref, xbuf, zblk, lsems, rsems, zsem = rest[n_src:]
    i = pl.program_id(0)
    n = pl.num_programs(0)

    def block_load(g, start):
        slot = lax.rem(g, N_STAGE)
        for h_ref, (b0, b1) in zip(h_refs, blk_ranges):
            @pl.when((g >= b0) & (g < b1))
            def _(h_ref=h_ref, b0=b0):
                cp = pltpu.make_async_copy(h_ref.at[pl.ds((g - b0) * tm, tm), :], xbuf.at[slot],
                                           lsems.at[slot])
                if start:
                    cp.start()
                else:
                    cp.wait()

    def rows_wait(g):
        slot = lax.rem(g, N_STAGE)
        for _k in range(TOP_K):
            _rows_wait(xbuf.at[slot], xs_ref, tm, rsems.at[slot])

    @pl.when(i == 0)
    def _():
        block_load(i, True)

    @pl.when(i >= N_STAGE - 1)
    def _():
        rows_wait(i - (N_STAGE - 1))

    @pl.when(i + 1 < n)
    def _():
        block_load(i + 1, True)

    block_load(i, False)
    cur = lax.rem(i, N_STAGE)

    def issue(j, carry):
        r0 = pl.multiple_of(j * SUBLANES, SUBLANES)
        for u in range(SUBLANES):
            for k in range(TOP_K):
                dst = dest_ref[0, 0, j * SUBLANES + (k * tm + u)]
                _row_copy(xbuf.at[cur], r0 + u, xs_ref, dst, rsems.at[cur]).start(priority=k)
        return carry

    lax.fori_loop(0, tm // SUBLANES, issue, 0)

    @pl.when(i == n - 1)
    def _():
        for back in range(N_STAGE - 2, -1, -1):
            @pl.when(i >= back)
            def _(back=back):
                rows_wait(i - back)

    @pl.when(i == 0)
    def _():
        zblk[...] = jnp.zeros_like(zblk)
        for e in range(N_EXPERTS):
            lo = zi_ref[e] + zi_ref[N_EXPERTS + e]
            hi = zi_ref[e] + zi_ref[2 * N_EXPERTS + e]

            def zissue(r, carry):
                _row_copy(zblk, 0, xs_ref, r, zsem).start()
                return carry

            def zdrain(r, carry):
                _row_copy(zblk, 0, xs_ref, r, zsem).wait()
                return carry

            lax.fori_loop(lo, hi, zissue, 0)
            lax.fori_loop(lo, hi, zdrain, 0)

        def bcopy(b):
            return pltpu.make_async_copy(zblk, xs_ref.at[pl.ds(b * tm_expert, tm_expert), :], zsem)

        def bissue(b, carry):
            bcopy(b).start()
            return carry

        def bdrain(b, carry):
            bcopy(b).wait()
            return carry

        lax.fori_loop(zi_ref[3 * N_EXPERTS], n_blocks, bissue, 0)
        lax.fori_loop(zi_ref[3 * N_EXPERTS], n_blocks, bdrain, 0)


def _dispatch(hs_list, dest_blk, zinfo, n_blocks, tm_expert):
    d = hs_list[0].shape[1]
    nb = dest_blk.shape[0]
    tm = dest_blk.shape[2] // TOP_K
    blk_ranges, b0 = [], 0
    for h in hs_list:
        assert h.shape[0] % tm == 0 and h.shape[0] >= tm
        blk_ranges.append((b0, b0 + h.shape[0] // tm))
        b0 = blk_ranges[-1][1]
    assert b0 == nb
    any_spec = pl.BlockSpec(memory_space=pl.ANY)
    return pl.pallas_call(
        functools.partial(_dispatch_kernel, tm, tm_expert, n_blocks, tuple(blk_ranges)),
        grid_spec=pltpu.PrefetchScalarGridSpec(
            num_scalar_prefetch=1,
            grid=(nb,),
            in_specs=[pl.BlockSpec((1, 1, TOP_K * tm), lambda i, zi: (i, 0, 0), memory_space=pltpu.SMEM)]
            + [any_spec] * len(hs_list),
            out_specs=any_spec,
            scratch_shapes=[pltpu.VMEM((N_STAGE, tm, d), F32), pltpu.VMEM((tm_expert, d), F32),
                            pltpu.SemaphoreType.DMA((N_STAGE,)), pltpu.SemaphoreType.DMA((N_STAGE,)),
                            pltpu.SemaphoreType.DMA(())],
        ),
        out_shape=jax.ShapeDtypeStruct((n_blocks * tm_expert, d), F32),
        compiler_params=_cparams(("arbitrary",)),
        name="moe_dispatch",
    )(zinfo, dest_blk, *hs_list)


def _ffn_moe_kernel(tf, be_ref, nu_ref, x_ref, wg_hbm, wu_hbm, wd_hbm, y_ref,
                    wg_ref, wu_ref, wd_ref, stage_in, stage_out, sems):
    b = pl.program_id(0)
    d, dff = wg_ref.shape
    rows_in = stage_in.shape[1]
    rows_out = stage_out.shape[1]
    n_in, n_out = d // rows_in, dff // rows_out
    e = be_ref[b]
    fresh = (b == 0) | (e != be_ref[jnp.maximum(b - 1, 0)])

    @pl.when((b < nu_ref[0]) & fresh)
    def _():
        def chunk(n):
            slot = n % 2
            if n < 2 * n_in:
                src, dst = (wg_hbm, wg_ref) if n < n_in else (wu_hbm, wu_ref)
                rows = pl.ds((n % n_in) * rows_in, rows_in)
                stage = stage_in.at[slot]
            else:
                src, dst = wd_hbm, wd_ref
                rows = pl.ds((n - 2 * n_in) * rows_out, rows_out)
                stage = stage_out.at[slot]
            return pltpu.make_async_copy(src.at[e, rows, :], stage, sems.at[slot]), stage, dst.at[rows, :]

        n_chunks = 2 * n_in + n_out
        chunk(0)[0].start()
        for n in range(n_chunks):
            if n + 1 < n_chunks:
                chunk(n + 1)[0].start()
            copy, staged, dst = chunk(n)
            copy.wait()
            dst[...] = staged[...].astype(BF16)

    @pl.when(b < nu_ref[0])
    def _():
        xb = x_ref[...].astype(BF16)
        ffn = None
        for c0 in range(0, dff, tf):
            mid = _silu(_dot(xb, wg_ref[:, c0:c0 + tf])) * _dot(xb, wu_ref[:, c0:c0 + tf])
            part = _dot(mid.astype(BF16), wd_ref[c0:c0 + tf, :])
            ffn = part if ffn is None else ffn + part
        y_ref[...] = ffn

    @pl.when(b >= nu_ref[0])
    def _():
        y_ref[...] = jnp.zeros_like(y_ref)


def _ffn_moe(xs, wg, wu, wd, block_e, n_used, tm, tf):
    cap, d = xs.shape
    dff = wg.shape[2]
    rows_in, rows_out = 128, 512
    assert cap % tm == 0 and dff % tf == 0 and d % rows_in == 0 and dff % rows_out == 0
    any_spec = pl.BlockSpec(memory_space=pl.ANY)
    return pl.pallas_call(
        functools.partial(_ffn_moe_kernel, tf),
        grid_spec=pltpu.PrefetchScalarGridSpec(
            num_scalar_prefetch=2,
            grid=(cap // tm,),
            in_specs=[pl.BlockSpec((tm, d), lambda b, be, nu: (jnp.minimum(b, nu[0] - 1), 0)),
                      any_spec, any_spec, any_spec],
            out_specs=pl.BlockSpec((tm, d), lambda b, be, nu: (b, 0)),
            scratch_shapes=[pltpu.VMEM((d, dff), BF16), pltpu.VMEM((d, dff), BF16), pltpu.VMEM((dff, d), BF16),
                            pltpu.VMEM((2, rows_in, dff), F32), pltpu.VMEM((2, rows_out, d), F32),
                            pltpu.SemaphoreType.DMA((2,))],
        ),
        out_shape=jax.ShapeDtypeStruct((cap, d), F32),
        compiler_params=_cparams(("arbitrary",)),
        name="ffn_moe",
    )(block_e, n_used, xs, wg, wu, wd)


def _combine_kernel(alpha, tm, dcur_ref, dnext_ref, h_ref, gate_ref, g_ref, b_ref, ys_ref, o_ref, ybuf, sems):
    i = pl.program_id(0)
    n = pl.num_programs(0)

    def issue(dref, slot):
        def body(j, carry):
            r0 = pl.multiple_of(j * SUBLANES, SUBLANES)
            for u in range(SUBLANES):
                for k in range(TOP_K):
                    src = dref[0, 0, j * SUBLANES + (k * tm + u)]
                    _row_copy(ys_ref, src, ybuf.at[slot, k], r0 + u, sems.at[slot]).start(priority=k)
            return carry

        lax.fori_loop(0, tm // SUBLANES, body, 0)

    @pl.when(i == 0)
    def _():
        issue(dcur_ref, 0)

    @pl.when(i + 1 < n)
    def _():
        issue(dnext_ref, (i + 1) % 2)

    slot = i % 2
    for k in range(TOP_K):
        _rows_wait(ys_ref, ybuf.at[slot, k], tm, sems.at[slot])

    gates = gate_ref[...]
    ffn = gates[:, 0:1] * ybuf[slot, 0] + gates[:, 1:2] * ybuf[slot, 1]
    o_ref[...] = _ln(alpha * h_ref[...] + ffn, g_ref[...], b_ref[...])


def _combine(h, dest_blk, gates, ys, g, b, alpha):
    t, d = h.shape
    nb = dest_blk.shape[0]
    tm = dest_blk.shape[2] // TOP_K
    dest_spec = lambda imap: pl.BlockSpec((1, 1, TOP_K * tm), imap, memory_space=pltpu.SMEM)
    return pl.pallas_call(
        functools.partial(_combine_kernel, alpha, tm),
        grid=(nb,),
        in_specs=[dest_spec(lambda i: (i, 0, 0)),
                  dest_spec(lambda i: (jnp.minimum(i + 1, nb - 1), 0, 0)),
                  pl.BlockSpec((tm, d), lambda i: (i, 0)),
                  pl.BlockSpec((tm, SUBLANES), lambda i: (i, 0)),
                  pl.BlockSpec((1, d), lambda i: (0, 0)),
                  pl.BlockSpec((1, d), lambda i: (0, 0)),
                  pl.BlockSpec(memory_space=pl.ANY)],
        out_specs=pl.BlockSpec((tm, d), lambda i: (i, 0)),
        out_shape=jax.ShapeDtypeStruct((t, d), F32),
        scratch_shapes=[pltpu.VMEM((2, TOP_K, tm, d), F32), pltpu.SemaphoreType.DMA((2,))],
        compiler_params=_cparams(("arbitrary",)),
        name="moe_combine",
    )(dest_blk, dest_blk, h, gates, g, b, ys)


def _ffn_routed(hs_list, mw, g, b, alpha, tm_expert):
    d = hs_list[0].shape[1]
    counts_in = jnp.zeros((SUBLANES, LANES), I32)
    routed = []
    for h in hs_list:
        meta_i, gates, counts_in = _router(h, mw['w_router_cat'], counts_in)
        routed.append((meta_i, gates))
    counts = counts_in[:N_EXPERTS, 0]
    padded = (counts + tm_expert - 1) // tm_expert * tm_expert
    pad_end = jnp.cumsum(padded)
    pad_start = (pad_end - padded).astype(I32)
    n_assign = sum(h.shape[0] for h in hs_list) * TOP_K
    n_blocks = -(-n_assign // tm_expert) + N_EXPERTS
    cap = n_blocks * tm_expert
    n_used = (pad_end[-1] // tm_expert).astype(I32)
    blk_start = jnp.minimum(jnp.arange(n_blocks, dtype=I32), n_used - 1) * tm_expert
    block_e = jnp.minimum(jnp.sum(blk_start[:, None] >= pad_end[None, :], axis=1), N_EXPERTS - 1).astype(I32)
    zinfo = jnp.concatenate([pad_start, counts, padded, n_used.reshape(1)]).astype(I32)

    dests = []
    for h, (meta_i, _) in zip(hs_list, routed):
        dest = meta_i[TOP_K:2 * TOP_K, :]
        for e in range(N_EXPERTS):
            dest = dest + jnp.where(meta_i[0:TOP_K, :] == e, pad_start[e], 0)
        nb = h.shape[0] // TM_MOVE
        dests.append(dest.reshape(TOP_K, nb, TM_MOVE).transpose(1, 0, 2).reshape(nb, 1, TOP_K * TM_MOVE))
    xs = _dispatch(hs_list, jnp.concatenate(dests, axis=0), zinfo, n_blocks, tm_expert)
    ys = _ffn_moe(xs, mw['w_e_gate'], mw['w_e_up'], mw['w_e_down'], block_e, n_used.reshape(1), tm_expert,
                  TF_EXPERT)
    return [_combine(h, dest_blk, gates, ys, g, b, alpha)
            for h, dest_blk, (_, gates) in zip(hs_list, dests, routed)]


def _prep_layer(i, p, n_step):
    dq = p['conv_a_w'].shape[2]
    hd = dq // N_HEADS_B
    tril = jnp.tril(jnp.ones((CHUNK, CHUNK), dtype=bool))
    w_s = p['w_s'][i]
    b_s = p['b_s'][i]
    wp = p['w_pool'][i]
    ng, gc = wp.shape[0], wp.shape[1]
    w_pool_bd = (jnp.eye(ng, dtype=F32)[:, None, :, None] * wp[:, :, None, :]).reshape(ng * gc, ng * gc)
    row = lambda a: a.reshape(1, -1)
    return dict(
        ln_in_g=row(p['ln_in_g']), ln_in_b=row(p['ln_in_b']),
        w_in=p['w_in'][i].astype(BF16),
        conv_a_w=p['conv_a_w'][i], conv_a_b=row(p['conv_a_b'][i]),
        ln_a_g=row(p['ln_a_g'][i]), ln_a_b=row(p['ln_a_b'][i]),
        ln_v_g=row(p['ln_v_g'][i]), ln_v_b=row(p['ln_v_b'][i]),
        ws_tril=jnp.where(tril[None], w_s, 0).astype(BF16),
        bs_mat=jnp.repeat(b_s.T, hd, axis=1),
        ws_vec=jnp.repeat(jnp.where(tril[None], w_s, 0)[:, :n_step, :n_step].transpose(1, 2, 0)
                          .reshape(n_step * n_step, N_HEADS_B), hd, axis=1),
        bs_vec=jnp.repeat(b_s[:, :n_step].T, hd, axis=1),
        conv_c_w=p['conv_c_w'][i],
        w_pool_bd=w_pool_bd.astype(BF16), pool_scale=row(p['pool_scale'][i]),
        w_out=p['w_out'][i].astype(BF16),
        ln_mix_g=row(p['ln_mix_g'][i]), ln_mix_b=row(p['ln_mix_b'][i]),
    )


def kernel(x_prompt, x_sample, state_conv_a, state_conv_c, state_pool_d, ln_in_g, ln_in_b, w_in, conv_a_w,
           conv_a_b, ln_a_g, ln_a_b, ln_v_g, ln_v_b, w_s, b_s, conv_c_w, w_pool, pool_scale, w_out,
           ln_mix_g, ln_mix_b, w_ff_gate, w_ff_up, w_ff_down, w_router, w_e_gate, w_e_up, w_e_down,
           ln_ffn_g, ln_ffn_b):
    p = dict(ln_in_g=ln_in_g, ln_in_b=ln_in_b, w_in=w_in, conv_a_w=conv_a_w, conv_a_b=conv_a_b,
             ln_a_g=ln_a_g, ln_a_b=ln_a_b, ln_v_g=ln_v_g, ln_v_b=ln_v_b, w_s=w_s, b_s=b_s,
             conv_c_w=conv_c_w, w_pool=w_pool, pool_scale=pool_scale, w_out=w_out,
             ln_mix_g=ln_mix_g, ln_mix_b=ln_mix_b)
    depth = w_in.shape[0]
    bsz, seq, d = x_prompt.shape
    n_seq, n_step, _ = x_sample.shape
    dq = conv_a_w.shape[2]
    alpha = float((2.0 * depth) ** 0.25)

    hp = x_prompt
    hs = x_sample.transpose(1, 0, 2).reshape(n_step * n_seq, d)
    outs = {k: [] for k in ('pa', 'pc', 'pd', 'pv', 'sa', 'sc', 'sd', 'sv')}
    for i in range(depth):
        lw = _prep_layer(i, p, n_step)
        hp, pa, pc, pd, pv = _mixer_prompt(hp, lw, i == 0, alpha)
        tm_state = lambda s: s.transpose(1, 0, 2).reshape(-1, dq)
        hs, sa, sc, sd, sv = _mixer_sample(hs, tm_state(state_conv_a[i]), tm_state(state_conv_c[i]),
                                           tm_state(state_pool_d[i]), lw, i == 0, alpha, n_seq, n_step)
        lg, lb = ln_ffn_g[i].reshape(1, -1), ln_ffn_b[i].reshape(1, -1)
        j = i // 2
        if i % 2 == 0:
            wg, wu, wd = w_ff_gate[j].astype(BF16), w_ff_up[j].astype(BF16), w_ff_down[j].astype(BF16)
            hp = _ffn_dense(hp.reshape(bsz * seq, d), wg, wu, wd, lg, lb, alpha, TM_DENSE,
                            TF_DENSE).reshape(bsz, seq, d)
            hs = _ffn_dense(hs, wg, wu, wd, lg, lb, alpha, hs.shape[0], TF_DENSE)
        else:
            wr = jnp.pad(w_router[j], ((0, 0), (0, LANES - N_EXPERTS)))
            wr_hi = wr.astype(BF16)
            wr_lo = (wr - wr_hi.astype(F32)).astype(BF16)
            mw = dict(w_router_cat=jnp.concatenate([wr_hi, wr_lo], axis=1),
                      w_e_gate=w_e_gate[j], w_e_up=w_e_up[j], w_e_down=w_e_down[j])
            hp, hs = _ffn_routed([hp.reshape(bsz * seq, d), hs], mw, lg, lb, alpha, TM_EXPERT)
            hp = hp.reshape(bsz, seq, d)
        back = lambda a: a.reshape(-1, n_seq, dq).transpose(1, 0, 2)
        for k, val in zip(('pa', 'pc', 'pd', 'pv', 'sa', 'sc', 'sd', 'sv'),
                          (pa, pc, pd, pv, back(sa), back(sc), back(sd), back(sv))):
            outs[k].append(val)
    y_sample = hs.reshape(n_step, n_seq, d).transpose(1, 0, 2)
    st = lambda k: jnp.stack(outs[k])
    return (hp, y_sample, st('pa'), st('pc'), st('pd'), st('pv'), st('sa'), st('sc'), st('sd'), st('sv'))
```

```python
import functools

import numpy as np
import jax
import jax.numpy as jnp
from jax import lax
from jax.experimental import pallas as pl
from jax.experimental.pallas import tpu as pltpu

F32 = jnp.float32
BF16 = jnp.bfloat16
I32 = jnp.int32

PAST_LEN = 16384
CHUNK = 128
N_HEADS_B = 4
POOL_WINDOWS = (2, 4, 8, 16)
POOL_PAST = 15
CONV_A_W = 31
CONV_C_W = 3
N_EXPERTS = 8
TOP_K = 2
LN_EPS = 1e-5
SQRT_HALF = float(np.sqrt(0.5).astype(np.float32))

LANES = 128
SUBLANES = 8
VMEM_LIMIT_BYTES = 56 * 1024 * 1024

HIST_A = 32
HIST_C = 8
HIST_D = 24

TL_PROMPT = 512
TM_DENSE = 512
TF_DENSE = 1792
TF_EXPERT = 1792
TM_ROUTE = 512
TM_MOVE = 256
TM_EXPERT = 512
N_STAGE = 3
N_WSTAGE = 4


def _ln(x, g, b):
    mu = jnp.mean(x, axis=-1, keepdims=True)
    xc = x - mu
    var = jnp.mean(xc * xc, axis=-1, keepdims=True)
    return xc * lax.rsqrt(var + LN_EPS) * g + b


def _gelu(x):
    return 0.5 * x * (1.0 + lax.erf(x * SQRT_HALF))


def _silu(x):
    return x * jax.nn.sigmoid(x)


def _dot(a, b):
    return jnp.dot(a, b, preferred_element_type=F32)


def _cparams(sem):
    return pltpu.CompilerParams(dimension_semantics=sem, vmem_limit_bytes=VMEM_LIMIT_BYTES)


def _mixer_prompt_kernel(first_layer, alpha, tl, d_mix,
                         x_ref, lng_ref, lnb_ref, win_ref, caw_ref, cab_ref, lag_ref, lab_ref,
                         lvg_ref, lvb_ref, ws_ref, bsm_ref, ccw_ref, wpool_ref, pscale_ref,
                         wout_ref, lmg_ref, lmb_ref,
                         h_out, sa_out, sc_out, sd_out, v_out,
                         ext_a, ext_c, ext_d, cat_ref, z_ref, hb_ref, mix_ref):
    dq = d_mix // 4
    j = pl.program_id(1)
    nj = pl.num_programs(1)

    @pl.when(j == 0)
    def _():
        ext_a[0, 0:HIST_A, :] = jnp.zeros((HIST_A, dq), F32)
        ext_c[0:HIST_C, :] = jnp.zeros((HIST_C, dq), F32)
        ext_d[0:HIST_D, :] = jnp.zeros((HIST_D, dq), F32)

    d = x_ref.shape[2]
    rb = CHUNK

    def resid(r0, n):
        return h_out[0, r0:r0 + n, :] if first_layer else x_ref[0, r0:r0 + n, :]

    for r0 in range(0, tl, rb):
        x = x_ref[0, r0:r0 + rb, :]
        if first_layer:
            x = _ln(x, lng_ref[...], lnb_ref[...])
            h_out[0, r0:r0 + rb, :] = x
        hb_ref[r0:r0 + rb, :] = x.astype(BF16)

    z_ref[...] = _dot(hb_ref[...], win_ref[...])

    def zcol(r0, i, n=rb):
        return z_ref[r0:r0 + n, i * dq:(i + 1) * dq]

    chunk_rows = range(0, tl, CHUNK)

    for r0 in chunk_rows:
        ext_a[0, HIST_A + r0:HIST_A + r0 + rb, :] = zcol(r0, 0) * jax.nn.sigmoid(zcol(r0, 1))
    n_ext = HIST_A + tl
    for r0 in range(0, n_ext - SUBLANES, rb):
        n = min(rb, n_ext - SUBLANES - r0)
        blk = ext_a[0, r0:r0 + n + SUBLANES, :]
        for s in range(1, SUBLANES):
            ext_a[s, r0:r0 + n, :] = pltpu.roll(blk, n + SUBLANES - s, axis=0)[0:n, :]
    off_a = HIST_A - (CONV_A_W - 1)
    for q0 in range(0, tl, 64):
        acc = jnp.zeros((64, dq), F32)
        for k in range(CONV_A_W):
            s = (off_a + k) % SUBLANES
            row = off_a + k - s + q0
            acc = acc + ext_a[s, row:row + 64, :] * caw_ref[k:k + 1, :]
        y = _silu(_ln(acc + cab_ref[...], lag_ref[...], lab_ref[...]))
        cat_ref[q0:q0 + 64, 0:dq] = y.astype(BF16)

    hd = dq // N_HEADS_B
    lane_head = lax.broadcasted_iota(I32, (CHUNK, dq), 1) // hd
    for r0 in chunk_rows:
        v = _ln(_gelu(zcol(r0, 3)), lvg_ref[...], lvb_ref[...])
        sm = bsm_ref[...]
        for hh in range(N_HEADS_B):
            vm = jnp.where(lane_head == hh, v, 0.0).astype(BF16)
            sm = sm + _dot(ws_ref[hh], vm)
        cat_ref[r0:r0 + CHUNK, dq:2 * dq] = (_gelu(zcol(r0, 2)) * sm).astype(BF16)
        if r0 == tl - CHUNK:
            @pl.when(j == nj - 1)
            def _(v=v):
                v_out[0] = v

    off_c = HIST_C - (CONV_C_W - 1)
    for r0 in chunk_rows:
        ext_c[HIST_C + r0:HIST_C + r0 + rb, :] = zcol(r0, 6) * zcol(r0, 4)
    for r0 in chunk_rows:
        conv_c = jnp.zeros((CHUNK, dq), F32)
        for k in range(CONV_C_W):
            conv_c = conv_c + ext_c[off_c + r0 + k:off_c + r0 + k + CHUNK, :] * ccw_ref[k:k + 1, :]
        cat_ref[r0:r0 + CHUNK, 2 * dq:3 * dq] = (zcol(r0, 5) * conv_c).astype(BF16)

    gc = dq // len(POOL_WINDOWS)
    lane_grp = lax.broadcasted_iota(I32, (CHUNK, dq), 1) // gc
    lead = 2 * SUBLANES
    for r0 in chunk_rows:
        ext_d[HIST_D + r0:HIST_D + r0 + rb, :] = zcol(r0, 7)
    for r0 in chunk_rows:
        e = ext_d[HIST_D + r0 - lead:HIST_D + r0 + CHUNK, :]
        s2 = e + pltpu.roll(e, 1, axis=0)
        s4 = s2 + pltpu.roll(s2, 2, axis=0)
        s8 = s4 + pltpu.roll(s4, 4, axis=0)
        s16 = s8 + pltpu.roll(s8, 8, axis=0)
        sums = (s2, s4, s8, s16)
        pos = (j * tl + r0 + lax.broadcasted_iota(I32, (CHUNK, 1), 0)).astype(F32)
        mean = jnp.zeros((CHUNK, dq), F32)
        for g, w in enumerate(POOL_WINDOWS):
            inv = 1.0 / jnp.minimum(pos + 1.0, float(w))
            mean = jnp.where(lane_grp == g, sums[g][lead:lead + CHUNK, :] * inv, mean)
        dd = (mean - e[lead:lead + CHUNK, :]).astype(BF16)
        cat_ref[r0:r0 + CHUNK, 3 * dq:4 * dq] = (_dot(dd, wpool_ref[...]) * pscale_ref[...]).astype(BF16)

    mix_ref[...] = _dot(cat_ref[...], wout_ref[...])
    for r0 in range(0, tl, 64):
        h_out[0, r0:r0 + 64, :] = _ln(alpha * resid(r0, 64) + mix_ref[r0:r0 + 64, :], lmg_ref[...], lmb_ref[...])

    @pl.when(j == nj - 1)
    def _():
        sa_out[0] = ext_a[0, HIST_A + tl - (CONV_A_W - 1):HIST_A + tl, :]
        sc_out[0] = ext_c[HIST_C + tl - (CONV_C_W - 1):HIST_C + tl, :]
        sd_out[0] = ext_d[HIST_D + tl - POOL_PAST:HIST_D + tl, :]

    ext_a[0, 0:HIST_A, :] = ext_a[0, tl:tl + HIST_A, :]
    ext_c[0:HIST_C, :] = ext_c[tl:tl + HIST_C, :]
    ext_d[0:HIST_D, :] = ext_d[tl:tl + HIST_D, :]


def _full_spec(arr):
    nd = arr.ndim
    return pl.BlockSpec(arr.shape, lambda *_: (0,) * nd)


def _mixer_prompt(x, lw, first_layer, alpha):
    bsz, seq, d = x.shape
    d_mix = lw['w_out'].shape[0]
    dq = d_mix // 4
    tl = TL_PROMPT
    assert seq % tl == 0 and tl % CHUNK == 0 and seq >= CHUNK
    params = [lw['ln_in_g'], lw['ln_in_b'], lw['w_in'], lw['conv_a_w'], lw['conv_a_b'], lw['ln_a_g'],
              lw['ln_a_b'], lw['ln_v_g'], lw['ln_v_b'], lw['ws_tril'], lw['bs_mat'], lw['conv_c_w'],
              lw['w_pool_bd'], lw['pool_scale'], lw['w_out'], lw['ln_mix_g'], lw['ln_mix_b']]
    out_shape = (
        jax.ShapeDtypeStruct((bsz, seq, d), F32),
        jax.ShapeDtypeStruct((bsz, CONV_A_W - 1, dq), F32),
        jax.ShapeDtypeStruct((bsz, CONV_C_W - 1, dq), F32),
        jax.ShapeDtypeStruct((bsz, POOL_PAST, dq), F32),
        jax.ShapeDtypeStruct((bsz, CHUNK, dq), F32),
    )
    state_spec = lambda rows: pl.BlockSpec((1, rows, dq), lambda b, j: (b, 0, 0))
    return pl.pallas_call(
        functools.partial(_mixer_prompt_kernel, first_layer, alpha, tl, d_mix),
        grid=(bsz, seq // tl),
        in_specs=[pl.BlockSpec((1, tl, d), lambda b, j: (b, j, 0))] + [_full_spec(p) for p in params],
        out_specs=(pl.BlockSpec((1, tl, d), lambda b, j: (b, j, 0)),
                   state_spec(CONV_A_W - 1), state_spec(CONV_C_W - 1), state_spec(POOL_PAST),
                   state_spec(CHUNK)),
        out_shape=out_shape,
        scratch_shapes=[pltpu.VMEM((SUBLANES, HIST_A + tl, dq), F32), pltpu.VMEM((HIST_C + tl, dq), F32),
                        pltpu.VMEM((HIST_D + tl, dq), F32), pltpu.VMEM((tl, d_mix), BF16),
                        pltpu.VMEM((tl, lw['w_in'].shape[1]), F32), pltpu.VMEM((tl, d), BF16),
                        pltpu.VMEM((tl, d), F32)],
        compiler_params=_cparams(("arbitrary", "arbitrary")),
        name="mixer_prompt",
    )(x, *params)


def _mixer_sample_kernel(first_layer, alpha, n_seq, n_step, d_mix,
                         x_ref, sa_ref, sc_ref, sd_ref, lng_ref, lnb_ref, win_ref, caw_ref, cab_ref,
                         lag_ref, lab_ref, lvg_ref, lvb_ref, wsv_ref, bsv_ref, ccw_ref, wpool_ref,
                         pscale_ref, wout_ref, lmg_ref, lmb_ref,
                         h_out, sa_out, sc_out, sd_out, v_out, cat_ref):
    dq = d_mix // 4
    x = x_ref[...]
    h = _ln(x, lng_ref[...], lnb_ref[...]) if first_layer else x
    hb = h.astype(BF16)

    def proj(i):
        return _dot(hb, win_ref[:, i * dq:(i + 1) * dq])

    def slab(val, i):
        return val[i * n_seq:(i + 1) * n_seq, :]

    def ext_slabs(state_ref, n_past, cur):
        return ([state_ref[i * n_seq:(i + 1) * n_seq, :] for i in range(n_past)]
                + [slab(cur, i) for i in range(n_step)])

    def store_state(out_ref, slabs, n_keep):
        for i, sl in enumerate(slabs[len(slabs) - n_keep:]):
            out_ref[i * n_seq:(i + 1) * n_seq, :] = sl

    a_glu = proj(0) * jax.nn.sigmoid(proj(1))
    ea = ext_slabs(sa_ref, CONV_A_W - 1, a_glu)
    for l in range(n_step):
        acc = jnp.zeros((n_seq, dq), F32)
        for k in range(CONV_A_W):
            acc = acc + ea[l + k] * caw_ref[k:k + 1, :]
        y = _silu(_ln(acc + cab_ref[...], lag_ref[...], lab_ref[...]))
        cat_ref[l * n_seq:(l + 1) * n_seq, 0:dq] = y.astype(BF16)
    store_state(sa_out, ea, CONV_A_W - 1)

    u = _gelu(proj(2))
    v = _ln(_gelu(proj(3)), lvg_ref[...], lvb_ref[...])
    v_out[...] = v
    for l in range(n_step):
        s = jnp.zeros((n_seq, dq), F32) + bsv_ref[l:l + 1, :]
        for m in range(l + 1):
            s = s + slab(v, m) * wsv_ref[l * n_step + m:l * n_step + m + 1, :]
        cat_ref[l * n_seq:(l + 1) * n_seq, dq:2 * dq] = (slab(u, l) * s).astype(BF16)

    c_x = proj(4)
    c_b = proj(5)
    c_c = proj(6)
    gx = c_c * c_x
    ec = ext_slabs(sc_ref, CONV_C_W - 1, gx)
    for l in range(n_step):
        acc = jnp.zeros((n_seq, dq), F32)
        for k in range(CONV_C_W):
            acc = acc + ec[l + k] * ccw_ref[k:k + 1, :]
        cat_ref[l * n_seq:(l + 1) * n_seq, 2 * dq:3 * dq] = (slab(c_b, l) * acc).astype(BF16)
    store_state(sc_out, ec, CONV_C_W - 1)

    d_in = proj(7)
    ed = ext_slabs(sd_ref, POOL_PAST, d_in)
    memo = {}

    def wsum(i, w):
        if i < 0:
            return None
        if w == 1:
            return ed[i]
        if (i, w) not in memo:
            a, b = wsum(i, w // 2), wsum(i - w // 2, w // 2)
            memo[(i, w)] = a if b is None else a + b
        return memo[(i, w)]

    gc = dq // len(POOL_WINDOWS)
    lane_grp = lax.broadcasted_iota(I32, (n_seq, dq), 1) // gc
    for l in range(n_step):
        mean = jnp.zeros((n_seq, dq), F32)
        for g, w in enumerate(POOL_WINDOWS):
            count = min(PAST_LEN + l + 1, w)
            mean = jnp.where(lane_grp == g, wsum(POOL_PAST + l, w) * (1.0 / count), mean)
        dd = (mean - slab(d_in, l)).astype(BF16)
        cat_ref[l * n_seq:(l + 1) * n_seq, 3 * dq:4 * dq] = (
            _dot(dd, wpool_ref[...]) * pscale_ref[...]).astype(BF16)
    store_state(sd_out, ed, POOL_PAST)

    mix = _dot(cat_ref[...], wout_ref[...])
    h_out[...] = _ln(alpha * h + mix, lmg_ref[...], lmb_ref[...])


def _mixer_sample(x_tm, sa, sc, sd, lw, first_layer, alpha, n_seq, n_step):
    rows, d = x_tm.shape
    d_mix = lw['w_out'].shape[0]
    dq = d_mix // 4
    assert n_seq % SUBLANES == 0 and n_step <= CHUNK and PAST_LEN % CHUNK == 0
    ins = [x_tm, sa, sc, sd, lw['ln_in_g'], lw['ln_in_b'], lw['w_in'], lw['conv_a_w'], lw['conv_a_b'],
           lw['ln_a_g'], lw['ln_a_b'], lw['ln_v_g'], lw['ln_v_b'], lw['ws_vec'], lw['bs_vec'],
           lw['conv_c_w'], lw['w_pool_bd'], lw['pool_scale'], lw['w_out'], lw['ln_mix_g'], lw['ln_mix_b']]
    out_shape = (
        jax.ShapeDtypeStruct((rows, d), F32),
        jax.ShapeDtypeStruct(((CONV_A_W - 1) * n_seq, dq), F32),
        jax.ShapeDtypeStruct(((CONV_C_W - 1) * n_seq, dq), F32),
        jax.ShapeDtypeStruct((POOL_PAST * n_seq, dq), F32),
        jax.ShapeDtypeStruct((rows, dq), F32),
    )
    return pl.pallas_call(
        functools.partial(_mixer_sample_kernel, first_layer, alpha, n_seq, n_step, d_mix),
        grid=(1,),
        in_specs=[_full_spec(a) for a in ins],
        out_specs=tuple(pl.BlockSpec(s.shape, lambda i: (0, 0)) for s in out_shape),
        out_shape=out_shape,
        scratch_shapes=[pltpu.VMEM((rows, d_mix), BF16)],
        compiler_params=_cparams(("arbitrary",)),
        name="mixer_sample",
    )(*ins)


def _ffn_dense_kernel(alpha, tf, x_ref, wg_hbm, wu_hbm, wd_hbm, g_ref, b_ref, o_ref, wg_ref, wu_ref, wd_ref, sem):
    @pl.when(pl.program_id(0) == 0)
    def _():
        copies = [pltpu.make_async_copy(src, dst, sem.at[n])
                  for n, (src, dst) in enumerate(((wg_hbm, wg_ref), (wu_hbm, wu_ref), (wd_hbm, wd_ref)))]
        for cp in copies:
            cp.start()
        for cp in copies:
            cp.wait()

    x = x_ref[...]
    xb = x.astype(BF16)
    ffn = None
    for c0 in range(0, wg_ref.shape[1], tf):
        mid = _silu(_dot(xb, wg_ref[:, c0:c0 + tf])) * _dot(xb, wu_ref[:, c0:c0 + tf])
        part = _dot(mid.astype(BF16), wd_ref[c0:c0 + tf, :])
        ffn = part if ffn is None else ffn + part
    o_ref[...] = _ln(alpha * x + ffn, g_ref[...], b_ref[...])


def _ffn_dense(x, wg, wu, wd, g, b, alpha, tm, tf):
    t, d = x.shape
    dff = wg.shape[1]
    assert t % tm == 0 and dff % tf == 0
    any_spec = pl.BlockSpec(memory_space=pl.ANY)
    return pl.pallas_call(
        functools.partial(_ffn_dense_kernel, alpha, tf),
        grid=(t // tm,),
        in_specs=[pl.BlockSpec((tm, d), lambda i: (i, 0)), any_spec, any_spec, any_spec,
                  pl.BlockSpec((1, d), lambda i: (0, 0)),
                  pl.BlockSpec((1, d), lambda i: (0, 0))],
        out_specs=pl.BlockSpec((tm, d), lambda i: (i, 0)),
        out_shape=jax.ShapeDtypeStruct((t, d), F32),
        scratch_shapes=[pltpu.VMEM(wg.shape, BF16), pltpu.VMEM(wu.shape, BF16), pltpu.VMEM(wd.shape, BF16),
                        pltpu.SemaphoreType.DMA((3,))],
        compiler_params=_cparams(("arbitrary",)),
        name="ffn_dense",
    )(x, wg, wu, wd, g, b)


def _router_kernel(tm, h_ref, wcat_ref, cin_ref, mi_ref, mf_ref, cnt_ref, carry_ref, before_ref):
    i = pl.program_id(0)

    @pl.when(i == 0)
    def _():
        carry_ref[...] = cin_ref[...].astype(F32)
        row = lax.broadcasted_iota(I32, (tm, tm), 0)
        col = lax.broadcasted_iota(I32, (tm, tm), 1)
        before_ref[...] = jnp.where(row < col, 1.0, 0.0).astype(BF16)

    h = h_ref[...]
    h_hi = h.astype(BF16)
    h_lo = (h - h_hi.astype(F32)).astype(BF16)
    p_hi = _dot(h_hi, wcat_ref[...])
    p_lo = _dot(h_lo, wcat_ref[...])
    logits = (p_hi[:, 0:LANES] + (p_lo[:, 0:LANES] + p_hi[:, LANES:2 * LANES])) + p_lo[:, LANES:2 * LANES]
    lg = logits.T[0:N_EXPERTS, :]
    ex = lax.broadcasted_iota(I32, (N_EXPERTS, tm), 0).astype(F32)
    m1 = jnp.max(lg, axis=0, keepdims=True)
    i1 = jnp.min(jnp.where(lg == m1, ex, float(N_EXPERTS)), axis=0, keepdims=True)
    sel1 = ex == i1
    rest = jnp.where(sel1, -jnp.inf, lg)
    m2 = jnp.max(rest, axis=0, keepdims=True)
    i2 = jnp.min(jnp.where(rest == m2, ex, float(N_EXPERTS)), axis=0, keepdims=True)
    sel2 = ex == i2
    e2 = jnp.exp(m2 - m1)
    den = 1.0 + e2
    g1 = 1.0 / den
    g2 = e2 / den

    sel = jnp.where(sel1 | sel2, 1.0, 0.0)
    sel_pad = jnp.concatenate([sel, jnp.zeros_like(sel)], axis=0).astype(BF16)
    base = _dot(sel_pad, before_ref[...])[0:N_EXPERTS, :] + carry_ref[:, 0:1]
    r1 = jnp.sum(jnp.where(sel1, base, 0.0), axis=0, keepdims=True)
    r2 = jnp.sum(jnp.where(sel2, base, 0.0), axis=0, keepdims=True)
    carry_ref[...] = carry_ref[...] + jnp.sum(sel, axis=1, keepdims=True)

    meta = jnp.where(ex == 0, i1, jnp.where(ex == 1, i2, jnp.where(ex == 2, r1, r2)))
    mi_ref[...] = meta.astype(I32)
    gt = jnp.where(ex == 0, g1, jnp.where(ex == 1, g2, 0.0))
    gt = jnp.concatenate([gt, jnp.zeros((LANES - N_EXPERTS, tm), F32)], axis=0)
    mf_ref[...] = gt.T[:, 0:SUBLANES]
    cnt_ref[...] = carry_ref[...].astype(I32)


def _router(h, wr_cat, counts_in):
    t, d = h.shape
    tm = min(TM_ROUTE, t)
    assert t % tm == 0 and N_EXPERTS == SUBLANES
    return pl.pallas_call(
        functools.partial(_router_kernel, tm),
        grid=(t // tm,),
        in_specs=[pl.BlockSpec((tm, d), lambda i: (i, 0)), _full_spec(wr_cat), _full_spec(counts_in)],
        out_specs=(pl.BlockSpec((SUBLANES, tm), lambda i: (0, i)),
                   pl.BlockSpec((tm, SUBLANES), lambda i: (i, 0)),
                   pl.BlockSpec((SUBLANES, LANES), lambda i: (0, 0))),
        out_shape=(jax.ShapeDtypeStruct((SUBLANES, t), I32),
                   jax.ShapeDtypeStruct((t, SUBLANES), F32),
                   jax.ShapeDtypeStruct((SUBLANES, LANES), I32)),
        scratch_shapes=[pltpu.VMEM((SUBLANES, LANES), F32), pltpu.VMEM((tm, tm), BF16)],
        compiler_params=_cparams(("arbitrary",)),
        name="router",
    )(h, wr_cat, counts_in)


def _row_copy(src, src_row, dst, dst_row, sem):
    return pltpu.make_async_copy(src.at[pl.ds(src_row, 1), :], dst.at[pl.ds(dst_row, 1), :], sem)


def _rows_wait(src, dst, n_rows, sem):
    pltpu.make_async_copy(src.at[pl.ds(0, n_rows), :], dst.at[pl.ds(0, n_rows), :], sem).wait()


def _dispatch_kernel(tm, tm_expert, n_blocks, blk_ranges, zi_ref, dest_ref, *rest):
    n_src = len(blk_ranges)
    h_refs = rest[:n_src]
    xs_ref, xbuf, zblk, lsems, rsems, zsem = rest[n_src:]
    i = pl.program_id(0)
    n = pl.num_programs(0)

    def block_load(g, start):
        slot = lax.rem(g, N_STAGE)
        for h_ref, (b0, b1) in zip(h_refs, blk_ranges):
            @pl.when((g >= b0) & (g < b1))
            def _(h_ref=h_ref, b0=b0):
                cp = pltpu.make_async_copy(h_ref.at[pl.ds((g - b0) * tm, tm), :], xbuf.at[slot],
                                           lsems.at[slot])
                if start:
                    cp.start()
                else:
                    cp.wait()

    def rows_wait(g):
        slot = lax.rem(g, N_STAGE)
        for _k in range(TOP_K):
            _rows_wait(xbuf.at[slot], xs_ref, tm, rsems.at[slot])

    @pl.when(i == 0)
    def _():
        block_load(i, True)

    @pl.when(i >= N_STAGE - 1)
    def _():
        rows_wait(i - (N_STAGE - 1))

    @pl.when(i + 1 < n)
    def _():
        block_load(i + 1, True)

    block_load(i, False)
    cur = lax.rem(i, N_STAGE)

    def issue(j, carry):
        r0 = pl.multiple_of(j * SUBLANES, SUBLANES)
        for u in range(SUBLANES):
            for k in range(TOP_K):
                dst = dest_ref[0, 0, j * SUBLANES + (k * tm + u)]
                _row_copy(xbuf.at[cur], r0 + u, xs_ref, dst, rsems.at[cur]).start(priority=k)
        return carry

    lax.fori_loop(0, tm // SUBLANES, issue, 0)

    @pl.when(i == n - 1)
    def _():
        for back in range(N_STAGE - 2, -1, -1):
            @pl.when(i >= back)
            def _(back=back):
                rows_wait(i - back)

    @pl.when(i == 0)
    def _():
        zblk[...] = jnp.zeros_like(zblk)
        for e in range(N_EXPERTS):
            lo = zi_ref[e] + zi_ref[N_EXPERTS + e]
            hi = zi_ref[e] + zi_ref[2 * N_EXPERTS + e]

            def zissue(r, carry):
                _row_copy(zblk, 0, xs_ref, r, zsem).start()
                return carry

            def zdrain(r, carry):
                _row_copy(zblk, 0, xs_ref, r, zsem).wait()
                return carry

            lax.fori_loop(lo, hi, zissue, 0)
            lax.fori_loop(lo, hi, zdrain, 0)

        def bcopy(b):
            return pltpu.make_async_copy(zblk, xs_ref.at[pl.ds(b * tm_expert, tm_expert), :], zsem)

        def bissue(b, carry):
            bcopy(b).start()
            return carry

        def bdrain(b, carry):
            bcopy(b).wait()
            return carry

        lax.fori_loop(zi_ref[3 * N_EXPERTS], n_blocks, bissue, 0)
        lax.fori_loop(zi_ref[3 * N_EXPERTS], n_blocks, bdrain, 0)


def _dispatch(hs_list, dest_blk, zinfo, n_blocks, tm_expert):
    d = hs_list[0].shape[1]
    nb = dest_blk.shape[0]
    tm = dest_blk.shape[2] // TOP_K
    blk_ranges, b0 = [], 0
    for h in hs_list:
        assert h.shape[0] % tm == 0 and h.shape[0] >= tm
        blk_ranges.append((b0, b0 + h.shape[0] // tm))
        b0 = blk_ranges[-1][1]
    assert b0 == nb
    any_spec = pl.BlockSpec(memory_space=pl.ANY)
    return pl.pallas_call(
        functools.partial(_dispatch_kernel, tm, tm_expert, n_blocks, tuple(blk_ranges)),
        grid_spec=pltpu.PrefetchScalarGridSpec(
            num_scalar_prefetch=1,
            grid=(nb,),
            in_specs=[pl.BlockSpec((1, 1, TOP_K * tm), lambda i, zi: (i, 0, 0), memory_space=pltpu.SMEM)]
            + [any_spec] * len(hs_list),
            out_specs=any_spec,
            scratch_shapes=[pltpu.VMEM((N_STAGE, tm, d), F32), pltpu.VMEM((tm_expert, d), F32),
                            pltpu.SemaphoreType.DMA((N_STAGE,)), pltpu.SemaphoreType.DMA((N_STAGE,)),
                            pltpu.SemaphoreType.DMA(())],
        ),
        out_shape=jax.ShapeDtypeStruct((n_blocks * tm_expert, d), F32),
        compiler_params=_cparams(("arbitrary",)),
        name="moe_dispatch",
    )(zinfo, dest_blk, *hs_list)


def _ffn_moe_kernel(tf, be_ref, nu_ref, x_ref, wg_hbm, wu_hbm, wd_hbm, y_ref,
                    wg_ref, wu_ref, wd_ref, stage_in, stage_out, sems):
    b = pl.program_id(0)
    d, dff = wg_ref.shape
    rows_in = stage_in.shape[1]
    rows_out = stage_out.shape[1]
    n_in, n_out = d // rows_in, dff // rows_out
    e = be_ref[b]
    fresh = (b == 0) | (e != be_ref[jnp.maximum(b - 1, 0)])

    @pl.when((b < nu_ref[0]) & fresh)
    def _():
        n_slots = stage_in.shape[0]

        def chunk(n):
            slot = n % n_slots
            if n < 2 * n_in:
                src, dst = (wg_hbm, wg_ref) if n < n_in else (wu_hbm, wu_ref)
                rows = pl.ds((n % n_in) * rows_in, rows_in)
                stage = stage_in.at[slot]
            else:
                src, dst = wd_hbm, wd_ref
                rows = pl.ds((n - 2 * n_in) * rows_out, rows_out)
                stage = stage_out.at[slot]
            return pltpu.make_async_copy(src.at[e, rows, :], stage, sems.at[slot]), stage, dst.at[rows, :]

        n_chunks = 2 * n_in + n_out
        for n in range(n_slots - 1):
            chunk(n)[0].start()
        for n in range(n_chunks):
            if n + n_slots - 1 < n_chunks:
                chunk(n + n_slots - 1)[0].start()
            copy, staged, dst = chunk(n)
            copy.wait()
            dst[...] = staged[...].astype(BF16)

    @pl.when(b < nu_ref[0])
    def _():
        xb = x_ref[...].astype(BF16)
        ffn = None
        for c0 in range(0, dff, tf):
            mid = _silu(_dot(xb, wg_ref[:, c0:c0 + tf])) * _dot(xb, wu_ref[:, c0:c0 + tf])
            part = _dot(mid.astype(BF16), wd_ref[c0:c0 + tf, :])
            ffn = part if ffn is None else ffn + part
        y_ref[...] = ffn

    @pl.when(b >= nu_ref[0])
    def _():
        y_ref[...] = jnp.zeros_like(y_ref)


def _ffn_moe(xs, wg, wu, wd, block_e, n_used, tm, tf):
    cap, d = xs.shape
    dff = wg.shape[2]
    rows_in, rows_out = 128, 512
    assert cap % tm == 0 and dff % tf == 0 and d % rows_in == 0 and dff % rows_out == 0
    any_spec = pl.BlockSpec(memory_space=pl.ANY)
    return pl.pallas_call(
        functools.partial(_ffn_moe_kernel, tf),
        grid_spec=pltpu.PrefetchScalarGridSpec(
            num_scalar_prefetch=2,
            grid=(cap // tm,),
            in_specs=[pl.BlockSpec((tm, d), lambda b, be, nu: (jnp.minimum(b, nu[0] - 1), 0)),
                      any_spec, any_spec, any_spec],
            out_specs=pl.BlockSpec((tm, d), lambda b, be, nu: (b, 0)),
            scratch_shapes=[pltpu.VMEM((d, dff), BF16), pltpu.VMEM((d, dff), BF16), pltpu.VMEM((dff, d), BF16),
                            pltpu.VMEM((N_WSTAGE, rows_in, dff), F32), pltpu.VMEM((N_WSTAGE, rows_out, d), F32),
                            pltpu.SemaphoreType.DMA((N_WSTAGE,))],
        ),
        out_shape=jax.ShapeDtypeStruct((cap, d), F32),
        compiler_params=_cparams(("arbitrary",)),
        name="ffn_moe",
    )(block_e, n_used, xs, wg, wu, wd)


def _combine_kernel(alpha, tm, dcur_ref, dnext_ref, h_ref, gate_ref, g_ref, b_ref, ys_ref, o_ref, ybuf, sems):
    i = pl.program_id(0)
    n = pl.num_programs(0)

    def issue(dref, slot):
        def body(j, carry):
            r0 = pl.multiple_of(j * SUBLANES, SUBLANES)
            for u in range(SUBLANES):
                for k in range(TOP_K):
                    src = dref[0, 0, j * SUBLANES + (k * tm + u)]
                    _row_copy(ys_ref, src, ybuf.at[slot, k], r0 + u, sems.at[slot]).start(priority=k)
            return carry

        lax.fori_loop(0, tm // SUBLANES, body, 0)

    @pl.when(i == 0)
    def _():
        issue(dcur_ref, 0)

    @pl.when(i + 1 < n)
    def _():
        issue(dnext_ref, (i + 1) % 2)

    slot = i % 2
    for k in range(TOP_K):
        _rows_wait(ys_ref, ybuf.at[slot, k], tm, sems.at[slot])

    gates = gate_ref[...]
    ffn = gates[:, 0:1] * ybuf[slot, 0] + gates[:, 1:2] * ybuf[slot, 1]
    o_ref[...] = _ln(alpha * h_ref[...] + ffn, g_ref[...], b_ref[...])


def _combine(h, dest_blk, gates, ys, g, b, alpha):
    t, d = h.shape
    nb = dest_blk.shape[0]
    tm = dest_blk.shape[2] // TOP_K
    dest_spec = lambda imap: pl.BlockSpec((1, 1, TOP_K * tm), imap, memory_space=pltpu.SMEM)
    return pl.pallas_call(
        functools.partial(_combine_kernel, alpha, tm),
        grid=(nb,),
        in_specs=[dest_spec(lambda i: (i, 0, 0)),
                  dest_spec(lambda i: (jnp.minimum(i + 1, nb - 1), 0, 0)),
                  pl.BlockSpec((tm, d), lambda i: (i, 0)),
                  pl.BlockSpec((tm, SUBLANES), lambda i: (i, 0)),
                  pl.BlockSpec((1, d), lambda i: (0, 0)),
                  pl.BlockSpec((1, d), lambda i: (0, 0)),
                  pl.BlockSpec(memory_space=pl.ANY)],
        out_specs=pl.BlockSpec((tm, d), lambda i: (i, 0)),
        out_shape=jax.ShapeDtypeStruct((t, d), F32),
        scratch_shapes=[pltpu.VMEM((2, TOP_K, tm, d), F32), pltpu.SemaphoreType.DMA((2,))],
        compiler_params=_cparams(("arbitrary",)),
        name="moe_combine",
    )(dest_blk, dest_blk, h, gates, g, b, ys)


def _ffn_routed(hs_list, mw, g, b, alpha, tm_expert):
    d = hs_list[0].shape[1]
    counts_in = jnp.zeros((SUBLANES, LANES), I32)
    routed = []
    for h in hs_list:
        meta_i, gates, counts_in = _router(h, mw['w_router_cat'], counts_in)
        routed.append((meta_i, gates))
    counts = counts_in[:N_EXPERTS, 0]
    padded = (counts + tm_expert - 1) // tm_expert * tm_expert
    pad_end = jnp.cumsum(padded)
    pad_start = (pad_end - padded).astype(I32)
    n_assign = sum(h.shape[0] for h in hs_list) * TOP_K
    n_blocks = -(-n_assign // tm_expert) + N_EXPERTS
    cap = n_blocks * tm_expert
    n_used = (pad_end[-1] // tm_expert).astype(I32)
    blk_start = jnp.minimum(jnp.arange(n_blocks, dtype=I32), n_used - 1) * tm_expert
    block_e = jnp.minimum(jnp.sum(blk_start[:, None] >= pad_end[None, :], axis=1), N_EXPERTS - 1).astype(I32)
    zinfo = jnp.concatenate([pad_start, counts, padded, n_used.reshape(1)]).astype(I32)

    dests = []
    for h, (meta_i, _) in zip(hs_list, routed):
        dest = meta_i[TOP_K:2 * TOP_K, :]
        for e in range(N_EXPERTS):
            dest = dest + jnp.where(meta_i[0:TOP_K, :] == e, pad_start[e], 0)
        nb = h.shape[0] // TM_MOVE
        dests.append(dest.reshape(TOP_K, nb, TM_MOVE).transpose(1, 0, 2).reshape(nb, 1, TOP_K * TM_MOVE))
    xs = _dispatch(hs_list, jnp.concatenate(dests, axis=0), zinfo, n_blocks, tm_expert)
    ys = _ffn_moe(xs, mw['w_e_gate'], mw['w_e_up'], mw['w_e_down'], block_e, n_used.reshape(1), tm_expert,
                  TF_EXPERT)
    return [_combine(h, dest_blk, gates, ys, g, b, alpha)
            for h, dest_blk, (_, gates) in zip(hs_list, dests, routed)]


def _prep_layer(i, p, n_step):
    dq = p['conv_a_w'].shape[2]
    hd = dq // N_HEADS_B
    tril = jnp.tril(jnp.ones((CHUNK, CHUNK), dtype=bool))
    w_s = p['w_s'][i]
    b_s = p['b_s'][i]
    wp = p['w_pool'][i]
    ng, gc = wp.shape[0], wp.shape[1]
    w_pool_bd = (jnp.eye(ng, dtype=F32)[:, None, :, None] * wp[:, :, None, :]).reshape(ng * gc, ng * gc)
    row = lambda a: a.reshape(1, -1)
    return dict(
        ln_in_g=row(p['ln_in_g']), ln_in_b=row(p['ln_in_b']),
        w_in=p['w_in'][i].astype(BF16),
        conv_a_w=p['conv_a_w'][i], conv_a_b=row(p['conv_a_b'][i]),
        ln_a_g=row(p['ln_a_g'][i]), ln_a_b=row(p['ln_a_b'][i]),
        ln_v_g=row(p['ln_v_g'][i]), ln_v_b=row(p['ln_v_b'][i]),
        ws_tril=jnp.where(tril[None], w_s, 0).astype(BF16),
        bs_mat=jnp.repeat(b_s.T, hd, axis=1),
        ws_vec=jnp.repeat(jnp.where(tril[None], w_s, 0)[:, :n_step, :n_step].transpose(1, 2, 0)
                          .reshape(n_step * n_step, N_HEADS_B), hd, axis=1),
        bs_vec=jnp.repeat(b_s[:, :n_step].T, hd, axis=1),
        conv_c_w=p['conv_c_w'][i],
        w_pool_bd=w_pool_bd.astype(BF16), pool_scale=row(p['pool_scale'][i]),
        w_out=p['w_out'][i].astype(BF16),
        ln_mix_g=row(p['ln_mix_g'][i]), ln_mix_b=row(p['ln_mix_b'][i]),
    )


def kernel(x_prompt, x_sample, state_conv_a, state_conv_c, state_pool_d, ln_in_g, ln_in_b, w_in, conv_a_w,
           conv_a_b, ln_a_g, ln_a_b, ln_v_g, ln_v_b, w_s, b_s, conv_c_w, w_pool, pool_scale, w_out,
           ln_mix_g, ln_mix_b, w_ff_gate, w_ff_up, w_ff_down, w_router, w_e_gate, w_e_up, w_e_down,
           ln_ffn_g, ln_ffn_b):
    p = dict(ln_in_g=ln_in_g, ln_in_b=ln_in_b, w_in=w_in, conv_a_w=conv_a_w, conv_a_b=conv_a_b,
             ln_a_g=ln_a_g, ln_a_b=ln_a_b, ln_v_g=ln_v_g, ln_v_b=ln_v_b, w_s=w_s, b_s=b_s,
             conv_c_w=conv_c_w, w_pool=w_pool, pool_scale=pool_scale, w_out=w_out,
             ln_mix_g=ln_mix_g, ln_mix_b=ln_mix_b)
    depth = w_in.shape[0]
    bsz, seq, d = x_prompt.shape
    n_seq, n_step, _ = x_sample.shape
    dq = conv_a_w.shape[2]
    alpha = float((2.0 * depth) ** 0.25)

    hp = x_prompt
    hs = x_sample.transpose(1, 0, 2).reshape(n_step * n_seq, d)
    outs = {k: [] for k in ('pa', 'pc', 'pd', 'pv', 'sa', 'sc', 'sd', 'sv')}
    for i in range(depth):
        lw = _prep_layer(i, p, n_step)
        hp, pa, pc, pd, pv = _mixer_prompt(hp, lw, i == 0, alpha)
        tm_state = lambda s: s.transpose(1, 0, 2).reshape(-1, dq)
        hs, sa, sc, sd, sv = _mixer_sample(hs, tm_state(state_conv_a[i]), tm_state(state_conv_c[i]),
                                           tm_state(state_pool_d[i]), lw, i == 0, alpha, n_seq, n_step)
        lg, lb = ln_ffn_g[i].reshape(1, -1), ln_ffn_b[i].reshape(1, -1)
        j = i // 2
        if i % 2 == 0:
            wg, wu, wd = w_ff_gate[j].astype(BF16), w_ff_up[j].astype(BF16), w_ff_down[j].astype(BF16)
            hp = _ffn_dense(hp.reshape(bsz * seq, d), wg, wu, wd, lg, lb, alpha, TM_DENSE,
                            TF_DENSE).reshape(bsz, seq, d)
            hs = _ffn_dense(hs, wg, wu, wd, lg, lb, alpha, hs.shape[0], TF_DENSE)
        else:
            wr = jnp.pad(w_router[j], ((0, 0), (0, LANES - N_EXPERTS)))
            wr_hi = wr.astype(BF16)
            wr_lo = (wr - wr_hi.astype(F32)).astype(BF16)
            mw = dict(w_router_cat=jnp.concatenate([wr_hi, wr_lo], axis=1),
                      w_e_gate=w_e_gate[j], w_e_up=w_e_up[j], w_e_down=w_e_down[j])
            hp, hs = _ffn_routed([hp.reshape(bsz * seq, d), hs], mw, lg, lb, alpha, TM_EXPERT)
            hp = hp.reshape(bsz, seq, d)
        back = lambda a: a.reshape(-1, n_seq, dq).transpose(1, 0, 2)
        for k, val in zip(('pa', 'pc', 'pd', 'pv', 'sa', 'sc', 'sd', 'sv'),
                          (pa, pc, pd, pv, back(sa), back(sc), back(sd), back(sv))):
            outs[k].append(val)
    y_sample = hs.reshape(n_step, n_seq, d).transpose(1, 0, 2)
    st = lambda k: jnp.stack(outs[k])
    return (hp, y_sample, st('pa'), st('pc'), st('pd'), st('pv'), st('sa'), st('sc'), st('sd'), st('sv'))
```

```python
import functools

import numpy as np
import jax
import jax.numpy as jnp
from jax import lax
from jax.experimental import pallas as pl
from jax.experimental.pallas import tpu as pltpu

F32 = jnp.float32
BF16 = jnp.bfloat16
I32 = jnp.int32

PAST_LEN = 16384
CHUNK = 128
N_HEADS_B = 4
POOL_WINDOWS = (2, 4, 8, 16)
POOL_PAST = 15
CONV_A_W = 31
CONV_C_W = 3
N_EXPERTS = 8
TOP_K = 2
LN_EPS = 1e-5
SQRT_HALF = float(np.sqrt(0.5).astype(np.float32))

LANES = 128
SUBLANES = 8
VMEM_LIMIT_BYTES = 56 * 1024 * 1024

HIST_A = 32
HIST_C = 8
HIST_D = 24

TL_PROMPT = 512
TM_DENSE = 512
TF_DENSE = 1792
TF_EXPERT = 1792
TM_ROUTE = 512
TM_MOVE = 512
TM_EXPERT = 512
N_STAGE = 3
N_WSTAGE = 4


def _ln(x, g, b):
    mu = jnp.mean(x, axis=-1, keepdims=True)
    xc = x - mu
    var = jnp.mean(xc * xc, axis=-1, keepdims=True)
    return xc * lax.rsqrt(var + LN_EPS) * g + b


def _gelu(x):
    return 0.5 * x * (1.0 + lax.erf(x * SQRT_HALF))


def _silu(x):
    return x * jax.nn.sigmoid(x)


def _dot(a, b):
    return jnp.dot(a, b, preferred_element_type=F32)


def _cparams(sem):
    return pltpu.CompilerParams(dimension_semantics=sem, vmem_limit_bytes=VMEM_LIMIT_BYTES)


def _mixer_prompt_kernel(first_layer, alpha, tl, d_mix,
                         x_ref, lng_ref, lnb_ref, win_ref, caw_ref, cab_ref, lag_ref, lab_ref,
                         lvg_ref, lvb_ref, ws_ref, bsm_ref, ccw_ref, wpool_ref, pscale_ref,
                         wout_ref, lmg_ref, lmb_ref,
                         h_out, sa_out, sc_out, sd_out, v_out,
                         ext_a, ext_c, ext_d, cat_ref, z_ref, hb_ref, mix_ref):
    dq = d_mix // 4
    j = pl.program_id(1)
    nj = pl.num_programs(1)

    @pl.when(j == 0)
    def _():
        ext_a[0, 0:HIST_A, :] = jnp.zeros((HIST_A, dq), F32)
        ext_c[0:HIST_C, :] = jnp.zeros((HIST_C, dq), F32)
        ext_d[0:HIST_D, :] = jnp.zeros((HIST_D, dq), F32)

    d = x_ref.shape[2]
    rb = CHUNK

    def resid(r0, n):
        return h_out[0, r0:r0 + n, :] if first_layer else x_ref[0, r0:r0 + n, :]

    for r0 in range(0, tl, rb):
        x = x_ref[0, r0:r0 + rb, :]
        if first_layer:
            x = _ln(x, lng_ref[...], lnb_ref[...])
            h_out[0, r0:r0 + rb, :] = x
        hb_ref[r0:r0 + rb, :] = x.astype(BF16)

    z_ref[...] = _dot(hb_ref[...], win_ref[...])

    def zcol(r0, i, n=rb):
        return z_ref[r0:r0 + n, i * dq:(i + 1) * dq]

    chunk_rows = range(0, tl, CHUNK)

    for r0 in chunk_rows:
        ext_a[0, HIST_A + r0:HIST_A + r0 + rb, :] = zcol(r0, 0) * jax.nn.sigmoid(zcol(r0, 1))
    n_ext = HIST_A + tl
    for r0 in range(0, n_ext - SUBLANES, rb):
        n = min(rb, n_ext - SUBLANES - r0)
        blk = ext_a[0, r0:r0 + n + SUBLANES, :]
        for s in range(1, SUBLANES):
            ext_a[s, r0:r0 + n, :] = pltpu.roll(blk, n + SUBLANES - s, axis=0)[0:n, :]
    off_a = HIST_A - (CONV_A_W - 1)
    for q0 in range(0, tl, 64):
        acc = jnp.zeros((64, dq), F32)
        for k in range(CONV_A_W):
            s = (off_a + k) % SUBLANES
            row = off_a + k - s + q0
            acc = acc + ext_a[s, row:row + 64, :] * caw_ref[k:k + 1, :]
        y = _silu(_ln(acc + cab_ref[...], lag_ref[...], lab_ref[...]))
        cat_ref[q0:q0 + 64, 0:dq] = y.astype(BF16)

    hd = dq // N_HEADS_B
    lane_head = lax.broadcasted_iota(I32, (CHUNK, dq), 1) // hd
    for r0 in chunk_rows:
        v = _ln(_gelu(zcol(r0, 3)), lvg_ref[...], lvb_ref[...])
        sm = bsm_ref[...]
        for hh in range(N_HEADS_B):
            vm = jnp.where(lane_head == hh, v, 0.0).astype(BF16)
            sm = sm + _dot(ws_ref[hh], vm)
        cat_ref[r0:r0 + CHUNK, dq:2 * dq] = (_gelu(zcol(r0, 2)) * sm).astype(BF16)
        if r0 == tl - CHUNK:
            @pl.when(j == nj - 1)
            def _(v=v):
                v_out[0] = v

    off_c = HIST_C - (CONV_C_W - 1)
    for r0 in chunk_rows:
        ext_c[HIST_C + r0:HIST_C + r0 + rb, :] = zcol(r0, 6) * zcol(r0, 4)
    for r0 in chunk_rows:
        conv_c = jnp.zeros((CHUNK, dq), F32)
        for k in range(CONV_C_W):
            conv_c = conv_c + ext_c[off_c + r0 + k:off_c + r0 + k + CHUNK, :] * ccw_ref[k:k + 1, :]
        cat_ref[r0:r0 + CHUNK, 2 * dq:3 * dq] = (zcol(r0, 5) * conv_c).astype(BF16)

    gc = dq // len(POOL_WINDOWS)
    lane_grp = lax.broadcasted_iota(I32, (CHUNK, dq), 1) // gc
    lead = 2 * SUBLANES
    for r0 in chunk_rows:
        ext_d[HIST_D + r0:HIST_D + r0 + rb, :] = zcol(r0, 7)
    for r0 in chunk_rows:
        e = ext_d[HIST_D + r0 - lead:HIST_D + r0 + CHUNK, :]
        s2 = e + pltpu.roll(e, 1, axis=0)
        s4 = s2 + pltpu.roll(s2, 2, axis=0)
        s8 = s4 + pltpu.roll(s4, 4, axis=0)
        s16 = s8 + pltpu.roll(s8, 8, axis=0)
        sums = (s2, s4, s8, s16)
        pos = (j * tl + r0 + lax.broadcasted_iota(I32, (CHUNK, 1), 0)).astype(F32)
        mean = jnp.zeros((CHUNK, dq), F32)
        for g, w in enumerate(POOL_WINDOWS):
            inv = 1.0 / jnp.minimum(pos + 1.0, float(w))
            mean = jnp.where(lane_grp == g, sums[g][lead:lead + CHUNK, :] * inv, mean)
        dd = (mean - e[lead:lead + CHUNK, :]).astype(BF16)
        cat_ref[r0:r0 + CHUNK, 3 * dq:4 * dq] = (_dot(dd, wpool_ref[...]) * pscale_ref[...]).astype(BF16)

    mix_ref[...] = _dot(cat_ref[...], wout_ref[...])
    for r0 in range(0, tl, 64):
        h_out[0, r0:r0 + 64, :] = _ln(alpha * resid(r0, 64) + mix_ref[r0:r0 + 64, :], lmg_ref[...], lmb_ref[...])

    @pl.when(j == nj - 1)
    def _():
        sa_out[0] = ext_a[0, HIST_A + tl - (CONV_A_W - 1):HIST_A + tl, :]
        sc_out[0] = ext_c[HIST_C + tl - (CONV_C_W - 1):HIST_C + tl, :]
        sd_out[0] = ext_d[HIST_D + tl - POOL_PAST:HIST_D + tl, :]

    ext_a[0, 0:HIST_A, :] = ext_a[0, tl:tl + HIST_A, :]
    ext_c[0:HIST_C, :] = ext_c[tl:tl + HIST_C, :]
    ext_d[0:HIST_D, :] = ext_d[tl:tl + HIST_D, :]


def _full_spec(arr):
    nd = arr.ndim
    return pl.BlockSpec(arr.shape, lambda *_: (0,) * nd)


def _mixer_prompt(x, lw, first_layer, alpha):
    bsz, seq, d = x.shape
    d_mix = lw['w_out'].shape[0]
    dq = d_mix // 4
    tl = TL_PROMPT
    assert seq % tl == 0 and tl % CHUNK == 0 and seq >= CHUNK
    params = [lw['ln_in_g'], lw['ln_in_b'], lw['w_in'], lw['conv_a_w'], lw['conv_a_b'], lw['ln_a_g'],
              lw['ln_a_b'], lw['ln_v_g'], lw['ln_v_b'], lw['ws_tril'], lw['bs_mat'], lw['conv_c_w'],
              lw['w_pool_bd'], lw['pool_scale'], lw['w_out'], lw['ln_mix_g'], lw['ln_mix_b']]
    out_shape = (
        jax.ShapeDtypeStruct((bsz, seq, d), F32),
        jax.ShapeDtypeStruct((bsz, CONV_A_W - 1, dq), F32),
        jax.ShapeDtypeStruct((bsz, CONV_C_W - 1, dq), F32),
        jax.ShapeDtypeStruct((bsz, POOL_PAST, dq), F32),
        jax.ShapeDtypeStruct((bsz, CHUNK, dq), F32),
    )
    state_spec = lambda rows: pl.BlockSpec((1, rows, dq), lambda b, j: (b, 0, 0))
    return pl.pallas_call(
        functools.partial(_mixer_prompt_kernel, first_layer, alpha, tl, d_mix),
        grid=(bsz, seq // tl),
        in_specs=[pl.BlockSpec((1, tl, d), lambda b, j: (b, j, 0))] + [_full_spec(p) for p in params],
        out_specs=(pl.BlockSpec((1, tl, d), lambda b, j: (b, j, 0)),
                   state_spec(CONV_A_W - 1), state_spec(CONV_C_W - 1), state_spec(POOL_PAST),
                   state_spec(CHUNK)),
        out_shape=out_shape,
        scratch_shapes=[pltpu.VMEM((SUBLANES, HIST_A + tl, dq), F32), pltpu.VMEM((HIST_C + tl, dq), F32),
                        pltpu.VMEM((HIST_D + tl, dq), F32), pltpu.VMEM((tl, d_mix), BF16),
                        pltpu.VMEM((tl, lw['w_in'].shape[1]), F32), pltpu.VMEM((tl, d), BF16),
                        pltpu.VMEM((tl, d), F32)],
        compiler_params=_cparams(("arbitrary", "arbitrary")),
        name="mixer_prompt",
    )(x, *params)


def _mixer_sample_kernel(first_layer, alpha, n_seq, n_step, d_mix,
                         x_ref, sa_ref, sc_ref, sd_ref, lng_ref, lnb_ref, win_ref, caw_ref, cab_ref,
                         lag_ref, lab_ref, lvg_ref, lvb_ref, wsv_ref, bsv_ref, ccw_ref, wpool_ref,
                         pscale_ref, wout_ref, lmg_ref, lmb_ref,
                         h_out, sa_out, sc_out, sd_out, v_out, cat_ref):
    dq = d_mix // 4
    x = x_ref[...]
    h = _ln(x, lng_ref[...], lnb_ref[...]) if first_layer else x
    hb = h.astype(BF16)

    def proj(i):
        return _dot(hb, win_ref[:, i * dq:(i + 1) * dq])

    def slab(val, i):
        return val[i * n_seq:(i + 1) * n_seq, :]

    def ext_slabs(state_ref, n_past, cur):
        return ([state_ref[:, i, :] for i in range(n_past)]
                + [slab(cur, i) for i in range(n_step)])

    def store_state(out_ref, slabs, n_keep):
        for i, sl in enumerate(slabs[len(slabs) - n_keep:]):
            out_ref[:, i, :] = sl

    a_glu = proj(0) * jax.nn.sigmoid(proj(1))
    ea = ext_slabs(sa_ref, CONV_A_W - 1, a_glu)
    for l in range(n_step):
        acc = jnp.zeros((n_seq, dq), F32)
        for k in range(CONV_A_W):
            acc = acc + ea[l + k] * caw_ref[k:k + 1, :]
        y = _silu(_ln(acc + cab_ref[...], lag_ref[...], lab_ref[...]))
        cat_ref[l * n_seq:(l + 1) * n_seq, 0:dq] = y.astype(BF16)
    store_state(sa_out, ea, CONV_A_W - 1)

    u = _gelu(proj(2))
    v = _ln(_gelu(proj(3)), lvg_ref[...], lvb_ref[...])
    store_state(v_out, [slab(v, l) for l in range(n_step)], n_step)
    for l in range(n_step):
        s = jnp.zeros((n_seq, dq), F32) + bsv_ref[l:l + 1, :]
        for m in range(l + 1):
            s = s + slab(v, m) * wsv_ref[l * n_step + m:l * n_step + m + 1, :]
        cat_ref[l * n_seq:(l + 1) * n_seq, dq:2 * dq] = (slab(u, l) * s).astype(BF16)

    c_x = proj(4)
    c_b = proj(5)
    c_c = proj(6)
    gx = c_c * c_x
    ec = ext_slabs(sc_ref, CONV_C_W - 1, gx)
    for l in range(n_step):
        acc = jnp.zeros((n_seq, dq), F32)
        for k in range(CONV_C_W):
            acc = acc + ec[l + k] * ccw_ref[k:k + 1, :]
        cat_ref[l * n_seq:(l + 1) * n_seq, 2 * dq:3 * dq] = (slab(c_b, l) * acc).astype(BF16)
    store_state(sc_out, ec, CONV_C_W - 1)

    d_in = proj(7)
    ed = ext_slabs(sd_ref, POOL_PAST, d_in)
    memo = {}

    def wsum(i, w):
        if i < 0:
            return None
        if w == 1:
            return ed[i]
        if (i, w) not in memo:
            a, b = wsum(i, w // 2), wsum(i - w // 2, w // 2)
            memo[(i, w)] = a if b is None else a + b
        return memo[(i, w)]

    gc = dq // len(POOL_WINDOWS)
    lane_grp = lax.broadcasted_iota(I32, (n_seq, dq), 1) // gc
    for l in range(n_step):
        mean = jnp.zeros((n_seq, dq), F32)
        for g, w in enumerate(POOL_WINDOWS):
            count = min(PAST_LEN + l + 1, w)
            mean = jnp.where(lane_grp == g, wsum(POOL_PAST + l, w) * (1.0 / count), mean)
        dd = (mean - slab(d_in, l)).astype(BF16)
        cat_ref[l * n_seq:(l + 1) * n_seq, 3 * dq:4 * dq] = (
            _dot(dd, wpool_ref[...]) * pscale_ref[...]).astype(BF16)
    store_state(sd_out, ed, POOL_PAST)

    mix = _dot(cat_ref[...], wout_ref[...])
    h_out[...] = _ln(alpha * h + mix, lmg_ref[...], lmb_ref[...])


def _mixer_sample(x_tm, sa, sc, sd, lw, first_layer, alpha, n_seq, n_step):
    rows, d = x_tm.shape
    d_mix = lw['w_out'].shape[0]
    dq = d_mix // 4
    assert n_seq % SUBLANES == 0 and n_step <= CHUNK and PAST_LEN % CHUNK == 0
    ins = [x_tm, sa, sc, sd, lw['ln_in_g'], lw['ln_in_b'], lw['w_in'], lw['conv_a_w'], lw['conv_a_b'],
           lw['ln_a_g'], lw['ln_a_b'], lw['ln_v_g'], lw['ln_v_b'], lw['ws_vec'], lw['bs_vec'],
           lw['conv_c_w'], lw['w_pool_bd'], lw['pool_scale'], lw['w_out'], lw['ln_mix_g'], lw['ln_mix_b']]
    out_shape = (
        jax.ShapeDtypeStruct((rows, d), F32),
        jax.ShapeDtypeStruct((n_seq, CONV_A_W - 1, dq), F32),
        jax.ShapeDtypeStruct((n_seq, CONV_C_W - 1, dq), F32),
        jax.ShapeDtypeStruct((n_seq, POOL_PAST, dq), F32),
        jax.ShapeDtypeStruct((n_seq, n_step, dq), F32),
    )
    return pl.pallas_call(
        functools.partial(_mixer_sample_kernel, first_layer, alpha, n_seq, n_step, d_mix),
        grid=(1,),
        in_specs=[_full_spec(a) for a in ins],
        out_specs=tuple(_full_spec(s) for s in out_shape),
        out_shape=out_shape,
        scratch_shapes=[pltpu.VMEM((rows, d_mix), BF16)],
        compiler_params=_cparams(("arbitrary",)),
        name="mixer_sample",
    )(*ins)


def _ffn_dense_kernel(alpha, tf, x_ref, wg_hbm, wu_hbm, wd_hbm, g_ref, b_ref, o_ref, wg_ref, wu_ref, wd_ref, sem):
    @pl.when(pl.program_id(0) == 0)
    def _():
        copies = [pltpu.make_async_copy(src, dst, sem.at[n])
                  for n, (src, dst) in enumerate(((wg_hbm, wg_ref), (wu_hbm, wu_ref), (wd_hbm, wd_ref)))]
        for cp in copies:
            cp.start()
        for cp in copies:
            cp.wait()

    x = x_ref[...]
    xb = x.astype(BF16)
    ffn = None
    for c0 in range(0, wg_ref.shape[1], tf):
        mid = _silu(_dot(xb, wg_ref[:, c0:c0 + tf])) * _dot(xb, wu_ref[:, c0:c0 + tf])
        part = _dot(mid.astype(BF16), wd_ref[c0:c0 + tf, :])
        ffn = part if ffn is None else ffn + part
    o_ref[...] = _ln(alpha * x + ffn, g_ref[...], b_ref[...])


def _ffn_dense(x, wg, wu, wd, g, b, alpha, tm, tf):
    t, d = x.shape
    dff = wg.shape[1]
    assert t % tm == 0 and dff % tf == 0
    any_spec = pl.BlockSpec(memory_space=pl.ANY)
    return pl.pallas_call(
        functools.partial(_ffn_dense_kernel, alpha, tf),
        grid=(t // tm,),
        in_specs=[pl.BlockSpec((tm, d), lambda i: (i, 0)), any_spec, any_spec, any_spec,
                  pl.BlockSpec((1, d), lambda i: (0, 0)),
                  pl.BlockSpec((1, d), lambda i: (0, 0))],
        out_specs=pl.BlockSpec((tm, d), lambda i: (i, 0)),
        out_shape=jax.ShapeDtypeStruct((t, d), F32),
        scratch_shapes=[pltpu.VMEM(wg.shape, BF16), pltpu.VMEM(wu.shape, BF16), pltpu.VMEM(wd.shape, BF16),
                        pltpu.SemaphoreType.DMA((3,))],
        compiler_params=_cparams(("arbitrary",)),
        name="ffn_dense",
    )(x, wg, wu, wd, g, b)


def _router_kernel(tm, h_ref, wcat_ref, cin_ref, mi_ref, mf_ref, cnt_ref, carry_ref, before_ref):
    i = pl.program_id(0)

    @pl.when(i == 0)
    def _():
        carry_ref[...] = cin_ref[...].astype(F32)
        row = lax.broadcasted_iota(I32, (tm, tm), 0)
        col = lax.broadcasted_iota(I32, (tm, tm), 1)
        before_ref[...] = jnp.where(row < col, 1.0, 0.0).astype(BF16)

    h = h_ref[...]
    h_hi = h.astype(BF16)
    h_lo = (h - h_hi.astype(F32)).astype(BF16)
    p_hi = _dot(h_hi, wcat_ref[...])
    p_lo = _dot(h_lo, wcat_ref[...])
    logits = (p_hi[:, 0:LANES] + (p_lo[:, 0:LANES] + p_hi[:, LANES:2 * LANES])) + p_lo[:, LANES:2 * LANES]
    lg = logits.T[0:N_EXPERTS, :]
    ex = lax.broadcasted_iota(I32, (N_EXPERTS, tm), 0).astype(F32)
    m1 = jnp.max(lg, axis=0, keepdims=True)
    i1 = jnp.min(jnp.where(lg == m1, ex, float(N_EXPERTS)), axis=0, keepdims=True)
    sel1 = ex == i1
    rest = jnp.where(sel1, -jnp.inf, lg)
    m2 = jnp.max(rest, axis=0, keepdims=True)
    i2 = jnp.min(jnp.where(rest == m2, ex, float(N_EXPERTS)), axis=0, keepdims=True)
    sel2 = ex == i2
    e2 = jnp.exp(m2 - m1)
    den = 1.0 + e2
    g1 = 1.0 / den
    g2 = e2 / den

    sel = jnp.where(sel1 | sel2, 1.0, 0.0)
    sel_pad = jnp.concatenate([sel, jnp.zeros_like(sel)], axis=0).astype(BF16)
    base = _dot(sel_pad, before_ref[...])[0:N_EXPERTS, :] + carry_ref[:, 0:1]
    r1 = jnp.sum(jnp.where(sel1, base, 0.0), axis=0, keepdims=True)
    r2 = jnp.sum(jnp.where(sel2, base, 0.0), axis=0, keepdims=True)
    carry_ref[...] = carry_ref[...] + jnp.sum(sel, axis=1, keepdims=True)

    meta = jnp.where(ex == 0, i1, jnp.where(ex == 1, i2, jnp.where(ex == 2, r1, r2)))
    mi_ref[...] = meta.astype(I32)
    gt = jnp.where(ex == 0, g1, jnp.where(ex == 1, g2, 0.0))
    gt = jnp.concatenate([gt, jnp.zeros((LANES - N_EXPERTS, tm), F32)], axis=0)
    mf_ref[...] = gt.T[:, 0:SUBLANES]
    cnt_ref[...] = carry_ref[...].astype(I32)


def _router(h, wr_cat, counts_in):
    t, d = h.shape
    tm = min(TM_ROUTE, t)
    assert t % tm == 0 and N_EXPERTS == SUBLANES
    return pl.pallas_call(
        functools.partial(_router_kernel, tm),
        grid=(t // tm,),
        in_specs=[pl.BlockSpec((tm, d), lambda i: (i, 0)), _full_spec(wr_cat), _full_spec(counts_in)],
        out_specs=(pl.BlockSpec((SUBLANES, tm), lambda i: (0, i)),
                   pl.BlockSpec((tm, SUBLANES), lambda i: (i, 0)),
                   pl.BlockSpec((SUBLANES, LANES), lambda i: (0, 0))),
        out_shape=(jax.ShapeDtypeStruct((SUBLANES, t), I32),
                   jax.ShapeDtypeStruct((t, SUBLANES), F32),
                   jax.ShapeDtypeStruct((SUBLANES, LANES), I32)),
        scratch_shapes=[pltpu.VMEM((SUBLANES, LANES), F32), pltpu.VMEM((tm, tm), BF16)],
        compiler_params=_cparams(("arbitrary",)),
        name="router",
    )(h, wr_cat, counts_in)


def _row_copy(src, src_row, dst, dst_row, sem):
    return pltpu.make_async_copy(src.at[pl.ds(src_row, 1), :], dst.at[pl.ds(dst_row, 1), :], sem)


def _rows_wait(src, dst, n_rows, sem):
    pltpu.make_async_copy(src.at[pl.ds(0, n_rows), :], dst.at[pl.ds(0, n_rows), :], sem).wait()


def _dispatch_kernel(tm, tm_expert, n_blocks, blk_ranges, zi_ref, dest_ref, *rest):
    n_src = len(blk_ranges)
    h_refs = rest[:n_src]
    xs_ref, xbuf, zblk, lsems, rsems, zsem = rest[n_src:]
    i = pl.program_id(0)
    n = pl.num_programs(0)

    def block_load(g, start):
        slot = lax.rem(g, N_STAGE)
        for h_ref, (b0, b1) in zip(h_refs, blk_ranges):
            @pl.when((g >= b0) & (g < b1))
            def _(h_ref=h_ref, b0=b0):
                cp = pltpu.make_async_copy(h_ref.at[pl.ds((g - b0) * tm, tm), :], xbuf.at[slot],
                                           lsems.at[slot])
                if start:
                    cp.start()
                else:
                    cp.wait()

    def rows_wait(g):
        slot = lax.rem(g, N_STAGE)
        for _k in range(TOP_K):
            _rows_wait(xbuf.at[slot], xs_ref, tm, rsems.at[slot])

    @pl.when(i == 0)
    def _():
        block_load(i, True)

    @pl.when(i >= N_STAGE - 1)
    def _():
        rows_wait(i - (N_STAGE - 1))

    @pl.when(i + 1 < n)
    def _():
        block_load(i + 1, True)

    block_load(i, False)
    cur = lax.rem(i, N_STAGE)

    def issue(j, carry):
        r0 = pl.multiple_of(j * SUBLANES, SUBLANES)
        for u in range(SUBLANES):
            for k in range(TOP_K):
                dst = dest_ref[0, 0, j * SUBLANES + (k * tm + u)]
                _row_copy(xbuf.at[cur], r0 + u, xs_ref, dst, rsems.at[cur]).start(priority=k)
        return carry

    lax.fori_loop(0, tm // SUBLANES, issue, 0)

    @pl.when(i == n - 1)
    def _():
        for back in range(N_STAGE - 2, -1, -1):
            @pl.when(i >= back)
            def _(back=back):
                rows_wait(i - back)

    @pl.when(i == 0)
    def _():
        zblk[...] = jnp.zeros_like(zblk)
        for e in range(N_EXPERTS):
            lo = zi_ref[e] + zi_ref[N_EXPERTS + e]
            hi = zi_ref[e] + zi_ref[2 * N_EXPERTS + e]

            def zissue(r, carry):
                _row_copy(zblk, 0, xs_ref, r, zsem).start()
                return carry

            def zdrain(r, carry):
                _row_copy(zblk, 0, xs_ref, r, zsem).wait()
                return carry

            lax.fori_loop(lo, hi, zissue, 0)
            lax.fori_loop(lo, hi, zdrain, 0)

        def bcopy(b):
            return pltpu.make_async_copy(zblk, xs_ref.at[pl.ds(b * tm_expert, tm_expert), :], zsem)

        def bissue(b, carry):
            bcopy(b).start()
            return carry

        def bdrain(b, carry):
            bcopy(b).wait()
            return carry

        lax.fori_loop(zi_ref[3 * N_EXPERTS], n_blocks, bissue, 0)
        lax.fori_loop(zi_ref[3 * N_EXPERTS], n_blocks, bdrain, 0)


def _dispatch(hs_list, dest_blk, zinfo, n_blocks, tm_expert):
    d = hs_list[0].shape[1]
    nb = dest_blk.shape[0]
    tm = dest_blk.shape[2] // TOP_K
    blk_ranges, b0 = [], 0
    for h in hs_list:
        assert h.shape[0] % tm == 0 and h.shape[0] >= tm
        blk_ranges.append((b0, b0 + h.shape[0] // tm))
        b0 = blk_ranges[-1][1]
    assert b0 == nb
    any_spec = pl.BlockSpec(memory_space=pl.ANY)
    return pl.pallas_call(
        functools.partial(_dispatch_kernel, tm, tm_expert, n_blocks, tuple(blk_ranges)),
        grid_spec=pltpu.PrefetchScalarGridSpec(
            num_scalar_prefetch=1,
            grid=(nb,),
            in_specs=[pl.BlockSpec((1, 1, TOP_K * tm), lambda i, zi: (i, 0, 0), memory_space=pltpu.SMEM)]
            + [any_spec] * len(hs_list),
            out_specs=any_spec,
            scratch_shapes=[pltpu.VMEM((N_STAGE, tm, d), F32), pltpu.VMEM((tm_expert, d), F32),
                            pltpu.SemaphoreType.DMA((N_STAGE,)), pltpu.SemaphoreType.DMA((N_STAGE,)),
                            pltpu.SemaphoreType.DMA(())],
        ),
        out_shape=jax.ShapeDtypeStruct((n_blocks * tm_expert, d), F32),
        compiler_params=_cparams(("arbitrary",)),
        name="moe_dispatch",
    )(zinfo, dest_blk, *hs_list)


def _ffn_moe_kernel(tf, be_ref, nu_ref, x_ref, wg_hbm, wu_hbm, wd_hbm, y_ref,
                    wg_ref, wu_ref, wd_ref, stage_in, stage_out, sems):
    b = pl.program_id(0)
    d, dff = wg_ref.shape
    rows_in = stage_in.shape[1]
    rows_out = stage_out.shape[1]
    n_in, n_out = d // rows_in, dff // rows_out
    e = be_ref[b]
    fresh = (b == 0) | (e != be_ref[jnp.maximum(b - 1, 0)])

    @pl.when((b < nu_ref[0]) & fresh)
    def _():
        n_slots = stage_in.shape[0]

        def chunk(n):
            slot = n % n_slots
            if n < 2 * n_in:
                src, dst = (wg_hbm, wg_ref) if n < n_in else (wu_hbm, wu_ref)
                rows = pl.ds((n % n_in) * rows_in, rows_in)
                stage = stage_in.at[slot]
            else:
                src, dst = wd_hbm, wd_ref
                rows = pl.ds((n - 2 * n_in) * rows_out, rows_out)
                stage = stage_out.at[slot]
            return pltpu.make_async_copy(src.at[e, rows, :], stage, sems.at[slot]), stage, dst.at[rows, :]

        n_chunks = 2 * n_in + n_out
        for n in range(n_slots - 1):
            chunk(n)[0].start()
        for n in range(n_chunks):
            if n + n_slots - 1 < n_chunks:
                chunk(n + n_slots - 1)[0].start()
            copy, staged, dst = chunk(n)
            copy.wait()
            dst[...] = staged[...].astype(BF16)

    @pl.when(b < nu_ref[0])
    def _():
        xb = x_ref[...].astype(BF16)
        ffn = None
        for c0 in range(0, dff, tf):
            mid = _silu(_dot(xb, wg_ref[:, c0:c0 + tf])) * _dot(xb, wu_ref[:, c0:c0 + tf])
            part = _dot(mid.astype(BF16), wd_ref[c0:c0 + tf, :])
            ffn = part if ffn is None else ffn + part
        y_ref[...] = ffn

    @pl.when(b >= nu_ref[0])
    def _():
        y_ref[...] = jnp.zeros_like(y_ref)


def _ffn_moe(xs, wg, wu, wd, block_e, n_used, tm, tf):
    cap, d = xs.shape
    dff = wg.shape[2]
    rows_in, rows_out = 128, 512
    assert cap % tm == 0 and dff % tf == 0 and d % rows_in == 0 and dff % rows_out == 0
    any_spec = pl.BlockSpec(memory_space=pl.ANY)
    return pl.pallas_call(
        functools.partial(_ffn_moe_kernel, tf),
        grid_spec=pltpu.PrefetchScalarGridSpec(
            num_scalar_prefetch=2,
            grid=(cap // tm,),
            in_specs=[pl.BlockSpec((tm, d), lambda b, be, nu: (jnp.minimum(b, nu[0] - 1), 0)),
                      any_spec, any_spec, any_spec],
            out_specs=pl.BlockSpec((tm, d), lambda b, be, nu: (b, 0)),
            scratch_shapes=[pltpu.VMEM((d, dff), BF16), pltpu.VMEM((d, dff), BF16), pltpu.VMEM((dff, d), BF16),
                            pltpu.VMEM((N_WSTAGE, rows_in, dff), F32), pltpu.VMEM((N_WSTAGE, rows_out, d), F32),
                            pltpu.SemaphoreType.DMA((N_WSTAGE,))],
        ),
        out_shape=jax.ShapeDtypeStruct((cap, d), F32),
        compiler_params=_cparams(("arbitrary",)),
        name="ffn_moe",
    )(block_e, n_used, xs, wg, wu, wd)


def _combine_kernel(alpha, tm, dcur_ref, dnext_ref, h_ref, gate_ref, g_ref, b_ref, ys_ref, o_ref, ybuf, sems):
    i = pl.program_id(0)
    n = pl.num_programs(0)

    def issue(dref, slot):
        def body(j, carry):
            r0 = pl.multiple_of(j * SUBLANES, SUBLANES)
            for u in range(SUBLANES):
                for k in range(TOP_K):
                    src = dref[0, 0, j * SUBLANES + (k * tm + u)]
                    _row_copy(ys_ref, src, ybuf.at[slot, k], r0 + u, sems.at[slot]).start(priority=k)
            return carry

        lax.fori_loop(0, tm // SUBLANES, body, 0)

    @pl.when(i == 0)
    def _():
        issue(dcur_ref, 0)

    @pl.when(i + 1 < n)
    def _():
        issue(dnext_ref, (i + 1) % 2)

    slot = i % 2
    for k in range(TOP_K):
        _rows_wait(ys_ref, ybuf.at[slot, k], tm, sems.at[slot])

    gates = gate_ref[...]
    ffn = gates[:, 0:1] * ybuf[slot, 0] + gates[:, 1:2] * ybuf[slot, 1]
    o_ref[...] = _ln(alpha * h_ref[...] + ffn, g_ref[...], b_ref[...])


def _combine(h, dest_blk, gates, ys, g, b, alpha):
    t, d = h.shape
    nb = dest_blk.shape[0]
    tm = dest_blk.shape[2] // TOP_K
    dest_spec = lambda imap: pl.BlockSpec((1, 1, TOP_K * tm), imap, memory_space=pltpu.SMEM)
    return pl.pallas_call(
        functools.partial(_combine_kernel, alpha, tm),
        grid=(nb,),
        in_specs=[dest_spec(lambda i: (i, 0, 0)),
                  dest_spec(lambda i: (jnp.minimum(i + 1, nb - 1), 0, 0)),
                  pl.BlockSpec((tm, d), lambda i: (i, 0)),
                  pl.BlockSpec((tm, SUBLANES), lambda i: (i, 0)),
                  pl.BlockSpec((1, d), lambda i: (0, 0)),
                  pl.BlockSpec((1, d), lambda i: (0, 0)),
                  pl.BlockSpec(memory_space=pl.ANY)],
        out_specs=pl.BlockSpec((tm, d), lambda i: (i, 0)),
        out_shape=jax.ShapeDtypeStruct((t, d), F32),
        scratch_shapes=[pltpu.VMEM((2, TOP_K, tm, d), F32), pltpu.SemaphoreType.DMA((2,))],
        compiler_params=_cparams(("arbitrary",)),
        name="moe_combine",
    )(dest_blk, dest_blk, h, gates, g, b, ys)


def _ffn_routed(hs_list, mw, g, b, alpha, tm_expert):
    d = hs_list[0].shape[1]
    counts_in = jnp.zeros((SUBLANES, LANES), I32)
    routed = []
    for h in hs_list:
        meta_i, gates, counts_in = _router(h, mw['w_router_cat'], counts_in)
        routed.append((meta_i, gates))
    counts = counts_in[:N_EXPERTS, 0]
    padded = (counts + tm_expert - 1) // tm_expert * tm_expert
    pad_end = jnp.cumsum(padded)
    pad_start = (pad_end - padded).astype(I32)
    n_assign = sum(h.shape[0] for h in hs_list) * TOP_K
    n_blocks = -(-n_assign // tm_expert) + N_EXPERTS
    cap = n_blocks * tm_expert
    n_used = (pad_end[-1] // tm_expert).astype(I32)
    blk_start = jnp.minimum(jnp.arange(n_blocks, dtype=I32), n_used - 1) * tm_expert
    block_e = jnp.minimum(jnp.sum(blk_start[:, None] >= pad_end[None, :], axis=1), N_EXPERTS - 1).astype(I32)
    zinfo = jnp.concatenate([pad_start, counts, padded, n_used.reshape(1)]).astype(I32)

    dests = []
    for h, (meta_i, _) in zip(hs_list, routed):
        dest = meta_i[TOP_K:2 * TOP_K, :]
        for e in range(N_EXPERTS):
            dest = dest + jnp.where(meta_i[0:TOP_K, :] == e, pad_start[e], 0)
        nb = h.shape[0] // TM_MOVE
        dests.append(dest.reshape(TOP_K, nb, TM_MOVE).transpose(1, 0, 2).reshape(nb, 1, TOP_K * TM_MOVE))
    xs = _dispatch(hs_list, jnp.concatenate(dests, axis=0), zinfo, n_blocks, tm_expert)
    ys = _ffn_moe(xs, mw['w_e_gate'], mw['w_e_up'], mw['w_e_down'], block_e, n_used.reshape(1), tm_expert,
                  TF_EXPERT)
    return [_combine(h, dest_blk, gates, ys, g, b, alpha)
            for h, dest_blk, (_, gates) in zip(hs_list, dests, routed)]


def _prep_layer(i, p, n_step):
    dq = p['conv_a_w'].shape[2]
    hd = dq // N_HEADS_B
    tril = jnp.tril(jnp.ones((CHUNK, CHUNK), dtype=bool))
    w_s = p['w_s'][i]
    b_s = p['b_s'][i]
    wp = p['w_pool'][i]
    ng, gc = wp.shape[0], wp.shape[1]
    w_pool_bd = (jnp.eye(ng, dtype=F32)[:, None, :, None] * wp[:, :, None, :]).reshape(ng * gc, ng * gc)
    row = lambda a: a.reshape(1, -1)
    return dict(
        ln_in_g=row(p['ln_in_g']), ln_in_b=row(p['ln_in_b']),
        w_in=p['w_in'][i].astype(BF16),
        conv_a_w=p['conv_a_w'][i], conv_a_b=row(p['conv_a_b'][i]),
        ln_a_g=row(p['ln_a_g'][i]), ln_a_b=row(p['ln_a_b'][i]),
        ln_v_g=row(p['ln_v_g'][i]), ln_v_b=row(p['ln_v_b'][i]),
        ws_tril=jnp.where(tril[None], w_s, 0).astype(BF16),
        bs_mat=jnp.repeat(b_s.T, hd, axis=1),
        ws_vec=jnp.repeat(jnp.where(tril[None], w_s, 0)[:, :n_step, :n_step].transpose(1, 2, 0)
                          .reshape(n_step * n_step, N_HEADS_B), hd, axis=1),
        bs_vec=jnp.repeat(b_s[:, :n_step].T, hd, axis=1),
        conv_c_w=p['conv_c_w'][i],
        w_pool_bd=w_pool_bd.astype(BF16), pool_scale=row(p['pool_scale'][i]),
        w_out=p['w_out'][i].astype(BF16),
        ln_mix_g=row(p['ln_mix_g'][i]), ln_mix_b=row(p['ln_mix_b'][i]),
    )


def kernel(x_prompt, x_sample, state_conv_a, state_conv_c, state_pool_d, ln_in_g, ln_in_b, w_in, conv_a_w,
           conv_a_b, ln_a_g, ln_a_b, ln_v_g, ln_v_b, w_s, b_s, conv_c_w, w_pool, pool_scale, w_out,
           ln_mix_g, ln_mix_b, w_ff_gate, w_ff_up, w_ff_down, w_router, w_e_gate, w_e_up, w_e_down,
           ln_ffn_g, ln_ffn_b):
    p = dict(ln_in_g=ln_in_g, ln_in_b=ln_in_b, w_in=w_in, conv_a_w=conv_a_w, conv_a_b=conv_a_b,
             ln_a_g=ln_a_g, ln_a_b=ln_a_b, ln_v_g=ln_v_g, ln_v_b=ln_v_b, w_s=w_s, b_s=b_s,
             conv_c_w=conv_c_w, w_pool=w_pool, pool_scale=pool_scale, w_out=w_out,
             ln_mix_g=ln_mix_g, ln_mix_b=ln_mix_b)
    depth = w_in.shape[0]
    bsz, seq, d = x_prompt.shape
    n_seq, n_step, _ = x_sample.shape
    dq = conv_a_w.shape[2]
    alpha = float((2.0 * depth) ** 0.25)

    hp = x_prompt
    hs = x_sample.transpose(1, 0, 2).reshape(n_step * n_seq, d)
    outs = {k: [] for k in ('pa', 'pc', 'pd', 'pv', 'sa', 'sc', 'sd', 'sv')}
    for i in range(depth):
        lw = _prep_layer(i, p, n_step)
        hp, pa, pc, pd, pv = _mixer_prompt(hp, lw, i == 0, alpha)
        hs, sa, sc, sd, sv = _mixer_sample(hs, state_conv_a[i], state_conv_c[i], state_pool_d[i], lw,
                                           i == 0, alpha, n_seq, n_step)
        lg, lb = ln_ffn_g[i].reshape(1, -1), ln_ffn_b[i].reshape(1, -1)
        j = i // 2
        if i % 2 == 0:
            wg, wu, wd = w_ff_gate[j].astype(BF16), w_ff_up[j].astype(BF16), w_ff_down[j].astype(BF16)
            hp = _ffn_dense(hp.reshape(bsz * seq, d), wg, wu, wd, lg, lb, alpha, TM_DENSE,
                            TF_DENSE).reshape(bsz, seq, d)
            hs = _ffn_dense(hs, wg, wu, wd, lg, lb, alpha, hs.shape[0], TF_DENSE)
        else:
            wr = jnp.pad(w_router[j], ((0, 0), (0, LANES - N_EXPERTS)))
            wr_hi = wr.astype(BF16)
            wr_lo = (wr - wr_hi.astype(F32)).astype(BF16)
            mw = dict(w_router_cat=jnp.concatenate([wr_hi, wr_lo], axis=1),
                      w_e_gate=w_e_gate[j], w_e_up=w_e_up[j], w_e_down=w_e_down[j])
            hp, hs = _ffn_routed([hp.reshape(bsz * seq, d), hs], mw, lg, lb, alpha, TM_EXPERT)
            hp = hp.reshape(bsz, seq, d)
        for k, val in zip(('pa', 'pc', 'pd', 'pv', 'sa', 'sc', 'sd', 'sv'), (pa, pc, pd, pv, sa, sc, sd, sv)):
            outs[k].append(val)
    y_sample = hs.reshape(n_step, n_seq, d).transpose(1, 0, 2)
    st = lambda k: jnp.stack(outs[k])
    return (hp, y_sample, st('pa'), st('pc'), st('pd'), st('pv'), st('sa'), st('sc'), st('sd'), st('sv'))
```

```python
import functools

import numpy as np
import jax
import jax.numpy as jnp
from jax import lax
from jax.experimental import pallas as pl
from jax.experimental.pallas import tpu as pltpu

F32 = jnp.float32
BF16 = jnp.bfloat16
I32 = jnp.int32

PAST_LEN = 16384
CHUNK = 128
N_HEADS_B = 4
POOL_WINDOWS = (2, 4, 8, 16)
POOL_PAST = 15
CONV_A_W = 31
CONV_C_W = 3
N_EXPERTS = 8
TOP_K = 2
LN_EPS = 1e-5
SQRT_HALF = float(np.sqrt(0.5).astype(np.float32))

LANES = 128
SUBLANES = 8
VMEM_LIMIT_BYTES = 56 * 1024 * 1024

HIST_A = 32
HIST_C = 8
HIST_D = 24

TL_PROMPT = 512
TM_DENSE = 512
TF_DENSE = 1792
TF_EXPERT = 1792
TM_ROUTE = 512
TM_MOVE = 256
TM_EXPERT = 512
N_STAGE = 3
N_WSTAGE = 4


def _ln(x, g, b):
    mu = jnp.mean(x, axis=-1, keepdims=True)
    xc = x - mu
    var = jnp.mean(xc * xc, axis=-1, keepdims=True)
    return xc * lax.rsqrt(var + LN_EPS) * g + b


def _gelu(x):
    return 0.5 * x * (1.0 + lax.erf(x * SQRT_HALF))


def _silu(x):
    return x * jax.nn.sigmoid(x)


def _dot(a, b):
    return jnp.dot(a, b, preferred_element_type=F32)


def _cparams(sem):
    return pltpu.CompilerParams(dimension_semantics=sem, vmem_limit_bytes=VMEM_LIMIT_BYTES)


def _mixer_prompt_kernel(first_layer, route, alpha, tl, d_mix,
                         x_ref, lng_ref, lnb_ref, win_ref, caw_ref, cab_ref, lag_ref, lab_ref,
                         lvg_ref, lvb_ref, ws_ref, bsm_ref, ccw_ref, wpool_ref, pscale_ref,
                         wout_ref, lmg_ref, lmb_ref, *refs):
    if route:
        wcat_ref, cin_ref = refs[:2]
        refs = refs[2:]
    h_out, sa_out, sc_out, sd_out, v_out = refs[:5]
    refs = refs[5:]
    if route:
        mi_ref, mf_ref, cnt_ref = refs[:3]
        refs = refs[3:]
    ext_a, ext_c, ext_d, cat_ref, z_ref, hb_ref, mix_ref = refs[:7]
    dq = d_mix // 4
    j = pl.program_id(1)
    nj = pl.num_programs(1)
    if route:
        carry_ref, before_ref = refs[7:]

        @pl.when((pl.program_id(0) == 0) & (j == 0))
        def _():
            _route_init(tl, cin_ref, carry_ref, before_ref)

    @pl.when(j == 0)
    def _():
        ext_a[0, 0:HIST_A, :] = jnp.zeros((HIST_A, dq), F32)
        ext_c[0:HIST_C, :] = jnp.zeros((HIST_C, dq), F32)
        ext_d[0:HIST_D, :] = jnp.zeros((HIST_D, dq), F32)

    d = x_ref.shape[2]
    rb = CHUNK

    def resid(r0, n):
        return h_out[0, r0:r0 + n, :] if first_layer else x_ref[0, r0:r0 + n, :]

    for r0 in range(0, tl, rb):
        x = x_ref[0, r0:r0 + rb, :]
        if first_layer:
            x = _ln(x, lng_ref[...], lnb_ref[...])
            h_out[0, r0:r0 + rb, :] = x
        hb_ref[r0:r0 + rb, :] = x.astype(BF16)

    z_ref[...] = _dot(hb_ref[...], win_ref[...])

    def zcol(r0, i, n=rb):
        return z_ref[r0:r0 + n, i * dq:(i + 1) * dq]

    chunk_rows = range(0, tl, CHUNK)

    for r0 in chunk_rows:
        ext_a[0, HIST_A + r0:HIST_A + r0 + rb, :] = zcol(r0, 0) * jax.nn.sigmoid(zcol(r0, 1))
    n_ext = HIST_A + tl
    for r0 in range(0, n_ext - SUBLANES, rb):
        n = min(rb, n_ext - SUBLANES - r0)
        blk = ext_a[0, r0:r0 + n + SUBLANES, :]
        for s in range(1, SUBLANES):
            ext_a[s, r0:r0 + n, :] = pltpu.roll(blk, n + SUBLANES - s, axis=0)[0:n, :]
    off_a = HIST_A - (CONV_A_W - 1)
    for q0 in range(0, tl, 64):
        acc = jnp.zeros((64, dq), F32)
        for k in range(CONV_A_W):
            s = (off_a + k) % SUBLANES
            row = off_a + k - s + q0
            acc = acc + ext_a[s, row:row + 64, :] * caw_ref[k:k + 1, :]
        y = _silu(_ln(acc + cab_ref[...], lag_ref[...], lab_ref[...]))
        cat_ref[q0:q0 + 64, 0:dq] = y.astype(BF16)

    hd = dq // N_HEADS_B
    lane_head = lax.broadcasted_iota(I32, (CHUNK, dq), 1) // hd
    for r0 in chunk_rows:
        v = _ln(_gelu(zcol(r0, 3)), lvg_ref[...], lvb_ref[...])
        sm = bsm_ref[...]
        for hh in range(N_HEADS_B):
            vm = jnp.where(lane_head == hh, v, 0.0).astype(BF16)
            sm = sm + _dot(ws_ref[hh], vm)
        cat_ref[r0:r0 + CHUNK, dq:2 * dq] = (_gelu(zcol(r0, 2)) * sm).astype(BF16)
        if r0 == tl - CHUNK:
            @pl.when(j == nj - 1)
            def _(v=v):
                v_out[0] = v

    off_c = HIST_C - (CONV_C_W - 1)
    for r0 in chunk_rows:
        ext_c[HIST_C + r0:HIST_C + r0 + rb, :] = zcol(r0, 6) * zcol(r0, 4)
    for r0 in chunk_rows:
        conv_c = jnp.zeros((CHUNK, dq), F32)
        for k in range(CONV_C_W):
            conv_c = conv_c + ext_c[off_c + r0 + k:off_c + r0 + k + CHUNK, :] * ccw_ref[k:k + 1, :]
        cat_ref[r0:r0 + CHUNK, 2 * dq:3 * dq] = (zcol(r0, 5) * conv_c).astype(BF16)

    gc = dq // len(POOL_WINDOWS)
    lane_grp = lax.broadcasted_iota(I32, (CHUNK, dq), 1) // gc
    lead = 2 * SUBLANES
    for r0 in chunk_rows:
        ext_d[HIST_D + r0:HIST_D + r0 + rb, :] = zcol(r0, 7)
    for r0 in chunk_rows:
        e = ext_d[HIST_D + r0 - lead:HIST_D + r0 + CHUNK, :]
        s2 = e + pltpu.roll(e, 1, axis=0)
        s4 = s2 + pltpu.roll(s2, 2, axis=0)
        s8 = s4 + pltpu.roll(s4, 4, axis=0)
        s16 = s8 + pltpu.roll(s8, 8, axis=0)
        sums = (s2, s4, s8, s16)
        pos = (j * tl + r0 + lax.broadcasted_iota(I32, (CHUNK, 1), 0)).astype(F32)
        mean = jnp.zeros((CHUNK, dq), F32)
        for g, w in enumerate(POOL_WINDOWS):
            inv = 1.0 / jnp.minimum(pos + 1.0, float(w))
            mean = jnp.where(lane_grp == g, sums[g][lead:lead + CHUNK, :] * inv, mean)
        dd = (mean - e[lead:lead + CHUNK, :]).astype(BF16)
        cat_ref[r0:r0 + CHUNK, 3 * dq:4 * dq] = (_dot(dd, wpool_ref[...]) * pscale_ref[...]).astype(BF16)

    mix_ref[...] = _dot(cat_ref[...], wout_ref[...])
    for r0 in range(0, tl, 64):
        h_out[0, r0:r0 + 64, :] = _ln(alpha * resid(r0, 64) + mix_ref[r0:r0 + 64, :], lmg_ref[...], lmb_ref[...])
    if route:
        _route_block(tl, h_out[0], wcat_ref, mi_ref, mf_ref, cnt_ref, carry_ref, before_ref)

    @pl.when(j == nj - 1)
    def _():
        sa_out[0] = ext_a[0, HIST_A + tl - (CONV_A_W - 1):HIST_A + tl, :]
        sc_out[0] = ext_c[HIST_C + tl - (CONV_C_W - 1):HIST_C + tl, :]
        sd_out[0] = ext_d[HIST_D + tl - POOL_PAST:HIST_D + tl, :]

    ext_a[0, 0:HIST_A, :] = ext_a[0, tl:tl + HIST_A, :]
    ext_c[0:HIST_C, :] = ext_c[tl:tl + HIST_C, :]
    ext_d[0:HIST_D, :] = ext_d[tl:tl + HIST_D, :]


def _full_spec(arr):
    nd = arr.ndim
    return pl.BlockSpec(arr.shape, lambda *_: (0,) * nd)


def _mixer_prompt(x, lw, first_layer, alpha, route_w=None):
    bsz, seq, d = x.shape
    d_mix = lw['w_out'].shape[0]
    dq = d_mix // 4
    tl = TL_PROMPT
    nj = seq // tl
    route = route_w is not None
    assert seq % tl == 0 and tl % CHUNK == 0 and seq >= CHUNK
    params = [lw['ln_in_g'], lw['ln_in_b'], lw['w_in'], lw['conv_a_w'], lw['conv_a_b'], lw['ln_a_g'],
              lw['ln_a_b'], lw['ln_v_g'], lw['ln_v_b'], lw['ws_tril'], lw['bs_mat'], lw['conv_c_w'],
              lw['w_pool_bd'], lw['pool_scale'], lw['w_out'], lw['ln_mix_g'], lw['ln_mix_b']]
    out_shape = [
        jax.ShapeDtypeStruct((bsz, seq, d), F32),
        jax.ShapeDtypeStruct((bsz, CONV_A_W - 1, dq), F32),
        jax.ShapeDtypeStruct((bsz, CONV_C_W - 1, dq), F32),
        jax.ShapeDtypeStruct((bsz, POOL_PAST, dq), F32),
        jax.ShapeDtypeStruct((bsz, CHUNK, dq), F32),
    ]
    state_spec = lambda rows: pl.BlockSpec((1, rows, dq), lambda b, j: (b, 0, 0))
    out_specs = [pl.BlockSpec((1, tl, d), lambda b, j: (b, j, 0)),
                 state_spec(CONV_A_W - 1), state_spec(CONV_C_W - 1), state_spec(POOL_PAST), state_spec(CHUNK)]
    scratch = [pltpu.VMEM((SUBLANES, HIST_A + tl, dq), F32), pltpu.VMEM((HIST_C + tl, dq), F32),
               pltpu.VMEM((HIST_D + tl, dq), F32), pltpu.VMEM((tl, d_mix), BF16),
               pltpu.VMEM((tl, lw['w_in'].shape[1]), F32), pltpu.VMEM((tl, d), BF16),
               pltpu.VMEM((tl, d), F32)]
    if route:
        assert N_EXPERTS == SUBLANES
        params += [route_w, jnp.zeros((SUBLANES, LANES), I32)]
        out_shape += [jax.ShapeDtypeStruct((SUBLANES, bsz * seq), I32),
                      jax.ShapeDtypeStruct((bsz * seq, SUBLANES), F32),
                      jax.ShapeDtypeStruct((SUBLANES, LANES), I32)]
        out_specs += [pl.BlockSpec((SUBLANES, tl), lambda b, j: (0, b * nj + j)),
                      pl.BlockSpec((tl, SUBLANES), lambda b, j: (b * nj + j, 0)),
                      pl.BlockSpec((SUBLANES, LANES), lambda b, j: (0, 0))]
        scratch += [pltpu.VMEM((SUBLANES, LANES), F32), pltpu.VMEM((tl, tl), BF16)]
    return pl.pallas_call(
        functools.partial(_mixer_prompt_kernel, first_layer, route, alpha, tl, d_mix),
        grid=(bsz, nj),
        in_specs=[pl.BlockSpec((1, tl, d), lambda b, j: (b, j, 0))] + [_full_spec(p) for p in params],
        out_specs=tuple(out_specs),
        out_shape=tuple(out_shape),
        scratch_shapes=scratch,
        compiler_params=_cparams(("arbitrary", "arbitrary")),
        name="mixer_prompt",
    )(x, *params)


def _mixer_sample_kernel(first_layer, alpha, n_seq, n_step, d_mix,
                         x_ref, sa_ref, sc_ref, sd_ref, lng_ref, lnb_ref, win_ref, caw_ref, cab_ref,
                         lag_ref, lab_ref, lvg_ref, lvb_ref, wsv_ref, bsv_ref, ccw_ref, wpool_ref,
                         pscale_ref, wout_ref, lmg_ref, lmb_ref,
                         h_out, sa_out, sc_out, sd_out, v_out, cat_ref):
    dq = d_mix // 4
    x = x_ref[...]
    h = _ln(x, lng_ref[...], lnb_ref[...]) if first_layer else x
    hb = h.astype(BF16)

    def proj(i):
        return _dot(hb, win_ref[:, i * dq:(i + 1) * dq])

    def slab(val, i):
        return val[i * n_seq:(i + 1) * n_seq, :]

    def ext_slabs(state_ref, n_past, cur):
        return ([state_ref[i * n_seq:(i + 1) * n_seq, :] for i in range(n_past)]
                + [slab(cur, i) for i in range(n_step)])

    def store_state(out_ref, slabs, n_keep):
        for i, sl in enumerate(slabs[len(slabs) - n_keep:]):
            out_ref[i * n_seq:(i + 1) * n_seq, :] = sl

    a_glu = proj(0) * jax.nn.sigmoid(proj(1))
    ea = ext_slabs(sa_ref, CONV_A_W - 1, a_glu)
    for l in range(n_step):
        acc = jnp.zeros((n_seq, dq), F32)
        for k in range(CONV_A_W):
            acc = acc + ea[l + k] * caw_ref[k:k + 1, :]
        y = _silu(_ln(acc + cab_ref[...], lag_ref[...], lab_ref[...]))
        cat_ref[l * n_seq:(l + 1) * n_seq, 0:dq] = y.astype(BF16)
    store_state(sa_out, ea, CONV_A_W - 1)

    u = _gelu(proj(2))
    v = _ln(_gelu(proj(3)), lvg_ref[...], lvb_ref[...])
    v_out[...] = v
    for l in range(n_step):
        s = jnp.zeros((n_seq, dq), F32) + bsv_ref[l:l + 1, :]
        for m in range(l + 1):
            s = s + slab(v, m) * wsv_ref[l * n_step + m:l * n_step + m + 1, :]
        cat_ref[l * n_seq:(l + 1) * n_seq, dq:2 * dq] = (slab(u, l) * s).astype(BF16)

    c_x = proj(4)
    c_b = proj(5)
    c_c = proj(6)
    gx = c_c * c_x
    ec = ext_slabs(sc_ref, CONV_C_W - 1, gx)
    for l in range(n_step):
        acc = jnp.zeros((n_seq, dq), F32)
        for k in range(CONV_C_W):
            acc = acc + ec[l + k] * ccw_ref[k:k + 1, :]
        cat_ref[l * n_seq:(l + 1) * n_seq, 2 * dq:3 * dq] = (slab(c_b, l) * acc).astype(BF16)
    store_state(sc_out, ec, CONV_C_W - 1)

    d_in = proj(7)
    ed = ext_slabs(sd_ref, POOL_PAST, d_in)
    memo = {}

    def wsum(i, w):
        if i < 0:
            return None
        if w == 1:
            return ed[i]
        if (i, w) not in memo:
            a, b = wsum(i, w // 2), wsum(i - w // 2, w // 2)
            memo[(i, w)] = a if b is None else a + b
        return memo[(i, w)]

    gc = dq // len(POOL_WINDOWS)
    lane_grp = lax.broadcasted_iota(I32, (n_seq, dq), 1) // gc
    for l in range(n_step):
        mean = jnp.zeros((n_seq, dq), F32)
        for g, w in enumerate(POOL_WINDOWS):
            count = min(PAST_LEN + l + 1, w)
            mean = jnp.where(lane_grp == g, wsum(POOL_PAST + l, w) * (1.0 / count), mean)
        dd = (mean - slab(d_in, l)).astype(BF16)
        cat_ref[l * n_seq:(l + 1) * n_seq, 3 * dq:4 * dq] = (
            _dot(dd, wpool_ref[...]) * pscale_ref[...]).astype(BF16)
    store_state(sd_out, ed, POOL_PAST)

    mix = _dot(cat_ref[...], wout_ref[...])
    h_out[...] = _ln(alpha * h + mix, lmg_ref[...], lmb_ref[...])


def _mixer_sample(x_tm, sa, sc, sd, lw, first_layer, alpha, n_seq, n_step):
    rows, d = x_tm.shape
    d_mix = lw['w_out'].shape[0]
    dq = d_mix // 4
    assert n_seq % SUBLANES == 0 and n_step <= CHUNK and PAST_LEN % CHUNK == 0
    ins = [x_tm, sa, sc, sd, lw['ln_in_g'], lw['ln_in_b'], lw['w_in'], lw['conv_a_w'], lw['conv_a_b'],
           lw['ln_a_g'], lw['ln_a_b'], lw['ln_v_g'], lw['ln_v_b'], lw['ws_vec'], lw['bs_vec'],
           lw['conv_c_w'], lw['w_pool_bd'], lw['pool_scale'], lw['w_out'], lw['ln_mix_g'], lw['ln_mix_b']]
    out_shape = (
        jax.ShapeDtypeStruct((rows, d), F32),
        jax.ShapeDtypeStruct(((CONV_A_W - 1) * n_seq, dq), F32),
        jax.ShapeDtypeStruct(((CONV_C_W - 1) * n_seq, dq), F32),
        jax.ShapeDtypeStruct((POOL_PAST * n_seq, dq), F32),
        jax.ShapeDtypeStruct((rows, dq), F32),
    )
    return pl.pallas_call(
        functools.partial(_mixer_sample_kernel, first_layer, alpha, n_seq, n_step, d_mix),
        grid=(1,),
        in_specs=[_full_spec(a) for a in ins],
        out_specs=tuple(pl.BlockSpec(s.shape, lambda i: (0, 0)) for s in out_shape),
        out_shape=out_shape,
        scratch_shapes=[pltpu.VMEM((rows, d_mix), BF16)],
        compiler_params=_cparams(("arbitrary",)),
        name="mixer_sample",
    )(*ins)


def _ffn_dense_kernel(alpha, tf, x_ref, wg_hbm, wu_hbm, wd_hbm, g_ref, b_ref, o_ref, wg_ref, wu_ref, wd_ref, sem):
    @pl.when(pl.program_id(0) == 0)
    def _():
        copies = [pltpu.make_async_copy(src, dst, sem.at[n])
                  for n, (src, dst) in enumerate(((wg_hbm, wg_ref), (wu_hbm, wu_ref), (wd_hbm, wd_ref)))]
        for cp in copies:
            cp.start()
        for cp in copies:
            cp.wait()

    x = x_ref[...]
    xb = x.astype(BF16)
    ffn = None
    for c0 in range(0, wg_ref.shape[1], tf):
        mid = _silu(_dot(xb, wg_ref[:, c0:c0 + tf])) * _dot(xb, wu_ref[:, c0:c0 + tf])
        part = _dot(mid.astype(BF16), wd_ref[c0:c0 + tf, :])
        ffn = part if ffn is None else ffn + part
    o_ref[...] = _ln(alpha * x + ffn, g_ref[...], b_ref[...])


def _ffn_dense(x, wg, wu, wd, g, b, alpha, tm, tf):
    t, d = x.shape
    dff = wg.shape[1]
    assert t % tm == 0 and dff % tf == 0
    any_spec = pl.BlockSpec(memory_space=pl.ANY)
    return pl.pallas_call(
        functools.partial(_ffn_dense_kernel, alpha, tf),
        grid=(t // tm,),
        in_specs=[pl.BlockSpec((tm, d), lambda i: (i, 0)), any_spec, any_spec, any_spec,
                  pl.BlockSpec((1, d), lambda i: (0, 0)),
                  pl.BlockSpec((1, d), lambda i: (0, 0))],
        out_specs=pl.BlockSpec((tm, d), lambda i: (i, 0)),
        out_shape=jax.ShapeDtypeStruct((t, d), F32),
        scratch_shapes=[pltpu.VMEM(wg.shape, BF16), pltpu.VMEM(wu.shape, BF16), pltpu.VMEM(wd.shape, BF16),
                        pltpu.SemaphoreType.DMA((3,))],
        compiler_params=_cparams(("arbitrary",)),
        name="ffn_dense",
    )(x, wg, wu, wd, g, b)


def _route_init(tm, cin_ref, carry_ref, before_ref):
    carry_ref[...] = cin_ref[...].astype(F32)
    row = lax.broadcasted_iota(I32, (tm, tm), 0)
    col = lax.broadcasted_iota(I32, (tm, tm), 1)
    before_ref[...] = jnp.where(row < col, 1.0, 0.0).astype(BF16)


def _route_block(tm, h, wcat_ref, mi_ref, mf_ref, cnt_ref, carry_ref, before_ref):
    h_hi = h.astype(BF16)
    h_lo = (h - h_hi.astype(F32)).astype(BF16)
    p_hi = _dot(h_hi, wcat_ref[...])
    p_lo = _dot(h_lo, wcat_ref[...])
    logits = (p_hi[:, 0:LANES] + (p_lo[:, 0:LANES] + p_hi[:, LANES:2 * LANES])) + p_lo[:, LANES:2 * LANES]
    lg = logits.T[0:N_EXPERTS, :]
    ex = lax.broadcasted_iota(I32, (N_EXPERTS, tm), 0).astype(F32)
    m1 = jnp.max(lg, axis=0, keepdims=True)
    i1 = jnp.min(jnp.where(lg == m1, ex, float(N_EXPERTS)), axis=0, keepdims=True)
    sel1 = ex == i1
    rest = jnp.where(sel1, -jnp.inf, lg)
    m2 = jnp.max(rest, axis=0, keepdims=True)
    i2 = jnp.min(jnp.where(rest == m2, ex, float(N_EXPERTS)), axis=0, keepdims=True)
    sel2 = ex == i2
    e2 = jnp.exp(m2 - m1)
    den = 1.0 + e2
    g1 = 1.0 / den
    g2 = e2 / den

    sel = jnp.where(sel1 | sel2, 1.0, 0.0)
    sel_pad = jnp.concatenate([sel, jnp.zeros_like(sel)], axis=0).astype(BF16)
    base = _dot(sel_pad, before_ref[...])[0:N_EXPERTS, :] + carry_ref[:, 0:1]
    r1 = jnp.sum(jnp.where(sel1, base, 0.0), axis=0, keepdims=True)
    r2 = jnp.sum(jnp.where(sel2, base, 0.0), axis=0, keepdims=True)
    carry_ref[...] = carry_ref[...] + jnp.sum(sel, axis=1, keepdims=True)

    meta = jnp.where(ex == 0, i1, jnp.where(ex == 1, i2, jnp.where(ex == 2, r1, r2)))
    mi_ref[...] = meta.astype(I32)
    gt = jnp.where(ex == 0, g1, jnp.where(ex == 1, g2, 0.0))
    gt = jnp.concatenate([gt, jnp.zeros((LANES - N_EXPERTS, tm), F32)], axis=0)
    mf_ref[...] = gt.T[:, 0:SUBLANES]
    cnt_ref[...] = carry_ref[...].astype(I32)


def _router_kernel(tm, h_ref, wcat_ref, cin_ref, mi_ref, mf_ref, cnt_ref, carry_ref, before_ref):
    @pl.when(pl.program_id(0) == 0)
    def _():
        _route_init(tm, cin_ref, carry_ref, before_ref)

    _route_block(tm, h_ref[...], wcat_ref, mi_ref, mf_ref, cnt_ref, carry_ref, before_ref)


def _router(h, wr_cat, counts_in):
    t, d = h.shape
    tm = min(TM_ROUTE, t)
    assert t % tm == 0 and N_EXPERTS == SUBLANES
    return pl.pallas_call(
        functools.partial(_router_kernel, tm),
        grid=(t // tm,),
        in_specs=[pl.BlockSpec((tm, d), lambda i: (i, 0)), _full_spec(wr_cat), _full_spec(counts_in)],
        out_specs=(pl.BlockSpec((SUBLANES, tm), lambda i: (0, i)),
                   pl.BlockSpec((tm, SUBLANES), lambda i: (i, 0)),
                   pl.BlockSpec((SUBLANES, LANES), lambda i: (0, 0))),
        out_shape=(jax.ShapeDtypeStruct((SUBLANES, t), I32),
                   jax.ShapeDtypeStruct((t, SUBLANES), F32),
                   jax.ShapeDtypeStruct((SUBLANES, LANES), I32)),
        scratch_shapes=[pltpu.VMEM((SUBLANES, LANES), F32), pltpu.VMEM((tm, tm), BF16)],
        compiler_params=_cparams(("arbitrary",)),
        name="router",
    )(h, wr_cat, counts_in)


def _row_copy(src, src_row, dst, dst_row, sem):
    return pltpu.make_async_copy(src.at[pl.ds(src_row, 1), :], dst.at[pl.ds(dst_row, 1), :], sem)


def _rows_wait(src, dst, n_rows, sem):
    pltpu.make_async_copy(src.at[pl.ds(0, n_rows), :], dst.at[pl.ds(0, n_rows), :], sem).wait()


def _dispatch_kernel(tm, tm_expert, n_blocks, blk_ranges, zi_ref, dest_ref, *rest):
    n_src = len(blk_ranges)
    h_refs = rest[:n_src]
    xs_ref, xbuf, zblk, lsems, rsems, zsem = rest[n_src:]
    i = pl.program_id(0)
    n = pl.num_programs(0)

    def block_load(g, start):
        slot = lax.rem(g, N_STAGE)
        for h_ref, (b0, b1) in zip(h_refs, blk_ranges):
            @pl.when((g >= b0) & (g < b1))
            def _(h_ref=h_ref, b0=b0):
                cp = pltpu.make_async_copy(h_ref.at[pl.ds((g - b0) * tm, tm), :], xbuf.at[slot],
                                           lsems.at[slot])
                if start:
                    cp.start()
                else:
                    cp.wait()

    def rows_wait(g):
        slot = lax.rem(g, N_STAGE)
        for _k in range(TOP_K):
            _rows_wait(xbuf.at[slot], xs_ref, tm, rsems.at[slot])

    @pl.when(i == 0)
    def _():
        block_load(i, True)

    @pl.when(i >= N_STAGE - 1)
    def _():
        rows_wait(i - (N_STAGE - 1))

    @pl.when(i + 1 < n)
    def _():
        block_load(i + 1, True)

    block_load(i, False)
    cur = lax.rem(i, N_STAGE)

    def issue(j, carry):
        r0 = pl.multiple_of(j * SUBLANES, SUBLANES)
        for u in range(SUBLANES):
            for k in range(TOP_K):
                dst = dest_ref[0, 0, j * SUBLANES + (k * tm + u)]
                _row_copy(xbuf.at[cur], r0 + u, xs_ref, dst, rsems.at[cur]).start(priority=k)
        return carry

    lax.fori_loop(0, tm // SUBLANES, issue, 0)

    @pl.when(i == n - 1)
    def _():
        for back in range(N_STAGE - 2, -1, -1):
            @pl.when(i >= back)
            def _(back=back):
                rows_wait(i - back)

    @pl.when(i == 0)
    def _():
        zblk[...] = jnp.zeros_like(zblk)
        for e in range(N_EXPERTS):
            lo = zi_ref[e] + zi_ref[N_EXPERTS + e]
            hi = zi_ref[e] + zi_ref[2 * N_EXPERTS + e]

            def zissue(r, carry):
                _row_copy(zblk, 0, xs_ref, r, zsem).start()
                return carry

            def zdrain(r, carry):
                _row_copy(zblk, 0, xs_ref, r, zsem).wait()
                return carry

            lax.fori_loop(lo, hi, zissue, 0)
            lax.fori_loop(lo, hi, zdrain, 0)

        def bcopy(b):
            return pltpu.make_async_copy(zblk, xs_ref.at[pl.ds(b * tm_expert, tm_expert), :], zsem)

        def bissue(b, carry):
            bcopy(b).start()
            return carry

        def bdrain(b, carry):
            bcopy(b).wait()
            return carry

        lax.fori_loop(zi_ref[3 * N_EXPERTS], n_blocks, bissue, 0)
        lax.fori_loop(zi_ref[3 * N_EXPERTS], n_blocks, bdrain, 0)


def _dispatch(hs_list, dest_blk, zinfo, n_blocks, tm_expert):
    d = hs_list[0].shape[1]
    nb = dest_blk.shape[0]
    tm = dest_blk.shape[2] // TOP_K
    blk_ranges, b0 = [], 0
    for h in hs_list:
        assert h.shape[0] % tm == 0 and h.shape[0] >= tm
        blk_ranges.append((b0, b0 + h.shape[0] // tm))
        b0 = blk_ranges[-1][1]
    assert b0 == nb
    any_spec = pl.BlockSpec(memory_space=pl.ANY)
    return pl.pallas_call(
        functools.partial(_dispatch_kernel, tm, tm_expert, n_blocks, tuple(blk_ranges)),
        grid_spec=pltpu.PrefetchScalarGridSpec(
            num_scalar_prefetch=1,
            grid=(nb,),
            in_specs=[pl.BlockSpec((1, 1, TOP_K * tm), lambda i, zi: (i, 0, 0), memory_space=pltpu.SMEM)]
            + [any_spec] * len(hs_list),
            out_specs=any_spec,
            scratch_shapes=[pltpu.VMEM((N_STAGE, tm, d), F32), pltpu.VMEM((tm_expert, d), F32),
                            pltpu.SemaphoreType.DMA((N_STAGE,)), pltpu.SemaphoreType.DMA((N_STAGE,)),
                            pltpu.SemaphoreType.DMA(())],
        ),
        out_shape=jax.ShapeDtypeStruct((n_blocks * tm_expert, d), F32),
        compiler_params=_cparams(("arbitrary",)),
        name="moe_dispatch",
    )(zinfo, dest_blk, *hs_list)


def _ffn_moe_kernel(tf, be_ref, nu_ref, x_ref, wg_hbm, wu_hbm, wd_hbm, y_ref,
                    wg_ref, wu_ref, wd_ref, stage_in, stage_out, sems):
    b = pl.program_id(0)
    nb = pl.num_programs(0)
    d, dff = wg_ref.shape
    n_slots = stage_in.shape[0]
    rows_in = stage_in.shape[1]
    rows_out = stage_out.shape[1]
    n_in, n_out = 2 * (d // rows_in), dff // rows_out
    e = be_ref[b]
    fresh = (b == 0) | (e != be_ref[jnp.maximum(b - 1, 0)])
    e_next = be_ref[jnp.minimum(b + 1, nb - 1)]

    def chunk_in(ex, n):
        src, dst = (wg_hbm, wg_ref) if n < n_in // 2 else (wu_hbm, wu_ref)
        rows = pl.ds((n % (n_in // 2)) * rows_in, rows_in)
        stage = stage_in.at[n % n_slots]
        return pltpu.make_async_copy(src.at[ex, rows, :], stage, sems.at[0, n % n_slots]), stage, dst.at[rows, :]

    def chunk_out(ex, n):
        rows = pl.ds(n * rows_out, rows_out)
        stage = stage_out.at[n % n_slots]
        return pltpu.make_async_copy(wd_hbm.at[ex, rows, :], stage, sems.at[1, n % n_slots]), stage, wd_ref.at[rows, :]

    def request_first(ex):
        for n in range(n_slots):
            chunk_in(ex, n)[0].start()
            chunk_out(ex, n)[0].start()

    @pl.when((b < nu_ref[0]) & fresh)
    def _():
        @pl.when(b == 0)
        def _():
            request_first(e)

        for n in range(n_in):
            for chunk, count in ((chunk_in, n_in), (chunk_out, n_out)):
                if n < count:
                    copy, staged, dst = chunk(e, n)
                    copy.wait()
                    dst[...] = staged[...].astype(BF16)
                    if n + n_slots < count:
                        chunk(e, n + n_slots)[0].start()

    @pl.when((b + 1 < nu_ref[0]) & (e_next != e))
    def _():
        request_first(e_next)

    @pl.when(b < nu_ref[0])
    def _():
        xb = x_ref[...].astype(BF16)
        ffn = None
        for c0 in range(0, dff, tf):
            mid = _silu(_dot(xb, wg_ref[:, c0:c0 + tf])) * _dot(xb, wu_ref[:, c0:c0 + tf])
            part = _dot(mid.astype(BF16), wd_ref[c0:c0 + tf, :])
            ffn = part if ffn is None else ffn + part
        y_ref[...] = ffn

    @pl.when(b >= nu_ref[0])
    def _():
        y_ref[...] = jnp.zeros_like(y_ref)


def _ffn_moe(xs, wg, wu, wd, block_e, n_used, tm, tf):
    cap, d = xs.shape
    dff = wg.shape[2]
    rows_in, rows_out = 128, 512
    assert cap % tm == 0 and dff % tf == 0 and d % rows_in == 0 and dff % rows_out == 0
    assert min(2 * (d // rows_in), dff // rows_out) >= N_WSTAGE
    any_spec = pl.BlockSpec(memory_space=pl.ANY)
    return pl.pallas_call(
        functools.partial(_ffn_moe_kernel, tf),
        grid_spec=pltpu.PrefetchScalarGridSpec(
            num_scalar_prefetch=2,
            grid=(cap // tm,),
            in_specs=[pl.BlockSpec((tm, d), lambda b, be, nu: (jnp.minimum(b, nu[0] - 1), 0)),
                      any_spec, any_spec, any_spec],
            out_specs=pl.BlockSpec((tm, d), lambda b, be, nu: (b, 0)),
            scratch_shapes=[pltpu.VMEM((d, dff), BF16), pltpu.VMEM((d, dff), BF16), pltpu.VMEM((dff, d), BF16),
                            pltpu.VMEM((N_WSTAGE, rows_in, dff), F32), pltpu.VMEM((N_WSTAGE, rows_out, d), F32),
                            pltpu.SemaphoreType.DMA((2, N_WSTAGE))],
        ),
        out_shape=jax.ShapeDtypeStruct((cap, d), F32),
        compiler_params=_cparams(("arbitrary",)),
        name="ffn_moe",
    )(block_e, n_used, xs, wg, wu, wd)


def _combine_kernel(alpha, tm, dcur_ref, dnext_ref, h_ref, gate_ref, g_ref, b_ref, ys_ref, o_ref, ybuf, sems):
    i = pl.program_id(0)
    n = pl.num_programs(0)

    def issue(dref, slot):
        def body(j, carry):
            r0 = pl.multiple_of(j * SUBLANES, SUBLANES)
            for u in range(SUBLANES):
                for k in range(TOP_K):
                    src = dref[0, 0, j * SUBLANES + (k * tm + u)]
                    _row_copy(ys_ref, src, ybuf.at[slot, k], r0 + u, sems.at[slot]).start(priority=k)
            return carry

        lax.fori_loop(0, tm // SUBLANES, body, 0)

    @pl.when(i == 0)
    def _():
        issue(dcur_ref, 0)

    @pl.when(i + 1 < n)
    def _():
        issue(dnext_ref, (i + 1) % 2)

    slot = i % 2
    for k in range(TOP_K):
        _rows_wait(ys_ref, ybuf.at[slot, k], tm, sems.at[slot])

    gates = gate_ref[...]
    ffn = gates[:, 0:1] * ybuf[slot, 0] + gates[:, 1:2] * ybuf[slot, 1]
    o_ref[...] = _ln(alpha * h_ref[...] + ffn, g_ref[...], b_ref[...])


def _combine(h, dest_blk, gates, ys, g, b, alpha):
    t, d = h.shape
    nb = dest_blk.shape[0]
    tm = dest_blk.shape[2] // TOP_K
    dest_spec = lambda imap: pl.BlockSpec((1, 1, TOP_K * tm), imap, memory_space=pltpu.SMEM)
    return pl.pallas_call(
        functools.partial(_combine_kernel, alpha, tm),
        grid=(nb,),
        in_specs=[dest_spec(lambda i: (i, 0, 0)),
                  dest_spec(lambda i: (jnp.minimum(i + 1, nb - 1), 0, 0)),
                  pl.BlockSpec((tm, d), lambda i: (i, 0)),
                  pl.BlockSpec((tm, SUBLANES), lambda i: (i, 0)),
                  pl.BlockSpec((1, d), lambda i: (0, 0)),
                  pl.BlockSpec((1, d), lambda i: (0, 0)),
                  pl.BlockSpec(memory_space=pl.ANY)],
        out_specs=pl.BlockSpec((tm, d), lambda i: (i, 0)),
        out_shape=jax.ShapeDtypeStruct((t, d), F32),
        scratch_shapes=[pltpu.VMEM((2, TOP_K, tm, d), F32), pltpu.SemaphoreType.DMA((2,))],
        compiler_params=_cparams(("arbitrary",)),
        name="moe_combine",
    )(dest_blk, dest_blk, h, gates, g, b, ys)


def _ffn_routed(hs_list, first_routed, mw, g, b, alpha, tm_expert):
    routed = [first_routed[:2]]
    counts_in = first_routed[2]
    for h in hs_list[1:]:
        meta_i, gates, counts_in = _router(h, mw['w_router_cat'], counts_in)
        routed.append((meta_i, gates))
    counts = counts_in[:N_EXPERTS, 0]
    padded = (counts + tm_expert - 1) // tm_expert * tm_expert
    pad_end = jnp.cumsum(padded)
    pad_start = (pad_end - padded).astype(I32)
    n_assign = sum(h.shape[0] for h in hs_list) * TOP_K
    n_blocks = -(-n_assign // tm_expert) + N_EXPERTS
    cap = n_blocks * tm_expert
    n_used = (pad_end[-1] // tm_expert).astype(I32)
    blk_start = jnp.minimum(jnp.arange(n_blocks, dtype=I32), n_used - 1) * tm_expert
    block_e = jnp.minimum(jnp.sum(blk_start[:, None] >= pad_end[None, :], axis=1), N_EXPERTS - 1).astype(I32)
    zinfo = jnp.concatenate([pad_start, counts, padded, n_used.reshape(1)]).astype(I32)

    dests = []
    for h, (meta_i, _) in zip(hs_list, routed):
        dest = meta_i[TOP_K:2 * TOP_K, :]
        for e in range(N_EXPERTS):
            dest = dest + jnp.where(meta_i[0:TOP_K, :] == e, pad_start[e], 0)
        nb = h.shape[0] // TM_MOVE
        dests.append(dest.reshape(TOP_K, nb, TM_MOVE).transpose(1, 0, 2).reshape(nb, 1, TOP_K * TM_MOVE))
    xs = _dispatch(hs_list, jnp.concatenate(dests, axis=0), zinfo, n_blocks, tm_expert)
    ys = _ffn_moe(xs, mw['w_e_gate'], mw['w_e_up'], mw['w_e_down'], block_e, n_used.reshape(1), tm_expert,
                  TF_EXPERT)
    return [_combine(h, dest_blk, gates, ys, g, b, alpha)
            for h, dest_blk, (_, gates) in zip(hs_list, dests, routed)]


def _prep_layer(i, p, n_step):
    dq = p['conv_a_w'].shape[2]
    hd = dq // N_HEADS_B
    tril = jnp.tril(jnp.ones((CHUNK, CHUNK), dtype=bool))
    w_s = p['w_s'][i]
    b_s = p['b_s'][i]
    wp = p['w_pool'][i]
    ng, gc = wp.shape[0], wp.shape[1]
    w_pool_bd = (jnp.eye(ng, dtype=F32)[:, None, :, None] * wp[:, :, None, :]).reshape(ng * gc, ng * gc)
    row = lambda a: a.reshape(1, -1)
    return dict(
        ln_in_g=row(p['ln_in_g']), ln_in_b=row(p['ln_in_b']),
        w_in=p['w_in'][i].astype(BF16),
        conv_a_w=p['conv_a_w'][i], conv_a_b=row(p['conv_a_b'][i]),
        ln_a_g=row(p['ln_a_g'][i]), ln_a_b=row(p['ln_a_b'][i]),
        ln_v_g=row(p['ln_v_g'][i]), ln_v_b=row(p['ln_v_b'][i]),
        ws_tril=jnp.where(tril[None], w_s, 0).astype(BF16),
        bs_mat=jnp.repeat(b_s.T, hd, axis=1),
        ws_vec=jnp.repeat(jnp.where(tril[None], w_s, 0)[:, :n_step, :n_step].transpose(1, 2, 0)
                          .reshape(n_step * n_step, N_HEADS_B), hd, axis=1),
        bs_vec=jnp.repeat(b_s[:, :n_step].T, hd, axis=1),
        conv_c_w=p['conv_c_w'][i],
        w_pool_bd=w_pool_bd.astype(BF16), pool_scale=row(p['pool_scale'][i]),
        w_out=p['w_out'][i].astype(BF16),
        ln_mix_g=row(p['ln_mix_g'][i]), ln_mix_b=row(p['ln_mix_b'][i]),
    )


def kernel(x_prompt, x_sample, state_conv_a, state_conv_c, state_pool_d, ln_in_g, ln_in_b, w_in, conv_a_w,
           conv_a_b, ln_a_g, ln_a_b, ln_v_g, ln_v_b, w_s, b_s, conv_c_w, w_pool, pool_scale, w_out,
           ln_mix_g, ln_mix_b, w_ff_gate, w_ff_up, w_ff_down, w_router, w_e_gate, w_e_up, w_e_down,
           ln_ffn_g, ln_ffn_b):
    p = dict(ln_in_g=ln_in_g, ln_in_b=ln_in_b, w_in=w_in, conv_a_w=conv_a_w, conv_a_b=conv_a_b,
             ln_a_g=ln_a_g, ln_a_b=ln_a_b, ln_v_g=ln_v_g, ln_v_b=ln_v_b, w_s=w_s, b_s=b_s,
             conv_c_w=conv_c_w, w_pool=w_pool, pool_scale=pool_scale, w_out=w_out,
             ln_mix_g=ln_mix_g, ln_mix_b=ln_mix_b)
    depth = w_in.shape[0]
    bsz, seq, d = x_prompt.shape
    n_seq, n_step, _ = x_sample.shape
    dq = conv_a_w.shape[2]
    alpha = float((2.0 * depth) ** 0.25)

    hp = x_prompt
    hs = x_sample.transpose(1, 0, 2).reshape(n_step * n_seq, d)
    outs = {k: [] for k in ('pa', 'pc', 'pd', 'pv', 'sa', 'sc', 'sd', 'sv')}
    for i in range(depth):
        lw = _prep_layer(i, p, n_step)
        j = i // 2
        routed = i % 2 == 1
        if routed:
            wr = jnp.pad(w_router[j], ((0, 0), (0, LANES - N_EXPERTS)))
            wr_hi = wr.astype(BF16)
            wr_lo = (wr - wr_hi.astype(F32)).astype(BF16)
            wr_cat = jnp.concatenate([wr_hi, wr_lo], axis=1)
        hp, pa, pc, pd, pv, *route_p = _mixer_prompt(hp, lw, i == 0, alpha, wr_cat if routed else None)
        tm_state = lambda s: s.transpose(1, 0, 2).reshape(-1, dq)
        hs, sa, sc, sd, sv = _mixer_sample(hs, tm_state(state_conv_a[i]), tm_state(state_conv_c[i]),
                                           tm_state(state_pool_d[i]), lw, i == 0, alpha, n_seq, n_step)
        lg, lb = ln_ffn_g[i].reshape(1, -1), ln_ffn_b[i].reshape(1, -1)
        if not routed:
            wg, wu, wd = w_ff_gate[j].astype(BF16), w_ff_up[j].astype(BF16), w_ff_down[j].astype(BF16)
            hp = _ffn_dense(hp.reshape(bsz * seq, d), wg, wu, wd, lg, lb, alpha, TM_DENSE,
                            TF_DENSE).reshape(bsz, seq, d)
            hs = _ffn_dense(hs, wg, wu, wd, lg, lb, alpha, hs.shape[0], TF_DENSE)
        else:
            mw = dict(w_router_cat=wr_cat, w_e_gate=w_e_gate[j], w_e_up=w_e_up[j], w_e_down=w_e_down[j])
            hp, hs = _ffn_routed([hp.reshape(bsz * seq, d), hs], route_p, mw, lg, lb, alpha, TM_EXPERT)
            hp = hp.reshape(bsz, seq, d)
        back = lambda a: a.reshape(-1, n_seq, dq).transpose(1, 0, 2)
        for k, val in zip(('pa', 'pc', 'pd', 'pv', 'sa', 'sc', 'sd', 'sv'),
                          (pa, pc, pd, pv, back(sa), back(sc), back(sd), back(sv))):
            outs[k].append(val)
    y_sample = hs.reshape(n_step, n_seq, d).transpose(1, 0, 2)
    st = lambda k: jnp.stack(outs[k])
    return (hp, y_sample, st('pa'), st('pc'), st('pd'), st('pv'), st('sa'), st('sc'), st('sd'), st('sv'))
```

```python
import functools

import numpy as np
import jax
import jax.numpy as jnp
from jax import lax
from jax.experimental import pallas as pl
from jax.experimental.pallas import tpu as pltpu

F32 = jnp.float32
BF16 = jnp.bfloat16
I32 = jnp.int32

PAST_LEN = 16384
CHUNK = 128
N_HEADS_B = 4
POOL_WINDOWS = (2, 4, 8, 16)
POOL_PAST = 15
CONV_A_W = 31
CONV_C_W = 3
N_EXPERTS = 8
TOP_K = 2
LN_EPS = 1e-5
SQRT_HALF = float(np.sqrt(0.5).astype(np.float32))

LANES = 128
SUBLANES = 8
VMEM_LIMIT_BYTES = 56 * 1024 * 1024

HIST_A = 32
HIST_C = 8
HIST_D = 24

TL_PROMPT = 512
TM_DENSE = 512
TF_DENSE = 1792
TF_EXPERT = 1792
TM_ROUTE = 512
TM_MOVE = 256
TM_EXPERT = 512
N_STAGE = 3
N_WSTAGE = 4


def _ln(x, g, b):
    mu = jnp.mean(x, axis=-1, keepdims=True)
    xc = x - mu
    var = jnp.mean(xc * xc, axis=-1, keepdims=True)
    return xc * lax.rsqrt(var + LN_EPS) * g + b


def _gelu(x):
    return 0.5 * x * (1.0 + lax.erf(x * SQRT_HALF))


def _silu(x):
    return x * jax.nn.sigmoid(x)


def _dot(a, b):
    return jnp.dot(a, b, preferred_element_type=F32)


def _cparams(sem):
    return pltpu.CompilerParams(dimension_semantics=sem, vmem_limit_bytes=VMEM_LIMIT_BYTES)


def _mixer_prompt_kernel(first_layer, route, alpha, tl, d_mix, nj, n_real,
                         x_ref, lng_ref, lnb_ref, win_ref, caw_ref, cab_ref, lag_ref, lab_ref,
                         lvg_ref, lvb_ref, ws_ref, bsm_ref, ccw_ref, wpool_ref, pscale_ref,
                         wout_ref, lmg_ref, lmb_ref, *refs):
    if route:
        wcat_ref, cin_ref = refs[:2]
        refs = refs[2:]
    h_out, sa_out, sc_out, sd_out, v_out = refs[:5]
    refs = refs[5:]
    if route:
        mi_ref, mf_ref, cnt_ref = refs[:3]
        refs = refs[3:]
    ext_a, ext_c, ext_d, cat_ref, z_ref, hb_ref, mix_ref = refs[:7]
    dq = d_mix // 4
    step = pl.program_id(0)
    j = lax.rem(step, nj)
    if route:
        carry_ref, before_ref = refs[7:]

        @pl.when(step == 0)
        def _():
            _route_init(tl, cin_ref, carry_ref, before_ref)

    @pl.when(j == 0)
    def _():
        ext_a[0, 0:HIST_A, :] = jnp.zeros((HIST_A, dq), F32)
        ext_c[0:HIST_C, :] = jnp.zeros((HIST_C, dq), F32)
        ext_d[0:HIST_D, :] = jnp.zeros((HIST_D, dq), F32)

    d = x_ref.shape[2]
    rb = CHUNK

    def resid(r0, n):
        return h_out[r0:r0 + n, :] if first_layer else x_ref[0, r0:r0 + n, :]

    for r0 in range(0, tl, rb):
        x = x_ref[0, r0:r0 + rb, :]
        if first_layer:
            x = _ln(x, lng_ref[...], lnb_ref[...])
            h_out[r0:r0 + rb, :] = x
        hb_ref[r0:r0 + rb, :] = x.astype(BF16)

    z_ref[...] = _dot(hb_ref[...], win_ref[...])

    def zcol(r0, i, n=rb):
        return z_ref[r0:r0 + n, i * dq:(i + 1) * dq]

    chunk_rows = range(0, tl, CHUNK)

    for r0 in chunk_rows:
        ext_a[0, HIST_A + r0:HIST_A + r0 + rb, :] = zcol(r0, 0) * jax.nn.sigmoid(zcol(r0, 1))
    n_ext = HIST_A + tl
    for r0 in range(0, n_ext - SUBLANES, rb):
        n = min(rb, n_ext - SUBLANES - r0)
        blk = ext_a[0, r0:r0 + n + SUBLANES, :]
        for s in range(1, SUBLANES):
            ext_a[s, r0:r0 + n, :] = pltpu.roll(blk, n + SUBLANES - s, axis=0)[0:n, :]
    off_a = HIST_A - (CONV_A_W - 1)
    for q0 in range(0, tl, 64):
        acc = jnp.zeros((64, dq), F32)
        for k in range(CONV_A_W):
            s = (off_a + k) % SUBLANES
            row = off_a + k - s + q0
            acc = acc + ext_a[s, row:row + 64, :] * caw_ref[k:k + 1, :]
        y = _silu(_ln(acc + cab_ref[...], lag_ref[...], lab_ref[...]))
        cat_ref[q0:q0 + 64, 0:dq] = y.astype(BF16)

    hd = dq // N_HEADS_B
    lane_head = lax.broadcasted_iota(I32, (CHUNK, dq), 1) // hd
    for r0 in chunk_rows:
        v = _ln(_gelu(zcol(r0, 3)), lvg_ref[...], lvb_ref[...])
        sm = bsm_ref[...]
        for hh in range(N_HEADS_B):
            vm = jnp.where(lane_head == hh, v, 0.0).astype(BF16)
            sm = sm + _dot(ws_ref[hh], vm)
        cat_ref[r0:r0 + CHUNK, dq:2 * dq] = (_gelu(zcol(r0, 2)) * sm).astype(BF16)
        if r0 == tl - CHUNK:
            @pl.when(j == nj - 1)
            def _(v=v):
                v_out[0] = v

    off_c = HIST_C - (CONV_C_W - 1)
    for r0 in chunk_rows:
        ext_c[HIST_C + r0:HIST_C + r0 + rb, :] = zcol(r0, 6) * zcol(r0, 4)
    for r0 in chunk_rows:
        conv_c = jnp.zeros((CHUNK, dq), F32)
        for k in range(CONV_C_W):
            conv_c = conv_c + ext_c[off_c + r0 + k:off_c + r0 + k + CHUNK, :] * ccw_ref[k:k + 1, :]
        cat_ref[r0:r0 + CHUNK, 2 * dq:3 * dq] = (zcol(r0, 5) * conv_c).astype(BF16)

    gc = dq // len(POOL_WINDOWS)
    lane_grp = lax.broadcasted_iota(I32, (CHUNK, dq), 1) // gc
    lead = 2 * SUBLANES
    for r0 in chunk_rows:
        ext_d[HIST_D + r0:HIST_D + r0 + rb, :] = zcol(r0, 7)
    for r0 in chunk_rows:
        e = ext_d[HIST_D + r0 - lead:HIST_D + r0 + CHUNK, :]
        s2 = e + pltpu.roll(e, 1, axis=0)
        s4 = s2 + pltpu.roll(s2, 2, axis=0)
        s8 = s4 + pltpu.roll(s4, 4, axis=0)
        s16 = s8 + pltpu.roll(s8, 8, axis=0)
        sums = (s2, s4, s8, s16)
        pos = (j * tl + r0 + lax.broadcasted_iota(I32, (CHUNK, 1), 0)).astype(F32)
        mean = jnp.zeros((CHUNK, dq), F32)
        for g, w in enumerate(POOL_WINDOWS):
            inv = 1.0 / jnp.minimum(pos + 1.0, float(w))
            mean = jnp.where(lane_grp == g, sums[g][lead:lead + CHUNK, :] * inv, mean)
        dd = (mean - e[lead:lead + CHUNK, :]).astype(BF16)
        cat_ref[r0:r0 + CHUNK, 3 * dq:4 * dq] = (_dot(dd, wpool_ref[...]) * pscale_ref[...]).astype(BF16)

    mix_ref[...] = _dot(cat_ref[...], wout_ref[...])
    for r0 in range(0, tl, 64):
        h_out[r0:r0 + 64, :] = _ln(alpha * resid(r0, 64) + mix_ref[r0:r0 + 64, :], lmg_ref[...], lmb_ref[...])
    if route:
        @pl.when(step < n_real)
        def _():
            _route_block(tl, h_out[...], wcat_ref, mi_ref, mf_ref, cnt_ref, carry_ref, before_ref)

    @pl.when(j == nj - 1)
    def _():
        sa_out[0] = ext_a[0, HIST_A + tl - (CONV_A_W - 1):HIST_A + tl, :]
        sc_out[0] = ext_c[HIST_C + tl - (CONV_C_W - 1):HIST_C + tl, :]
        sd_out[0] = ext_d[HIST_D + tl - POOL_PAST:HIST_D + tl, :]

    ext_a[0, 0:HIST_A, :] = ext_a[0, tl:tl + HIST_A, :]
    ext_c[0:HIST_C, :] = ext_c[tl:tl + HIST_C, :]
    ext_d[0:HIST_D, :] = ext_d[tl:tl + HIST_D, :]


def _full_spec(arr):
    nd = arr.ndim
    return pl.BlockSpec(arr.shape, lambda *_: (0,) * nd)


def _mixer_prompt(x, lw, first_layer, alpha, route_w=None, spare_rows=0):
    bsz, seq, d = x.shape
    d_mix = lw['w_out'].shape[0]
    dq = d_mix // 4
    tl = TL_PROMPT
    nj = seq // tl
    n_real = bsz * nj
    route = route_w is not None
    assert seq % tl == 0 and tl % CHUNK == 0 and seq >= CHUNK and spare_rows % tl == 0
    n_steps = n_real + spare_rows // tl
    params = [lw['ln_in_g'], lw['ln_in_b'], lw['w_in'], lw['conv_a_w'], lw['conv_a_b'], lw['ln_a_g'],
              lw['ln_a_b'], lw['ln_v_g'], lw['ln_v_b'], lw['ws_tril'], lw['bs_mat'], lw['conv_c_w'],
              lw['w_pool_bd'], lw['pool_scale'], lw['w_out'], lw['ln_mix_g'], lw['ln_mix_b']]
    out_shape = [
        jax.ShapeDtypeStruct((n_steps * tl, d), F32),
        jax.ShapeDtypeStruct((bsz, CONV_A_W - 1, dq), F32),
        jax.ShapeDtypeStruct((bsz, CONV_C_W - 1, dq), F32),
        jax.ShapeDtypeStruct((bsz, POOL_PAST, dq), F32),
        jax.ShapeDtypeStruct((bsz, CHUNK, dq), F32),
    ]
    row_of = lambda s: jnp.minimum(s // nj, bsz - 1)
    tile_of = lambda s: jnp.minimum(s, n_real - 1)
    state_spec = lambda rows: pl.BlockSpec((1, rows, dq), lambda s: (row_of(s), 0, 0))
    out_specs = [pl.BlockSpec((tl, d), lambda s: (s, 0)),
                 state_spec(CONV_A_W - 1), state_spec(CONV_C_W - 1), state_spec(POOL_PAST), state_spec(CHUNK)]
    scratch = [pltpu.VMEM((SUBLANES, HIST_A + tl, dq), F32), pltpu.VMEM((HIST_C + tl, dq), F32),
               pltpu.VMEM((HIST_D + tl, dq), F32), pltpu.VMEM((tl, d_mix), BF16),
               pltpu.VMEM((tl, lw['w_in'].shape[1]), F32), pltpu.VMEM((tl, d), BF16),
               pltpu.VMEM((tl, d), F32)]
    if route:
        assert N_EXPERTS == SUBLANES
        params += [route_w, jnp.zeros((SUBLANES, LANES), I32)]
        out_shape += [jax.ShapeDtypeStruct((SUBLANES, bsz * seq), I32),
                      jax.ShapeDtypeStruct((bsz * seq, SUBLANES), F32),
                      jax.ShapeDtypeStruct((SUBLANES, LANES), I32)]
        out_specs += [pl.BlockSpec((SUBLANES, tl), lambda s: (0, tile_of(s))),
                      pl.BlockSpec((tl, SUBLANES), lambda s: (tile_of(s), 0)),
                      pl.BlockSpec((SUBLANES, LANES), lambda s: (0, 0))]
        scratch += [pltpu.VMEM((SUBLANES, LANES), F32), pltpu.VMEM((tl, tl), BF16)]
    return pl.pallas_call(
        functools.partial(_mixer_prompt_kernel, first_layer, route, alpha, tl, d_mix, nj, n_real),
        grid=(n_steps,),
        in_specs=[pl.BlockSpec((1, tl, d), lambda s: (row_of(s), s % nj, 0))] + [_full_spec(p) for p in params],
        out_specs=tuple(out_specs),
        out_shape=tuple(out_shape),
        scratch_shapes=scratch,
        compiler_params=_cparams(("arbitrary",)),
        name="mixer_prompt",
    )(x, *params)


def _mixer_sample_kernel(first_layer, alpha, n_seq, n_step, d_mix, into_shared,
                         x_ref, sa_ref, sc_ref, sd_ref, lng_ref, lnb_ref, win_ref, caw_ref, cab_ref,
                         lag_ref, lab_ref, lvg_ref, lvb_ref, wsv_ref, bsv_ref, ccw_ref, wpool_ref,
                         pscale_ref, wout_ref, lmg_ref, lmb_ref, *refs):
    h_out, sa_out, sc_out, sd_out, v_out, cat_ref = refs[1:] if into_shared else refs
    dq = d_mix // 4
    x = x_ref[...]
    h = _ln(x, lng_ref[...], lnb_ref[...]) if first_layer else x
    hb = h.astype(BF16)

    def proj(i):
        return _dot(hb, win_ref[:, i * dq:(i + 1) * dq])

    def slab(val, i):
        return val[i * n_seq:(i + 1) * n_seq, :]

    def ext_slabs(state_ref, n_past, cur):
        return ([state_ref[i * n_seq:(i + 1) * n_seq, :] for i in range(n_past)]
                + [slab(cur, i) for i in range(n_step)])

    def store_state(out_ref, slabs, n_keep):
        for i, sl in enumerate(slabs[len(slabs) - n_keep:]):
            out_ref[i * n_seq:(i + 1) * n_seq, :] = sl

    a_glu = proj(0) * jax.nn.sigmoid(proj(1))
    ea = ext_slabs(sa_ref, CONV_A_W - 1, a_glu)
    for l in range(n_step):
        acc = jnp.zeros((n_seq, dq), F32)
        for k in range(CONV_A_W):
            acc = acc + ea[l + k] * caw_ref[k:k + 1, :]
        y = _silu(_ln(acc + cab_ref[...], lag_ref[...], lab_ref[...]))
        cat_ref[l * n_seq:(l + 1) * n_seq, 0:dq] = y.astype(BF16)
    store_state(sa_out, ea, CONV_A_W - 1)

    u = _gelu(proj(2))
    v = _ln(_gelu(proj(3)), lvg_ref[...], lvb_ref[...])
    v_out[...] = v
    for l in range(n_step):
        s = jnp.zeros((n_seq, dq), F32) + bsv_ref[l:l + 1, :]
        for m in range(l + 1):
            s = s + slab(v, m) * wsv_ref[l * n_step + m:l * n_step + m + 1, :]
        cat_ref[l * n_seq:(l + 1) * n_seq, dq:2 * dq] = (slab(u, l) * s).astype(BF16)

    c_x = proj(4)
    c_b = proj(5)
    c_c = proj(6)
    gx = c_c * c_x
    ec = ext_slabs(sc_ref, CONV_C_W - 1, gx)
    for l in range(n_step):
        acc = jnp.zeros((n_seq, dq), F32)
        for k in range(CONV_C_W):
            acc = acc + ec[l + k] * ccw_ref[k:k + 1, :]
        cat_ref[l * n_seq:(l + 1) * n_seq, 2 * dq:3 * dq] = (slab(c_b, l) * acc).astype(BF16)
    store_state(sc_out, ec, CONV_C_W - 1)

    d_in = proj(7)
    ed = ext_slabs(sd_ref, POOL_PAST, d_in)
    memo = {}

    def wsum(i, w):
        if i < 0:
            return None
        if w == 1:
            return ed[i]
        if (i, w) not in memo:
            a, b = wsum(i, w // 2), wsum(i - w // 2, w // 2)
            memo[(i, w)] = a if b is None else a + b
        return memo[(i, w)]

    gc = dq // len(POOL_WINDOWS)
    lane_grp = lax.broadcasted_iota(I32, (n_seq, dq), 1) // gc
    for l in range(n_step):
        mean = jnp.zeros((n_seq, dq), F32)
        for g, w in enumerate(POOL_WINDOWS):
            count = min(PAST_LEN + l + 1, w)
            mean = jnp.where(lane_grp == g, wsum(POOL_PAST + l, w) * (1.0 / count), mean)
        dd = (mean - slab(d_in, l)).astype(BF16)
        cat_ref[l * n_seq:(l + 1) * n_seq, 3 * dq:4 * dq] = (
            _dot(dd, wpool_ref[...]) * pscale_ref[...]).astype(BF16)
    store_state(sd_out, ed, POOL_PAST)

    mix = _dot(cat_ref[...], wout_ref[...])
    h_out[...] = _ln(alpha * h + mix, lmg_ref[...], lmb_ref[...])


def _mixer_sample(x_tm, sa, sc, sd, lw, first_layer, alpha, n_seq, n_step, shared=None):
    rows, d = x_tm.shape
    d_mix = lw['w_out'].shape[0]
    dq = d_mix // 4
    assert n_seq % SUBLANES == 0 and n_step <= CHUNK and PAST_LEN % CHUNK == 0
    ins = [x_tm, sa, sc, sd, lw['ln_in_g'], lw['ln_in_b'], lw['w_in'], lw['conv_a_w'], lw['conv_a_b'],
           lw['ln_a_g'], lw['ln_a_b'], lw['ln_v_g'], lw['ln_v_b'], lw['ws_vec'], lw['bs_vec'],
           lw['conv_c_w'], lw['w_pool_bd'], lw['pool_scale'], lw['w_out'], lw['ln_mix_g'], lw['ln_mix_b']]
    in_specs = [_full_spec(a) for a in ins]
    h_shape, h_block = (rows, d), 0
    if shared is not None:
        assert shared.shape[0] % rows == 0 and shared.shape[1] == d
        h_shape, h_block = shared.shape, shared.shape[0] // rows - 1
        ins.append(shared)
        in_specs.append(pl.BlockSpec(memory_space=pl.ANY))
    out_shape = (
        jax.ShapeDtypeStruct(h_shape, F32),
        jax.ShapeDtypeStruct(((CONV_A_W - 1) * n_seq, dq), F32),
        jax.ShapeDtypeStruct(((CONV_C_W - 1) * n_seq, dq), F32),
        jax.ShapeDtypeStruct((POOL_PAST * n_seq, dq), F32),
        jax.ShapeDtypeStruct((rows, dq), F32),
    )
    return pl.pallas_call(
        functools.partial(_mixer_sample_kernel, first_layer, alpha, n_seq, n_step, d_mix, shared is not None),
        grid=(1,),
        in_specs=in_specs,
        out_specs=(pl.BlockSpec((rows, d), lambda i: (h_block, 0)),)
        + tuple(pl.BlockSpec(s.shape, lambda i: (0, 0)) for s in out_shape[1:]),
        out_shape=out_shape,
        input_output_aliases={} if shared is None else {len(ins) - 1: 0},
        scratch_shapes=[pltpu.VMEM((rows, d_mix), BF16)],
        compiler_params=_cparams(("arbitrary",)),
        name="mixer_sample",
    )(*ins)


def _ffn_dense_kernel(alpha, tf, x_ref, wg_hbm, wu_hbm, wd_hbm, g_ref, b_ref, o_ref, wg_ref, wu_ref, wd_ref, sem):
    @pl.when(pl.program_id(0) == 0)
    def _():
        copies = [pltpu.make_async_copy(src, dst, sem.at[n])
                  for n, (src, dst) in enumerate(((wg_hbm, wg_ref), (wu_hbm, wu_ref), (wd_hbm, wd_ref)))]
        for cp in copies:
            cp.start()
        for cp in copies:
            cp.wait()

    x = x_ref[...]
    xb = x.astype(BF16)
    ffn = None
    for c0 in range(0, wg_ref.shape[1], tf):
        mid = _silu(_dot(xb, wg_ref[:, c0:c0 + tf])) * _dot(xb, wu_ref[:, c0:c0 + tf])
        part = _dot(mid.astype(BF16), wd_ref[c0:c0 + tf, :])
        ffn = part if ffn is None else ffn + part
    o_ref[...] = _ln(alpha * x + ffn, g_ref[...], b_ref[...])


def _ffn_dense(x, wg, wu, wd, g, b, alpha, tm, tf):
    t, d = x.shape
    dff = wg.shape[1]
    assert t % tm == 0 and dff % tf == 0
    any_spec = pl.BlockSpec(memory_space=pl.ANY)
    return pl.pallas_call(
        functools.partial(_ffn_dense_kernel, alpha, tf),
        grid=(t // tm,),
        in_specs=[pl.BlockSpec((tm, d), lambda i: (i, 0)), any_spec, any_spec, any_spec,
                  pl.BlockSpec((1, d), lambda i: (0, 0)),
                  pl.BlockSpec((1, d), lambda i: (0, 0))],
        out_specs=pl.BlockSpec((tm, d), lambda i: (i, 0)),
        out_shape=jax.ShapeDtypeStruct((t, d), F32),
        scratch_shapes=[pltpu.VMEM(wg.shape, BF16), pltpu.VMEM(wu.shape, BF16), pltpu.VMEM(wd.shape, BF16),
                        pltpu.SemaphoreType.DMA((3,))],
        compiler_params=_cparams(("arbitrary",)),
        name="ffn_dense",
    )(x, wg, wu, wd, g, b)


def _route_init(tm, cin_ref, carry_ref, before_ref):
    carry_ref[...] = cin_ref[...].astype(F32)
    row = lax.broadcasted_iota(I32, (tm, tm), 0)
    col = lax.broadcasted_iota(I32, (tm, tm), 1)
    before_ref[...] = jnp.where(row < col, 1.0, 0.0).astype(BF16)


def _route_block(tm, h, wcat_ref, mi_ref, mf_ref, cnt_ref, carry_ref, before_ref):
    h_hi = h.astype(BF16)
    h_lo = (h - h_hi.astype(F32)).astype(BF16)
    p_hi = _dot(h_hi, wcat_ref[...])
    p_lo = _dot(h_lo, wcat_ref[...])
    logits = (p_hi[:, 0:LANES] + (p_lo[:, 0:LANES] + p_hi[:, LANES:2 * LANES])) + p_lo[:, LANES:2 * LANES]
    lg = logits.T[0:N_EXPERTS, :]
    ex = lax.broadcasted_iota(I32, (N_EXPERTS, tm), 0).astype(F32)
    m1 = jnp.max(lg, axis=0, keepdims=True)
    i1 = jnp.min(jnp.where(lg == m1, ex, float(N_EXPERTS)), axis=0, keepdims=True)
    sel1 = ex == i1
    rest = jnp.where(sel1, -jnp.inf, lg)
    m2 = jnp.max(rest, axis=0, keepdims=True)
    i2 = jnp.min(jnp.where(rest == m2, ex, float(N_EXPERTS)), axis=0, keepdims=True)
    sel2 = ex == i2
    e2 = jnp.exp(m2 - m1)
    den = 1.0 + e2
    g1 = 1.0 / den
    g2 = e2 / den

    sel = jnp.where(sel1 | sel2, 1.0, 0.0)
    sel_pad = jnp.concatenate([sel, jnp.zeros_like(sel)], axis=0).astype(BF16)
    base = _dot(sel_pad, before_ref[...])[0:N_EXPERTS, :] + carry_ref[:, 0:1]
    r1 = jnp.sum(jnp.where(sel1, base, 0.0), axis=0, keepdims=True)
    r2 = jnp.sum(jnp.where(sel2, base, 0.0), axis=0, keepdims=True)
    carry_ref[...] = carry_ref[...] + jnp.sum(sel, axis=1, keepdims=True)

    meta = jnp.where(ex == 0, i1, jnp.where(ex == 1, i2, jnp.where(ex == 2, r1, r2)))
    mi_ref[...] = meta.astype(I32)
    gt = jnp.where(ex == 0, g1, jnp.where(ex == 1, g2, 0.0))
    gt = jnp.concatenate([gt, jnp.zeros((LANES - N_EXPERTS, tm), F32)], axis=0)
    mf_ref[...] = gt.T[:, 0:SUBLANES]
    cnt_ref[...] = carry_ref[...].astype(I32)


def _router_kernel(tm, h_ref, wcat_ref, cin_ref, mi_ref, mf_ref, cnt_ref, carry_ref, before_ref):
    @pl.when(pl.program_id(0) == 0)
    def _():
        _route_init(tm, cin_ref, carry_ref, before_ref)

    _route_block(tm, h_ref[...], wcat_ref, mi_ref, mf_ref, cnt_ref, carry_ref, before_ref)


def _router(h, row0, t, wr_cat, counts_in):
    d = h.shape[1]
    tm = min(TM_ROUTE, t)
    assert t % tm == 0 and row0 % tm == 0 and N_EXPERTS == SUBLANES
    blk0 = row0 // tm
    return pl.pallas_call(
        functools.partial(_router_kernel, tm),
        grid=(t // tm,),
        in_specs=[pl.BlockSpec((tm, d), lambda i: (blk0 + i, 0)), _full_spec(wr_cat), _full_spec(counts_in)],
        out_specs=(pl.BlockSpec((SUBLANES, tm), lambda i: (0, i)),
                   pl.BlockSpec((tm, SUBLANES), lambda i: (i, 0)),
                   pl.BlockSpec((SUBLANES, LANES), lambda i: (0, 0))),
        out_shape=(jax.ShapeDtypeStruct((SUBLANES, t), I32),
                   jax.ShapeDtypeStruct((t, SUBLANES), F32),
                   jax.ShapeDtypeStruct((SUBLANES, LANES), I32)),
        scratch_shapes=[pltpu.VMEM((SUBLANES, LANES), F32), pltpu.VMEM((tm, tm), BF16)],
        compiler_params=_cparams(("arbitrary",)),
        name="router",
    )(h, wr_cat, counts_in)


def _row_copy(src, src_row, dst, dst_row, sem):
    return pltpu.make_async_copy(src.at[pl.ds(src_row, 1), :], dst.at[pl.ds(dst_row, 1), :], sem)


def _rows_wait(src, dst, n_rows, sem):
    pltpu.make_async_copy(src.at[pl.ds(0, n_rows), :], dst.at[pl.ds(0, n_rows), :], sem).wait()


def _dispatch_kernel(tm, tm_expert, n_blocks, blk_ranges, zi_ref, dest_ref, *rest):
    n_src = len(blk_ranges)
    h_refs = rest[:n_src]
    xs_ref, xbuf, zblk, lsems, rsems, zsem = rest[n_src:]
    i = pl.program_id(0)
    n = pl.num_programs(0)

    def block_load(g, start):
        slot = lax.rem(g, N_STAGE)
        for h_ref, (b0, b1) in zip(h_refs, blk_ranges):
            @pl.when((g >= b0) & (g < b1))
            def _(h_ref=h_ref, b0=b0):
                cp = pltpu.make_async_copy(h_ref.at[pl.ds((g - b0) * tm, tm), :], xbuf.at[slot],
                                           lsems.at[slot])
                if start:
                    cp.start()
                else:
                    cp.wait()

    def rows_wait(g):
        slot = lax.rem(g, N_STAGE)
        for _k in range(TOP_K):
            _rows_wait(xbuf.at[slot], xs_ref, tm, rsems.at[slot])

    @pl.when(i == 0)
    def _():
        block_load(i, True)

    @pl.when(i >= N_STAGE - 1)
    def _():
        rows_wait(i - (N_STAGE - 1))

    @pl.when(i + 1 < n)
    def _():
        block_load(i + 1, True)

    block_load(i, False)
    cur = lax.rem(i, N_STAGE)

    def issue(j, carry):
        r0 = pl.multiple_of(j * SUBLANES, SUBLANES)
        for u in range(SUBLANES):
            for k in range(TOP_K):
                dst = dest_ref[0, 0, j * SUBLANES + (k * tm + u)]
                _row_copy(xbuf.at[cur], r0 + u, xs_ref, dst, rsems.at[cur]).start(priority=k)
        return carry

    lax.fori_loop(0, tm // SUBLANES, issue, 0)

    @pl.when(i == n - 1)
    def _():
        for back in range(N_STAGE - 2, -1, -1):
            @pl.when(i >= back)
            def _(back=back):
                rows_wait(i - back)

    @pl.when(i == 0)
    def _():
        zblk[...] = jnp.zeros_like(zblk)
        for e in range(N_EXPERTS):
            lo = zi_ref[e] + zi_ref[N_EXPERTS + e]
            hi = zi_ref[e] + zi_ref[2 * N_EXPERTS + e]

            def zissue(r, carry):
                _row_copy(zblk, 0, xs_ref, r, zsem).start()
                return carry

            def zdrain(r, carry):
                _row_copy(zblk, 0, xs_ref, r, zsem).wait()
                return carry

            lax.fori_loop(lo, hi, zissue, 0)
            lax.fori_loop(lo, hi, zdrain, 0)

        def bcopy(b):
            return pltpu.make_async_copy(zblk, xs_ref.at[pl.ds(b * tm_expert, tm_expert), :], zsem)

        def bissue(b, carry):
            bcopy(b).start()
            return carry

        def bdrain(b, carry):
            bcopy(b).wait()
            return carry

        lax.fori_loop(zi_ref[3 * N_EXPERTS], n_blocks, bissue, 0)
        lax.fori_loop(zi_ref[3 * N_EXPERTS], n_blocks, bdrain, 0)


def _dispatch(hs_list, dest_blk, zinfo, n_blocks, tm_expert):
    d = hs_list[0].shape[1]
    nb = dest_blk.shape[0]
    tm = dest_blk.shape[2] // TOP_K
    blk_ranges, b0 = [], 0
    for h in hs_list:
        assert h.shape[0] % tm == 0 and h.shape[0] >= tm
        blk_ranges.append((b0, b0 + h.shape[0] // tm))
        b0 = blk_ranges[-1][1]
    assert b0 == nb
    any_spec = pl.BlockSpec(memory_space=pl.ANY)
    return pl.pallas_call(
        functools.partial(_dispatch_kernel, tm, tm_expert, n_blocks, tuple(blk_ranges)),
        grid_spec=pltpu.PrefetchScalarGridSpec(
            num_scalar_prefetch=1,
            grid=(nb,),
            in_specs=[pl.BlockSpec((1, 1, TOP_K * tm), lambda i, zi: (i, 0, 0), memory_space=pltpu.SMEM)]
            + [any_spec] * len(hs_list),
            out_specs=any_spec,
            scratch_shapes=[pltpu.VMEM((N_STAGE, tm, d), F32), pltpu.VMEM((tm_expert, d), F32),
                            pltpu.SemaphoreType.DMA((N_STAGE,)), pltpu.SemaphoreType.DMA((N_STAGE,)),
                            pltpu.SemaphoreType.DMA(())],
        ),
        out_shape=jax.ShapeDtypeStruct((n_blocks * tm_expert, d), F32),
        compiler_params=_cparams(("arbitrary",)),
        name="moe_dispatch",
    )(zinfo, dest_blk, *hs_list)


def _ffn_moe_kernel(tf, be_ref, nu_ref, tok_ref, tok_next_ref, h_hbm, wg_hbm, wu_hbm, wd_hbm, y_ref,
                    wg_ref, wu_ref, wd_ref, stage_in, stage_out, sems, xbuf, gsems):
    b = pl.program_id(0)
    nb = pl.num_programs(0)
    d, dff = wg_ref.shape
    n_slots = stage_in.shape[0]
    rows_in = stage_in.shape[1]
    rows_out = stage_out.shape[1]
    n_in, n_out = 2 * (d // rows_in), dff // rows_out
    e = be_ref[b]
    fresh = (b == 0) | (e != be_ref[jnp.maximum(b - 1, 0)])
    e_next = be_ref[jnp.minimum(b + 1, nb - 1)]

    def chunk_in(ex, n):
        src, dst = (wg_hbm, wg_ref) if n < n_in // 2 else (wu_hbm, wu_ref)
        rows = pl.ds((n % (n_in // 2)) * rows_in, rows_in)
        stage = stage_in.at[n % n_slots]
        return pltpu.make_async_copy(src.at[ex, rows, :], stage, sems.at[0, n % n_slots]), stage, dst.at[rows, :]

    def chunk_out(ex, n):
        rows = pl.ds(n * rows_out, rows_out)
        stage = stage_out.at[n % n_slots]
        return pltpu.make_async_copy(wd_hbm.at[ex, rows, :], stage, sems.at[1, n % n_slots]), stage, wd_ref.at[rows, :]

    def request_first(ex):
        for n in range(n_slots):
            chunk_in(ex, n)[0].start()
            chunk_out(ex, n)[0].start()

    @pl.when((b < nu_ref[0]) & fresh)
    def _():
        @pl.when(b == 0)
        def _():
            request_first(e)

        for n in range(n_in):
            for chunk, count in ((chunk_in, n_in), (chunk_out, n_out)):
                if n < count:
                    copy, staged, dst = chunk(e, n)
                    copy.wait()
                    dst[...] = staged[...].astype(BF16)
                    if n + n_slots < count:
                        chunk(e, n + n_slots)[0].start()

    @pl.when((b + 1 < nu_ref[0]) & (e_next != e))
    def _():
        request_first(e_next)

    tm = xbuf.shape[1]
    cur = lax.rem(b, 2)

    def gather(tok_ref, slot):
        for r in range(tm):
            _row_copy(h_hbm, tok_ref[0, 0, r], xbuf.at[slot], r, gsems.at[slot]).start(priority=r % 2)

    @pl.when(b == 0)
    def _():
        gather(tok_ref, 0)

    def compute(gather_next):
        _rows_wait(h_hbm, xbuf.at[cur], tm, gsems.at[cur])
        if gather_next:
            gather(tok_next_ref, 1 - cur)
        xb = xbuf[cur].astype(BF16)
        ffn = None
        for c0 in range(0, dff, tf):
            mid = _silu(_dot(xb, wg_ref[:, c0:c0 + tf])) * _dot(xb, wu_ref[:, c0:c0 + tf])
            part = _dot(mid.astype(BF16), wd_ref[c0:c0 + tf, :])
            ffn = part if ffn is None else ffn + part
        y_ref[...] = ffn

    @pl.when(b + 1 < nu_ref[0])
    def _():
        compute(True)

    @pl.when(b + 1 == nu_ref[0])
    def _():
        compute(False)

    @pl.when(b >= nu_ref[0])
    def _():
        y_ref[...] = jnp.zeros_like(y_ref)


def _ffn_moe(h, slot_tok, wg, wu, wd, block_e, n_used, tm, tf):
    d = h.shape[1]
    n_blocks = slot_tok.shape[0]
    cap = n_blocks * tm
    dff = wg.shape[2]
    rows_in, rows_out = 128, 512
    assert slot_tok.shape[2] == tm and dff % tf == 0 and d % rows_in == 0 and dff % rows_out == 0
    assert min(2 * (d // rows_in), dff // rows_out) >= N_WSTAGE
    any_spec = pl.BlockSpec(memory_space=pl.ANY)
    tok_spec = lambda imap: pl.BlockSpec((1, 1, tm), imap, memory_space=pltpu.SMEM)
    return pl.pallas_call(
        functools.partial(_ffn_moe_kernel, tf),
        grid_spec=pltpu.PrefetchScalarGridSpec(
            num_scalar_prefetch=2,
            grid=(n_blocks,),
            in_specs=[tok_spec(lambda b, be, nu: (b, 0, 0)),
                      tok_spec(lambda b, be, nu: (jnp.minimum(b + 1, n_blocks - 1), 0, 0)),
                      any_spec, any_spec, any_spec, any_spec],
            out_specs=pl.BlockSpec((tm, d), lambda b, be, nu: (b, 0)),
            scratch_shapes=[pltpu.VMEM((d, dff), BF16), pltpu.VMEM((d, dff), BF16), pltpu.VMEM((dff, d), BF16),
                            pltpu.VMEM((N_WSTAGE, rows_in, dff), F32), pltpu.VMEM((N_WSTAGE, rows_out, d), F32),
                            pltpu.SemaphoreType.DMA((2, N_WSTAGE)),
                            pltpu.VMEM((2, tm, d), F32), pltpu.SemaphoreType.DMA((2,))],
        ),
        out_shape=jax.ShapeDtypeStruct((cap, d), F32),
        compiler_params=_cparams(("arbitrary",)),
        name="ffn_moe",
    )(block_e, n_used, slot_tok, slot_tok, h, wg, wu, wd)


def _combine_kernel(alpha, tm, dcur_ref, dnext_ref, h_ref, gate_ref, g_ref, b_ref, ys_ref, o_ref, ybuf, sems):
    i = pl.program_id(0)
    n = pl.num_programs(0)

    def issue(dref, slot):
        def body(j, carry):
            r0 = pl.multiple_of(j * SUBLANES, SUBLANES)
            for u in range(SUBLANES):
                for k in range(TOP_K):
                    src = dref[0, 0, j * SUBLANES + (k * tm + u)]
                    _row_copy(ys_ref, src, ybuf.at[slot, k], r0 + u, sems.at[slot]).start(priority=k)
            return carry

        lax.fori_loop(0, tm // SUBLANES, body, 0)

    @pl.when(i == 0)
    def _():
        issue(dcur_ref, 0)

    @pl.when(i + 1 < n)
    def _():
        issue(dnext_ref, (i + 1) % 2)

    slot = i % 2
    for k in range(TOP_K):
        _rows_wait(ys_ref, ybuf.at[slot, k], tm, sems.at[slot])

    gates = gate_ref[...]
    ffn = gates[:, 0:1] * ybuf[slot, 0] + gates[:, 1:2] * ybuf[slot, 1]
    o_ref[...] = _ln(alpha * h_ref[...] + ffn, g_ref[...], b_ref[...])


def _combine(h, row0, dest_blk, gates, ys, g, b, alpha):
    d = h.shape[1]
    nb = dest_blk.shape[0]
    tm = dest_blk.shape[2] // TOP_K
    t = nb * tm
    assert row0 % tm == 0
    blk0 = row0 // tm
    dest_spec = lambda imap: pl.BlockSpec((1, 1, TOP_K * tm), imap, memory_space=pltpu.SMEM)
    return pl.pallas_call(
        functools.partial(_combine_kernel, alpha, tm),
        grid=(nb,),
        in_specs=[dest_spec(lambda i: (i, 0, 0)),
                  dest_spec(lambda i: (jnp.minimum(i + 1, nb - 1), 0, 0)),
                  pl.BlockSpec((tm, d), lambda i: (blk0 + i, 0)),
                  pl.BlockSpec((tm, SUBLANES), lambda i: (i, 0)),
                  pl.BlockSpec((1, d), lambda i: (0, 0)),
                  pl.BlockSpec((1, d), lambda i: (0, 0)),
                  pl.BlockSpec(memory_space=pl.ANY)],
        out_specs=pl.BlockSpec((tm, d), lambda i: (i, 0)),
        out_shape=jax.ShapeDtypeStruct((t, d), F32),
        scratch_shapes=[pltpu.VMEM((2, TOP_K, tm, d), F32), pltpu.SemaphoreType.DMA((2,))],
        compiler_params=_cparams(("arbitrary",)),
        name="moe_combine",
    )(dest_blk, dest_blk, h, gates, g, b, ys)


def _ffn_routed(h, parts, first_routed, mw, g, b, alpha, tm_expert):
    starts = [sum(parts[:n]) for n in range(len(parts))]
    n_tok = sum(parts)
    routed = [first_routed[:2]]
    counts_in = first_routed[2]
    for row0, t in zip(starts[1:], parts[1:]):
        meta_i, gates, counts_in = _router(h, row0, t, mw['w_router_cat'], counts_in)
        routed.append((meta_i, gates))
    counts = counts_in[:N_EXPERTS, 0]
    padded = (counts + tm_expert - 1) // tm_expert * tm_expert
    pad_end = jnp.cumsum(padded)
    pad_start = (pad_end - padded).astype(I32)
    n_blocks = -(-(n_tok * TOP_K) // tm_expert) + N_EXPERTS
    n_used = (pad_end[-1] // tm_expert).astype(I32)
    blk_start = jnp.minimum(jnp.arange(n_blocks, dtype=I32), n_used - 1) * tm_expert
    block_e = jnp.minimum(jnp.sum(blk_start[:, None] >= pad_end[None, :], axis=1), N_EXPERTS - 1).astype(I32)

    dests = []
    for meta_i, _ in routed:
        dest = meta_i[TOP_K:2 * TOP_K, :]
        for e in range(N_EXPERTS):
            dest = dest + jnp.where(meta_i[0:TOP_K, :] == e, pad_start[e], 0)
        dests.append(dest)
    dest_all = jnp.concatenate(dests, axis=1)
    tok_ids = jnp.broadcast_to(jnp.arange(n_tok, dtype=I32), (TOP_K, n_tok))
    slot_tok = jnp.zeros((n_blocks * tm_expert,), I32).at[dest_all.reshape(-1)].set(
        tok_ids.reshape(-1), unique_indices=True).reshape(n_blocks, 1, tm_expert)
    ys = _ffn_moe(h, slot_tok, mw['w_e_gate'], mw['w_e_up'], mw['w_e_down'], block_e, n_used.reshape(1),
                  tm_expert, TF_EXPERT)
    outs = []
    for row0, t, dest, (_, gates) in zip(starts, parts, dests, routed):
        nb = t // TM_MOVE
        dest_blk = dest.reshape(TOP_K, nb, TM_MOVE).transpose(1, 0, 2).reshape(nb, 1, TOP_K * TM_MOVE)
        outs.append(_combine(h, row0, dest_blk, gates, ys, g, b, alpha))
    return outs


def _prep_layer(i, p, n_step):
    dq = p['conv_a_w'].shape[2]
    hd = dq // N_HEADS_B
    tril = jnp.tril(jnp.ones((CHUNK, CHUNK), dtype=bool))
    w_s = p['w_s'][i]
    b_s = p['b_s'][i]
    wp = p['w_pool'][i]
    ng, gc = wp.shape[0], wp.shape[1]
    w_pool_bd = (jnp.eye(ng, dtype=F32)[:, None, :, None] * wp[:, :, None, :]).reshape(ng * gc, ng * gc)
    row = lambda a: a.reshape(1, -1)
    return dict(
        ln_in_g=row(p['ln_in_g']), ln_in_b=row(p['ln_in_b']),
        w_in=p['w_in'][i].astype(BF16),
        conv_a_w=p['conv_a_w'][i], conv_a_b=row(p['conv_a_b'][i]),
        ln_a_g=row(p['ln_a_g'][i]), ln_a_b=row(p['ln_a_b'][i]),
        ln_v_g=row(p['ln_v_g'][i]), ln_v_b=row(p['ln_v_b'][i]),
        ws_tril=jnp.where(tril[None], w_s, 0).astype(BF16),
        bs_mat=jnp.repeat(b_s.T, hd, axis=1),
        ws_vec=jnp.repeat(jnp.where(tril[None], w_s, 0)[:, :n_step, :n_step].transpose(1, 2, 0)
                          .reshape(n_step * n_step, N_HEADS_B), hd, axis=1),
        bs_vec=jnp.repeat(b_s[:, :n_step].T, hd, axis=1),
        conv_c_w=p['conv_c_w'][i],
        w_pool_bd=w_pool_bd.astype(BF16), pool_scale=row(p['pool_scale'][i]),
        w_out=p['w_out'][i].astype(BF16),
        ln_mix_g=row(p['ln_mix_g'][i]), ln_mix_b=row(p['ln_mix_b'][i]),
    )


def kernel(x_prompt, x_sample, state_conv_a, state_conv_c, state_pool_d, ln_in_g, ln_in_b, w_in, conv_a_w,
           conv_a_b, ln_a_g, ln_a_b, ln_v_g, ln_v_b, w_s, b_s, conv_c_w, w_pool, pool_scale, w_out,
           ln_mix_g, ln_mix_b, w_ff_gate, w_ff_up, w_ff_down, w_router, w_e_gate, w_e_up, w_e_down,
           ln_ffn_g, ln_ffn_b):
    p = dict(ln_in_g=ln_in_g, ln_in_b=ln_in_b, w_in=w_in, conv_a_w=conv_a_w, conv_a_b=conv_a_b,
             ln_a_g=ln_a_g, ln_a_b=ln_a_b, ln_v_g=ln_v_g, ln_v_b=ln_v_b, w_s=w_s, b_s=b_s,
             conv_c_w=conv_c_w, w_pool=w_pool, pool_scale=pool_scale, w_out=w_out,
             ln_mix_g=ln_mix_g, ln_mix_b=ln_mix_b)
    depth = w_in.shape[0]
    bsz, seq, d = x_prompt.shape
    n_seq, n_step, _ = x_sample.shape
    dq = conv_a_w.shape[2]
    alpha = float((2.0 * depth) ** 0.25)

    hp = x_prompt
    hs = x_sample.transpose(1, 0, 2).reshape(n_step * n_seq, d)
    outs = {k: [] for k in ('pa', 'pc', 'pd', 'pv', 'sa', 'sc', 'sd', 'sv')}
    for i in range(depth):
        lw = _prep_layer(i, p, n_step)
        j = i // 2
        routed = i % 2 == 1
        if routed:
            wr = jnp.pad(w_router[j], ((0, 0), (0, LANES - N_EXPERTS)))
            wr_hi = wr.astype(BF16)
            wr_lo = (wr - wr_hi.astype(F32)).astype(BF16)
            wr_cat = jnp.concatenate([wr_hi, wr_lo], axis=1)
        hp, pa, pc, pd, pv, *route_p = _mixer_prompt(hp.reshape(bsz, seq, d), lw, i == 0, alpha,
                                                     wr_cat if routed else None,
                                                     hs.shape[0] if routed else 0)
        tm_state = lambda s: s.transpose(1, 0, 2).reshape(-1, dq)
        hs, sa, sc, sd, sv = _mixer_sample(hs, tm_state(state_conv_a[i]), tm_state(state_conv_c[i]),
                                           tm_state(state_pool_d[i]), lw, i == 0, alpha, n_seq, n_step,
                                           hp if routed else None)
        lg, lb = ln_ffn_g[i].reshape(1, -1), ln_ffn_b[i].reshape(1, -1)
        if not routed:
            wg, wu, wd = w_ff_gate[j].astype(BF16), w_ff_up[j].astype(BF16), w_ff_down[j].astype(BF16)
            hp = _ffn_dense(hp, wg, wu, wd, lg, lb, alpha, TM_DENSE, TF_DENSE)
            hs = _ffn_dense(hs, wg, wu, wd, lg, lb, alpha, hs.shape[0], TF_DENSE)
        else:
            mw = dict(w_router_cat=wr_cat, w_e_gate=w_e_gate[j], w_e_up=w_e_up[j], w_e_down=w_e_down[j])
            hp, hs = _ffn_routed(hs, [bsz * seq, n_seq * n_step], route_p, mw, lg, lb, alpha, TM_EXPERT)
        back = lambda a: a.reshape(-1, n_seq, dq).transpose(1, 0, 2)
        for k, val in zip(('pa', 'pc', 'pd', 'pv', 'sa', 'sc', 'sd', 'sv'),
                          (pa, pc, pd, pv, back(sa), back(sc), back(sd), back(sv))):
            outs[k].append(val)
    y_sample = hs.reshape(n_step, n_seq, d).transpose(1, 0, 2)
    st = lambda k: jnp.stack(outs[k])
    return (hp.reshape(bsz, seq, d), y_sample, st('pa'), st('pc'), st('pd'), st('pv'), st('sa'), st('sc'), st('sd'), st('sv'))
```

```python
import functools

import numpy as np
import jax
import jax.numpy as jnp
from jax import lax
from jax.experimental import pallas as pl
from jax.experimental.pallas import tpu as pltpu

F32 = jnp.float32
BF16 = jnp.bfloat16
I32 = jnp.int32

PAST_LEN = 16384
CHUNK = 128
N_HEADS_B = 4
POOL_WINDOWS = (2, 4, 8, 16)
POOL_PAST = 15
CONV_A_W = 31
CONV_C_W = 3
N_EXPERTS = 8
TOP_K = 2
LN_EPS = 1e-5
SQRT_HALF = float(np.sqrt(0.5).astype(np.float32))

LANES = 128
SUBLANES = 8
VMEM_LIMIT_BYTES = 56 * 1024 * 1024

HIST_A = 32
HIST_C = 8
HIST_D = 24

TL_PROMPT = 512
TM_DENSE = 512
TF_DENSE = 1792
TF_EXPERT = 1792
TM_ROUTE = 512
TM_MOVE = 256
TM_EXPERT = 512
N_STAGE = 3
N_WSTAGE = 5


def _ln(x, g, b):
    mu = jnp.mean(x, axis=-1, keepdims=True)
    xc = x - mu
    var = jnp.mean(xc * xc, axis=-1, keepdims=True)
    return xc * lax.rsqrt(var + LN_EPS) * g + b


def _gelu(x):
    return 0.5 * x * (1.0 + lax.erf(x * SQRT_HALF))


def _silu(x):
    return x * jax.nn.sigmoid(x)


def _dot(a, b):
    return jnp.dot(a, b, preferred_element_type=F32)


def _cparams(sem):
    return pltpu.CompilerParams(dimension_semantics=sem, vmem_limit_bytes=VMEM_LIMIT_BYTES)


def _mixer_prompt_kernel(first_layer, route, alpha, tl, d_mix,
                         x_ref, lng_ref, lnb_ref, win_ref, caw_ref, cab_ref, lag_ref, lab_ref,
                         lvg_ref, lvb_ref, ws_ref, bsm_ref, ccw_ref, wpool_ref, pscale_ref,
                         wout_ref, lmg_ref, lmb_ref, *refs):
    if route:
        wcat_ref, cin_ref = refs[:2]
        refs = refs[2:]
    h_out, sa_out, sc_out, sd_out, v_out = refs[:5]
    refs = refs[5:]
    if route:
        mi_ref, mf_ref, cnt_ref = refs[:3]
        refs = refs[3:]
    ext_a, ext_c, ext_d, cat_ref, z_ref, hb_ref, mix_ref = refs[:7]
    dq = d_mix // 4
    j = pl.program_id(1)
    nj = pl.num_programs(1)
    if route:
        carry_ref, before_ref = refs[7:]

        @pl.when((pl.program_id(0) == 0) & (j == 0))
        def _():
            _route_init(tl, cin_ref, carry_ref, before_ref)

    @pl.when(j == 0)
    def _():
        ext_a[0, 0:HIST_A, :] = jnp.zeros((HIST_A, dq), F32)
        ext_c[0:HIST_C, :] = jnp.zeros((HIST_C, dq), F32)
        ext_d[0:HIST_D, :] = jnp.zeros((HIST_D, dq), F32)

    d = x_ref.shape[2]
    rb = CHUNK

    def resid(r0, n):
        return h_out[0, r0:r0 + n, :] if first_layer else x_ref[0, r0:r0 + n, :]

    for r0 in range(0, tl, rb):
        x = x_ref[0, r0:r0 + rb, :]
        if first_layer:
            x = _ln(x, lng_ref[...], lnb_ref[...])
            h_out[0, r0:r0 + rb, :] = x
        hb_ref[r0:r0 + rb, :] = x.astype(BF16)

    z_ref[...] = _dot(hb_ref[...], win_ref[...])

    def zcol(r0, i, n=rb):
        return z_ref[r0:r0 + n, i * dq:(i + 1) * dq]

    chunk_rows = range(0, tl, CHUNK)

    for r0 in chunk_rows:
        ext_a[0, HIST_A + r0:HIST_A + r0 + rb, :] = zcol(r0, 0) * jax.nn.sigmoid(zcol(r0, 1))
    n_ext = HIST_A + tl
    for r0 in range(0, n_ext - SUBLANES, rb):
        n = min(rb, n_ext - SUBLANES - r0)
        blk = ext_a[0, r0:r0 + n + SUBLANES, :]
        for s in range(1, SUBLANES):
            ext_a[s, r0:r0 + n, :] = pltpu.roll(blk, n + SUBLANES - s, axis=0)[0:n, :]
    off_a = HIST_A - (CONV_A_W - 1)
    for q0 in range(0, tl, 64):
        acc = jnp.zeros((64, dq), F32)
        for k in range(CONV_A_W):
            s = (off_a + k) % SUBLANES
            row = off_a + k - s + q0
            acc = acc + ext_a[s, row:row + 64, :] * caw_ref[k:k + 1, :]
        y = _silu(_ln(acc + cab_ref[...], lag_ref[...], lab_ref[...]))
        cat_ref[q0:q0 + 64, 0:dq] = y.astype(BF16)

    hd = dq // N_HEADS_B
    lane_head = lax.broadcasted_iota(I32, (CHUNK, dq), 1) // hd
    for r0 in chunk_rows:
        v = _ln(_gelu(zcol(r0, 3)), lvg_ref[...], lvb_ref[...])
        sm = bsm_ref[...]
        for hh in range(N_HEADS_B):
            vm = jnp.where(lane_head == hh, v, 0.0).astype(BF16)
            sm = sm + _dot(ws_ref[hh], vm)
        cat_ref[r0:r0 + CHUNK, dq:2 * dq] = (_gelu(zcol(r0, 2)) * sm).astype(BF16)
        if r0 == tl - CHUNK:
            @pl.when(j == nj - 1)
            def _(v=v):
                v_out[0] = v

    off_c = HIST_C - (CONV_C_W - 1)
    for r0 in chunk_rows:
        ext_c[HIST_C + r0:HIST_C + r0 + rb, :] = zcol(r0, 6) * zcol(r0, 4)
    for r0 in chunk_rows:
        conv_c = jnp.zeros((CHUNK, dq), F32)
        for k in range(CONV_C_W):
            conv_c = conv_c + ext_c[off_c + r0 + k:off_c + r0 + k + CHUNK, :] * ccw_ref[k:k + 1, :]
        cat_ref[r0:r0 + CHUNK, 2 * dq:3 * dq] = (zcol(r0, 5) * conv_c).astype(BF16)

    gc = dq // len(POOL_WINDOWS)
    lane_grp = lax.broadcasted_iota(I32, (CHUNK, dq), 1) // gc
    lead = 2 * SUBLANES
    for r0 in chunk_rows:
        ext_d[HIST_D + r0:HIST_D + r0 + rb, :] = zcol(r0, 7)
    for r0 in chunk_rows:
        e = ext_d[HIST_D + r0 - lead:HIST_D + r0 + CHUNK, :]
        s2 = e + pltpu.roll(e, 1, axis=0)
        s4 = s2 + pltpu.roll(s2, 2, axis=0)
        s8 = s4 + pltpu.roll(s4, 4, axis=0)
        s16 = s8 + pltpu.roll(s8, 8, axis=0)
        sums = (s2, s4, s8, s16)
        pos = (j * tl + r0 + lax.broadcasted_iota(I32, (CHUNK, 1), 0)).astype(F32)
        mean = jnp.zeros((CHUNK, dq), F32)
        for g, w in enumerate(POOL_WINDOWS):
            inv = 1.0 / jnp.minimum(pos + 1.0, float(w))
            mean = jnp.where(lane_grp == g, sums[g][lead:lead + CHUNK, :] * inv, mean)
        dd = (mean - e[lead:lead + CHUNK, :]).astype(BF16)
        cat_ref[r0:r0 + CHUNK, 3 * dq:4 * dq] = (_dot(dd, wpool_ref[...]) * pscale_ref[...]).astype(BF16)

    mix_ref[...] = _dot(cat_ref[...], wout_ref[...])
    for r0 in range(0, tl, 64):
        h_out[0, r0:r0 + 64, :] = _ln(alpha * resid(r0, 64) + mix_ref[r0:r0 + 64, :], lmg_ref[...], lmb_ref[...])
    if route:
        _route_block(tl, h_out[0], wcat_ref, mi_ref, mf_ref, cnt_ref, carry_ref, before_ref)

    @pl.when(j == nj - 1)
    def _():
        sa_out[0] = ext_a[0, HIST_A + tl - (CONV_A_W - 1):HIST_A + tl, :]
        sc_out[0] = ext_c[HIST_C + tl - (CONV_C_W - 1):HIST_C + tl, :]
        sd_out[0] = ext_d[HIST_D + tl - POOL_PAST:HIST_D + tl, :]

    ext_a[0, 0:HIST_A, :] = ext_a[0, tl:tl + HIST_A, :]
    ext_c[0:HIST_C, :] = ext_c[tl:tl + HIST_C, :]
    ext_d[0:HIST_D, :] = ext_d[tl:tl + HIST_D, :]


def _full_spec(arr):
    nd = arr.ndim
    return pl.BlockSpec(arr.shape, lambda *_: (0,) * nd)


def _mixer_prompt(x, lw, first_layer, alpha, route_w=None):
    bsz, seq, d = x.shape
    d_mix = lw['w_out'].shape[0]
    dq = d_mix // 4
    tl = TL_PROMPT
    nj = seq // tl
    route = route_w is not None
    assert seq % tl == 0 and tl % CHUNK == 0 and seq >= CHUNK
    params = [lw['ln_in_g'], lw['ln_in_b'], lw['w_in'], lw['conv_a_w'], lw['conv_a_b'], lw['ln_a_g'],
              lw['ln_a_b'], lw['ln_v_g'], lw['ln_v_b'], lw['ws_tril'], lw['bs_mat'], lw['conv_c_w'],
              lw['w_pool_bd'], lw['pool_scale'], lw['w_out'], lw['ln_mix_g'], lw['ln_mix_b']]
    out_shape = [
        jax.ShapeDtypeStruct((bsz, seq, d), F32),
        jax.ShapeDtypeStruct((bsz, CONV_A_W - 1, dq), F32),
        jax.ShapeDtypeStruct((bsz, CONV_C_W - 1, dq), F32),
        jax.ShapeDtypeStruct((bsz, POOL_PAST, dq), F32),
        jax.ShapeDtypeStruct((bsz, CHUNK, dq), F32),
    ]
    state_spec = lambda rows: pl.BlockSpec((1, rows, dq), lambda b, j: (b, 0, 0))
    out_specs = [pl.BlockSpec((1, tl, d), lambda b, j: (b, j, 0)),
                 state_spec(CONV_A_W - 1), state_spec(CONV_C_W - 1), state_spec(POOL_PAST), state_spec(CHUNK)]
    scratch = [pltpu.VMEM((SUBLANES, HIST_A + tl, dq), F32), pltpu.VMEM((HIST_C + tl, dq), F32),
               pltpu.VMEM((HIST_D + tl, dq), F32), pltpu.VMEM((tl, d_mix), BF16),
               pltpu.VMEM((tl, lw['w_in'].shape[1]), F32), pltpu.VMEM((tl, d), BF16),
               pltpu.VMEM((tl, d), F32)]
    if route:
        assert N_EXPERTS == SUBLANES
        params += [route_w, jnp.zeros((SUBLANES, LANES), I32)]
        out_shape += [jax.ShapeDtypeStruct((SUBLANES, bsz * seq), I32),
                      jax.ShapeDtypeStruct((bsz * seq, SUBLANES), F32),
                      jax.ShapeDtypeStruct((SUBLANES, LANES), I32)]
        out_specs += [pl.BlockSpec((SUBLANES, tl), lambda b, j: (0, b * nj + j)),
                      pl.BlockSpec((tl, SUBLANES), lambda b, j: (b * nj + j, 0)),
                      pl.BlockSpec((SUBLANES, LANES), lambda b, j: (0, 0))]
        scratch += [pltpu.VMEM((SUBLANES, LANES), F32), pltpu.VMEM((tl, tl), BF16)]
    return pl.pallas_call(
        functools.partial(_mixer_prompt_kernel, first_layer, route, alpha, tl, d_mix),
        grid=(bsz, nj),
        in_specs=[pl.BlockSpec((1, tl, d), lambda b, j: (b, j, 0))] + [_full_spec(p) for p in params],
        out_specs=tuple(out_specs),
        out_shape=tuple(out_shape),
        scratch_shapes=scratch,
        compiler_params=_cparams(("arbitrary", "arbitrary")),
        name="mixer_prompt",
    )(x, *params)


def _mixer_sample_kernel(first_layer, alpha, n_seq, n_step, d_mix,
                         x_ref, sa_ref, sc_ref, sd_ref, lng_ref, lnb_ref, win_ref, caw_ref, cab_ref,
                         lag_ref, lab_ref, lvg_ref, lvb_ref, wsv_ref, bsv_ref, ccw_ref, wpool_ref,
                         pscale_ref, wout_ref, lmg_ref, lmb_ref,
                         h_out, sa_out, sc_out, sd_out, v_out, cat_ref):
    dq = d_mix // 4
    x = x_ref[...]
    h = _ln(x, lng_ref[...], lnb_ref[...]) if first_layer else x
    hb = h.astype(BF16)

    def proj(i):
        return _dot(hb, win_ref[:, i * dq:(i + 1) * dq])

    def slab(val, i):
        return val[i * n_seq:(i + 1) * n_seq, :]

    def ext_slabs(state_ref, n_past, cur):
        return ([state_ref[i * n_seq:(i + 1) * n_seq, :] for i in range(n_past)]
                + [slab(cur, i) for i in range(n_step)])

    def store_state(out_ref, slabs, n_keep):
        for i, sl in enumerate(slabs[len(slabs) - n_keep:]):
            out_ref[i * n_seq:(i + 1) * n_seq, :] = sl

    a_glu = proj(0) * jax.nn.sigmoid(proj(1))
    ea = ext_slabs(sa_ref, CONV_A_W - 1, a_glu)
    for l in range(n_step):
        acc = jnp.zeros((n_seq, dq), F32)
        for k in range(CONV_A_W):
            acc = acc + ea[l + k] * caw_ref[k:k + 1, :]
        y = _silu(_ln(acc + cab_ref[...], lag_ref[...], lab_ref[...]))
        cat_ref[l * n_seq:(l + 1) * n_seq, 0:dq] = y.astype(BF16)
    store_state(sa_out, ea, CONV_A_W - 1)

    u = _gelu(proj(2))
    v = _ln(_gelu(proj(3)), lvg_ref[...], lvb_ref[...])
    v_out[...] = v
    for l in range(n_step):
        s = jnp.zeros((n_seq, dq), F32) + bsv_ref[l:l + 1, :]
        for m in range(l + 1):
            s = s + slab(v, m) * wsv_ref[l * n_step + m:l * n_step + m + 1, :]
        cat_ref[l * n_seq:(l + 1) * n_seq, dq:2 * dq] = (slab(u, l) * s).astype(BF16)

    c_x = proj(4)
    c_b = proj(5)
    c_c = proj(6)
    gx = c_c * c_x
    ec = ext_slabs(sc_ref, CONV_C_W - 1, gx)
    for l in range(n_step):
        acc = jnp.zeros((n_seq, dq), F32)
        for k in range(CONV_C_W):
            acc = acc + ec[l + k] * ccw_ref[k:k + 1, :]
        cat_ref[l * n_seq:(l + 1) * n_seq, 2 * dq:3 * dq] = (slab(c_b, l) * acc).astype(BF16)
    store_state(sc_out, ec, CONV_C_W - 1)

    d_in = proj(7)
    ed = ext_slabs(sd_ref, POOL_PAST, d_in)
    memo = {}

    def wsum(i, w):
        if i < 0:
            return None
        if w == 1:
            return ed[i]
        if (i, w) not in memo:
            a, b = wsum(i, w // 2), wsum(i - w // 2, w // 2)
            memo[(i, w)] = a if b is None else a + b
        return memo[(i, w)]

    gc = dq // len(POOL_WINDOWS)
    lane_grp = lax.broadcasted_iota(I32, (n_seq, dq), 1) // gc
    for l in range(n_step):
        mean = jnp.zeros((n_seq, dq), F32)
        for g, w in enumerate(POOL_WINDOWS):
            count = min(PAST_LEN + l + 1, w)
            mean = jnp.where(lane_grp == g, wsum(POOL_PAST + l, w) * (1.0 / count), mean)
        dd = (mean - slab(d_in, l)).astype(BF16)
        cat_ref[l * n_seq:(l + 1) * n_seq, 3 * dq:4 * dq] = (
            _dot(dd, wpool_ref[...]) * pscale_ref[...]).astype(BF16)
    store_state(sd_out, ed, POOL_PAST)

    mix = _dot(cat_ref[...], wout_ref[...])
    h_out[...] = _ln(alpha * h + mix, lmg_ref[...], lmb_ref[...])


def _mixer_sample(x_tm, sa, sc, sd, lw, first_layer, alpha, n_seq, n_step):
    rows, d = x_tm.shape
    d_mix = lw['w_out'].shape[0]
    dq = d_mix // 4
    assert n_seq % SUBLANES == 0 and n_step <= CHUNK and PAST_LEN % CHUNK == 0
    ins = [x_tm, sa, sc, sd, lw['ln_in_g'], lw['ln_in_b'], lw['w_in'], lw['conv_a_w'], lw['conv_a_b'],
           lw['ln_a_g'], lw['ln_a_b'], lw['ln_v_g'], lw['ln_v_b'], lw['ws_vec'], lw['bs_vec'],
           lw['conv_c_w'], lw['w_pool_bd'], lw['pool_scale'], lw['w_out'], lw['ln_mix_g'], lw['ln_mix_b']]
    out_shape = (
        jax.ShapeDtypeStruct((rows, d), F32),
        jax.ShapeDtypeStruct(((CONV_A_W - 1) * n_seq, dq), F32),
        jax.ShapeDtypeStruct(((CONV_C_W - 1) * n_seq, dq), F32),
        jax.ShapeDtypeStruct((POOL_PAST * n_seq, dq), F32),
        jax.ShapeDtypeStruct((rows, dq), F32),
    )
    return pl.pallas_call(
        functools.partial(_mixer_sample_kernel, first_layer, alpha, n_seq, n_step, d_mix),
        grid=(1,),
        in_specs=[_full_spec(a) for a in ins],
        out_specs=tuple(pl.BlockSpec(s.shape, lambda i: (0, 0)) for s in out_shape),
        out_shape=out_shape,
        scratch_shapes=[pltpu.VMEM((rows, d_mix), BF16)],
        compiler_params=_cparams(("arbitrary",)),
        name="mixer_sample",
    )(*ins)


def _stream_cast(src_hbm, dst_ref, stage, sems):
    n_slots, rows = stage.shape[0], stage.shape[1]
    n = src_hbm.shape[0] // rows

    def copy(c):
        return pltpu.make_async_copy(src_hbm.at[pl.ds(c * rows, rows), :], stage.at[c % n_slots],
                                     sems.at[c % n_slots])

    for c in range(min(n_slots, n)):
        copy(c).start()
    for c in range(n):
        copy(c).wait()
        dst_ref[pl.ds(c * rows, rows), :] = stage[c % n_slots].astype(BF16)
        if c + n_slots < n:
            copy(c + n_slots).start()


def _ffn_dense_kernel(alpha, tf, n_main, x_ref, xe_ref, wg_hbm, wu_hbm, wd_hbm, g_ref, b_ref, o_ref, oe_ref,
                      wg_ref, wu_ref, wd_ref, stage_in, stage_out, sems):
    i = pl.program_id(0)

    @pl.when(i == 0)
    def _():
        _stream_cast(wg_hbm, wg_ref, stage_in, sems.at[0])
        _stream_cast(wu_hbm, wu_ref, stage_in, sems.at[0])
        _stream_cast(wd_hbm, wd_ref, stage_out, sems.at[1])

    def run(xr, orf):
        x = xr[...]
        xb = x.astype(BF16)
        ffn = None
        for c0 in range(0, wg_ref.shape[1], tf):
            mid = _silu(_dot(xb, wg_ref[:, c0:c0 + tf])) * _dot(xb, wu_ref[:, c0:c0 + tf])
            part = _dot(mid.astype(BF16), wd_ref[c0:c0 + tf, :])
            ffn = part if ffn is None else ffn + part
        orf[...] = _ln(alpha * x + ffn, g_ref[...], b_ref[...])

    @pl.when(i < n_main)
    def _():
        run(x_ref, o_ref)

    @pl.when(i >= n_main)
    def _():
        run(xe_ref, oe_ref)


def _ffn_dense(x, x_extra, wg, wu, wd, g, b, alpha, tm, tf):
    t, d = x.shape
    te = x_extra.shape[0]
    dff = wg.shape[1]
    n_main, n_extra = t // tm, te // tm
    rows_in, rows_out = 128, 512
    assert t % tm == 0 and te % tm == 0 and dff % tf == 0 and d % rows_in == 0 and dff % rows_out == 0
    any_spec = pl.BlockSpec(memory_space=pl.ANY)
    main_blk = lambda i: (jnp.minimum(i, n_main - 1), 0)
    extra_blk = lambda i: (jnp.maximum(i - n_main, 0), 0)
    return pl.pallas_call(
        functools.partial(_ffn_dense_kernel, alpha, tf, n_main),
        grid=(n_main + n_extra,),
        in_specs=[pl.BlockSpec((tm, d), main_blk), pl.BlockSpec((tm, d), extra_blk),
                  any_spec, any_spec, any_spec,
                  pl.BlockSpec((1, d), lambda i: (0, 0)),
                  pl.BlockSpec((1, d), lambda i: (0, 0))],
        out_specs=(pl.BlockSpec((tm, d), main_blk), pl.BlockSpec((tm, d), extra_blk)),
        out_shape=(jax.ShapeDtypeStruct((t, d), F32), jax.ShapeDtypeStruct((te, d), F32)),
        scratch_shapes=[pltpu.VMEM(wg.shape, BF16), pltpu.VMEM(wu.shape, BF16), pltpu.VMEM(wd.shape, BF16),
                        pltpu.VMEM((2, rows_in, dff), F32), pltpu.VMEM((2, rows_out, d), F32),
                        pltpu.SemaphoreType.DMA((2, 2))],
        compiler_params=_cparams(("arbitrary",)),
        name="ffn_dense",
    )(x, x_extra, wg, wu, wd, g, b)


def _route_init(tm, cin_ref, carry_ref, before_ref):
    carry_ref[...] = cin_ref[...].astype(F32)
    row = lax.broadcasted_iota(I32, (tm, tm), 0)
    col = lax.broadcasted_iota(I32, (tm, tm), 1)
    before_ref[...] = jnp.where(row < col, 1.0, 0.0).astype(BF16)


def _route_block(tm, h, wcat_ref, mi_ref, mf_ref, cnt_ref, carry_ref, before_ref):
    h_hi = h.astype(BF16)
    h_lo = (h - h_hi.astype(F32)).astype(BF16)
    p_hi = _dot(h_hi, wcat_ref[...])
    p_lo = _dot(h_lo, wcat_ref[...])
    logits = (p_hi[:, 0:LANES] + (p_lo[:, 0:LANES] + p_hi[:, LANES:2 * LANES])) + p_lo[:, LANES:2 * LANES]
    lg = logits.T[0:N_EXPERTS, :]
    ex = lax.broadcasted_iota(I32, (N_EXPERTS, tm), 0).astype(F32)
    m1 = jnp.max(lg, axis=0, keepdims=True)
    i1 = jnp.min(jnp.where(lg == m1, ex, float(N_EXPERTS)), axis=0, keepdims=True)
    sel1 = ex == i1
    rest = jnp.where(sel1, -jnp.inf, lg)
    m2 = jnp.max(rest, axis=0, keepdims=True)
    i2 = jnp.min(jnp.where(rest == m2, ex, float(N_EXPERTS)), axis=0, keepdims=True)
    sel2 = ex == i2
    e2 = jnp.exp(m2 - m1)
    den = 1.0 + e2
    g1 = 1.0 / den
    g2 = e2 / den

    sel = jnp.where(sel1 | sel2, 1.0, 0.0)
    sel_pad = jnp.concatenate([sel, jnp.zeros_like(sel)], axis=0).astype(BF16)
    base = _dot(sel_pad, before_ref[...])[0:N_EXPERTS, :] + carry_ref[:, 0:1]
    r1 = jnp.sum(jnp.where(sel1, base, 0.0), axis=0, keepdims=True)
    r2 = jnp.sum(jnp.where(sel2, base, 0.0), axis=0, keepdims=True)
    carry_ref[...] = carry_ref[...] + jnp.sum(sel, axis=1, keepdims=True)

    meta = jnp.where(ex == 0, i1, jnp.where(ex == 1, i2, jnp.where(ex == 2, r1, r2)))
    mi_ref[...] = meta.astype(I32)
    gt = jnp.where(ex == 0, g1, jnp.where(ex == 1, g2, 0.0))
    gt = jnp.concatenate([gt, jnp.zeros((LANES - N_EXPERTS, tm), F32)], axis=0)
    mf_ref[...] = gt.T[:, 0:SUBLANES]
    cnt_ref[...] = carry_ref[...].astype(I32)


def _router_kernel(tm, h_ref, wcat_ref, cin_ref, mi_ref, mf_ref, cnt_ref, carry_ref, before_ref):
    @pl.when(pl.program_id(0) == 0)
    def _():
        _route_init(tm, cin_ref, carry_ref, before_ref)

    _route_block(tm, h_ref[...], wcat_ref, mi_ref, mf_ref, cnt_ref, carry_ref, before_ref)


def _router(h, wr_cat, counts_in):
    t, d = h.shape
    tm = min(TM_ROUTE, t)
    assert t % tm == 0 and N_EXPERTS == SUBLANES
    return pl.pallas_call(
        functools.partial(_router_kernel, tm),
        grid=(t // tm,),
        in_specs=[pl.BlockSpec((tm, d), lambda i: (i, 0)), _full_spec(wr_cat), _full_spec(counts_in)],
        out_specs=(pl.BlockSpec((SUBLANES, tm), lambda i: (0, i)),
                   pl.BlockSpec((tm, SUBLANES), lambda i: (i, 0)),
                   pl.BlockSpec((SUBLANES, LANES), lambda i: (0, 0))),
        out_shape=(jax.ShapeDtypeStruct((SUBLANES, t), I32),
                   jax.ShapeDtypeStruct((t, SUBLANES), F32),
                   jax.ShapeDtypeStruct((SUBLANES, LANES), I32)),
        scratch_shapes=[pltpu.VMEM((SUBLANES, LANES), F32), pltpu.VMEM((tm, tm), BF16)],
        compiler_params=_cparams(("arbitrary",)),
        name="router",
    )(h, wr_cat, counts_in)


def _row_copy(src, src_row, dst, dst_row, sem):
    return pltpu.make_async_copy(src.at[pl.ds(src_row, 1), :], dst.at[pl.ds(dst_row, 1), :], sem)


def _rows_wait(src, dst, n_rows, sem):
    pltpu.make_async_copy(src.at[pl.ds(0, n_rows), :], dst.at[pl.ds(0, n_rows), :], sem).wait()


def _dispatch_kernel(tm, tm_expert, n_blocks, blk_ranges, zi_ref, dest_ref, *rest):
    n_src = len(blk_ranges)
    h_refs = rest[:n_src]
    xs_ref, xbuf, zblk, lsems, rsems, zsem = rest[n_src:]
    i = pl.program_id(0)
    n = pl.num_programs(0)

    def block_load(g, start):
        slot = lax.rem(g, N_STAGE)
        for h_ref, (b0, b1) in zip(h_refs, blk_ranges):
            @pl.when((g >= b0) & (g < b1))
            def _(h_ref=h_ref, b0=b0):
                cp = pltpu.make_async_copy(h_ref.at[pl.ds((g - b0) * tm, tm), :], xbuf.at[slot],
                                           lsems.at[slot])
                if start:
                    cp.start()
                else:
                    cp.wait()

    def rows_wait(g):
        slot = lax.rem(g, N_STAGE)
        for _k in range(TOP_K):
            _rows_wait(xbuf.at[slot], xs_ref, tm, rsems.at[slot])

    @pl.when(i == 0)
    def _():
        block_load(i, True)

    @pl.when(i >= N_STAGE - 1)
    def _():
        rows_wait(i - (N_STAGE - 1))

    @pl.when(i + 1 < n)
    def _():
        block_load(i + 1, True)

    block_load(i, False)
    cur = lax.rem(i, N_STAGE)

    def issue(j, carry):
        r0 = pl.multiple_of(j * SUBLANES, SUBLANES)
        for u in range(SUBLANES):
            for k in range(TOP_K):
                dst = dest_ref[0, 0, j * SUBLANES + (k * tm + u)]
                _row_copy(xbuf.at[cur], r0 + u, xs_ref, dst, rsems.at[cur]).start(priority=k)
        return carry

    lax.fori_loop(0, tm // SUBLANES, issue, 0)

    @pl.when(i == n - 1)
    def _():
        for back in range(N_STAGE - 2, -1, -1):
            @pl.when(i >= back)
            def _(back=back):
                rows_wait(i - back)

    @pl.when(i == 0)
    def _():
        zblk[...] = jnp.zeros_like(zblk)
        for e in range(N_EXPERTS):
            lo = zi_ref[e] + zi_ref[N_EXPERTS + e]
            hi = zi_ref[e] + zi_ref[2 * N_EXPERTS + e]

            def zissue(r, carry):
                _row_copy(zblk, 0, xs_ref, r, zsem).start()
                return carry

            def zdrain(r, carry):
                _row_copy(zblk, 0, xs_ref, r, zsem).wait()
                return carry

            lax.fori_loop(lo, hi, zissue, 0)
            lax.fori_loop(lo, hi, zdrain, 0)

        def bcopy(b):
            return pltpu.make_async_copy(zblk, xs_ref.at[pl.ds(b * tm_expert, tm_expert), :], zsem)

        def bissue(b, carry):
            bcopy(b).start()
            return carry

        def bdrain(b, carry):
            bcopy(b).wait()
            return carry

        lax.fori_loop(zi_ref[3 * N_EXPERTS], n_blocks, bissue, 0)
        lax.fori_loop(zi_ref[3 * N_EXPERTS], n_blocks, bdrain, 0)


def _dispatch(hs_list, dest_blk, zinfo, n_blocks, tm_expert):
    d = hs_list[0].shape[1]
    nb = dest_blk.shape[0]
    tm = dest_blk.shape[2] // TOP_K
    blk_ranges, b0 = [], 0
    for h in hs_list:
        assert h.shape[0] % tm == 0 and h.shape[0] >= tm
        blk_ranges.append((b0, b0 + h.shape[0] // tm))
        b0 = blk_ranges[-1][1]
    assert b0 == nb
    any_spec = pl.BlockSpec(memory_space=pl.ANY)
    return pl.pallas_call(
        functools.partial(_dispatch_kernel, tm, tm_expert, n_blocks, tuple(blk_ranges)),
        grid_spec=pltpu.PrefetchScalarGridSpec(
            num_scalar_prefetch=1,
            grid=(nb,),
            in_specs=[pl.BlockSpec((1, 1, TOP_K * tm), lambda i, zi: (i, 0, 0), memory_space=pltpu.SMEM)]
            + [any_spec] * len(hs_list),
            out_specs=any_spec,
            scratch_shapes=[pltpu.VMEM((N_STAGE, tm, d), F32), pltpu.VMEM((tm_expert, d), F32),
                            pltpu.SemaphoreType.DMA((N_STAGE,)), pltpu.SemaphoreType.DMA((N_STAGE,)),
                            pltpu.SemaphoreType.DMA(())],
        ),
        out_shape=jax.ShapeDtypeStruct((n_blocks * tm_expert, d), F32),
        compiler_params=_cparams(("arbitrary",)),
        name="moe_dispatch",
    )(zinfo, dest_blk, *hs_list)


def _ffn_moe_kernel(tf, be_ref, nu_ref, x_ref, wg_hbm, wu_hbm, wd_hbm, y_ref,
                    wg_ref, wu_ref, wd_ref, stage_in, stage_out, sems):
    b = pl.program_id(0)
    nb = pl.num_programs(0)
    d, dff = wg_ref.shape
    n_slots = stage_in.shape[0]
    rows_in = stage_in.shape[1]
    rows_out = stage_out.shape[1]
    n_in, n_out = 2 * (d // rows_in), dff // rows_out
    e = be_ref[b]
    fresh = (b == 0) | (e != be_ref[jnp.maximum(b - 1, 0)])
    e_next = be_ref[jnp.minimum(b + 1, nb - 1)]

    def chunk_in(ex, n):
        src, dst = (wg_hbm, wg_ref) if n < n_in // 2 else (wu_hbm, wu_ref)
        rows = pl.ds((n % (n_in // 2)) * rows_in, rows_in)
        stage = stage_in.at[n % n_slots]
        return pltpu.make_async_copy(src.at[ex, rows, :], stage, sems.at[0, n % n_slots]), stage, dst.at[rows, :]

    def chunk_out(ex, n):
        rows = pl.ds(n * rows_out, rows_out)
        stage = stage_out.at[n % n_slots]
        return pltpu.make_async_copy(wd_hbm.at[ex, rows, :], stage, sems.at[1, n % n_slots]), stage, wd_ref.at[rows, :]

    def request_first(ex):
        for n in range(n_slots):
            chunk_in(ex, n)[0].start()
            chunk_out(ex, n)[0].start()

    @pl.when((b < nu_ref[0]) & fresh)
    def _():
        @pl.when(b == 0)
        def _():
            request_first(e)

        for n in range(n_in):
            for chunk, count in ((chunk_in, n_in), (chunk_out, n_out)):
                if n < count:
                    copy, staged, dst = chunk(e, n)
                    copy.wait()
                    dst[...] = staged[...].astype(BF16)
                    if n + n_slots < count:
                        chunk(e, n + n_slots)[0].start()

    @pl.when((b + 1 < nu_ref[0]) & (e_next != e))
    def _():
        request_first(e_next)

    @pl.when(b < nu_ref[0])
    def _():
        xb = x_ref[...].astype(BF16)
        ffn = None
        for c0 in range(0, dff, tf):
            mid = _silu(_dot(xb, wg_ref[:, c0:c0 + tf])) * _dot(xb, wu_ref[:, c0:c0 + tf])
            part = _dot(mid.astype(BF16), wd_ref[c0:c0 + tf, :])
            ffn = part if ffn is None else ffn + part
        y_ref[...] = ffn

    @pl.when(b >= nu_ref[0])
    def _():
        y_ref[...] = jnp.zeros_like(y_ref)


def _ffn_moe(xs, wg, wu, wd, block_e, n_used, tm, tf):
    cap, d = xs.shape
    dff = wg.shape[2]
    rows_in, rows_out = 128, 512
    assert cap % tm == 0 and dff % tf == 0 and d % rows_in == 0 and dff % rows_out == 0
    assert min(2 * (d // rows_in), dff // rows_out) >= N_WSTAGE
    any_spec = pl.BlockSpec(memory_space=pl.ANY)
    return pl.pallas_call(
        functools.partial(_ffn_moe_kernel, tf),
        grid_spec=pltpu.PrefetchScalarGridSpec(
            num_scalar_prefetch=2,
            grid=(cap // tm,),
            in_specs=[pl.BlockSpec((tm, d), lambda b, be, nu: (jnp.minimum(b, nu[0] - 1), 0)),
                      any_spec, any_spec, any_spec],
            out_specs=pl.BlockSpec((tm, d), lambda b, be, nu: (b, 0)),
            scratch_shapes=[pltpu.VMEM((d, dff), BF16), pltpu.VMEM((d, dff), BF16), pltpu.VMEM((dff, d), BF16),
                            pltpu.VMEM((N_WSTAGE, rows_in, dff), F32), pltpu.VMEM((N_WSTAGE, rows_out, d), F32),
                            pltpu.SemaphoreType.DMA((2, N_WSTAGE))],
        ),
        out_shape=jax.ShapeDtypeStruct((cap, d), F32),
        compiler_params=_cparams(("arbitrary",)),
        name="ffn_moe",
    )(block_e, n_used, xs, wg, wu, wd)


def _combine_kernel(alpha, tm, dcur_ref, dnext_ref, h_ref, gate_ref, g_ref, b_ref, ys_ref, o_ref, ybuf, sems):
    i = pl.program_id(0)
    n = pl.num_programs(0)

    def issue(dref, slot):
        def body(j, carry):
            r0 = pl.multiple_of(j * SUBLANES, SUBLANES)
            for u in range(SUBLANES):
                for k in range(TOP_K):
                    src = dref[0, 0, j * SUBLANES + (k * tm + u)]
                    _row_copy(ys_ref, src, ybuf.at[slot, k], r0 + u, sems.at[slot]).start(priority=k)
            return carry

        lax.fori_loop(0, tm // SUBLANES, body, 0)

    @pl.when(i == 0)
    def _():
        issue(dcur_ref, 0)

    @pl.when(i + 1 < n)
    def _():
        issue(dnext_ref, (i + 1) % 2)

    slot = i % 2
    for k in range(TOP_K):
        _rows_wait(ys_ref, ybuf.at[slot, k], tm, sems.at[slot])

    gates = gate_ref[...]
    ffn = gates[:, 0:1] * ybuf[slot, 0] + gates[:, 1:2] * ybuf[slot, 1]
    o_ref[...] = _ln(alpha * h_ref[...] + ffn, g_ref[...], b_ref[...])


def _combine(h, dest_blk, gates, ys, g, b, alpha):
    t, d = h.shape
    nb = dest_blk.shape[0]
    tm = dest_blk.shape[2] // TOP_K
    dest_spec = lambda imap: pl.BlockSpec((1, 1, TOP_K * tm), imap, memory_space=pltpu.SMEM)
    return pl.pallas_call(
        functools.partial(_combine_kernel, alpha, tm),
        grid=(nb,),
        in_specs=[dest_spec(lambda i: (i, 0, 0)),
                  dest_spec(lambda i: (jnp.minimum(i + 1, nb - 1), 0, 0)),
                  pl.BlockSpec((tm, d), lambda i: (i, 0)),
                  pl.BlockSpec((tm, SUBLANES), lambda i: (i, 0)),
                  pl.BlockSpec((1, d), lambda i: (0, 0)),
                  pl.BlockSpec((1, d), lambda i: (0, 0)),
                  pl.BlockSpec(memory_space=pl.ANY)],
        out_specs=pl.BlockSpec((tm, d), lambda i: (i, 0)),
        out_shape=jax.ShapeDtypeStruct((t, d), F32),
        scratch_shapes=[pltpu.VMEM((2, TOP_K, tm, d), F32), pltpu.SemaphoreType.DMA((2,))],
        compiler_params=_cparams(("arbitrary",)),
        name="moe_combine",
    )(dest_blk, dest_blk, h, gates, g, b, ys)


def _ffn_routed(hs_list, first_routed, mw, g, b, alpha, tm_expert):
    routed = [first_routed[:2]]
    counts_in = first_routed[2]
    for h in hs_list[1:]:
        meta_i, gates, counts_in = _router(h, mw['w_router_cat'], counts_in)
        routed.append((meta_i, gates))
    counts = counts_in[:N_EXPERTS, 0]
    padded = (counts + tm_expert - 1) // tm_expert * tm_expert
    pad_end = jnp.cumsum(padded)
    pad_start = (pad_end - padded).astype(I32)
    n_assign = sum(h.shape[0] for h in hs_list) * TOP_K
    n_blocks = -(-n_assign // tm_expert) + N_EXPERTS
    cap = n_blocks * tm_expert
    n_used = (pad_end[-1] // tm_expert).astype(I32)
    blk_start = jnp.minimum(jnp.arange(n_blocks, dtype=I32), n_used - 1) * tm_expert
    block_e = jnp.minimum(jnp.sum(blk_start[:, None] >= pad_end[None, :], axis=1), N_EXPERTS - 1).astype(I32)
    zinfo = jnp.concatenate([pad_start, counts, padded, n_used.reshape(1)]).astype(I32)

    dests = []
    for h, (meta_i, _) in zip(hs_list, routed):
        dest = meta_i[TOP_K:2 * TOP_K, :]
        for e in range(N_EXPERTS):
            dest = dest + jnp.where(meta_i[0:TOP_K, :] == e, pad_start[e], 0)
        nb = h.shape[0] // TM_MOVE
        dests.append(dest.reshape(TOP_K, nb, TM_MOVE).transpose(1, 0, 2).reshape(nb, 1, TOP_K * TM_MOVE))
    xs = _dispatch(hs_list, jnp.concatenate(dests, axis=0), zinfo, n_blocks, tm_expert)
    ys = _ffn_moe(xs, mw['w_e_gate'], mw['w_e_up'], mw['w_e_down'], block_e, n_used.reshape(1), tm_expert,
                  TF_EXPERT)
    return [_combine(h, dest_blk, gates, ys, g, b, alpha)
            for h, dest_blk, (_, gates) in zip(hs_list, dests, routed)]


def _prep_layer(i, p, n_step):
    dq = p['conv_a_w'].shape[2]
    hd = dq // N_HEADS_B
    tril = jnp.tril(jnp.ones((CHUNK, CHUNK), dtype=bool))
    w_s = p['w_s'][i]
    b_s = p['b_s'][i]
    wp = p['w_pool'][i]
    ng, gc = wp.shape[0], wp.shape[1]
    w_pool_bd = (jnp.eye(ng, dtype=F32)[:, None, :, None] * wp[:, :, None, :]).reshape(ng * gc, ng * gc)
    row = lambda a: a.reshape(1, -1)
    return dict(
        ln_in_g=row(p['ln_in_g']), ln_in_b=row(p['ln_in_b']),
        w_in=p['w_in'][i].astype(BF16),
        conv_a_w=p['conv_a_w'][i], conv_a_b=row(p['conv_a_b'][i]),
        ln_a_g=row(p['ln_a_g'][i]), ln_a_b=row(p['ln_a_b'][i]),
        ln_v_g=row(p['ln_v_g'][i]), ln_v_b=row(p['ln_v_b'][i]),
        ws_tril=jnp.where(tril[None], w_s, 0).astype(BF16),
        bs_mat=jnp.repeat(b_s.T, hd, axis=1),
        ws_vec=jnp.repeat(jnp.where(tril[None], w_s, 0)[:, :n_step, :n_step].transpose(1, 2, 0)
                          .reshape(n_step * n_step, N_HEADS_B), hd, axis=1),
        bs_vec=jnp.repeat(b_s[:, :n_step].T, hd, axis=1),
        conv_c_w=p['conv_c_w'][i],
        w_pool_bd=w_pool_bd.astype(BF16), pool_scale=row(p['pool_scale'][i]),
        w_out=p['w_out'][i].astype(BF16),
        ln_mix_g=row(p['ln_mix_g'][i]), ln_mix_b=row(p['ln_mix_b'][i]),
    )


def kernel(x_prompt, x_sample, state_conv_a, state_conv_c, state_pool_d, ln_in_g, ln_in_b, w_in, conv_a_w,
           conv_a_b, ln_a_g, ln_a_b, ln_v_g, ln_v_b, w_s, b_s, conv_c_w, w_pool, pool_scale, w_out,
           ln_mix_g, ln_mix_b, w_ff_gate, w_ff_up, w_ff_down, w_router, w_e_gate, w_e_up, w_e_down,
           ln_ffn_g, ln_ffn_b):
    p = dict(ln_in_g=ln_in_g, ln_in_b=ln_in_b, w_in=w_in, conv_a_w=conv_a_w, conv_a_b=conv_a_b,
             ln_a_g=ln_a_g, ln_a_b=ln_a_b, ln_v_g=ln_v_g, ln_v_b=ln_v_b, w_s=w_s, b_s=b_s,
             conv_c_w=conv_c_w, w_pool=w_pool, pool_scale=pool_scale, w_out=w_out,
             ln_mix_g=ln_mix_g, ln_mix_b=ln_mix_b)
    depth = w_in.shape[0]
    bsz, seq, d = x_prompt.shape
    n_seq, n_step, _ = x_sample.shape
    dq = conv_a_w.shape[2]
    alpha = float((2.0 * depth) ** 0.25)

    hp = x_prompt
    hs = x_sample.transpose(1, 0, 2).reshape(n_step * n_seq, d)
    outs = {k: [] for k in ('pa', 'pc', 'pd', 'pv', 'sa', 'sc', 'sd', 'sv')}
    for i in range(depth):
        lw = _prep_layer(i, p, n_step)
        j = i // 2
        routed = i % 2 == 1
        if routed:
            wr = jnp.pad(w_router[j], ((0, 0), (0, LANES - N_EXPERTS)))
            wr_hi = wr.astype(BF16)
            wr_lo = (wr - wr_hi.astype(F32)).astype(BF16)
            wr_cat = jnp.concatenate([wr_hi, wr_lo], axis=1)
        hp, pa, pc, pd, pv, *route_p = _mixer_prompt(hp, lw, i == 0, alpha, wr_cat if routed else None)
        tm_state = lambda s: s.transpose(1, 0, 2).reshape(-1, dq)
        hs, sa, sc, sd, sv = _mixer_sample(hs, tm_state(state_conv_a[i]), tm_state(state_conv_c[i]),
                                           tm_state(state_pool_d[i]), lw, i == 0, alpha, n_seq, n_step)
        lg, lb = ln_ffn_g[i].reshape(1, -1), ln_ffn_b[i].reshape(1, -1)
        if not routed:
            hp, hs = _ffn_dense(hp.reshape(bsz * seq, d), hs, w_ff_gate[j], w_ff_up[j], w_ff_down[j], lg, lb,
                                alpha, TM_DENSE, TF_DENSE)
            hp = hp.reshape(bsz, seq, d)
        else:
            mw = dict(w_router_cat=wr_cat, w_e_gate=w_e_gate[j], w_e_up=w_e_up[j], w_e_down=w_e_down[j])
            hp, hs = _ffn_routed([hp.reshape(bsz * seq, d), hs], route_p, mw, lg, lb, alpha, TM_EXPERT)
            hp = hp.reshape(bsz, seq, d)
        back = lambda a: a.reshape(-1, n_seq, dq).transpose(1, 0, 2)
        for k, val in zip(('pa', 'pc', 'pd', 'pv', 'sa', 'sc', 'sd', 'sv'),
                          (pa, pc, pd, pv, back(sa), back(sc), back(sd), back(sv))):
            outs[k].append(val)
    y_sample = hs.reshape(n_step, n_seq, d).transpose(1, 0, 2)
    st = lambda k: jnp.stack(outs[k])
    return (hp, y_sample, st('pa'), st('pc'), st('pd'), st('pv'), st('sa'), st('sc'), st('sd'), st('sv'))
```

```python
import functools

import numpy as np
import jax
import jax.numpy as jnp
from jax import lax
from jax.experimental import pallas as pl
from jax.experimental.pallas import tpu as pltpu

F32 = jnp.float32
BF16 = jnp.bfloat16
I32 = jnp.int32

PAST_LEN = 16384
CHUNK = 128
N_HEADS_B = 4
POOL_WINDOWS = (2, 4, 8, 16)
POOL_PAST = 15
CONV_A_W = 31
CONV_C_W = 3
N_EXPERTS = 8
TOP_K = 2
LN_EPS = 1e-5
SQRT_HALF = float(np.sqrt(0.5).astype(np.float32))

LANES = 128
SUBLANES = 8
VMEM_LIMIT_BYTES = 56 * 1024 * 1024

HIST_A = 32
HIST_C = 8
HIST_D = 24

TL_PROMPT = 512
TM_DENSE = 512
TF_DENSE = 1792
TF_EXPERT = 1792
TM_ROUTE = 512
TM_MOVE = 256
TM_EXPERT = 512
N_STAGE = 3
N_WSTAGE = 5


def _ln(x, g, b):
    mu = jnp.mean(x, axis=-1, keepdims=True)
    xc = x - mu
    var = jnp.mean(xc * xc, axis=-1, keepdims=True)
    return xc * lax.rsqrt(var + LN_EPS) * g + b


def _gelu(x):
    return 0.5 * x * (1.0 + lax.erf(x * SQRT_HALF))


def _silu(x):
    return x * jax.nn.sigmoid(x)


def _dot(a, b):
    return jnp.dot(a, b, preferred_element_type=F32)


def _cparams(sem):
    return pltpu.CompilerParams(dimension_semantics=sem, vmem_limit_bytes=VMEM_LIMIT_BYTES)


def _mixer_prompt_kernel(first_layer, route, alpha, tl, d_mix,
                         x_ref, lng_ref, lnb_ref, win_ref, caw_ref, cab_ref, lag_ref, lab_ref,
                         lvg_ref, lvb_ref, ws_ref, bsm_ref, ccw_ref, wpool_ref, pscale_ref,
                         wout_ref, lmg_ref, lmb_ref, *refs):
    if route:
        wcat_ref, cin_ref = refs[:2]
        refs = refs[2:]
    h_out, sa_out, sc_out, sd_out, v_out = refs[:5]
    refs = refs[5:]
    if route:
        mi_ref, mf_ref, cnt_ref = refs[:3]
        refs = refs[3:]
    ext_a, ext_c, ext_d, cat_ref, z_ref, hb_ref, mix_ref = refs[:7]
    dq = d_mix // 4
    j = pl.program_id(1)
    nj = pl.num_programs(1)
    if route:
        carry_ref, before_ref = refs[7:]

        @pl.when((pl.program_id(0) == 0) & (j == 0))
        def _():
            _route_init(tl, cin_ref, carry_ref, before_ref)

    @pl.when(j == 0)
    def _():
        ext_a[0, 0:HIST_A, :] = jnp.zeros((HIST_A, dq), F32)
        ext_c[0:HIST_C, :] = jnp.zeros((HIST_C, dq), F32)
        ext_d[0:HIST_D, :] = jnp.zeros((HIST_D, dq), F32)

    d = x_ref.shape[2]
    rb = CHUNK

    def resid(r0, n):
        return h_out[0, r0:r0 + n, :] if first_layer else x_ref[0, r0:r0 + n, :]

    for r0 in range(0, tl, rb):
        x = x_ref[0, r0:r0 + rb, :]
        if first_layer:
            x = _ln(x, lng_ref[...], lnb_ref[...])
            h_out[0, r0:r0 + rb, :] = x
        hb_ref[r0:r0 + rb, :] = x.astype(BF16)

    z_ref[...] = _dot(hb_ref[...], win_ref[...])

    def zcol(r0, i, n=rb):
        return z_ref[r0:r0 + n, i * dq:(i + 1) * dq]

    chunk_rows = range(0, tl, CHUNK)

    for r0 in chunk_rows:
        ext_a[0, HIST_A + r0:HIST_A + r0 + rb, :] = zcol(r0, 0) * jax.nn.sigmoid(zcol(r0, 1))
    n_ext = HIST_A + tl
    for r0 in range(0, n_ext - SUBLANES, rb):
        n = min(rb, n_ext - SUBLANES - r0)
        blk = ext_a[0, r0:r0 + n + SUBLANES, :]
        for s in range(1, SUBLANES):
            ext_a[s, r0:r0 + n, :] = pltpu.roll(blk, n + SUBLANES - s, axis=0)[0:n, :]
    off_a = HIST_A - (CONV_A_W - 1)
    for q0 in range(0, tl, 64):
        acc = jnp.zeros((64, dq), F32)
        for k in range(CONV_A_W):
            s = (off_a + k) % SUBLANES
            row = off_a + k - s + q0
            acc = acc + ext_a[s, row:row + 64, :] * caw_ref[k:k + 1, :]
        y = _silu(_ln(acc + cab_ref[...], lag_ref[...], lab_ref[...]))
        cat_ref[q0:q0 + 64, 0:dq] = y.astype(BF16)

    hd = dq // N_HEADS_B
    lane_head = lax.broadcasted_iota(I32, (CHUNK, dq), 1) // hd
    for r0 in chunk_rows:
        v = _ln(_gelu(zcol(r0, 3)), lvg_ref[...], lvb_ref[...])
        sm = bsm_ref[...]
        for hh in range(N_HEADS_B):
            vm = jnp.where(lane_head == hh, v, 0.0).astype(BF16)
            sm = sm + _dot(ws_ref[hh], vm)
        cat_ref[r0:r0 + CHUNK, dq:2 * dq] = (_gelu(zcol(r0, 2)) * sm).astype(BF16)
        if r0 == tl - CHUNK:
            @pl.when(j == nj - 1)
            def _(v=v):
                v_out[0] = v

    off_c = HIST_C - (CONV_C_W - 1)
    for r0 in chunk_rows:
        ext_c[HIST_C + r0:HIST_C + r0 + rb, :] = zcol(r0, 6) * zcol(r0, 4)
    for r0 in chunk_rows:
        conv_c = jnp.zeros((CHUNK, dq), F32)
        for k in range(CONV_C_W):
            conv_c = conv_c + ext_c[off_c + r0 + k:off_c + r0 + k + CHUNK, :] * ccw_ref[k:k + 1, :]
        cat_ref[r0:r0 + CHUNK, 2 * dq:3 * dq] = (zcol(r0, 5) * conv_c).astype(BF16)

    gc = dq // len(POOL_WINDOWS)
    lane_grp = lax.broadcasted_iota(I32, (CHUNK, dq), 1) // gc
    lead = 2 * SUBLANES
    for r0 in chunk_rows:
        ext_d[HIST_D + r0:HIST_D + r0 + rb, :] = zcol(r0, 7)
    for r0 in chunk_rows:
        e = ext_d[HIST_D + r0 - lead:HIST_D + r0 + CHUNK, :]
        s2 = e + pltpu.roll(e, 1, axis=0)
        s4 = s2 + pltpu.roll(s2, 2, axis=0)
        s8 = s4 + pltpu.roll(s4, 4, axis=0)
        s16 = s8 + pltpu.roll(s8, 8, axis=0)
        sums = (s2, s4, s8, s16)
        pos = (j * tl + r0 + lax.broadcasted_iota(I32, (CHUNK, 1), 0)).astype(F32)
        mean = jnp.zeros((CHUNK, dq), F32)
        for g, w in enumerate(POOL_WINDOWS):
            inv = 1.0 / jnp.minimum(pos + 1.0, float(w))
            mean = jnp.where(lane_grp == g, sums[g][lead:lead + CHUNK, :] * inv, mean)
        dd = (mean - e[lead:lead + CHUNK, :]).astype(BF16)
        cat_ref[r0:r0 + CHUNK, 3 * dq:4 * dq] = (_dot(dd, wpool_ref[...]) * pscale_ref[...]).astype(BF16)

    mix_ref[...] = _dot(cat_ref[...], wout_ref[...])
    for r0 in range(0, tl, 64):
        h_out[0, r0:r0 + 64, :] = _ln(alpha * resid(r0, 64) + mix_ref[r0:r0 + 64, :], lmg_ref[...], lmb_ref[...])
    if route:
        _route_block(tl, h_out[0], wcat_ref, mi_ref, mf_ref, cnt_ref, carry_ref, before_ref)

    @pl.when(j == nj - 1)
    def _():
        sa_out[0] = ext_a[0, HIST_A + tl - (CONV_A_W - 1):HIST_A + tl, :]
        sc_out[0] = ext_c[HIST_C + tl - (CONV_C_W - 1):HIST_C + tl, :]
        sd_out[0] = ext_d[HIST_D + tl - POOL_PAST:HIST_D + tl, :]

    ext_a[0, 0:HIST_A, :] = ext_a[0, tl:tl + HIST_A, :]
    ext_c[0:HIST_C, :] = ext_c[tl:tl + HIST_C, :]
    ext_d[0:HIST_D, :] = ext_d[tl:tl + HIST_D, :]


def _full_spec(arr):
    nd = arr.ndim
    return pl.BlockSpec(arr.shape, lambda *_: (0,) * nd)


def _mixer_prompt(x, lw, first_layer, alpha, route_w=None):
    bsz, seq, d = x.shape
    d_mix = lw['w_out'].shape[0]
    dq = d_mix // 4
    tl = TL_PROMPT
    nj = seq // tl
    route = route_w is not None
    assert seq % tl == 0 and tl % CHUNK == 0 and seq >= CHUNK
    params = [lw['ln_in_g'], lw['ln_in_b'], lw['w_in'], lw['conv_a_w'], lw['conv_a_b'], lw['ln_a_g'],
              lw['ln_a_b'], lw['ln_v_g'], lw['ln_v_b'], lw['ws_tril'], lw['bs_mat'], lw['conv_c_w'],
              lw['w_pool_bd'], lw['pool_scale'], lw['w_out'], lw['ln_mix_g'], lw['ln_mix_b']]
    out_shape = [
        jax.ShapeDtypeStruct((bsz, seq, d), F32),
        jax.ShapeDtypeStruct((bsz, CONV_A_W - 1, dq), F32),
        jax.ShapeDtypeStruct((bsz, CONV_C_W - 1, dq), F32),
        jax.ShapeDtypeStruct((bsz, POOL_PAST, dq), F32),
        jax.ShapeDtypeStruct((bsz, CHUNK, dq), F32),
    ]
    state_spec = lambda rows: pl.BlockSpec((1, rows, dq), lambda b, j: (b, 0, 0))
    out_specs = [pl.BlockSpec((1, tl, d), lambda b, j: (b, j, 0)),
                 state_spec(CONV_A_W - 1), state_spec(CONV_C_W - 1), state_spec(POOL_PAST), state_spec(CHUNK)]
    scratch = [pltpu.VMEM((SUBLANES, HIST_A + tl, dq), F32), pltpu.VMEM((HIST_C + tl, dq), F32),
               pltpu.VMEM((HIST_D + tl, dq), F32), pltpu.VMEM((tl, d_mix), BF16),
               pltpu.VMEM((tl, lw['w_in'].shape[1]), F32), pltpu.VMEM((tl, d), BF16),
               pltpu.VMEM((tl, d), F32)]
    if route:
        assert N_EXPERTS == SUBLANES
        params += [route_w, jnp.zeros((SUBLANES, LANES), I32)]
        out_shape += [jax.ShapeDtypeStruct((SUBLANES, bsz * seq), I32),
                      jax.ShapeDtypeStruct((bsz * seq, SUBLANES), F32),
                      jax.ShapeDtypeStruct((SUBLANES, LANES), I32)]
        out_specs += [pl.BlockSpec((SUBLANES, tl), lambda b, j: (0, b * nj + j)),
                      pl.BlockSpec((tl, SUBLANES), lambda b, j: (b * nj + j, 0)),
                      pl.BlockSpec((SUBLANES, LANES), lambda b, j: (0, 0))]
        scratch += [pltpu.VMEM((SUBLANES, LANES), F32), pltpu.VMEM((tl, tl), BF16)]
    return pl.pallas_call(
        functools.partial(_mixer_prompt_kernel, first_layer, route, alpha, tl, d_mix),
        grid=(bsz, nj),
        in_specs=[pl.BlockSpec((1, tl, d), lambda b, j: (b, j, 0))] + [_full_spec(p) for p in params],
        out_specs=tuple(out_specs),
        out_shape=tuple(out_shape),
        scratch_shapes=scratch,
        compiler_params=_cparams(("arbitrary", "arbitrary")),
        name="mixer_prompt",
    )(x, *params)


def _mixer_sample_kernel(first_layer, alpha, n_seq, n_step, d_mix,
                         x_ref, sa_ref, sc_ref, sd_ref, lng_ref, lnb_ref, win_ref, caw_ref, cab_ref,
                         lag_ref, lab_ref, lvg_ref, lvb_ref, wsv_ref, bsv_ref, ccw_ref, wpool_ref,
                         pscale_ref, wout_ref, lmg_ref, lmb_ref,
                         h_out, sa_out, sc_out, sd_out, v_out, cat_ref):
    dq = d_mix // 4
    x = x_ref[...]
    h = _ln(x, lng_ref[...], lnb_ref[...]) if first_layer else x
    hb = h.astype(BF16)

    def proj(i):
        return _dot(hb, win_ref[:, i * dq:(i + 1) * dq])

    def slab(val, i):
        return val[i * n_seq:(i + 1) * n_seq, :]

    def ext_slabs(state_ref, n_past, cur):
        return ([state_ref[i * n_seq:(i + 1) * n_seq, :] for i in range(n_past)]
                + [slab(cur, i) for i in range(n_step)])

    def store_state(out_ref, slabs, n_keep):
        for i, sl in enumerate(slabs[len(slabs) - n_keep:]):
            out_ref[i * n_seq:(i + 1) * n_seq, :] = sl

    a_glu = proj(0) * jax.nn.sigmoid(proj(1))
    ea = ext_slabs(sa_ref, CONV_A_W - 1, a_glu)
    for l in range(n_step):
        acc = jnp.zeros((n_seq, dq), F32)
        for k in range(CONV_A_W):
            acc = acc + ea[l + k] * caw_ref[k:k + 1, :]
        y = _silu(_ln(acc + cab_ref[...], lag_ref[...], lab_ref[...]))
        cat_ref[l * n_seq:(l + 1) * n_seq, 0:dq] = y.astype(BF16)
    store_state(sa_out, ea, CONV_A_W - 1)

    u = _gelu(proj(2))
    v = _ln(_gelu(proj(3)), lvg_ref[...], lvb_ref[...])
    v_out[...] = v
    for l in range(n_step):
        s = jnp.zeros((n_seq, dq), F32) + bsv_ref[l:l + 1, :]
        for m in range(l + 1):
            s = s + slab(v, m) * wsv_ref[l * n_step + m:l * n_step + m + 1, :]
        cat_ref[l * n_seq:(l + 1) * n_seq, dq:2 * dq] = (slab(u, l) * s).astype(BF16)

    c_x = proj(4)
    c_b = proj(5)
    c_c = proj(6)
    gx = c_c * c_x
    ec = ext_slabs(sc_ref, CONV_C_W - 1, gx)
    for l in range(n_step):
        acc = jnp.zeros((n_seq, dq), F32)
        for k in range(CONV_C_W):
            acc = acc + ec[l + k] * ccw_ref[k:k + 1, :]
        cat_ref[l * n_seq:(l + 1) * n_seq, 2 * dq:3 * dq] = (slab(c_b, l) * acc).astype(BF16)
    store_state(sc_out, ec, CONV_C_W - 1)

    d_in = proj(7)
    ed = ext_slabs(sd_ref, POOL_PAST, d_in)
    memo = {}

    def wsum(i, w):
        if i < 0:
            return None
        if w == 1:
            return ed[i]
        if (i, w) not in memo:
            a, b = wsum(i, w // 2), wsum(i - w // 2, w // 2)
            memo[(i, w)] = a if b is None else a + b
        return memo[(i, w)]

    gc = dq // len(POOL_WINDOWS)
    lane_grp = lax.broadcasted_iota(I32, (n_seq, dq), 1) // gc
    for l in range(n_step):
        mean = jnp.zeros((n_seq, dq), F32)
        for g, w in enumerate(POOL_WINDOWS):
            count = min(PAST_LEN + l + 1, w)
            mean = jnp.where(lane_grp == g, wsum(POOL_PAST + l, w) * (1.0 / count), mean)
        dd = (mean - slab(d_in, l)).astype(BF16)
        cat_ref[l * n_seq:(l + 1) * n_seq, 3 * dq:4 * dq] = (
            _dot(dd, wpool_ref[...]) * pscale_ref[...]).astype(BF16)
    store_state(sd_out, ed, POOL_PAST)

    mix = _dot(cat_ref[...], wout_ref[...])
    h_out[...] = _ln(alpha * h + mix, lmg_ref[...], lmb_ref[...])


def _mixer_sample(x_tm, sa, sc, sd, lw, first_layer, alpha, n_seq, n_step):
    rows, d = x_tm.shape
    d_mix = lw['w_out'].shape[0]
    dq = d_mix // 4
    assert n_seq % SUBLANES == 0 and n_step <= CHUNK and PAST_LEN % CHUNK == 0
    ins = [x_tm, sa, sc, sd, lw['ln_in_g'], lw['ln_in_b'], lw['w_in'], lw['conv_a_w'], lw['conv_a_b'],
           lw['ln_a_g'], lw['ln_a_b'], lw['ln_v_g'], lw['ln_v_b'], lw['ws_vec'], lw['bs_vec'],
           lw['conv_c_w'], lw['w_pool_bd'], lw['pool_scale'], lw['w_out'], lw['ln_mix_g'], lw['ln_mix_b']]
    out_shape = (
        jax.ShapeDtypeStruct((rows, d), F32),
        jax.ShapeDtypeStruct(((CONV_A_W - 1) * n_seq, dq), F32),
        jax.ShapeDtypeStruct(((CONV_C_W - 1) * n_seq, dq), F32),
        jax.ShapeDtypeStruct((POOL_PAST * n_seq, dq), F32),
        jax.ShapeDtypeStruct((rows, dq), F32),
    )
    return pl.pallas_call(
        functools.partial(_mixer_sample_kernel, first_layer, alpha, n_seq, n_step, d_mix),
        grid=(1,),
        in_specs=[_full_spec(a) for a in ins],
        out_specs=tuple(pl.BlockSpec(s.shape, lambda i: (0, 0)) for s in out_shape),
        out_shape=out_shape,
        scratch_shapes=[pltpu.VMEM((rows, d_mix), BF16)],
        compiler_params=_cparams(("arbitrary",)),
        name="mixer_sample",
    )(*ins)


def _stream_cast(pairs, stage, sems):
    n_slots, rows = stage.shape[0], stage.shape[1]
    chunks = [(src, dst, r0) for src, dst in pairs for r0 in range(0, src.shape[0], rows)]

    def copy(c):
        src, _, r0 = chunks[c]
        return pltpu.make_async_copy(src.at[pl.ds(r0, rows), :], stage.at[c % n_slots], sems.at[c % n_slots])

    for c in range(min(n_slots, len(chunks))):
        copy(c).start()
    for c, (_, dst, r0) in enumerate(chunks):
        copy(c).wait()
        dst[pl.ds(r0, rows), :] = stage[c % n_slots].astype(BF16)
        if c + n_slots < len(chunks):
            copy(c + n_slots).start()


def _ffn_dense_kernel(alpha, tf, n_main, x_ref, xe_ref, wg_hbm, wu_hbm, wd_hbm, g_ref, b_ref, o_ref, oe_ref,
                      wg_ref, wu_ref, wd_ref, stage_in, stage_out, sems):
    i = pl.program_id(0)

    @pl.when(i == 0)
    def _():
        _stream_cast([(wg_hbm, wg_ref), (wu_hbm, wu_ref)], stage_in, sems.at[0])
        _stream_cast([(wd_hbm, wd_ref)], stage_out, sems.at[1])

    def run(xr, orf):
        x = xr[...]
        xb = x.astype(BF16)
        ffn = None
        for c0 in range(0, wg_ref.shape[1], tf):
            mid = _silu(_dot(xb, wg_ref[:, c0:c0 + tf])) * _dot(xb, wu_ref[:, c0:c0 + tf])
            part = _dot(mid.astype(BF16), wd_ref[c0:c0 + tf, :])
            ffn = part if ffn is None else ffn + part
        orf[...] = _ln(alpha * x + ffn, g_ref[...], b_ref[...])

    @pl.when(i < n_main)
    def _():
        run(x_ref, o_ref)

    @pl.when(i >= n_main)
    def _():
        run(xe_ref, oe_ref)


def _ffn_dense(x, x_extra, wg, wu, wd, g, b, alpha, tm, tf):
    t, d = x.shape
    te = x_extra.shape[0]
    dff = wg.shape[1]
    n_main, n_extra = t // tm, te // tm
    rows_in, rows_out = 128, 512
    assert t % tm == 0 and te % tm == 0 and dff % tf == 0 and d % rows_in == 0 and dff % rows_out == 0
    any_spec = pl.BlockSpec(memory_space=pl.ANY)
    main_blk = lambda i: (jnp.minimum(i, n_main - 1), 0)
    extra_blk = lambda i: (jnp.maximum(i - n_main, 0), 0)
    return pl.pallas_call(
        functools.partial(_ffn_dense_kernel, alpha, tf, n_main),
        grid=(n_main + n_extra,),
        in_specs=[pl.BlockSpec((tm, d), main_blk), pl.BlockSpec((tm, d), extra_blk),
                  any_spec, any_spec, any_spec,
                  pl.BlockSpec((1, d), lambda i: (0, 0)),
                  pl.BlockSpec((1, d), lambda i: (0, 0))],
        out_specs=(pl.BlockSpec((tm, d), main_blk), pl.BlockSpec((tm, d), extra_blk)),
        out_shape=(jax.ShapeDtypeStruct((t, d), F32), jax.ShapeDtypeStruct((te, d), F32)),
        scratch_shapes=[pltpu.VMEM(wg.shape, BF16), pltpu.VMEM(wu.shape, BF16), pltpu.VMEM(wd.shape, BF16),
                        pltpu.VMEM((3, rows_in, dff), F32), pltpu.VMEM((3, rows_out, d), F32),
                        pltpu.SemaphoreType.DMA((2, 3))],
        compiler_params=_cparams(("arbitrary",)),
        name="ffn_dense",
    )(x, x_extra, wg, wu, wd, g, b)


def _route_init(tm, cin_ref, carry_ref, before_ref):
    carry_ref[...] = cin_ref[...].astype(F32)
    row = lax.broadcasted_iota(I32, (tm, tm), 0)
    col = lax.broadcasted_iota(I32, (tm, tm), 1)
    before_ref[...] = jnp.where(row < col, 1.0, 0.0).astype(BF16)


def _route_block(tm, h, wcat_ref, mi_ref, mf_ref, cnt_ref, carry_ref, before_ref):
    h_hi = h.astype(BF16)
    h_lo = (h - h_hi.astype(F32)).astype(BF16)
    p_hi = _dot(h_hi, wcat_ref[...])
    p_lo = _dot(h_lo, wcat_ref[...])
    logits = (p_hi[:, 0:LANES] + (p_lo[:, 0:LANES] + p_hi[:, LANES:2 * LANES])) + p_lo[:, LANES:2 * LANES]
    lg = logits.T[0:N_EXPERTS, :]
    ex = lax.broadcasted_iota(I32, (N_EXPERTS, tm), 0).astype(F32)
    m1 = jnp.max(lg, axis=0, keepdims=True)
    i1 = jnp.min(jnp.where(lg == m1, ex, float(N_EXPERTS)), axis=0, keepdims=True)
    sel1 = ex == i1
    rest = jnp.where(sel1, -jnp.inf, lg)
    m2 = jnp.max(rest, axis=0, keepdims=True)
    i2 = jnp.min(jnp.where(rest == m2, ex, float(N_EXPERTS)), axis=0, keepdims=True)
    sel2 = ex == i2
    e2 = jnp.exp(m2 - m1)
    den = 1.0 + e2
    g1 = 1.0 / den
    g2 = e2 / den

    sel = jnp.where(sel1 | sel2, 1.0, 0.0)
    sel_pad = jnp.concatenate([sel, jnp.zeros_like(sel)], axis=0).astype(BF16)
    base = _dot(sel_pad, before_ref[...])[0:N_EXPERTS, :] + carry_ref[:, 0:1]
    r1 = jnp.sum(jnp.where(sel1, base, 0.0), axis=0, keepdims=True)
    r2 = jnp.sum(jnp.where(sel2, base, 0.0), axis=0, keepdims=True)
    carry_ref[...] = carry_ref[...] + jnp.sum(sel, axis=1, keepdims=True)

    meta = jnp.where(ex == 0, i1, jnp.where(ex == 1, i2, jnp.where(ex == 2, r1, r2)))
    mi_ref[...] = meta.astype(I32)
    gt = jnp.where(ex == 0, g1, jnp.where(ex == 1, g2, 0.0))
    gt = jnp.concatenate([gt, jnp.zeros((LANES - N_EXPERTS, tm), F32)], axis=0)
    mf_ref[...] = gt.T[:, 0:SUBLANES]
    cnt_ref[...] = carry_ref[...].astype(I32)


def _router_kernel(tm, h_ref, wcat_ref, cin_ref, mi_ref, mf_ref, cnt_ref, carry_ref, before_ref):
    @pl.when(pl.program_id(0) == 0)
    def _():
        _route_init(tm, cin_ref, carry_ref, before_ref)

    _route_block(tm, h_ref[...], wcat_ref, mi_ref, mf_ref, cnt_ref, carry_ref, before_ref)


def _router(h, wr_cat, counts_in):
    t, d = h.shape
    tm = min(TM_ROUTE, t)
    assert t % tm == 0 and N_EXPERTS == SUBLANES
    return pl.pallas_call(
        functools.partial(_router_kernel, tm),
        grid=(t // tm,),
        in_specs=[pl.BlockSpec((tm, d), lambda i: (i, 0)), _full_spec(wr_cat), _full_spec(counts_in)],
        out_specs=(pl.BlockSpec((SUBLANES, tm), lambda i: (0, i)),
                   pl.BlockSpec((tm, SUBLANES), lambda i: (i, 0)),
                   pl.BlockSpec((SUBLANES, LANES), lambda i: (0, 0))),
        out_shape=(jax.ShapeDtypeStruct((SUBLANES, t), I32),
                   jax.ShapeDtypeStruct((t, SUBLANES), F32),
                   jax.ShapeDtypeStruct((SUBLANES, LANES), I32)),
        scratch_shapes=[pltpu.VMEM((SUBLANES, LANES), F32), pltpu.VMEM((tm, tm), BF16)],
        compiler_params=_cparams(("arbitrary",)),
        name="router",
    )(h, wr_cat, counts_in)


def _row_copy(src, src_row, dst, dst_row, sem):
    return pltpu.make_async_copy(src.at[pl.ds(src_row, 1), :], dst.at[pl.ds(dst_row, 1), :], sem)


def _rows_wait(src, dst, n_rows, sem):
    pltpu.make_async_copy(src.at[pl.ds(0, n_rows), :], dst.at[pl.ds(0, n_rows), :], sem).wait()


def _dispatch_kernel(tm, tm_expert, n_blocks, blk_ranges, zi_ref, dest_ref, *rest):
    n_src = len(blk_ranges)
    h_refs = rest[:n_src]
    xs_ref, xbuf, zblk, lsems, rsems, zsem = rest[n_src:]
    i = pl.program_id(0)
    n = pl.num_programs(0)

    def block_load(g, start):
        slot = lax.rem(g, N_STAGE)
        for h_ref, (b0, b1) in zip(h_refs, blk_ranges):
            @pl.when((g >= b0) & (g < b1))
            def _(h_ref=h_ref, b0=b0):
                cp = pltpu.make_async_copy(h_ref.at[pl.ds((g - b0) * tm, tm), :], xbuf.at[slot],
                                           lsems.at[slot])
                if start:
                    cp.start()
                else:
                    cp.wait()

    def rows_wait(g):
        slot = lax.rem(g, N_STAGE)
        for _k in range(TOP_K):
            _rows_wait(xbuf.at[slot], xs_ref, tm, rsems.at[slot])

    @pl.when(i == 0)
    def _():
        block_load(i, True)

    @pl.when(i >= N_STAGE - 1)
    def _():
        rows_wait(i - (N_STAGE - 1))

    @pl.when(i + 1 < n)
    def _():
        block_load(i + 1, True)

    block_load(i, False)
    cur = lax.rem(i, N_STAGE)

    def issue(j, carry):
        r0 = pl.multiple_of(j * SUBLANES, SUBLANES)
        for u in range(SUBLANES):
            for k in range(TOP_K):
                dst = dest_ref[0, 0, j * SUBLANES + (k * tm + u)]
                _row_copy(xbuf.at[cur], r0 + u, xs_ref, dst, rsems.at[cur]).start(priority=k)
        return carry

    lax.fori_loop(0, tm // SUBLANES, issue, 0)

    @pl.when(i == n - 1)
    def _():
        for back in range(N_STAGE - 2, -1, -1):
            @pl.when(i >= back)
            def _(back=back):
                rows_wait(i - back)

    @pl.when(i == 0)
    def _():
        zblk[...] = jnp.zeros_like(zblk)
        for e in range(N_EXPERTS):
            lo = zi_ref[e] + zi_ref[N_EXPERTS + e]
            hi = zi_ref[e] + zi_ref[2 * N_EXPERTS + e]

            def zissue(r, carry):
                _row_copy(zblk, 0, xs_ref, r, zsem).start()
                return carry

            def zdrain(r, carry):
                _row_copy(zblk, 0, xs_ref, r, zsem).wait()
                return carry

            lax.fori_loop(lo, hi, zissue, 0)
            lax.fori_loop(lo, hi, zdrain, 0)

        def bcopy(b):
            return pltpu.make_async_copy(zblk, xs_ref.at[pl.ds(b * tm_expert, tm_expert), :], zsem)

        def bissue(b, carry):
            bcopy(b).start()
            return carry

        def bdrain(b, carry):
            bcopy(b).wait()
            return carry

        lax.fori_loop(zi_ref[3 * N_EXPERTS], n_blocks, bissue, 0)
        lax.fori_loop(zi_ref[3 * N_EXPERTS], n_blocks, bdrain, 0)


def _dispatch(hs_list, dest_blk, zinfo, n_blocks, tm_expert):
    d = hs_list[0].shape[1]
    nb = dest_blk.shape[0]
    tm = dest_blk.shape[2] // TOP_K
    blk_ranges, b0 = [], 0
    for h in hs_list:
        assert h.shape[0] % tm == 0 and h.shape[0] >= tm
        blk_ranges.append((b0, b0 + h.shape[0] // tm))
        b0 = blk_ranges[-1][1]
    assert b0 == nb
    any_spec = pl.BlockSpec(memory_space=pl.ANY)
    return pl.pallas_call(
        functools.partial(_dispatch_kernel, tm, tm_expert, n_blocks, tuple(blk_ranges)),
        grid_spec=pltpu.PrefetchScalarGridSpec(
            num_scalar_prefetch=1,
            grid=(nb,),
            in_specs=[pl.BlockSpec((1, 1, TOP_K * tm), lambda i, zi: (i, 0, 0), memory_space=pltpu.SMEM)]
            + [any_spec] * len(hs_list),
            out_specs=any_spec,
            scratch_shapes=[pltpu.VMEM((N_STAGE, tm, d), F32), pltpu.VMEM((tm_expert, d), F32),
                            pltpu.SemaphoreType.DMA((N_STAGE,)), pltpu.SemaphoreType.DMA((N_STAGE,)),
                            pltpu.SemaphoreType.DMA(())],
        ),
        out_shape=jax.ShapeDtypeStruct((n_blocks * tm_expert, d), F32),
        compiler_params=_cparams(("arbitrary",)),
        name="moe_dispatch",
    )(zinfo, dest_blk, *hs_list)


def _ffn_moe_kernel(tf, be_ref, nu_ref, x_ref, wg_hbm, wu_hbm, wd_hbm, y_ref,
                    wg_ref, wu_ref, wd_ref, stage_in, stage_out, sems):
    b = pl.program_id(0)
    nb = pl.num_programs(0)
    d, dff = wg_ref.shape
    n_slots = stage_in.shape[0]
    rows_in = stage_in.shape[1]
    rows_out = stage_out.shape[1]
    n_in, n_out = 2 * (d // rows_in), dff // rows_out
    e = be_ref[b]
    fresh = (b == 0) | (e != be_ref[jnp.maximum(b - 1, 0)])
    e_next = be_ref[jnp.minimum(b + 1, nb - 1)]

    def chunk_in(ex, n):
        src, dst = (wg_hbm, wg_ref) if n < n_in // 2 else (wu_hbm, wu_ref)
        rows = pl.ds((n % (n_in // 2)) * rows_in, rows_in)
        stage = stage_in.at[n % n_slots]
        return pltpu.make_async_copy(src.at[ex, rows, :], stage, sems.at[0, n % n_slots]), stage, dst.at[rows, :]

    def chunk_out(ex, n):
        rows = pl.ds(n * rows_out, rows_out)
        stage = stage_out.at[n % n_slots]
        return pltpu.make_async_copy(wd_hbm.at[ex, rows, :], stage, sems.at[1, n % n_slots]), stage, wd_ref.at[rows, :]

    def request_first(ex):
        for n in range(n_slots):
            chunk_in(ex, n)[0].start()
            chunk_out(ex, n)[0].start()

    @pl.when((b < nu_ref[0]) & fresh)
    def _():
        @pl.when(b == 0)
        def _():
            request_first(e)

        for n in range(n_in):
            for chunk, count in ((chunk_in, n_in), (chunk_out, n_out)):
                if n < count:
                    copy, staged, dst = chunk(e, n)
                    copy.wait()
                    dst[...] = staged[...].astype(BF16)
                    if n + n_slots < count:
                        chunk(e, n + n_slots)[0].start()

    @pl.when((b + 1 < nu_ref[0]) & (e_next != e))
    def _():
        request_first(e_next)

    @pl.when(b < nu_ref[0])
    def _():
        xb = x_ref[...].astype(BF16)
        ffn = None
        for c0 in range(0, dff, tf):
            mid = _silu(_dot(xb, wg_ref[:, c0:c0 + tf])) * _dot(xb, wu_ref[:, c0:c0 + tf])
            part = _dot(mid.astype(BF16), wd_ref[c0:c0 + tf, :])
            ffn = part if ffn is None else ffn + part
        y_ref[...] = ffn

    @pl.when(b >= nu_ref[0])
    def _():
        y_ref[...] = jnp.zeros_like(y_ref)


def _ffn_moe(xs, wg, wu, wd, block_e, n_used, tm, tf):
    cap, d = xs.shape
    dff = wg.shape[2]
    rows_in, rows_out = 128, 512
    assert cap % tm == 0 and dff % tf == 0 and d % rows_in == 0 and dff % rows_out == 0
    assert min(2 * (d // rows_in), dff // rows_out) >= N_WSTAGE
    any_spec = pl.BlockSpec(memory_space=pl.ANY)
    return pl.pallas_call(
        functools.partial(_ffn_moe_kernel, tf),
        grid_spec=pltpu.PrefetchScalarGridSpec(
            num_scalar_prefetch=2,
            grid=(cap // tm,),
            in_specs=[pl.BlockSpec((tm, d), lambda b, be, nu: (jnp.minimum(b, nu[0] - 1), 0)),
                      any_spec, any_spec, any_spec],
            out_specs=pl.BlockSpec((tm, d), lambda b, be, nu: (b, 0)),
            scratch_shapes=[pltpu.VMEM((d, dff), BF16), pltpu.VMEM((d, dff), BF16), pltpu.VMEM((dff, d), BF16),
                            pltpu.VMEM((N_WSTAGE, rows_in, dff), F32), pltpu.VMEM((N_WSTAGE, rows_out, d), F32),
                            pltpu.SemaphoreType.DMA((2, N_WSTAGE))],
        ),
        out_shape=jax.ShapeDtypeStruct((cap, d), F32),
        compiler_params=_cparams(("arbitrary",)),
        name="ffn_moe",
    )(block_e, n_used, xs, wg, wu, wd)


def _combine_kernel(alpha, tm, dcur_ref, dnext_ref, h_ref, gate_ref, g_ref, b_ref, ys_ref, o_ref, ybuf, sems):
    i = pl.program_id(0)
    n = pl.num_programs(0)

    def issue(dref, slot):
        def body(j, carry):
            r0 = pl.multiple_of(j * SUBLANES, SUBLANES)
            for u in range(SUBLANES):
                for k in range(TOP_K):
                    src = dref[0, 0, j * SUBLANES + (k * tm + u)]
                    _row_copy(ys_ref, src, ybuf.at[slot, k], r0 + u, sems.at[slot]).start(priority=k)
            return carry

        lax.fori_loop(0, tm // SUBLANES, body, 0)

    @pl.when(i == 0)
    def _():
        issue(dcur_ref, 0)

    @pl.when(i + 1 < n)
    def _():
        issue(dnext_ref, (i + 1) % 2)

    slot = i % 2
    for k in range(TOP_K):
        _rows_wait(ys_ref, ybuf.at[slot, k], tm, sems.at[slot])

    gates = gate_ref[...]
    ffn = gates[:, 0:1] * ybuf[slot, 0] + gates[:, 1:2] * ybuf[slot, 1]
    o_ref[...] = _ln(alpha * h_ref[...] + ffn, g_ref[...], b_ref[...])


def _combine(h, dest_blk, gates, ys, g, b, alpha):
    t, d = h.shape
    nb = dest_blk.shape[0]
    tm = dest_blk.shape[2] // TOP_K
    dest_spec = lambda imap: pl.BlockSpec((1, 1, TOP_K * tm), imap, memory_space=pltpu.SMEM)
    return pl.pallas_call(
        functools.partial(_combine_kernel, alpha, tm),
        grid=(nb,),
        in_specs=[dest_spec(lambda i: (i, 0, 0)),
                  dest_spec(lambda i: (jnp.minimum(i + 1, nb - 1), 0, 0)),
                  pl.BlockSpec((tm, d), lambda i: (i, 0)),
                  pl.BlockSpec((tm, SUBLANES), lambda i: (i, 0)),
                  pl.BlockSpec((1, d), lambda i: (0, 0)),
                  pl.BlockSpec((1, d), lambda i: (0, 0)),
                  pl.BlockSpec(memory_space=pl.ANY)],
        out_specs=pl.BlockSpec((tm, d), lambda i: (i, 0)),
        out_shape=jax.ShapeDtypeStruct((t, d), F32),
        scratch_shapes=[pltpu.VMEM((2, TOP_K, tm, d), F32), pltpu.SemaphoreType.DMA((2,))],
        compiler_params=_cparams(("arbitrary",)),
        name="moe_combine",
    )(dest_blk, dest_blk, h, gates, g, b, ys)


def _ffn_routed(hs_list, first_routed, mw, g, b, alpha, tm_expert):
    routed = [first_routed[:2]]
    counts_in = first_routed[2]
    for h in hs_list[1:]:
        meta_i, gates, counts_in = _router(h, mw['w_router_cat'], counts_in)
        routed.append((meta_i, gates))
    counts = counts_in[:N_EXPERTS, 0]
    padded = (counts + tm_expert - 1) // tm_expert * tm_expert
    pad_end = jnp.cumsum(padded)
    pad_start = (pad_end - padded).astype(I32)
    n_assign = sum(h.shape[0] for h in hs_list) * TOP_K
    n_blocks = -(-n_assign // tm_expert) + N_EXPERTS
    cap = n_blocks * tm_expert
    n_used = (pad_end[-1] // tm_expert).astype(I32)
    blk_start = jnp.minimum(jnp.arange(n_blocks, dtype=I32), n_used - 1) * tm_expert
    block_e = jnp.minimum(jnp.sum(blk_start[:, None] >= pad_end[None, :], axis=1), N_EXPERTS - 1).astype(I32)
    zinfo = jnp.concatenate([pad_start, counts, padded, n_used.reshape(1)]).astype(I32)

    dests = []
    for h, (meta_i, _) in zip(hs_list, routed):
        dest = meta_i[TOP_K:2 * TOP_K, :]
        for e in range(N_EXPERTS):
            dest = dest + jnp.where(meta_i[0:TOP_K, :] == e, pad_start[e], 0)
        nb = h.shape[0] // TM_MOVE
        dests.append(dest.reshape(TOP_K, nb, TM_MOVE).transpose(1, 0, 2).reshape(nb, 1, TOP_K * TM_MOVE))
    xs = _dispatch(hs_list, jnp.concatenate(dests, axis=0), zinfo, n_blocks, tm_expert)
    ys = _ffn_moe(xs, mw['w_e_gate'], mw['w_e_up'], mw['w_e_down'], block_e, n_used.reshape(1), tm_expert,
                  TF_EXPERT)
    return [_combine(h, dest_blk, gates, ys, g, b, alpha)
            for h, dest_blk, (_, gates) in zip(hs_list, dests, routed)]


def _prep_layer(i, p, n_step):
    dq = p['conv_a_w'].shape[2]
    hd = dq // N_HEADS_B
    tril = jnp.tril(jnp.ones((CHUNK, CHUNK), dtype=bool))
    w_s = p['w_s'][i]
    b_s = p['b_s'][i]
    wp = p['w_pool'][i]
    ng, gc = wp.shape[0], wp.shape[1]
    w_pool_bd = (jnp.eye(ng, dtype=F32)[:, None, :, None] * wp[:, :, None, :]).reshape(ng * gc, ng * gc)
    row = lambda a: a.reshape(1, -1)
    return dict(
        ln_in_g=row(p['ln_in_g']), ln_in_b=row(p['ln_in_b']),
        w_in=p['w_in'][i].astype(BF16),
        conv_a_w=p['conv_a_w'][i], conv_a_b=row(p['conv_a_b'][i]),
        ln_a_g=row(p['ln_a_g'][i]), ln_a_b=row(p['ln_a_b'][i]),
        ln_v_g=row(p['ln_v_g'][i]), ln_v_b=row(p['ln_v_b'][i]),
        ws_tril=jnp.where(tril[None], w_s, 0).astype(BF16),
        bs_mat=jnp.repeat(b_s.T, hd, axis=1),
        ws_vec=jnp.repeat(jnp.where(tril[None], w_s, 0)[:, :n_step, :n_step].transpose(1, 2, 0)
                          .reshape(n_step * n_step, N_HEADS_B), hd, axis=1),
        bs_vec=jnp.repeat(b_s[:, :n_step].T, hd, axis=1),
        conv_c_w=p['conv_c_w'][i],
        w_pool_bd=w_pool_bd.astype(BF16), pool_scale=row(p['pool_scale'][i]),
        w_out=p['w_out'][i].astype(BF16),
        ln_mix_g=row(p['ln_mix_g'][i]), ln_mix_b=row(p['ln_mix_b'][i]),
    )


def kernel(x_prompt, x_sample, state_conv_a, state_conv_c, state_pool_d, ln_in_g, ln_in_b, w_in, conv_a_w,
           conv_a_b, ln_a_g, ln_a_b, ln_v_g, ln_v_b, w_s, b_s, conv_c_w, w_pool, pool_scale, w_out,
           ln_mix_g, ln_mix_b, w_ff_gate, w_ff_up, w_ff_down, w_router, w_e_gate, w_e_up, w_e_down,
           ln_ffn_g, ln_ffn_b):
    p = dict(ln_in_g=ln_in_g, ln_in_b=ln_in_b, w_in=w_in, conv_a_w=conv_a_w, conv_a_b=conv_a_b,
             ln_a_g=ln_a_g, ln_a_b=ln_a_b, ln_v_g=ln_v_g, ln_v_b=ln_v_b, w_s=w_s, b_s=b_s,
             conv_c_w=conv_c_w, w_pool=w_pool, pool_scale=pool_scale, w_out=w_out,
             ln_mix_g=ln_mix_g, ln_mix_b=ln_mix_b)
    depth = w_in.shape[0]
    bsz, seq, d = x_prompt.shape
    n_seq, n_step, _ = x_sample.shape
    dq = conv_a_w.shape[2]
    alpha = float((2.0 * depth) ** 0.25)

    hp = x_prompt
    hs = x_sample.transpose(1, 0, 2).reshape(n_step * n_seq, d)
    outs = {k: [] for k in ('pa', 'pc', 'pd', 'pv', 'sa', 'sc', 'sd', 'sv')}
    for i in range(depth):
        lw = _prep_layer(i, p, n_step)
        j = i // 2
        routed = i % 2 == 1
        if routed:
            wr = jnp.pad(w_router[j], ((0, 0), (0, LANES - N_EXPERTS)))
            wr_hi = wr.astype(BF16)
            wr_lo = (wr - wr_hi.astype(F32)).astype(BF16)
            wr_cat = jnp.concatenate([wr_hi, wr_lo], axis=1)
        hp, pa, pc, pd, pv, *route_p = _mixer_prompt(hp, lw, i == 0, alpha, wr_cat if routed else None)
        tm_state = lambda s: s.transpose(1, 0, 2).reshape(-1, dq)
        hs, sa, sc, sd, sv = _mixer_sample(hs, tm_state(state_conv_a[i]), tm_state(state_conv_c[i]),
                                           tm_state(state_pool_d[i]), lw, i == 0, alpha, n_seq, n_step)
        lg, lb = ln_ffn_g[i].reshape(1, -1), ln_ffn_b[i].reshape(1, -1)
        if not routed:
            hp, hs = _ffn_dense(hp.reshape(bsz * seq, d), hs, w_ff_gate[j], w_ff_up[j], w_ff_down[j], lg, lb,
                                alpha, TM_DENSE, TF_DENSE)
            hp = hp.reshape(bsz, seq, d)
        else:
            mw = dict(w_router_cat=wr_cat, w_e_gate=w_e_gate[j], w_e_up=w_e_up[j], w_e_down=w_e_down[j])
            hp, hs = _ffn_routed([hp.reshape(bsz * seq, d), hs], route_p, mw, lg, lb, alpha, TM_EXPERT)
            hp = hp.reshape(bsz, seq, d)
        back = lambda a: a.reshape(-1, n_seq, dq).transpose(1, 0, 2)
        for k, val in zip(('pa', 'pc', 'pd', 'pv', 'sa', 'sc', 'sd', 'sv'),
                          (pa, pc, pd, pv, back(sa), back(sc), back(sd), back(sv))):
            outs[k].append(val)
    y_sample = hs.reshape(n_step, n_seq, d).transpose(1, 0, 2)
    st = lambda k: jnp.stack(outs[k])
    return (hp, y_sample, st('pa'), st('pc'), st('pd'), st('pv'), st('sa'), st('sc'), st('sd'), st('sv'))
```

```python
import functools

import numpy as np
import jax
import jax.numpy as jnp
from jax import lax
from jax.experimental import pallas as pl
from jax.experimental.pallas import tpu as pltpu

F32 = jnp.float32
BF16 = jnp.bfloat16
I32 = jnp.int32

PAST_LEN = 16384
CHUNK = 128
N_HEADS_B = 4
POOL_WINDOWS = (2, 4, 8, 16)
POOL_PAST = 15
CONV_A_W = 31
CONV_C_W = 3
N_EXPERTS = 8
TOP_K = 2
LN_EPS = 1e-5
SQRT_HALF = float(np.sqrt(0.5).astype(np.float32))

LANES = 128
SUBLANES = 8
VMEM_LIMIT_BYTES = 56 * 1024 * 1024

HIST_A = 32
HIST_C = 8
HIST_D = 24

TL_PROMPT = 512
TM_DENSE = 512
TF_DENSE = 1792
TF_EXPERT = 1792
TM_ROUTE = 512
TM_MOVE = 256
TM_EXPERT = 512
N_STAGE = 3
N_WSTAGE = 5


def _ln(x, g, b):
    mu = jnp.mean(x, axis=-1, keepdims=True)
    xc = x - mu
    var = jnp.mean(xc * xc, axis=-1, keepdims=True)
    return xc * lax.rsqrt(var + LN_EPS) * g + b


def _gelu(x):
    return 0.5 * x * (1.0 + lax.erf(x * SQRT_HALF))


def _silu(x):
    return x * jax.nn.sigmoid(x)


def _dot(a, b):
    return jnp.dot(a, b, preferred_element_type=F32)


def _cparams(sem):
    return pltpu.CompilerParams(dimension_semantics=sem, vmem_limit_bytes=VMEM_LIMIT_BYTES)


def _mixer_prompt_kernel(first_layer, route, alpha, tl, d_mix,
                         x_ref, lng_ref, lnb_ref, win_ref, caw_ref, cab_ref, lag_ref, lab_ref,
                         lvg_ref, lvb_ref, ws_ref, bsm_ref, ccw_ref, wpool_ref, pscale_ref,
                         wout_ref, lmg_ref, lmb_ref, *refs):
    if route:
        wcat_ref, cin_ref = refs[:2]
        refs = refs[2:]
    h_out, sa_out, sc_out, sd_out, v_out = refs[:5]
    refs = refs[5:]
    if route:
        mi_ref, mf_ref, cnt_ref = refs[:3]
        refs = refs[3:]
    ext_a, ext_c, ext_d, cat_ref, z_ref, hb_ref, mix_ref = refs[:7]
    dq = d_mix // 4
    j = pl.program_id(1)
    nj = pl.num_programs(1)
    if route:
        carry_ref, before_ref = refs[7:]

        @pl.when((pl.program_id(0) == 0) & (j == 0))
        def _():
            _route_init(tl, cin_ref, carry_ref, before_ref)

    @pl.when(j == 0)
    def _():
        ext_a[0, 0:HIST_A, :] = jnp.zeros((HIST_A, dq), F32)
        ext_c[0:HIST_C, :] = jnp.zeros((HIST_C, dq), F32)
        ext_d[0:HIST_D, :] = jnp.zeros((HIST_D, dq), F32)

    d = x_ref.shape[2]
    rb = CHUNK

    def resid(r0, n):
        return h_out[0, r0:r0 + n, :] if first_layer else x_ref[0, r0:r0 + n, :]

    for r0 in range(0, tl, rb):
        x = x_ref[0, r0:r0 + rb, :]
        if first_layer:
            x = _ln(x, lng_ref[...], lnb_ref[...])
            h_out[0, r0:r0 + rb, :] = x
        hb_ref[r0:r0 + rb, :] = x.astype(BF16)

    z_ref[...] = _dot(hb_ref[...], win_ref[...])

    def zcol(r0, i, n=rb):
        return z_ref[r0:r0 + n, i * dq:(i + 1) * dq]

    chunk_rows = range(0, tl, CHUNK)

    for r0 in chunk_rows:
        ext_a[0, HIST_A + r0:HIST_A + r0 + rb, :] = zcol(r0, 0) * jax.nn.sigmoid(zcol(r0, 1))
    n_ext = HIST_A + tl
    for r0 in range(0, n_ext - SUBLANES, rb):
        n = min(rb, n_ext - SUBLANES - r0)
        blk = ext_a[0, r0:r0 + n + SUBLANES, :]
        for s in range(1, SUBLANES):
            ext_a[s, r0:r0 + n, :] = pltpu.roll(blk, n + SUBLANES - s, axis=0)[0:n, :]
    off_a = HIST_A - (CONV_A_W - 1)
    for q0 in range(0, tl, 64):
        acc = jnp.zeros((64, dq), F32)
        for k in range(CONV_A_W):
            s = (off_a + k) % SUBLANES
            row = off_a + k - s + q0
            acc = acc + ext_a[s, row:row + 64, :] * caw_ref[k:k + 1, :]
        y = _silu(_ln(acc + cab_ref[...], lag_ref[...], lab_ref[...]))
        cat_ref[q0:q0 + 64, 0:dq] = y.astype(BF16)

    hd = dq // N_HEADS_B
    lane_head = lax.broadcasted_iota(I32, (CHUNK, dq), 1) // hd
    for r0 in chunk_rows:
        v = _ln(_gelu(zcol(r0, 3)), lvg_ref[...], lvb_ref[...])
        sm = bsm_ref[...]
        for hh in range(N_HEADS_B):
            vm = jnp.where(lane_head == hh, v, 0.0).astype(BF16)
            sm = sm + _dot(ws_ref[hh], vm)
        cat_ref[r0:r0 + CHUNK, dq:2 * dq] = (_gelu(zcol(r0, 2)) * sm).astype(BF16)
        if r0 == tl - CHUNK:
            @pl.when(j == nj - 1)
            def _(v=v):
                v_out[0] = v

    off_c = HIST_C - (CONV_C_W - 1)
    for r0 in chunk_rows:
        ext_c[HIST_C + r0:HIST_C + r0 + rb, :] = zcol(r0, 6) * zcol(r0, 4)
    for r0 in chunk_rows:
        conv_c = jnp.zeros((CHUNK, dq), F32)
        for k in range(CONV_C_W):
            conv_c = conv_c + ext_c[off_c + r0 + k:off_c + r0 + k + CHUNK, :] * ccw_ref[k:k + 1, :]
        cat_ref[r0:r0 + CHUNK, 2 * dq:3 * dq] = (zcol(r0, 5) * conv_c).astype(BF16)

    gc = dq // len(POOL_WINDOWS)
    lane_grp = lax.broadcasted_iota(I32, (CHUNK, dq), 1) // gc
    lead = 2 * SUBLANES
    for r0 in chunk_rows:
        ext_d[HIST_D + r0:HIST_D + r0 + rb, :] = zcol(r0, 7)
    for r0 in chunk_rows:
        e = ext_d[HIST_D + r0 - lead:HIST_D + r0 + CHUNK, :]
        s2 = e + pltpu.roll(e, 1, axis=0)
        s4 = s2 + pltpu.roll(s2, 2, axis=0)
        s8 = s4 + pltpu.roll(s4, 4, axis=0)
        s16 = s8 + pltpu.roll(s8, 8, axis=0)
        sums = (s2, s4, s8, s16)
        pos = (j * tl + r0 + lax.broadcasted_iota(I32, (CHUNK, 1), 0)).astype(F32)
        mean = jnp.zeros((CHUNK, dq), F32)
        for g, w in enumerate(POOL_WINDOWS):
            inv = 1.0 / jnp.minimum(pos + 1.0, float(w))
            mean = jnp.where(lane_grp == g, sums[g][lead:lead + CHUNK, :] * inv, mean)
        dd = (mean - e[lead:lead + CHUNK, :]).astype(BF16)
        cat_ref[r0:r0 + CHUNK, 3 * dq:4 * dq] = (_dot(dd, wpool_ref[...]) * pscale_ref[...]).astype(BF16)

    mix_ref[...] = _dot(cat_ref[...], wout_ref[...])
    for r0 in range(0, tl, 64):
        h_out[0, r0:r0 + 64, :] = _ln(alpha * resid(r0, 64) + mix_ref[r0:r0 + 64, :], lmg_ref[...], lmb_ref[...])
    if route:
        _route_block(tl, h_out[0], wcat_ref, mi_ref, mf_ref, cnt_ref, carry_ref, before_ref)

    @pl.when(j == nj - 1)
    def _():
        sa_out[0] = ext_a[0, HIST_A + tl - (CONV_A_W - 1):HIST_A + tl, :]
        sc_out[0] = ext_c[HIST_C + tl - (CONV_C_W - 1):HIST_C + tl, :]
        sd_out[0] = ext_d[HIST_D + tl - POOL_PAST:HIST_D + tl, :]

    ext_a[0, 0:HIST_A, :] = ext_a[0, tl:tl + HIST_A, :]
    ext_c[0:HIST_C, :] = ext_c[tl:tl + HIST_C, :]
    ext_d[0:HIST_D, :] = ext_d[tl:tl + HIST_D, :]


def _full_spec(arr):
    nd = arr.ndim
    return pl.BlockSpec(arr.shape, lambda *_: (0,) * nd)


def _mixer_prompt(x, lw, first_layer, alpha, route_w=None):
    bsz, seq, d = x.shape
    d_mix = lw['w_out'].shape[0]
    dq = d_mix // 4
    tl = TL_PROMPT
    nj = seq // tl
    route = route_w is not None
    assert seq % tl == 0 and tl % CHUNK == 0 and seq >= CHUNK
    params = [lw['ln_in_g'], lw['ln_in_b'], lw['w_in'], lw['conv_a_w'], lw['conv_a_b'], lw['ln_a_g'],
              lw['ln_a_b'], lw['ln_v_g'], lw['ln_v_b'], lw['ws_tril'], lw['bs_mat'], lw['conv_c_w'],
              lw['w_pool_bd'], lw['pool_scale'], lw['w_out'], lw['ln_mix_g'], lw['ln_mix_b']]
    out_shape = [
        jax.ShapeDtypeStruct((bsz, seq, d), F32),
        jax.ShapeDtypeStruct((bsz, CONV_A_W - 1, dq), F32),
        jax.ShapeDtypeStruct((bsz, CONV_C_W - 1, dq), F32),
        jax.ShapeDtypeStruct((bsz, POOL_PAST, dq), F32),
        jax.ShapeDtypeStruct((bsz, CHUNK, dq), F32),
    ]
    state_spec = lambda rows: pl.BlockSpec((1, rows, dq), lambda b, j: (b, 0, 0))
    out_specs = [pl.BlockSpec((1, tl, d), lambda b, j: (b, j, 0)),
                 state_spec(CONV_A_W - 1), state_spec(CONV_C_W - 1), state_spec(POOL_PAST), state_spec(CHUNK)]
    scratch = [pltpu.VMEM((SUBLANES, HIST_A + tl, dq), F32), pltpu.VMEM((HIST_C + tl, dq), F32),
               pltpu.VMEM((HIST_D + tl, dq), F32), pltpu.VMEM((tl, d_mix), BF16),
               pltpu.VMEM((tl, lw['w_in'].shape[1]), F32), pltpu.VMEM((tl, d), BF16),
               pltpu.VMEM((tl, d), F32)]
    if route:
        assert N_EXPERTS == SUBLANES
        params += [route_w, jnp.zeros((SUBLANES, LANES), I32)]
        out_shape += [jax.ShapeDtypeStruct((SUBLANES, bsz * seq), I32),
                      jax.ShapeDtypeStruct((bsz * seq, SUBLANES), F32),
                      jax.ShapeDtypeStruct((SUBLANES, LANES), I32)]
        out_specs += [pl.BlockSpec((SUBLANES, tl), lambda b, j: (0, b * nj + j)),
                      pl.BlockSpec((tl, SUBLANES), lambda b, j: (b * nj + j, 0)),
                      pl.BlockSpec((SUBLANES, LANES), lambda b, j: (0, 0))]
        scratch += [pltpu.VMEM((SUBLANES, LANES), F32), pltpu.VMEM((tl, tl), BF16)]
    return pl.pallas_call(
        functools.partial(_mixer_prompt_kernel, first_layer, route, alpha, tl, d_mix),
        grid=(bsz, nj),
        in_specs=[pl.BlockSpec((1, tl, d), lambda b, j: (b, j, 0))] + [_full_spec(p) for p in params],
        out_specs=tuple(out_specs),
        out_shape=tuple(out_shape),
        scratch_shapes=scratch,
        compiler_params=_cparams(("arbitrary", "arbitrary")),
        name="mixer_prompt",
    )(x, *params)


def _mixer_sample_kernel(first_layer, alpha, n_seq, n_step, d_mix,
                         x_ref, sa_ref, sc_ref, sd_ref, lng_ref, lnb_ref, win_ref, caw_ref, cab_ref,
                         lag_ref, lab_ref, lvg_ref, lvb_ref, wsv_ref, bsv_ref, ccw_ref, wpool_ref,
                         pscale_ref, wout_ref, lmg_ref, lmb_ref,
                         h_out, sa_out, sc_out, sd_out, v_out, cat_ref):
    dq = d_mix // 4
    x = x_ref[...]
    h = _ln(x, lng_ref[...], lnb_ref[...]) if first_layer else x
    hb = h.astype(BF16)

    def proj(i):
        return _dot(hb, win_ref[:, i * dq:(i + 1) * dq])

    def slab(val, i):
        return val[i * n_seq:(i + 1) * n_seq, :]

    def ext_slabs(state_ref, n_past, cur):
        return ([state_ref[i * n_seq:(i + 1) * n_seq, :] for i in range(n_past)]
                + [slab(cur, i) for i in range(n_step)])

    def store_state(out_ref, slabs, n_keep):
        for i, sl in enumerate(slabs[len(slabs) - n_keep:]):
            out_ref[i * n_seq:(i + 1) * n_seq, :] = sl

    a_glu = proj(0) * jax.nn.sigmoid(proj(1))
    ea = ext_slabs(sa_ref, CONV_A_W - 1, a_glu)
    for l in range(n_step):
        acc = jnp.zeros((n_seq, dq), F32)
        for k in range(CONV_A_W):
            acc = acc + ea[l + k] * caw_ref[k:k + 1, :]
        y = _silu(_ln(acc + cab_ref[...], lag_ref[...], lab_ref[...]))
        cat_ref[l * n_seq:(l + 1) * n_seq, 0:dq] = y.astype(BF16)
    store_state(sa_out, ea, CONV_A_W - 1)

    u = _gelu(proj(2))
    v = _ln(_gelu(proj(3)), lvg_ref[...], lvb_ref[...])
    v_out[...] = v
    for l in range(n_step):
        s = jnp.zeros((n_seq, dq), F32) + bsv_ref[l:l + 1, :]
        for m in range(l + 1):
            s = s + slab(v, m) * wsv_ref[l * n_step + m:l * n_step + m + 1, :]
        cat_ref[l * n_seq:(l + 1) * n_seq, dq:2 * dq] = (slab(u, l) * s).astype(BF16)

    c_x = proj(4)
    c_b = proj(5)
    c_c = proj(6)
    gx = c_c * c_x
    ec = ext_slabs(sc_ref, CONV_C_W - 1, gx)
    for l in range(n_step):
        acc = jnp.zeros((n_seq, dq), F32)
        for k in range(CONV_C_W):
            acc = acc + ec[l + k] * ccw_ref[k:k + 1, :]
        cat_ref[l * n_seq:(l + 1) * n_seq, 2 * dq:3 * dq] = (slab(c_b, l) * acc).astype(BF16)
    store_state(sc_out, ec, CONV_C_W - 1)

    d_in = proj(7)
    ed = ext_slabs(sd_ref, POOL_PAST, d_in)
    memo = {}

    def wsum(i, w):
        if i < 0:
            return None
        if w == 1:
            return ed[i]
        if (i, w) not in memo:
            a, b = wsum(i, w // 2), wsum(i - w // 2, w // 2)
            memo[(i, w)] = a if b is None else a + b
        return memo[(i, w)]

    gc = dq // len(POOL_WINDOWS)
    lane_grp = lax.broadcasted_iota(I32, (n_seq, dq), 1) // gc
    for l in range(n_step):
        mean = jnp.zeros((n_seq, dq), F32)
        for g, w in enumerate(POOL_WINDOWS):
            count = min(PAST_LEN + l + 1, w)
            mean = jnp.where(lane_grp == g, wsum(POOL_PAST + l, w) * (1.0 / count), mean)
        dd = (mean - slab(d_in, l)).astype(BF16)
        cat_ref[l * n_seq:(l + 1) * n_seq, 3 * dq:4 * dq] = (
            _dot(dd, wpool_ref[...]) * pscale_ref[...]).astype(BF16)
    store_state(sd_out, ed, POOL_PAST)

    mix = _dot(cat_ref[...], wout_ref[...])
    h_out[...] = _ln(alpha * h + mix, lmg_ref[...], lmb_ref[...])


def _mixer_sample(x_tm, sa, sc, sd, lw, first_layer, alpha, n_seq, n_step):
    rows, d = x_tm.shape
    d_mix = lw['w_out'].shape[0]
    dq = d_mix // 4
    assert n_seq % SUBLANES == 0 and n_step <= CHUNK and PAST_LEN % CHUNK == 0
    ins = [x_tm, sa, sc, sd, lw['ln_in_g'], lw['ln_in_b'], lw['w_in'], lw['conv_a_w'], lw['conv_a_b'],
           lw['ln_a_g'], lw['ln_a_b'], lw['ln_v_g'], lw['ln_v_b'], lw['ws_vec'], lw['bs_vec'],
           lw['conv_c_w'], lw['w_pool_bd'], lw['pool_scale'], lw['w_out'], lw['ln_mix_g'], lw['ln_mix_b']]
    out_shape = (
        jax.ShapeDtypeStruct((rows, d), F32),
        jax.ShapeDtypeStruct(((CONV_A_W - 1) * n_seq, dq), F32),
        jax.ShapeDtypeStruct(((CONV_C_W - 1) * n_seq, dq), F32),
        jax.ShapeDtypeStruct((POOL_PAST * n_seq, dq), F32),
        jax.ShapeDtypeStruct((rows, dq), F32),
    )
    return pl.pallas_call(
        functools.partial(_mixer_sample_kernel, first_layer, alpha, n_seq, n_step, d_mix),
        grid=(1,),
        in_specs=[_full_spec(a) for a in ins],
        out_specs=tuple(pl.BlockSpec(s.shape, lambda i: (0, 0)) for s in out_shape),
        out_shape=out_shape,
        scratch_shapes=[pltpu.VMEM((rows, d_mix), BF16)],
        compiler_params=_cparams(("arbitrary",)),
        name="mixer_sample",
    )(*ins)


def _stream_cast(streams):
    plans = []
    for pairs, stage, sems in streams:
        rows = stage.shape[1]
        plans.append(([(src, dst, r0) for src, dst in pairs for r0 in range(0, src.shape[0], rows)],
                      stage, sems))

    def copy(plan, c):
        chunks, stage, sems = plan
        src, _, r0 = chunks[c]
        slot = c % stage.shape[0]
        return pltpu.make_async_copy(src.at[pl.ds(r0, stage.shape[1]), :], stage.at[slot], sems.at[slot])

    for plan in plans:
        for c in range(min(plan[1].shape[0], len(plan[0]))):
            copy(plan, c).start()
    for c in range(max(len(plan[0]) for plan in plans)):
        for plan in plans:
            chunks, stage, _ = plan
            if c < len(chunks):
                _, dst, r0 = chunks[c]
                copy(plan, c).wait()
                dst[pl.ds(r0, stage.shape[1]), :] = stage[c % stage.shape[0]].astype(BF16)
                if c + stage.shape[0] < len(chunks):
                    copy(plan, c + stage.shape[0]).start()


def _ffn_dense_kernel(alpha, tf, n_main, x_ref, xe_ref, wg_hbm, wu_hbm, wd_hbm, g_ref, b_ref, o_ref, oe_ref,
                      wg_ref, wu_ref, wd_ref, stage_in, stage_out, sems):
    i = pl.program_id(0)

    @pl.when(i == 0)
    def _():
        _stream_cast([([(wg_hbm, wg_ref), (wu_hbm, wu_ref)], stage_in, sems.at[0]),
                      ([(wd_hbm, wd_ref)], stage_out, sems.at[1])])

    def run(xr, orf):
        x = xr[...]
        xb = x.astype(BF16)
        ffn = None
        for c0 in range(0, wg_ref.shape[1], tf):
            mid = _silu(_dot(xb, wg_ref[:, c0:c0 + tf])) * _dot(xb, wu_ref[:, c0:c0 + tf])
            part = _dot(mid.astype(BF16), wd_ref[c0:c0 + tf, :])
            ffn = part if ffn is None else ffn + part
        orf[...] = _ln(alpha * x + ffn, g_ref[...], b_ref[...])

    @pl.when(i < n_main)
    def _():
        run(x_ref, o_ref)

    @pl.when(i >= n_main)
    def _():
        run(xe_ref, oe_ref)


def _ffn_dense(x, x_extra, wg, wu, wd, g, b, alpha, tm, tf):
    t, d = x.shape
    te = x_extra.shape[0]
    dff = wg.shape[1]
    n_main, n_extra = t // tm, te // tm
    rows_in, rows_out = 128, 512
    assert t % tm == 0 and te % tm == 0 and dff % tf == 0 and d % rows_in == 0 and dff % rows_out == 0
    any_spec = pl.BlockSpec(memory_space=pl.ANY)
    main_blk = lambda i: (jnp.minimum(i, n_main - 1), 0)
    extra_blk = lambda i: (jnp.maximum(i - n_main, 0), 0)
    return pl.pallas_call(
        functools.partial(_ffn_dense_kernel, alpha, tf, n_main),
        grid=(n_main + n_extra,),
        in_specs=[pl.BlockSpec((tm, d), main_blk), pl.BlockSpec((tm, d), extra_blk),
                  any_spec, any_spec, any_spec,
                  pl.BlockSpec((1, d), lambda i: (0, 0)),
                  pl.BlockSpec((1, d), lambda i: (0, 0))],
        out_specs=(pl.BlockSpec((tm, d), main_blk), pl.BlockSpec((tm, d), extra_blk)),
        out_shape=(jax.ShapeDtypeStruct((t, d), F32), jax.ShapeDtypeStruct((te, d), F32)),
        scratch_shapes=[pltpu.VMEM(wg.shape, BF16), pltpu.VMEM(wu.shape, BF16), pltpu.VMEM(wd.shape, BF16),
                        pltpu.VMEM((3, rows_in, dff), F32), pltpu.VMEM((3, rows_out, d), F32),
                        pltpu.SemaphoreType.DMA((2, 3))],
        compiler_params=_cparams(("arbitrary",)),
        name="ffn_dense",
    )(x, x_extra, wg, wu, wd, g, b)


def _route_init(tm, cin_ref, carry_ref, before_ref):
    carry_ref[...] = cin_ref[...].astype(F32)
    row = lax.broadcasted_iota(I32, (tm, tm), 0)
    col = lax.broadcasted_iota(I32, (tm, tm), 1)
    before_ref[...] = jnp.where(row < col, 1.0, 0.0).astype(BF16)


def _route_block(tm, h, wcat_ref, mi_ref, mf_ref, cnt_ref, carry_ref, before_ref):
    h_hi = h.astype(BF16)
    h_lo = (h - h_hi.astype(F32)).astype(BF16)
    p_hi = _dot(h_hi, wcat_ref[...])
    p_lo = _dot(h_lo, wcat_ref[...])
    logits = (p_hi[:, 0:LANES] + (p_lo[:, 0:LANES] + p_hi[:, LANES:2 * LANES])) + p_lo[:, LANES:2 * LANES]
    lg = logits.T[0:N_EXPERTS, :]
    ex = lax.broadcasted_iota(I32, (N_EXPERTS, tm), 0).astype(F32)
    m1 = jnp.max(lg, axis=0, keepdims=True)
    i1 = jnp.min(jnp.where(lg == m1, ex, float(N_EXPERTS)), axis=0, keepdims=True)
    sel1 = ex == i1
    rest = jnp.where(sel1, -jnp.inf, lg)
    m2 = jnp.max(rest, axis=0, keepdims=True)
    i2 = jnp.min(jnp.where(rest == m2, ex, float(N_EXPERTS)), axis=0, keepdims=True)
    sel2 = ex == i2
    e2 = jnp.exp(m2 - m1)
    den = 1.0 + e2
    g1 = 1.0 / den
    g2 = e2 / den

    sel = jnp.where(sel1 | sel2, 1.0, 0.0)
    sel_pad = jnp.concatenate([sel, jnp.zeros_like(sel)], axis=0).astype(BF16)
    base = _dot(sel_pad, before_ref[...])[0:N_EXPERTS, :] + carry_ref[:, 0:1]
    r1 = jnp.sum(jnp.where(sel1, base, 0.0), axis=0, keepdims=True)
    r2 = jnp.sum(jnp.where(sel2, base, 0.0), axis=0, keepdims=True)
    carry_ref[...] = carry_ref[...] + jnp.sum(sel, axis=1, keepdims=True)

    meta = jnp.where(ex == 0, i1, jnp.where(ex == 1, i2, jnp.where(ex == 2, r1, r2)))
    mi_ref[...] = meta.astype(I32)
    gt = jnp.where(ex == 0, g1, jnp.where(ex == 1, g2, 0.0))
    gt = jnp.concatenate([gt, jnp.zeros((LANES - N_EXPERTS, tm), F32)], axis=0)
    mf_ref[...] = gt.T[:, 0:SUBLANES]
    cnt_ref[...] = carry_ref[...].astype(I32)


def _router_kernel(tm, h_ref, wcat_ref, cin_ref, mi_ref, mf_ref, cnt_ref, carry_ref, before_ref):
    @pl.when(pl.program_id(0) == 0)
    def _():
        _route_init(tm, cin_ref, carry_ref, before_ref)

    _route_block(tm, h_ref[...], wcat_ref, mi_ref, mf_ref, cnt_ref, carry_ref, before_ref)


def _router(h, wr_cat, counts_in):
    t, d = h.shape
    tm = min(TM_ROUTE, t)
    assert t % tm == 0 and N_EXPERTS == SUBLANES
    return pl.pallas_call(
        functools.partial(_router_kernel, tm),
        grid=(t // tm,),
        in_specs=[pl.BlockSpec((tm, d), lambda i: (i, 0)), _full_spec(wr_cat), _full_spec(counts_in)],
        out_specs=(pl.BlockSpec((SUBLANES, tm), lambda i: (0, i)),
                   pl.BlockSpec((tm, SUBLANES), lambda i: (i, 0)),
                   pl.BlockSpec((SUBLANES, LANES), lambda i: (0, 0))),
        out_shape=(jax.ShapeDtypeStruct((SUBLANES, t), I32),
                   jax.ShapeDtypeStruct((t, SUBLANES), F32),
                   jax.ShapeDtypeStruct((SUBLANES, LANES), I32)),
        scratch_shapes=[pltpu.VMEM((SUBLANES, LANES), F32), pltpu.VMEM((tm, tm), BF16)],
        compiler_params=_cparams(("arbitrary",)),
        name="router",
    )(h, wr_cat, counts_in)


def _row_copy(src, src_row, dst, dst_row, sem):
    return pltpu.make_async_copy(src.at[pl.ds(src_row, 1), :], dst.at[pl.ds(dst_row, 1), :], sem)


def _rows_wait(src, dst, n_rows, sem):
    pltpu.make_async_copy(src.at[pl.ds(0, n_rows), :], dst.at[pl.ds(0, n_rows), :], sem).wait()


def _dispatch_kernel(tm, tm_expert, n_blocks, blk_ranges, zi_ref, dest_ref, *rest):
    n_src = len(blk_ranges)
    h_refs = rest[:n_src]
    xs_ref, xbuf, zblk, lsems, rsems, zsem = rest[n_src:]
    i = pl.program_id(0)
    n = pl.num_programs(0)

    def block_load(g, start):
        slot = lax.rem(g, N_STAGE)
        for h_ref, (b0, b1) in zip(h_refs, blk_ranges):
            @pl.when((g >= b0) & (g < b1))
            def _(h_ref=h_ref, b0=b0):
                cp = pltpu.make_async_copy(h_ref.at[pl.ds((g - b0) * tm, tm), :], xbuf.at[slot],
                                           lsems.at[slot])
                if start:
                    cp.start()
                else:
                    cp.wait()

    def rows_wait(g):
        slot = lax.rem(g, N_STAGE)
        for _k in range(TOP_K):
            _rows_wait(xbuf.at[slot], xs_ref, tm, rsems.at[slot])

    @pl.when(i == 0)
    def _():
        block_load(i, True)

    @pl.when(i >= N_STAGE - 1)
    def _():
        rows_wait(i - (N_STAGE - 1))

    @pl.when(i + 1 < n)
    def _():
        block_load(i + 1, True)

    block_load(i, False)
    cur = lax.rem(i, N_STAGE)

    def issue(j, carry):
        r0 = pl.multiple_of(j * SUBLANES, SUBLANES)
        for u in range(SUBLANES):
            for k in range(TOP_K):
                dst = dest_ref[0, 0, j * SUBLANES + (k * tm + u)]
                _row_copy(xbuf.at[cur], r0 + u, xs_ref, dst, rsems.at[cur]).start(priority=k)
        return carry

    lax.fori_loop(0, tm // SUBLANES, issue, 0)

    @pl.when(i == n - 1)
    def _():
        for back in range(N_STAGE - 2, -1, -1):
            @pl.when(i >= back)
            def _(back=back):
                rows_wait(i - back)

    @pl.when(i == 0)
    def _():
        zblk[...] = jnp.zeros_like(zblk)
        for e in range(N_EXPERTS):
            lo = zi_ref[e] + zi_ref[N_EXPERTS + e]
            hi = zi_ref[e] + zi_ref[2 * N_EXPERTS + e]

            def zissue(r, carry):
                _row_copy(zblk, 0, xs_ref, r, zsem).start()
                return carry

            def zdrain(r, carry):
                _row_copy(zblk, 0, xs_ref, r, zsem).wait()
                return carry

            lax.fori_loop(lo, hi, zissue, 0)
            lax.fori_loop(lo, hi, zdrain, 0)

        def bcopy(b):
            return pltpu.make_async_copy(zblk, xs_ref.at[pl.ds(b * tm_expert, tm_expert), :], zsem)

        def bissue(b, carry):
            bcopy(b).start()
            return carry

        def bdrain(b, carry):
            bcopy(b).wait()
            return carry

        lax.fori_loop(zi_ref[3 * N_EXPERTS], n_blocks, bissue, 0)
        lax.fori_loop(zi_ref[3 * N_EXPERTS], n_blocks, bdrain, 0)


def _dispatch(hs_list, dest_blk, zinfo, n_blocks, tm_expert):
    d = hs_list[0].shape[1]
    nb = dest_blk.shape[0]
    tm = dest_blk.shape[2] // TOP_K
    blk_ranges, b0 = [], 0
    for h in hs_list:
        assert h.shape[0] % tm == 0 and h.shape[0] >= tm
        blk_ranges.append((b0, b0 + h.shape[0] // tm))
        b0 = blk_ranges[-1][1]
    assert b0 == nb
    any_spec = pl.BlockSpec(memory_space=pl.ANY)
    return pl.pallas_call(
        functools.partial(_dispatch_kernel, tm, tm_expert, n_blocks, tuple(blk_ranges)),
        grid_spec=pltpu.PrefetchScalarGridSpec(
            num_scalar_prefetch=1,
            grid=(nb,),
            in_specs=[pl.BlockSpec((1, 1, TOP_K * tm), lambda i, zi: (i, 0, 0), memory_space=pltpu.SMEM)]
            + [any_spec] * len(hs_list),
            out_specs=any_spec,
            scratch_shapes=[pltpu.VMEM((N_STAGE, tm, d), F32), pltpu.VMEM((tm_expert, d), F32),
                            pltpu.SemaphoreType.DMA((N_STAGE,)), pltpu.SemaphoreType.DMA((N_STAGE,)),
                            pltpu.SemaphoreType.DMA(())],
        ),
        out_shape=jax.ShapeDtypeStruct((n_blocks * tm_expert, d), F32),
        compiler_params=_cparams(("arbitrary",)),
        name="moe_dispatch",
    )(zinfo, dest_blk, *hs_list)


def _ffn_moe_kernel(tf, be_ref, nu_ref, x_ref, wg_hbm, wu_hbm, wd_hbm, y_ref,
                    wg_ref, wu_ref, wd_ref, stage_in, stage_out, sems):
    b = pl.program_id(0)
    nb = pl.num_programs(0)
    d, dff = wg_ref.shape
    n_slots = stage_in.shape[0]
    rows_in = stage_in.shape[1]
    rows_out = stage_out.shape[1]
    n_in, n_out = 2 * (d // rows_in), dff // rows_out
    e = be_ref[b]
    fresh = (b == 0) | (e != be_ref[jnp.maximum(b - 1, 0)])
    e_next = be_ref[jnp.minimum(b + 1, nb - 1)]

    def chunk_in(ex, n):
        src, dst = (wg_hbm, wg_ref) if n < n_in // 2 else (wu_hbm, wu_ref)
        rows = pl.ds((n % (n_in // 2)) * rows_in, rows_in)
        stage = stage_in.at[n % n_slots]
        return pltpu.make_async_copy(src.at[ex, rows, :], stage, sems.at[0, n % n_slots]), stage, dst.at[rows, :]

    def chunk_out(ex, n):
        rows = pl.ds(n * rows_out, rows_out)
        stage = stage_out.at[n % n_slots]
        return pltpu.make_async_copy(wd_hbm.at[ex, rows, :], stage, sems.at[1, n % n_slots]), stage, wd_ref.at[rows, :]

    def request_first(ex):
        for n in range(n_slots):
            chunk_in(ex, n)[0].start()
            chunk_out(ex, n)[0].start()

    @pl.when((b < nu_ref[0]) & fresh)
    def _():
        @pl.when(b == 0)
        def _():
            request_first(e)

        for n in range(n_in):
            for chunk, count in ((chunk_in, n_in), (chunk_out, n_out)):
                if n < count:
                    copy, staged, dst = chunk(e, n)
                    copy.wait()
                    dst[...] = staged[...].astype(BF16)
                    if n + n_slots < count:
                        chunk(e, n + n_slots)[0].start()

    @pl.when((b + 1 < nu_ref[0]) & (e_next != e))
    def _():
        request_first(e_next)

    @pl.when(b < nu_ref[0])
    def _():
        xb = x_ref[...].astype(BF16)
        ffn = None
        for c0 in range(0, dff, tf):
            mid = _silu(_dot(xb, wg_ref[:, c0:c0 + tf])) * _dot(xb, wu_ref[:, c0:c0 + tf])
            part = _dot(mid.astype(BF16), wd_ref[c0:c0 + tf, :])
            ffn = part if ffn is None else ffn + part
        y_ref[...] = ffn

    @pl.when(b >= nu_ref[0])
    def _():
        y_ref[...] = jnp.zeros_like(y_ref)


def _ffn_moe(xs, wg, wu, wd, block_e, n_used, tm, tf):
    cap, d = xs.shape
    dff = wg.shape[2]
    rows_in, rows_out = 128, 512
    assert cap % tm == 0 and dff % tf == 0 and d % rows_in == 0 and dff % rows_out == 0
    assert min(2 * (d // rows_in), dff // rows_out) >= N_WSTAGE
    any_spec = pl.BlockSpec(memory_space=pl.ANY)
    return pl.pallas_call(
        functools.partial(_ffn_moe_kernel, tf),
        grid_spec=pltpu.PrefetchScalarGridSpec(
            num_scalar_prefetch=2,
            grid=(cap // tm,),
            in_specs=[pl.BlockSpec((tm, d), lambda b, be, nu: (jnp.minimum(b, nu[0] - 1), 0)),
                      any_spec, any_spec, any_spec],
            out_specs=pl.BlockSpec((tm, d), lambda b, be, nu: (b, 0)),
            scratch_shapes=[pltpu.VMEM((d, dff), BF16), pltpu.VMEM((d, dff), BF16), pltpu.VMEM((dff, d), BF16),
                            pltpu.VMEM((N_WSTAGE, rows_in, dff), F32), pltpu.VMEM((N_WSTAGE, rows_out, d), F32),
                            pltpu.SemaphoreType.DMA((2, N_WSTAGE))],
        ),
        out_shape=jax.ShapeDtypeStruct((cap, d), F32),
        compiler_params=_cparams(("arbitrary",)),
        name="ffn_moe",
    )(block_e, n_used, xs, wg, wu, wd)


def _combine_kernel(alpha, tm, dcur_ref, dnext_ref, h_ref, gate_ref, g_ref, b_ref, ys_ref, o_ref, ybuf, sems):
    i = pl.program_id(0)
    n = pl.num_programs(0)

    def issue(dref, slot):
        def body(j, carry):
            r0 = pl.multiple_of(j * SUBLANES, SUBLANES)
            for u in range(SUBLANES):
                for k in range(TOP_K):
                    src = dref[0, 0, j * SUBLANES + (k * tm + u)]
                    _row_copy(ys_ref, src, ybuf.at[slot, k], r0 + u, sems.at[slot]).start(priority=k)
            return carry

        lax.fori_loop(0, tm // SUBLANES, body, 0)

    @pl.when(i == 0)
    def _():
        issue(dcur_ref, 0)

    @pl.when(i + 1 < n)
    def _():
        issue(dnext_ref, (i + 1) % 2)

    slot = i % 2
    for k in range(TOP_K):
        _rows_wait(ys_ref, ybuf.at[slot, k], tm, sems.at[slot])

    gates = gate_ref[...]
    ffn = gates[:, 0:1] * ybuf[slot, 0] + gates[:, 1:2] * ybuf[slot, 1]
    o_ref[...] = _ln(alpha * h_ref[...] + ffn, g_ref[...], b_ref[...])


def _combine(h, dest_blk, gates, ys, g, b, alpha):
    t, d = h.shape
    nb = dest_blk.shape[0]
    tm = dest_blk.shape[2] // TOP_K
    dest_spec = lambda imap: pl.BlockSpec((1, 1, TOP_K * tm), imap, memory_space=pltpu.SMEM)
    return pl.pallas_call(
        functools.partial(_combine_kernel, alpha, tm),
        grid=(nb,),
        in_specs=[dest_spec(lambda i: (i, 0, 0)),
                  dest_spec(lambda i: (jnp.minimum(i + 1, nb - 1), 0, 0)),
                  pl.BlockSpec((tm, d), lambda i: (i, 0)),
                  pl.BlockSpec((tm, SUBLANES), lambda i: (i, 0)),
                  pl.BlockSpec((1, d), lambda i: (0, 0)),
                  pl.BlockSpec((1, d), lambda i: (0, 0)),
                  pl.BlockSpec(memory_space=pl.ANY)],
        out_specs=pl.BlockSpec((tm, d), lambda i: (i, 0)),
        out_shape=jax.ShapeDtypeStruct((t, d), F32),
        scratch_shapes=[pltpu.VMEM((2, TOP_K, tm, d), F32), pltpu.SemaphoreType.DMA((2,))],
        compiler_params=_cparams(("arbitrary",)),
        name="moe_combine",
    )(dest_blk, dest_blk, h, gates, g, b, ys)


def _ffn_routed(hs_list, first_routed, mw, g, b, alpha, tm_expert):
    routed = [first_routed[:2]]
    counts_in = first_routed[2]
    for h in hs_list[1:]:
        meta_i, gates, counts_in = _router(h, mw['w_router_cat'], counts_in)
        routed.append((meta_i, gates))
    counts = counts_in[:N_EXPERTS, 0]
    padded = (counts + tm_expert - 1) // tm_expert * tm_expert
    pad_end = jnp.cumsum(padded)
    pad_start = (pad_end - padded).astype(I32)
    n_assign = sum(h.shape[0] for h in hs_list) * TOP_K
    n_blocks = -(-n_assign // tm_expert) + N_EXPERTS
    cap = n_blocks * tm_expert
    n_used = (pad_end[-1] // tm_expert).astype(I32)
    blk_start = jnp.minimum(jnp.arange(n_blocks, dtype=I32), n_used - 1) * tm_expert
    block_e = jnp.minimum(jnp.sum(blk_start[:, None] >= pad_end[None, :], axis=1), N_EXPERTS - 1).astype(I32)
    zinfo = jnp.concatenate([pad_start, counts, padded, n_used.reshape(1)]).astype(I32)

    dests = []
    for h, (meta_i, _) in zip(hs_list, routed):
        dest = meta_i[TOP_K:2 * TOP_K, :]
        for e in range(N_EXPERTS):
            dest = dest + jnp.where(meta_i[0:TOP_K, :] == e, pad_start[e], 0)
        nb = h.shape[0] // TM_MOVE
        dests.append(dest.reshape(TOP_K, nb, TM_MOVE).transpose(1, 0, 2).reshape(nb, 1, TOP_K * TM_MOVE))
    xs = _dispatch(hs_list, jnp.concatenate(dests, axis=0), zinfo, n_blocks, tm_expert)
    ys = _ffn_moe(xs, mw['w_e_gate'], mw['w_e_up'], mw['w_e_down'], block_e, n_used.reshape(1), tm_expert,
                  TF_EXPERT)
    return [_combine(h, dest_blk, gates, ys, g, b, alpha)
            for h, dest_blk, (_, gates) in zip(hs_list, dests, routed)]


def _prep_layer(i, p, n_step):
    dq = p['conv_a_w'].shape[2]
    hd = dq // N_HEADS_B
    tril = jnp.tril(jnp.ones((CHUNK, CHUNK), dtype=bool))
    w_s = p['w_s'][i]
    b_s = p['b_s'][i]
    wp = p['w_pool'][i]
    ng, gc = wp.shape[0], wp.shape[1]
    w_pool_bd = (jnp.eye(ng, dtype=F32)[:, None, :, None] * wp[:, :, None, :]).reshape(ng * gc, ng * gc)
    row = lambda a: a.reshape(1, -1)
    return dict(
        ln_in_g=row(p['ln_in_g']), ln_in_b=row(p['ln_in_b']),
        w_in=p['w_in'][i].astype(BF16),
        conv_a_w=p['conv_a_w'][i], conv_a_b=row(p['conv_a_b'][i]),
        ln_a_g=row(p['ln_a_g'][i]), ln_a_b=row(p['ln_a_b'][i]),
        ln_v_g=row(p['ln_v_g'][i]), ln_v_b=row(p['ln_v_b'][i]),
        ws_tril=jnp.where(tril[None], w_s, 0).astype(BF16),
        bs_mat=jnp.repeat(b_s.T, hd, axis=1),
        ws_vec=jnp.repeat(jnp.where(tril[None], w_s, 0)[:, :n_step, :n_step].transpose(1, 2, 0)
                          .reshape(n_step * n_step, N_HEADS_B), hd, axis=1),
        bs_vec=jnp.repeat(b_s[:, :n_step].T, hd, axis=1),
        conv_c_w=p['conv_c_w'][i],
        w_pool_bd=w_pool_bd.astype(BF16), pool_scale=row(p['pool_scale'][i]),
        w_out=p['w_out'][i].astype(BF16),
        ln_mix_g=row(p['ln_mix_g'][i]), ln_mix_b=row(p['ln_mix_b'][i]),
    )


def kernel(x_prompt, x_sample, state_conv_a, state_conv_c, state_pool_d, ln_in_g, ln_in_b, w_in, conv_a_w,
           conv_a_b, ln_a_g, ln_a_b, ln_v_g, ln_v_b, w_s, b_s, conv_c_w, w_pool, pool_scale, w_out,
           ln_mix_g, ln_mix_b, w_ff_gate, w_ff_up, w_ff_down, w_router, w_e_gate, w_e_up, w_e_down,
           ln_ffn_g, ln_ffn_b):
    p = dict(ln_in_g=ln_in_g, ln_in_b=ln_in_b, w_in=w_in, conv_a_w=conv_a_w, conv_a_b=conv_a_b,
             ln_a_g=ln_a_g, ln_a_b=ln_a_b, ln_v_g=ln_v_g, ln_v_b=ln_v_b, w_s=w_s, b_s=b_s,
             conv_c_w=conv_c_w, w_pool=w_pool, pool_scale=pool_scale, w_out=w_out,
             ln_mix_g=ln_mix_g, ln_mix_b=ln_mix_b)
    depth = w_in.shape[0]
    bsz, seq, d = x_prompt.shape
    n_seq, n_step, _ = x_sample.shape
    dq = conv_a_w.shape[2]
    alpha = float((2.0 * depth) ** 0.25)

    hp = x_prompt
    hs = x_sample.transpose(1, 0, 2).reshape(n_step * n_seq, d)
    outs = {k: [] for k in ('pa', 'pc', 'pd', 'pv', 'sa', 'sc', 'sd', 'sv')}
    for i in range(depth):
        lw = _prep_layer(i, p, n_step)
        j = i // 2
        routed = i % 2 == 1
        if routed:
            wr = jnp.pad(w_router[j], ((0, 0), (0, LANES - N_EXPERTS)))
            wr_hi = wr.astype(BF16)
            wr_lo = (wr - wr_hi.astype(F32)).astype(BF16)
            wr_cat = jnp.concatenate([wr_hi, wr_lo], axis=1)
        hp, pa, pc, pd, pv, *route_p = _mixer_prompt(hp, lw, i == 0, alpha, wr_cat if routed else None)
        tm_state = lambda s: s.transpose(1, 0, 2).reshape(-1, dq)
        hs, sa, sc, sd, sv = _mixer_sample(hs, tm_state(state_conv_a[i]), tm_state(state_conv_c[i]),
                                           tm_state(state_pool_d[i]), lw, i == 0, alpha, n_seq, n_step)
        lg, lb = ln_ffn_g[i].reshape(1, -1), ln_ffn_b[i].reshape(1, -1)
        if not routed:
            hp, hs = _ffn_dense(hp.reshape(bsz * seq, d), hs, w_ff_gate[j], w_ff_up[j], w_ff_down[j], lg, lb,
                                alpha, TM_DENSE, TF_DENSE)
            hp = hp.reshape(bsz, seq, d)
        else:
            mw = dict(w_router_cat=wr_cat, w_e_gate=w_e_gate[j], w_e_up=w_e_up[j], w_e_down=w_e_down[j])
            hp, hs = _ffn_routed([hp.reshape(bsz * seq, d), hs], route_p, mw, lg, lb, alpha, TM_EXPERT)
            hp = hp.reshape(bsz, seq, d)
        back = lambda a: a.reshape(-1, n_seq, dq).transpose(1, 0, 2)
        for k, val in zip(('pa', 'pc', 'pd', 'pv', 'sa', 'sc', 'sd', 'sv'),
                          (pa, pc, pd, pv, back(sa), back(sc), back(sd), back(sv))):
            outs[k].append(val)
    y_sample = hs.reshape(n_step, n_seq, d).transpose(1, 0, 2)
    st = lambda k: jnp.stack(outs[k])
    return (hp, y_sample, st('pa'), st('pc'), st('pd'), st('pv'), st('sa'), st('sc'), st('sd'), st('sv'))
```

```python
import functools

import numpy as np
import jax
import jax.numpy as jnp
from jax import lax
from jax.experimental import pallas as pl
from jax.experimental.pallas import tpu as pltpu

F32 = jnp.float32
BF16 = jnp.bfloat16
I32 = jnp.int32

PAST_LEN = 16384
CHUNK = 128
N_HEADS_B = 4
POOL_WINDOWS = (2, 4, 8, 16)
POOL_PAST = 15
CONV_A_W = 31
CONV_C_W = 3
N_EXPERTS = 8
TOP_K = 2
LN_EPS = 1e-5
SQRT_HALF = float(np.sqrt(0.5).astype(np.float32))

LANES = 128
SUBLANES = 8
VMEM_LIMIT_BYTES = 56 * 1024 * 1024

HIST_A = 32
HIST_C = 8
HIST_D = 24

TL_PROMPT = 512
TM_DENSE = 512
TF_DENSE = 1792
TF_EXPERT = 1792
TM_ROUTE = 512
TM_MOVE = 256
TM_EXPERT = 512
N_STAGE = 3
N_WSTAGE = 5


def _ln(x, g, b):
    mu = jnp.mean(x, axis=-1, keepdims=True)
    xc = x - mu
    var = jnp.mean(xc * xc, axis=-1, keepdims=True)
    return xc * lax.rsqrt(var + LN_EPS) * g + b


def _gelu(x):
    return 0.5 * x * (1.0 + lax.erf(x * SQRT_HALF))


def _silu(x):
    return x * jax.nn.sigmoid(x)


def _dot(a, b):
    return jnp.dot(a, b, preferred_element_type=F32)


def _cparams(sem):
    return pltpu.CompilerParams(dimension_semantics=sem, vmem_limit_bytes=VMEM_LIMIT_BYTES)


def _mixer_prompt_kernel(first_layer, route, alpha, tl, d_mix,
                         x_ref, lng_ref, lnb_ref, win_ref, caw_ref, cab_ref, lag_ref, lab_ref,
                         lvg_ref, lvb_ref, ws_ref, bsm_ref, ccw_ref, wpool_ref, pscale_ref,
                         wout_ref, lmg_ref, lmb_ref, *refs):
    if route:
        wcat_ref, cin_ref = refs[:2]
        refs = refs[2:]
    h_out, sa_out, sc_out, sd_out, v_out = refs[:5]
    refs = refs[5:]
    if route:
        mi_ref, mf_ref, cnt_ref = refs[:3]
        refs = refs[3:]
    ext_a, ext_c, ext_d, cat_ref, z_ref, hb_ref, mix_ref = refs[:7]
    dq = d_mix // 4
    j = pl.program_id(1)
    nj = pl.num_programs(1)
    if route:
        carry_ref, before_ref = refs[7:]

        @pl.when((pl.program_id(0) == 0) & (j == 0))
        def _():
            _route_init(tl, cin_ref, carry_ref, before_ref)

    @pl.when(j == 0)
    def _():
        ext_a[0, 0:HIST_A, :] = jnp.zeros((HIST_A, dq), F32)
        ext_c[0:HIST_C, :] = jnp.zeros((HIST_C, dq), F32)
        ext_d[0:HIST_D, :] = jnp.zeros((HIST_D, dq), F32)

    d = x_ref.shape[2]
    rb = CHUNK

    def resid(r0, n):
        return h_out[0, r0:r0 + n, :] if first_layer else x_ref[0, r0:r0 + n, :]

    for r0 in range(0, tl, rb):
        x = x_ref[0, r0:r0 + rb, :]
        if first_layer:
            x = _ln(x, lng_ref[...], lnb_ref[...])
            h_out[0, r0:r0 + rb, :] = x
        hb_ref[r0:r0 + rb, :] = x.astype(BF16)

    z_ref[...] = _dot(hb_ref[...], win_ref[...])

    def zcol(r0, i, n=rb):
        return z_ref[r0:r0 + n, i * dq:(i + 1) * dq]

    chunk_rows = range(0, tl, CHUNK)

    for r0 in chunk_rows:
        ext_a[0, HIST_A + r0:HIST_A + r0 + rb, :] = zcol(r0, 0) * jax.nn.sigmoid(zcol(r0, 1))
    n_ext = HIST_A + tl
    for r0 in range(0, n_ext - SUBLANES, rb):
        n = min(rb, n_ext - SUBLANES - r0)
        blk = ext_a[0, r0:r0 + n + SUBLANES, :]
        for s in range(1, SUBLANES):
            ext_a[s, r0:r0 + n, :] = pltpu.roll(blk, n + SUBLANES - s, axis=0)[0:n, :]
    off_a = HIST_A - (CONV_A_W - 1)
    for q0 in range(0, tl, 64):
        acc = jnp.zeros((64, dq), F32)
        for k in range(CONV_A_W):
            s = (off_a + k) % SUBLANES
            row = off_a + k - s + q0
            acc = acc + ext_a[s, row:row + 64, :] * caw_ref[k:k + 1, :]
        y = _silu(_ln(acc + cab_ref[...], lag_ref[...], lab_ref[...]))
        cat_ref[q0:q0 + 64, 0:dq] = y.astype(BF16)

    hd = dq // N_HEADS_B
    lane_head = lax.broadcasted_iota(I32, (CHUNK, dq), 1) // hd
    for r0 in chunk_rows:
        v = _ln(_gelu(zcol(r0, 3)), lvg_ref[...], lvb_ref[...])
        sm = bsm_ref[...]
        for hh in range(N_HEADS_B):
            vm = jnp.where(lane_head == hh, v, 0.0).astype(BF16)
            sm = sm + _dot(ws_ref[hh], vm)
        cat_ref[r0:r0 + CHUNK, dq:2 * dq] = (_gelu(zcol(r0, 2)) * sm).astype(BF16)
        if r0 == tl - CHUNK:
            @pl.when(j == nj - 1)
            def _(v=v):
                v_out[0] = v

    off_c = HIST_C - (CONV_C_W - 1)
    for r0 in chunk_rows:
        ext_c[HIST_C + r0:HIST_C + r0 + rb, :] = zcol(r0, 6) * zcol(r0, 4)
    for r0 in chunk_rows:
        conv_c = jnp.zeros((CHUNK, dq), F32)
        for k in range(CONV_C_W):
            conv_c = conv_c + ext_c[off_c + r0 + k:off_c + r0 + k + CHUNK, :] * ccw_ref[k:k + 1, :]
        cat_ref[r0:r0 + CHUNK, 2 * dq:3 * dq] = (zcol(r0, 5) * conv_c).astype(BF16)

    gc = dq // len(POOL_WINDOWS)
    lane_grp = lax.broadcasted_iota(I32, (CHUNK, dq), 1) // gc
    lead = 2 * SUBLANES
    for r0 in chunk_rows:
        ext_d[HIST_D + r0:HIST_D + r0 + rb, :] = zcol(r0, 7)
    for r0 in chunk_rows:
        e = ext_d[HIST_D + r0 - lead:HIST_D + r0 + CHUNK, :]
        s2 = e + pltpu.roll(e, 1, axis=0)
        s4 = s2 + pltpu.roll(s2, 2, axis=0)
        s8 = s4 + pltpu.roll(s4, 4, axis=0)
        s16 = s8 + pltpu.roll(s8, 8, axis=0)
        sums = (s2, s4, s8, s16)
        pos = (j * tl + r0 + lax.broadcasted_iota(I32, (CHUNK, 1), 0)).astype(F32)
        mean = jnp.zeros((CHUNK, dq), F32)
        for g, w in enumerate(POOL_WINDOWS):
            inv = 1.0 / jnp.minimum(pos + 1.0, float(w))
            mean = jnp.where(lane_grp == g, sums[g][lead:lead + CHUNK, :] * inv, mean)
        dd = (mean - e[lead:lead + CHUNK, :]).astype(BF16)
        cat_ref[r0:r0 + CHUNK, 3 * dq:4 * dq] = (_dot(dd, wpool_ref[...]) * pscale_ref[...]).astype(BF16)

    mix_ref[...] = _dot(cat_ref[...], wout_ref[...])
    for r0 in range(0, tl, 64):
        h_out[0, r0:r0 + 64, :] = _ln(alpha * resid(r0, 64) + mix_ref[r0:r0 + 64, :], lmg_ref[...], lmb_ref[...])
    if route:
        _route_block(tl, h_out[0], wcat_ref, mi_ref, mf_ref, cnt_ref, carry_ref, before_ref)

    @pl.when(j == nj - 1)
    def _():
        sa_out[0] = ext_a[0, HIST_A + tl - (CONV_A_W - 1):HIST_A + tl, :]
        sc_out[0] = ext_c[HIST_C + tl - (CONV_C_W - 1):HIST_C + tl, :]
        sd_out[0] = ext_d[HIST_D + tl - POOL_PAST:HIST_D + tl, :]

    ext_a[0, 0:HIST_A, :] = ext_a[0, tl:tl + HIST_A, :]
    ext_c[0:HIST_C, :] = ext_c[tl:tl + HIST_C, :]
    ext_d[0:HIST_D, :] = ext_d[tl:tl + HIST_D, :]


def _full_spec(arr):
    nd = arr.ndim
    return pl.BlockSpec(arr.shape, lambda *_: (0,) * nd)


def _mixer_prompt(x, lw, first_layer, alpha, route_w=None):
    bsz, seq, d = x.shape
    d_mix = lw['w_out'].shape[0]
    dq = d_mix // 4
    tl = TL_PROMPT
    nj = seq // tl
    route = route_w is not None
    assert seq % tl == 0 and tl % CHUNK == 0 and seq >= CHUNK
    params = [lw['ln_in_g'], lw['ln_in_b'], lw['w_in'], lw['conv_a_w'], lw['conv_a_b'], lw['ln_a_g'],
              lw['ln_a_b'], lw['ln_v_g'], lw['ln_v_b'], lw['ws_tril'], lw['bs_mat'], lw['conv_c_w'],
              lw['w_pool_bd'], lw['pool_scale'], lw['w_out'], lw['ln_mix_g'], lw['ln_mix_b']]
    out_shape = [
        jax.ShapeDtypeStruct((bsz, seq, d), F32),
        jax.ShapeDtypeStruct((bsz, CONV_A_W - 1, dq), F32),
        jax.ShapeDtypeStruct((bsz, CONV_C_W - 1, dq), F32),
        jax.ShapeDtypeStruct((bsz, POOL_PAST, dq), F32),
        jax.ShapeDtypeStruct((bsz, CHUNK, dq), F32),
    ]
    state_spec = lambda rows: pl.BlockSpec((1, rows, dq), lambda b, j: (b, 0, 0))
    out_specs = [pl.BlockSpec((1, tl, d), lambda b, j: (b, j, 0)),
                 state_spec(CONV_A_W - 1), state_spec(CONV_C_W - 1), state_spec(POOL_PAST), state_spec(CHUNK)]
    scratch = [pltpu.VMEM((SUBLANES, HIST_A + tl, dq), F32), pltpu.VMEM((HIST_C + tl, dq), F32),
               pltpu.VMEM((HIST_D + tl, dq), F32), pltpu.VMEM((tl, d_mix), BF16),
               pltpu.VMEM((tl, lw['w_in'].shape[1]), F32), pltpu.VMEM((tl, d), BF16),
               pltpu.VMEM((tl, d), F32)]
    if route:
        assert N_EXPERTS == SUBLANES
        params += [route_w, jnp.zeros((SUBLANES, LANES), I32)]
        out_shape += [jax.ShapeDtypeStruct((SUBLANES, bsz * seq), I32),
                      jax.ShapeDtypeStruct((bsz * seq, SUBLANES), F32),
                      jax.ShapeDtypeStruct((SUBLANES, LANES), I32)]
        out_specs += [pl.BlockSpec((SUBLANES, tl), lambda b, j: (0, b * nj + j)),
                      pl.BlockSpec((tl, SUBLANES), lambda b, j: (b * nj + j, 0)),
                      pl.BlockSpec((SUBLANES, LANES), lambda b, j: (0, 0))]
        scratch += [pltpu.VMEM((SUBLANES, LANES), F32), pltpu.VMEM((tl, tl), BF16)]
    return pl.pallas_call(
        functools.partial(_mixer_prompt_kernel, first_layer, route, alpha, tl, d_mix),
        grid=(bsz, nj),
        in_specs=[pl.BlockSpec((1, tl, d), lambda b, j: (b, j, 0))] + [_full_spec(p) for p in params],
        out_specs=tuple(out_specs),
        out_shape=tuple(out_shape),
        scratch_shapes=scratch,
        compiler_params=_cparams(("arbitrary", "arbitrary")),
        name="mixer_prompt",
    )(x, *params)


def _mixer_sample_kernel(first_layer, alpha, n_seq, n_step, d_mix,
                         x_ref, sa_ref, sc_ref, sd_ref, lng_ref, lnb_ref, win_ref, caw_ref, cab_ref,
                         lag_ref, lab_ref, lvg_ref, lvb_ref, wsv_ref, bsv_ref, ccw_ref, wpool_ref,
                         pscale_ref, wout_ref, lmg_ref, lmb_ref,
                         h_out, sa_out, sc_out, sd_out, v_out, cat_ref):
    dq = d_mix // 4
    x = x_ref[...]
    h = _ln(x, lng_ref[...], lnb_ref[...]) if first_layer else x
    hb = h.astype(BF16)

    def proj(i):
        return _dot(hb, win_ref[:, i * dq:(i + 1) * dq])

    def slab(val, i):
        return val[i * n_seq:(i + 1) * n_seq, :]

    def ext_slabs(state_ref, n_past, cur):
        return ([state_ref[i * n_seq:(i + 1) * n_seq, :] for i in range(n_past)]
                + [slab(cur, i) for i in range(n_step)])

    def store_state(out_ref, slabs, n_keep):
        for i, sl in enumerate(slabs[len(slabs) - n_keep:]):
            out_ref[i * n_seq:(i + 1) * n_seq, :] = sl

    a_glu = proj(0) * jax.nn.sigmoid(proj(1))
    ea = ext_slabs(sa_ref, CONV_A_W - 1, a_glu)
    for l in range(n_step):
        acc = jnp.zeros((n_seq, dq), F32)
        for k in range(CONV_A_W):
            acc = acc + ea[l + k] * caw_ref[k:k + 1, :]
        y = _silu(_ln(acc + cab_ref[...], lag_ref[...], lab_ref[...]))
        cat_ref[l * n_seq:(l + 1) * n_seq, 0:dq] = y.astype(BF16)
    store_state(sa_out, ea, CONV_A_W - 1)

    u = _gelu(proj(2))
    v = _ln(_gelu(proj(3)), lvg_ref[...], lvb_ref[...])
    v_out[...] = v
    for l in range(n_step):
        s = jnp.zeros((n_seq, dq), F32) + bsv_ref[l:l + 1, :]
        for m in range(l + 1):
            s = s + slab(v, m) * wsv_ref[l * n_step + m:l * n_step + m + 1, :]
        cat_ref[l * n_seq:(l + 1) * n_seq, dq:2 * dq] = (slab(u, l) * s).astype(BF16)

    c_x = proj(4)
    c_b = proj(5)
    c_c = proj(6)
    gx = c_c * c_x
    ec = ext_slabs(sc_ref, CONV_C_W - 1, gx)
    for l in range(n_step):
        acc = jnp.zeros((n_seq, dq), F32)
        for k in range(CONV_C_W):
            acc = acc + ec[l + k] * ccw_ref[k:k + 1, :]
        cat_ref[l * n_seq:(l + 1) * n_seq, 2 * dq:3 * dq] = (slab(c_b, l) * acc).astype(BF16)
    store_state(sc_out, ec, CONV_C_W - 1)

    d_in = proj(7)
    ed = ext_slabs(sd_ref, POOL_PAST, d_in)
    memo = {}

    def wsum(i, w):
        if i < 0:
            return None
        if w == 1:
            return ed[i]
        if (i, w) not in memo:
            a, b = wsum(i, w // 2), wsum(i - w // 2, w // 2)
            memo[(i, w)] = a if b is None else a + b
        return memo[(i, w)]

    gc = dq // len(POOL_WINDOWS)
    lane_grp = lax.broadcasted_iota(I32, (n_seq, dq), 1) // gc
    for l in range(n_step):
        mean = jnp.zeros((n_seq, dq), F32)
        for g, w in enumerate(POOL_WINDOWS):
            count = min(PAST_LEN + l + 1, w)
            mean = jnp.where(lane_grp == g, wsum(POOL_PAST + l, w) * (1.0 / count), mean)
        dd = (mean - slab(d_in, l)).astype(BF16)
        cat_ref[l * n_seq:(l + 1) * n_seq, 3 * dq:4 * dq] = (
            _dot(dd, wpool_ref[...]) * pscale_ref[...]).astype(BF16)
    store_state(sd_out, ed, POOL_PAST)

    mix = _dot(cat_ref[...], wout_ref[...])
    h_out[...] = _ln(alpha * h + mix, lmg_ref[...], lmb_ref[...])


def _mixer_sample(x_tm, sa, sc, sd, lw, first_layer, alpha, n_seq, n_step):
    rows, d = x_tm.shape
    d_mix = lw['w_out'].shape[0]
    dq = d_mix // 4
    assert n_seq % SUBLANES == 0 and n_step <= CHUNK and PAST_LEN % CHUNK == 0
    ins = [x_tm, sa, sc, sd, lw['ln_in_g'], lw['ln_in_b'], lw['w_in'], lw['conv_a_w'], lw['conv_a_b'],
           lw['ln_a_g'], lw['ln_a_b'], lw['ln_v_g'], lw['ln_v_b'], lw['ws_vec'], lw['bs_vec'],
           lw['conv_c_w'], lw['w_pool_bd'], lw['pool_scale'], lw['w_out'], lw['ln_mix_g'], lw['ln_mix_b']]
    out_shape = (
        jax.ShapeDtypeStruct((rows, d), F32),
        jax.ShapeDtypeStruct(((CONV_A_W - 1) * n_seq, dq), F32),
        jax.ShapeDtypeStruct(((CONV_C_W - 1) * n_seq, dq), F32),
        jax.ShapeDtypeStruct((POOL_PAST * n_seq, dq), F32),
        jax.ShapeDtypeStruct((rows, dq), F32),
    )
    return pl.pallas_call(
        functools.partial(_mixer_sample_kernel, first_layer, alpha, n_seq, n_step, d_mix),
        grid=(1,),
        in_specs=[_full_spec(a) for a in ins],
        out_specs=tuple(pl.BlockSpec(s.shape, lambda i: (0, 0)) for s in out_shape),
        out_shape=out_shape,
        scratch_shapes=[pltpu.VMEM((rows, d_mix), BF16)],
        compiler_params=_cparams(("arbitrary",)),
        name="mixer_sample",
    )(*ins)


def _stream_cast(streams):
    plans = []
    for pairs, stage, sems in streams:
        rows = stage.shape[1]
        plans.append(([(src, dst, r0) for src, dst in pairs for r0 in range(0, src.shape[0], rows)],
                      stage, sems))

    def copy(plan, c):
        chunks, stage, sems = plan
        src, _, r0 = chunks[c]
        slot = c % stage.shape[0]
        return pltpu.make_async_copy(src.at[pl.ds(r0, stage.shape[1]), :], stage.at[slot], sems.at[slot])

    for plan in plans:
        for c in range(min(plan[1].shape[0], len(plan[0]))):
            copy(plan, c).start()
    for c in range(max(len(plan[0]) for plan in plans)):
        for plan in plans:
            chunks, stage, _ = plan
            if c < len(chunks):
                _, dst, r0 = chunks[c]
                copy(plan, c).wait()
                dst[pl.ds(r0, stage.shape[1]), :] = stage[c % stage.shape[0]].astype(BF16)
                if c + stage.shape[0] < len(chunks):
                    copy(plan, c + stage.shape[0]).start()


def _ffn_dense_kernel(alpha, tf, n_main, x_ref, xe_ref, wg_hbm, wu_hbm, wd_hbm, g_ref, b_ref, o_ref, oe_ref,
                      wg_ref, wu_ref, wd_ref, stage_in, stage_out, sems):
    i = pl.program_id(0)

    @pl.when(i == 0)
    def _():
        _stream_cast([([(wg_hbm, wg_ref), (wu_hbm, wu_ref)], stage_in, sems.at[0]),
                      ([(wd_hbm, wd_ref)], stage_out, sems.at[1])])

    def run(xr, orf):
        x = xr[...]
        xb = x.astype(BF16)
        ffn = None
        for c0 in range(0, wg_ref.shape[1], tf):
            mid = _silu(_dot(xb, wg_ref[:, c0:c0 + tf])) * _dot(xb, wu_ref[:, c0:c0 + tf])
            part = _dot(mid.astype(BF16), wd_ref[c0:c0 + tf, :])
            ffn = part if ffn is None else ffn + part
        orf[...] = _ln(alpha * x + ffn, g_ref[...], b_ref[...])

    @pl.when(i < n_main)
    def _():
        run(x_ref, o_ref)

    @pl.when(i >= n_main)
    def _():
        run(xe_ref, oe_ref)


def _ffn_dense(x, x_extra, wg, wu, wd, g, b, alpha, tm, tf):
    t, d = x.shape
    te = x_extra.shape[0]
    dff = wg.shape[1]
    n_main, n_extra = t // tm, te // tm
    rows_in, rows_out = 128, 512
    assert t % tm == 0 and te % tm == 0 and dff % tf == 0 and d % rows_in == 0 and dff % rows_out == 0
    any_spec = pl.BlockSpec(memory_space=pl.ANY)
    main_blk = lambda i: (jnp.minimum(i, n_main - 1), 0)
    extra_blk = lambda i: (jnp.maximum(i - n_main, 0), 0)
    return pl.pallas_call(
        functools.partial(_ffn_dense_kernel, alpha, tf, n_main),
        grid=(n_main + n_extra,),
        in_specs=[pl.BlockSpec((tm, d), main_blk), pl.BlockSpec((tm, d), extra_blk),
                  any_spec, any_spec, any_spec,
                  pl.BlockSpec((1, d), lambda i: (0, 0)),
                  pl.BlockSpec((1, d), lambda i: (0, 0))],
        out_specs=(pl.BlockSpec((tm, d), main_blk), pl.BlockSpec((tm, d), extra_blk)),
        out_shape=(jax.ShapeDtypeStruct((t, d), F32), jax.ShapeDtypeStruct((te, d), F32)),
        scratch_shapes=[pltpu.VMEM(wg.shape, BF16), pltpu.VMEM(wu.shape, BF16), pltpu.VMEM(wd.shape, BF16),
                        pltpu.VMEM((3, rows_in, dff), F32), pltpu.VMEM((3, rows_out, d), F32),
                        pltpu.SemaphoreType.DMA((2, 3))],
        compiler_params=_cparams(("arbitrary",)),
        name="ffn_dense",
    )(x, x_extra, wg, wu, wd, g, b)


def _route_init(tm, cin_ref, carry_ref, before_ref):
    carry_ref[...] = cin_ref[...].astype(F32)
    row = lax.broadcasted_iota(I32, (tm, tm), 0)
    col = lax.broadcasted_iota(I32, (tm, tm), 1)
    before_ref[...] = jnp.where(row < col, 1.0, 0.0).astype(BF16)


def _route_block(tm, h, wcat_ref, mi_ref, mf_ref, cnt_ref, carry_ref, before_ref):
    h_hi = h.astype(BF16)
    h_lo = (h - h_hi.astype(F32)).astype(BF16)
    p_hi = _dot(h_hi, wcat_ref[...])
    p_lo = _dot(h_lo, wcat_ref[...])
    logits = (p_hi[:, 0:LANES] + (p_lo[:, 0:LANES] + p_hi[:, LANES:2 * LANES])) + p_lo[:, LANES:2 * LANES]
    lg = logits.T[0:N_EXPERTS, :]
    ex = lax.broadcasted_iota(I32, (N_EXPERTS, tm), 0).astype(F32)
    m1 = jnp.max(lg, axis=0, keepdims=True)
    i1 = jnp.min(jnp.where(lg == m1, ex, float(N_EXPERTS)), axis=0, keepdims=True)
    sel1 = ex == i1
    rest = jnp.where(sel1, -jnp.inf, lg)
    m2 = jnp.max(rest, axis=0, keepdims=True)
    i2 = jnp.min(jnp.where(rest == m2, ex, float(N_EXPERTS)), axis=0, keepdims=True)
    sel2 = ex == i2
    e2 = jnp.exp(m2 - m1)
    den = 1.0 + e2
    g1 = 1.0 / den
    g2 = e2 / den

    sel = jnp.where(sel1 | sel2, 1.0, 0.0)
    sel_pad = jnp.concatenate([sel, jnp.zeros_like(sel)], axis=0).astype(BF16)
    base = _dot(sel_pad, before_ref[...])[0:N_EXPERTS, :] + carry_ref[:, 0:1]
    r1 = jnp.sum(jnp.where(sel1, base, 0.0), axis=0, keepdims=True)
    r2 = jnp.sum(jnp.where(sel2, base, 0.0), axis=0, keepdims=True)
    carry_ref[...] = carry_ref[...] + jnp.sum(sel, axis=1, keepdims=True)

    meta = jnp.where(ex == 0, i1, jnp.where(ex == 1, i2, jnp.where(ex == 2, r1, r2)))
    mi_ref[...] = meta.astype(I32)
    gt = jnp.where(ex == 0, g1, jnp.where(ex == 1, g2, 0.0))
    gt = jnp.concatenate([gt, jnp.zeros((LANES - N_EXPERTS, tm), F32)], axis=0)
    mf_ref[...] = gt.T[:, 0:SUBLANES]
    cnt_ref[...] = carry_ref[...].astype(I32)


def _router_kernel(tm, h_ref, wcat_ref, cin_ref, mi_ref, mf_ref, cnt_ref, carry_ref, before_ref):
    @pl.when(pl.program_id(0) == 0)
    def _():
        _route_init(tm, cin_ref, carry_ref, before_ref)

    _route_block(tm, h_ref[...], wcat_ref, mi_ref, mf_ref, cnt_ref, carry_ref, before_ref)


def _router(h, wr_cat, counts_in):
    t, d = h.shape
    tm = min(TM_ROUTE, t)
    assert t % tm == 0 and N_EXPERTS == SUBLANES
    return pl.pallas_call(
        functools.partial(_router_kernel, tm),
        grid=(t // tm,),
        in_specs=[pl.BlockSpec((tm, d), lambda i: (i, 0)), _full_spec(wr_cat), _full_spec(counts_in)],
        out_specs=(pl.BlockSpec((SUBLANES, tm), lambda i: (0, i)),
                   pl.BlockSpec((tm, SUBLANES), lambda i: (i, 0)),
                   pl.BlockSpec((SUBLANES, LANES), lambda i: (0, 0))),
        out_shape=(jax.ShapeDtypeStruct((SUBLANES, t), I32),
                   jax.ShapeDtypeStruct((t, SUBLANES), F32),
                   jax.ShapeDtypeStruct((SUBLANES, LANES), I32)),
        scratch_shapes=[pltpu.VMEM((SUBLANES, LANES), F32), pltpu.VMEM((tm, tm), BF16)],
        compiler_params=_cparams(("arbitrary",)),
        name="router",
    )(h, wr_cat, counts_in)


def _pack_bf16_pairs(x):
    n = x.shape[1] // 2
    lo = lax.bitcast_convert_type(x[:, :n].astype(BF16).astype(F32), jnp.uint32)
    hi = lax.bitcast_convert_type(x[:, n:].astype(BF16).astype(F32), jnp.uint32)
    return (hi & jnp.uint32(0xFFFF0000)) | (lo >> 16)


def _unpack_bf16_pairs(u):
    lo = lax.bitcast_convert_type(u << 16, F32).astype(BF16)
    hi = lax.bitcast_convert_type(u & jnp.uint32(0xFFFF0000), F32).astype(BF16)
    return jnp.concatenate([lo, hi], axis=1)


def _row_copy(src, src_row, dst, dst_row, sem):
    return pltpu.make_async_copy(src.at[pl.ds(src_row, 1), :], dst.at[pl.ds(dst_row, 1), :], sem)


def _rows_wait(src, dst, n_rows, sem):
    pltpu.make_async_copy(src.at[pl.ds(0, n_rows), :], dst.at[pl.ds(0, n_rows), :], sem).wait()


def _dispatch_kernel(tm, tm_expert, n_blocks, blk_ranges, zi_ref, dest_ref, *rest):
    n_src = len(blk_ranges)
    h_refs = rest[:n_src]
    xs_ref, xbuf, pbuf, zblk, lsems, rsems, zsem = rest[n_src:]
    i = pl.program_id(0)
    n = pl.num_programs(0)

    def block_load(g, start):
        slot = lax.rem(g, N_STAGE)
        for h_ref, (b0, b1) in zip(h_refs, blk_ranges):
            @pl.when((g >= b0) & (g < b1))
            def _(h_ref=h_ref, b0=b0):
                cp = pltpu.make_async_copy(h_ref.at[pl.ds((g - b0) * tm, tm), :], xbuf.at[slot],
                                           lsems.at[slot])
                if start:
                    cp.start()
                else:
                    cp.wait()

    def rows_wait(g):
        slot = lax.rem(g, N_STAGE)
        for _k in range(TOP_K):
            _rows_wait(pbuf.at[slot], xs_ref, tm, rsems.at[slot])

    @pl.when(i == 0)
    def _():
        block_load(i, True)

    @pl.when(i >= N_STAGE - 1)
    def _():
        rows_wait(i - (N_STAGE - 1))

    @pl.when(i + 1 < n)
    def _():
        block_load(i + 1, True)

    block_load(i, False)
    cur = lax.rem(i, N_STAGE)
    pbuf[cur] = _pack_bf16_pairs(xbuf[cur])

    def issue(j, carry):
        r0 = pl.multiple_of(j * SUBLANES, SUBLANES)
        for u in range(SUBLANES):
            for k in range(TOP_K):
                dst = dest_ref[0, 0, j * SUBLANES + (k * tm + u)]
                _row_copy(pbuf.at[cur], r0 + u, xs_ref, dst, rsems.at[cur]).start(priority=k)
        return carry

    lax.fori_loop(0, tm // SUBLANES, issue, 0)

    @pl.when(i == n - 1)
    def _():
        for back in range(N_STAGE - 2, -1, -1):
            @pl.when(i >= back)
            def _(back=back):
                rows_wait(i - back)

    @pl.when(i == 0)
    def _():
        zblk[...] = jnp.zeros_like(zblk)
        for e in range(N_EXPERTS):
            lo = zi_ref[e] + zi_ref[N_EXPERTS + e]
            hi = zi_ref[e] + zi_ref[2 * N_EXPERTS + e]

            def zissue(r, carry):
                _row_copy(zblk, 0, xs_ref, r, zsem).start()
                return carry

            def zdrain(r, carry):
                _row_copy(zblk, 0, xs_ref, r, zsem).wait()
                return carry

            lax.fori_loop(lo, hi, zissue, 0)
            lax.fori_loop(lo, hi, zdrain, 0)

        def bcopy(b):
            return pltpu.make_async_copy(zblk, xs_ref.at[pl.ds(b * tm_expert, tm_expert), :], zsem)

        def bissue(b, carry):
            bcopy(b).start()
            return carry

        def bdrain(b, carry):
            bcopy(b).wait()
            return carry

        lax.fori_loop(zi_ref[3 * N_EXPERTS], n_blocks, bissue, 0)
        lax.fori_loop(zi_ref[3 * N_EXPERTS], n_blocks, bdrain, 0)


def _dispatch(hs_list, dest_blk, zinfo, n_blocks, tm_expert):
    d = hs_list[0].shape[1]
    nb = dest_blk.shape[0]
    tm = dest_blk.shape[2] // TOP_K
    blk_ranges, b0 = [], 0
    for h in hs_list:
        assert h.shape[0] % tm == 0 and h.shape[0] >= tm
        blk_ranges.append((b0, b0 + h.shape[0] // tm))
        b0 = blk_ranges[-1][1]
    assert b0 == nb
    any_spec = pl.BlockSpec(memory_space=pl.ANY)
    return pl.pallas_call(
        functools.partial(_dispatch_kernel, tm, tm_expert, n_blocks, tuple(blk_ranges)),
        grid_spec=pltpu.PrefetchScalarGridSpec(
            num_scalar_prefetch=1,
            grid=(nb,),
            in_specs=[pl.BlockSpec((1, 1, TOP_K * tm), lambda i, zi: (i, 0, 0), memory_space=pltpu.SMEM)]
            + [any_spec] * len(hs_list),
            out_specs=any_spec,
            scratch_shapes=[pltpu.VMEM((N_STAGE, tm, d), F32), pltpu.VMEM((N_STAGE, tm, d // 2), jnp.uint32),
                            pltpu.VMEM((tm_expert, d // 2), jnp.uint32),
                            pltpu.SemaphoreType.DMA((N_STAGE,)), pltpu.SemaphoreType.DMA((N_STAGE,)),
                            pltpu.SemaphoreType.DMA(())],
        ),
        out_shape=jax.ShapeDtypeStruct((n_blocks * tm_expert, d // 2), jnp.uint32),
        compiler_params=_cparams(("arbitrary",)),
        name="moe_dispatch",
    )(zinfo, dest_blk, *hs_list)


def _ffn_moe_kernel(tf, be_ref, nu_ref, x_ref, wg_hbm, wu_hbm, wd_hbm, y_ref,
                    wg_ref, wu_ref, wd_ref, stage_in, stage_out, sems):
    b = pl.program_id(0)
    nb = pl.num_programs(0)
    d, dff = wg_ref.shape
    n_slots = stage_in.shape[0]
    rows_in = stage_in.shape[1]
    rows_out = stage_out.shape[1]
    n_in, n_out = 2 * (d // rows_in), dff // rows_out
    e = be_ref[b]
    fresh = (b == 0) | (e != be_ref[jnp.maximum(b - 1, 0)])
    e_next = be_ref[jnp.minimum(b + 1, nb - 1)]

    def chunk_in(ex, n):
        src, dst = (wg_hbm, wg_ref) if n < n_in // 2 else (wu_hbm, wu_ref)
        rows = pl.ds((n % (n_in // 2)) * rows_in, rows_in)
        stage = stage_in.at[n % n_slots]
        return pltpu.make_async_copy(src.at[ex, rows, :], stage, sems.at[0, n % n_slots]), stage, dst.at[rows, :]

    def chunk_out(ex, n):
        rows = pl.ds(n * rows_out, rows_out)
        stage = stage_out.at[n % n_slots]
        return pltpu.make_async_copy(wd_hbm.at[ex, rows, :], stage, sems.at[1, n % n_slots]), stage, wd_ref.at[rows, :]

    def request_first(ex):
        for n in range(n_slots):
            chunk_in(ex, n)[0].start()
            chunk_out(ex, n)[0].start()

    @pl.when((b < nu_ref[0]) & fresh)
    def _():
        @pl.when(b == 0)
        def _():
            request_first(e)

        for n in range(n_in):
            for chunk, count in ((chunk_in, n_in), (chunk_out, n_out)):
                if n < count:
                    copy, staged, dst = chunk(e, n)
                    copy.wait()
                    dst[...] = staged[...].astype(BF16)
                    if n + n_slots < count:
                        chunk(e, n + n_slots)[0].start()

    @pl.when((b + 1 < nu_ref[0]) & (e_next != e))
    def _():
        request_first(e_next)

    @pl.when(b < nu_ref[0])
    def _():
        xb = _unpack_bf16_pairs(x_ref[...])
        ffn = None
        for c0 in range(0, dff, tf):
            mid = _silu(_dot(xb, wg_ref[:, c0:c0 + tf])) * _dot(xb, wu_ref[:, c0:c0 + tf])
            part = _dot(mid.astype(BF16), wd_ref[c0:c0 + tf, :])
            ffn = part if ffn is None else ffn + part
        y_ref[...] = ffn

    @pl.when(b >= nu_ref[0])
    def _():
        y_ref[...] = jnp.zeros_like(y_ref)


def _ffn_moe(xs, wg, wu, wd, block_e, n_used, tm, tf):
    cap, d = xs.shape[0], wg.shape[1]
    dff = wg.shape[2]
    rows_in, rows_out = 128, 512
    assert cap % tm == 0 and dff % tf == 0 and d % rows_in == 0 and dff % rows_out == 0
    assert min(2 * (d // rows_in), dff // rows_out) >= N_WSTAGE
    any_spec = pl.BlockSpec(memory_space=pl.ANY)
    return pl.pallas_call(
        functools.partial(_ffn_moe_kernel, tf),
        grid_spec=pltpu.PrefetchScalarGridSpec(
            num_scalar_prefetch=2,
            grid=(cap // tm,),
            in_specs=[pl.BlockSpec((tm, d // 2), lambda b, be, nu: (jnp.minimum(b, nu[0] - 1), 0)),
                      any_spec, any_spec, any_spec],
            out_specs=pl.BlockSpec((tm, d), lambda b, be, nu: (b, 0)),
            scratch_shapes=[pltpu.VMEM((d, dff), BF16), pltpu.VMEM((d, dff), BF16), pltpu.VMEM((dff, d), BF16),
                            pltpu.VMEM((N_WSTAGE, rows_in, dff), F32), pltpu.VMEM((N_WSTAGE, rows_out, d), F32),
                            pltpu.SemaphoreType.DMA((2, N_WSTAGE))],
        ),
        out_shape=jax.ShapeDtypeStruct((cap, d), F32),
        compiler_params=_cparams(("arbitrary",)),
        name="ffn_moe",
    )(block_e, n_used, xs, wg, wu, wd)


def _combine_kernel(alpha, tm, dcur_ref, dnext_ref, h_ref, gate_ref, g_ref, b_ref, ys_ref, o_ref, ybuf, sems):
    i = pl.program_id(0)
    n = pl.num_programs(0)

    def issue(dref, slot):
        def body(j, carry):
            r0 = pl.multiple_of(j * SUBLANES, SUBLANES)
            for u in range(SUBLANES):
                for k in range(TOP_K):
                    src = dref[0, 0, j * SUBLANES + (k * tm + u)]
                    _row_copy(ys_ref, src, ybuf.at[slot, k], r0 + u, sems.at[slot]).start(priority=k)
            return carry

        lax.fori_loop(0, tm // SUBLANES, body, 0)

    @pl.when(i == 0)
    def _():
        issue(dcur_ref, 0)

    @pl.when(i + 1 < n)
    def _():
        issue(dnext_ref, (i + 1) % 2)

    slot = i % 2
    for k in range(TOP_K):
        _rows_wait(ys_ref, ybuf.at[slot, k], tm, sems.at[slot])

    gates = gate_ref[...]
    ffn = gates[:, 0:1] * ybuf[slot, 0] + gates[:, 1:2] * ybuf[slot, 1]
    o_ref[...] = _ln(alpha * h_ref[...] + ffn, g_ref[...], b_ref[...])


def _combine(h, dest_blk, gates, ys, g, b, alpha):
    t, d = h.shape
    nb = dest_blk.shape[0]
    tm = dest_blk.shape[2] // TOP_K
    dest_spec = lambda imap: pl.BlockSpec((1, 1, TOP_K * tm), imap, memory_space=pltpu.SMEM)
    return pl.pallas_call(
        functools.partial(_combine_kernel, alpha, tm),
        grid=(nb,),
        in_specs=[dest_spec(lambda i: (i, 0, 0)),
                  dest_spec(lambda i: (jnp.minimum(i + 1, nb - 1), 0, 0)),
                  pl.BlockSpec((tm, d), lambda i: (i, 0)),
                  pl.BlockSpec((tm, SUBLANES), lambda i: (i, 0)),
                  pl.BlockSpec((1, d), lambda i: (0, 0)),
                  pl.BlockSpec((1, d), lambda i: (0, 0)),
                  pl.BlockSpec(memory_space=pl.ANY)],
        out_specs=pl.BlockSpec((tm, d), lambda i: (i, 0)),
        out_shape=jax.ShapeDtypeStruct((t, d), F32),
        scratch_shapes=[pltpu.VMEM((2, TOP_K, tm, d), F32), pltpu.SemaphoreType.DMA((2,))],
        compiler_params=_cparams(("arbitrary",)),
        name="moe_combine",
    )(dest_blk, dest_blk, h, gates, g, b, ys)


def _ffn_routed(hs_list, first_routed, mw, g, b, alpha, tm_expert):
    routed = [first_routed[:2]]
    counts_in = first_routed[2]
    for h in hs_list[1:]:
        meta_i, gates, counts_in = _router(h, mw['w_router_cat'], counts_in)
        routed.append((meta_i, gates))
    counts = counts_in[:N_EXPERTS, 0]
    padded = (counts + tm_expert - 1) // tm_expert * tm_expert
    pad_end = jnp.cumsum(padded)
    pad_start = (pad_end - padded).astype(I32)
    n_assign = sum(h.shape[0] for h in hs_list) * TOP_K
    n_blocks = -(-n_assign // tm_expert) + N_EXPERTS
    cap = n_blocks * tm_expert
    n_used = (pad_end[-1] // tm_expert).astype(I32)
    blk_start = jnp.minimum(jnp.arange(n_blocks, dtype=I32), n_used - 1) * tm_expert
    block_e = jnp.minimum(jnp.sum(blk_start[:, None] >= pad_end[None, :], axis=1), N_EXPERTS - 1).astype(I32)
    zinfo = jnp.concatenate([pad_start, counts, padded, n_used.reshape(1)]).astype(I32)

    dests = []
    for h, (meta_i, _) in zip(hs_list, routed):
        dest = meta_i[TOP_K:2 * TOP_K, :]
        for e in range(N_EXPERTS):
            dest = dest + jnp.where(meta_i[0:TOP_K, :] == e, pad_start[e], 0)
        nb = h.shape[0] // TM_MOVE
        dests.append(dest.reshape(TOP_K, nb, TM_MOVE).transpose(1, 0, 2).reshape(nb, 1, TOP_K * TM_MOVE))
    xs = _dispatch(hs_list, jnp.concatenate(dests, axis=0), zinfo, n_blocks, tm_expert)
    ys = _ffn_moe(xs, mw['w_e_gate'], mw['w_e_up'], mw['w_e_down'], block_e, n_used.reshape(1), tm_expert,
                  TF_EXPERT)
    return [_combine(h, dest_blk, gates, ys, g, b, alpha)
            for h, dest_blk, (_, gates) in zip(hs_list, dests, routed)]


def _prep_layer(i, p, n_step):
    dq = p['conv_a_w'].shape[2]
    hd = dq // N_HEADS_B
    tril = jnp.tril(jnp.ones((CHUNK, CHUNK), dtype=bool))
    w_s = p['w_s'][i]
    b_s = p['b_s'][i]
    wp = p['w_pool'][i]
    ng, gc = wp.shape[0], wp.shape[1]
    w_pool_bd = (jnp.eye(ng, dtype=F32)[:, None, :, None] * wp[:, :, None, :]).reshape(ng * gc, ng * gc)
    row = lambda a: a.reshape(1, -1)
    return dict(
        ln_in_g=row(p['ln_in_g']), ln_in_b=row(p['ln_in_b']),
        w_in=p['w_in'][i].astype(BF16),
        conv_a_w=p['conv_a_w'][i], conv_a_b=row(p['conv_a_b'][i]),
        ln_a_g=row(p['ln_a_g'][i]), ln_a_b=row(p['ln_a_b'][i]),
        ln_v_g=row(p['ln_v_g'][i]), ln_v_b=row(p['ln_v_b'][i]),
        ws_tril=jnp.where(tril[None], w_s, 0).astype(BF16),
        bs_mat=jnp.repeat(b_s.T, hd, axis=1),
        ws_vec=jnp.repeat(jnp.where(tril[None], w_s, 0)[:, :n_step, :n_step].transpose(1, 2, 0)
                          .reshape(n_step * n_step, N_HEADS_B), hd, axis=1),
        bs_vec=jnp.repeat(b_s[:, :n_step].T, hd, axis=1),
        conv_c_w=p['conv_c_w'][i],
        w_pool_bd=w_pool_bd.astype(BF16), pool_scale=row(p['pool_scale'][i]),
        w_out=p['w_out'][i].astype(BF16),
        ln_mix_g=row(p['ln_mix_g'][i]), ln_mix_b=row(p['ln_mix_b'][i]),
    )


def kernel(x_prompt, x_sample, state_conv_a, state_conv_c, state_pool_d, ln_in_g, ln_in_b, w_in, conv_a_w,
           conv_a_b, ln_a_g, ln_a_b, ln_v_g, ln_v_b, w_s, b_s, conv_c_w, w_pool, pool_scale, w_out,
           ln_mix_g, ln_mix_b, w_ff_gate, w_ff_up, w_ff_down, w_router, w_e_gate, w_e_up, w_e_down,
           ln_ffn_g, ln_ffn_b):
    p = dict(ln_in_g=ln_in_g, ln_in_b=ln_in_b, w_in=w_in, conv_a_w=conv_a_w, conv_a_b=conv_a_b,
             ln_a_g=ln_a_g, ln_a_b=ln_a_b, ln_v_g=ln_v_g, ln_v_b=ln_v_b, w_s=w_s, b_s=b_s,
             conv_c_w=conv_c_w, w_pool=w_pool, pool_scale=pool_scale, w_out=w_out,
             ln_mix_g=ln_mix_g, ln_mix_b=ln_mix_b)
    depth = w_in.shape[0]
    bsz, seq, d = x_prompt.shape
    n_seq, n_step, _ = x_sample.shape
    dq = conv_a_w.shape[2]
    alpha = float((2.0 * depth) ** 0.25)

    hp = x_prompt
    hs = x_sample.transpose(1, 0, 2).reshape(n_step * n_seq, d)
    outs = {k: [] for k in ('pa', 'pc', 'pd', 'pv', 'sa', 'sc', 'sd', 'sv')}
    for i in range(depth):
        lw = _prep_layer(i, p, n_step)
        j = i // 2
        routed = i % 2 == 1
        if routed:
            wr = jnp.pad(w_router[j], ((0, 0), (0, LANES - N_EXPERTS)))
            wr_hi = wr.astype(BF16)
            wr_lo = (wr - wr_hi.astype(F32)).astype(BF16)
            wr_cat = jnp.concatenate([wr_hi, wr_lo], axis=1)
        hp, pa, pc, pd, pv, *route_p = _mixer_prompt(hp, lw, i == 0, alpha, wr_cat if routed else None)
        tm_state = lambda s: s.transpose(1, 0, 2).reshape(-1, dq)
        hs, sa, sc, sd, sv = _mixer_sample(hs, tm_state(state_conv_a[i]), tm_state(state_conv_c[i]),
                                           tm_state(state_pool_d[i]), lw, i == 0, alpha, n_seq, n_step)
        lg, lb = ln_ffn_g[i].reshape(1, -1), ln_ffn_b[i].reshape(1, -1)
        if not routed:
            hp, hs = _ffn_dense(hp.reshape(bsz * seq, d), hs, w_ff_gate[j], w_ff_up[j], w_ff_down[j], lg, lb,
                                alpha, TM_DENSE, TF_DENSE)
            hp = hp.reshape(bsz, seq, d)
        else:
            mw = dict(w_router_cat=wr_cat, w_e_gate=w_e_gate[j], w_e_up=w_e_up[j], w_e_down=w_e_down[j])
            hp, hs = _ffn_routed([hp.reshape(bsz * seq, d), hs], route_p, mw, lg, lb, alpha, TM_EXPERT)
            hp = hp.reshape(bsz, seq, d)
        back = lambda a: a.reshape(-1, n_seq, dq).transpose(1, 0, 2)
        for k, val in zip(('pa', 'pc', 'pd', 'pv', 'sa', 'sc', 'sd', 'sv'),
                          (pa, pc, pd, pv, back(sa), back(sc), back(sd), back(sv))):
            outs[k].append(val)
    y_sample = hs.reshape(n_step, n_seq, d).transpose(1, 0, 2)
    st = lambda k: jnp.stack(outs[k])
    return (hp, y_sample, st('pa'), st('pc'), st('pd'), st('pv'), st('sa'), st('sc'), st('sd'), st('sv'))
```

```python
import functools

import numpy as np
import jax
import jax.numpy as jnp
from jax import lax
from jax.experimental import pallas as pl
from jax.experimental.pallas import tpu as pltpu

F32 = jnp.float32
BF16 = jnp.bfloat16
I32 = jnp.int32

PAST_LEN = 16384
CHUNK = 128
N_HEADS_B = 4
POOL_WINDOWS = (2, 4, 8, 16)
POOL_PAST = 15
CONV_A_W = 31
CONV_C_W = 3
N_EXPERTS = 8
TOP_K = 2
LN_EPS = 1e-5
SQRT_HALF = float(np.sqrt(0.5).astype(np.float32))

LANES = 128
SUBLANES = 8
VMEM_LIMIT_BYTES = 56 * 1024 * 1024

HIST_A = 32
HIST_C = 8
HIST_D = 24

TL_PROMPT = 512
TM_DENSE = 512
TF_DENSE = 1792
TF_EXPERT = 1792
TM_ROUTE = 512
TM_MOVE = 256
TM_EXPERT = 512
N_STAGE = 3
N_WSTAGE = 5


def _ln(x, g, b):
    mu = jnp.mean(x, axis=-1, keepdims=True)
    xc = x - mu
    var = jnp.mean(xc * xc, axis=-1, keepdims=True)
    return xc * lax.rsqrt(var + LN_EPS) * g + b


def _gelu(x):
    return 0.5 * x * (1.0 + lax.erf(x * SQRT_HALF))


def _silu(x):
    return x * jax.nn.sigmoid(x)


def _dot(a, b):
    return jnp.dot(a, b, preferred_element_type=F32)


def _cparams(sem):
    return pltpu.CompilerParams(dimension_semantics=sem, vmem_limit_bytes=VMEM_LIMIT_BYTES)


def _mixer_prompt_kernel(first_layer, route, alpha, tl, d_mix,
                         x_ref, lng_ref, lnb_ref, win_ref, caw_ref, cab_ref, lag_ref, lab_ref,
                         lvg_ref, lvb_ref, ws_ref, bsm_ref, ccw_ref, wpool_ref, pscale_ref,
                         wout_ref, lmg_ref, lmb_ref, *refs):
    if route:
        wcat_ref, cin_ref = refs[:2]
        refs = refs[2:]
    h_out, sa_out, sc_out, sd_out, v_out = refs[:5]
    refs = refs[5:]
    if route:
        mi_ref, mf_ref, cnt_ref = refs[:3]
        refs = refs[3:]
    ext_a, ext_c, ext_d, cat_ref, z_ref, hb_ref, mix_ref = refs[:7]
    dq = d_mix // 4
    j = pl.program_id(1)
    nj = pl.num_programs(1)
    if route:
        carry_ref, before_ref = refs[7:]

        @pl.when((pl.program_id(0) == 0) & (j == 0))
        def _():
            _route_init(tl, cin_ref, carry_ref, before_ref)

    @pl.when(j == 0)
    def _():
        ext_a[0, 0:HIST_A, :] = jnp.zeros((HIST_A, dq), F32)
        ext_c[0:HIST_C, :] = jnp.zeros((HIST_C, dq), F32)
        ext_d[0:HIST_D, :] = jnp.zeros((HIST_D, dq), F32)

    d = x_ref.shape[2]
    rb = CHUNK

    def resid(r0, n):
        return h_out[0, r0:r0 + n, :] if first_layer else x_ref[0, r0:r0 + n, :]

    for r0 in range(0, tl, rb):
        x = x_ref[0, r0:r0 + rb, :]
        if first_layer:
            x = _ln(x, lng_ref[...], lnb_ref[...])
            h_out[0, r0:r0 + rb, :] = x
        hb_ref[r0:r0 + rb, :] = x.astype(BF16)

    z_ref[...] = _dot(hb_ref[...], win_ref[...])

    def zcol(r0, i, n=rb):
        return z_ref[r0:r0 + n, i * dq:(i + 1) * dq]

    chunk_rows = range(0, tl, CHUNK)

    for r0 in chunk_rows:
        ext_a[0, HIST_A + r0:HIST_A + r0 + rb, :] = zcol(r0, 0) * jax.nn.sigmoid(zcol(r0, 1))
    n_ext = HIST_A + tl
    for r0 in range(0, n_ext - SUBLANES, rb):
        n = min(rb, n_ext - SUBLANES - r0)
        blk = ext_a[0, r0:r0 + n + SUBLANES, :]
        for s in range(1, SUBLANES):
            ext_a[s, r0:r0 + n, :] = pltpu.roll(blk, n + SUBLANES - s, axis=0)[0:n, :]
    off_a = HIST_A - (CONV_A_W - 1)
    for q0 in range(0, tl, 64):
        acc = jnp.zeros((64, dq), F32)
        for k in range(CONV_A_W):
            s = (off_a + k) % SUBLANES
            row = off_a + k - s + q0
            acc = acc + ext_a[s, row:row + 64, :] * caw_ref[k:k + 1, :]
        y = _silu(_ln(acc + cab_ref[...], lag_ref[...], lab_ref[...]))
        cat_ref[q0:q0 + 64, 0:dq] = y.astype(BF16)

    hd = dq // N_HEADS_B
    lane_head = lax.broadcasted_iota(I32, (CHUNK, dq), 1) // hd
    for r0 in chunk_rows:
        v = _ln(_gelu(zcol(r0, 3)), lvg_ref[...], lvb_ref[...])
        sm = bsm_ref[...]
        for hh in range(N_HEADS_B):
            vm = jnp.where(lane_head == hh, v, 0.0).astype(BF16)
            sm = sm + _dot(ws_ref[hh], vm)
        cat_ref[r0:r0 + CHUNK, dq:2 * dq] = (_gelu(zcol(r0, 2)) * sm).astype(BF16)
        if r0 == tl - CHUNK:
            @pl.when(j == nj - 1)
            def _(v=v):
                v_out[0] = v

    off_c = HIST_C - (CONV_C_W - 1)
    for r0 in chunk_rows:
        ext_c[HIST_C + r0:HIST_C + r0 + rb, :] = zcol(r0, 6) * zcol(r0, 4)
    for r0 in chunk_rows:
        conv_c = jnp.zeros((CHUNK, dq), F32)
        for k in range(CONV_C_W):
            conv_c = conv_c + ext_c[off_c + r0 + k:off_c + r0 + k + CHUNK, :] * ccw_ref[k:k + 1, :]
        cat_ref[r0:r0 + CHUNK, 2 * dq:3 * dq] = (zcol(r0, 5) * conv_c).astype(BF16)

    gc = dq // len(POOL_WINDOWS)
    lane_grp = lax.broadcasted_iota(I32, (CHUNK, dq), 1) // gc
    lead = 2 * SUBLANES
    for r0 in chunk_rows:
        ext_d[HIST_D + r0:HIST_D + r0 + rb, :] = zcol(r0, 7)
    for r0 in chunk_rows:
        e = ext_d[HIST_D + r0 - lead:HIST_D + r0 + CHUNK, :]
        s2 = e + pltpu.roll(e, 1, axis=0)
        s4 = s2 + pltpu.roll(s2, 2, axis=0)
        s8 = s4 + pltpu.roll(s4, 4, axis=0)
        s16 = s8 + pltpu.roll(s8, 8, axis=0)
        sums = (s2, s4, s8, s16)
        pos = (j * tl + r0 + lax.broadcasted_iota(I32, (CHUNK, 1), 0)).astype(F32)
        mean = jnp.zeros((CHUNK, dq), F32)
        for g, w in enumerate(POOL_WINDOWS):
            inv = 1.0 / jnp.minimum(pos + 1.0, float(w))
            mean = jnp.where(lane_grp == g, sums[g][lead:lead + CHUNK, :] * inv, mean)
        dd = (mean - e[lead:lead + CHUNK, :]).astype(BF16)
        cat_ref[r0:r0 + CHUNK, 3 * dq:4 * dq] = (_dot(dd, wpool_ref[...]) * pscale_ref[...]).astype(BF16)

    mix_ref[...] = _dot(cat_ref[...], wout_ref[...])
    for r0 in range(0, tl, 64):
        h_out[0, r0:r0 + 64, :] = _ln(alpha * resid(r0, 64) + mix_ref[r0:r0 + 64, :], lmg_ref[...], lmb_ref[...])
    if route:
        _route_block(tl, h_out[0], wcat_ref, mi_ref, mf_ref, cnt_ref, carry_ref, before_ref)

    @pl.when(j == nj - 1)
    def _():
        sa_out[0] = ext_a[0, HIST_A + tl - (CONV_A_W - 1):HIST_A + tl, :]
        sc_out[0] = ext_c[HIST_C + tl - (CONV_C_W - 1):HIST_C + tl, :]
        sd_out[0] = ext_d[HIST_D + tl - POOL_PAST:HIST_D + tl, :]

    ext_a[0, 0:HIST_A, :] = ext_a[0, tl:tl + HIST_A, :]
    ext_c[0:HIST_C, :] = ext_c[tl:tl + HIST_C, :]
    ext_d[0:HIST_D, :] = ext_d[tl:tl + HIST_D, :]


def _full_spec(arr):
    nd = arr.ndim
    return pl.BlockSpec(arr.shape, lambda *_: (0,) * nd)


def _mixer_prompt(x, lw, first_layer, alpha, route_w=None):
    bsz, seq, d = x.shape
    d_mix = lw['w_out'].shape[0]
    dq = d_mix // 4
    tl = TL_PROMPT
    nj = seq // tl
    route = route_w is not None
    assert seq % tl == 0 and tl % CHUNK == 0 and seq >= CHUNK
    params = [lw['ln_in_g'], lw['ln_in_b'], lw['w_in'], lw['conv_a_w'], lw['conv_a_b'], lw['ln_a_g'],
              lw['ln_a_b'], lw['ln_v_g'], lw['ln_v_b'], lw['ws_tril'], lw['bs_mat'], lw['conv_c_w'],
              lw['w_pool_bd'], lw['pool_scale'], lw['w_out'], lw['ln_mix_g'], lw['ln_mix_b']]
    out_shape = [
        jax.ShapeDtypeStruct((bsz, seq, d), F32),
        jax.ShapeDtypeStruct((bsz, CONV_A_W - 1, dq), F32),
        jax.ShapeDtypeStruct((bsz, CONV_C_W - 1, dq), F32),
        jax.ShapeDtypeStruct((bsz, POOL_PAST, dq), F32),
        jax.ShapeDtypeStruct((bsz, CHUNK, dq), F32),
    ]
    state_spec = lambda rows: pl.BlockSpec((1, rows, dq), lambda b, j: (b, 0, 0))
    out_specs = [pl.BlockSpec((1, tl, d), lambda b, j: (b, j, 0)),
                 state_spec(CONV_A_W - 1), state_spec(CONV_C_W - 1), state_spec(POOL_PAST), state_spec(CHUNK)]
    scratch = [pltpu.VMEM((SUBLANES, HIST_A + tl, dq), F32), pltpu.VMEM((HIST_C + tl, dq), F32),
               pltpu.VMEM((HIST_D + tl, dq), F32), pltpu.VMEM((tl, d_mix), BF16),
               pltpu.VMEM((tl, lw['w_in'].shape[1]), F32), pltpu.VMEM((tl, d), BF16),
               pltpu.VMEM((tl, d), F32)]
    if route:
        assert N_EXPERTS == SUBLANES
        params += [route_w, jnp.zeros((SUBLANES, LANES), I32)]
        out_shape += [jax.ShapeDtypeStruct((SUBLANES, bsz * seq), I32),
                      jax.ShapeDtypeStruct((bsz * seq, SUBLANES), F32),
                      jax.ShapeDtypeStruct((SUBLANES, LANES), I32)]
        out_specs += [pl.BlockSpec((SUBLANES, tl), lambda b, j: (0, b * nj + j)),
                      pl.BlockSpec((tl, SUBLANES), lambda b, j: (b * nj + j, 0)),
                      pl.BlockSpec((SUBLANES, LANES), lambda b, j: (0, 0))]
        scratch += [pltpu.VMEM((SUBLANES, LANES), F32), pltpu.VMEM((tl, tl), BF16)]
    return pl.pallas_call(
        functools.partial(_mixer_prompt_kernel, first_layer, route, alpha, tl, d_mix),
        grid=(bsz, nj),
        in_specs=[pl.BlockSpec((1, tl, d), lambda b, j: (b, j, 0))] + [_full_spec(p) for p in params],
        out_specs=tuple(out_specs),
        out_shape=tuple(out_shape),
        scratch_shapes=scratch,
        compiler_params=_cparams(("arbitrary", "arbitrary")),
        name="mixer_prompt",
    )(x, *params)


def _mixer_sample_kernel(first_layer, alpha, n_seq, n_step, d_mix,
                         x_ref, sa_ref, sc_ref, sd_ref, lng_ref, lnb_ref, win_ref, caw_ref, cab_ref,
                         lag_ref, lab_ref, lvg_ref, lvb_ref, wsv_ref, bsv_ref, ccw_ref, wpool_ref,
                         pscale_ref, wout_ref, lmg_ref, lmb_ref,
                         h_out, sa_out, sc_out, sd_out, v_out, cat_ref):
    dq = d_mix // 4
    x = x_ref[...]
    h = _ln(x, lng_ref[...], lnb_ref[...]) if first_layer else x
    hb = h.astype(BF16)

    def proj(i):
        return _dot(hb, win_ref[:, i * dq:(i + 1) * dq])

    def slab(val, i):
        return val[i * n_seq:(i + 1) * n_seq, :]

    def ext_slabs(state_ref, n_past, cur):
        return ([state_ref[i * n_seq:(i + 1) * n_seq, :] for i in range(n_past)]
                + [slab(cur, i) for i in range(n_step)])

    def store_state(out_ref, slabs, n_keep):
        for i, sl in enumerate(slabs[len(slabs) - n_keep:]):
            out_ref[i * n_seq:(i + 1) * n_seq, :] = sl

    a_glu = proj(0) * jax.nn.sigmoid(proj(1))
    ea = ext_slabs(sa_ref, CONV_A_W - 1, a_glu)
    for l in range(n_step):
        acc = jnp.zeros((n_seq, dq), F32)
        for k in range(CONV_A_W):
            acc = acc + ea[l + k] * caw_ref[k:k + 1, :]
        y = _silu(_ln(acc + cab_ref[...], lag_ref[...], lab_ref[...]))
        cat_ref[l * n_seq:(l + 1) * n_seq, 0:dq] = y.astype(BF16)
    store_state(sa_out, ea, CONV_A_W - 1)

    u = _gelu(proj(2))
    v = _ln(_gelu(proj(3)), lvg_ref[...], lvb_ref[...])
    v_out[...] = v
    for l in range(n_step):
        s = jnp.zeros((n_seq, dq), F32) + bsv_ref[l:l + 1, :]
        for m in range(l + 1):
            s = s + slab(v, m) * wsv_ref[l * n_step + m:l * n_step + m + 1, :]
        cat_ref[l * n_seq:(l + 1) * n_seq, dq:2 * dq] = (slab(u, l) * s).astype(BF16)

    c_x = proj(4)
    c_b = proj(5)
    c_c = proj(6)
    gx = c_c * c_x
    ec = ext_slabs(sc_ref, CONV_C_W - 1, gx)
    for l in range(n_step):
        acc = jnp.zeros((n_seq, dq), F32)
        for k in range(CONV_C_W):
            acc = acc + ec[l + k] * ccw_ref[k:k + 1, :]
        cat_ref[l * n_seq:(l + 1) * n_seq, 2 * dq:3 * dq] = (slab(c_b, l) * acc).astype(BF16)
    store_state(sc_out, ec, CONV_C_W - 1)

    d_in = proj(7)
    ed = ext_slabs(sd_ref, POOL_PAST, d_in)
    memo = {}

    def wsum(i, w):
        if i < 0:
            return None
        if w == 1:
            return ed[i]
        if (i, w) not in memo:
            a, b = wsum(i, w // 2), wsum(i - w // 2, w // 2)
            memo[(i, w)] = a if b is None else a + b
        return memo[(i, w)]

    gc = dq // len(POOL_WINDOWS)
    lane_grp = lax.broadcasted_iota(I32, (n_seq, dq), 1) // gc
    for l in range(n_step):
        mean = jnp.zeros((n_seq, dq), F32)
        for g, w in enumerate(POOL_WINDOWS):
            count = min(PAST_LEN + l + 1, w)
            mean = jnp.where(lane_grp == g, wsum(POOL_PAST + l, w) * (1.0 / count), mean)
        dd = (mean - slab(d_in, l)).astype(BF16)
        cat_ref[l * n_seq:(l + 1) * n_seq, 3 * dq:4 * dq] = (
            _dot(dd, wpool_ref[...]) * pscale_ref[...]).astype(BF16)
    store_state(sd_out, ed, POOL_PAST)

    mix = _dot(cat_ref[...], wout_ref[...])
    h_out[...] = _ln(alpha * h + mix, lmg_ref[...], lmb_ref[...])


def _mixer_sample(x_tm, sa, sc, sd, lw, first_layer, alpha, n_seq, n_step):
    rows, d = x_tm.shape
    d_mix = lw['w_out'].shape[0]
    dq = d_mix // 4
    assert n_seq % SUBLANES == 0 and n_step <= CHUNK and PAST_LEN % CHUNK == 0
    ins = [x_tm, sa, sc, sd, lw['ln_in_g'], lw['ln_in_b'], lw['w_in'], lw['conv_a_w'], lw['conv_a_b'],
           lw['ln_a_g'], lw['ln_a_b'], lw['ln_v_g'], lw['ln_v_b'], lw['ws_vec'], lw['bs_vec'],
           lw['conv_c_w'], lw['w_pool_bd'], lw['pool_scale'], lw['w_out'], lw['ln_mix_g'], lw['ln_mix_b']]
    out_shape = (
        jax.ShapeDtypeStruct((rows, d), F32),
        jax.ShapeDtypeStruct(((CONV_A_W - 1) * n_seq, dq), F32),
        jax.ShapeDtypeStruct(((CONV_C_W - 1) * n_seq, dq), F32),
        jax.ShapeDtypeStruct((POOL_PAST * n_seq, dq), F32),
        jax.ShapeDtypeStruct((rows, dq), F32),
    )
    return pl.pallas_call(
        functools.partial(_mixer_sample_kernel, first_layer, alpha, n_seq, n_step, d_mix),
        grid=(1,),
        in_specs=[_full_spec(a) for a in ins],
        out_specs=tuple(pl.BlockSpec(s.shape, lambda i: (0, 0)) for s in out_shape),
        out_shape=out_shape,
        scratch_shapes=[pltpu.VMEM((rows, d_mix), BF16)],
        compiler_params=_cparams(("arbitrary",)),
        name="mixer_sample",
    )(*ins)


def _stream_cast(streams):
    plans = []
    for pairs, stage, sems in streams:
        rows = stage.shape[1]
        plans.append(([(src, dst, r0) for src, dst in pairs for r0 in range(0, src.shape[0], rows)],
                      stage, sems))

    def copy(plan, c):
        chunks, stage, sems = plan
        src, _, r0 = chunks[c]
        slot = c % stage.shape[0]
        return pltpu.make_async_copy(src.at[pl.ds(r0, stage.shape[1]), :], stage.at[slot], sems.at[slot])

    for plan in plans:
        for c in range(min(plan[1].shape[0], len(plan[0]))):
            copy(plan, c).start()
    for c in range(max(len(plan[0]) for plan in plans)):
        for plan in plans:
            chunks, stage, _ = plan
            if c < len(chunks):
                _, dst, r0 = chunks[c]
                copy(plan, c).wait()
                dst[pl.ds(r0, stage.shape[1]), :] = stage[c % stage.shape[0]].astype(BF16)
                if c + stage.shape[0] < len(chunks):
                    copy(plan, c + stage.shape[0]).start()


def _ffn_dense_kernel(alpha, tf, n_main, x_ref, xe_ref, wg_hbm, wu_hbm, wd_hbm, g_ref, b_ref, o_ref, oe_ref,
                      wg_ref, wu_ref, wd_ref, stage_in, stage_out, sems):
    i = pl.program_id(0)

    @pl.when(i == 0)
    def _():
        _stream_cast([([(wg_hbm, wg_ref), (wu_hbm, wu_ref)], stage_in, sems.at[0]),
                      ([(wd_hbm, wd_ref)], stage_out, sems.at[1])])

    def run(xr, orf):
        x = xr[...]
        xb = x.astype(BF16)
        ffn = None
        for c0 in range(0, wg_ref.shape[1], tf):
            mid = _silu(_dot(xb, wg_ref[:, c0:c0 + tf])) * _dot(xb, wu_ref[:, c0:c0 + tf])
            part = _dot(mid.astype(BF16), wd_ref[c0:c0 + tf, :])
            ffn = part if ffn is None else ffn + part
        orf[...] = _ln(alpha * x + ffn, g_ref[...], b_ref[...])

    @pl.when(i < n_main)
    def _():
        run(x_ref, o_ref)

    @pl.when(i >= n_main)
    def _():
        run(xe_ref, oe_ref)


def _ffn_dense(x, x_extra, wg, wu, wd, g, b, alpha, tm, tf):
    t, d = x.shape
    te = x_extra.shape[0]
    dff = wg.shape[1]
    n_main, n_extra = t // tm, te // tm
    rows_in, rows_out = 128, 512
    assert t % tm == 0 and te % tm == 0 and dff % tf == 0 and d % rows_in == 0 and dff % rows_out == 0
    any_spec = pl.BlockSpec(memory_space=pl.ANY)
    main_blk = lambda i: (jnp.minimum(i, n_main - 1), 0)
    extra_blk = lambda i: (jnp.maximum(i - n_main, 0), 0)
    return pl.pallas_call(
        functools.partial(_ffn_dense_kernel, alpha, tf, n_main),
        grid=(n_main + n_extra,),
        in_specs=[pl.BlockSpec((tm, d), main_blk), pl.BlockSpec((tm, d), extra_blk),
                  any_spec, any_spec, any_spec,
                  pl.BlockSpec((1, d), lambda i: (0, 0)),
                  pl.BlockSpec((1, d), lambda i: (0, 0))],
        out_specs=(pl.BlockSpec((tm, d), main_blk), pl.BlockSpec((tm, d), extra_blk)),
        out_shape=(jax.ShapeDtypeStruct((t, d), F32), jax.ShapeDtypeStruct((te, d), F32)),
        scratch_shapes=[pltpu.VMEM(wg.shape, BF16), pltpu.VMEM(wu.shape, BF16), pltpu.VMEM(wd.shape, BF16),
                        pltpu.VMEM((3, rows_in, dff), F32), pltpu.VMEM((3, rows_out, d), F32),
                        pltpu.SemaphoreType.DMA((2, 3))],
        compiler_params=_cparams(("arbitrary",)),
        name="ffn_dense",
    )(x, x_extra, wg, wu, wd, g, b)


def _route_init(tm, cin_ref, carry_ref, before_ref):
    carry_ref[...] = cin_ref[...].astype(F32)
    row = lax.broadcasted_iota(I32, (tm, tm), 0)
    col = lax.broadcasted_iota(I32, (tm, tm), 1)
    before_ref[...] = jnp.where(row < col, 1.0, 0.0).astype(BF16)


def _route_block(tm, h, wcat_ref, mi_ref, mf_ref, cnt_ref, carry_ref, before_ref):
    h_hi = h.astype(BF16)
    h_lo = (h - h_hi.astype(F32)).astype(BF16)
    p_hi = _dot(h_hi, wcat_ref[...])
    p_lo = _dot(h_lo, wcat_ref[...])
    logits = (p_hi[:, 0:LANES] + (p_lo[:, 0:LANES] + p_hi[:, LANES:2 * LANES])) + p_lo[:, LANES:2 * LANES]
    lg = logits.T[0:N_EXPERTS, :]
    ex = lax.broadcasted_iota(I32, (N_EXPERTS, tm), 0).astype(F32)
    m1 = jnp.max(lg, axis=0, keepdims=True)
    i1 = jnp.min(jnp.where(lg == m1, ex, float(N_EXPERTS)), axis=0, keepdims=True)
    sel1 = ex == i1
    rest = jnp.where(sel1, -jnp.inf, lg)
    m2 = jnp.max(rest, axis=0, keepdims=True)
    i2 = jnp.min(jnp.where(rest == m2, ex, float(N_EXPERTS)), axis=0, keepdims=True)
    sel2 = ex == i2
    e2 = jnp.exp(m2 - m1)
    den = 1.0 + e2
    g1 = 1.0 / den
    g2 = e2 / den

    sel = jnp.where(sel1 | sel2, 1.0, 0.0)
    sel_pad = jnp.concatenate([sel, jnp.zeros_like(sel)], axis=0).astype(BF16)
    base = _dot(sel_pad, before_ref[...])[0:N_EXPERTS, :] + carry_ref[:, 0:1]
    r1 = jnp.sum(jnp.where(sel1, base, 0.0), axis=0, keepdims=True)
    r2 = jnp.sum(jnp.where(sel2, base, 0.0), axis=0, keepdims=True)
    carry_ref[...] = carry_ref[...] + jnp.sum(sel, axis=1, keepdims=True)

    meta = jnp.where(ex == 0, i1, jnp.where(ex == 1, i2, jnp.where(ex == 2, r1, r2)))
    mi_ref[...] = meta.astype(I32)
    gt = jnp.where(ex == 0, g1, jnp.where(ex == 1, g2, 0.0))
    gt = jnp.concatenate([gt, jnp.zeros((LANES - N_EXPERTS, tm), F32)], axis=0)
    mf_ref[...] = gt.T[:, 0:SUBLANES]
    cnt_ref[...] = carry_ref[...].astype(I32)


def _router_kernel(tm, h_ref, wcat_ref, cin_ref, mi_ref, mf_ref, cnt_ref, carry_ref, before_ref):
    @pl.when(pl.program_id(0) == 0)
    def _():
        _route_init(tm, cin_ref, carry_ref, before_ref)

    _route_block(tm, h_ref[...], wcat_ref, mi_ref, mf_ref, cnt_ref, carry_ref, before_ref)


def _router(h, wr_cat, counts_in):
    t, d = h.shape
    tm = min(TM_ROUTE, t)
    assert t % tm == 0 and N_EXPERTS == SUBLANES
    return pl.pallas_call(
        functools.partial(_router_kernel, tm),
        grid=(t // tm,),
        in_specs=[pl.BlockSpec((tm, d), lambda i: (i, 0)), _full_spec(wr_cat), _full_spec(counts_in)],
        out_specs=(pl.BlockSpec((SUBLANES, tm), lambda i: (0, i)),
                   pl.BlockSpec((tm, SUBLANES), lambda i: (i, 0)),
                   pl.BlockSpec((SUBLANES, LANES), lambda i: (0, 0))),
        out_shape=(jax.ShapeDtypeStruct((SUBLANES, t), I32),
                   jax.ShapeDtypeStruct((t, SUBLANES), F32),
                   jax.ShapeDtypeStruct((SUBLANES, LANES), I32)),
        scratch_shapes=[pltpu.VMEM((SUBLANES, LANES), F32), pltpu.VMEM((tm, tm), BF16)],
        compiler_params=_cparams(("arbitrary",)),
        name="router",
    )(h, wr_cat, counts_in)


def _row_copy(src, src_row, dst, dst_row, sem):
    return pltpu.make_async_copy(src.at[pl.ds(src_row, 1), :], dst.at[pl.ds(dst_row, 1), :], sem)


def _rows_wait(src, dst, n_rows, sem):
    pltpu.make_async_copy(src.at[pl.ds(0, n_rows), :], dst.at[pl.ds(0, n_rows), :], sem).wait()


def _dispatch_kernel(tm, tm_expert, n_blocks, blk_ranges, zi_ref, dest_ref, *rest):
    n_src = len(blk_ranges)
    h_refs = rest[:n_src]
    xs_ref, xbuf, zblk, lsems, rsems, zsem = rest[n_src:]
    i = pl.program_id(0)
    n = pl.num_programs(0)

    def block_load(g, start):
        slot = lax.rem(g, N_STAGE)
        for h_ref, (b0, b1) in zip(h_refs, blk_ranges):
            @pl.when((g >= b0) & (g < b1))
            def _(h_ref=h_ref, b0=b0):
                cp = pltpu.make_async_copy(h_ref.at[pl.ds((g - b0) * tm, tm), :], xbuf.at[slot],
                                           lsems.at[slot])
                if start:
                    cp.start()
                else:
                    cp.wait()

    def rows_wait(g):
        slot = lax.rem(g, N_STAGE)
        for _k in range(TOP_K):
            _rows_wait(xbuf.at[slot], xs_ref, tm, rsems.at[slot])

    @pl.when(i == 0)
    def _():
        block_load(i, True)

    @pl.when(i >= N_STAGE - 1)
    def _():
        rows_wait(i - (N_STAGE - 1))

    @pl.when(i + 1 < n)
    def _():
        block_load(i + 1, True)

    block_load(i, False)
    cur = lax.rem(i, N_STAGE)

    def issue(j, carry):
        r0 = pl.multiple_of(j * SUBLANES, SUBLANES)
        for u in range(SUBLANES):
            for k in range(TOP_K):
                dst = dest_ref[0, 0, j * SUBLANES + (k * tm + u)]
                _row_copy(xbuf.at[cur], r0 + u, xs_ref, dst, rsems.at[cur]).start(priority=k)
        return carry

    lax.fori_loop(0, tm // SUBLANES, issue, 0)

    @pl.when(i == n - 1)
    def _():
        for back in range(N_STAGE - 2, -1, -1):
            @pl.when(i >= back)
            def _(back=back):
                rows_wait(i - back)

    @pl.when(i == 0)
    def _():
        zblk[...] = jnp.zeros_like(zblk)
        for e in range(N_EXPERTS):
            lo = zi_ref[e] + zi_ref[N_EXPERTS + e]
            hi = zi_ref[e] + zi_ref[2 * N_EXPERTS + e]

            def zissue(r, carry):
                _row_copy(zblk, 0, xs_ref, r, zsem).start()
                return carry

            def zdrain(r, carry):
                _row_copy(zblk, 0, xs_ref, r, zsem).wait()
                return carry

            lax.fori_loop(lo, hi, zissue, 0)
            lax.fori_loop(lo, hi, zdrain, 0)

        def bcopy(b):
            return pltpu.make_async_copy(zblk, xs_ref.at[pl.ds(b * tm_expert, tm_expert), :], zsem)

        def bissue(b, carry):
            bcopy(b).start()
            return carry

        def bdrain(b, carry):
            bcopy(b).wait()
            return carry

        lax.fori_loop(zi_ref[3 * N_EXPERTS], n_blocks, bissue, 0)
        lax.fori_loop(zi_ref[3 * N_EXPERTS], n_blocks, bdrain, 0)


def _dispatch(hs_list, dest_blk, zinfo, n_blocks, tm_expert):
    d = hs_list[0].shape[1]
    nb = dest_blk.shape[0]
    tm = dest_blk.shape[2] // TOP_K
    blk_ranges, b0 = [], 0
    for h in hs_list:
        assert h.shape[0] % tm == 0 and h.shape[0] >= tm
        blk_ranges.append((b0, b0 + h.shape[0] // tm))
        b0 = blk_ranges[-1][1]
    assert b0 == nb
    any_spec = pl.BlockSpec(memory_space=pl.ANY)
    return pl.pallas_call(
        functools.partial(_dispatch_kernel, tm, tm_expert, n_blocks, tuple(blk_ranges)),
        grid_spec=pltpu.PrefetchScalarGridSpec(
            num_scalar_prefetch=1,
            grid=(nb,),
            in_specs=[pl.BlockSpec((1, 1, TOP_K * tm), lambda i, zi: (i, 0, 0), memory_space=pltpu.SMEM)]
            + [any_spec] * len(hs_list),
            out_specs=any_spec,
            scratch_shapes=[pltpu.VMEM((N_STAGE, tm, d), F32), pltpu.VMEM((tm_expert, d), F32),
                            pltpu.SemaphoreType.DMA((N_STAGE,)), pltpu.SemaphoreType.DMA((N_STAGE,)),
                            pltpu.SemaphoreType.DMA(())],
        ),
        out_shape=jax.ShapeDtypeStruct((n_blocks * tm_expert, d), F32),
        compiler_params=_cparams(("arbitrary",)),
        name="moe_dispatch",
    )(zinfo, dest_blk, *hs_list)


def _ffn_moe_kernel(tf, be_ref, nu_ref, x_ref, wg_hbm, wu_hbm, wd_hbm, y_ref,
                    wg_ref, wu_ref, wd_ref, stage_in, stage_out, sems):
    b = pl.program_id(0)
    nb = pl.num_programs(0)
    d, dff = wg_ref.shape
    n_slots = stage_in.shape[0]
    rows_in = stage_in.shape[1]
    rows_out = stage_out.shape[1]
    n_in, n_out = 2 * (d // rows_in), dff // rows_out
    e = be_ref[b]
    fresh = (b == 0) | (e != be_ref[jnp.maximum(b - 1, 0)])
    e_next = be_ref[jnp.minimum(b + 1, nb - 1)]

    def chunk_in(ex, n):
        src, dst = (wg_hbm, wg_ref) if n < n_in // 2 else (wu_hbm, wu_ref)
        rows = pl.ds((n % (n_in // 2)) * rows_in, rows_in)
        stage = stage_in.at[n % n_slots]
        return pltpu.make_async_copy(src.at[ex, rows, :], stage, sems.at[0, n % n_slots]), stage, dst.at[rows, :]

    def chunk_out(ex, n):
        rows = pl.ds(n * rows_out, rows_out)
        stage = stage_out.at[n % n_slots]
        return pltpu.make_async_copy(wd_hbm.at[ex, rows, :], stage, sems.at[1, n % n_slots]), stage, wd_ref.at[rows, :]

    def request_first(ex):
        for n in range(n_slots):
            chunk_in(ex, n)[0].start()
            chunk_out(ex, n)[0].start()

    @pl.when((b < nu_ref[0]) & fresh)
    def _():
        @pl.when(b == 0)
        def _():
            request_first(e)

        for n in range(n_in):
            for chunk, count in ((chunk_in, n_in), (chunk_out, n_out)):
                if n < count:
                    copy, staged, dst = chunk(e, n)
                    copy.wait()
                    dst[...] = staged[...].astype(BF16)
                    if n + n_slots < count:
                        chunk(e, n + n_slots)[0].start()

    @pl.when((b + 1 < nu_ref[0]) & (e_next != e))
    def _():
        request_first(e_next)

    tm = x_ref.shape[0]
    rows_used = nu_ref[1 + b]

    def compute(rows):
        xb = x_ref[0:rows, :].astype(BF16)
        ffn = None
        for c0 in range(0, dff, tf):
            mid = _silu(_dot(xb, wg_ref[:, c0:c0 + tf])) * _dot(xb, wu_ref[:, c0:c0 + tf])
            part = _dot(mid.astype(BF16), wd_ref[c0:c0 + tf, :])
            ffn = part if ffn is None else ffn + part
        y_ref[0:rows, :] = ffn
        if rows < tm:
            y_ref[rows:tm, :] = jnp.zeros((tm - rows, y_ref.shape[1]), F32)

    @pl.when((b < nu_ref[0]) & (rows_used > tm // 2))
    def _():
        compute(tm)

    @pl.when((b < nu_ref[0]) & (rows_used <= tm // 2))
    def _():
        compute(tm // 2)

    @pl.when(b >= nu_ref[0])
    def _():
        y_ref[...] = jnp.zeros_like(y_ref)


def _ffn_moe(xs, wg, wu, wd, block_e, n_used, tm, tf):
    cap, d = xs.shape
    dff = wg.shape[2]
    rows_in, rows_out = 128, 512
    assert cap % tm == 0 and dff % tf == 0 and d % rows_in == 0 and dff % rows_out == 0
    assert min(2 * (d // rows_in), dff // rows_out) >= N_WSTAGE
    any_spec = pl.BlockSpec(memory_space=pl.ANY)
    return pl.pallas_call(
        functools.partial(_ffn_moe_kernel, tf),
        grid_spec=pltpu.PrefetchScalarGridSpec(
            num_scalar_prefetch=2,
            grid=(cap // tm,),
            in_specs=[pl.BlockSpec((tm, d), lambda b, be, nu: (jnp.minimum(b, nu[0] - 1), 0)),
                      any_spec, any_spec, any_spec],
            out_specs=pl.BlockSpec((tm, d), lambda b, be, nu: (b, 0)),
            scratch_shapes=[pltpu.VMEM((d, dff), BF16), pltpu.VMEM((d, dff), BF16), pltpu.VMEM((dff, d), BF16),
                            pltpu.VMEM((N_WSTAGE, rows_in, dff), F32), pltpu.VMEM((N_WSTAGE, rows_out, d), F32),
                            pltpu.SemaphoreType.DMA((2, N_WSTAGE))],
        ),
        out_shape=jax.ShapeDtypeStruct((cap, d), F32),
        compiler_params=_cparams(("arbitrary",)),
        name="ffn_moe",
    )(block_e, n_used, xs, wg, wu, wd)


def _combine_kernel(alpha, tm, dcur_ref, dnext_ref, h_ref, gate_ref, g_ref, b_ref, ys_ref, o_ref, ybuf, sems):
    i = pl.program_id(0)
    n = pl.num_programs(0)

    def issue(dref, slot):
        def body(j, carry):
            r0 = pl.multiple_of(j * SUBLANES, SUBLANES)
            for u in range(SUBLANES):
                for k in range(TOP_K):
                    src = dref[0, 0, j * SUBLANES + (k * tm + u)]
                    _row_copy(ys_ref, src, ybuf.at[slot, k], r0 + u, sems.at[slot]).start(priority=k)
            return carry

        lax.fori_loop(0, tm // SUBLANES, body, 0)

    @pl.when(i == 0)
    def _():
        issue(dcur_ref, 0)

    @pl.when(i + 1 < n)
    def _():
        issue(dnext_ref, (i + 1) % 2)

    slot = i % 2
    for k in range(TOP_K):
        _rows_wait(ys_ref, ybuf.at[slot, k], tm, sems.at[slot])

    gates = gate_ref[...]
    ffn = gates[:, 0:1] * ybuf[slot, 0] + gates[:, 1:2] * ybuf[slot, 1]
    o_ref[...] = _ln(alpha * h_ref[...] + ffn, g_ref[...], b_ref[...])


def _combine(h, dest_blk, gates, ys, g, b, alpha):
    t, d = h.shape
    nb = dest_blk.shape[0]
    tm = dest_blk.shape[2] // TOP_K
    dest_spec = lambda imap: pl.BlockSpec((1, 1, TOP_K * tm), imap, memory_space=pltpu.SMEM)
    return pl.pallas_call(
        functools.partial(_combine_kernel, alpha, tm),
        grid=(nb,),
        in_specs=[dest_spec(lambda i: (i, 0, 0)),
                  dest_spec(lambda i: (jnp.minimum(i + 1, nb - 1), 0, 0)),
                  pl.BlockSpec((tm, d), lambda i: (i, 0)),
                  pl.BlockSpec((tm, SUBLANES), lambda i: (i, 0)),
                  pl.BlockSpec((1, d), lambda i: (0, 0)),
                  pl.BlockSpec((1, d), lambda i: (0, 0)),
                  pl.BlockSpec(memory_space=pl.ANY)],
        out_specs=pl.BlockSpec((tm, d), lambda i: (i, 0)),
        out_shape=jax.ShapeDtypeStruct((t, d), F32),
        scratch_shapes=[pltpu.VMEM((2, TOP_K, tm, d), F32), pltpu.SemaphoreType.DMA((2,))],
        compiler_params=_cparams(("arbitrary",)),
        name="moe_combine",
    )(dest_blk, dest_blk, h, gates, g, b, ys)


def _ffn_routed(hs_list, first_routed, mw, g, b, alpha, tm_expert):
    routed = [first_routed[:2]]
    counts_in = first_routed[2]
    for h in hs_list[1:]:
        meta_i, gates, counts_in = _router(h, mw['w_router_cat'], counts_in)
        routed.append((meta_i, gates))
    counts = counts_in[:N_EXPERTS, 0]
    padded = (counts + tm_expert - 1) // tm_expert * tm_expert
    pad_end = jnp.cumsum(padded)
    pad_start = (pad_end - padded).astype(I32)
    n_assign = sum(h.shape[0] for h in hs_list) * TOP_K
    n_blocks = -(-n_assign // tm_expert) + N_EXPERTS
    cap = n_blocks * tm_expert
    n_used = (pad_end[-1] // tm_expert).astype(I32)
    blk_start = jnp.minimum(jnp.arange(n_blocks, dtype=I32), n_used - 1) * tm_expert
    block_e = jnp.minimum(jnp.sum(blk_start[:, None] >= pad_end[None, :], axis=1), N_EXPERTS - 1).astype(I32)
    zinfo = jnp.concatenate([pad_start, counts, padded, n_used.reshape(1)]).astype(I32)

    dests = []
    for h, (meta_i, _) in zip(hs_list, routed):
        dest = meta_i[TOP_K:2 * TOP_K, :]
        for e in range(N_EXPERTS):
            dest = dest + jnp.where(meta_i[0:TOP_K, :] == e, pad_start[e], 0)
        nb = h.shape[0] // TM_MOVE
        dests.append(dest.reshape(TOP_K, nb, TM_MOVE).transpose(1, 0, 2).reshape(nb, 1, TOP_K * TM_MOVE))
    xs = _dispatch(hs_list, jnp.concatenate(dests, axis=0), zinfo, n_blocks, tm_expert)
    filled_end = (pad_start + counts)[block_e]
    occupancy = jnp.clip(filled_end - jnp.arange(n_blocks, dtype=I32) * tm_expert, 0, tm_expert)
    used_info = jnp.concatenate([n_used.reshape(1), occupancy]).astype(I32)
    ys = _ffn_moe(xs, mw['w_e_gate'], mw['w_e_up'], mw['w_e_down'], block_e, used_info, tm_expert,
                  TF_EXPERT)
    return [_combine(h, dest_blk, gates, ys, g, b, alpha)
            for h, dest_blk, (_, gates) in zip(hs_list, dests, routed)]


def _prep_layer(i, p, n_step):
    dq = p['conv_a_w'].shape[2]
    hd = dq // N_HEADS_B
    tril = jnp.tril(jnp.ones((CHUNK, CHUNK), dtype=bool))
    w_s = p['w_s'][i]
    b_s = p['b_s'][i]
    wp = p['w_pool'][i]
    ng, gc = wp.shape[0], wp.shape[1]
    w_pool_bd = (jnp.eye(ng, dtype=F32)[:, None, :, None] * wp[:, :, None, :]).reshape(ng * gc, ng * gc)
    row = lambda a: a.reshape(1, -1)
    return dict(
        ln_in_g=row(p['ln_in_g']), ln_in_b=row(p['ln_in_b']),
        w_in=p['w_in'][i].astype(BF16),
        conv_a_w=p['conv_a_w'][i], conv_a_b=row(p['conv_a_b'][i]),
        ln_a_g=row(p['ln_a_g'][i]), ln_a_b=row(p['ln_a_b'][i]),
        ln_v_g=row(p['ln_v_g'][i]), ln_v_b=row(p['ln_v_b'][i]),
        ws_tril=jnp.where(tril[None], w_s, 0).astype(BF16),
        bs_mat=jnp.repeat(b_s.T, hd, axis=1),
        ws_vec=jnp.repeat(jnp.where(tril[None], w_s, 0)[:, :n_step, :n_step].transpose(1, 2, 0)
                          .reshape(n_step * n_step, N_HEADS_B), hd, axis=1),
        bs_vec=jnp.repeat(b_s[:, :n_step].T, hd, axis=1),
        conv_c_w=p['conv_c_w'][i],
        w_pool_bd=w_pool_bd.astype(BF16), pool_scale=row(p['pool_scale'][i]),
        w_out=p['w_out'][i].astype(BF16),
        ln_mix_g=row(p['ln_mix_g'][i]), ln_mix_b=row(p['ln_mix_b'][i]),
    )


def kernel(x_prompt, x_sample, state_conv_a, state_conv_c, state_pool_d, ln_in_g, ln_in_b, w_in, conv_a_w,
           conv_a_b, ln_a_g, ln_a_b, ln_v_g, ln_v_b, w_s, b_s, conv_c_w, w_pool, pool_scale, w_out,
           ln_mix_g, ln_mix_b, w_ff_gate, w_ff_up, w_ff_down, w_router, w_e_gate, w_e_up, w_e_down,
           ln_ffn_g, ln_ffn_b):
    p = dict(ln_in_g=ln_in_g, ln_in_b=ln_in_b, w_in=w_in, conv_a_w=conv_a_w, conv_a_b=conv_a_b,
             ln_a_g=ln_a_g, ln_a_b=ln_a_b, ln_v_g=ln_v_g, ln_v_b=ln_v_b, w_s=w_s, b_s=b_s,
             conv_c_w=conv_c_w, w_pool=w_pool, pool_scale=pool_scale, w_out=w_out,
             ln_mix_g=ln_mix_g, ln_mix_b=ln_mix_b)
    depth = w_in.shape[0]
    bsz, seq, d = x_prompt.shape
    n_seq, n_step, _ = x_sample.shape
    dq = conv_a_w.shape[2]
    alpha = float((2.0 * depth) ** 0.25)

    hp = x_prompt
    hs = x_sample.transpose(1, 0, 2).reshape(n_step * n_seq, d)
    outs = {k: [] for k in ('pa', 'pc', 'pd', 'pv', 'sa', 'sc', 'sd', 'sv')}
    for i in range(depth):
        lw = _prep_layer(i, p, n_step)
        j = i // 2
        routed = i % 2 == 1
        if routed:
            wr = jnp.pad(w_router[j], ((0, 0), (0, LANES - N_EXPERTS)))
            wr_hi = wr.astype(BF16)
            wr_lo = (wr - wr_hi.astype(F32)).astype(BF16)
            wr_cat = jnp.concatenate([wr_hi, wr_lo], axis=1)
        hp, pa, pc, pd, pv, *route_p = _mixer_prompt(hp, lw, i == 0, alpha, wr_cat if routed else None)
        tm_state = lambda s: s.transpose(1, 0, 2).reshape(-1, dq)
        hs, sa, sc, sd, sv = _mixer_sample(hs, tm_state(state_conv_a[i]), tm_state(state_conv_c[i]),
                                           tm_state(state_pool_d[i]), lw, i == 0, alpha, n_seq, n_step)
        lg, lb = ln_ffn_g[i].reshape(1, -1), ln_ffn_b[i].reshape(1, -1)
        if not routed:
            hp, hs = _ffn_dense(hp.reshape(bsz * seq, d), hs, w_ff_gate[j], w_ff_up[j], w_ff_down[j], lg, lb,
                                alpha, TM_DENSE, TF_DENSE)
            hp = hp.reshape(bsz, seq, d)
        else:
            mw = dict(w_router_cat=wr_cat, w_e_gate=w_e_gate[j], w_e_up=w_e_up[j], w_e_down=w_e_down[j])
            hp, hs = _ffn_routed([hp.reshape(bsz * seq, d), hs], route_p, mw, lg, lb, alpha, TM_EXPERT)
            hp = hp.reshape(bsz, seq, d)
        back = lambda a: a.reshape(-1, n_seq, dq).transpose(1, 0, 2)
        for k, val in zip(('pa', 'pc', 'pd', 'pv', 'sa', 'sc', 'sd', 'sv'),
                          (pa, pc, pd, pv, back(sa), back(sc), back(sd), back(sv))):
            outs[k].append(val)
    y_sample = hs.reshape(n_step, n_seq, d).transpose(1, 0, 2)
    st = lambda k: jnp.stack(outs[k])
    return (hp, y_sample, st('pa'), st('pc'), st('pd'), st('pv'), st('sa'), st('sc'), st('sd'), st('sv'))
```

```python
import functools

import numpy as np
import jax
import jax.numpy as jnp
from jax import lax
from jax.experimental import pallas as pl
from jax.experimental.pallas import tpu as pltpu

F32 = jnp.float32
BF16 = jnp.bfloat16
I32 = jnp.int32

PAST_LEN = 16384
CHUNK = 128
N_HEADS_B = 4
POOL_WINDOWS = (2, 4, 8, 16)
POOL_PAST = 15
CONV_A_W = 31
CONV_C_W = 3
N_EXPERTS = 8
TOP_K = 2
LN_EPS = 1e-5
SQRT_HALF = float(np.sqrt(0.5).astype(np.float32))

LANES = 128
SUBLANES = 8
VMEM_LIMIT_BYTES = 56 * 1024 * 1024

HIST_A = 32
HIST_C = 8
HIST_D = 24

TL_PROMPT = 512
TM_DENSE = 512
TF_DENSE = 1792
TF_EXPERT = 1792
TM_ROUTE = 512
TM_MOVE = 256
TM_EXPERT = 512
N_STAGE = 3
N_WSTAGE = 5


def _ln(x, g, b):
    mu = jnp.mean(x, axis=-1, keepdims=True)
    xc = x - mu
    var = jnp.mean(xc * xc, axis=-1, keepdims=True)
    return xc * lax.rsqrt(var + LN_EPS) * g + b


def _gelu(x):
    return 0.5 * x * (1.0 + lax.erf(x * SQRT_HALF))


def _silu(x):
    return x * jax.nn.sigmoid(x)


def _dot(a, b):
    return jnp.dot(a, b, preferred_element_type=F32)


def _cparams(sem):
    return pltpu.CompilerParams(dimension_semantics=sem, vmem_limit_bytes=VMEM_LIMIT_BYTES)


def _mixer_prompt_kernel(first_layer, route, alpha, tl, d_mix,
                         x_ref, lng_ref, lnb_ref, win_ref, caw_ref, cab_ref, lag_ref, lab_ref,
                         lvg_ref, lvb_ref, ws_ref, bsm_ref, ccw_ref, wpool_ref, pscale_ref,
                         wout_ref, lmg_ref, lmb_ref, *refs):
    if route:
        wcat_ref, cin_ref = refs[:2]
        refs = refs[2:]
    h_out, sa_out, sc_out, sd_out, v_out = refs[:5]
    refs = refs[5:]
    if route:
        mi_ref, mf_ref, cnt_ref = refs[:3]
        refs = refs[3:]
    ext_a, ext_c, ext_d, cat_ref, z_ref, hb_ref, mix_ref = refs[:7]
    dq = d_mix // 4
    j = pl.program_id(1)
    nj = pl.num_programs(1)
    if route:
        carry_ref, before_ref = refs[7:]

        @pl.when((pl.program_id(0) == 0) & (j == 0))
        def _():
            _route_init(tl, cin_ref, carry_ref, before_ref)

    @pl.when(j == 0)
    def _():
        ext_a[0, 0:HIST_A, :] = jnp.zeros((HIST_A, dq), F32)
        ext_c[0:HIST_C, :] = jnp.zeros((HIST_C, dq), F32)
        ext_d[0:HIST_D, :] = jnp.zeros((HIST_D, dq), F32)

    d = x_ref.shape[2]
    rb = CHUNK

    def resid(r0, n):
        return h_out[0, r0:r0 + n, :] if first_layer else x_ref[0, r0:r0 + n, :]

    for r0 in range(0, tl, rb):
        x = x_ref[0, r0:r0 + rb, :]
        if first_layer:
            x = _ln(x, lng_ref[...], lnb_ref[...])
            h_out[0, r0:r0 + rb, :] = x
        hb_ref[r0:r0 + rb, :] = x.astype(BF16)

    z_ref[...] = _dot(hb_ref[...], win_ref[...])

    def zcol(r0, i, n=rb):
        return z_ref[r0:r0 + n, i * dq:(i + 1) * dq]

    chunk_rows = range(0, tl, CHUNK)

    for r0 in chunk_rows:
        ext_a[0, HIST_A + r0:HIST_A + r0 + rb, :] = zcol(r0, 0) * jax.nn.sigmoid(zcol(r0, 1))
    n_ext = HIST_A + tl
    for r0 in range(0, n_ext - SUBLANES, rb):
        n = min(rb, n_ext - SUBLANES - r0)
        blk = ext_a[0, r0:r0 + n + SUBLANES, :]
        for s in range(1, SUBLANES):
            ext_a[s, r0:r0 + n, :] = pltpu.roll(blk, n + SUBLANES - s, axis=0)[0:n, :]
    off_a = HIST_A - (CONV_A_W - 1)
    for q0 in range(0, tl, 64):
        acc = jnp.zeros((64, dq), F32)
        for k in range(CONV_A_W):
            s = (off_a + k) % SUBLANES
            row = off_a + k - s + q0
            acc = acc + ext_a[s, row:row + 64, :] * caw_ref[k:k + 1, :]
        y = _silu(_ln(acc + cab_ref[...], lag_ref[...], lab_ref[...]))
        cat_ref[q0:q0 + 64, 0:dq] = y.astype(BF16)

    hd = dq // N_HEADS_B
    lane_head = lax.broadcasted_iota(I32, (CHUNK, dq), 1) // hd
    for r0 in chunk_rows:
        v = _ln(_gelu(zcol(r0, 3)), lvg_ref[...], lvb_ref[...])
        sm = bsm_ref[...]
        for hh in range(N_HEADS_B):
            vm = jnp.where(lane_head == hh, v, 0.0).astype(BF16)
            sm = sm + _dot(ws_ref[hh], vm)
        cat_ref[r0:r0 + CHUNK, dq:2 * dq] = (_gelu(zcol(r0, 2)) * sm).astype(BF16)
        if r0 == tl - CHUNK:
            @pl.when(j == nj - 1)
            def _(v=v):
                v_out[0] = v

    off_c = HIST_C - (CONV_C_W - 1)
    for r0 in chunk_rows:
        ext_c[HIST_C + r0:HIST_C + r0 + rb, :] = zcol(r0, 6) * zcol(r0, 4)
    for r0 in chunk_rows:
        conv_c = jnp.zeros((CHUNK, dq), F32)
        for k in range(CONV_C_W):
            conv_c = conv_c + ext_c[off_c + r0 + k:off_c + r0 + k + CHUNK, :] * ccw_ref[k:k + 1, :]
        cat_ref[r0:r0 + CHUNK, 2 * dq:3 * dq] = (zcol(r0, 5) * conv_c).astype(BF16)

    gc = dq // len(POOL_WINDOWS)
    lane_grp = lax.broadcasted_iota(I32, (CHUNK, dq), 1) // gc
    lead = 2 * SUBLANES
    for r0 in chunk_rows:
        ext_d[HIST_D + r0:HIST_D + r0 + rb, :] = zcol(r0, 7)
    for r0 in chunk_rows:
        e = ext_d[HIST_D + r0 - lead:HIST_D + r0 + CHUNK, :]
        s2 = e + pltpu.roll(e, 1, axis=0)
        s4 = s2 + pltpu.roll(s2, 2, axis=0)
        s8 = s4 + pltpu.roll(s4, 4, axis=0)
        s16 = s8 + pltpu.roll(s8, 8, axis=0)
        sums = (s2, s4, s8, s16)
        pos = (j * tl + r0 + lax.broadcasted_iota(I32, (CHUNK, 1), 0)).astype(F32)
        mean = jnp.zeros((CHUNK, dq), F32)
        for g, w in enumerate(POOL_WINDOWS):
            inv = 1.0 / jnp.minimum(pos + 1.0, float(w))
            mean = jnp.where(lane_grp == g, sums[g][lead:lead + CHUNK, :] * inv, mean)
        dd = (mean - e[lead:lead + CHUNK, :]).astype(BF16)
        cat_ref[r0:r0 + CHUNK, 3 * dq:4 * dq] = (_dot(dd, wpool_ref[...]) * pscale_ref[...]).astype(BF16)

    mix_ref[...] = _dot(cat_ref[...], wout_ref[...])
    for r0 in range(0, tl, 64):
        h_out[0, r0:r0 + 64, :] = _ln(alpha * resid(r0, 64) + mix_ref[r0:r0 + 64, :], lmg_ref[...], lmb_ref[...])
    if route:
        _route_block(tl, h_out[0], wcat_ref, mi_ref, mf_ref, cnt_ref, carry_ref, before_ref)

    @pl.when(j == nj - 1)
    def _():
        sa_out[0] = ext_a[0, HIST_A + tl - (CONV_A_W - 1):HIST_A + tl, :]
        sc_out[0] = ext_c[HIST_C + tl - (CONV_C_W - 1):HIST_C + tl, :]
        sd_out[0] = ext_d[HIST_D + tl - POOL_PAST:HIST_D + tl, :]

    ext_a[0, 0:HIST_A, :] = ext_a[0, tl:tl + HIST_A, :]
    ext_c[0:HIST_C, :] = ext_c[tl:tl + HIST_C, :]
    ext_d[0:HIST_D, :] = ext_d[tl:tl + HIST_D, :]


def _full_spec(arr):
    nd = arr.ndim
    return pl.BlockSpec(arr.shape, lambda *_: (0,) * nd)


def _mixer_prompt(x, lw, first_layer, alpha, route_w=None):
    bsz, seq, d = x.shape
    d_mix = lw['w_out'].shape[0]
    dq = d_mix // 4
    tl = TL_PROMPT
    nj = seq // tl
    route = route_w is not None
    assert seq % tl == 0 and tl % CHUNK == 0 and seq >= CHUNK
    params = [lw['ln_in_g'], lw['ln_in_b'], lw['w_in'], lw['conv_a_w'], lw['conv_a_b'], lw['ln_a_g'],
              lw['ln_a_b'], lw['ln_v_g'], lw['ln_v_b'], lw['ws_tril'], lw['bs_mat'], lw['conv_c_w'],
              lw['w_pool_bd'], lw['pool_scale'], lw['w_out'], lw['ln_mix_g'], lw['ln_mix_b']]
    out_shape = [
        jax.ShapeDtypeStruct((bsz, seq, d), F32),
        jax.ShapeDtypeStruct((bsz, CONV_A_W - 1, dq), F32),
        jax.ShapeDtypeStruct((bsz, CONV_C_W - 1, dq), F32),
        jax.ShapeDtypeStruct((bsz, POOL_PAST, dq), F32),
        jax.ShapeDtypeStruct((bsz, CHUNK, dq), F32),
    ]
    state_spec = lambda rows: pl.BlockSpec((1, rows, dq), lambda b, j: (b, 0, 0))
    out_specs = [pl.BlockSpec((1, tl, d), lambda b, j: (b, j, 0)),
                 state_spec(CONV_A_W - 1), state_spec(CONV_C_W - 1), state_spec(POOL_PAST), state_spec(CHUNK)]
    scratch = [pltpu.VMEM((SUBLANES, HIST_A + tl, dq), F32), pltpu.VMEM((HIST_C + tl, dq), F32),
               pltpu.VMEM((HIST_D + tl, dq), F32), pltpu.VMEM((tl, d_mix), BF16),
               pltpu.VMEM((tl, lw['w_in'].shape[1]), F32), pltpu.VMEM((tl, d), BF16),
               pltpu.VMEM((tl, d), F32)]
    if route:
        assert N_EXPERTS == SUBLANES
        params += [route_w, jnp.zeros((SUBLANES, LANES), I32)]
        out_shape += [jax.ShapeDtypeStruct((SUBLANES, bsz * seq), I32),
                      jax.ShapeDtypeStruct((bsz * seq, SUBLANES), F32),
                      jax.ShapeDtypeStruct((SUBLANES, LANES), I32)]
        out_specs += [pl.BlockSpec((SUBLANES, tl), lambda b, j: (0, b * nj + j)),
                      pl.BlockSpec((tl, SUBLANES), lambda b, j: (b * nj + j, 0)),
                      pl.BlockSpec((SUBLANES, LANES), lambda b, j: (0, 0))]
        scratch += [pltpu.VMEM((SUBLANES, LANES), F32), pltpu.VMEM((tl, tl), BF16)]
    return pl.pallas_call(
        functools.partial(_mixer_prompt_kernel, first_layer, route, alpha, tl, d_mix),
        grid=(bsz, nj),
        in_specs=[pl.BlockSpec((1, tl, d), lambda b, j: (b, j, 0))] + [_full_spec(p) for p in params],
        out_specs=tuple(out_specs),
        out_shape=tuple(out_shape),
        scratch_shapes=scratch,
        compiler_params=_cparams(("arbitrary", "arbitrary")),
        name="mixer_prompt",
    )(x, *params)


def _mixer_sample_kernel(first_layer, alpha, n_seq, n_step, d_mix,
                         x_ref, sa_ref, sc_ref, sd_ref, lng_ref, lnb_ref, win_ref, caw_ref, cab_ref,
                         lag_ref, lab_ref, lvg_ref, lvb_ref, wsv_ref, bsv_ref, ccw_ref, wpool_ref,
                         pscale_ref, wout_ref, lmg_ref, lmb_ref,
                         h_out, sa_out, sc_out, sd_out, v_out, cat_ref):
    dq = d_mix // 4
    x = x_ref[...]
    h = _ln(x, lng_ref[...], lnb_ref[...]) if first_layer else x
    hb = h.astype(BF16)

    def proj(i):
        return _dot(hb, win_ref[:, i * dq:(i + 1) * dq])

    def slab(val, i):
        return val[i * n_seq:(i + 1) * n_seq, :]

    def ext_slabs(state_ref, n_past, cur):
        return ([state_ref[i * n_seq:(i + 1) * n_seq, :] for i in range(n_past)]
                + [slab(cur, i) for i in range(n_step)])

    def store_state(out_ref, slabs, n_keep):
        for i, sl in enumerate(slabs[len(slabs) - n_keep:]):
            out_ref[i * n_seq:(i + 1) * n_seq, :] = sl

    a_glu = proj(0) * jax.nn.sigmoid(proj(1))
    ea = ext_slabs(sa_ref, CONV_A_W - 1, a_glu)
    for l in range(n_step):
        acc = jnp.zeros((n_seq, dq), F32)
        for k in range(CONV_A_W):
            acc = acc + ea[l + k] * caw_ref[k:k + 1, :]
        y = _silu(_ln(acc + cab_ref[...], lag_ref[...], lab_ref[...]))
        cat_ref[l * n_seq:(l + 1) * n_seq, 0:dq] = y.astype(BF16)
    store_state(sa_out, ea, CONV_A_W - 1)

    u = _gelu(proj(2))
    v = _ln(_gelu(proj(3)), lvg_ref[...], lvb_ref[...])
    v_out[...] = v
    for l in range(n_step):
        s = jnp.zeros((n_seq, dq), F32) + bsv_ref[l:l + 1, :]
        for m in range(l + 1):
            s = s + slab(v, m) * wsv_ref[l * n_step + m:l * n_step + m + 1, :]
        cat_ref[l * n_seq:(l + 1) * n_seq, dq:2 * dq] = (slab(u, l) * s).astype(BF16)

    c_x = proj(4)
    c_b = proj(5)
    c_c = proj(6)
    gx = c_c * c_x
    ec = ext_slabs(sc_ref, CONV_C_W - 1, gx)
    for l in range(n_step):
        acc = jnp.zeros((n_seq, dq), F32)
        for k in range(CONV_C_W):
            acc = acc + ec[l + k] * ccw_ref[k:k + 1, :]
        cat_ref[l * n_seq:(l + 1) * n_seq, 2 * dq:3 * dq] = (slab(c_b, l) * acc).astype(BF16)
    store_state(sc_out, ec, CONV_C_W - 1)

    d_in = proj(7)
    ed = ext_slabs(sd_ref, POOL_PAST, d_in)
    memo = {}

    def wsum(i, w):
        if i < 0:
            return None
        if w == 1:
            return ed[i]
        if (i, w) not in memo:
            a, b = wsum(i, w // 2), wsum(i - w // 2, w // 2)
            memo[(i, w)] = a if b is None else a + b
        return memo[(i, w)]

    gc = dq // len(POOL_WINDOWS)
    lane_grp = lax.broadcasted_iota(I32, (n_seq, dq), 1) // gc
    for l in range(n_step):
        mean = jnp.zeros((n_seq, dq), F32)
        for g, w in enumerate(POOL_WINDOWS):
            count = min(PAST_LEN + l + 1, w)
            mean = jnp.where(lane_grp == g, wsum(POOL_PAST + l, w) * (1.0 / count), mean)
        dd = (mean - slab(d_in, l)).astype(BF16)
        cat_ref[l * n_seq:(l + 1) * n_seq, 3 * dq:4 * dq] = (
            _dot(dd, wpool_ref[...]) * pscale_ref[...]).astype(BF16)
    store_state(sd_out, ed, POOL_PAST)

    mix = _dot(cat_ref[...], wout_ref[...])
    h_out[...] = _ln(alpha * h + mix, lmg_ref[...], lmb_ref[...])


def _mixer_sample(x_tm, sa, sc, sd, lw, first_layer, alpha, n_seq, n_step):
    rows, d = x_tm.shape
    d_mix = lw['w_out'].shape[0]
    dq = d_mix // 4
    assert n_seq % SUBLANES == 0 and n_step <= CHUNK and PAST_LEN % CHUNK == 0
    ins = [x_tm, sa, sc, sd, lw['ln_in_g'], lw['ln_in_b'], lw['w_in'], lw['conv_a_w'], lw['conv_a_b'],
           lw['ln_a_g'], lw['ln_a_b'], lw['ln_v_g'], lw['ln_v_b'], lw['ws_vec'], lw['bs_vec'],
           lw['conv_c_w'], lw['w_pool_bd'], lw['pool_scale'], lw['w_out'], lw['ln_mix_g'], lw['ln_mix_b']]
    out_shape = (
        jax.ShapeDtypeStruct((rows, d), F32),
        jax.ShapeDtypeStruct(((CONV_A_W - 1) * n_seq, dq), F32),
        jax.ShapeDtypeStruct(((CONV_C_W - 1) * n_seq, dq), F32),
        jax.ShapeDtypeStruct((POOL_PAST * n_seq, dq), F32),
        jax.ShapeDtypeStruct((rows, dq), F32),
    )
    return pl.pallas_call(
        functools.partial(_mixer_sample_kernel, first_layer, alpha, n_seq, n_step, d_mix),
        grid=(1,),
        in_specs=[_full_spec(a) for a in ins],
        out_specs=tuple(pl.BlockSpec(s.shape, lambda i: (0, 0)) for s in out_shape),
        out_shape=out_shape,
        scratch_shapes=[pltpu.VMEM((rows, d_mix), BF16)],
        compiler_params=_cparams(("arbitrary",)),
        name="mixer_sample",
    )(*ins)


def _stream_cast(streams):
    plans = []
    for pairs, stage, sems in streams:
        rows = stage.shape[1]
        plans.append(([(src, dst, r0) for src, dst in pairs for r0 in range(0, src.shape[0], rows)],
                      stage, sems))

    def copy(plan, c):
        chunks, stage, sems = plan
        src, _, r0 = chunks[c]
        slot = c % stage.shape[0]
        return pltpu.make_async_copy(src.at[pl.ds(r0, stage.shape[1]), :], stage.at[slot], sems.at[slot])

    for plan in plans:
        for c in range(min(plan[1].shape[0], len(plan[0]))):
            copy(plan, c).start()
    for c in range(max(len(plan[0]) for plan in plans)):
        for plan in plans:
            chunks, stage, _ = plan
            if c < len(chunks):
                _, dst, r0 = chunks[c]
                copy(plan, c).wait()
                dst[pl.ds(r0, stage.shape[1]), :] = stage[c % stage.shape[0]].astype(BF16)
                if c + stage.shape[0] < len(chunks):
                    copy(plan, c + stage.shape[0]).start()


def _ffn_dense_kernel(alpha, tf, n_main, x_ref, xe_ref, wg_hbm, wu_hbm, wd_hbm, g_ref, b_ref, o_ref, oe_ref,
                      wg_ref, wu_ref, wd_ref, stage_in, stage_out, sems):
    i = pl.program_id(0)

    @pl.when(i == 0)
    def _():
        _stream_cast([([(wg_hbm, wg_ref), (wu_hbm, wu_ref)], stage_in, sems.at[0]),
                      ([(wd_hbm, wd_ref)], stage_out, sems.at[1])])

    def run(xr, orf):
        x = xr[...]
        xb = x.astype(BF16)
        ffn = None
        for c0 in range(0, wg_ref.shape[1], tf):
            mid = _silu(_dot(xb, wg_ref[:, c0:c0 + tf])) * _dot(xb, wu_ref[:, c0:c0 + tf])
            part = _dot(mid.astype(BF16), wd_ref[c0:c0 + tf, :])
            ffn = part if ffn is None else ffn + part
        orf[...] = _ln(alpha * x + ffn, g_ref[...], b_ref[...])

    @pl.when(i < n_main)
    def _():
        run(x_ref, o_ref)

    @pl.when(i >= n_main)
    def _():
        run(xe_ref, oe_ref)


def _ffn_dense(x, x_extra, wg, wu, wd, g, b, alpha, tm, tf):
    t, d = x.shape
    te = x_extra.shape[0]
    dff = wg.shape[1]
    n_main, n_extra = t // tm, te // tm
    rows_in, rows_out = 128, 512
    assert t % tm == 0 and te % tm == 0 and dff % tf == 0 and d % rows_in == 0 and dff % rows_out == 0
    any_spec = pl.BlockSpec(memory_space=pl.ANY)
    main_blk = lambda i: (jnp.minimum(i, n_main - 1), 0)
    extra_blk = lambda i: (jnp.maximum(i - n_main, 0), 0)
    return pl.pallas_call(
        functools.partial(_ffn_dense_kernel, alpha, tf, n_main),
        grid=(n_main + n_extra,),
        in_specs=[pl.BlockSpec((tm, d), main_blk), pl.BlockSpec((tm, d), extra_blk),
                  any_spec, any_spec, any_spec,
                  pl.BlockSpec((1, d), lambda i: (0, 0)),
                  pl.BlockSpec((1, d), lambda i: (0, 0))],
        out_specs=(pl.BlockSpec((tm, d), main_blk), pl.BlockSpec((tm, d), extra_blk)),
        out_shape=(jax.ShapeDtypeStruct((t, d), F32), jax.ShapeDtypeStruct((te, d), F32)),
        scratch_shapes=[pltpu.VMEM(wg.shape, BF16), pltpu.VMEM(wu.shape, BF16), pltpu.VMEM(wd.shape, BF16),
                        pltpu.VMEM((3, rows_in, dff), F32), pltpu.VMEM((3, rows_out, d), F32),
                        pltpu.SemaphoreType.DMA((2, 3))],
        compiler_params=_cparams(("arbitrary",)),
        name="ffn_dense",
    )(x, x_extra, wg, wu, wd, g, b)


def _route_init(tm, cin_ref, carry_ref, before_ref):
    carry_ref[...] = cin_ref[...].astype(F32)
    row = lax.broadcasted_iota(I32, (tm, tm), 0)
    col = lax.broadcasted_iota(I32, (tm, tm), 1)
    before_ref[...] = jnp.where(row < col, 1.0, 0.0).astype(BF16)


def _route_block(tm, h, wcat_ref, mi_ref, mf_ref, cnt_ref, carry_ref, before_ref):
    h_hi = h.astype(BF16)
    h_lo = (h - h_hi.astype(F32)).astype(BF16)
    p_hi = _dot(h_hi, wcat_ref[...])
    p_lo = _dot(h_lo, wcat_ref[...])
    logits = (p_hi[:, 0:LANES] + (p_lo[:, 0:LANES] + p_hi[:, LANES:2 * LANES])) + p_lo[:, LANES:2 * LANES]
    lg = logits.T[0:N_EXPERTS, :]
    ex = lax.broadcasted_iota(I32, (N_EXPERTS, tm), 0).astype(F32)
    m1 = jnp.max(lg, axis=0, keepdims=True)
    i1 = jnp.min(jnp.where(lg == m1, ex, float(N_EXPERTS)), axis=0, keepdims=True)
    sel1 = ex == i1
    rest = jnp.where(sel1, -jnp.inf, lg)
    m2 = jnp.max(rest, axis=0, keepdims=True)
    i2 = jnp.min(jnp.where(rest == m2, ex, float(N_EXPERTS)), axis=0, keepdims=True)
    sel2 = ex == i2
    e2 = jnp.exp(m2 - m1)
    den = 1.0 + e2
    g1 = 1.0 / den
    g2 = e2 / den

    sel = jnp.where(sel1 | sel2, 1.0, 0.0)
    sel_pad = jnp.concatenate([sel, jnp.zeros_like(sel)], axis=0).astype(BF16)
    base = _dot(sel_pad, before_ref[...])[0:N_EXPERTS, :] + carry_ref[:, 0:1]
    r1 = jnp.sum(jnp.where(sel1, base, 0.0), axis=0, keepdims=True)
    r2 = jnp.sum(jnp.where(sel2, base, 0.0), axis=0, keepdims=True)
    carry_ref[...] = carry_ref[...] + jnp.sum(sel, axis=1, keepdims=True)

    meta = jnp.where(ex == 0, i1, jnp.where(ex == 1, i2, jnp.where(ex == 2, r1, r2)))
    mi_ref[...] = meta.astype(I32)
    gt = jnp.where(ex == 0, g1, jnp.where(ex == 1, g2, 0.0))
    gt = jnp.concatenate([gt, jnp.zeros((LANES - N_EXPERTS, tm), F32)], axis=0)
    mf_ref[...] = gt.T[:, 0:SUBLANES]
    cnt_ref[...] = carry_ref[...].astype(I32)


def _router_kernel(tm, h_ref, wcat_ref, cin_ref, mi_ref, mf_ref, cnt_ref, carry_ref, before_ref):
    @pl.when(pl.program_id(0) == 0)
    def _():
        _route_init(tm, cin_ref, carry_ref, before_ref)

    _route_block(tm, h_ref[...], wcat_ref, mi_ref, mf_ref, cnt_ref, carry_ref, before_ref)


def _router(h, wr_cat, counts_in):
    t, d = h.shape
    tm = min(TM_ROUTE, t)
    assert t % tm == 0 and N_EXPERTS == SUBLANES
    return pl.pallas_call(
        functools.partial(_router_kernel, tm),
        grid=(t // tm,),
        in_specs=[pl.BlockSpec((tm, d), lambda i: (i, 0)), _full_spec(wr_cat), _full_spec(counts_in)],
        out_specs=(pl.BlockSpec((SUBLANES, tm), lambda i: (0, i)),
                   pl.BlockSpec((tm, SUBLANES), lambda i: (i, 0)),
                   pl.BlockSpec((SUBLANES, LANES), lambda i: (0, 0))),
        out_shape=(jax.ShapeDtypeStruct((SUBLANES, t), I32),
                   jax.ShapeDtypeStruct((t, SUBLANES), F32),
                   jax.ShapeDtypeStruct((SUBLANES, LANES), I32)),
        scratch_shapes=[pltpu.VMEM((SUBLANES, LANES), F32), pltpu.VMEM((tm, tm), BF16)],
        compiler_params=_cparams(("arbitrary",)),
        name="router",
    )(h, wr_cat, counts_in)


def _row_copy(src, src_row, dst, dst_row, sem):
    return pltpu.make_async_copy(src.at[pl.ds(src_row, 1), :], dst.at[pl.ds(dst_row, 1), :], sem)


def _rows_wait(src, dst, n_rows, sem):
    pltpu.make_async_copy(src.at[pl.ds(0, n_rows), :], dst.at[pl.ds(0, n_rows), :], sem).wait()


def _dispatch_kernel(tm, tm_expert, n_blocks, blk_ranges, zi_ref, dest_ref, *rest):
    n_src = len(blk_ranges)
    h_refs = rest[:n_src]
    xs_ref, xbuf, zblk, lsems, rsems, zsem = rest[n_src:]
    i = pl.program_id(0)
    n = pl.num_programs(0)

    def block_load(g, start):
        slot = lax.rem(g, N_STAGE)
        for h_ref, (b0, b1) in zip(h_refs, blk_ranges):
            @pl.when((g >= b0) & (g < b1))
            def _(h_ref=h_ref, b0=b0):
                cp = pltpu.make_async_copy(h_ref.at[pl.ds((g - b0) * tm, tm), :], xbuf.at[slot],
                                           lsems.at[slot])
                if start:
                    cp.start()
                else:
                    cp.wait()

    def rows_wait(g):
        slot = lax.rem(g, N_STAGE)
        for _k in range(TOP_K):
            _rows_wait(xbuf.at[slot], xs_ref, tm, rsems.at[slot])

    @pl.when(i == 0)
    def _():
        block_load(i, True)

    @pl.when(i >= N_STAGE - 1)
    def _():
        rows_wait(i - (N_STAGE - 1))

    @pl.when(i + 1 < n)
    def _():
        block_load(i + 1, True)

    block_load(i, False)
    cur = lax.rem(i, N_STAGE)

    def issue(j, carry):
        r0 = pl.multiple_of(j * SUBLANES, SUBLANES)
        for u in range(SUBLANES):
            for k in range(TOP_K):
                dst = dest_ref[0, 0, j * SUBLANES + (k * tm + u)]
                _row_copy(xbuf.at[cur], r0 + u, xs_ref, dst, rsems.at[cur]).start(priority=k)
        return carry

    lax.fori_loop(0, tm // SUBLANES, issue, 0)

    @pl.when(i == n - 1)
    def _():
        for back in range(N_STAGE - 2, -1, -1):
            @pl.when(i >= back)
            def _(back=back):
                rows_wait(i - back)

    @pl.when(i == 0)
    def _():
        zblk[...] = jnp.zeros_like(zblk)
        for e in range(N_EXPERTS):
            lo = zi_ref[e] + zi_ref[N_EXPERTS + e]
            hi = zi_ref[e] + zi_ref[2 * N_EXPERTS + e]

            def zissue(r, carry):
                _row_copy(zblk, 0, xs_ref, r, zsem).start()
                return carry

            def zdrain(r, carry):
                _row_copy(zblk, 0, xs_ref, r, zsem).wait()
                return carry

            lax.fori_loop(lo, hi, zissue, 0)
            lax.fori_loop(lo, hi, zdrain, 0)

        def bcopy(b):
            return pltpu.make_async_copy(zblk, xs_ref.at[pl.ds(b * tm_expert, tm_expert), :], zsem)

        def bissue(b, carry):
            bcopy(b).start()
            return carry

        def bdrain(b, carry):
            bcopy(b).wait()
            return carry

        lax.fori_loop(zi_ref[3 * N_EXPERTS], n_blocks, bissue, 0)
        lax.fori_loop(zi_ref[3 * N_EXPERTS], n_blocks, bdrain, 0)


def _dispatch(hs_list, dest_blk, zinfo, n_blocks, tm_expert):
    d = hs_list[0].shape[1]
    nb = dest_blk.shape[0]
    tm = dest_blk.shape[2] // TOP_K
    blk_ranges, b0 = [], 0
    for h in hs_list:
        assert h.shape[0] % tm == 0 and h.shape[0] >= tm
        blk_ranges.append((b0, b0 + h.shape[0] // tm))
        b0 = blk_ranges[-1][1]
    assert b0 == nb
    any_spec = pl.BlockSpec(memory_space=pl.ANY)
    return pl.pallas_call(
        functools.partial(_dispatch_kernel, tm, tm_expert, n_blocks, tuple(blk_ranges)),
        grid_spec=pltpu.PrefetchScalarGridSpec(
            num_scalar_prefetch=1,
            grid=(nb,),
            in_specs=[pl.BlockSpec((1, 1, TOP_K * tm), lambda i, zi: (i, 0, 0), memory_space=pltpu.SMEM)]
            + [any_spec] * len(hs_list),
            out_specs=any_spec,
            scratch_shapes=[pltpu.VMEM((N_STAGE, tm, d), F32), pltpu.VMEM((tm_expert, d), F32),
                            pltpu.SemaphoreType.DMA((N_STAGE,)), pltpu.SemaphoreType.DMA((N_STAGE,)),
                            pltpu.SemaphoreType.DMA(())],
        ),
        out_shape=jax.ShapeDtypeStruct((n_blocks * tm_expert, d), F32),
        compiler_params=_cparams(("arbitrary",)),
        name="moe_dispatch",
    )(zinfo, dest_blk, *hs_list)


def _ffn_moe_kernel(tf, be_ref, nu_ref, x_ref, wg_hbm, wu_hbm, wd_hbm, y_ref,
                    wg_ref, wu_ref, wd_ref, stage_in, stage_out, sems):
    b = pl.program_id(0)
    nb = pl.num_programs(0)
    d, dff = wg_ref.shape
    n_slots = stage_in.shape[0]
    rows_in = stage_in.shape[1]
    rows_out = stage_out.shape[1]
    n_in, n_out = 2 * (d // rows_in), dff // rows_out
    e = be_ref[b]
    fresh = (b == 0) | (e != be_ref[jnp.maximum(b - 1, 0)])
    e_next = be_ref[jnp.minimum(b + 1, nb - 1)]

    def chunk_in(ex, n):
        src, dst = (wg_hbm, wg_ref) if n < n_in // 2 else (wu_hbm, wu_ref)
        rows = pl.ds((n % (n_in // 2)) * rows_in, rows_in)
        stage = stage_in.at[n % n_slots]
        return pltpu.make_async_copy(src.at[ex, rows, :], stage, sems.at[0, n % n_slots]), stage, dst.at[rows, :]

    def chunk_out(ex, n):
        rows = pl.ds(n * rows_out, rows_out)
        stage = stage_out.at[n % n_slots]
        return pltpu.make_async_copy(wd_hbm.at[ex, rows, :], stage, sems.at[1, n % n_slots]), stage, wd_ref.at[rows, :]

    def request_first(ex):
        for n in range(n_slots):
            chunk_in(ex, n)[0].start()
            chunk_out(ex, n)[0].start()

    @pl.when((b < nu_ref[0]) & fresh)
    def _():
        @pl.when(b == 0)
        def _():
            request_first(e)

        for n in range(n_in):
            for chunk, count in ((chunk_in, n_in), (chunk_out, n_out)):
                if n < count:
                    copy, staged, dst = chunk(e, n)
                    copy.wait()
                    dst[...] = staged[...].astype(BF16)
                    if n + n_slots < count:
                        chunk(e, n + n_slots)[0].start()

    @pl.when((b + 1 < nu_ref[0]) & (e_next != e))
    def _():
        request_first(e_next)

    tm = x_ref.shape[0]
    rows_used = nu_ref[1 + b]

    def compute(rows):
        xb = x_ref[0:rows, :].astype(BF16)
        ffn = None
        for c0 in range(0, dff, tf):
            mid = _silu(_dot(xb, wg_ref[:, c0:c0 + tf])) * _dot(xb, wu_ref[:, c0:c0 + tf])
            part = _dot(mid.astype(BF16), wd_ref[c0:c0 + tf, :])
            ffn = part if ffn is None else ffn + part
        y_ref[0:rows, :] = ffn
        if rows < tm:
            y_ref[rows:tm, :] = jnp.zeros((tm - rows, y_ref.shape[1]), F32)

    step = tm // 4
    for rows in range(step, tm + 1, step):
        fits = (rows_used <= rows) if rows < tm else True
        needs = (rows_used > rows - step) if rows > step else True
        @pl.when((b < nu_ref[0]) & fits & needs)
        def _(rows=rows):
            compute(rows)

    @pl.when(b >= nu_ref[0])
    def _():
        y_ref[...] = jnp.zeros_like(y_ref)


def _ffn_moe(xs, wg, wu, wd, block_e, n_used, tm, tf):
    cap, d = xs.shape
    dff = wg.shape[2]
    rows_in, rows_out = 128, 512
    assert cap % tm == 0 and dff % tf == 0 and d % rows_in == 0 and dff % rows_out == 0
    assert min(2 * (d // rows_in), dff // rows_out) >= N_WSTAGE
    any_spec = pl.BlockSpec(memory_space=pl.ANY)
    return pl.pallas_call(
        functools.partial(_ffn_moe_kernel, tf),
        grid_spec=pltpu.PrefetchScalarGridSpec(
            num_scalar_prefetch=2,
            grid=(cap // tm,),
            in_specs=[pl.BlockSpec((tm, d), lambda b, be, nu: (jnp.minimum(b, nu[0] - 1), 0)),
                      any_spec, any_spec, any_spec],
            out_specs=pl.BlockSpec((tm, d), lambda b, be, nu: (b, 0)),
            scratch_shapes=[pltpu.VMEM((d, dff), BF16), pltpu.VMEM((d, dff), BF16), pltpu.VMEM((dff, d), BF16),
                            pltpu.VMEM((N_WSTAGE, rows_in, dff), F32), pltpu.VMEM((N_WSTAGE, rows_out, d), F32),
                            pltpu.SemaphoreType.DMA((2, N_WSTAGE))],
        ),
        out_shape=jax.ShapeDtypeStruct((cap, d), F32),
        compiler_params=_cparams(("arbitrary",)),
        name="ffn_moe",
    )(block_e, n_used, xs, wg, wu, wd)


def _combine_kernel(alpha, tm, dcur_ref, dnext_ref, h_ref, gate_ref, g_ref, b_ref, ys_ref, o_ref, ybuf, sems):
    i = pl.program_id(0)
    n = pl.num_programs(0)

    def issue(dref, slot):
        def body(j, carry):
            r0 = pl.multiple_of(j * SUBLANES, SUBLANES)
            for u in range(SUBLANES):
                for k in range(TOP_K):
                    src = dref[0, 0, j * SUBLANES + (k * tm + u)]
                    _row_copy(ys_ref, src, ybuf.at[slot, k], r0 + u, sems.at[slot]).start(priority=k)
            return carry

        lax.fori_loop(0, tm // SUBLANES, body, 0)

    @pl.when(i == 0)
    def _():
        issue(dcur_ref, 0)

    @pl.when(i + 1 < n)
    def _():
        issue(dnext_ref, (i + 1) % 2)

    slot = i % 2
    for k in range(TOP_K):
        _rows_wait(ys_ref, ybuf.at[slot, k], tm, sems.at[slot])

    gates = gate_ref[...]
    ffn = gates[:, 0:1] * ybuf[slot, 0] + gates[:, 1:2] * ybuf[slot, 1]
    o_ref[...] = _ln(alpha * h_ref[...] + ffn, g_ref[...], b_ref[...])


def _combine(h, dest_blk, gates, ys, g, b, alpha):
    t, d = h.shape
    nb = dest_blk.shape[0]
    tm = dest_blk.shape[2] // TOP_K
    dest_spec = lambda imap: pl.BlockSpec((1, 1, TOP_K * tm), imap, memory_space=pltpu.SMEM)
    return pl.pallas_call(
        functools.partial(_combine_kernel, alpha, tm),
        grid=(nb,),
        in_specs=[dest_spec(lambda i: (i, 0, 0)),
                  dest_spec(lambda i: (jnp.minimum(i + 1, nb - 1), 0, 0)),
                  pl.BlockSpec((tm, d), lambda i: (i, 0)),
                  pl.BlockSpec((tm, SUBLANES), lambda i: (i, 0)),
                  pl.BlockSpec((1, d), lambda i: (0, 0)),
                  pl.BlockSpec((1, d), lambda i: (0, 0)),
                  pl.BlockSpec(memory_space=pl.ANY)],
        out_specs=pl.BlockSpec((tm, d), lambda i: (i, 0)),
        out_shape=jax.ShapeDtypeStruct((t, d), F32),
        scratch_shapes=[pltpu.VMEM((2, TOP_K, tm, d), F32), pltpu.SemaphoreType.DMA((2,))],
        compiler_params=_cparams(("arbitrary",)),
        name="moe_combine",
    )(dest_blk, dest_blk, h, gates, g, b, ys)


def _ffn_routed(hs_list, first_routed, mw, g, b, alpha, tm_expert):
    routed = [first_routed[:2]]
    counts_in = first_routed[2]
    for h in hs_list[1:]:
        meta_i, gates, counts_in = _router(h, mw['w_router_cat'], counts_in)
        routed.append((meta_i, gates))
    counts = counts_in[:N_EXPERTS, 0]
    padded = (counts + tm_expert - 1) // tm_expert * tm_expert
    pad_end = jnp.cumsum(padded)
    pad_start = (pad_end - padded).astype(I32)
    n_assign = sum(h.shape[0] for h in hs_list) * TOP_K
    n_blocks = -(-n_assign // tm_expert) + N_EXPERTS
    cap = n_blocks * tm_expert
    n_used = (pad_end[-1] // tm_expert).astype(I32)
    blk_start = jnp.minimum(jnp.arange(n_blocks, dtype=I32), n_used - 1) * tm_expert
    block_e = jnp.minimum(jnp.sum(blk_start[:, None] >= pad_end[None, :], axis=1), N_EXPERTS - 1).astype(I32)
    zinfo = jnp.concatenate([pad_start, counts, padded, n_used.reshape(1)]).astype(I32)

    dests = []
    for h, (meta_i, _) in zip(hs_list, routed):
        dest = meta_i[TOP_K:2 * TOP_K, :]
        for e in range(N_EXPERTS):
            dest = dest + jnp.where(meta_i[0:TOP_K, :] == e, pad_start[e], 0)
        nb = h.shape[0] // TM_MOVE
        dests.append(dest.reshape(TOP_K, nb, TM_MOVE).transpose(1, 0, 2).reshape(nb, 1, TOP_K * TM_MOVE))
    xs = _dispatch(hs_list, jnp.concatenate(dests, axis=0), zinfo, n_blocks, tm_expert)
    filled_end = (pad_start + counts)[block_e]
    occupancy = jnp.clip(filled_end - jnp.arange(n_blocks, dtype=I32) * tm_expert, 0, tm_expert)
    used_info = jnp.concatenate([n_used.reshape(1), occupancy]).astype(I32)
    ys = _ffn_moe(xs, mw['w_e_gate'], mw['w_e_up'], mw['w_e_down'], block_e, used_info, tm_expert,
                  TF_EXPERT)
    return [_combine(h, dest_blk, gates, ys, g, b, alpha)
            for h, dest_blk, (_, gates) in zip(hs_list, dests, routed)]


def _prep_layer(i, p, n_step):
    dq = p['conv_a_w'].shape[2]
    hd = dq // N_HEADS_B
    tril = jnp.tril(jnp.ones((CHUNK, CHUNK), dtype=bool))
    w_s = p['w_s'][i]
    b_s = p['b_s'][i]
    wp = p['w_pool'][i]
    ng, gc = wp.shape[0], wp.shape[1]
    w_pool_bd = (jnp.eye(ng, dtype=F32)[:, None, :, None] * wp[:, :, None, :]).reshape(ng * gc, ng * gc)
    row = lambda a: a.reshape(1, -1)
    return dict(
        ln_in_g=row(p['ln_in_g']), ln_in_b=row(p['ln_in_b']),
        w_in=p['w_in'][i].astype(BF16),
        conv_a_w=p['conv_a_w'][i], conv_a_b=row(p['conv_a_b'][i]),
        ln_a_g=row(p['ln_a_g'][i]), ln_a_b=row(p['ln_a_b'][i]),
        ln_v_g=row(p['ln_v_g'][i]), ln_v_b=row(p['ln_v_b'][i]),
        ws_tril=jnp.where(tril[None], w_s, 0).astype(BF16),
        bs_mat=jnp.repeat(b_s.T, hd, axis=1),
        ws_vec=jnp.repeat(jnp.where(tril[None], w_s, 0)[:, :n_step, :n_step].transpose(1, 2, 0)
                          .reshape(n_step * n_step, N_HEADS_B), hd, axis=1),
        bs_vec=jnp.repeat(b_s[:, :n_step].T, hd, axis=1),
        conv_c_w=p['conv_c_w'][i],
        w_pool_bd=w_pool_bd.astype(BF16), pool_scale=row(p['pool_scale'][i]),
        w_out=p['w_out'][i].astype(BF16),
        ln_mix_g=row(p['ln_mix_g'][i]), ln_mix_b=row(p['ln_mix_b'][i]),
    )


def kernel(x_prompt, x_sample, state_conv_a, state_conv_c, state_pool_d, ln_in_g, ln_in_b, w_in, conv_a_w,
           conv_a_b, ln_a_g, ln_a_b, ln_v_g, ln_v_b, w_s, b_s, conv_c_w, w_pool, pool_scale, w_out,
           ln_mix_g, ln_mix_b, w_ff_gate, w_ff_up, w_ff_down, w_router, w_e_gate, w_e_up, w_e_down,
           ln_ffn_g, ln_ffn_b):
    p = dict(ln_in_g=ln_in_g, ln_in_b=ln_in_b, w_in=w_in, conv_a_w=conv_a_w, conv_a_b=conv_a_b,
             ln_a_g=ln_a_g, ln_a_b=ln_a_b, ln_v_g=ln_v_g, ln_v_b=ln_v_b, w_s=w_s, b_s=b_s,
             conv_c_w=conv_c_w, w_pool=w_pool, pool_scale=pool_scale, w_out=w_out,
             ln_mix_g=ln_mix_g, ln_mix_b=ln_mix_b)
    depth = w_in.shape[0]
    bsz, seq, d = x_prompt.shape
    n_seq, n_step, _ = x_sample.shape
    dq = conv_a_w.shape[2]
    alpha = float((2.0 * depth) ** 0.25)

    hp = x_prompt
    hs = x_sample.transpose(1, 0, 2).reshape(n_step * n_seq, d)
    outs = {k: [] for k in ('pa', 'pc', 'pd', 'pv', 'sa', 'sc', 'sd', 'sv')}
    for i in range(depth):
        lw = _prep_layer(i, p, n_step)
        j = i // 2
        routed = i % 2 == 1
        if routed:
            wr = jnp.pad(w_router[j], ((0, 0), (0, LANES - N_EXPERTS)))
            wr_hi = wr.astype(BF16)
            wr_lo = (wr - wr_hi.astype(F32)).astype(BF16)
            wr_cat = jnp.concatenate([wr_hi, wr_lo], axis=1)
        hp, pa, pc, pd, pv, *route_p = _mixer_prompt(hp, lw, i == 0, alpha, wr_cat if routed else None)
        tm_state = lambda s: s.transpose(1, 0, 2).reshape(-1, dq)
        hs, sa, sc, sd, sv = _mixer_sample(hs, tm_state(state_conv_a[i]), tm_state(state_conv_c[i]),
                                           tm_state(state_pool_d[i]), lw, i == 0, alpha, n_seq, n_step)
        lg, lb = ln_ffn_g[i].reshape(1, -1), ln_ffn_b[i].reshape(1, -1)
        if not routed:
            hp, hs = _ffn_dense(hp.reshape(bsz * seq, d), hs, w_ff_gate[j], w_ff_up[j], w_ff_down[j], lg, lb,
                                alpha, TM_DENSE, TF_DENSE)
            hp = hp.reshape(bsz, seq, d)
        else:
            mw = dict(w_router_cat=wr_cat, w_e_gate=w_e_gate[j], w_e_up=w_e_up[j], w_e_down=w_e_down[j])
            hp, hs = _ffn_routed([hp.reshape(bsz * seq, d), hs], route_p, mw, lg, lb, alpha, TM_EXPERT)
            hp = hp.reshape(bsz, seq, d)
        back = lambda a: a.reshape(-1, n_seq, dq).transpose(1, 0, 2)
        for k, val in zip(('pa', 'pc', 'pd', 'pv', 'sa', 'sc', 'sd', 'sv'),
                          (pa, pc, pd, pv, back(sa), back(sc), back(sd), back(sv))):
            outs[k].append(val)
    y_sample = hs.reshape(n_step, n_seq, d).transpose(1, 0, 2)
    st = lambda k: jnp.stack(outs[k])
    return (hp, y_sample, st('pa'), st('pc'), st('pd'), st('pv'), st('sa'), st('sc'), st('sd'), st('sv'))
```

```python
import functools

import numpy as np
import jax
import jax.numpy as jnp
from jax import lax
from jax.experimental import pallas as pl
from jax.experimental.pallas import tpu as pltpu

F32 = jnp.float32
BF16 = jnp.bfloat16
I32 = jnp.int32

PAST_LEN = 16384
CHUNK = 128
N_HEADS_B = 4
POOL_WINDOWS = (2, 4, 8, 16)
POOL_PAST = 15
CONV_A_W = 31
CONV_C_W = 3
N_EXPERTS = 8
TOP_K = 2
LN_EPS = 1e-5
SQRT_HALF = float(np.sqrt(0.5).astype(np.float32))

LANES = 128
SUBLANES = 8
VMEM_LIMIT_BYTES = 56 * 1024 * 1024

HIST_A = 32
HIST_C = 8
HIST_D = 24

TL_PROMPT = 512
TM_DENSE = 512
TF_DENSE = 1792
TF_EXPERT = 1792
TM_ROUTE = 512
TM_MOVE = 256
TM_EXPERT = 512
N_STAGE = 3
N_WSTAGE = 5


def _ln(x, g, b):
    mu = jnp.mean(x, axis=-1, keepdims=True)
    xc = x - mu
    var = jnp.mean(xc * xc, axis=-1, keepdims=True)
    return xc * lax.rsqrt(var + LN_EPS) * g + b


def _gelu(x):
    return 0.5 * x * (1.0 + lax.erf(x * SQRT_HALF))


def _silu(x):
    return x * jax.nn.sigmoid(x)


def _dot(a, b):
    return jnp.dot(a, b, preferred_element_type=F32)


def _cparams(sem):
    return pltpu.CompilerParams(dimension_semantics=sem, vmem_limit_bytes=VMEM_LIMIT_BYTES)


def _mixer_prompt_kernel(first_layer, route, defer_norm, alpha, tl, d_mix,
                         x_ref, lng_ref, lnb_ref, win_ref, caw_ref, cab_ref, lag_ref, lab_ref,
                         lvg_ref, lvb_ref, ws_ref, bsm_ref, ccw_ref, wpool_ref, pscale_ref,
                         wout_ref, lmg_ref, lmb_ref, *refs):
    if route:
        wcat_ref, cin_ref = refs[:2]
        refs = refs[2:]
    h_out, sa_out, sc_out, sd_out, v_out = refs[:5]
    refs = refs[5:]
    if route:
        mi_ref, mf_ref, cnt_ref = refs[:3]
        refs = refs[3:]
    ext_a, ext_c, ext_d, cat_ref, z_ref, hb_ref, mix_ref = refs[:7]
    dq = d_mix // 4
    j = pl.program_id(1)
    nj = pl.num_programs(1)
    if route:
        carry_ref, before_ref = refs[7:]

        @pl.when((pl.program_id(0) == 0) & (j == 0))
        def _():
            _route_init(tl, cin_ref, carry_ref, before_ref)

    @pl.when(j == 0)
    def _():
        ext_a[0, 0:HIST_A, :] = jnp.zeros((HIST_A, dq), F32)
        ext_c[0:HIST_C, :] = jnp.zeros((HIST_C, dq), F32)
        ext_d[0:HIST_D, :] = jnp.zeros((HIST_D, dq), F32)

    d = x_ref.shape[2]
    rb = CHUNK

    def resid(r0, n):
        return h_out[0, r0:r0 + n, :] if first_layer else x_ref[0, r0:r0 + n, :]

    for r0 in range(0, tl, rb):
        x = x_ref[0, r0:r0 + rb, :]
        if first_layer:
            x = _ln(x, lng_ref[...], lnb_ref[...])
            h_out[0, r0:r0 + rb, :] = x
        hb_ref[r0:r0 + rb, :] = x.astype(BF16)

    z_ref[...] = _dot(hb_ref[...], win_ref[...])

    def zcol(r0, i, n=rb):
        return z_ref[r0:r0 + n, i * dq:(i + 1) * dq]

    chunk_rows = range(0, tl, CHUNK)

    for r0 in chunk_rows:
        ext_a[0, HIST_A + r0:HIST_A + r0 + rb, :] = zcol(r0, 0) * jax.nn.sigmoid(zcol(r0, 1))
    n_ext = HIST_A + tl
    for r0 in range(0, n_ext - SUBLANES, rb):
        n = min(rb, n_ext - SUBLANES - r0)
        blk = ext_a[0, r0:r0 + n + SUBLANES, :]
        for s in range(1, SUBLANES):
            ext_a[s, r0:r0 + n, :] = pltpu.roll(blk, n + SUBLANES - s, axis=0)[0:n, :]
    off_a = HIST_A - (CONV_A_W - 1)
    for q0 in range(0, tl, 64):
        acc = jnp.zeros((64, dq), F32)
        for k in range(CONV_A_W):
            s = (off_a + k) % SUBLANES
            row = off_a + k - s + q0
            acc = acc + ext_a[s, row:row + 64, :] * caw_ref[k:k + 1, :]
        y = _silu(_ln(acc + cab_ref[...], lag_ref[...], lab_ref[...]))
        cat_ref[q0:q0 + 64, 0:dq] = y.astype(BF16)

    hd = dq // N_HEADS_B
    lane_head = lax.broadcasted_iota(I32, (CHUNK, dq), 1) // hd
    for r0 in chunk_rows:
        v = _ln(_gelu(zcol(r0, 3)), lvg_ref[...], lvb_ref[...])
        sm = bsm_ref[...]
        for hh in range(N_HEADS_B):
            vm = jnp.where(lane_head == hh, v, 0.0).astype(BF16)
            sm = sm + _dot(ws_ref[hh], vm)
        cat_ref[r0:r0 + CHUNK, dq:2 * dq] = (_gelu(zcol(r0, 2)) * sm).astype(BF16)
        if r0 == tl - CHUNK:
            @pl.when(j == nj - 1)
            def _(v=v):
                v_out[0] = v

    off_c = HIST_C - (CONV_C_W - 1)
    for r0 in chunk_rows:
        ext_c[HIST_C + r0:HIST_C + r0 + rb, :] = zcol(r0, 6) * zcol(r0, 4)
    for r0 in chunk_rows:
        conv_c = jnp.zeros((CHUNK, dq), F32)
        for k in range(CONV_C_W):
            conv_c = conv_c + ext_c[off_c + r0 + k:off_c + r0 + k + CHUNK, :] * ccw_ref[k:k + 1, :]
        cat_ref[r0:r0 + CHUNK, 2 * dq:3 * dq] = (zcol(r0, 5) * conv_c).astype(BF16)

    gc = dq // len(POOL_WINDOWS)
    lane_grp = lax.broadcasted_iota(I32, (CHUNK, dq), 1) // gc
    lead = 2 * SUBLANES
    for r0 in chunk_rows:
        ext_d[HIST_D + r0:HIST_D + r0 + rb, :] = zcol(r0, 7)
    for r0 in chunk_rows:
        e = ext_d[HIST_D + r0 - lead:HIST_D + r0 + CHUNK, :]
        s2 = e + pltpu.roll(e, 1, axis=0)
        s4 = s2 + pltpu.roll(s2, 2, axis=0)
        s8 = s4 + pltpu.roll(s4, 4, axis=0)
        s16 = s8 + pltpu.roll(s8, 8, axis=0)
        sums = (s2, s4, s8, s16)
        pos = (j * tl + r0 + lax.broadcasted_iota(I32, (CHUNK, 1), 0)).astype(F32)
        mean = jnp.zeros((CHUNK, dq), F32)
        for g, w in enumerate(POOL_WINDOWS):
            inv = 1.0 / jnp.minimum(pos + 1.0, float(w))
            mean = jnp.where(lane_grp == g, sums[g][lead:lead + CHUNK, :] * inv, mean)
        dd = (mean - e[lead:lead + CHUNK, :]).astype(BF16)
        cat_ref[r0:r0 + CHUNK, 3 * dq:4 * dq] = (_dot(dd, wpool_ref[...]) * pscale_ref[...]).astype(BF16)

    mix_ref[...] = _dot(cat_ref[...], wout_ref[...])
    for r0 in range(0, tl, 64):
        pre = alpha * resid(r0, 64) + mix_ref[r0:r0 + 64, :]
        h_out[0, r0:r0 + 64, :] = pre if defer_norm else _ln(pre, lmg_ref[...], lmb_ref[...])
    if route:
        _route_block(tl, h_out[0], wcat_ref, mi_ref, mf_ref, cnt_ref, carry_ref, before_ref)

    @pl.when(j == nj - 1)
    def _():
        sa_out[0] = ext_a[0, HIST_A + tl - (CONV_A_W - 1):HIST_A + tl, :]
        sc_out[0] = ext_c[HIST_C + tl - (CONV_C_W - 1):HIST_C + tl, :]
        sd_out[0] = ext_d[HIST_D + tl - POOL_PAST:HIST_D + tl, :]

    ext_a[0, 0:HIST_A, :] = ext_a[0, tl:tl + HIST_A, :]
    ext_c[0:HIST_C, :] = ext_c[tl:tl + HIST_C, :]
    ext_d[0:HIST_D, :] = ext_d[tl:tl + HIST_D, :]


def _full_spec(arr):
    nd = arr.ndim
    return pl.BlockSpec(arr.shape, lambda *_: (0,) * nd)


def _mixer_prompt(x, lw, first_layer, alpha, route_w=None, defer_norm=False):
    assert not (defer_norm and route_w is not None)
    bsz, seq, d = x.shape
    d_mix = lw['w_out'].shape[0]
    dq = d_mix // 4
    tl = TL_PROMPT
    nj = seq // tl
    route = route_w is not None
    assert seq % tl == 0 and tl % CHUNK == 0 and seq >= CHUNK
    params = [lw['ln_in_g'], lw['ln_in_b'], lw['w_in'], lw['conv_a_w'], lw['conv_a_b'], lw['ln_a_g'],
              lw['ln_a_b'], lw['ln_v_g'], lw['ln_v_b'], lw['ws_tril'], lw['bs_mat'], lw['conv_c_w'],
              lw['w_pool_bd'], lw['pool_scale'], lw['w_out'], lw['ln_mix_g'], lw['ln_mix_b']]
    out_shape = [
        jax.ShapeDtypeStruct((bsz, seq, d), F32),
        jax.ShapeDtypeStruct((bsz, CONV_A_W - 1, dq), F32),
        jax.ShapeDtypeStruct((bsz, CONV_C_W - 1, dq), F32),
        jax.ShapeDtypeStruct((bsz, POOL_PAST, dq), F32),
        jax.ShapeDtypeStruct((bsz, CHUNK, dq), F32),
    ]
    state_spec = lambda rows: pl.BlockSpec((1, rows, dq), lambda b, j: (b, 0, 0))
    out_specs = [pl.BlockSpec((1, tl, d), lambda b, j: (b, j, 0)),
                 state_spec(CONV_A_W - 1), state_spec(CONV_C_W - 1), state_spec(POOL_PAST), state_spec(CHUNK)]
    scratch = [pltpu.VMEM((SUBLANES, HIST_A + tl, dq), F32), pltpu.VMEM((HIST_C + tl, dq), F32),
               pltpu.VMEM((HIST_D + tl, dq), F32), pltpu.VMEM((tl, d_mix), BF16),
               pltpu.VMEM((tl, lw['w_in'].shape[1]), F32), pltpu.VMEM((tl, d), BF16),
               pltpu.VMEM((tl, d), F32)]
    if route:
        assert N_EXPERTS == SUBLANES
        params += [route_w, jnp.zeros((SUBLANES, LANES), I32)]
        out_shape += [jax.ShapeDtypeStruct((SUBLANES, bsz * seq), I32),
                      jax.ShapeDtypeStruct((bsz * seq, SUBLANES), F32),
                      jax.ShapeDtypeStruct((SUBLANES, LANES), I32)]
        out_specs += [pl.BlockSpec((SUBLANES, tl), lambda b, j: (0, b * nj + j)),
                      pl.BlockSpec((tl, SUBLANES), lambda b, j: (b * nj + j, 0)),
                      pl.BlockSpec((SUBLANES, LANES), lambda b, j: (0, 0))]
        scratch += [pltpu.VMEM((SUBLANES, LANES), F32), pltpu.VMEM((tl, tl), BF16)]
    return pl.pallas_call(
        functools.partial(_mixer_prompt_kernel, first_layer, route, defer_norm, alpha, tl, d_mix),
        grid=(bsz, nj),
        in_specs=[pl.BlockSpec((1, tl, d), lambda b, j: (b, j, 0))] + [_full_spec(p) for p in params],
        out_specs=tuple(out_specs),
        out_shape=tuple(out_shape),
        scratch_shapes=scratch,
        compiler_params=_cparams(("arbitrary", "arbitrary")),
        name="mixer_prompt",
    )(x, *params)


def _mixer_sample_kernel(first_layer, alpha, n_seq, n_step, d_mix,
                         x_ref, sa_ref, sc_ref, sd_ref, lng_ref, lnb_ref, win_ref, caw_ref, cab_ref,
                         lag_ref, lab_ref, lvg_ref, lvb_ref, wsv_ref, bsv_ref, ccw_ref, wpool_ref,
                         pscale_ref, wout_ref, lmg_ref, lmb_ref,
                         h_out, sa_out, sc_out, sd_out, v_out, cat_ref):
    dq = d_mix // 4
    x = x_ref[...]
    h = _ln(x, lng_ref[...], lnb_ref[...]) if first_layer else x
    hb = h.astype(BF16)

    def proj(i):
        return _dot(hb, win_ref[:, i * dq:(i + 1) * dq])

    def slab(val, i):
        return val[i * n_seq:(i + 1) * n_seq, :]

    def ext_slabs(state_ref, n_past, cur):
        return ([state_ref[i * n_seq:(i + 1) * n_seq, :] for i in range(n_past)]
                + [slab(cur, i) for i in range(n_step)])

    def store_state(out_ref, slabs, n_keep):
        for i, sl in enumerate(slabs[len(slabs) - n_keep:]):
            out_ref[i * n_seq:(i + 1) * n_seq, :] = sl

    a_glu = proj(0) * jax.nn.sigmoid(proj(1))
    ea = ext_slabs(sa_ref, CONV_A_W - 1, a_glu)
    for l in range(n_step):
        acc = jnp.zeros((n_seq, dq), F32)
        for k in range(CONV_A_W):
            acc = acc + ea[l + k] * caw_ref[k:k + 1, :]
        y = _silu(_ln(acc + cab_ref[...], lag_ref[...], lab_ref[...]))
        cat_ref[l * n_seq:(l + 1) * n_seq, 0:dq] = y.astype(BF16)
    store_state(sa_out, ea, CONV_A_W - 1)

    u = _gelu(proj(2))
    v = _ln(_gelu(proj(3)), lvg_ref[...], lvb_ref[...])
    v_out[...] = v
    for l in range(n_step):
        s = jnp.zeros((n_seq, dq), F32) + bsv_ref[l:l + 1, :]
        for m in range(l + 1):
            s = s + slab(v, m) * wsv_ref[l * n_step + m:l * n_step + m + 1, :]
        cat_ref[l * n_seq:(l + 1) * n_seq, dq:2 * dq] = (slab(u, l) * s).astype(BF16)

    c_x = proj(4)
    c_b = proj(5)
    c_c = proj(6)
    gx = c_c * c_x
    ec = ext_slabs(sc_ref, CONV_C_W - 1, gx)
    for l in range(n_step):
        acc = jnp.zeros((n_seq, dq), F32)
        for k in range(CONV_C_W):
            acc = acc + ec[l + k] * ccw_ref[k:k + 1, :]
        cat_ref[l * n_seq:(l + 1) * n_seq, 2 * dq:3 * dq] = (slab(c_b, l) * acc).astype(BF16)
    store_state(sc_out, ec, CONV_C_W - 1)

    d_in = proj(7)
    ed = ext_slabs(sd_ref, POOL_PAST, d_in)
    memo = {}

    def wsum(i, w):
        if i < 0:
            return None
        if w == 1:
            return ed[i]
        if (i, w) not in memo:
            a, b = wsum(i, w // 2), wsum(i - w // 2, w // 2)
            memo[(i, w)] = a if b is None else a + b
        return memo[(i, w)]

    gc = dq // len(POOL_WINDOWS)
    lane_grp = lax.broadcasted_iota(I32, (n_seq, dq), 1) // gc
    for l in range(n_step):
        mean = jnp.zeros((n_seq, dq), F32)
        for g, w in enumerate(POOL_WINDOWS):
            count = min(PAST_LEN + l + 1, w)
            mean = jnp.where(lane_grp == g, wsum(POOL_PAST + l, w) * (1.0 / count), mean)
        dd = (mean - slab(d_in, l)).astype(BF16)
        cat_ref[l * n_seq:(l + 1) * n_seq, 3 * dq:4 * dq] = (
            _dot(dd, wpool_ref[...]) * pscale_ref[...]).astype(BF16)
    store_state(sd_out, ed, POOL_PAST)

    mix = _dot(cat_ref[...], wout_ref[...])
    h_out[...] = _ln(alpha * h + mix, lmg_ref[...], lmb_ref[...])


def _mixer_sample(x_tm, sa, sc, sd, lw, first_layer, alpha, n_seq, n_step):
    rows, d = x_tm.shape
    d_mix = lw['w_out'].shape[0]
    dq = d_mix // 4
    assert n_seq % SUBLANES == 0 and n_step <= CHUNK and PAST_LEN % CHUNK == 0
    ins = [x_tm, sa, sc, sd, lw['ln_in_g'], lw['ln_in_b'], lw['w_in'], lw['conv_a_w'], lw['conv_a_b'],
           lw['ln_a_g'], lw['ln_a_b'], lw['ln_v_g'], lw['ln_v_b'], lw['ws_vec'], lw['bs_vec'],
           lw['conv_c_w'], lw['w_pool_bd'], lw['pool_scale'], lw['w_out'], lw['ln_mix_g'], lw['ln_mix_b']]
    out_shape = (
        jax.ShapeDtypeStruct((rows, d), F32),
        jax.ShapeDtypeStruct(((CONV_A_W - 1) * n_seq, dq), F32),
        jax.ShapeDtypeStruct(((CONV_C_W - 1) * n_seq, dq), F32),
        jax.ShapeDtypeStruct((POOL_PAST * n_seq, dq), F32),
        jax.ShapeDtypeStruct((rows, dq), F32),
    )
    return pl.pallas_call(
        functools.partial(_mixer_sample_kernel, first_layer, alpha, n_seq, n_step, d_mix),
        grid=(1,),
        in_specs=[_full_spec(a) for a in ins],
        out_specs=tuple(pl.BlockSpec(s.shape, lambda i: (0, 0)) for s in out_shape),
        out_shape=out_shape,
        scratch_shapes=[pltpu.VMEM((rows, d_mix), BF16)],
        compiler_params=_cparams(("arbitrary",)),
        name="mixer_sample",
    )(*ins)


def _stream_cast(streams):
    plans = []
    for pairs, stage, sems in streams:
        rows = stage.shape[1]
        plans.append(([(src, dst, r0) for src, dst in pairs for r0 in range(0, src.shape[0], rows)],
                      stage, sems))

    def copy(plan, c):
        chunks, stage, sems = plan
        src, _, r0 = chunks[c]
        slot = c % stage.shape[0]
        return pltpu.make_async_copy(src.at[pl.ds(r0, stage.shape[1]), :], stage.at[slot], sems.at[slot])

    for plan in plans:
        for c in range(min(plan[1].shape[0], len(plan[0]))):
            copy(plan, c).start()
    for c in range(max(len(plan[0]) for plan in plans)):
        for plan in plans:
            chunks, stage, _ = plan
            if c < len(chunks):
                _, dst, r0 = chunks[c]
                copy(plan, c).wait()
                dst[pl.ds(r0, stage.shape[1]), :] = stage[c % stage.shape[0]].astype(BF16)
                if c + stage.shape[0] < len(chunks):
                    copy(plan, c + stage.shape[0]).start()


def _ffn_dense_kernel(alpha, tf, n_main, x_ref, xe_ref, wg_hbm, wu_hbm, wd_hbm, pg_ref, pb_ref, g_ref, b_ref,
                      o_ref, oe_ref, wg_ref, wu_ref, wd_ref, stage_in, stage_out, sems):
    i = pl.program_id(0)

    @pl.when(i == 0)
    def _():
        _stream_cast([([(wg_hbm, wg_ref), (wu_hbm, wu_ref)], stage_in, sems.at[0]),
                      ([(wd_hbm, wd_ref)], stage_out, sems.at[1])])

    def run(xr, orf, pre_norm):
        x = xr[...]
        if pre_norm:
            x = _ln(x, pg_ref[...], pb_ref[...])
        xb = x.astype(BF16)
        ffn = None
        for c0 in range(0, wg_ref.shape[1], tf):
            mid = _silu(_dot(xb, wg_ref[:, c0:c0 + tf])) * _dot(xb, wu_ref[:, c0:c0 + tf])
            part = _dot(mid.astype(BF16), wd_ref[c0:c0 + tf, :])
            ffn = part if ffn is None else ffn + part
        orf[...] = _ln(alpha * x + ffn, g_ref[...], b_ref[...])

    @pl.when(i < n_main)
    def _():
        run(x_ref, o_ref, True)

    @pl.when(i >= n_main)
    def _():
        run(xe_ref, oe_ref, False)


def _ffn_dense(x, x_extra, pre_g, pre_b, wg, wu, wd, g, b, alpha, tm, tf):
    t, d = x.shape
    te = x_extra.shape[0]
    dff = wg.shape[1]
    n_main, n_extra = t // tm, te // tm
    rows_in, rows_out = 128, 512
    assert t % tm == 0 and te % tm == 0 and dff % tf == 0 and d % rows_in == 0 and dff % rows_out == 0
    any_spec = pl.BlockSpec(memory_space=pl.ANY)
    main_blk = lambda i: (jnp.minimum(i, n_main - 1), 0)
    extra_blk = lambda i: (jnp.maximum(i - n_main, 0), 0)
    return pl.pallas_call(
        functools.partial(_ffn_dense_kernel, alpha, tf, n_main),
        grid=(n_main + n_extra,),
        in_specs=[pl.BlockSpec((tm, d), main_blk), pl.BlockSpec((tm, d), extra_blk),
                  any_spec, any_spec, any_spec] + [pl.BlockSpec((1, d), lambda i: (0, 0))] * 4,
        out_specs=(pl.BlockSpec((tm, d), main_blk), pl.BlockSpec((tm, d), extra_blk)),
        out_shape=(jax.ShapeDtypeStruct((t, d), F32), jax.ShapeDtypeStruct((te, d), F32)),
        scratch_shapes=[pltpu.VMEM(wg.shape, BF16), pltpu.VMEM(wu.shape, BF16), pltpu.VMEM(wd.shape, BF16),
                        pltpu.VMEM((2, rows_in, dff), F32), pltpu.VMEM((3, rows_out, d), F32),
                        pltpu.SemaphoreType.DMA((2, 3))],
        compiler_params=_cparams(("arbitrary",)),
        name="ffn_dense",
    )(x, x_extra, wg, wu, wd, pre_g, pre_b, g, b)


def _route_init(tm, cin_ref, carry_ref, before_ref):
    carry_ref[...] = cin_ref[...].astype(F32)
    row = lax.broadcasted_iota(I32, (tm, tm), 0)
    col = lax.broadcasted_iota(I32, (tm, tm), 1)
    before_ref[...] = jnp.where(row < col, 1.0, 0.0).astype(BF16)


def _route_block(tm, h, wcat_ref, mi_ref, mf_ref, cnt_ref, carry_ref, before_ref):
    h_hi = h.astype(BF16)
    h_lo = (h - h_hi.astype(F32)).astype(BF16)
    p_hi = _dot(h_hi, wcat_ref[...])
    p_lo = _dot(h_lo, wcat_ref[...])
    logits = (p_hi[:, 0:LANES] + (p_lo[:, 0:LANES] + p_hi[:, LANES:2 * LANES])) + p_lo[:, LANES:2 * LANES]
    lg = logits.T[0:N_EXPERTS, :]
    ex = lax.broadcasted_iota(I32, (N_EXPERTS, tm), 0).astype(F32)
    m1 = jnp.max(lg, axis=0, keepdims=True)
    i1 = jnp.min(jnp.where(lg == m1, ex, float(N_EXPERTS)), axis=0, keepdims=True)
    sel1 = ex == i1
    rest = jnp.where(sel1, -jnp.inf, lg)
    m2 = jnp.max(rest, axis=0, keepdims=True)
    i2 = jnp.min(jnp.where(rest == m2, ex, float(N_EXPERTS)), axis=0, keepdims=True)
    sel2 = ex == i2
    e2 = jnp.exp(m2 - m1)
    den = 1.0 + e2
    g1 = 1.0 / den
    g2 = e2 / den

    sel = jnp.where(sel1 | sel2, 1.0, 0.0)
    sel_pad = jnp.concatenate([sel, jnp.zeros_like(sel)], axis=0).astype(BF16)
    base = _dot(sel_pad, before_ref[...])[0:N_EXPERTS, :] + carry_ref[:, 0:1]
    r1 = jnp.sum(jnp.where(sel1, base, 0.0), axis=0, keepdims=True)
    r2 = jnp.sum(jnp.where(sel2, base, 0.0), axis=0, keepdims=True)
    carry_ref[...] = carry_ref[...] + jnp.sum(sel, axis=1, keepdims=True)

    meta = jnp.where(ex == 0, i1, jnp.where(ex == 1, i2, jnp.where(ex == 2, r1, r2)))
    mi_ref[...] = meta.astype(I32)
    gt = jnp.where(ex == 0, g1, jnp.where(ex == 1, g2, 0.0))
    gt = jnp.concatenate([gt, jnp.zeros((LANES - N_EXPERTS, tm), F32)], axis=0)
    mf_ref[...] = gt.T[:, 0:SUBLANES]
    cnt_ref[...] = carry_ref[...].astype(I32)


def _router_kernel(tm, h_ref, wcat_ref, cin_ref, mi_ref, mf_ref, cnt_ref, carry_ref, before_ref):
    @pl.when(pl.program_id(0) == 0)
    def _():
        _route_init(tm, cin_ref, carry_ref, before_ref)

    _route_block(tm, h_ref[...], wcat_ref, mi_ref, mf_ref, cnt_ref, carry_ref, before_ref)


def _router(h, wr_cat, counts_in):
    t, d = h.shape
    tm = min(TM_ROUTE, t)
    assert t % tm == 0 and N_EXPERTS == SUBLANES
    return pl.pallas_call(
        functools.partial(_router_kernel, tm),
        grid=(t // tm,),
        in_specs=[pl.BlockSpec((tm, d), lambda i: (i, 0)), _full_spec(wr_cat), _full_spec(counts_in)],
        out_specs=(pl.BlockSpec((SUBLANES, tm), lambda i: (0, i)),
                   pl.BlockSpec((tm, SUBLANES), lambda i: (i, 0)),
                   pl.BlockSpec((SUBLANES, LANES), lambda i: (0, 0))),
        out_shape=(jax.ShapeDtypeStruct((SUBLANES, t), I32),
                   jax.ShapeDtypeStruct((t, SUBLANES), F32),
                   jax.ShapeDtypeStruct((SUBLANES, LANES), I32)),
        scratch_shapes=[pltpu.VMEM((SUBLANES, LANES), F32), pltpu.VMEM((tm, tm), BF16)],
        compiler_params=_cparams(("arbitrary",)),
        name="router",
    )(h, wr_cat, counts_in)


def _row_copy(src, src_row, dst, dst_row, sem):
    return pltpu.make_async_copy(src.at[pl.ds(src_row, 1), :], dst.at[pl.ds(dst_row, 1), :], sem)


def _rows_wait(src, dst, n_rows, sem):
    pltpu.make_async_copy(src.at[pl.ds(0, n_rows), :], dst.at[pl.ds(0, n_rows), :], sem).wait()


def _dispatch_kernel(tm, tm_expert, n_blocks, blk_ranges, zi_ref, dest_ref, *rest):
    n_src = len(blk_ranges)
    h_refs = rest[:n_src]
    xs_ref, xbuf, zblk, lsems, rsems, zsem = rest[n_src:]
    i = pl.program_id(0)
    n = pl.num_programs(0)

    def block_load(g, start):
        slot = lax.rem(g, N_STAGE)
        for h_ref, (b0, b1) in zip(h_refs, blk_ranges):
            @pl.when((g >= b0) & (g < b1))
            def _(h_ref=h_ref, b0=b0):
                cp = pltpu.make_async_copy(h_ref.at[pl.ds((g - b0) * tm, tm), :], xbuf.at[slot],
                                           lsems.at[slot])
                if start:
                    cp.start()
                else:
                    cp.wait()

    def rows_wait(g):
        slot = lax.rem(g, N_STAGE)
        for _k in range(TOP_K):
            _rows_wait(xbuf.at[slot], xs_ref, tm, rsems.at[slot])

    @pl.when(i == 0)
    def _():
        block_load(i, True)

    @pl.when(i >= N_STAGE - 1)
    def _():
        rows_wait(i - (N_STAGE - 1))

    @pl.when(i + 1 < n)
    def _():
        block_load(i + 1, True)

    block_load(i, False)
    cur = lax.rem(i, N_STAGE)

    def issue(j, carry):
        r0 = pl.multiple_of(j * SUBLANES, SUBLANES)
        for u in range(SUBLANES):
            for k in range(TOP_K):
                dst = dest_ref[0, 0, j * SUBLANES + (k * tm + u)]
                _row_copy(xbuf.at[cur], r0 + u, xs_ref, dst, rsems.at[cur]).start(priority=k)
        return carry

    lax.fori_loop(0, tm // SUBLANES, issue, 0)

    @pl.when(i == n - 1)
    def _():
        for back in range(N_STAGE - 2, -1, -1):
            @pl.when(i >= back)
            def _(back=back):
                rows_wait(i - back)

    @pl.when(i == 0)
    def _():
        zblk[...] = jnp.zeros_like(zblk)
        for e in range(N_EXPERTS):
            lo = zi_ref[e] + zi_ref[N_EXPERTS + e]
            hi = zi_ref[e] + zi_ref[2 * N_EXPERTS + e]

            def zissue(r, carry):
                _row_copy(zblk, 0, xs_ref, r, zsem).start()
                return carry

            def zdrain(r, carry):
                _row_copy(zblk, 0, xs_ref, r, zsem).wait()
                return carry

            lax.fori_loop(lo, hi, zissue, 0)
            lax.fori_loop(lo, hi, zdrain, 0)

        def bcopy(b):
            return pltpu.make_async_copy(zblk, xs_ref.at[pl.ds(b * tm_expert, tm_expert), :], zsem)

        def bissue(b, carry):
            bcopy(b).start()
            return carry

        def bdrain(b, carry):
            bcopy(b).wait()
            return carry

        lax.fori_loop(zi_ref[3 * N_EXPERTS], n_blocks, bissue, 0)
        lax.fori_loop(zi_ref[3 * N_EXPERTS], n_blocks, bdrain, 0)


def _dispatch(hs_list, dest_blk, zinfo, n_blocks, tm_expert):
    d = hs_list[0].shape[1]
    nb = dest_blk.shape[0]
    tm = dest_blk.shape[2] // TOP_K
    blk_ranges, b0 = [], 0
    for h in hs_list:
        assert h.shape[0] % tm == 0 and h.shape[0] >= tm
        blk_ranges.append((b0, b0 + h.shape[0] // tm))
        b0 = blk_ranges[-1][1]
    assert b0 == nb
    any_spec = pl.BlockSpec(memory_space=pl.ANY)
    return pl.pallas_call(
        functools.partial(_dispatch_kernel, tm, tm_expert, n_blocks, tuple(blk_ranges)),
        grid_spec=pltpu.PrefetchScalarGridSpec(
            num_scalar_prefetch=1,
            grid=(nb,),
            in_specs=[pl.BlockSpec((1, 1, TOP_K * tm), lambda i, zi: (i, 0, 0), memory_space=pltpu.SMEM)]
            + [any_spec] * len(hs_list),
            out_specs=any_spec,
            scratch_shapes=[pltpu.VMEM((N_STAGE, tm, d), F32), pltpu.VMEM((tm_expert, d), F32),
                            pltpu.SemaphoreType.DMA((N_STAGE,)), pltpu.SemaphoreType.DMA((N_STAGE,)),
                            pltpu.SemaphoreType.DMA(())],
        ),
        out_shape=jax.ShapeDtypeStruct((n_blocks * tm_expert, d), F32),
        compiler_params=_cparams(("arbitrary",)),
        name="moe_dispatch",
    )(zinfo, dest_blk, *hs_list)


def _ffn_moe_kernel(tf, be_ref, nu_ref, x_ref, wg_hbm, wu_hbm, wd_hbm, y_ref,
                    wg_ref, wu_ref, wd_ref, stage_in, stage_out, sems):
    b = pl.program_id(0)
    nb = pl.num_programs(0)
    d, dff = wg_ref.shape
    n_slots = stage_in.shape[0]
    rows_in = stage_in.shape[1]
    rows_out = stage_out.shape[1]
    n_in, n_out = 2 * (d // rows_in), dff // rows_out
    e = be_ref[b]
    fresh = (b == 0) | (e != be_ref[jnp.maximum(b - 1, 0)])
    e_next = be_ref[jnp.minimum(b + 1, nb - 1)]

    def chunk_in(ex, n):
        src, dst = (wg_hbm, wg_ref) if n < n_in // 2 else (wu_hbm, wu_ref)
        rows = pl.ds((n % (n_in // 2)) * rows_in, rows_in)
        stage = stage_in.at[n % n_slots]
        return pltpu.make_async_copy(src.at[ex, rows, :], stage, sems.at[0, n % n_slots]), stage, dst.at[rows, :]

    def chunk_out(ex, n):
        rows = pl.ds(n * rows_out, rows_out)
        stage = stage_out.at[n % n_slots]
        return pltpu.make_async_copy(wd_hbm.at[ex, rows, :], stage, sems.at[1, n % n_slots]), stage, wd_ref.at[rows, :]

    def request_first(ex):
        for n in range(n_slots):
            chunk_in(ex, n)[0].start()
            chunk_out(ex, n)[0].start()

    @pl.when((b < nu_ref[0]) & fresh)
    def _():
        @pl.when(b == 0)
        def _():
            request_first(e)

        for n in range(n_in):
            for chunk, count in ((chunk_in, n_in), (chunk_out, n_out)):
                if n < count:
                    copy, staged, dst = chunk(e, n)
                    copy.wait()
                    dst[...] = staged[...].astype(BF16)
                    if n + n_slots < count:
                        chunk(e, n + n_slots)[0].start()

    @pl.when((b + 1 < nu_ref[0]) & (e_next != e))
    def _():
        request_first(e_next)

    tm = x_ref.shape[0]
    rows_used = nu_ref[1 + b]

    def compute(rows):
        xb = x_ref[0:rows, :].astype(BF16)
        ffn = None
        for c0 in range(0, dff, tf):
            mid = _silu(_dot(xb, wg_ref[:, c0:c0 + tf])) * _dot(xb, wu_ref[:, c0:c0 + tf])
            part = _dot(mid.astype(BF16), wd_ref[c0:c0 + tf, :])
            ffn = part if ffn is None else ffn + part
        y_ref[0:rows, :] = ffn
        if rows < tm:
            y_ref[rows:tm, :] = jnp.zeros((tm - rows, y_ref.shape[1]), F32)

    step = tm // 4
    for rows in range(step, tm + 1, step):
        fits = (rows_used <= rows) if rows < tm else True
        needs = (rows_used > rows - step) if rows > step else True
        @pl.when((b < nu_ref[0]) & fits & needs)
        def _(rows=rows):
            compute(rows)

    @pl.when(b >= nu_ref[0])
    def _():
        y_ref[...] = jnp.zeros_like(y_ref)


def _ffn_moe(xs, wg, wu, wd, block_e, n_used, tm, tf):
    cap, d = xs.shape
    dff = wg.shape[2]
    rows_in, rows_out = 128, 512
    assert cap % tm == 0 and dff % tf == 0 and d % rows_in == 0 and dff % rows_out == 0
    assert min(2 * (d // rows_in), dff // rows_out) >= N_WSTAGE
    any_spec = pl.BlockSpec(memory_space=pl.ANY)
    return pl.pallas_call(
        functools.partial(_ffn_moe_kernel, tf),
        grid_spec=pltpu.PrefetchScalarGridSpec(
            num_scalar_prefetch=2,
            grid=(cap // tm,),
            in_specs=[pl.BlockSpec((tm, d), lambda b, be, nu: (jnp.minimum(b, nu[0] - 1), 0)),
                      any_spec, any_spec, any_spec],
            out_specs=pl.BlockSpec((tm, d), lambda b, be, nu: (b, 0)),
            scratch_shapes=[pltpu.VMEM((d, dff), BF16), pltpu.VMEM((d, dff), BF16), pltpu.VMEM((dff, d), BF16),
                            pltpu.VMEM((N_WSTAGE, rows_in, dff), F32), pltpu.VMEM((N_WSTAGE, rows_out, d), F32),
                            pltpu.SemaphoreType.DMA((2, N_WSTAGE))],
        ),
        out_shape=jax.ShapeDtypeStruct((cap, d), F32),
        compiler_params=_cparams(("arbitrary",)),
        name="ffn_moe",
    )(block_e, n_used, xs, wg, wu, wd)


def _combine_kernel(alpha, tm, dcur_ref, dnext_ref, h_ref, gate_ref, g_ref, b_ref, ys_ref, o_ref, ybuf, sems):
    i = pl.program_id(0)
    n = pl.num_programs(0)

    def issue(dref, slot):
        def body(j, carry):
            r0 = pl.multiple_of(j * SUBLANES, SUBLANES)
            for u in range(SUBLANES):
                for k in range(TOP_K):
                    src = dref[0, 0, j * SUBLANES + (k * tm + u)]
                    _row_copy(ys_ref, src, ybuf.at[slot, k], r0 + u, sems.at[slot]).start(priority=k)
            return carry

        lax.fori_loop(0, tm // SUBLANES, body, 0)

    @pl.when(i == 0)
    def _():
        issue(dcur_ref, 0)

    @pl.when(i + 1 < n)
    def _():
        issue(dnext_ref, (i + 1) % 2)

    slot = i % 2
    for k in range(TOP_K):
        _rows_wait(ys_ref, ybuf.at[slot, k], tm, sems.at[slot])

    gates = gate_ref[...]
    ffn = gates[:, 0:1] * ybuf[slot, 0] + gates[:, 1:2] * ybuf[slot, 1]
    o_ref[...] = _ln(alpha * h_ref[...] + ffn, g_ref[...], b_ref[...])


def _combine(h, dest_blk, gates, ys, g, b, alpha):
    t, d = h.shape
    nb = dest_blk.shape[0]
    tm = dest_blk.shape[2] // TOP_K
    dest_spec = lambda imap: pl.BlockSpec((1, 1, TOP_K * tm), imap, memory_space=pltpu.SMEM)
    return pl.pallas_call(
        functools.partial(_combine_kernel, alpha, tm),
        grid=(nb,),
        in_specs=[dest_spec(lambda i: (i, 0, 0)),
                  dest_spec(lambda i: (jnp.minimum(i + 1, nb - 1), 0, 0)),
                  pl.BlockSpec((tm, d), lambda i: (i, 0)),
                  pl.BlockSpec((tm, SUBLANES), lambda i: (i, 0)),
                  pl.BlockSpec((1, d), lambda i: (0, 0)),
                  pl.BlockSpec((1, d), lambda i: (0, 0)),
                  pl.BlockSpec(memory_space=pl.ANY)],
        out_specs=pl.BlockSpec((tm, d), lambda i: (i, 0)),
        out_shape=jax.ShapeDtypeStruct((t, d), F32),
        scratch_shapes=[pltpu.VMEM((2, TOP_K, tm, d), F32), pltpu.SemaphoreType.DMA((2,))],
        compiler_params=_cparams(("arbitrary",)),
        name="moe_combine",
    )(dest_blk, dest_blk, h, gates, g, b, ys)


def _ffn_routed(hs_list, first_routed, mw, g, b, alpha, tm_expert):
    routed = [first_routed[:2]]
    counts_in = first_routed[2]
    for h in hs_list[1:]:
        meta_i, gates, counts_in = _router(h, mw['w_router_cat'], counts_in)
        routed.append((meta_i, gates))
    counts = counts_in[:N_EXPERTS, 0]
    padded = (counts + tm_expert - 1) // tm_expert * tm_expert
    pad_end = jnp.cumsum(padded)
    pad_start = (pad_end - padded).astype(I32)
    n_assign = sum(h.shape[0] for h in hs_list) * TOP_K
    n_blocks = -(-n_assign // tm_expert) + N_EXPERTS
    cap = n_blocks * tm_expert
    n_used = (pad_end[-1] // tm_expert).astype(I32)
    blk_start = jnp.minimum(jnp.arange(n_blocks, dtype=I32), n_used - 1) * tm_expert
    block_e = jnp.minimum(jnp.sum(blk_start[:, None] >= pad_end[None, :], axis=1), N_EXPERTS - 1).astype(I32)
    zinfo = jnp.concatenate([pad_start, counts, padded, n_used.reshape(1)]).astype(I32)

    dests = []
    for h, (meta_i, _) in zip(hs_list, routed):
        dest = meta_i[TOP_K:2 * TOP_K, :]
        for e in range(N_EXPERTS):
            dest = dest + jnp.where(meta_i[0:TOP_K, :] == e, pad_start[e], 0)
        nb = h.shape[0] // TM_MOVE
        dests.append(dest.reshape(TOP_K, nb, TM_MOVE).transpose(1, 0, 2).reshape(nb, 1, TOP_K * TM_MOVE))
    xs = _dispatch(hs_list, jnp.concatenate(dests, axis=0), zinfo, n_blocks, tm_expert)
    filled_end = (pad_start + counts)[block_e]
    occupancy = jnp.clip(filled_end - jnp.arange(n_blocks, dtype=I32) * tm_expert, 0, tm_expert)
    used_info = jnp.concatenate([n_used.reshape(1), occupancy]).astype(I32)
    ys = _ffn_moe(xs, mw['w_e_gate'], mw['w_e_up'], mw['w_e_down'], block_e, used_info, tm_expert,
                  TF_EXPERT)
    return [_combine(h, dest_blk, gates, ys, g, b, alpha)
            for h, dest_blk, (_, gates) in zip(hs_list, dests, routed)]


def _prep_layer(i, p, n_step):
    dq = p['conv_a_w'].shape[2]
    hd = dq // N_HEADS_B
    tril = jnp.tril(jnp.ones((CHUNK, CHUNK), dtype=bool))
    w_s = p['w_s'][i]
    b_s = p['b_s'][i]
    wp = p['w_pool'][i]
    ng, gc = wp.shape[0], wp.shape[1]
    w_pool_bd = (jnp.eye(ng, dtype=F32)[:, None, :, None] * wp[:, :, None, :]).reshape(ng * gc, ng * gc)
    row = lambda a: a.reshape(1, -1)
    return dict(
        ln_in_g=row(p['ln_in_g']), ln_in_b=row(p['ln_in_b']),
        w_in=p['w_in'][i].astype(BF16),
        conv_a_w=p['conv_a_w'][i], conv_a_b=row(p['conv_a_b'][i]),
        ln_a_g=row(p['ln_a_g'][i]), ln_a_b=row(p['ln_a_b'][i]),
        ln_v_g=row(p['ln_v_g'][i]), ln_v_b=row(p['ln_v_b'][i]),
        ws_tril=jnp.where(tril[None], w_s, 0).astype(BF16),
        bs_mat=jnp.repeat(b_s.T, hd, axis=1),
        ws_vec=jnp.repeat(jnp.where(tril[None], w_s, 0)[:, :n_step, :n_step].transpose(1, 2, 0)
                          .reshape(n_step * n_step, N_HEADS_B), hd, axis=1),
        bs_vec=jnp.repeat(b_s[:, :n_step].T, hd, axis=1),
        conv_c_w=p['conv_c_w'][i],
        w_pool_bd=w_pool_bd.astype(BF16), pool_scale=row(p['pool_scale'][i]),
        w_out=p['w_out'][i].astype(BF16),
        ln_mix_g=row(p['ln_mix_g'][i]), ln_mix_b=row(p['ln_mix_b'][i]),
    )


def kernel(x_prompt, x_sample, state_conv_a, state_conv_c, state_pool_d, ln_in_g, ln_in_b, w_in, conv_a_w,
           conv_a_b, ln_a_g, ln_a_b, ln_v_g, ln_v_b, w_s, b_s, conv_c_w, w_pool, pool_scale, w_out,
           ln_mix_g, ln_mix_b, w_ff_gate, w_ff_up, w_ff_down, w_router, w_e_gate, w_e_up, w_e_down,
           ln_ffn_g, ln_ffn_b):
    p = dict(ln_in_g=ln_in_g, ln_in_b=ln_in_b, w_in=w_in, conv_a_w=conv_a_w, conv_a_b=conv_a_b,
             ln_a_g=ln_a_g, ln_a_b=ln_a_b, ln_v_g=ln_v_g, ln_v_b=ln_v_b, w_s=w_s, b_s=b_s,
             conv_c_w=conv_c_w, w_pool=w_pool, pool_scale=pool_scale, w_out=w_out,
             ln_mix_g=ln_mix_g, ln_mix_b=ln_mix_b)
    depth = w_in.shape[0]
    bsz, seq, d = x_prompt.shape
    n_seq, n_step, _ = x_sample.shape
    dq = conv_a_w.shape[2]
    alpha = float((2.0 * depth) ** 0.25)

    hp = x_prompt
    hs = x_sample.transpose(1, 0, 2).reshape(n_step * n_seq, d)
    outs = {k: [] for k in ('pa', 'pc', 'pd', 'pv', 'sa', 'sc', 'sd', 'sv')}
    for i in range(depth):
        lw = _prep_layer(i, p, n_step)
        j = i // 2
        routed = i % 2 == 1
        if routed:
            wr = jnp.pad(w_router[j], ((0, 0), (0, LANES - N_EXPERTS)))
            wr_hi = wr.astype(BF16)
            wr_lo = (wr - wr_hi.astype(F32)).astype(BF16)
            wr_cat = jnp.concatenate([wr_hi, wr_lo], axis=1)
        hp, pa, pc, pd, pv, *route_p = _mixer_prompt(hp, lw, i == 0, alpha, wr_cat if routed else None,
                                                     defer_norm=not routed)
        tm_state = lambda s: s.transpose(1, 0, 2).reshape(-1, dq)
        hs, sa, sc, sd, sv = _mixer_sample(hs, tm_state(state_conv_a[i]), tm_state(state_conv_c[i]),
                                           tm_state(state_pool_d[i]), lw, i == 0, alpha, n_seq, n_step)
        lg, lb = ln_ffn_g[i].reshape(1, -1), ln_ffn_b[i].reshape(1, -1)
        if not routed:
            hp, hs = _ffn_dense(hp.reshape(bsz * seq, d), hs, lw['ln_mix_g'], lw['ln_mix_b'],
                                w_ff_gate[j], w_ff_up[j], w_ff_down[j], lg, lb, alpha, TM_DENSE, TF_DENSE)
            hp = hp.reshape(bsz, seq, d)
        else:
            mw = dict(w_router_cat=wr_cat, w_e_gate=w_e_gate[j], w_e_up=w_e_up[j], w_e_down=w_e_down[j])
            hp, hs = _ffn_routed([hp.reshape(bsz * seq, d), hs], route_p, mw, lg, lb, alpha, TM_EXPERT)
            hp = hp.reshape(bsz, seq, d)
        back = lambda a: a.reshape(-1, n_seq, dq).transpose(1, 0, 2)
        for k, val in zip(('pa', 'pc', 'pd', 'pv', 'sa', 'sc', 'sd', 'sv'),
                          (pa, pc, pd, pv, back(sa), back(sc), back(sd), back(sv))):
            outs[k].append(val)
    y_sample = hs.reshape(n_step, n_seq, d).transpose(1, 0, 2)
    st = lambda k: jnp.stack(outs[k])
    return (hp, y_sample, st('pa'), st('pc'), st('pd'), st('pv'), st('sa'), st('sc'), st('sd'), st('sv'))
```

```python
import functools

import numpy as np
import jax
import jax.numpy as jnp
from jax import lax
from jax.experimental import pallas as pl
from jax.experimental.pallas import tpu as pltpu

F32 = jnp.float32
BF16 = jnp.bfloat16
I32 = jnp.int32

PAST_LEN = 16384
CHUNK = 128
N_HEADS_B = 4
POOL_WINDOWS = (2, 4, 8, 16)
POOL_PAST = 15
CONV_A_W = 31
CONV_C_W = 3
N_EXPERTS = 8
TOP_K = 2
LN_EPS = 1e-5
SQRT_HALF = float(np.sqrt(0.5).astype(np.float32))

LANES = 128
SUBLANES = 8
VMEM_LIMIT_BYTES = 56 * 1024 * 1024

HIST_A = 32
HIST_C = 8
HIST_D = 24

TL_PROMPT = 512
TM_DENSE = 512
TF_DENSE = 1792
TF_EXPERT = 1792
TM_ROUTE = 512
TM_MOVE = 256
TM_EXPERT = 512
N_STAGE = 3
N_WSTAGE = 5


def _ln(x, g, b):
    mu = jnp.mean(x, axis=-1, keepdims=True)
    xc = x - mu
    var = jnp.mean(xc * xc, axis=-1, keepdims=True)
    return xc * lax.rsqrt(var + LN_EPS) * g + b


def _gelu(x):
    return 0.5 * x * (1.0 + lax.erf(x * SQRT_HALF))


def _silu(x):
    return x * jax.nn.sigmoid(x)


def _dot(a, b):
    return jnp.dot(a, b, preferred_element_type=F32)


def _cparams(sem):
    return pltpu.CompilerParams(dimension_semantics=sem, vmem_limit_bytes=VMEM_LIMIT_BYTES)


def _mixer_prompt_kernel(first_layer, route, alpha, tl, d_mix,
                         x_ref, lng_ref, lnb_ref, win_ref, caw_ref, cab_ref, lag_ref, lab_ref,
                         lvg_ref, lvb_ref, ws_ref, bsm_ref, ccw_ref, wpool_ref, pscale_ref,
                         wout_ref, lmg_ref, lmb_ref, *refs):
    if route:
        wcat_ref, cin_ref = refs[:2]
        refs = refs[2:]
    h_out, sa_out, sc_out, sd_out, v_out = refs[:5]
    refs = refs[5:]
    if route:
        mi_ref, mf_ref, cnt_ref = refs[:3]
        refs = refs[3:]
    ext_a, ext_c, ext_d, cat_ref, z_ref, hb_ref, mix_ref = refs[:7]
    dq = d_mix // 4
    j = pl.program_id(1)
    nj = pl.num_programs(1)
    if route:
        carry_ref, before_ref = refs[7:]

        @pl.when((pl.program_id(0) == 0) & (j == 0))
        def _():
            _route_init(tl, cin_ref, carry_ref, before_ref)

    @pl.when(j == 0)
    def _():
        ext_a[0, 0:HIST_A, :] = jnp.zeros((HIST_A, dq), F32)
        ext_c[0:HIST_C, :] = jnp.zeros((HIST_C, dq), F32)
        ext_d[0:HIST_D, :] = jnp.zeros((HIST_D, dq), F32)

    d = x_ref.shape[2]
    rb = CHUNK

    def resid(r0, n):
        return h_out[0, r0:r0 + n, :] if first_layer else x_ref[0, r0:r0 + n, :]

    for r0 in range(0, tl, rb):
        x = x_ref[0, r0:r0 + rb, :]
        if first_layer:
            x = _ln(x, lng_ref[...], lnb_ref[...])
            h_out[0, r0:r0 + rb, :] = x
        hb_ref[r0:r0 + rb, :] = x.astype(BF16)

    z_ref[...] = _dot(hb_ref[...], win_ref[...])

    def zcol(r0, i, n=rb):
        return z_ref[r0:r0 + n, i * dq:(i + 1) * dq]

    chunk_rows = range(0, tl, CHUNK)

    for r0 in chunk_rows:
        ext_a[0, HIST_A + r0:HIST_A + r0 + rb, :] = zcol(r0, 0) * jax.nn.sigmoid(zcol(r0, 1))
    n_ext = HIST_A + tl
    for r0 in range(0, n_ext - SUBLANES, rb):
        n = min(rb, n_ext - SUBLANES - r0)
        blk = ext_a[0, r0:r0 + n + SUBLANES, :]
        for s in range(1, SUBLANES):
            ext_a[s, r0:r0 + n, :] = pltpu.roll(blk, n + SUBLANES - s, axis=0)[0:n, :]
    off_a = HIST_A - (CONV_A_W - 1)
    for q0 in range(0, tl, 64):
        acc = jnp.zeros((64, dq), F32)
        for k in range(CONV_A_W):
            s = (off_a + k) % SUBLANES
            row = off_a + k - s + q0
            acc = acc + ext_a[s, row:row + 64, :] * caw_ref[k:k + 1, :]
        y = _silu(_ln(acc + cab_ref[...], lag_ref[...], lab_ref[...]))
        cat_ref[q0:q0 + 64, 0:dq] = y.astype(BF16)

    hd = dq // N_HEADS_B
    lane_head = lax.broadcasted_iota(I32, (CHUNK, dq), 1) // hd
    for r0 in chunk_rows:
        v = _ln(_gelu(zcol(r0, 3)), lvg_ref[...], lvb_ref[...])
        sm = bsm_ref[...]
        for hh in range(N_HEADS_B):
            vm = jnp.where(lane_head == hh, v, 0.0).astype(BF16)
            sm = sm + _dot(ws_ref[hh], vm)
        cat_ref[r0:r0 + CHUNK, dq:2 * dq] = (_gelu(zcol(r0, 2)) * sm).astype(BF16)
        if r0 == tl - CHUNK:
            @pl.when(j == nj - 1)
            def _(v=v):
                v_out[0] = v

    off_c = HIST_C - (CONV_C_W - 1)
    for r0 in chunk_rows:
        ext_c[HIST_C + r0:HIST_C + r0 + rb, :] = zcol(r0, 6) * zcol(r0, 4)
    for r0 in chunk_rows:
        conv_c = jnp.zeros((CHUNK, dq), F32)
        for k in range(CONV_C_W):
            conv_c = conv_c + ext_c[off_c + r0 + k:off_c + r0 + k + CHUNK, :] * ccw_ref[k:k + 1, :]
        cat_ref[r0:r0 + CHUNK, 2 * dq:3 * dq] = (zcol(r0, 5) * conv_c).astype(BF16)

    gc = dq // len(POOL_WINDOWS)
    lane_grp = lax.broadcasted_iota(I32, (CHUNK, dq), 1) // gc
    lead = 2 * SUBLANES
    for r0 in chunk_rows:
        ext_d[HIST_D + r0:HIST_D + r0 + rb, :] = zcol(r0, 7)
    for r0 in chunk_rows:
        e = ext_d[HIST_D + r0 - lead:HIST_D + r0 + CHUNK, :]
        s2 = e + pltpu.roll(e, 1, axis=0)
        s4 = s2 + pltpu.roll(s2, 2, axis=0)
        s8 = s4 + pltpu.roll(s4, 4, axis=0)
        s16 = s8 + pltpu.roll(s8, 8, axis=0)
        sums = (s2, s4, s8, s16)
        pos = (j * tl + r0 + lax.broadcasted_iota(I32, (CHUNK, 1), 0)).astype(F32)
        mean = jnp.zeros((CHUNK, dq), F32)
        for g, w in enumerate(POOL_WINDOWS):
            inv = 1.0 / jnp.minimum(pos + 1.0, float(w))
            mean = jnp.where(lane_grp == g, sums[g][lead:lead + CHUNK, :] * inv, mean)
        dd = (mean - e[lead:lead + CHUNK, :]).astype(BF16)
        cat_ref[r0:r0 + CHUNK, 3 * dq:4 * dq] = (_dot(dd, wpool_ref[...]) * pscale_ref[...]).astype(BF16)

    mix_ref[...] = _dot(cat_ref[...], wout_ref[...])
    for r0 in range(0, tl, 64):
        h_out[0, r0:r0 + 64, :] = _ln(alpha * resid(r0, 64) + mix_ref[r0:r0 + 64, :], lmg_ref[...], lmb_ref[...])
    if route:
        _route_block(tl, h_out[0], wcat_ref, mi_ref, mf_ref, cnt_ref, carry_ref, before_ref)

    @pl.when(j == nj - 1)
    def _():
        sa_out[0] = ext_a[0, HIST_A + tl - (CONV_A_W - 1):HIST_A + tl, :]
        sc_out[0] = ext_c[HIST_C + tl - (CONV_C_W - 1):HIST_C + tl, :]
        sd_out[0] = ext_d[HIST_D + tl - POOL_PAST:HIST_D + tl, :]

    ext_a[0, 0:HIST_A, :] = ext_a[0, tl:tl + HIST_A, :]
    ext_c[0:HIST_C, :] = ext_c[tl:tl + HIST_C, :]
    ext_d[0:HIST_D, :] = ext_d[tl:tl + HIST_D, :]


def _full_spec(arr):
    nd = arr.ndim
    return pl.BlockSpec(arr.shape, lambda *_: (0,) * nd)


def _mixer_prompt(x, lw, first_layer, alpha, route_w=None):
    bsz, seq, d = x.shape
    d_mix = lw['w_out'].shape[0]
    dq = d_mix // 4
    tl = TL_PROMPT
    nj = seq // tl
    route = route_w is not None
    assert seq % tl == 0 and tl % CHUNK == 0 and seq >= CHUNK
    params = [lw['ln_in_g'], lw['ln_in_b'], lw['w_in'], lw['conv_a_w'], lw['conv_a_b'], lw['ln_a_g'],
              lw['ln_a_b'], lw['ln_v_g'], lw['ln_v_b'], lw['ws_tril'], lw['bs_mat'], lw['conv_c_w'],
              lw['w_pool_bd'], lw['pool_scale'], lw['w_out'], lw['ln_mix_g'], lw['ln_mix_b']]
    out_shape = [
        jax.ShapeDtypeStruct((bsz, seq, d), F32),
        jax.ShapeDtypeStruct((bsz, CONV_A_W - 1, dq), F32),
        jax.ShapeDtypeStruct((bsz, CONV_C_W - 1, dq), F32),
        jax.ShapeDtypeStruct((bsz, POOL_PAST, dq), F32),
        jax.ShapeDtypeStruct((bsz, CHUNK, dq), F32),
    ]
    state_spec = lambda rows: pl.BlockSpec((1, rows, dq), lambda b, j: (b, 0, 0))
    out_specs = [pl.BlockSpec((1, tl, d), lambda b, j: (b, j, 0)),
                 state_spec(CONV_A_W - 1), state_spec(CONV_C_W - 1), state_spec(POOL_PAST), state_spec(CHUNK)]
    scratch = [pltpu.VMEM((SUBLANES, HIST_A + tl, dq), F32), pltpu.VMEM((HIST_C + tl, dq), F32),
               pltpu.VMEM((HIST_D + tl, dq), F32), pltpu.VMEM((tl, d_mix), BF16),
               pltpu.VMEM((tl, lw['w_in'].shape[1]), F32), pltpu.VMEM((tl, d), BF16),
               pltpu.VMEM((tl, d), F32)]
    if route:
        assert N_EXPERTS == SUBLANES
        params += [route_w, jnp.zeros((SUBLANES, LANES), I32)]
        out_shape += [jax.ShapeDtypeStruct((SUBLANES, bsz * seq), I32),
                      jax.ShapeDtypeStruct((bsz * seq, SUBLANES), F32),
                      jax.ShapeDtypeStruct((SUBLANES, LANES), I32)]
        out_specs += [pl.BlockSpec((SUBLANES, tl), lambda b, j: (0, b * nj + j)),
                      pl.BlockSpec((tl, SUBLANES), lambda b, j: (b * nj + j, 0)),
                      pl.BlockSpec((SUBLANES, LANES), lambda b, j: (0, 0))]
        scratch += [pltpu.VMEM((SUBLANES, LANES), F32), pltpu.VMEM((tl, tl), BF16)]
    return pl.pallas_call(
        functools.partial(_mixer_prompt_kernel, first_layer, route, alpha, tl, d_mix),
        grid=(bsz, nj),
        in_specs=[pl.BlockSpec((1, tl, d), lambda b, j: (b, j, 0))] + [_full_spec(p) for p in params],
        out_specs=tuple(out_specs),
        out_shape=tuple(out_shape),
        scratch_shapes=scratch,
        compiler_params=_cparams(("arbitrary", "arbitrary")),
        name="mixer_prompt",
    )(x, *params)


def _mixer_sample_kernel(first_layer, alpha, n_seq, n_step, d_mix,
                         x_ref, sa_ref, sc_ref, sd_ref, lng_ref, lnb_ref, win_ref, caw_ref, cab_ref,
                         lag_ref, lab_ref, lvg_ref, lvb_ref, wsv_ref, bsv_ref, ccw_ref, wpool_ref,
                         pscale_ref, wout_ref, lmg_ref, lmb_ref, *rest):
    h_out, sa_out, sc_out, sd_out, v_out, cat_ref = rest[-6:]
    dq = d_mix // 4
    x = x_ref[...]
    h = _ln(x, lng_ref[...], lnb_ref[...]) if first_layer else x
    hb = h.astype(BF16)

    def proj(i):
        return _dot(hb, win_ref[:, i * dq:(i + 1) * dq])

    def slab(val, i):
        return val[i * n_seq:(i + 1) * n_seq, :]

    def ext_slabs(state_ref, n_past, cur):
        return ([state_ref[i * n_seq:(i + 1) * n_seq, :] for i in range(n_past)]
                + [slab(cur, i) for i in range(n_step)])

    def store_state(out_ref, slabs, n_keep):
        for i, sl in enumerate(slabs[len(slabs) - n_keep:]):
            out_ref[i * n_seq:(i + 1) * n_seq, :] = sl

    a_glu = proj(0) * jax.nn.sigmoid(proj(1))
    ea = ext_slabs(sa_ref, CONV_A_W - 1, a_glu)
    for l in range(n_step):
        acc = jnp.zeros((n_seq, dq), F32)
        for k in range(CONV_A_W):
            acc = acc + ea[l + k] * caw_ref[k:k + 1, :]
        y = _silu(_ln(acc + cab_ref[...], lag_ref[...], lab_ref[...]))
        cat_ref[l * n_seq:(l + 1) * n_seq, 0:dq] = y.astype(BF16)
    store_state(sa_out, ea, CONV_A_W - 1)

    u = _gelu(proj(2))
    v = _ln(_gelu(proj(3)), lvg_ref[...], lvb_ref[...])
    v_out[...] = v
    for l in range(n_step):
        s = jnp.zeros((n_seq, dq), F32) + bsv_ref[l:l + 1, :]
        for m in range(l + 1):
            s = s + slab(v, m) * wsv_ref[l * n_step + m:l * n_step + m + 1, :]
        cat_ref[l * n_seq:(l + 1) * n_seq, dq:2 * dq] = (slab(u, l) * s).astype(BF16)

    c_x = proj(4)
    c_b = proj(5)
    c_c = proj(6)
    gx = c_c * c_x
    ec = ext_slabs(sc_ref, CONV_C_W - 1, gx)
    for l in range(n_step):
        acc = jnp.zeros((n_seq, dq), F32)
        for k in range(CONV_C_W):
            acc = acc + ec[l + k] * ccw_ref[k:k + 1, :]
        cat_ref[l * n_seq:(l + 1) * n_seq, 2 * dq:3 * dq] = (slab(c_b, l) * acc).astype(BF16)
    store_state(sc_out, ec, CONV_C_W - 1)

    d_in = proj(7)
    ed = ext_slabs(sd_ref, POOL_PAST, d_in)
    memo = {}

    def wsum(i, w):
        if i < 0:
            return None
        if w == 1:
            return ed[i]
        if (i, w) not in memo:
            a, b = wsum(i, w // 2), wsum(i - w // 2, w // 2)
            memo[(i, w)] = a if b is None else a + b
        return memo[(i, w)]

    gc = dq // len(POOL_WINDOWS)
    lane_grp = lax.broadcasted_iota(I32, (n_seq, dq), 1) // gc
    for l in range(n_step):
        mean = jnp.zeros((n_seq, dq), F32)
        for g, w in enumerate(POOL_WINDOWS):
            count = min(PAST_LEN + l + 1, w)
            mean = jnp.where(lane_grp == g, wsum(POOL_PAST + l, w) * (1.0 / count), mean)
        dd = (mean - slab(d_in, l)).astype(BF16)
        cat_ref[l * n_seq:(l + 1) * n_seq, 3 * dq:4 * dq] = (
            _dot(dd, wpool_ref[...]) * pscale_ref[...]).astype(BF16)
    store_state(sd_out, ed, POOL_PAST)

    mix = _dot(cat_ref[...], wout_ref[...])
    h_out[...] = _ln(alpha * h + mix, lmg_ref[...], lmb_ref[...])


def _mixer_sample(x_tm, sa_all, sc, sd_all, prev, layer, lw, first_layer, alpha, n_seq, n_step):
    rows, d = x_tm.shape
    d_mix = lw['w_out'].shape[0]
    dq = d_mix // 4
    depth = sa_all.shape[0]
    assert n_seq % SUBLANES == 0 and n_step <= CHUNK and PAST_LEN % CHUNK == 0
    ins = [x_tm, sa_all, sc, sd_all, lw['ln_in_g'], lw['ln_in_b'], lw['w_in'], lw['conv_a_w'], lw['conv_a_b'],
           lw['ln_a_g'], lw['ln_a_b'], lw['ln_v_g'], lw['ln_v_b'], lw['ws_vec'], lw['bs_vec'],
           lw['conv_c_w'], lw['w_pool_bd'], lw['pool_scale'], lw['w_out'], lw['ln_mix_g'], lw['ln_mix_b']]
    layer_spec = lambda a: pl.BlockSpec((None,) + a.shape[1:], lambda g: (layer, 0, 0))
    in_specs = [layer_spec(a) if a is sa_all or a is sd_all else _full_spec(a) for a in ins]
    out_shape = (
        jax.ShapeDtypeStruct((rows, d), F32),
        jax.ShapeDtypeStruct(sa_all.shape, F32),
        jax.ShapeDtypeStruct(((CONV_C_W - 1) * n_seq, dq), F32),
        jax.ShapeDtypeStruct(sd_all.shape, F32),
        jax.ShapeDtypeStruct((rows, dq), F32),
    )
    out_specs = tuple(layer_spec(s) if len(s.shape) == 3 else pl.BlockSpec(s.shape, lambda g: (0, 0))
                      for s in out_shape)
    aliases = {}
    if prev is not None:
        aliases = {len(ins): 1, len(ins) + 1: 3}
        ins = ins + list(prev)
        in_specs = in_specs + [pl.BlockSpec(memory_space=pl.ANY)] * 2
    return pl.pallas_call(
        functools.partial(_mixer_sample_kernel, first_layer, alpha, n_seq, n_step, d_mix),
        grid=(1,),
        in_specs=in_specs,
        out_specs=out_specs,
        out_shape=out_shape,
        input_output_aliases=aliases,
        scratch_shapes=[pltpu.VMEM((rows, d_mix), BF16)],
        compiler_params=_cparams(("arbitrary",)),
        name="mixer_sample",
    )(*ins)


def _stream_cast(streams):
    plans = []
    for pairs, stage, sems in streams:
        rows = stage.shape[1]
        plans.append(([(src, dst, r0) for src, dst in pairs for r0 in range(0, src.shape[0], rows)],
                      stage, sems))

    def copy(plan, c):
        chunks, stage, sems = plan
        src, _, r0 = chunks[c]
        slot = c % stage.shape[0]
        return pltpu.make_async_copy(src.at[pl.ds(r0, stage.shape[1]), :], stage.at[slot], sems.at[slot])

    for plan in plans:
        for c in range(min(plan[1].shape[0], len(plan[0]))):
            copy(plan, c).start()
    for c in range(max(len(plan[0]) for plan in plans)):
        for plan in plans:
            chunks, stage, _ = plan
            if c < len(chunks):
                _, dst, r0 = chunks[c]
                copy(plan, c).wait()
                dst[pl.ds(r0, stage.shape[1]), :] = stage[c % stage.shape[0]].astype(BF16)
                if c + stage.shape[0] < len(chunks):
                    copy(plan, c + stage.shape[0]).start()


def _ffn_dense_kernel(alpha, tf, n_main, x_ref, xe_ref, wg_hbm, wu_hbm, wd_hbm, g_ref, b_ref, o_ref, oe_ref,
                      wg_ref, wu_ref, wd_ref, stage_in, stage_out, sems):
    i = pl.program_id(0)

    @pl.when(i == 0)
    def _():
        _stream_cast([([(wg_hbm, wg_ref), (wu_hbm, wu_ref)], stage_in, sems.at[0]),
                      ([(wd_hbm, wd_ref)], stage_out, sems.at[1])])

    def run(xr, orf):
        x = xr[...]
        xb = x.astype(BF16)
        ffn = None
        for c0 in range(0, wg_ref.shape[1], tf):
            mid = _silu(_dot(xb, wg_ref[:, c0:c0 + tf])) * _dot(xb, wu_ref[:, c0:c0 + tf])
            part = _dot(mid.astype(BF16), wd_ref[c0:c0 + tf, :])
            ffn = part if ffn is None else ffn + part
        orf[...] = _ln(alpha * x + ffn, g_ref[...], b_ref[...])

    @pl.when(i < n_main)
    def _():
        run(x_ref, o_ref)

    @pl.when(i >= n_main)
    def _():
        run(xe_ref, oe_ref)


def _ffn_dense(x, x_extra, wg, wu, wd, g, b, alpha, tm, tf):
    t, d = x.shape
    te = x_extra.shape[0]
    dff = wg.shape[1]
    n_main, n_extra = t // tm, te // tm
    rows_in, rows_out = 128, 512
    assert t % tm == 0 and te % tm == 0 and dff % tf == 0 and d % rows_in == 0 and dff % rows_out == 0
    any_spec = pl.BlockSpec(memory_space=pl.ANY)
    main_blk = lambda i: (jnp.minimum(i, n_main - 1), 0)
    extra_blk = lambda i: (jnp.maximum(i - n_main, 0), 0)
    return pl.pallas_call(
        functools.partial(_ffn_dense_kernel, alpha, tf, n_main),
        grid=(n_main + n_extra,),
        in_specs=[pl.BlockSpec((tm, d), main_blk), pl.BlockSpec((tm, d), extra_blk),
                  any_spec, any_spec, any_spec,
                  pl.BlockSpec((1, d), lambda i: (0, 0)),
                  pl.BlockSpec((1, d), lambda i: (0, 0))],
        out_specs=(pl.BlockSpec((tm, d), main_blk), pl.BlockSpec((tm, d), extra_blk)),
        out_shape=(jax.ShapeDtypeStruct((t, d), F32), jax.ShapeDtypeStruct((te, d), F32)),
        scratch_shapes=[pltpu.VMEM(wg.shape, BF16), pltpu.VMEM(wu.shape, BF16), pltpu.VMEM(wd.shape, BF16),
                        pltpu.VMEM((3, rows_in, dff), F32), pltpu.VMEM((3, rows_out, d), F32),
                        pltpu.SemaphoreType.DMA((2, 3))],
        compiler_params=_cparams(("arbitrary",)),
        name="ffn_dense",
    )(x, x_extra, wg, wu, wd, g, b)


def _route_init(tm, cin_ref, carry_ref, before_ref):
    carry_ref[...] = cin_ref[...].astype(F32)
    row = lax.broadcasted_iota(I32, (tm, tm), 0)
    col = lax.broadcasted_iota(I32, (tm, tm), 1)
    before_ref[...] = jnp.where(row < col, 1.0, 0.0).astype(BF16)


def _route_block(tm, h, wcat_ref, mi_ref, mf_ref, cnt_ref, carry_ref, before_ref):
    h_hi = h.astype(BF16)
    h_lo = (h - h_hi.astype(F32)).astype(BF16)
    p_hi = _dot(h_hi, wcat_ref[...])
    p_lo = _dot(h_lo, wcat_ref[...])
    logits = (p_hi[:, 0:LANES] + (p_lo[:, 0:LANES] + p_hi[:, LANES:2 * LANES])) + p_lo[:, LANES:2 * LANES]
    lg = logits.T[0:N_EXPERTS, :]
    ex = lax.broadcasted_iota(I32, (N_EXPERTS, tm), 0).astype(F32)
    m1 = jnp.max(lg, axis=0, keepdims=True)
    i1 = jnp.min(jnp.where(lg == m1, ex, float(N_EXPERTS)), axis=0, keepdims=True)
    sel1 = ex == i1
    rest = jnp.where(sel1, -jnp.inf, lg)
    m2 = jnp.max(rest, axis=0, keepdims=True)
    i2 = jnp.min(jnp.where(rest == m2, ex, float(N_EXPERTS)), axis=0, keepdims=True)
    sel2 = ex == i2
    e2 = jnp.exp(m2 - m1)
    den = 1.0 + e2
    g1 = 1.0 / den
    g2 = e2 / den

    sel = jnp.where(sel1 | sel2, 1.0, 0.0)
    sel_pad = jnp.concatenate([sel, jnp.zeros_like(sel)], axis=0).astype(BF16)
    base = _dot(sel_pad, before_ref[...])[0:N_EXPERTS, :] + carry_ref[:, 0:1]
    r1 = jnp.sum(jnp.where(sel1, base, 0.0), axis=0, keepdims=True)
    r2 = jnp.sum(jnp.where(sel2, base, 0.0), axis=0, keepdims=True)
    carry_ref[...] = carry_ref[...] + jnp.sum(sel, axis=1, keepdims=True)

    meta = jnp.where(ex == 0, i1, jnp.where(ex == 1, i2, jnp.where(ex == 2, r1, r2)))
    mi_ref[...] = meta.astype(I32)
    gt = jnp.where(ex == 0, g1, jnp.where(ex == 1, g2, 0.0))
    gt = jnp.concatenate([gt, jnp.zeros((LANES - N_EXPERTS, tm), F32)], axis=0)
    mf_ref[...] = gt.T[:, 0:SUBLANES]
    cnt_ref[...] = carry_ref[...].astype(I32)


def _router_kernel(tm, h_ref, wcat_ref, cin_ref, mi_ref, mf_ref, cnt_ref, carry_ref, before_ref):
    @pl.when(pl.program_id(0) == 0)
    def _():
        _route_init(tm, cin_ref, carry_ref, before_ref)

    _route_block(tm, h_ref[...], wcat_ref, mi_ref, mf_ref, cnt_ref, carry_ref, before_ref)


def _router(h, wr_cat, counts_in):
    t, d = h.shape
    tm = min(TM_ROUTE, t)
    assert t % tm == 0 and N_EXPERTS == SUBLANES
    return pl.pallas_call(
        functools.partial(_router_kernel, tm),
        grid=(t // tm,),
        in_specs=[pl.BlockSpec((tm, d), lambda i: (i, 0)), _full_spec(wr_cat), _full_spec(counts_in)],
        out_specs=(pl.BlockSpec((SUBLANES, tm), lambda i: (0, i)),
                   pl.BlockSpec((tm, SUBLANES), lambda i: (i, 0)),
                   pl.BlockSpec((SUBLANES, LANES), lambda i: (0, 0))),
        out_shape=(jax.ShapeDtypeStruct((SUBLANES, t), I32),
                   jax.ShapeDtypeStruct((t, SUBLANES), F32),
                   jax.ShapeDtypeStruct((SUBLANES, LANES), I32)),
        scratch_shapes=[pltpu.VMEM((SUBLANES, LANES), F32), pltpu.VMEM((tm, tm), BF16)],
        compiler_params=_cparams(("arbitrary",)),
        name="router",
    )(h, wr_cat, counts_in)


def _row_copy(src, src_row, dst, dst_row, sem):
    return pltpu.make_async_copy(src.at[pl.ds(src_row, 1), :], dst.at[pl.ds(dst_row, 1), :], sem)


def _rows_wait(src, dst, n_rows, sem):
    pltpu.make_async_copy(src.at[pl.ds(0, n_rows), :], dst.at[pl.ds(0, n_rows), :], sem).wait()


def _dispatch_kernel(tm, tm_expert, n_blocks, blk_ranges, zi_ref, dest_ref, *rest):
    n_src = len(blk_ranges)
    h_refs = rest[:n_src]
    xs_ref, xbuf, zblk, lsems, rsems, zsem = rest[n_src:]
    i = pl.program_id(0)
    n = pl.num_programs(0)

    def block_load(g, start):
        slot = lax.rem(g, N_STAGE)
        for h_ref, (b0, b1) in zip(h_refs, blk_ranges):
            @pl.when((g >= b0) & (g < b1))
            def _(h_ref=h_ref, b0=b0):
                cp = pltpu.make_async_copy(h_ref.at[pl.ds((g - b0) * tm, tm), :], xbuf.at[slot],
                                           lsems.at[slot])
                if start:
                    cp.start()
                else:
                    cp.wait()

    def rows_wait(g):
        slot = lax.rem(g, N_STAGE)
        for _k in range(TOP_K):
            _rows_wait(xbuf.at[slot], xs_ref, tm, rsems.at[slot])

    @pl.when(i == 0)
    def _():
        block_load(i, True)

    @pl.when(i >= N_STAGE - 1)
    def _():
        rows_wait(i - (N_STAGE - 1))

    @pl.when(i + 1 < n)
    def _():
        block_load(i + 1, True)

    block_load(i, False)
    cur = lax.rem(i, N_STAGE)

    def issue(j, carry):
        r0 = pl.multiple_of(j * SUBLANES, SUBLANES)
        for u in range(SUBLANES):
            for k in range(TOP_K):
                dst = dest_ref[0, 0, j * SUBLANES + (k * tm + u)]
                _row_copy(xbuf.at[cur], r0 + u, xs_ref, dst, rsems.at[cur]).start(priority=k)
        return carry

    lax.fori_loop(0, tm // SUBLANES, issue, 0)

    @pl.when(i == n - 1)
    def _():
        for back in range(N_STAGE - 2, -1, -1):
            @pl.when(i >= back)
            def _(back=back):
                rows_wait(i - back)

    @pl.when(i == 0)
    def _():
        zblk[...] = jnp.zeros_like(zblk)
        for e in range(N_EXPERTS):
            lo = zi_ref[e] + zi_ref[N_EXPERTS + e]
            hi = zi_ref[e] + zi_ref[2 * N_EXPERTS + e]

            def zissue(r, carry):
                _row_copy(zblk, 0, xs_ref, r, zsem).start()
                return carry

            def zdrain(r, carry):
                _row_copy(zblk, 0, xs_ref, r, zsem).wait()
                return carry

            lax.fori_loop(lo, hi, zissue, 0)
            lax.fori_loop(lo, hi, zdrain, 0)

        def bcopy(b):
            return pltpu.make_async_copy(zblk, xs_ref.at[pl.ds(b * tm_expert, tm_expert), :], zsem)

        def bissue(b, carry):
            bcopy(b).start()
            return carry

        def bdrain(b, carry):
            bcopy(b).wait()
            return carry

        lax.fori_loop(zi_ref[3 * N_EXPERTS], n_blocks, bissue, 0)
        lax.fori_loop(zi_ref[3 * N_EXPERTS], n_blocks, bdrain, 0)


def _dispatch(hs_list, dest_blk, zinfo, n_blocks, tm_expert):
    d = hs_list[0].shape[1]
    nb = dest_blk.shape[0]
    tm = dest_blk.shape[2] // TOP_K
    blk_ranges, b0 = [], 0
    for h in hs_list:
        assert h.shape[0] % tm == 0 and h.shape[0] >= tm
        blk_ranges.append((b0, b0 + h.shape[0] // tm))
        b0 = blk_ranges[-1][1]
    assert b0 == nb
    any_spec = pl.BlockSpec(memory_space=pl.ANY)
    return pl.pallas_call(
        functools.partial(_dispatch_kernel, tm, tm_expert, n_blocks, tuple(blk_ranges)),
        grid_spec=pltpu.PrefetchScalarGridSpec(
            num_scalar_prefetch=1,
            grid=(nb,),
            in_specs=[pl.BlockSpec((1, 1, TOP_K * tm), lambda i, zi: (i, 0, 0), memory_space=pltpu.SMEM)]
            + [any_spec] * len(hs_list),
            out_specs=any_spec,
            scratch_shapes=[pltpu.VMEM((N_STAGE, tm, d), F32), pltpu.VMEM((tm_expert, d), F32),
                            pltpu.SemaphoreType.DMA((N_STAGE,)), pltpu.SemaphoreType.DMA((N_STAGE,)),
                            pltpu.SemaphoreType.DMA(())],
        ),
        out_shape=jax.ShapeDtypeStruct((n_blocks * tm_expert, d), F32),
        compiler_params=_cparams(("arbitrary",)),
        name="moe_dispatch",
    )(zinfo, dest_blk, *hs_list)


def _ffn_moe_kernel(tf, be_ref, nu_ref, x_ref, wg_hbm, wu_hbm, wd_hbm, y_ref,
                    wg_ref, wu_ref, wd_ref, stage_in, stage_out, sems):
    b = pl.program_id(0)
    nb = pl.num_programs(0)
    d, dff = wg_ref.shape
    n_slots = stage_in.shape[0]
    rows_in = stage_in.shape[1]
    rows_out = stage_out.shape[1]
    n_in, n_out = 2 * (d // rows_in), dff // rows_out
    e = be_ref[b]
    fresh = (b == 0) | (e != be_ref[jnp.maximum(b - 1, 0)])
    e_next = be_ref[jnp.minimum(b + 1, nb - 1)]

    def chunk_in(ex, n):
        src, dst = (wg_hbm, wg_ref) if n < n_in // 2 else (wu_hbm, wu_ref)
        rows = pl.ds((n % (n_in // 2)) * rows_in, rows_in)
        stage = stage_in.at[n % n_slots]
        return pltpu.make_async_copy(src.at[ex, rows, :], stage, sems.at[0, n % n_slots]), stage, dst.at[rows, :]

    def chunk_out(ex, n):
        rows = pl.ds(n * rows_out, rows_out)
        stage = stage_out.at[n % n_slots]
        return pltpu.make_async_copy(wd_hbm.at[ex, rows, :], stage, sems.at[1, n % n_slots]), stage, wd_ref.at[rows, :]

    def request_first(ex):
        for n in range(n_slots):
            chunk_in(ex, n)[0].start()
            chunk_out(ex, n)[0].start()

    @pl.when((b < nu_ref[0]) & fresh)
    def _():
        @pl.when(b == 0)
        def _():
            request_first(e)

        for n in range(n_in):
            for chunk, count in ((chunk_in, n_in), (chunk_out, n_out)):
                if n < count:
                    copy, staged, dst = chunk(e, n)
                    copy.wait()
                    dst[...] = staged[...].astype(BF16)
                    if n + n_slots < count:
                        chunk(e, n + n_slots)[0].start()

    @pl.when((b + 1 < nu_ref[0]) & (e_next != e))
    def _():
        request_first(e_next)

    tm = x_ref.shape[0]
    rows_used = nu_ref[1 + b]

    def compute(rows):
        xb = x_ref[0:rows, :].astype(BF16)
        ffn = None
        for c0 in range(0, dff, tf):
            mid = _silu(_dot(xb, wg_ref[:, c0:c0 + tf])) * _dot(xb, wu_ref[:, c0:c0 + tf])
            part = _dot(mid.astype(BF16), wd_ref[c0:c0 + tf, :])
            ffn = part if ffn is None else ffn + part
        y_ref[0:rows, :] = ffn
        if rows < tm:
            y_ref[rows:tm, :] = jnp.zeros((tm - rows, y_ref.shape[1]), F32)

    step = tm // 4
    for rows in range(step, tm + 1, step):
        fits = (rows_used <= rows) if rows < tm else True
        needs = (rows_used > rows - step) if rows > step else True
        @pl.when((b < nu_ref[0]) & fits & needs)
        def _(rows=rows):
            compute(rows)

    @pl.when(b >= nu_ref[0])
    def _():
        y_ref[...] = jnp.zeros_like(y_ref)


def _ffn_moe(xs, wg, wu, wd, block_e, n_used, tm, tf):
    cap, d = xs.shape
    dff = wg.shape[2]
    rows_in, rows_out = 128, 512
    assert cap % tm == 0 and dff % tf == 0 and d % rows_in == 0 and dff % rows_out == 0
    assert min(2 * (d // rows_in), dff // rows_out) >= N_WSTAGE
    any_spec = pl.BlockSpec(memory_space=pl.ANY)
    return pl.pallas_call(
        functools.partial(_ffn_moe_kernel, tf),
        grid_spec=pltpu.PrefetchScalarGridSpec(
            num_scalar_prefetch=2,
            grid=(cap // tm,),
            in_specs=[pl.BlockSpec((tm, d), lambda b, be, nu: (jnp.minimum(b, nu[0] - 1), 0)),
                      any_spec, any_spec, any_spec],
            out_specs=pl.BlockSpec((tm, d), lambda b, be, nu: (b, 0)),
            scratch_shapes=[pltpu.VMEM((d, dff), BF16), pltpu.VMEM((d, dff), BF16), pltpu.VMEM((dff, d), BF16),
                            pltpu.VMEM((N_WSTAGE, rows_in, dff), F32), pltpu.VMEM((N_WSTAGE, rows_out, d), F32),
                            pltpu.SemaphoreType.DMA((2, N_WSTAGE))],
        ),
        out_shape=jax.ShapeDtypeStruct((cap, d), F32),
        compiler_params=_cparams(("arbitrary",)),
        name="ffn_moe",
    )(block_e, n_used, xs, wg, wu, wd)


def _combine_kernel(alpha, tm, dcur_ref, dnext_ref, h_ref, gate_ref, g_ref, b_ref, ys_ref, o_ref, ybuf, sems):
    i = pl.program_id(0)
    n = pl.num_programs(0)

    def issue(dref, slot):
        def body(j, carry):
            r0 = pl.multiple_of(j * SUBLANES, SUBLANES)
            for u in range(SUBLANES):
                for k in range(TOP_K):
                    src = dref[0, 0, j * SUBLANES + (k * tm + u)]
                    _row_copy(ys_ref, src, ybuf.at[slot, k], r0 + u, sems.at[slot]).start(priority=k)
            return carry

        lax.fori_loop(0, tm // SUBLANES, body, 0)

    @pl.when(i == 0)
    def _():
        issue(dcur_ref, 0)

    @pl.when(i + 1 < n)
    def _():
        issue(dnext_ref, (i + 1) % 2)

    slot = i % 2
    for k in range(TOP_K):
        _rows_wait(ys_ref, ybuf.at[slot, k], tm, sems.at[slot])

    gates = gate_ref[...]
    ffn = gates[:, 0:1] * ybuf[slot, 0] + gates[:, 1:2] * ybuf[slot, 1]
    o_ref[...] = _ln(alpha * h_ref[...] + ffn, g_ref[...], b_ref[...])


def _combine(h, dest_blk, gates, ys, g, b, alpha):
    t, d = h.shape
    nb = dest_blk.shape[0]
    tm = dest_blk.shape[2] // TOP_K
    dest_spec = lambda imap: pl.BlockSpec((1, 1, TOP_K * tm), imap, memory_space=pltpu.SMEM)
    return pl.pallas_call(
        functools.partial(_combine_kernel, alpha, tm),
        grid=(nb,),
        in_specs=[dest_spec(lambda i: (i, 0, 0)),
                  dest_spec(lambda i: (jnp.minimum(i + 1, nb - 1), 0, 0)),
                  pl.BlockSpec((tm, d), lambda i: (i, 0)),
                  pl.BlockSpec((tm, SUBLANES), lambda i: (i, 0)),
                  pl.BlockSpec((1, d), lambda i: (0, 0)),
                  pl.BlockSpec((1, d), lambda i: (0, 0)),
                  pl.BlockSpec(memory_space=pl.ANY)],
        out_specs=pl.BlockSpec((tm, d), lambda i: (i, 0)),
        out_shape=jax.ShapeDtypeStruct((t, d), F32),
        scratch_shapes=[pltpu.VMEM((2, TOP_K, tm, d), F32), pltpu.SemaphoreType.DMA((2,))],
        compiler_params=_cparams(("arbitrary",)),
        name="moe_combine",
    )(dest_blk, dest_blk, h, gates, g, b, ys)


def _ffn_routed(hs_list, first_routed, mw, g, b, alpha, tm_expert):
    routed = [first_routed[:2]]
    counts_in = first_routed[2]
    for h in hs_list[1:]:
        meta_i, gates, counts_in = _router(h, mw['w_router_cat'], counts_in)
        routed.append((meta_i, gates))
    counts = counts_in[:N_EXPERTS, 0]
    padded = (counts + tm_expert - 1) // tm_expert * tm_expert
    pad_end = jnp.cumsum(padded)
    pad_start = (pad_end - padded).astype(I32)
    n_assign = sum(h.shape[0] for h in hs_list) * TOP_K
    n_blocks = -(-n_assign // tm_expert) + N_EXPERTS
    cap = n_blocks * tm_expert
    n_used = (pad_end[-1] // tm_expert).astype(I32)
    blk_start = jnp.minimum(jnp.arange(n_blocks, dtype=I32), n_used - 1) * tm_expert
    block_e = jnp.minimum(jnp.sum(blk_start[:, None] >= pad_end[None, :], axis=1), N_EXPERTS - 1).astype(I32)
    zinfo = jnp.concatenate([pad_start, counts, padded, n_used.reshape(1)]).astype(I32)

    dests = []
    for h, (meta_i, _) in zip(hs_list, routed):
        dest = meta_i[TOP_K:2 * TOP_K, :]
        for e in range(N_EXPERTS):
            dest = dest + jnp.where(meta_i[0:TOP_K, :] == e, pad_start[e], 0)
        nb = h.shape[0] // TM_MOVE
        dests.append(dest.reshape(TOP_K, nb, TM_MOVE).transpose(1, 0, 2).reshape(nb, 1, TOP_K * TM_MOVE))
    xs = _dispatch(hs_list, jnp.concatenate(dests, axis=0), zinfo, n_blocks, tm_expert)
    filled_end = (pad_start + counts)[block_e]
    occupancy = jnp.clip(filled_end - jnp.arange(n_blocks, dtype=I32) * tm_expert, 0, tm_expert)
    used_info = jnp.concatenate([n_used.reshape(1), occupancy]).astype(I32)
    ys = _ffn_moe(xs, mw['w_e_gate'], mw['w_e_up'], mw['w_e_down'], block_e, used_info, tm_expert,
                  TF_EXPERT)
    return [_combine(h, dest_blk, gates, ys, g, b, alpha)
            for h, dest_blk, (_, gates) in zip(hs_list, dests, routed)]


def _prep_layer(i, p, n_step):
    dq = p['conv_a_w'].shape[2]
    hd = dq // N_HEADS_B
    tril = jnp.tril(jnp.ones((CHUNK, CHUNK), dtype=bool))
    w_s = p['w_s'][i]
    b_s = p['b_s'][i]
    wp = p['w_pool'][i]
    ng, gc = wp.shape[0], wp.shape[1]
    w_pool_bd = (jnp.eye(ng, dtype=F32)[:, None, :, None] * wp[:, :, None, :]).reshape(ng * gc, ng * gc)
    row = lambda a: a.reshape(1, -1)
    return dict(
        ln_in_g=row(p['ln_in_g']), ln_in_b=row(p['ln_in_b']),
        w_in=p['w_in'][i].astype(BF16),
        conv_a_w=p['conv_a_w'][i], conv_a_b=row(p['conv_a_b'][i]),
        ln_a_g=row(p['ln_a_g'][i]), ln_a_b=row(p['ln_a_b'][i]),
        ln_v_g=row(p['ln_v_g'][i]), ln_v_b=row(p['ln_v_b'][i]),
        ws_tril=jnp.where(tril[None], w_s, 0).astype(BF16),
        bs_mat=jnp.repeat(b_s.T, hd, axis=1),
        ws_vec=jnp.repeat(jnp.where(tril[None], w_s, 0)[:, :n_step, :n_step].transpose(1, 2, 0)
                          .reshape(n_step * n_step, N_HEADS_B), hd, axis=1),
        bs_vec=jnp.repeat(b_s[:, :n_step].T, hd, axis=1),
        conv_c_w=p['conv_c_w'][i],
        w_pool_bd=w_pool_bd.astype(BF16), pool_scale=row(p['pool_scale'][i]),
        w_out=p['w_out'][i].astype(BF16),
        ln_mix_g=row(p['ln_mix_g'][i]), ln_mix_b=row(p['ln_mix_b'][i]),
    )


def kernel(x_prompt, x_sample, state_conv_a, state_conv_c, state_pool_d, ln_in_g, ln_in_b, w_in, conv_a_w,
           conv_a_b, ln_a_g, ln_a_b, ln_v_g, ln_v_b, w_s, b_s, conv_c_w, w_pool, pool_scale, w_out,
           ln_mix_g, ln_mix_b, w_ff_gate, w_ff_up, w_ff_down, w_router, w_e_gate, w_e_up, w_e_down,
           ln_ffn_g, ln_ffn_b):
    p = dict(ln_in_g=ln_in_g, ln_in_b=ln_in_b, w_in=w_in, conv_a_w=conv_a_w, conv_a_b=conv_a_b,
             ln_a_g=ln_a_g, ln_a_b=ln_a_b, ln_v_g=ln_v_g, ln_v_b=ln_v_b, w_s=w_s, b_s=b_s,
             conv_c_w=conv_c_w, w_pool=w_pool, pool_scale=pool_scale, w_out=w_out,
             ln_mix_g=ln_mix_g, ln_mix_b=ln_mix_b)
    depth = w_in.shape[0]
    bsz, seq, d = x_prompt.shape
    n_seq, n_step, _ = x_sample.shape
    dq = conv_a_w.shape[2]
    alpha = float((2.0 * depth) ** 0.25)

    hp = x_prompt
    hs = x_sample.transpose(1, 0, 2).reshape(n_step * n_seq, d)
    outs = {k: [] for k in ('pa', 'pc', 'pd', 'pv', 'sc', 'sv')}
    tm_all = lambda s: s.transpose(0, 2, 1, 3).reshape(depth, -1, dq)
    sa_all, sd_all = tm_all(state_conv_a), tm_all(state_pool_d)
    new_sa_sd = None
    for i in range(depth):
        lw = _prep_layer(i, p, n_step)
        j = i // 2
        routed = i % 2 == 1
        if routed:
            wr = jnp.pad(w_router[j], ((0, 0), (0, LANES - N_EXPERTS)))
            wr_hi = wr.astype(BF16)
            wr_lo = (wr - wr_hi.astype(F32)).astype(BF16)
            wr_cat = jnp.concatenate([wr_hi, wr_lo], axis=1)
        hp, pa, pc, pd, pv, *route_p = _mixer_prompt(hp, lw, i == 0, alpha, wr_cat if routed else None)
        tm_state = lambda s: s.transpose(1, 0, 2).reshape(-1, dq)
        hs, sa, sc, sd, sv = _mixer_sample(hs, sa_all, tm_state(state_conv_c[i]), sd_all, new_sa_sd, i, lw,
                                           i == 0, alpha, n_seq, n_step)
        new_sa_sd = (sa, sd)
        lg, lb = ln_ffn_g[i].reshape(1, -1), ln_ffn_b[i].reshape(1, -1)
        if not routed:
            hp, hs = _ffn_dense(hp.reshape(bsz * seq, d), hs, w_ff_gate[j], w_ff_up[j], w_ff_down[j], lg, lb,
                                alpha, TM_DENSE, TF_DENSE)
            hp = hp.reshape(bsz, seq, d)
        else:
            mw = dict(w_router_cat=wr_cat, w_e_gate=w_e_gate[j], w_e_up=w_e_up[j], w_e_down=w_e_down[j])
            hp, hs = _ffn_routed([hp.reshape(bsz * seq, d), hs], route_p, mw, lg, lb, alpha, TM_EXPERT)
            hp = hp.reshape(bsz, seq, d)
        back = lambda a: a.reshape(-1, n_seq, dq).transpose(1, 0, 2)
        for k, val in zip(('pa', 'pc', 'pd', 'pv', 'sc', 'sv'), (pa, pc, pd, pv, back(sc), back(sv))):
            outs[k].append(val)
    y_sample = hs.reshape(n_step, n_seq, d).transpose(1, 0, 2)
    st = lambda k: jnp.stack(outs[k])
    back_all = lambda a: a.reshape(depth, -1, n_seq, dq).transpose(0, 2, 1, 3)
    return (hp, y_sample, st('pa'), st('pc'), st('pd'), st('pv'), back_all(new_sa_sd[0]), st('sc'),
            back_all(new_sa_sd[1]), st('sv'))
```

```python
import functools

import numpy as np
import jax
import jax.numpy as jnp
from jax import lax
from jax.experimental import pallas as pl
from jax.experimental.pallas import tpu as pltpu

F32 = jnp.float32
BF16 = jnp.bfloat16
I32 = jnp.int32

PAST_LEN = 16384
CHUNK = 128
N_HEADS_B = 4
POOL_WINDOWS = (2, 4, 8, 16)
POOL_PAST = 15
CONV_A_W = 31
CONV_C_W = 3
N_EXPERTS = 8
TOP_K = 2
LN_EPS = 1e-5
SQRT_HALF = float(np.sqrt(0.5).astype(np.float32))

LANES = 128
SUBLANES = 8
VMEM_LIMIT_BYTES = 56 * 1024 * 1024

HIST_A = 32
HIST_C = 8
HIST_D = 24

TL_PROMPT = 512
TM_DENSE = 512
TF_DENSE = 1792
TF_EXPERT = 1792
TM_ROUTE = 512
TM_MOVE = 256
TM_EXPERT = 512
N_STAGE = 3
N_WSTAGE = 5


def _ln(x, g, b):
    mu = jnp.mean(x, axis=-1, keepdims=True)
    xc = x - mu
    var = jnp.mean(xc * xc, axis=-1, keepdims=True)
    return xc * lax.rsqrt(var + LN_EPS) * g + b


def _gelu(x):
    return 0.5 * x * (1.0 + lax.erf(x * SQRT_HALF))


def _silu(x):
    return x * jax.nn.sigmoid(x)


def _dot(a, b):
    return jnp.dot(a, b, preferred_element_type=F32)


def _cparams(sem):
    return pltpu.CompilerParams(dimension_semantics=sem, vmem_limit_bytes=VMEM_LIMIT_BYTES)


def _mixer_prompt_kernel(first_layer, route, alpha, tl, d_mix,
                         x_ref, lng_ref, lnb_ref, win_ref, caw_ref, cab_ref, lag_ref, lab_ref,
                         lvg_ref, lvb_ref, ws_ref, bsm_ref, ccw_ref, wpool_ref, pscale_ref,
                         wout_ref, lmg_ref, lmb_ref, *refs):
    if route:
        wcat_ref, cin_ref = refs[:2]
        refs = refs[2:]
    h_out, sa_out, sc_out, sd_out, v_out = refs[:5]
    refs = refs[5:]
    if route:
        mi_ref, mf_ref, cnt_ref = refs[:3]
        refs = refs[3:]
    ext_a, ext_c, ext_d, cat_ref, z_ref, hb_ref, mix_ref = refs[:7]
    dq = d_mix // 4
    j = pl.program_id(1)
    nj = pl.num_programs(1)
    if route:
        carry_ref, before_ref = refs[7:]

        @pl.when((pl.program_id(0) == 0) & (j == 0))
        def _():
            _route_init(tl, cin_ref, carry_ref, before_ref)

    @pl.when(j == 0)
    def _():
        ext_a[0, 0:HIST_A, :] = jnp.zeros((HIST_A, dq), F32)
        ext_c[0:HIST_C, :] = jnp.zeros((HIST_C, dq), F32)
        ext_d[0:HIST_D, :] = jnp.zeros((HIST_D, dq), F32)

    d = x_ref.shape[2]
    rb = CHUNK

    def resid(r0, n):
        return h_out[0, r0:r0 + n, :] if first_layer else x_ref[0, r0:r0 + n, :]

    for r0 in range(0, tl, rb):
        x = x_ref[0, r0:r0 + rb, :]
        if first_layer:
            x = _ln(x, lng_ref[...], lnb_ref[...])
            h_out[0, r0:r0 + rb, :] = x
        hb_ref[r0:r0 + rb, :] = x.astype(BF16)

    z_ref[...] = _dot(hb_ref[...], win_ref[...])

    def zcol(r0, i, n=rb):
        return z_ref[r0:r0 + n, i * dq:(i + 1) * dq]

    chunk_rows = range(0, tl, CHUNK)

    for r0 in chunk_rows:
        ext_a[0, HIST_A + r0:HIST_A + r0 + rb, :] = zcol(r0, 0) * jax.nn.sigmoid(zcol(r0, 1))
    n_ext = HIST_A + tl
    for r0 in range(0, n_ext - SUBLANES, rb):
        n = min(rb, n_ext - SUBLANES - r0)
        blk = ext_a[0, r0:r0 + n + SUBLANES, :]
        for s in range(1, SUBLANES):
            ext_a[s, r0:r0 + n, :] = pltpu.roll(blk, n + SUBLANES - s, axis=0)[0:n, :]
    off_a = HIST_A - (CONV_A_W - 1)
    for q0 in range(0, tl, 64):
        acc = jnp.zeros((64, dq), F32)
        for k in range(CONV_A_W):
            s = (off_a + k) % SUBLANES
            row = off_a + k - s + q0
            acc = acc + ext_a[s, row:row + 64, :] * caw_ref[k:k + 1, :]
        y = _silu(_ln(acc + cab_ref[...], lag_ref[...], lab_ref[...]))
        cat_ref[q0:q0 + 64, 0:dq] = y.astype(BF16)

    hd = dq // N_HEADS_B
    lane_head = lax.broadcasted_iota(I32, (CHUNK, dq), 1) // hd
    for r0 in chunk_rows:
        v = _ln(_gelu(zcol(r0, 3)), lvg_ref[...], lvb_ref[...])
        sm = bsm_ref[...]
        for hh in range(N_HEADS_B):
            vm = jnp.where(lane_head == hh, v, 0.0).astype(BF16)
            sm = sm + _dot(ws_ref[hh], vm)
        cat_ref[r0:r0 + CHUNK, dq:2 * dq] = (_gelu(zcol(r0, 2)) * sm).astype(BF16)
        if r0 == tl - CHUNK:
            @pl.when(j == nj - 1)
            def _(v=v):
                v_out[0] = v

    off_c = HIST_C - (CONV_C_W - 1)
    for r0 in chunk_rows:
        ext_c[HIST_C + r0:HIST_C + r0 + rb, :] = zcol(r0, 6) * zcol(r0, 4)
    for r0 in chunk_rows:
        conv_c = jnp.zeros((CHUNK, dq), F32)
        for k in range(CONV_C_W):
            conv_c = conv_c + ext_c[off_c + r0 + k:off_c + r0 + k + CHUNK, :] * ccw_ref[k:k + 1, :]
        cat_ref[r0:r0 + CHUNK, 2 * dq:3 * dq] = (zcol(r0, 5) * conv_c).astype(BF16)

    gc = dq // len(POOL_WINDOWS)
    lane_grp = lax.broadcasted_iota(I32, (CHUNK, dq), 1) // gc
    lead = 2 * SUBLANES
    for r0 in chunk_rows:
        ext_d[HIST_D + r0:HIST_D + r0 + rb, :] = zcol(r0, 7)
    for r0 in chunk_rows:
        e = ext_d[HIST_D + r0 - lead:HIST_D + r0 + CHUNK, :]
        s2 = e + pltpu.roll(e, 1, axis=0)
        s4 = s2 + pltpu.roll(s2, 2, axis=0)
        s8 = s4 + pltpu.roll(s4, 4, axis=0)
        s16 = s8 + pltpu.roll(s8, 8, axis=0)
        sums = (s2, s4, s8, s16)
        pos = (j * tl + r0 + lax.broadcasted_iota(I32, (CHUNK, 1), 0)).astype(F32)
        mean = jnp.zeros((CHUNK, dq), F32)
        for g, w in enumerate(POOL_WINDOWS):
            inv = 1.0 / jnp.minimum(pos + 1.0, float(w))
            mean = jnp.where(lane_grp == g, sums[g][lead:lead + CHUNK, :] * inv, mean)
        dd = (mean - e[lead:lead + CHUNK, :]).astype(BF16)
        cat_ref[r0:r0 + CHUNK, 3 * dq:4 * dq] = (_dot(dd, wpool_ref[...]) * pscale_ref[...]).astype(BF16)

    mix_ref[...] = _dot(cat_ref[...], wout_ref[...])
    for r0 in range(0, tl, 64):
        h_out[0, r0:r0 + 64, :] = _ln(alpha * resid(r0, 64) + mix_ref[r0:r0 + 64, :], lmg_ref[...], lmb_ref[...])
    if route:
        _route_block(tl, h_out[0], wcat_ref, mi_ref, mf_ref, cnt_ref, carry_ref, before_ref)

    @pl.when(j == nj - 1)
    def _():
        sa_out[0] = ext_a[0, HIST_A + tl - (CONV_A_W - 1):HIST_A + tl, :]
        sc_out[0] = ext_c[HIST_C + tl - (CONV_C_W - 1):HIST_C + tl, :]
        sd_out[0] = ext_d[HIST_D + tl - POOL_PAST:HIST_D + tl, :]

    ext_a[0, 0:HIST_A, :] = ext_a[0, tl:tl + HIST_A, :]
    ext_c[0:HIST_C, :] = ext_c[tl:tl + HIST_C, :]
    ext_d[0:HIST_D, :] = ext_d[tl:tl + HIST_D, :]


def _full_spec(arr):
    nd = arr.ndim
    return pl.BlockSpec(arr.shape, lambda *_: (0,) * nd)


def _layer_spec(arr, layer):
    nd = len(arr.shape) - 1
    return pl.BlockSpec((None,) + tuple(arr.shape[1:]), lambda *_: (layer,) + (0,) * nd)


def _param_spec(lw, arr):
    stacked = arr is lw['w_in'] or arr is lw['w_out']
    return _layer_spec(arr, lw['layer']) if stacked else _full_spec(arr)


def _mixer_prompt(x, lw, first_layer, alpha, route_w=None):
    bsz, seq, d = x.shape
    d_mix = lw['w_out'].shape[1]
    dq = d_mix // 4
    tl = TL_PROMPT
    nj = seq // tl
    route = route_w is not None
    assert seq % tl == 0 and tl % CHUNK == 0 and seq >= CHUNK
    params = [lw['ln_in_g'], lw['ln_in_b'], lw['w_in'], lw['conv_a_w'], lw['conv_a_b'], lw['ln_a_g'],
              lw['ln_a_b'], lw['ln_v_g'], lw['ln_v_b'], lw['ws_tril'], lw['bs_mat'], lw['conv_c_w'],
              lw['w_pool_bd'], lw['pool_scale'], lw['w_out'], lw['ln_mix_g'], lw['ln_mix_b']]
    out_shape = [
        jax.ShapeDtypeStruct((bsz, seq, d), F32),
        jax.ShapeDtypeStruct((bsz, CONV_A_W - 1, dq), F32),
        jax.ShapeDtypeStruct((bsz, CONV_C_W - 1, dq), F32),
        jax.ShapeDtypeStruct((bsz, POOL_PAST, dq), F32),
        jax.ShapeDtypeStruct((bsz, CHUNK, dq), F32),
    ]
    state_spec = lambda rows: pl.BlockSpec((1, rows, dq), lambda b, j: (b, 0, 0))
    out_specs = [pl.BlockSpec((1, tl, d), lambda b, j: (b, j, 0)),
                 state_spec(CONV_A_W - 1), state_spec(CONV_C_W - 1), state_spec(POOL_PAST), state_spec(CHUNK)]
    scratch = [pltpu.VMEM((SUBLANES, HIST_A + tl, dq), F32), pltpu.VMEM((HIST_C + tl, dq), F32),
               pltpu.VMEM((HIST_D + tl, dq), F32), pltpu.VMEM((tl, d_mix), BF16),
               pltpu.VMEM((tl, lw['w_in'].shape[2]), F32), pltpu.VMEM((tl, d), BF16),
               pltpu.VMEM((tl, d), F32)]
    if route:
        assert N_EXPERTS == SUBLANES
        params += [route_w, jnp.zeros((SUBLANES, LANES), I32)]
        out_shape += [jax.ShapeDtypeStruct((SUBLANES, bsz * seq), I32),
                      jax.ShapeDtypeStruct((bsz * seq, SUBLANES), F32),
                      jax.ShapeDtypeStruct((SUBLANES, LANES), I32)]
        out_specs += [pl.BlockSpec((SUBLANES, tl), lambda b, j: (0, b * nj + j)),
                      pl.BlockSpec((tl, SUBLANES), lambda b, j: (b * nj + j, 0)),
                      pl.BlockSpec((SUBLANES, LANES), lambda b, j: (0, 0))]
        scratch += [pltpu.VMEM((SUBLANES, LANES), F32), pltpu.VMEM((tl, tl), BF16)]
    return pl.pallas_call(
        functools.partial(_mixer_prompt_kernel, first_layer, route, alpha, tl, d_mix),
        grid=(bsz, nj),
        in_specs=[pl.BlockSpec((1, tl, d), lambda b, j: (b, j, 0))] + [_param_spec(lw, p) for p in params],
        out_specs=tuple(out_specs),
        out_shape=tuple(out_shape),
        scratch_shapes=scratch,
        compiler_params=_cparams(("arbitrary", "arbitrary")),
        name="mixer_prompt",
    )(x, *params)


def _mixer_sample_kernel(first_layer, alpha, n_seq, n_step, d_mix,
                         x_ref, sa_ref, sc_ref, sd_ref, lng_ref, lnb_ref, win_ref, caw_ref, cab_ref,
                         lag_ref, lab_ref, lvg_ref, lvb_ref, wsv_ref, bsv_ref, ccw_ref, wpool_ref,
                         pscale_ref, wout_ref, lmg_ref, lmb_ref, *rest):
    h_out, sa_out, sc_out, sd_out, v_out, cat_ref = rest[-6:]
    dq = d_mix // 4
    x = x_ref[...]
    h = _ln(x, lng_ref[...], lnb_ref[...]) if first_layer else x
    hb = h.astype(BF16)

    def proj(i):
        return _dot(hb, win_ref[:, i * dq:(i + 1) * dq])

    def slab(val, i):
        return val[i * n_seq:(i + 1) * n_seq, :]

    def ext_slabs(state_ref, n_past, cur):
        return ([state_ref[i * n_seq:(i + 1) * n_seq, :] for i in range(n_past)]
                + [slab(cur, i) for i in range(n_step)])

    def store_state(out_ref, slabs, n_keep):
        for i, sl in enumerate(slabs[len(slabs) - n_keep:]):
            out_ref[i * n_seq:(i + 1) * n_seq, :] = sl

    a_glu = proj(0) * jax.nn.sigmoid(proj(1))
    ea = ext_slabs(sa_ref, CONV_A_W - 1, a_glu)
    for l in range(n_step):
        acc = jnp.zeros((n_seq, dq), F32)
        for k in range(CONV_A_W):
            acc = acc + ea[l + k] * caw_ref[k:k + 1, :]
        y = _silu(_ln(acc + cab_ref[...], lag_ref[...], lab_ref[...]))
        cat_ref[l * n_seq:(l + 1) * n_seq, 0:dq] = y.astype(BF16)
    store_state(sa_out, ea, CONV_A_W - 1)

    u = _gelu(proj(2))
    v = _ln(_gelu(proj(3)), lvg_ref[...], lvb_ref[...])
    v_out[...] = v
    for l in range(n_step):
        s = jnp.zeros((n_seq, dq), F32) + bsv_ref[l:l + 1, :]
        for m in range(l + 1):
            s = s + slab(v, m) * wsv_ref[l * n_step + m:l * n_step + m + 1, :]
        cat_ref[l * n_seq:(l + 1) * n_seq, dq:2 * dq] = (slab(u, l) * s).astype(BF16)

    c_x = proj(4)
    c_b = proj(5)
    c_c = proj(6)
    gx = c_c * c_x
    ec = ext_slabs(sc_ref, CONV_C_W - 1, gx)
    for l in range(n_step):
        acc = jnp.zeros((n_seq, dq), F32)
        for k in range(CONV_C_W):
            acc = acc + ec[l + k] * ccw_ref[k:k + 1, :]
        cat_ref[l * n_seq:(l + 1) * n_seq, 2 * dq:3 * dq] = (slab(c_b, l) * acc).astype(BF16)
    store_state(sc_out, ec, CONV_C_W - 1)

    d_in = proj(7)
    ed = ext_slabs(sd_ref, POOL_PAST, d_in)
    memo = {}

    def wsum(i, w):
        if i < 0:
            return None
        if w == 1:
            return ed[i]
        if (i, w) not in memo:
            a, b = wsum(i, w // 2), wsum(i - w // 2, w // 2)
            memo[(i, w)] = a if b is None else a + b
        return memo[(i, w)]

    gc = dq // len(POOL_WINDOWS)
    lane_grp = lax.broadcasted_iota(I32, (n_seq, dq), 1) // gc
    for l in range(n_step):
        mean = jnp.zeros((n_seq, dq), F32)
        for g, w in enumerate(POOL_WINDOWS):
            count = min(PAST_LEN + l + 1, w)
            mean = jnp.where(lane_grp == g, wsum(POOL_PAST + l, w) * (1.0 / count), mean)
        dd = (mean - slab(d_in, l)).astype(BF16)
        cat_ref[l * n_seq:(l + 1) * n_seq, 3 * dq:4 * dq] = (
            _dot(dd, wpool_ref[...]) * pscale_ref[...]).astype(BF16)
    store_state(sd_out, ed, POOL_PAST)

    mix = _dot(cat_ref[...], wout_ref[...])
    h_out[...] = _ln(alpha * h + mix, lmg_ref[...], lmb_ref[...])


def _mixer_sample(x_tm, sa_all, sc, sd_all, prev, layer, lw, first_layer, alpha, n_seq, n_step):
    rows, d = x_tm.shape
    d_mix = lw['w_out'].shape[1]
    dq = d_mix // 4
    assert n_seq % SUBLANES == 0 and n_step <= CHUNK and PAST_LEN % CHUNK == 0
    ins = [x_tm, sa_all, sc, sd_all, lw['ln_in_g'], lw['ln_in_b'], lw['w_in'], lw['conv_a_w'], lw['conv_a_b'],
           lw['ln_a_g'], lw['ln_a_b'], lw['ln_v_g'], lw['ln_v_b'], lw['ws_vec'], lw['bs_vec'],
           lw['conv_c_w'], lw['w_pool_bd'], lw['pool_scale'], lw['w_out'], lw['ln_mix_g'], lw['ln_mix_b']]
    layer_spec = lambda a: _layer_spec(a, layer)
    in_specs = [layer_spec(a) if a is sa_all or a is sd_all else _param_spec(lw, a) for a in ins]
    out_shape = (
        jax.ShapeDtypeStruct((rows, d), F32),
        jax.ShapeDtypeStruct(sa_all.shape, F32),
        jax.ShapeDtypeStruct(((CONV_C_W - 1) * n_seq, dq), F32),
        jax.ShapeDtypeStruct(sd_all.shape, F32),
        jax.ShapeDtypeStruct((rows, dq), F32),
    )
    out_specs = tuple(layer_spec(s) if len(s.shape) == 3 else pl.BlockSpec(s.shape, lambda g: (0, 0))
                      for s in out_shape)
    aliases = {}
    if prev is not None:
        aliases = {len(ins): 1, len(ins) + 1: 3}
        ins = ins + list(prev)
        in_specs = in_specs + [pl.BlockSpec(memory_space=pl.ANY)] * 2
    return pl.pallas_call(
        functools.partial(_mixer_sample_kernel, first_layer, alpha, n_seq, n_step, d_mix),
        grid=(1,),
        in_specs=in_specs,
        out_specs=out_specs,
        out_shape=out_shape,
        input_output_aliases=aliases,
        scratch_shapes=[pltpu.VMEM((rows, d_mix), BF16)],
        compiler_params=_cparams(("arbitrary",)),
        name="mixer_sample",
    )(*ins)


def _stream_cast(streams):
    plans = []
    for pairs, stage, sems in streams:
        rows = stage.shape[1]
        plans.append(([(src, dst, r0) for src, dst in pairs for r0 in range(0, src.shape[0], rows)],
                      stage, sems))

    def copy(plan, c):
        chunks, stage, sems = plan
        src, _, r0 = chunks[c]
        slot = c % stage.shape[0]
        return pltpu.make_async_copy(src.at[pl.ds(r0, stage.shape[1]), :], stage.at[slot], sems.at[slot])

    for plan in plans:
        for c in range(min(plan[1].shape[0], len(plan[0]))):
            copy(plan, c).start()
    for c in range(max(len(plan[0]) for plan in plans)):
        for plan in plans:
            chunks, stage, _ = plan
            if c < len(chunks):
                _, dst, r0 = chunks[c]
                copy(plan, c).wait()
                dst[pl.ds(r0, stage.shape[1]), :] = stage[c % stage.shape[0]].astype(BF16)
                if c + stage.shape[0] < len(chunks):
                    copy(plan, c + stage.shape[0]).start()


def _ffn_dense_kernel(alpha, tf, n_main, x_ref, xe_ref, wg_hbm, wu_hbm, wd_hbm, g_ref, b_ref, o_ref, oe_ref,
                      wg_ref, wu_ref, wd_ref, stage_in, stage_out, sems):
    i = pl.program_id(0)

    @pl.when(i == 0)
    def _():
        _stream_cast([([(wg_hbm, wg_ref), (wu_hbm, wu_ref)], stage_in, sems.at[0]),
                      ([(wd_hbm, wd_ref)], stage_out, sems.at[1])])

    def run(xr, orf):
        x = xr[...]
        xb = x.astype(BF16)
        ffn = None
        for c0 in range(0, wg_ref.shape[1], tf):
            mid = _silu(_dot(xb, wg_ref[:, c0:c0 + tf])) * _dot(xb, wu_ref[:, c0:c0 + tf])
            part = _dot(mid.astype(BF16), wd_ref[c0:c0 + tf, :])
            ffn = part if ffn is None else ffn + part
        orf[...] = _ln(alpha * x + ffn, g_ref[...], b_ref[...])

    @pl.when(i < n_main)
    def _():
        run(x_ref, o_ref)

    @pl.when(i >= n_main)
    def _():
        run(xe_ref, oe_ref)


def _ffn_dense(x, x_extra, wg, wu, wd, g, b, alpha, tm, tf):
    t, d = x.shape
    te = x_extra.shape[0]
    dff = wg.shape[1]
    n_main, n_extra = t // tm, te // tm
    rows_in, rows_out = 128, 512
    assert t % tm == 0 and te % tm == 0 and dff % tf == 0 and d % rows_in == 0 and dff % rows_out == 0
    any_spec = pl.BlockSpec(memory_space=pl.ANY)
    main_blk = lambda i: (jnp.minimum(i, n_main - 1), 0)
    extra_blk = lambda i: (jnp.maximum(i - n_main, 0), 0)
    return pl.pallas_call(
        functools.partial(_ffn_dense_kernel, alpha, tf, n_main),
        grid=(n_main + n_extra,),
        in_specs=[pl.BlockSpec((tm, d), main_blk), pl.BlockSpec((tm, d), extra_blk),
                  any_spec, any_spec, any_spec,
                  pl.BlockSpec((1, d), lambda i: (0, 0)),
                  pl.BlockSpec((1, d), lambda i: (0, 0))],
        out_specs=(pl.BlockSpec((tm, d), main_blk), pl.BlockSpec((tm, d), extra_blk)),
        out_shape=(jax.ShapeDtypeStruct((t, d), F32), jax.ShapeDtypeStruct((te, d), F32)),
        scratch_shapes=[pltpu.VMEM(wg.shape, BF16), pltpu.VMEM(wu.shape, BF16), pltpu.VMEM(wd.shape, BF16),
                        pltpu.VMEM((3, rows_in, dff), F32), pltpu.VMEM((3, rows_out, d), F32),
                        pltpu.SemaphoreType.DMA((2, 3))],
        compiler_params=_cparams(("arbitrary",)),
        name="ffn_dense",
    )(x, x_extra, wg, wu, wd, g, b)


def _route_init(tm, cin_ref, carry_ref, before_ref):
    carry_ref[...] = cin_ref[...].astype(F32)
    row = lax.broadcasted_iota(I32, (tm, tm), 0)
    col = lax.broadcasted_iota(I32, (tm, tm), 1)
    before_ref[...] = jnp.where(row < col, 1.0, 0.0).astype(BF16)


def _route_block(tm, h, wcat_ref, mi_ref, mf_ref, cnt_ref, carry_ref, before_ref):
    h_hi = h.astype(BF16)
    h_lo = (h - h_hi.astype(F32)).astype(BF16)
    p_hi = _dot(h_hi, wcat_ref[...])
    p_lo = _dot(h_lo, wcat_ref[...])
    logits = (p_hi[:, 0:LANES] + (p_lo[:, 0:LANES] + p_hi[:, LANES:2 * LANES])) + p_lo[:, LANES:2 * LANES]
    lg = logits.T[0:N_EXPERTS, :]
    ex = lax.broadcasted_iota(I32, (N_EXPERTS, tm), 0).astype(F32)
    m1 = jnp.max(lg, axis=0, keepdims=True)
    i1 = jnp.min(jnp.where(lg == m1, ex, float(N_EXPERTS)), axis=0, keepdims=True)
    sel1 = ex == i1
    rest = jnp.where(sel1, -jnp.inf, lg)
    m2 = jnp.max(rest, axis=0, keepdims=True)
    i2 = jnp.min(jnp.where(rest == m2, ex, float(N_EXPERTS)), axis=0, keepdims=True)
    sel2 = ex == i2
    e2 = jnp.exp(m2 - m1)
    den = 1.0 + e2
    g1 = 1.0 / den
    g2 = e2 / den

    sel = jnp.where(sel1 | sel2, 1.0, 0.0)
    sel_pad = jnp.concatenate([sel, jnp.zeros_like(sel)], axis=0).astype(BF16)
    base = _dot(sel_pad, before_ref[...])[0:N_EXPERTS, :] + carry_ref[:, 0:1]
    r1 = jnp.sum(jnp.where(sel1, base, 0.0), axis=0, keepdims=True)
    r2 = jnp.sum(jnp.where(sel2, base, 0.0), axis=0, keepdims=True)
    carry_ref[...] = carry_ref[...] + jnp.sum(sel, axis=1, keepdims=True)

    meta = jnp.where(ex == 0, i1, jnp.where(ex == 1, i2, jnp.where(ex == 2, r1, r2)))
    mi_ref[...] = meta.astype(I32)
    gt = jnp.where(ex == 0, g1, jnp.where(ex == 1, g2, 0.0))
    gt = jnp.concatenate([gt, jnp.zeros((LANES - N_EXPERTS, tm), F32)], axis=0)
    mf_ref[...] = gt.T[:, 0:SUBLANES]
    cnt_ref[...] = carry_ref[...].astype(I32)


def _router_kernel(tm, h_ref, wcat_ref, cin_ref, mi_ref, mf_ref, cnt_ref, carry_ref, before_ref):
    @pl.when(pl.program_id(0) == 0)
    def _():
        _route_init(tm, cin_ref, carry_ref, before_ref)

    _route_block(tm, h_ref[...], wcat_ref, mi_ref, mf_ref, cnt_ref, carry_ref, before_ref)


def _router(h, wr_cat, counts_in):
    t, d = h.shape
    tm = min(TM_ROUTE, t)
    assert t % tm == 0 and N_EXPERTS == SUBLANES
    return pl.pallas_call(
        functools.partial(_router_kernel, tm),
        grid=(t // tm,),
        in_specs=[pl.BlockSpec((tm, d), lambda i: (i, 0)), _full_spec(wr_cat), _full_spec(counts_in)],
        out_specs=(pl.BlockSpec((SUBLANES, tm), lambda i: (0, i)),
                   pl.BlockSpec((tm, SUBLANES), lambda i: (i, 0)),
                   pl.BlockSpec((SUBLANES, LANES), lambda i: (0, 0))),
        out_shape=(jax.ShapeDtypeStruct((SUBLANES, t), I32),
                   jax.ShapeDtypeStruct((t, SUBLANES), F32),
                   jax.ShapeDtypeStruct((SUBLANES, LANES), I32)),
        scratch_shapes=[pltpu.VMEM((SUBLANES, LANES), F32), pltpu.VMEM((tm, tm), BF16)],
        compiler_params=_cparams(("arbitrary",)),
        name="router",
    )(h, wr_cat, counts_in)


def _row_copy(src, src_row, dst, dst_row, sem):
    return pltpu.make_async_copy(src.at[pl.ds(src_row, 1), :], dst.at[pl.ds(dst_row, 1), :], sem)


def _rows_wait(src, dst, n_rows, sem):
    pltpu.make_async_copy(src.at[pl.ds(0, n_rows), :], dst.at[pl.ds(0, n_rows), :], sem).wait()


def _dispatch_kernel(tm, tm_expert, n_blocks, blk_ranges, zi_ref, dest_ref, *rest):
    n_src = len(blk_ranges)
    h_refs = rest[:n_src]
    xs_ref, xbuf, zblk, lsems, rsems, zsem = rest[n_src:]
    i = pl.program_id(0)
    n = pl.num_programs(0)

    def block_load(g, start):
        slot = lax.rem(g, N_STAGE)
        for h_ref, (b0, b1) in zip(h_refs, blk_ranges):
            @pl.when((g >= b0) & (g < b1))
            def _(h_ref=h_ref, b0=b0):
                cp = pltpu.make_async_copy(h_ref.at[pl.ds((g - b0) * tm, tm), :], xbuf.at[slot],
                                           lsems.at[slot])
                if start:
                    cp.start()
                else:
                    cp.wait()

    def rows_wait(g):
        slot = lax.rem(g, N_STAGE)
        for _k in range(TOP_K):
            _rows_wait(xbuf.at[slot], xs_ref, tm, rsems.at[slot])

    @pl.when(i == 0)
    def _():
        block_load(i, True)

    @pl.when(i >= N_STAGE - 1)
    def _():
        rows_wait(i - (N_STAGE - 1))

    @pl.when(i + 1 < n)
    def _():
        block_load(i + 1, True)

    block_load(i, False)
    cur = lax.rem(i, N_STAGE)

    def issue(j, carry):
        r0 = pl.multiple_of(j * SUBLANES, SUBLANES)
        for u in range(SUBLANES):
            for k in range(TOP_K):
                dst = dest_ref[0, 0, j * SUBLANES + (k * tm + u)]
                _row_copy(xbuf.at[cur], r0 + u, xs_ref, dst, rsems.at[cur]).start(priority=k)
        return carry

    lax.fori_loop(0, tm // SUBLANES, issue, 0)

    @pl.when(i == n - 1)
    def _():
        for back in range(N_STAGE - 2, -1, -1):
            @pl.when(i >= back)
            def _(back=back):
                rows_wait(i - back)

    @pl.when(i == 0)
    def _():
        zblk[...] = jnp.zeros_like(zblk)
        for e in range(N_EXPERTS):
            lo = zi_ref[e] + zi_ref[N_EXPERTS + e]
            hi = zi_ref[e] + zi_ref[2 * N_EXPERTS + e]

            def zissue(r, carry):
                _row_copy(zblk, 0, xs_ref, r, zsem).start()
                return carry

            def zdrain(r, carry):
                _row_copy(zblk, 0, xs_ref, r, zsem).wait()
                return carry

            lax.fori_loop(lo, hi, zissue, 0)
            lax.fori_loop(lo, hi, zdrain, 0)

        def bcopy(b):
            return pltpu.make_async_copy(zblk, xs_ref.at[pl.ds(b * tm_expert, tm_expert), :], zsem)

        def bissue(b, carry):
            bcopy(b).start()
            return carry

        def bdrain(b, carry):
            bcopy(b).wait()
            return carry

        lax.fori_loop(zi_ref[3 * N_EXPERTS], n_blocks, bissue, 0)
        lax.fori_loop(zi_ref[3 * N_EXPERTS], n_blocks, bdrain, 0)


def _dispatch(hs_list, dest_blk, zinfo, n_blocks, tm_expert):
    d = hs_list[0].shape[1]
    nb = dest_blk.shape[0]
    tm = dest_blk.shape[2] // TOP_K
    blk_ranges, b0 = [], 0
    for h in hs_list:
        assert h.shape[0] % tm == 0 and h.shape[0] >= tm
        blk_ranges.append((b0, b0 + h.shape[0] // tm))
        b0 = blk_ranges[-1][1]
    assert b0 == nb
    any_spec = pl.BlockSpec(memory_space=pl.ANY)
    return pl.pallas_call(
        functools.partial(_dispatch_kernel, tm, tm_expert, n_blocks, tuple(blk_ranges)),
        grid_spec=pltpu.PrefetchScalarGridSpec(
            num_scalar_prefetch=1,
            grid=(nb,),
            in_specs=[pl.BlockSpec((1, 1, TOP_K * tm), lambda i, zi: (i, 0, 0), memory_space=pltpu.SMEM)]
            + [any_spec] * len(hs_list),
            out_specs=any_spec,
            scratch_shapes=[pltpu.VMEM((N_STAGE, tm, d), F32), pltpu.VMEM((tm_expert, d), F32),
                            pltpu.SemaphoreType.DMA((N_STAGE,)), pltpu.SemaphoreType.DMA((N_STAGE,)),
                            pltpu.SemaphoreType.DMA(())],
        ),
        out_shape=jax.ShapeDtypeStruct((n_blocks * tm_expert, d), F32),
        compiler_params=_cparams(("arbitrary",)),
        name="moe_dispatch",
    )(zinfo, dest_blk, *hs_list)


def _ffn_moe_kernel(tf, be_ref, nu_ref, x_ref, wg_hbm, wu_hbm, wd_hbm, y_ref,
                    wg_ref, wu_ref, wd_ref, stage_in, stage_out, sems):
    b = pl.program_id(0)
    nb = pl.num_programs(0)
    d, dff = wg_ref.shape
    n_slots = stage_in.shape[0]
    rows_in = stage_in.shape[1]
    rows_out = stage_out.shape[1]
    n_in, n_out = 2 * (d // rows_in), dff // rows_out
    e = be_ref[b]
    fresh = (b == 0) | (e != be_ref[jnp.maximum(b - 1, 0)])
    e_next = be_ref[jnp.minimum(b + 1, nb - 1)]

    def chunk_in(ex, n):
        src, dst = (wg_hbm, wg_ref) if n < n_in // 2 else (wu_hbm, wu_ref)
        rows = pl.ds((n % (n_in // 2)) * rows_in, rows_in)
        stage = stage_in.at[n % n_slots]
        return pltpu.make_async_copy(src.at[ex, rows, :], stage, sems.at[0, n % n_slots]), stage, dst.at[rows, :]

    def chunk_out(ex, n):
        rows = pl.ds(n * rows_out, rows_out)
        stage = stage_out.at[n % n_slots]
        return pltpu.make_async_copy(wd_hbm.at[ex, rows, :], stage, sems.at[1, n % n_slots]), stage, wd_ref.at[rows, :]

    def request_first(ex):
        for n in range(n_slots):
            chunk_in(ex, n)[0].start()
            chunk_out(ex, n)[0].start()

    @pl.when((b < nu_ref[0]) & fresh)
    def _():
        @pl.when(b == 0)
        def _():
            request_first(e)

        for n in range(n_in):
            for chunk, count in ((chunk_in, n_in), (chunk_out, n_out)):
                if n < count:
                    copy, staged, dst = chunk(e, n)
                    copy.wait()
                    dst[...] = staged[...].astype(BF16)
                    if n + n_slots < count:
                        chunk(e, n + n_slots)[0].start()

    @pl.when((b + 1 < nu_ref[0]) & (e_next != e))
    def _():
        request_first(e_next)

    tm = x_ref.shape[0]
    rows_used = nu_ref[1 + b]

    def compute(rows):
        xb = x_ref[0:rows, :].astype(BF16)
        ffn = None
        for c0 in range(0, dff, tf):
            mid = _silu(_dot(xb, wg_ref[:, c0:c0 + tf])) * _dot(xb, wu_ref[:, c0:c0 + tf])
            part = _dot(mid.astype(BF16), wd_ref[c0:c0 + tf, :])
            ffn = part if ffn is None else ffn + part
        y_ref[0:rows, :] = ffn
        if rows < tm:
            y_ref[rows:tm, :] = jnp.zeros((tm - rows, y_ref.shape[1]), F32)

    step = tm // 4
    for rows in range(step, tm + 1, step):
        fits = (rows_used <= rows) if rows < tm else True
        needs = (rows_used > rows - step) if rows > step else True
        @pl.when((b < nu_ref[0]) & fits & needs)
        def _(rows=rows):
            compute(rows)

    @pl.when(b >= nu_ref[0])
    def _():
        y_ref[...] = jnp.zeros_like(y_ref)


def _ffn_moe(xs, wg, wu, wd, block_e, n_used, tm, tf):
    cap, d = xs.shape
    dff = wg.shape[2]
    rows_in, rows_out = 128, 512
    assert cap % tm == 0 and dff % tf == 0 and d % rows_in == 0 and dff % rows_out == 0
    assert min(2 * (d // rows_in), dff // rows_out) >= N_WSTAGE
    any_spec = pl.BlockSpec(memory_space=pl.ANY)
    return pl.pallas_call(
        functools.partial(_ffn_moe_kernel, tf),
        grid_spec=pltpu.PrefetchScalarGridSpec(
            num_scalar_prefetch=2,
            grid=(cap // tm,),
            in_specs=[pl.BlockSpec((tm, d), lambda b, be, nu: (jnp.minimum(b, nu[0] - 1), 0)),
                      any_spec, any_spec, any_spec],
            out_specs=pl.BlockSpec((tm, d), lambda b, be, nu: (b, 0)),
            scratch_shapes=[pltpu.VMEM((d, dff), BF16), pltpu.VMEM((d, dff), BF16), pltpu.VMEM((dff, d), BF16),
                            pltpu.VMEM((N_WSTAGE, rows_in, dff), F32), pltpu.VMEM((N_WSTAGE, rows_out, d), F32),
                            pltpu.SemaphoreType.DMA((2, N_WSTAGE))],
        ),
        out_shape=jax.ShapeDtypeStruct((cap, d), F32),
        compiler_params=_cparams(("arbitrary",)),
        name="ffn_moe",
    )(block_e, n_used, xs, wg, wu, wd)


def _combine_kernel(alpha, tm, dcur_ref, dnext_ref, h_ref, gate_ref, g_ref, b_ref, ys_ref, o_ref, ybuf, sems):
    i = pl.program_id(0)
    n = pl.num_programs(0)

    def issue(dref, slot):
        def body(j, carry):
            r0 = pl.multiple_of(j * SUBLANES, SUBLANES)
            for u in range(SUBLANES):
                for k in range(TOP_K):
                    src = dref[0, 0, j * SUBLANES + (k * tm + u)]
                    _row_copy(ys_ref, src, ybuf.at[slot, k], r0 + u, sems.at[slot]).start(priority=k)
            return carry

        lax.fori_loop(0, tm // SUBLANES, body, 0)

    @pl.when(i == 0)
    def _():
        issue(dcur_ref, 0)

    @pl.when(i + 1 < n)
    def _():
        issue(dnext_ref, (i + 1) % 2)

    slot = i % 2
    for k in range(TOP_K):
        _rows_wait(ys_ref, ybuf.at[slot, k], tm, sems.at[slot])

    gates = gate_ref[...]
    ffn = gates[:, 0:1] * ybuf[slot, 0] + gates[:, 1:2] * ybuf[slot, 1]
    o_ref[...] = _ln(alpha * h_ref[...] + ffn, g_ref[...], b_ref[...])


def _combine(h, dest_blk, gates, ys, g, b, alpha):
    t, d = h.shape
    nb = dest_blk.shape[0]
    tm = dest_blk.shape[2] // TOP_K
    dest_spec = lambda imap: pl.BlockSpec((1, 1, TOP_K * tm), imap, memory_space=pltpu.SMEM)
    return pl.pallas_call(
        functools.partial(_combine_kernel, alpha, tm),
        grid=(nb,),
        in_specs=[dest_spec(lambda i: (i, 0, 0)),
                  dest_spec(lambda i: (jnp.minimum(i + 1, nb - 1), 0, 0)),
                  pl.BlockSpec((tm, d), lambda i: (i, 0)),
                  pl.BlockSpec((tm, SUBLANES), lambda i: (i, 0)),
                  pl.BlockSpec((1, d), lambda i: (0, 0)),
                  pl.BlockSpec((1, d), lambda i: (0, 0)),
                  pl.BlockSpec(memory_space=pl.ANY)],
        out_specs=pl.BlockSpec((tm, d), lambda i: (i, 0)),
        out_shape=jax.ShapeDtypeStruct((t, d), F32),
        scratch_shapes=[pltpu.VMEM((2, TOP_K, tm, d), F32), pltpu.SemaphoreType.DMA((2,))],
        compiler_params=_cparams(("arbitrary",)),
        name="moe_combine",
    )(dest_blk, dest_blk, h, gates, g, b, ys)


def _ffn_routed(hs_list, first_routed, mw, g, b, alpha, tm_expert):
    routed = [first_routed[:2]]
    counts_in = first_routed[2]
    for h in hs_list[1:]:
        meta_i, gates, counts_in = _router(h, mw['w_router_cat'], counts_in)
        routed.append((meta_i, gates))
    counts = counts_in[:N_EXPERTS, 0]
    padded = (counts + tm_expert - 1) // tm_expert * tm_expert
    pad_end = jnp.cumsum(padded)
    pad_start = (pad_end - padded).astype(I32)
    n_assign = sum(h.shape[0] for h in hs_list) * TOP_K
    n_blocks = -(-n_assign // tm_expert) + N_EXPERTS
    cap = n_blocks * tm_expert
    n_used = (pad_end[-1] // tm_expert).astype(I32)
    blk_start = jnp.minimum(jnp.arange(n_blocks, dtype=I32), n_used - 1) * tm_expert
    block_e = jnp.minimum(jnp.sum(blk_start[:, None] >= pad_end[None, :], axis=1), N_EXPERTS - 1).astype(I32)
    zinfo = jnp.concatenate([pad_start, counts, padded, n_used.reshape(1)]).astype(I32)

    meta_all = jnp.concatenate([meta_i for meta_i, _ in routed], axis=1)
    dest = meta_all[TOP_K:2 * TOP_K, :]
    for e in range(N_EXPERTS):
        dest = dest + jnp.where(meta_all[0:TOP_K, :] == e, pad_start[e], 0)
    nb_all = dest.shape[1] // TM_MOVE
    dest_all = dest.reshape(TOP_K, nb_all, TM_MOVE).transpose(1, 0, 2).reshape(nb_all, 1, TOP_K * TM_MOVE)
    dests, b0 = [], 0
    for h in hs_list:
        assert h.shape[0] % TM_MOVE == 0
        dests.append(dest_all[b0:b0 + h.shape[0] // TM_MOVE])
        b0 += h.shape[0] // TM_MOVE
    xs = _dispatch(hs_list, dest_all, zinfo, n_blocks, tm_expert)
    filled_end = jnp.sum(jnp.where(block_e[:, None] == jnp.arange(N_EXPERTS, dtype=I32)[None, :],
                                   (pad_start + counts)[None, :], 0), axis=1)
    occupancy = jnp.clip(filled_end - jnp.arange(n_blocks, dtype=I32) * tm_expert, 0, tm_expert)
    used_info = jnp.concatenate([n_used.reshape(1), occupancy]).astype(I32)
    ys = _ffn_moe(xs, mw['w_e_gate'], mw['w_e_up'], mw['w_e_down'], block_e, used_info, tm_expert,
                  TF_EXPERT)
    return [_combine(h, dest_blk, gates, ys, g, b, alpha)
            for h, dest_blk, (_, gates) in zip(hs_list, dests, routed)]


def _prep_layer(i, p, n_step):
    dq = p['conv_a_w'].shape[2]
    hd = dq // N_HEADS_B
    tril = jnp.tril(jnp.ones((CHUNK, CHUNK), dtype=bool))
    w_s = p['w_s'][i]
    b_s = p['b_s'][i]
    wp = p['w_pool'][i]
    ng, gc = wp.shape[0], wp.shape[1]
    w_pool_bd = (jnp.eye(ng, dtype=F32)[:, None, :, None] * wp[:, :, None, :]).reshape(ng * gc, ng * gc)
    row = lambda a: a.reshape(1, -1)
    return dict(
        ln_in_g=row(p['ln_in_g']), ln_in_b=row(p['ln_in_b']),
        layer=i, w_in=p['w_in_bf'],
        conv_a_w=p['conv_a_w'][i], conv_a_b=row(p['conv_a_b'][i]),
        ln_a_g=row(p['ln_a_g'][i]), ln_a_b=row(p['ln_a_b'][i]),
        ln_v_g=row(p['ln_v_g'][i]), ln_v_b=row(p['ln_v_b'][i]),
        ws_tril=jnp.where(tril[None], w_s, 0).astype(BF16),
        bs_mat=jnp.repeat(b_s.T, hd, axis=1),
        ws_vec=jnp.repeat(jnp.where(tril[None], w_s, 0)[:, :n_step, :n_step].transpose(1, 2, 0)
                          .reshape(n_step * n_step, N_HEADS_B), hd, axis=1),
        bs_vec=jnp.repeat(b_s[:, :n_step].T, hd, axis=1),
        conv_c_w=p['conv_c_w'][i],
        w_pool_bd=w_pool_bd.astype(BF16), pool_scale=row(p['pool_scale'][i]),
        w_out=p['w_out_bf'],
        ln_mix_g=row(p['ln_mix_g'][i]), ln_mix_b=row(p['ln_mix_b'][i]),
    )


def kernel(x_prompt, x_sample, state_conv_a, state_conv_c, state_pool_d, ln_in_g, ln_in_b, w_in, conv_a_w,
           conv_a_b, ln_a_g, ln_a_b, ln_v_g, ln_v_b, w_s, b_s, conv_c_w, w_pool, pool_scale, w_out,
           ln_mix_g, ln_mix_b, w_ff_gate, w_ff_up, w_ff_down, w_router, w_e_gate, w_e_up, w_e_down,
           ln_ffn_g, ln_ffn_b):
    p = dict(ln_in_g=ln_in_g, ln_in_b=ln_in_b, w_in=w_in, conv_a_w=conv_a_w, conv_a_b=conv_a_b,
             ln_a_g=ln_a_g, ln_a_b=ln_a_b, ln_v_g=ln_v_g, ln_v_b=ln_v_b, w_s=w_s, b_s=b_s,
             conv_c_w=conv_c_w, w_pool=w_pool, pool_scale=pool_scale, w_out=w_out,
             ln_mix_g=ln_mix_g, ln_mix_b=ln_mix_b,
             w_in_bf=w_in.astype(BF16), w_out_bf=w_out.astype(BF16))
    depth = w_in.shape[0]
    bsz, seq, d = x_prompt.shape
    n_seq, n_step, _ = x_sample.shape
    dq = conv_a_w.shape[2]
    alpha = float((2.0 * depth) ** 0.25)

    hp = x_prompt
    hs = x_sample.transpose(1, 0, 2).reshape(n_step * n_seq, d)
    outs = {k: [] for k in ('pa', 'pc', 'pd', 'pv', 'sc', 'sv')}
    tm_all = lambda s: s.transpose(0, 2, 1, 3).reshape(depth, -1, dq)
    sa_all, sd_all = tm_all(state_conv_a), tm_all(state_pool_d)
    new_sa_sd = None
    for i in range(depth):
        lw = _prep_layer(i, p, n_step)
        j = i // 2
        routed = i % 2 == 1
        if routed:
            wr = jnp.pad(w_router[j], ((0, 0), (0, LANES - N_EXPERTS)))
            wr_hi = wr.astype(BF16)
            wr_lo = (wr - wr_hi.astype(F32)).astype(BF16)
            wr_cat = jnp.concatenate([wr_hi, wr_lo], axis=1)
        hp, pa, pc, pd, pv, *route_p = _mixer_prompt(hp, lw, i == 0, alpha, wr_cat if routed else None)
        tm_state = lambda s: s.transpose(1, 0, 2).reshape(-1, dq)
        hs, sa, sc, sd, sv = _mixer_sample(hs, sa_all, tm_state(state_conv_c[i]), sd_all, new_sa_sd, i, lw,
                                           i == 0, alpha, n_seq, n_step)
        new_sa_sd = (sa, sd)
        lg, lb = ln_ffn_g[i].reshape(1, -1), ln_ffn_b[i].reshape(1, -1)
        if not routed:
            hp, hs = _ffn_dense(hp.reshape(bsz * seq, d), hs, w_ff_gate[j], w_ff_up[j], w_ff_down[j], lg, lb,
                                alpha, TM_DENSE, TF_DENSE)
            hp = hp.reshape(bsz, seq, d)
        else:
            mw = dict(w_router_cat=wr_cat, w_e_gate=w_e_gate[j], w_e_up=w_e_up[j], w_e_down=w_e_down[j])
            hp, hs = _ffn_routed([hp.reshape(bsz * seq, d), hs], route_p, mw, lg, lb, alpha, TM_EXPERT)
            hp = hp.reshape(bsz, seq, d)
        back = lambda a: a.reshape(-1, n_seq, dq).transpose(1, 0, 2)
        for k, val in zip(('pa', 'pc', 'pd', 'pv', 'sc', 'sv'), (pa, pc, pd, pv, back(sc), back(sv))):
            outs[k].append(val)
    y_sample = hs.reshape(n_step, n_seq, d).transpose(1, 0, 2)
    st = lambda k: jnp.stack(outs[k])
    back_all = lambda a: a.reshape(depth, -1, n_seq, dq).transpose(0, 2, 1, 3)
    return (hp, y_sample, st('pa'), st('pc'), st('pd'), st('pv'), back_all(new_sa_sd[0]), st('sc'),
            back_all(new_sa_sd[1]), st('sv'))
```

```python
import functools

import numpy as np
import jax
import jax.numpy as jnp
from jax import lax
from jax.experimental import pallas as pl
from jax.experimental.pallas import tpu as pltpu

F32 = jnp.float32
BF16 = jnp.bfloat16
I32 = jnp.int32

PAST_LEN = 16384
CHUNK = 128
N_HEADS_B = 4
POOL_WINDOWS = (2, 4, 8, 16)
POOL_PAST = 15
CONV_A_W = 31
CONV_C_W = 3
N_EXPERTS = 8
TOP_K = 2
LN_EPS = 1e-5
SQRT_HALF = float(np.sqrt(0.5).astype(np.float32))

LANES = 128
SUBLANES = 8
VMEM_LIMIT_BYTES = 56 * 1024 * 1024

HIST_A = 32
HIST_C = 8
HIST_D = 24

TL_PROMPT = 512
TM_DENSE = 512
TF_DENSE = 1792
TF_EXPERT = 1792
TM_ROUTE = 512
TM_MOVE = 256
TM_EXPERT = 512
N_STAGE = 3
N_WSTAGE = 5


def _ln(x, g, b):
    mu = jnp.mean(x, axis=-1, keepdims=True)
    xc = x - mu
    var = jnp.mean(xc * xc, axis=-1, keepdims=True)
    return xc * lax.rsqrt(var + LN_EPS) * g + b


def _gelu(x):
    return 0.5 * x * (1.0 + lax.erf(x * SQRT_HALF))


def _silu(x):
    return x * jax.nn.sigmoid(x)


def _dot(a, b):
    return jnp.dot(a, b, preferred_element_type=F32)


def _cparams(sem):
    return pltpu.CompilerParams(dimension_semantics=sem, vmem_limit_bytes=VMEM_LIMIT_BYTES)


def _mixer_prompt_kernel(first_layer, route, alpha, tl, d_mix,
                         x_ref, lng_ref, lnb_ref, win_ref, caw_ref, cab_ref, lag_ref, lab_ref,
                         lvg_ref, lvb_ref, ws_ref, bsm_ref, ccw_ref, wpool_ref, pscale_ref,
                         wout_ref, lmg_ref, lmb_ref, *refs):
    if route:
        wcat_ref, cin_ref = refs[:2]
        refs = refs[2:]
    h_out, sa_out, sc_out, sd_out, v_out = refs[:5]
    refs = refs[5:]
    if route:
        mi_ref, mf_ref, cnt_ref = refs[:3]
        refs = refs[3:]
    ext_a, ext_c, ext_d, cat_ref, z_ref, hb_ref, mix_ref = refs[:7]
    dq = d_mix // 4
    j = pl.program_id(1)
    nj = pl.num_programs(1)
    if route:
        carry_ref, before_ref = refs[7:]

        @pl.when((pl.program_id(0) == 0) & (j == 0))
        def _():
            _route_init(tl, cin_ref, carry_ref, before_ref)

    @pl.when(j == 0)
    def _():
        ext_a[0, 0:HIST_A, :] = jnp.zeros((HIST_A, dq), F32)
        ext_c[0:HIST_C, :] = jnp.zeros((HIST_C, dq), F32)
        ext_d[0:HIST_D, :] = jnp.zeros((HIST_D, dq), F32)

    d = x_ref.shape[2]
    rb = CHUNK

    def resid(r0, n):
        return h_out[0, r0:r0 + n, :] if first_layer else x_ref[0, r0:r0 + n, :]

    for r0 in range(0, tl, rb):
        x = x_ref[0, r0:r0 + rb, :]
        if first_layer:
            x = _ln(x, lng_ref[...], lnb_ref[...])
            h_out[0, r0:r0 + rb, :] = x
        hb_ref[r0:r0 + rb, :] = x.astype(BF16)

    z_ref[...] = _dot(hb_ref[...], win_ref[...])

    def zcol(r0, i, n=rb):
        return z_ref[r0:r0 + n, i * dq:(i + 1) * dq]

    chunk_rows = range(0, tl, CHUNK)

    for r0 in chunk_rows:
        ext_a[0, HIST_A + r0:HIST_A + r0 + rb, :] = zcol(r0, 0) * jax.nn.sigmoid(zcol(r0, 1))
    n_ext = HIST_A + tl
    for r0 in range(0, n_ext - SUBLANES, rb):
        n = min(rb, n_ext - SUBLANES - r0)
        blk = ext_a[0, r0:r0 + n + SUBLANES, :]
        for s in range(1, SUBLANES):
            ext_a[s, r0:r0 + n, :] = pltpu.roll(blk, n + SUBLANES - s, axis=0)[0:n, :]
    off_a = HIST_A - (CONV_A_W - 1)
    for q0 in range(0, tl, 64):
        acc = jnp.zeros((64, dq), F32)
        for k in range(CONV_A_W):
            s = (off_a + k) % SUBLANES
            row = off_a + k - s + q0
            acc = acc + ext_a[s, row:row + 64, :] * caw_ref[k:k + 1, :]
        y = _silu(_ln(acc + cab_ref[...], lag_ref[...], lab_ref[...]))
        cat_ref[q0:q0 + 64, 0:dq] = y.astype(BF16)

    hd = dq // N_HEADS_B
    lane_head = lax.broadcasted_iota(I32, (CHUNK, dq), 1) // hd
    for r0 in chunk_rows:
        v = _ln(_gelu(zcol(r0, 3)), lvg_ref[...], lvb_ref[...])
        sm = bsm_ref[...]
        for hh in range(N_HEADS_B):
            vm = jnp.where(lane_head == hh, v, 0.0).astype(BF16)
            sm = sm + _dot(ws_ref[hh], vm)
        cat_ref[r0:r0 + CHUNK, dq:2 * dq] = (_gelu(zcol(r0, 2)) * sm).astype(BF16)
        if r0 == tl - CHUNK:
            @pl.when(j == nj - 1)
            def _(v=v):
                v_out[0] = v

    off_c = HIST_C - (CONV_C_W - 1)
    for r0 in chunk_rows:
        ext_c[HIST_C + r0:HIST_C + r0 + rb, :] = zcol(r0, 6) * zcol(r0, 4)
    for r0 in chunk_rows:
        conv_c = jnp.zeros((CHUNK, dq), F32)
        for k in range(CONV_C_W):
            conv_c = conv_c + ext_c[off_c + r0 + k:off_c + r0 + k + CHUNK, :] * ccw_ref[k:k + 1, :]
        cat_ref[r0:r0 + CHUNK, 2 * dq:3 * dq] = (zcol(r0, 5) * conv_c).astype(BF16)

    gc = dq // len(POOL_WINDOWS)
    lane_grp = lax.broadcasted_iota(I32, (CHUNK, dq), 1) // gc
    lead = 2 * SUBLANES
    for r0 in chunk_rows:
        ext_d[HIST_D + r0:HIST_D + r0 + rb, :] = zcol(r0, 7)
    for r0 in chunk_rows:
        e = ext_d[HIST_D + r0 - lead:HIST_D + r0 + CHUNK, :]
        s2 = e + pltpu.roll(e, 1, axis=0)
        s4 = s2 + pltpu.roll(s2, 2, axis=0)
        s8 = s4 + pltpu.roll(s4, 4, axis=0)
        s16 = s8 + pltpu.roll(s8, 8, axis=0)
        sums = (s2, s4, s8, s16)
        pos = (j * tl + r0 + lax.broadcasted_iota(I32, (CHUNK, 1), 0)).astype(F32)
        mean = jnp.zeros((CHUNK, dq), F32)
        for g, w in enumerate(POOL_WINDOWS):
            inv = 1.0 / jnp.minimum(pos + 1.0, float(w))
            mean = jnp.where(lane_grp == g, sums[g][lead:lead + CHUNK, :] * inv, mean)
        dd = (mean - e[lead:lead + CHUNK, :]).astype(BF16)
        cat_ref[r0:r0 + CHUNK, 3 * dq:4 * dq] = (_dot(dd, wpool_ref[...]) * pscale_ref[...]).astype(BF16)

    mix_ref[...] = _dot(cat_ref[...], wout_ref[...])
    for r0 in range(0, tl, 64):
        h_out[0, r0:r0 + 64, :] = _ln(alpha * resid(r0, 64) + mix_ref[r0:r0 + 64, :], lmg_ref[...], lmb_ref[...])
    if route:
        _route_block(tl, h_out[0], wcat_ref, mi_ref, mf_ref, cnt_ref, carry_ref, before_ref)

    @pl.when(j == nj - 1)
    def _():
        sa_out[0] = ext_a[0, HIST_A + tl - (CONV_A_W - 1):HIST_A + tl, :]
        sc_out[0] = ext_c[HIST_C + tl - (CONV_C_W - 1):HIST_C + tl, :]
        sd_out[0] = ext_d[HIST_D + tl - POOL_PAST:HIST_D + tl, :]

    ext_a[0, 0:HIST_A, :] = ext_a[0, tl:tl + HIST_A, :]
    ext_c[0:HIST_C, :] = ext_c[tl:tl + HIST_C, :]
    ext_d[0:HIST_D, :] = ext_d[tl:tl + HIST_D, :]


def _full_spec(arr):
    nd = arr.ndim
    return pl.BlockSpec(arr.shape, lambda *_: (0,) * nd)


def _layer_spec(arr, layer):
    nd = len(arr.shape) - 1
    return pl.BlockSpec((None,) + tuple(arr.shape[1:]), lambda *_: (layer,) + (0,) * nd)


def _param_spec(lw, arr):
    stacked = arr is lw['w_in'] or arr is lw['w_out']
    return _layer_spec(arr, lw['layer']) if stacked else _full_spec(arr)


def _mixer_prompt(x, lw, first_layer, alpha, route_w=None):
    bsz, seq, d = x.shape
    d_mix = lw['w_out'].shape[1]
    dq = d_mix // 4
    tl = TL_PROMPT
    nj = seq // tl
    route = route_w is not None
    assert seq % tl == 0 and tl % CHUNK == 0 and seq >= CHUNK
    params = [lw['ln_in_g'], lw['ln_in_b'], lw['w_in'], lw['conv_a_w'], lw['conv_a_b'], lw['ln_a_g'],
              lw['ln_a_b'], lw['ln_v_g'], lw['ln_v_b'], lw['ws_tril'], lw['bs_mat'], lw['conv_c_w'],
              lw['w_pool_bd'], lw['pool_scale'], lw['w_out'], lw['ln_mix_g'], lw['ln_mix_b']]
    out_shape = [
        jax.ShapeDtypeStruct((bsz, seq, d), F32),
        jax.ShapeDtypeStruct((bsz, CONV_A_W - 1, dq), F32),
        jax.ShapeDtypeStruct((bsz, CONV_C_W - 1, dq), F32),
        jax.ShapeDtypeStruct((bsz, POOL_PAST, dq), F32),
        jax.ShapeDtypeStruct((bsz, CHUNK, dq), F32),
    ]
    state_spec = lambda rows: pl.BlockSpec((1, rows, dq), lambda b, j: (b, 0, 0))
    out_specs = [pl.BlockSpec((1, tl, d), lambda b, j: (b, j, 0)),
                 state_spec(CONV_A_W - 1), state_spec(CONV_C_W - 1), state_spec(POOL_PAST), state_spec(CHUNK)]
    scratch = [pltpu.VMEM((SUBLANES, HIST_A + tl, dq), F32), pltpu.VMEM((HIST_C + tl, dq), F32),
               pltpu.VMEM((HIST_D + tl, dq), F32), pltpu.VMEM((tl, d_mix), BF16),
               pltpu.VMEM((tl, lw['w_in'].shape[2]), F32), pltpu.VMEM((tl, d), BF16),
               pltpu.VMEM((tl, d), F32)]
    if route:
        assert N_EXPERTS == SUBLANES
        params += [route_w, jnp.zeros((SUBLANES, LANES), I32)]
        out_shape += [jax.ShapeDtypeStruct((SUBLANES, bsz * seq), I32),
                      jax.ShapeDtypeStruct((bsz * seq, SUBLANES), F32),
                      jax.ShapeDtypeStruct((SUBLANES, LANES), I32)]
        out_specs += [pl.BlockSpec((SUBLANES, tl), lambda b, j: (0, b * nj + j)),
                      pl.BlockSpec((tl, SUBLANES), lambda b, j: (b * nj + j, 0)),
                      pl.BlockSpec((SUBLANES, LANES), lambda b, j: (0, 0))]
        scratch += [pltpu.VMEM((SUBLANES, LANES), F32), pltpu.VMEM((tl, tl), BF16)]
    return pl.pallas_call(
        functools.partial(_mixer_prompt_kernel, first_layer, route, alpha, tl, d_mix),
        grid=(bsz, nj),
        in_specs=[pl.BlockSpec((1, tl, d), lambda b, j: (b, j, 0))] + [_param_spec(lw, p) for p in params],
        out_specs=tuple(out_specs),
        out_shape=tuple(out_shape),
        scratch_shapes=scratch,
        compiler_params=_cparams(("arbitrary", "arbitrary")),
        name="mixer_prompt",
    )(x, *params)


def _mixer_sample_kernel(first_layer, alpha, n_seq, n_step, d_mix,
                         x_ref, sa_ref, sc_ref, sd_ref, lng_ref, lnb_ref, win_ref, caw_ref, cab_ref,
                         lag_ref, lab_ref, lvg_ref, lvb_ref, wsv_ref, bsv_ref, ccw_ref, wpool_ref,
                         pscale_ref, wout_ref, lmg_ref, lmb_ref, *rest):
    h_out, sa_out, sc_out, sd_out, v_out, cat_ref = rest[-6:]
    dq = d_mix // 4
    x = x_ref[...]
    h = _ln(x, lng_ref[...], lnb_ref[...]) if first_layer else x
    hb = h.astype(BF16)

    def proj(i):
        return _dot(hb, win_ref[:, i * dq:(i + 1) * dq])

    def slab(val, i):
        return val[i * n_seq:(i + 1) * n_seq, :]

    def ext_slabs(state_ref, n_past, cur):
        return ([state_ref[i * n_seq:(i + 1) * n_seq, :] for i in range(n_past)]
                + [slab(cur, i) for i in range(n_step)])

    def store_state(out_ref, slabs, n_keep):
        for i, sl in enumerate(slabs[len(slabs) - n_keep:]):
            out_ref[i * n_seq:(i + 1) * n_seq, :] = sl

    a_glu = proj(0) * jax.nn.sigmoid(proj(1))
    ea = ext_slabs(sa_ref, CONV_A_W - 1, a_glu)
    for l in range(n_step):
        acc = jnp.zeros((n_seq, dq), F32)
        for k in range(CONV_A_W):
            acc = acc + ea[l + k] * caw_ref[k:k + 1, :]
        y = _silu(_ln(acc + cab_ref[...], lag_ref[...], lab_ref[...]))
        cat_ref[l * n_seq:(l + 1) * n_seq, 0:dq] = y.astype(BF16)
    store_state(sa_out, ea, CONV_A_W - 1)

    u = _gelu(proj(2))
    v = _ln(_gelu(proj(3)), lvg_ref[...], lvb_ref[...])
    v_out[...] = v
    for l in range(n_step):
        s = jnp.zeros((n_seq, dq), F32) + bsv_ref[l:l + 1, :]
        for m in range(l + 1):
            s = s + slab(v, m) * wsv_ref[l * n_step + m:l * n_step + m + 1, :]
        cat_ref[l * n_seq:(l + 1) * n_seq, dq:2 * dq] = (slab(u, l) * s).astype(BF16)

    c_x = proj(4)
    c_b = proj(5)
    c_c = proj(6)
    gx = c_c * c_x
    ec = ext_slabs(sc_ref, CONV_C_W - 1, gx)
    for l in range(n_step):
        acc = jnp.zeros((n_seq, dq), F32)
        for k in range(CONV_C_W):
            acc = acc + ec[l + k] * ccw_ref[k:k + 1, :]
        cat_ref[l * n_seq:(l + 1) * n_seq, 2 * dq:3 * dq] = (slab(c_b, l) * acc).astype(BF16)
    store_state(sc_out, ec, CONV_C_W - 1)

    d_in = proj(7)
    ed = ext_slabs(sd_ref, POOL_PAST, d_in)
    memo = {}

    def wsum(i, w):
        if i < 0:
            return None
        if w == 1:
            return ed[i]
        if (i, w) not in memo:
            a, b = wsum(i, w // 2), wsum(i - w // 2, w // 2)
            memo[(i, w)] = a if b is None else a + b
        return memo[(i, w)]

    gc = dq // len(POOL_WINDOWS)
    lane_grp = lax.broadcasted_iota(I32, (n_seq, dq), 1) // gc
    for l in range(n_step):
        mean = jnp.zeros((n_seq, dq), F32)
        for g, w in enumerate(POOL_WINDOWS):
            count = min(PAST_LEN + l + 1, w)
            mean = jnp.where(lane_grp == g, wsum(POOL_PAST + l, w) * (1.0 / count), mean)
        dd = (mean - slab(d_in, l)).astype(BF16)
        cat_ref[l * n_seq:(l + 1) * n_seq, 3 * dq:4 * dq] = (
            _dot(dd, wpool_ref[...]) * pscale_ref[...]).astype(BF16)
    store_state(sd_out, ed, POOL_PAST)

    mix = _dot(cat_ref[...], wout_ref[...])
    h_out[...] = _ln(alpha * h + mix, lmg_ref[...], lmb_ref[...])


def _mixer_sample(x_tm, sa_all, sc, sd_all, prev, layer, lw, first_layer, alpha, n_seq, n_step):
    rows, d = x_tm.shape
    d_mix = lw['w_out'].shape[1]
    dq = d_mix // 4
    assert n_seq % SUBLANES == 0 and n_step <= CHUNK and PAST_LEN % CHUNK == 0
    ins = [x_tm, sa_all, sc, sd_all, lw['ln_in_g'], lw['ln_in_b'], lw['w_in'], lw['conv_a_w'], lw['conv_a_b'],
           lw['ln_a_g'], lw['ln_a_b'], lw['ln_v_g'], lw['ln_v_b'], lw['ws_vec'], lw['bs_vec'],
           lw['conv_c_w'], lw['w_pool_bd'], lw['pool_scale'], lw['w_out'], lw['ln_mix_g'], lw['ln_mix_b']]
    layer_spec = lambda a: _layer_spec(a, layer)
    in_specs = [layer_spec(a) if a is sa_all or a is sd_all else _param_spec(lw, a) for a in ins]
    out_shape = (
        jax.ShapeDtypeStruct((rows, d), F32),
        jax.ShapeDtypeStruct(sa_all.shape, F32),
        jax.ShapeDtypeStruct(((CONV_C_W - 1) * n_seq, dq), F32),
        jax.ShapeDtypeStruct(sd_all.shape, F32),
        jax.ShapeDtypeStruct((rows, dq), F32),
    )
    out_specs = tuple(layer_spec(s) if len(s.shape) == 3 else pl.BlockSpec(s.shape, lambda g: (0, 0))
                      for s in out_shape)
    aliases = {}
    if prev is not None:
        aliases = {len(ins): 1, len(ins) + 1: 3}
        ins = ins + list(prev)
        in_specs = in_specs + [pl.BlockSpec(memory_space=pl.ANY)] * 2
    return pl.pallas_call(
        functools.partial(_mixer_sample_kernel, first_layer, alpha, n_seq, n_step, d_mix),
        grid=(1,),
        in_specs=in_specs,
        out_specs=out_specs,
        out_shape=out_shape,
        input_output_aliases=aliases,
        scratch_shapes=[pltpu.VMEM((rows, d_mix), BF16)],
        compiler_params=_cparams(("arbitrary",)),
        name="mixer_sample",
    )(*ins)


def _stream_cast(streams):
    plans = []
    for pairs, stage, sems in streams:
        rows = stage.shape[1]
        plans.append(([(src, dst, r0) for src, dst in pairs for r0 in range(0, src.shape[0], rows)],
                      stage, sems))

    def copy(plan, c):
        chunks, stage, sems = plan
        src, _, r0 = chunks[c]
        slot = c % stage.shape[0]
        return pltpu.make_async_copy(src.at[pl.ds(r0, stage.shape[1]), :], stage.at[slot], sems.at[slot])

    for plan in plans:
        for c in range(min(plan[1].shape[0], len(plan[0]))):
            copy(plan, c).start()
    for c in range(max(len(plan[0]) for plan in plans)):
        for plan in plans:
            chunks, stage, _ = plan
            if c < len(chunks):
                _, dst, r0 = chunks[c]
                copy(plan, c).wait()
                dst[pl.ds(r0, stage.shape[1]), :] = stage[c % stage.shape[0]].astype(BF16)
                if c + stage.shape[0] < len(chunks):
                    copy(plan, c + stage.shape[0]).start()


def _ffn_dense_kernel(alpha, tf, n_main, x_ref, xe_ref, wg_hbm, wu_hbm, wd_hbm, g_ref, b_ref, o_ref, oe_ref,
                      wg_ref, wu_ref, wd_ref, stage_in, stage_out, sems):
    i = pl.program_id(0)

    @pl.when(i == 0)
    def _():
        _stream_cast([([(wg_hbm, wg_ref), (wu_hbm, wu_ref)], stage_in, sems.at[0]),
                      ([(wd_hbm, wd_ref)], stage_out, sems.at[1])])

    def run(xr, orf):
        x = xr[...]
        xb = x.astype(BF16)
        ffn = None
        for c0 in range(0, wg_ref.shape[1], tf):
            mid = _silu(_dot(xb, wg_ref[:, c0:c0 + tf])) * _dot(xb, wu_ref[:, c0:c0 + tf])
            part = _dot(mid.astype(BF16), wd_ref[c0:c0 + tf, :])
            ffn = part if ffn is None else ffn + part
        orf[...] = _ln(alpha * x + ffn, g_ref[...], b_ref[...])

    @pl.when(i < n_main)
    def _():
        run(x_ref, o_ref)

    @pl.when(i >= n_main)
    def _():
        run(xe_ref, oe_ref)


def _ffn_dense(x, x_extra, wg, wu, wd, g, b, alpha, tm, tf):
    t, d = x.shape
    te = x_extra.shape[0]
    dff = wg.shape[1]
    n_main, n_extra = t // tm, te // tm
    rows_in, rows_out = 128, 512
    assert t % tm == 0 and te % tm == 0 and dff % tf == 0 and d % rows_in == 0 and dff % rows_out == 0
    any_spec = pl.BlockSpec(memory_space=pl.ANY)
    main_blk = lambda i: (jnp.minimum(i, n_main - 1), 0)
    extra_blk = lambda i: (jnp.maximum(i - n_main, 0), 0)
    return pl.pallas_call(
        functools.partial(_ffn_dense_kernel, alpha, tf, n_main),
        grid=(n_main + n_extra,),
        in_specs=[pl.BlockSpec((tm, d), main_blk), pl.BlockSpec((tm, d), extra_blk),
                  any_spec, any_spec, any_spec,
                  pl.BlockSpec((1, d), lambda i: (0, 0)),
                  pl.BlockSpec((1, d), lambda i: (0, 0))],
        out_specs=(pl.BlockSpec((tm, d), main_blk), pl.BlockSpec((tm, d), extra_blk)),
        out_shape=(jax.ShapeDtypeStruct((t, d), F32), jax.ShapeDtypeStruct((te, d), F32)),
        scratch_shapes=[pltpu.VMEM(wg.shape, BF16), pltpu.VMEM(wu.shape, BF16), pltpu.VMEM(wd.shape, BF16),
                        pltpu.VMEM((3, rows_in, dff), F32), pltpu.VMEM((3, rows_out, d), F32),
                        pltpu.SemaphoreType.DMA((2, 3))],
        compiler_params=_cparams(("arbitrary",)),
        name="ffn_dense",
    )(x, x_extra, wg, wu, wd, g, b)


def _route_init(tm, cin_ref, carry_ref, before_ref):
    carry_ref[...] = cin_ref[...].astype(F32)
    row = lax.broadcasted_iota(I32, (tm, tm), 0)
    col = lax.broadcasted_iota(I32, (tm, tm), 1)
    before_ref[...] = jnp.where(row < col, 1.0, 0.0).astype(BF16)


def _route_block(tm, h, wcat_ref, mi_ref, mf_ref, cnt_ref, carry_ref, before_ref):
    h_hi = h.astype(BF16)
    h_lo = (h - h_hi.astype(F32)).astype(BF16)
    p_hi = _dot(h_hi, wcat_ref[...])
    p_lo = _dot(h_lo, wcat_ref[...])
    logits = (p_hi[:, 0:LANES] + (p_lo[:, 0:LANES] + p_hi[:, LANES:2 * LANES])) + p_lo[:, LANES:2 * LANES]
    lg = logits.T[0:N_EXPERTS, :]
    ex = lax.broadcasted_iota(I32, (N_EXPERTS, tm), 0).astype(F32)
    m1 = jnp.max(lg, axis=0, keepdims=True)
    i1 = jnp.min(jnp.where(lg == m1, ex, float(N_EXPERTS)), axis=0, keepdims=True)
    sel1 = ex == i1
    rest = jnp.where(sel1, -jnp.inf, lg)
    m2 = jnp.max(rest, axis=0, keepdims=True)
    i2 = jnp.min(jnp.where(rest == m2, ex, float(N_EXPERTS)), axis=0, keepdims=True)
    sel2 = ex == i2
    e2 = jnp.exp(m2 - m1)
    den = 1.0 + e2
    g1 = 1.0 / den
    g2 = e2 / den

    sel = jnp.where(sel1 | sel2, 1.0, 0.0)
    sel_pad = jnp.concatenate([sel, jnp.zeros_like(sel)], axis=0).astype(BF16)
    base = _dot(sel_pad, before_ref[...])[0:N_EXPERTS, :] + carry_ref[:, 0:1]
    r1 = jnp.sum(jnp.where(sel1, base, 0.0), axis=0, keepdims=True)
    r2 = jnp.sum(jnp.where(sel2, base, 0.0), axis=0, keepdims=True)
    carry_ref[...] = carry_ref[...] + jnp.sum(sel, axis=1, keepdims=True)

    meta = jnp.where(ex == 0, i1, jnp.where(ex == 1, i2, jnp.where(ex == 2, r1, r2)))
    mi_ref[...] = meta.astype(I32)
    gt = jnp.where(ex == 0, g1, jnp.where(ex == 1, g2, 0.0))
    gt = jnp.concatenate([gt, jnp.zeros((LANES - N_EXPERTS, tm), F32)], axis=0)
    mf_ref[...] = gt.T[:, 0:SUBLANES]
    cnt_ref[...] = carry_ref[...].astype(I32)


def _router_kernel(tm, h_ref, wcat_ref, cin_ref, mi_ref, mf_ref, cnt_ref, carry_ref, before_ref):
    @pl.when(pl.program_id(0) == 0)
    def _():
        _route_init(tm, cin_ref, carry_ref, before_ref)

    _route_block(tm, h_ref[...], wcat_ref, mi_ref, mf_ref, cnt_ref, carry_ref, before_ref)


def _router(h, wr_cat, counts_in):
    t, d = h.shape
    tm = min(TM_ROUTE, t)
    assert t % tm == 0 and N_EXPERTS == SUBLANES
    return pl.pallas_call(
        functools.partial(_router_kernel, tm),
        grid=(t // tm,),
        in_specs=[pl.BlockSpec((tm, d), lambda i: (i, 0)), _full_spec(wr_cat), _full_spec(counts_in)],
        out_specs=(pl.BlockSpec((SUBLANES, tm), lambda i: (0, i)),
                   pl.BlockSpec((tm, SUBLANES), lambda i: (i, 0)),
                   pl.BlockSpec((SUBLANES, LANES), lambda i: (0, 0))),
        out_shape=(jax.ShapeDtypeStruct((SUBLANES, t), I32),
                   jax.ShapeDtypeStruct((t, SUBLANES), F32),
                   jax.ShapeDtypeStruct((SUBLANES, LANES), I32)),
        scratch_shapes=[pltpu.VMEM((SUBLANES, LANES), F32), pltpu.VMEM((tm, tm), BF16)],
        compiler_params=_cparams(("arbitrary",)),
        name="router",
    )(h, wr_cat, counts_in)


def _row_copy(src, src_row, dst, dst_row, sem):
    return pltpu.make_async_copy(src.at[pl.ds(src_row, 1), :], dst.at[pl.ds(dst_row, 1), :], sem)


def _rows_wait(src, dst, n_rows, sem):
    pltpu.make_async_copy(src.at[pl.ds(0, n_rows), :], dst.at[pl.ds(0, n_rows), :], sem).wait()


def _dispatch_kernel(tm, tm_expert, n_blocks, blk_ranges, zi_ref, dest_ref, *rest):
    n_src = len(blk_ranges)
    h_refs = rest[:n_src]
    xs_ref, xbuf, zblk, lsems, rsems, zsem = rest[n_src:]
    i = pl.program_id(0)
    n = pl.num_programs(0)

    def block_load(g, start):
        slot = lax.rem(g, N_STAGE)
        for h_ref, (b0, b1) in zip(h_refs, blk_ranges):
            @pl.when((g >= b0) & (g < b1))
            def _(h_ref=h_ref, b0=b0):
                cp = pltpu.make_async_copy(h_ref.at[pl.ds((g - b0) * tm, tm), :], xbuf.at[slot],
                                           lsems.at[slot])
                if start:
                    cp.start()
                else:
                    cp.wait()

    def rows_wait(g):
        slot = lax.rem(g, N_STAGE)
        for _k in range(TOP_K):
            _rows_wait(xbuf.at[slot], xs_ref, tm, rsems.at[slot])

    @pl.when(i == 0)
    def _():
        block_load(i, True)

    @pl.when(i >= N_STAGE - 1)
    def _():
        rows_wait(i - (N_STAGE - 1))

    @pl.when(i + 1 < n)
    def _():
        block_load(i + 1, True)

    block_load(i, False)
    cur = lax.rem(i, N_STAGE)

    def issue(j, carry):
        r0 = pl.multiple_of(j * SUBLANES, SUBLANES)
        for u in range(SUBLANES):
            for k in range(TOP_K):
                dst = dest_ref[0, 0, j * SUBLANES + (k * tm + u)]
                _row_copy(xbuf.at[cur], r0 + u, xs_ref, dst, rsems.at[cur]).start(priority=k)
        return carry

    lax.fori_loop(0, tm // SUBLANES, issue, 0)

    @pl.when(i == n - 1)
    def _():
        for back in range(N_STAGE - 2, -1, -1):
            @pl.when(i >= back)
            def _(back=back):
                rows_wait(i - back)

    @pl.when(i == 0)
    def _():
        zblk[...] = jnp.zeros_like(zblk)
        for e in range(N_EXPERTS):
            lo = zi_ref[e] + zi_ref[N_EXPERTS + e]
            hi = zi_ref[e] + zi_ref[2 * N_EXPERTS + e]

            def zissue(r, carry):
                _row_copy(zblk, 0, xs_ref, r, zsem).start()
                return carry

            def zdrain(r, carry):
                _row_copy(zblk, 0, xs_ref, r, zsem).wait()
                return carry

            lax.fori_loop(lo, hi, zissue, 0)
            lax.fori_loop(lo, hi, zdrain, 0)

        def bcopy(b):
            return pltpu.make_async_copy(zblk, xs_ref.at[pl.ds(b * tm_expert, tm_expert), :], zsem)

        def bissue(b, carry):
            bcopy(b).start()
            return carry

        def bdrain(b, carry):
            bcopy(b).wait()
            return carry

        lax.fori_loop(zi_ref[3 * N_EXPERTS], n_blocks, bissue, 0)
        lax.fori_loop(zi_ref[3 * N_EXPERTS], n_blocks, bdrain, 0)


def _dispatch(hs_list, dest_blk, zinfo, n_blocks, tm_expert):
    d = hs_list[0].shape[1]
    nb = dest_blk.shape[0]
    tm = dest_blk.shape[2] // TOP_K
    blk_ranges, b0 = [], 0
    for h in hs_list:
        assert h.shape[0] % tm == 0 and h.shape[0] >= tm
        blk_ranges.append((b0, b0 + h.shape[0] // tm))
        b0 = blk_ranges[-1][1]
    assert b0 == nb
    any_spec = pl.BlockSpec(memory_space=pl.ANY)
    return pl.pallas_call(
        functools.partial(_dispatch_kernel, tm, tm_expert, n_blocks, tuple(blk_ranges)),
        grid_spec=pltpu.PrefetchScalarGridSpec(
            num_scalar_prefetch=1,
            grid=(nb,),
            in_specs=[pl.BlockSpec((1, 1, TOP_K * tm), lambda i, zi: (i, 0, 0), memory_space=pltpu.SMEM)]
            + [any_spec] * len(hs_list),
            out_specs=any_spec,
            scratch_shapes=[pltpu.VMEM((N_STAGE, tm, d), F32), pltpu.VMEM((tm_expert, d), F32),
                            pltpu.SemaphoreType.DMA((N_STAGE,)), pltpu.SemaphoreType.DMA((N_STAGE,)),
                            pltpu.SemaphoreType.DMA(())],
        ),
        out_shape=jax.ShapeDtypeStruct((n_blocks * tm_expert, d), F32),
        compiler_params=_cparams(("arbitrary",)),
        name="moe_dispatch",
    )(zinfo, dest_blk, *hs_list)


def _ffn_moe_kernel(tf, be_ref, nu_ref, x_ref, wg_hbm, wu_hbm, wd_hbm, y_ref,
                    wg_ref, wu_ref, wd_ref, stage_in, stage_out, sems):
    b = pl.program_id(0)
    nb = pl.num_programs(0)
    d, dff = wg_ref.shape
    n_slots = stage_in.shape[0]
    rows_in = stage_in.shape[1]
    rows_out = stage_out.shape[1]
    n_in, n_out = 2 * (d // rows_in), dff // rows_out
    e = be_ref[b]
    fresh = (b == 0) | (e != be_ref[jnp.maximum(b - 1, 0)])
    e_next = be_ref[jnp.minimum(b + 1, nb - 1)]

    def chunk_in(ex, n):
        src, dst = (wg_hbm, wg_ref) if n < n_in // 2 else (wu_hbm, wu_ref)
        rows = pl.ds((n % (n_in // 2)) * rows_in, rows_in)
        stage = stage_in.at[n % n_slots]
        return pltpu.make_async_copy(src.at[ex, rows, :], stage, sems.at[0, n % n_slots]), stage, dst.at[rows, :]

    def chunk_out(ex, n):
        rows = pl.ds(n * rows_out, rows_out)
        stage = stage_out.at[n % n_slots]
        return pltpu.make_async_copy(wd_hbm.at[ex, rows, :], stage, sems.at[1, n % n_slots]), stage, wd_ref.at[rows, :]

    def request_first(ex):
        for n in range(n_slots):
            chunk_in(ex, n)[0].start()
            chunk_out(ex, n)[0].start()

    @pl.when((b < nu_ref[0]) & fresh)
    def _():
        @pl.when(b == 0)
        def _():
            request_first(e)

        for n in range(n_in):
            for chunk, count in ((chunk_in, n_in), (chunk_out, n_out)):
                if n < count:
                    copy, staged, dst = chunk(e, n)
                    copy.wait()
                    dst[...] = staged[...].astype(BF16)
                    if n + n_slots < count:
                        chunk(e, n + n_slots)[0].start()

    @pl.when((b + 1 < nu_ref[0]) & (e_next != e))
    def _():
        request_first(e_next)

    tm = x_ref.shape[0]
    rows_used = nu_ref[1 + b]

    def compute(rows):
        xb = x_ref[0:rows, :].astype(BF16)
        ffn = None
        for c0 in range(0, dff, tf):
            mid = _silu(_dot(xb, wg_ref[:, c0:c0 + tf])) * _dot(xb, wu_ref[:, c0:c0 + tf])
            part = _dot(mid.astype(BF16), wd_ref[c0:c0 + tf, :])
            ffn = part if ffn is None else ffn + part
        y_ref[0:rows, :] = ffn
        if rows < tm:
            y_ref[rows:tm, :] = jnp.zeros((tm - rows, y_ref.shape[1]), F32)

    step = tm // 8
    for rows in range(step, tm + 1, step):
        fits = (rows_used <= rows) if rows < tm else True
        needs = (rows_used > rows - step) if rows > step else True
        @pl.when((b < nu_ref[0]) & fits & needs)
        def _(rows=rows):
            compute(rows)

    @pl.when(b >= nu_ref[0])
    def _():
        y_ref[...] = jnp.zeros_like(y_ref)


def _ffn_moe(xs, wg, wu, wd, block_e, n_used, tm, tf):
    cap, d = xs.shape
    dff = wg.shape[2]
    rows_in, rows_out = 128, 512
    assert cap % tm == 0 and dff % tf == 0 and d % rows_in == 0 and dff % rows_out == 0
    assert min(2 * (d // rows_in), dff // rows_out) >= N_WSTAGE
    any_spec = pl.BlockSpec(memory_space=pl.ANY)
    return pl.pallas_call(
        functools.partial(_ffn_moe_kernel, tf),
        grid_spec=pltpu.PrefetchScalarGridSpec(
            num_scalar_prefetch=2,
            grid=(cap // tm,),
            in_specs=[pl.BlockSpec((tm, d), lambda b, be, nu: (jnp.minimum(b, nu[0] - 1), 0)),
                      any_spec, any_spec, any_spec],
            out_specs=pl.BlockSpec((tm, d), lambda b, be, nu: (b, 0)),
            scratch_shapes=[pltpu.VMEM((d, dff), BF16), pltpu.VMEM((d, dff), BF16), pltpu.VMEM((dff, d), BF16),
                            pltpu.VMEM((N_WSTAGE, rows_in, dff), F32), pltpu.VMEM((N_WSTAGE, rows_out, d), F32),
                            pltpu.SemaphoreType.DMA((2, N_WSTAGE))],
        ),
        out_shape=jax.ShapeDtypeStruct((cap, d), F32),
        compiler_params=_cparams(("arbitrary",)),
        name="ffn_moe",
    )(block_e, n_used, xs, wg, wu, wd)


def _combine_kernel(alpha, tm, dcur_ref, dnext_ref, h_ref, gate_ref, g_ref, b_ref, ys_ref, o_ref, ybuf, sems):
    i = pl.program_id(0)
    n = pl.num_programs(0)

    def issue(dref, slot):
        def body(j, carry):
            r0 = pl.multiple_of(j * SUBLANES, SUBLANES)
            for u in range(SUBLANES):
                for k in range(TOP_K):
                    src = dref[0, 0, j * SUBLANES + (k * tm + u)]
                    _row_copy(ys_ref, src, ybuf.at[slot, k], r0 + u, sems.at[slot]).start(priority=k)
            return carry

        lax.fori_loop(0, tm // SUBLANES, body, 0)

    @pl.when(i == 0)
    def _():
        issue(dcur_ref, 0)

    @pl.when(i + 1 < n)
    def _():
        issue(dnext_ref, (i + 1) % 2)

    slot = i % 2
    for k in range(TOP_K):
        _rows_wait(ys_ref, ybuf.at[slot, k], tm, sems.at[slot])

    gates = gate_ref[...]
    ffn = gates[:, 0:1] * ybuf[slot, 0] + gates[:, 1:2] * ybuf[slot, 1]
    o_ref[...] = _ln(alpha * h_ref[...] + ffn, g_ref[...], b_ref[...])


def _combine(h, dest_blk, gates, ys, g, b, alpha):
    t, d = h.shape
    nb = dest_blk.shape[0]
    tm = dest_blk.shape[2] // TOP_K
    dest_spec = lambda imap: pl.BlockSpec((1, 1, TOP_K * tm), imap, memory_space=pltpu.SMEM)
    return pl.pallas_call(
        functools.partial(_combine_kernel, alpha, tm),
        grid=(nb,),
        in_specs=[dest_spec(lambda i: (i, 0, 0)),
                  dest_spec(lambda i: (jnp.minimum(i + 1, nb - 1), 0, 0)),
                  pl.BlockSpec((tm, d), lambda i: (i, 0)),
                  pl.BlockSpec((tm, SUBLANES), lambda i: (i, 0)),
                  pl.BlockSpec((1, d), lambda i: (0, 0)),
                  pl.BlockSpec((1, d), lambda i: (0, 0)),
                  pl.BlockSpec(memory_space=pl.ANY)],
        out_specs=pl.BlockSpec((tm, d), lambda i: (i, 0)),
        out_shape=jax.ShapeDtypeStruct((t, d), F32),
        scratch_shapes=[pltpu.VMEM((2, TOP_K, tm, d), F32), pltpu.SemaphoreType.DMA((2,))],
        compiler_params=_cparams(("arbitrary",)),
        name="moe_combine",
    )(dest_blk, dest_blk, h, gates, g, b, ys)


def _ffn_routed(hs_list, first_routed, mw, g, b, alpha, tm_expert):
    routed = [first_routed[:2]]
    counts_in = first_routed[2]
    for h in hs_list[1:]:
        meta_i, gates, counts_in = _router(h, mw['w_router_cat'], counts_in)
        routed.append((meta_i, gates))
    counts = counts_in[:N_EXPERTS, 0]
    padded = (counts + tm_expert - 1) // tm_expert * tm_expert
    pad_end = jnp.cumsum(padded)
    pad_start = (pad_end - padded).astype(I32)
    n_assign = sum(h.shape[0] for h in hs_list) * TOP_K
    n_blocks = -(-n_assign // tm_expert) + N_EXPERTS
    cap = n_blocks * tm_expert
    n_used = (pad_end[-1] // tm_expert).astype(I32)
    blk_start = jnp.minimum(jnp.arange(n_blocks, dtype=I32), n_used - 1) * tm_expert
    block_e = jnp.minimum(jnp.sum(blk_start[:, None] >= pad_end[None, :], axis=1), N_EXPERTS - 1).astype(I32)
    zinfo = jnp.concatenate([pad_start, counts, padded, n_used.reshape(1)]).astype(I32)

    meta_all = jnp.concatenate([meta_i for meta_i, _ in routed], axis=1)
    dest = meta_all[TOP_K:2 * TOP_K, :]
    for e in range(N_EXPERTS):
        dest = dest + jnp.where(meta_all[0:TOP_K, :] == e, pad_start[e], 0)
    nb_all = dest.shape[1] // TM_MOVE
    dest_all = dest.reshape(TOP_K, nb_all, TM_MOVE).transpose(1, 0, 2).reshape(nb_all, 1, TOP_K * TM_MOVE)
    dests, b0 = [], 0
    for h in hs_list:
        assert h.shape[0] % TM_MOVE == 0
        dests.append(dest_all[b0:b0 + h.shape[0] // TM_MOVE])
        b0 += h.shape[0] // TM_MOVE
    xs = _dispatch(hs_list, dest_all, zinfo, n_blocks, tm_expert)
    filled_end = jnp.sum(jnp.where(block_e[:, None] == jnp.arange(N_EXPERTS, dtype=I32)[None, :],
                                   (pad_start + counts)[None, :], 0), axis=1)
    occupancy = jnp.clip(filled_end - jnp.arange(n_blocks, dtype=I32) * tm_expert, 0, tm_expert)
    used_info = jnp.concatenate([n_used.reshape(1), occupancy]).astype(I32)
    ys = _ffn_moe(xs, mw['w_e_gate'], mw['w_e_up'], mw['w_e_down'], block_e, used_info, tm_expert,
                  TF_EXPERT)
    return [_combine(h, dest_blk, gates, ys, g, b, alpha)
            for h, dest_blk, (_, gates) in zip(hs_list, dests, routed)]


def _prep_layer(i, p, n_step):
    dq = p['conv_a_w'].shape[2]
    hd = dq // N_HEADS_B
    tril = jnp.tril(jnp.ones((CHUNK, CHUNK), dtype=bool))
    w_s = p['w_s'][i]
    b_s = p['b_s'][i]
    wp = p['w_pool'][i]
    ng, gc = wp.shape[0], wp.shape[1]
    w_pool_bd = (jnp.eye(ng, dtype=F32)[:, None, :, None] * wp[:, :, None, :]).reshape(ng * gc, ng * gc)
    row = lambda a: a.reshape(1, -1)
    return dict(
        ln_in_g=row(p['ln_in_g']), ln_in_b=row(p['ln_in_b']),
        layer=i, w_in=p['w_in_bf'],
        conv_a_w=p['conv_a_w'][i], conv_a_b=row(p['conv_a_b'][i]),
        ln_a_g=row(p['ln_a_g'][i]), ln_a_b=row(p['ln_a_b'][i]),
        ln_v_g=row(p['ln_v_g'][i]), ln_v_b=row(p['ln_v_b'][i]),
        ws_tril=jnp.where(tril[None], w_s, 0).astype(BF16),
        bs_mat=jnp.repeat(b_s.T, hd, axis=1),
        ws_vec=jnp.repeat(jnp.where(tril[None], w_s, 0)[:, :n_step, :n_step].transpose(1, 2, 0)
                          .reshape(n_step * n_step, N_HEADS_B), hd, axis=1),
        bs_vec=jnp.repeat(b_s[:, :n_step].T, hd, axis=1),
        conv_c_w=p['conv_c_w'][i],
        w_pool_bd=w_pool_bd.astype(BF16), pool_scale=row(p['pool_scale'][i]),
        w_out=p['w_out_bf'],
        ln_mix_g=row(p['ln_mix_g'][i]), ln_mix_b=row(p['ln_mix_b'][i]),
    )


def kernel(x_prompt, x_sample, state_conv_a, state_conv_c, state_pool_d, ln_in_g, ln_in_b, w_in, conv_a_w,
           conv_a_b, ln_a_g, ln_a_b, ln_v_g, ln_v_b, w_s, b_s, conv_c_w, w_pool, pool_scale, w_out,
           ln_mix_g, ln_mix_b, w_ff_gate, w_ff_up, w_ff_down, w_router, w_e_gate, w_e_up, w_e_down,
           ln_ffn_g, ln_ffn_b):
    p = dict(ln_in_g=ln_in_g, ln_in_b=ln_in_b, w_in=w_in, conv_a_w=conv_a_w, conv_a_b=conv_a_b,
             ln_a_g=ln_a_g, ln_a_b=ln_a_b, ln_v_g=ln_v_g, ln_v_b=ln_v_b, w_s=w_s, b_s=b_s,
             conv_c_w=conv_c_w, w_pool=w_pool, pool_scale=pool_scale, w_out=w_out,
             ln_mix_g=ln_mix_g, ln_mix_b=ln_mix_b,
             w_in_bf=w_in.astype(BF16), w_out_bf=w_out.astype(BF16))
    depth = w_in.shape[0]
    bsz, seq, d = x_prompt.shape
    n_seq, n_step, _ = x_sample.shape
    dq = conv_a_w.shape[2]
    alpha = float((2.0 * depth) ** 0.25)

    hp = x_prompt
    hs = x_sample.transpose(1, 0, 2).reshape(n_step * n_seq, d)
    outs = {k: [] for k in ('pa', 'pc', 'pd', 'pv', 'sc', 'sv')}
    tm_all = lambda s: s.transpose(0, 2, 1, 3).reshape(depth, -1, dq)
    sa_all, sd_all = tm_all(state_conv_a), tm_all(state_pool_d)
    new_sa_sd = None
    for i in range(depth):
        lw = _prep_layer(i, p, n_step)
        j = i // 2
        routed = i % 2 == 1
        if routed:
            wr = jnp.pad(w_router[j], ((0, 0), (0, LANES - N_EXPERTS)))
            wr_hi = wr.astype(BF16)
            wr_lo = (wr - wr_hi.astype(F32)).astype(BF16)
            wr_cat = jnp.concatenate([wr_hi, wr_lo], axis=1)
        hp, pa, pc, pd, pv, *route_p = _mixer_prompt(hp, lw, i == 0, alpha, wr_cat if routed else None)
        tm_state = lambda s: s.transpose(1, 0, 2).reshape(-1, dq)
        hs, sa, sc, sd, sv = _mixer_sample(hs, sa_all, tm_state(state_conv_c[i]), sd_all, new_sa_sd, i, lw,
                                           i == 0, alpha, n_seq, n_step)
        new_sa_sd = (sa, sd)
        lg, lb = ln_ffn_g[i].reshape(1, -1), ln_ffn_b[i].reshape(1, -1)
        if not routed:
            hp, hs = _ffn_dense(hp.reshape(bsz * seq, d), hs, w_ff_gate[j], w_ff_up[j], w_ff_down[j], lg, lb,
                                alpha, TM_DENSE, TF_DENSE)
            hp = hp.reshape(bsz, seq, d)
        else:
            mw = dict(w_router_cat=wr_cat, w_e_gate=w_e_gate[j], w_e_up=w_e_up[j], w_e_down=w_e_down[j])
            hp, hs = _ffn_routed([hp.reshape(bsz * seq, d), hs], route_p, mw, lg, lb, alpha, TM_EXPERT)
            hp = hp.reshape(bsz, seq, d)
        back = lambda a: a.reshape(-1, n_seq, dq).transpose(1, 0, 2)
        for k, val in zip(('pa', 'pc', 'pd', 'pv', 'sc', 'sv'), (pa, pc, pd, pv, back(sc), back(sv))):
            outs[k].append(val)
    y_sample = hs.reshape(n_step, n_seq, d).transpose(1, 0, 2)
    st = lambda k: jnp.stack(outs[k])
    back_all = lambda a: a.reshape(depth, -1, n_seq, dq).transpose(0, 2, 1, 3)
    return (hp, y_sample, st('pa'), st('pc'), st('pd'), st('pv'), back_all(new_sa_sd[0]), st('sc'),
            back_all(new_sa_sd[1]), st('sv'))
```

```python
import functools

import numpy as np
import jax
import jax.numpy as jnp
from jax import lax
from jax.experimental import pallas as pl
from jax.experimental.pallas import tpu as pltpu

F32 = jnp.float32
BF16 = jnp.bfloat16
I32 = jnp.int32

PAST_LEN = 16384
CHUNK = 128
N_HEADS_B = 4
POOL_WINDOWS = (2, 4, 8, 16)
POOL_PAST = 15
CONV_A_W = 31
CONV_C_W = 3
N_EXPERTS = 8
TOP_K = 2
LN_EPS = 1e-5
SQRT_HALF = float(np.sqrt(0.5).astype(np.float32))

LANES = 128
SUBLANES = 8
VMEM_LIMIT_BYTES = 56 * 1024 * 1024

HIST_A = 32
HIST_C = 8
HIST_D = 24

TL_PROMPT = 512
TM_DENSE = 512
TF_DENSE = 1792
TF_EXPERT = 1792
TM_ROUTE = 512
TM_MOVE = 256
TM_EXPERT = 512
N_STAGE = 3
N_WSTAGE = 5


def _ln(x, g, b):
    mu = jnp.mean(x, axis=-1, keepdims=True)
    xc = x - mu
    var = jnp.mean(xc * xc, axis=-1, keepdims=True)
    return xc * lax.rsqrt(var + LN_EPS) * g + b


def _gelu(x):
    return 0.5 * x * (1.0 + lax.erf(x * SQRT_HALF))


def _silu(x):
    return x * jax.nn.sigmoid(x)


def _dot(a, b):
    return jnp.dot(a, b, preferred_element_type=F32)


def _cparams(sem):
    return pltpu.CompilerParams(dimension_semantics=sem, vmem_limit_bytes=VMEM_LIMIT_BYTES)


def _mixer_prompt_kernel(first_layer, route, alpha, tl, d_mix,
                         x_ref, lng_ref, lnb_ref, win_ref, caw_ref, cab_ref, lag_ref, lab_ref,
                         lvg_ref, lvb_ref, ws_ref, bsm_ref, ccw_ref, wpool_ref, pscale_ref,
                         wout_ref, lmg_ref, lmb_ref, *refs):
    if route:
        wcat_ref, cin_ref = refs[:2]
        refs = refs[2:]
    h_out, sa_out, sc_out, sd_out, v_out = refs[:5]
    refs = refs[5:]
    if route:
        mi_ref, mf_ref, cnt_ref = refs[:3]
        refs = refs[3:]
    ext_a, ext_c, ext_d, cat_ref, z_ref, hb_ref, mix_ref = refs[:7]
    dq = d_mix // 4
    j = pl.program_id(1)
    nj = pl.num_programs(1)
    if route:
        carry_ref, before_ref = refs[7:]

        @pl.when((pl.program_id(0) == 0) & (j == 0))
        def _():
            _route_init(tl, cin_ref, carry_ref, before_ref)

    @pl.when(j == 0)
    def _():
        ext_a[0, 0:HIST_A, :] = jnp.zeros((HIST_A, dq), F32)
        ext_c[0:HIST_C, :] = jnp.zeros((HIST_C, dq), F32)
        ext_d[0:HIST_D, :] = jnp.zeros((HIST_D, dq), F32)

    d = x_ref.shape[2]
    rb = CHUNK

    def resid(r0, n):
        return h_out[0, r0:r0 + n, :] if first_layer else x_ref[0, r0:r0 + n, :]

    for r0 in range(0, tl, rb):
        x = x_ref[0, r0:r0 + rb, :]
        if first_layer:
            x = _ln(x, lng_ref[...], lnb_ref[...])
            h_out[0, r0:r0 + rb, :] = x
        hb_ref[r0:r0 + rb, :] = x.astype(BF16)

    z_ref[...] = _dot(hb_ref[...], win_ref[...])

    def zcol(r0, i, n=rb):
        return z_ref[r0:r0 + n, i * dq:(i + 1) * dq]

    chunk_rows = range(0, tl, CHUNK)

    for r0 in chunk_rows:
        ext_a[0, HIST_A + r0:HIST_A + r0 + rb, :] = zcol(r0, 0) * jax.nn.sigmoid(zcol(r0, 1))
    n_ext = HIST_A + tl
    for r0 in range(0, n_ext - SUBLANES, rb):
        n = min(rb, n_ext - SUBLANES - r0)
        blk = ext_a[0, r0:r0 + n + SUBLANES, :]
        for s in range(1, SUBLANES):
            ext_a[s, r0:r0 + n, :] = pltpu.roll(blk, n + SUBLANES - s, axis=0)[0:n, :]
    off_a = HIST_A - (CONV_A_W - 1)
    for q0 in range(0, tl, 64):
        acc = jnp.zeros((64, dq), F32)
        for k in range(CONV_A_W):
            s = (off_a + k) % SUBLANES
            row = off_a + k - s + q0
            acc = acc + ext_a[s, row:row + 64, :] * caw_ref[k:k + 1, :]
        y = _silu(_ln(acc + cab_ref[...], lag_ref[...], lab_ref[...]))
        cat_ref[q0:q0 + 64, 0:dq] = y.astype(BF16)

    hd = dq // N_HEADS_B
    lane_head = lax.broadcasted_iota(I32, (CHUNK, dq), 1) // hd
    for r0 in chunk_rows:
        v = _ln(_gelu(zcol(r0, 3)), lvg_ref[...], lvb_ref[...])
        sm = bsm_ref[...]
        for hh in range(N_HEADS_B):
            vm = jnp.where(lane_head == hh, v, 0.0).astype(BF16)
            sm = sm + _dot(ws_ref[hh], vm)
        cat_ref[r0:r0 + CHUNK, dq:2 * dq] = (_gelu(zcol(r0, 2)) * sm).astype(BF16)
        if r0 == tl - CHUNK:
            @pl.when(j == nj - 1)
            def _(v=v):
                v_out[0] = v

    off_c = HIST_C - (CONV_C_W - 1)
    for r0 in chunk_rows:
        ext_c[HIST_C + r0:HIST_C + r0 + rb, :] = zcol(r0, 6) * zcol(r0, 4)
    for r0 in chunk_rows:
        conv_c = jnp.zeros((CHUNK, dq), F32)
        for k in range(CONV_C_W):
            conv_c = conv_c + ext_c[off_c + r0 + k:off_c + r0 + k + CHUNK, :] * ccw_ref[k:k + 1, :]
        cat_ref[r0:r0 + CHUNK, 2 * dq:3 * dq] = (zcol(r0, 5) * conv_c).astype(BF16)

    gc = dq // len(POOL_WINDOWS)
    lane_grp = lax.broadcasted_iota(I32, (CHUNK, dq), 1) // gc
    lead = 2 * SUBLANES
    for r0 in chunk_rows:
        ext_d[HIST_D + r0:HIST_D + r0 + rb, :] = zcol(r0, 7)
    for r0 in chunk_rows:
        e = ext_d[HIST_D + r0 - lead:HIST_D + r0 + CHUNK, :]
        s2 = e + pltpu.roll(e, 1, axis=0)
        s4 = s2 + pltpu.roll(s2, 2, axis=0)
        s8 = s4 + pltpu.roll(s4, 4, axis=0)
        s16 = s8 + pltpu.roll(s8, 8, axis=0)
        sums = (s2, s4, s8, s16)
        pos = (j * tl + r0 + lax.broadcasted_iota(I32, (CHUNK, 1), 0)).astype(F32)
        mean = jnp.zeros((CHUNK, dq), F32)
        for g, w in enumerate(POOL_WINDOWS):
            inv = 1.0 / jnp.minimum(pos + 1.0, float(w))
            mean = jnp.where(lane_grp == g, sums[g][lead:lead + CHUNK, :] * inv, mean)
        dd = (mean - e[lead:lead + CHUNK, :]).astype(BF16)
        cat_ref[r0:r0 + CHUNK, 3 * dq:4 * dq] = (_dot(dd, wpool_ref[...]) * pscale_ref[...]).astype(BF16)

    mix_ref[...] = _dot(cat_ref[...], wout_ref[...])
    for r0 in range(0, tl, 64):
        h_out[0, r0:r0 + 64, :] = _ln(alpha * resid(r0, 64) + mix_ref[r0:r0 + 64, :], lmg_ref[...], lmb_ref[...])
    if route:
        _route_block(tl, h_out[0], wcat_ref, mi_ref, mf_ref, cnt_ref, carry_ref, before_ref)

    @pl.when(j == nj - 1)
    def _():
        sa_out[0] = ext_a[0, HIST_A + tl - (CONV_A_W - 1):HIST_A + tl, :]
        sc_out[0] = ext_c[HIST_C + tl - (CONV_C_W - 1):HIST_C + tl, :]
        sd_out[0] = ext_d[HIST_D + tl - POOL_PAST:HIST_D + tl, :]

    ext_a[0, 0:HIST_A, :] = ext_a[0, tl:tl + HIST_A, :]
    ext_c[0:HIST_C, :] = ext_c[tl:tl + HIST_C, :]
    ext_d[0:HIST_D, :] = ext_d[tl:tl + HIST_D, :]


def _full_spec(arr):
    nd = arr.ndim
    return pl.BlockSpec(arr.shape, lambda *_: (0,) * nd)


def _layer_spec(arr, layer):
    nd = len(arr.shape) - 1
    return pl.BlockSpec((None,) + tuple(arr.shape[1:]), lambda *_: (layer,) + (0,) * nd)


def _param_spec(lw, arr):
    stacked = any(arr is lw[k] for k in lw['stacked'])
    return _layer_spec(arr, lw['layer']) if stacked else _full_spec(arr)


def _mixer_prompt(x, lw, first_layer, alpha, route_w=None):
    bsz, seq, d = x.shape
    d_mix = lw['w_out'].shape[1]
    dq = d_mix // 4
    tl = TL_PROMPT
    nj = seq // tl
    route = route_w is not None
    assert seq % tl == 0 and tl % CHUNK == 0 and seq >= CHUNK
    params = [lw['ln_in_g'], lw['ln_in_b'], lw['w_in'], lw['conv_a_w'], lw['conv_a_b'], lw['ln_a_g'],
              lw['ln_a_b'], lw['ln_v_g'], lw['ln_v_b'], lw['ws_tril'], lw['bs_mat'], lw['conv_c_w'],
              lw['w_pool_bd'], lw['pool_scale'], lw['w_out'], lw['ln_mix_g'], lw['ln_mix_b']]
    out_shape = [
        jax.ShapeDtypeStruct((bsz, seq, d), F32),
        jax.ShapeDtypeStruct((bsz, CONV_A_W - 1, dq), F32),
        jax.ShapeDtypeStruct((bsz, CONV_C_W - 1, dq), F32),
        jax.ShapeDtypeStruct((bsz, POOL_PAST, dq), F32),
        jax.ShapeDtypeStruct((bsz, CHUNK, dq), F32),
    ]
    state_spec = lambda rows: pl.BlockSpec((1, rows, dq), lambda b, j: (b, 0, 0))
    out_specs = [pl.BlockSpec((1, tl, d), lambda b, j: (b, j, 0)),
                 state_spec(CONV_A_W - 1), state_spec(CONV_C_W - 1), state_spec(POOL_PAST), state_spec(CHUNK)]
    scratch = [pltpu.VMEM((SUBLANES, HIST_A + tl, dq), F32), pltpu.VMEM((HIST_C + tl, dq), F32),
               pltpu.VMEM((HIST_D + tl, dq), F32), pltpu.VMEM((tl, d_mix), BF16),
               pltpu.VMEM((tl, lw['w_in'].shape[2]), F32), pltpu.VMEM((tl, d), BF16),
               pltpu.VMEM((tl, d), F32)]
    if route:
        assert N_EXPERTS == SUBLANES
        params += [route_w, jnp.zeros((SUBLANES, LANES), I32)]
        out_shape += [jax.ShapeDtypeStruct((SUBLANES, bsz * seq), I32),
                      jax.ShapeDtypeStruct((bsz * seq, SUBLANES), F32),
                      jax.ShapeDtypeStruct((SUBLANES, LANES), I32)]
        out_specs += [pl.BlockSpec((SUBLANES, tl), lambda b, j: (0, b * nj + j)),
                      pl.BlockSpec((tl, SUBLANES), lambda b, j: (b * nj + j, 0)),
                      pl.BlockSpec((SUBLANES, LANES), lambda b, j: (0, 0))]
        scratch += [pltpu.VMEM((SUBLANES, LANES), F32), pltpu.VMEM((tl, tl), BF16)]
    return pl.pallas_call(
        functools.partial(_mixer_prompt_kernel, first_layer, route, alpha, tl, d_mix),
        grid=(bsz, nj),
        in_specs=[pl.BlockSpec((1, tl, d), lambda b, j: (b, j, 0))] + [_param_spec(lw, p) for p in params],
        out_specs=tuple(out_specs),
        out_shape=tuple(out_shape),
        scratch_shapes=scratch,
        compiler_params=_cparams(("arbitrary", "arbitrary")),
        name="mixer_prompt",
    )(x, *params)


def _mixer_sample_kernel(first_layer, alpha, n_seq, n_step, d_mix,
                         x_ref, sa_ref, sc_ref, sd_ref, lng_ref, lnb_ref, win_ref, caw_ref, cab_ref,
                         lag_ref, lab_ref, lvg_ref, lvb_ref, wsv_ref, bsv_ref, ccw_ref, wpool_ref,
                         pscale_ref, wout_ref, lmg_ref, lmb_ref, *rest):
    h_out, sa_out, sc_out, sd_out, v_out, cat_ref = rest[-6:]
    dq = d_mix // 4
    x = x_ref[...]
    h = _ln(x, lng_ref[...], lnb_ref[...]) if first_layer else x
    hb = h.astype(BF16)

    def proj(i):
        return _dot(hb, win_ref[:, i * dq:(i + 1) * dq])

    def slab(val, i):
        return val[i * n_seq:(i + 1) * n_seq, :]

    def ext_slabs(state_ref, n_past, cur):
        return ([state_ref[i * n_seq:(i + 1) * n_seq, :] for i in range(n_past)]
                + [slab(cur, i) for i in range(n_step)])

    def store_state(out_ref, slabs, n_keep):
        for i, sl in enumerate(slabs[len(slabs) - n_keep:]):
            out_ref[i * n_seq:(i + 1) * n_seq, :] = sl

    a_glu = proj(0) * jax.nn.sigmoid(proj(1))
    ea = ext_slabs(sa_ref, CONV_A_W - 1, a_glu)
    for l in range(n_step):
        acc = jnp.zeros((n_seq, dq), F32)
        for k in range(CONV_A_W):
            acc = acc + ea[l + k] * caw_ref[k:k + 1, :]
        y = _silu(_ln(acc + cab_ref[...], lag_ref[...], lab_ref[...]))
        cat_ref[l * n_seq:(l + 1) * n_seq, 0:dq] = y.astype(BF16)
    store_state(sa_out, ea, CONV_A_W - 1)

    u = _gelu(proj(2))
    v = _ln(_gelu(proj(3)), lvg_ref[...], lvb_ref[...])
    v_out[...] = v
    for l in range(n_step):
        s = jnp.zeros((n_seq, dq), F32) + bsv_ref[l:l + 1, :]
        for m in range(l + 1):
            s = s + slab(v, m) * wsv_ref[l * n_step + m:l * n_step + m + 1, :]
        cat_ref[l * n_seq:(l + 1) * n_seq, dq:2 * dq] = (slab(u, l) * s).astype(BF16)

    c_x = proj(4)
    c_b = proj(5)
    c_c = proj(6)
    gx = c_c * c_x
    ec = ext_slabs(sc_ref, CONV_C_W - 1, gx)
    for l in range(n_step):
        acc = jnp.zeros((n_seq, dq), F32)
        for k in range(CONV_C_W):
            acc = acc + ec[l + k] * ccw_ref[k:k + 1, :]
        cat_ref[l * n_seq:(l + 1) * n_seq, 2 * dq:3 * dq] = (slab(c_b, l) * acc).astype(BF16)
    store_state(sc_out, ec, CONV_C_W - 1)

    d_in = proj(7)
    ed = ext_slabs(sd_ref, POOL_PAST, d_in)
    memo = {}

    def wsum(i, w):
        if i < 0:
            return None
        if w == 1:
            return ed[i]
        if (i, w) not in memo:
            a, b = wsum(i, w // 2), wsum(i - w // 2, w // 2)
            memo[(i, w)] = a if b is None else a + b
        return memo[(i, w)]

    gc = dq // len(POOL_WINDOWS)
    lane_grp = lax.broadcasted_iota(I32, (n_seq, dq), 1) // gc
    for l in range(n_step):
        mean = jnp.zeros((n_seq, dq), F32)
        for g, w in enumerate(POOL_WINDOWS):
            count = min(PAST_LEN + l + 1, w)
            mean = jnp.where(lane_grp == g, wsum(POOL_PAST + l, w) * (1.0 / count), mean)
        dd = (mean - slab(d_in, l)).astype(BF16)
        cat_ref[l * n_seq:(l + 1) * n_seq, 3 * dq:4 * dq] = (
            _dot(dd, wpool_ref[...]) * pscale_ref[...]).astype(BF16)
    store_state(sd_out, ed, POOL_PAST)

    mix = _dot(cat_ref[...], wout_ref[...])
    h_out[...] = _ln(alpha * h + mix, lmg_ref[...], lmb_ref[...])


def _mixer_sample(x_tm, sa_all, sc, sd_all, prev, layer, lw, first_layer, alpha, n_seq, n_step):
    rows, d = x_tm.shape
    d_mix = lw['w_out'].shape[1]
    dq = d_mix // 4
    assert n_seq % SUBLANES == 0 and n_step <= CHUNK and PAST_LEN % CHUNK == 0
    ins = [x_tm, sa_all, sc, sd_all, lw['ln_in_g'], lw['ln_in_b'], lw['w_in'], lw['conv_a_w'], lw['conv_a_b'],
           lw['ln_a_g'], lw['ln_a_b'], lw['ln_v_g'], lw['ln_v_b'], lw['ws_vec'], lw['bs_vec'],
           lw['conv_c_w'], lw['w_pool_bd'], lw['pool_scale'], lw['w_out'], lw['ln_mix_g'], lw['ln_mix_b']]
    layer_spec = lambda a: _layer_spec(a, layer)
    in_specs = [layer_spec(a) if a is sa_all or a is sd_all else _param_spec(lw, a) for a in ins]
    out_shape = (
        jax.ShapeDtypeStruct((rows, d), F32),
        jax.ShapeDtypeStruct(sa_all.shape, F32),
        jax.ShapeDtypeStruct(((CONV_C_W - 1) * n_seq, dq), F32),
        jax.ShapeDtypeStruct(sd_all.shape, F32),
        jax.ShapeDtypeStruct((rows, dq), F32),
    )
    out_specs = tuple(layer_spec(s) if len(s.shape) == 3 else pl.BlockSpec(s.shape, lambda g: (0, 0))
                      for s in out_shape)
    aliases = {}
    if prev is not None:
        aliases = {len(ins): 1, len(ins) + 1: 3}
        ins = ins + list(prev)
        in_specs = in_specs + [pl.BlockSpec(memory_space=pl.ANY)] * 2
    return pl.pallas_call(
        functools.partial(_mixer_sample_kernel, first_layer, alpha, n_seq, n_step, d_mix),
        grid=(1,),
        in_specs=in_specs,
        out_specs=out_specs,
        out_shape=out_shape,
        input_output_aliases=aliases,
        scratch_shapes=[pltpu.VMEM((rows, d_mix), BF16)],
        compiler_params=_cparams(("arbitrary",)),
        name="mixer_sample",
    )(*ins)


def _stream_cast(streams):
    plans = []
    for pairs, stage, sems in streams:
        rows = stage.shape[1]
        plans.append(([(src, dst, r0) for src, dst in pairs for r0 in range(0, src.shape[0], rows)],
                      stage, sems))

    def copy(plan, c):
        chunks, stage, sems = plan
        src, _, r0 = chunks[c]
        slot = c % stage.shape[0]
        return pltpu.make_async_copy(src.at[pl.ds(r0, stage.shape[1]), :], stage.at[slot], sems.at[slot])

    for plan in plans:
        for c in range(min(plan[1].shape[0], len(plan[0]))):
            copy(plan, c).start()
    for c in range(max(len(plan[0]) for plan in plans)):
        for plan in plans:
            chunks, stage, _ = plan
            if c < len(chunks):
                _, dst, r0 = chunks[c]
                copy(plan, c).wait()
                dst[pl.ds(r0, stage.shape[1]), :] = stage[c % stage.shape[0]].astype(BF16)
                if c + stage.shape[0] < len(chunks):
                    copy(plan, c + stage.shape[0]).start()


def _ffn_dense_kernel(alpha, tf, n_main, x_ref, xe_ref, wg_hbm, wu_hbm, wd_hbm, g_ref, b_ref, o_ref, oe_ref,
                      wg_ref, wu_ref, wd_ref, stage_in, stage_out, sems):
    i = pl.program_id(0)

    @pl.when(i == 0)
    def _():
        _stream_cast([([(wg_hbm, wg_ref), (wu_hbm, wu_ref)], stage_in, sems.at[0]),
                      ([(wd_hbm, wd_ref)], stage_out, sems.at[1])])

    def run(xr, orf):
        x = xr[...]
        xb = x.astype(BF16)
        ffn = None
        for c0 in range(0, wg_ref.shape[1], tf):
            mid = _silu(_dot(xb, wg_ref[:, c0:c0 + tf])) * _dot(xb, wu_ref[:, c0:c0 + tf])
            part = _dot(mid.astype(BF16), wd_ref[c0:c0 + tf, :])
            ffn = part if ffn is None else ffn + part
        orf[...] = _ln(alpha * x + ffn, g_ref[...], b_ref[...])

    @pl.when(i < n_main)
    def _():
        run(x_ref, o_ref)

    @pl.when(i >= n_main)
    def _():
        run(xe_ref, oe_ref)


def _ffn_dense(x, x_extra, wg, wu, wd, g, b, alpha, tm, tf):
    t, d = x.shape
    te = x_extra.shape[0]
    dff = wg.shape[1]
    n_main, n_extra = t // tm, te // tm
    rows_in, rows_out = 128, 512
    assert t % tm == 0 and te % tm == 0 and dff % tf == 0 and d % rows_in == 0 and dff % rows_out == 0
    any_spec = pl.BlockSpec(memory_space=pl.ANY)
    main_blk = lambda i: (jnp.minimum(i, n_main - 1), 0)
    extra_blk = lambda i: (jnp.maximum(i - n_main, 0), 0)
    return pl.pallas_call(
        functools.partial(_ffn_dense_kernel, alpha, tf, n_main),
        grid=(n_main + n_extra,),
        in_specs=[pl.BlockSpec((tm, d), main_blk), pl.BlockSpec((tm, d), extra_blk),
                  any_spec, any_spec, any_spec,
                  pl.BlockSpec((1, d), lambda i: (0, 0)),
                  pl.BlockSpec((1, d), lambda i: (0, 0))],
        out_specs=(pl.BlockSpec((tm, d), main_blk), pl.BlockSpec((tm, d), extra_blk)),
        out_shape=(jax.ShapeDtypeStruct((t, d), F32), jax.ShapeDtypeStruct((te, d), F32)),
        scratch_shapes=[pltpu.VMEM(wg.shape, BF16), pltpu.VMEM(wu.shape, BF16), pltpu.VMEM(wd.shape, BF16),
                        pltpu.VMEM((3, rows_in, dff), F32), pltpu.VMEM((3, rows_out, d), F32),
                        pltpu.SemaphoreType.DMA((2, 3))],
        compiler_params=_cparams(("arbitrary",)),
        name="ffn_dense",
    )(x, x_extra, wg, wu, wd, g, b)


def _route_init(tm, cin_ref, carry_ref, before_ref):
    carry_ref[...] = cin_ref[...].astype(F32)
    row = lax.broadcasted_iota(I32, (tm, tm), 0)
    col = lax.broadcasted_iota(I32, (tm, tm), 1)
    before_ref[...] = jnp.where(row < col, 1.0, 0.0).astype(BF16)


def _route_block(tm, h, wcat_ref, mi_ref, mf_ref, cnt_ref, carry_ref, before_ref):
    h_hi = h.astype(BF16)
    h_lo = (h - h_hi.astype(F32)).astype(BF16)
    p_hi = _dot(h_hi, wcat_ref[...])
    p_lo = _dot(h_lo, wcat_ref[...])
    logits = (p_hi[:, 0:LANES] + (p_lo[:, 0:LANES] + p_hi[:, LANES:2 * LANES])) + p_lo[:, LANES:2 * LANES]
    lg = logits.T[0:N_EXPERTS, :]
    ex = lax.broadcasted_iota(I32, (N_EXPERTS, tm), 0).astype(F32)
    m1 = jnp.max(lg, axis=0, keepdims=True)
    i1 = jnp.min(jnp.where(lg == m1, ex, float(N_EXPERTS)), axis=0, keepdims=True)
    sel1 = ex == i1
    rest = jnp.where(sel1, -jnp.inf, lg)
    m2 = jnp.max(rest, axis=0, keepdims=True)
    i2 = jnp.min(jnp.where(rest == m2, ex, float(N_EXPERTS)), axis=0, keepdims=True)
    sel2 = ex == i2
    e2 = jnp.exp(m2 - m1)
    den = 1.0 + e2
    g1 = 1.0 / den
    g2 = e2 / den

    sel = jnp.where(sel1 | sel2, 1.0, 0.0)
    sel_pad = jnp.concatenate([sel, jnp.zeros_like(sel)], axis=0).astype(BF16)
    base = _dot(sel_pad, before_ref[...])[0:N_EXPERTS, :] + carry_ref[:, 0:1]
    r1 = jnp.sum(jnp.where(sel1, base, 0.0), axis=0, keepdims=True)
    r2 = jnp.sum(jnp.where(sel2, base, 0.0), axis=0, keepdims=True)
    carry_ref[...] = carry_ref[...] + jnp.sum(sel, axis=1, keepdims=True)

    meta = jnp.where(ex == 0, i1, jnp.where(ex == 1, i2, jnp.where(ex == 2, r1, r2)))
    mi_ref[...] = meta.astype(I32)
    gt = jnp.where(ex == 0, g1, jnp.where(ex == 1, g2, 0.0))
    gt = jnp.concatenate([gt, jnp.zeros((LANES - N_EXPERTS, tm), F32)], axis=0)
    mf_ref[...] = gt.T[:, 0:SUBLANES]
    cnt_ref[...] = carry_ref[...].astype(I32)


def _router_kernel(tm, h_ref, wcat_ref, cin_ref, mi_ref, mf_ref, cnt_ref, carry_ref, before_ref):
    @pl.when(pl.program_id(0) == 0)
    def _():
        _route_init(tm, cin_ref, carry_ref, before_ref)

    _route_block(tm, h_ref[...], wcat_ref, mi_ref, mf_ref, cnt_ref, carry_ref, before_ref)


def _router(h, wr_cat, counts_in):
    t, d = h.shape
    tm = min(TM_ROUTE, t)
    assert t % tm == 0 and N_EXPERTS == SUBLANES
    return pl.pallas_call(
        functools.partial(_router_kernel, tm),
        grid=(t // tm,),
        in_specs=[pl.BlockSpec((tm, d), lambda i: (i, 0)), _full_spec(wr_cat), _full_spec(counts_in)],
        out_specs=(pl.BlockSpec((SUBLANES, tm), lambda i: (0, i)),
                   pl.BlockSpec((tm, SUBLANES), lambda i: (i, 0)),
                   pl.BlockSpec((SUBLANES, LANES), lambda i: (0, 0))),
        out_shape=(jax.ShapeDtypeStruct((SUBLANES, t), I32),
                   jax.ShapeDtypeStruct((t, SUBLANES), F32),
                   jax.ShapeDtypeStruct((SUBLANES, LANES), I32)),
        scratch_shapes=[pltpu.VMEM((SUBLANES, LANES), F32), pltpu.VMEM((tm, tm), BF16)],
        compiler_params=_cparams(("arbitrary",)),
        name="router",
    )(h, wr_cat, counts_in)


def _row_copy(src, src_row, dst, dst_row, sem):
    return pltpu.make_async_copy(src.at[pl.ds(src_row, 1), :], dst.at[pl.ds(dst_row, 1), :], sem)


def _rows_wait(src, dst, n_rows, sem):
    pltpu.make_async_copy(src.at[pl.ds(0, n_rows), :], dst.at[pl.ds(0, n_rows), :], sem).wait()


def _dispatch_kernel(tm, tm_expert, n_blocks, blk_ranges, zi_ref, dest_ref, *rest):
    n_src = len(blk_ranges)
    h_refs = rest[:n_src]
    xs_ref, xbuf, zblk, lsems, rsems, zsem = rest[n_src:]
    i = pl.program_id(0)
    n = pl.num_programs(0)

    def block_load(g, start):
        slot = lax.rem(g, N_STAGE)
        for h_ref, (b0, b1) in zip(h_refs, blk_ranges):
            @pl.when((g >= b0) & (g < b1))
            def _(h_ref=h_ref, b0=b0):
                cp = pltpu.make_async_copy(h_ref.at[pl.ds((g - b0) * tm, tm), :], xbuf.at[slot],
                                           lsems.at[slot])
                if start:
                    cp.start()
                else:
                    cp.wait()

    def rows_wait(g):
        slot = lax.rem(g, N_STAGE)
        for _k in range(TOP_K):
            _rows_wait(xbuf.at[slot], xs_ref, tm, rsems.at[slot])

    @pl.when(i == 0)
    def _():
        block_load(i, True)

    @pl.when(i >= N_STAGE - 1)
    def _():
        rows_wait(i - (N_STAGE - 1))

    @pl.when(i + 1 < n)
    def _():
        block_load(i + 1, True)

    block_load(i, False)
    cur = lax.rem(i, N_STAGE)

    def issue(j, carry):
        r0 = pl.multiple_of(j * SUBLANES, SUBLANES)
        for u in range(SUBLANES):
            for k in range(TOP_K):
                dst = dest_ref[0, 0, j * SUBLANES + (k * tm + u)]
                _row_copy(xbuf.at[cur], r0 + u, xs_ref, dst, rsems.at[cur]).start(priority=k)
        return carry

    lax.fori_loop(0, tm // SUBLANES, issue, 0)

    @pl.when(i == n - 1)
    def _():
        for back in range(N_STAGE - 2, -1, -1):
            @pl.when(i >= back)
            def _(back=back):
                rows_wait(i - back)

    @pl.when(i == 0)
    def _():
        zblk[...] = jnp.zeros_like(zblk)
        for e in range(N_EXPERTS):
            lo = zi_ref[e] + zi_ref[N_EXPERTS + e]
            hi = zi_ref[e] + zi_ref[2 * N_EXPERTS + e]

            def zissue(r, carry):
                _row_copy(zblk, 0, xs_ref, r, zsem).start()
                return carry

            def zdrain(r, carry):
                _row_copy(zblk, 0, xs_ref, r, zsem).wait()
                return carry

            lax.fori_loop(lo, hi, zissue, 0)
            lax.fori_loop(lo, hi, zdrain, 0)

        def bcopy(b):
            return pltpu.make_async_copy(zblk, xs_ref.at[pl.ds(b * tm_expert, tm_expert), :], zsem)

        def bissue(b, carry):
            bcopy(b).start()
            return carry

        def bdrain(b, carry):
            bcopy(b).wait()
            return carry

        lax.fori_loop(zi_ref[3 * N_EXPERTS], n_blocks, bissue, 0)
        lax.fori_loop(zi_ref[3 * N_EXPERTS], n_blocks, bdrain, 0)


def _dispatch(hs_list, dest_blk, zinfo, n_blocks, tm_expert):
    d = hs_list[0].shape[1]
    nb = dest_blk.shape[0]
    tm = dest_blk.shape[2] // TOP_K
    blk_ranges, b0 = [], 0
    for h in hs_list:
        assert h.shape[0] % tm == 0 and h.shape[0] >= tm
        blk_ranges.append((b0, b0 + h.shape[0] // tm))
        b0 = blk_ranges[-1][1]
    assert b0 == nb
    any_spec = pl.BlockSpec(memory_space=pl.ANY)
    return pl.pallas_call(
        functools.partial(_dispatch_kernel, tm, tm_expert, n_blocks, tuple(blk_ranges)),
        grid_spec=pltpu.PrefetchScalarGridSpec(
            num_scalar_prefetch=1,
            grid=(nb,),
            in_specs=[pl.BlockSpec((1, 1, TOP_K * tm), lambda i, zi: (i, 0, 0), memory_space=pltpu.SMEM)]
            + [any_spec] * len(hs_list),
            out_specs=any_spec,
            scratch_shapes=[pltpu.VMEM((N_STAGE, tm, d), F32), pltpu.VMEM((tm_expert, d), F32),
                            pltpu.SemaphoreType.DMA((N_STAGE,)), pltpu.SemaphoreType.DMA((N_STAGE,)),
                            pltpu.SemaphoreType.DMA(())],
        ),
        out_shape=jax.ShapeDtypeStruct((n_blocks * tm_expert, d), F32),
        compiler_params=_cparams(("arbitrary",)),
        name="moe_dispatch",
    )(zinfo, dest_blk, *hs_list)


def _ffn_moe_kernel(tf, be_ref, nu_ref, x_ref, wg_hbm, wu_hbm, wd_hbm, y_ref,
                    wg_ref, wu_ref, wd_ref, stage_in, stage_out, sems):
    b = pl.program_id(0)
    nb = pl.num_programs(0)
    d, dff = wg_ref.shape
    n_slots = stage_in.shape[0]
    rows_in = stage_in.shape[1]
    rows_out = stage_out.shape[1]
    n_in, n_out = 2 * (d // rows_in), dff // rows_out
    e = be_ref[b]
    fresh = (b == 0) | (e != be_ref[jnp.maximum(b - 1, 0)])
    e_next = be_ref[jnp.minimum(b + 1, nb - 1)]

    def chunk_in(ex, n):
        src, dst = (wg_hbm, wg_ref) if n < n_in // 2 else (wu_hbm, wu_ref)
        rows = pl.ds((n % (n_in // 2)) * rows_in, rows_in)
        stage = stage_in.at[n % n_slots]
        return pltpu.make_async_copy(src.at[ex, rows, :], stage, sems.at[0, n % n_slots]), stage, dst.at[rows, :]

    def chunk_out(ex, n):
        rows = pl.ds(n * rows_out, rows_out)
        stage = stage_out.at[n % n_slots]
        return pltpu.make_async_copy(wd_hbm.at[ex, rows, :], stage, sems.at[1, n % n_slots]), stage, wd_ref.at[rows, :]

    def request_first(ex):
        for n in range(n_slots):
            chunk_in(ex, n)[0].start()
            chunk_out(ex, n)[0].start()

    @pl.when((b < nu_ref[0]) & fresh)
    def _():
        @pl.when(b == 0)
        def _():
            request_first(e)

        for n in range(n_in):
            for chunk, count in ((chunk_in, n_in), (chunk_out, n_out)):
                if n < count:
                    copy, staged, dst = chunk(e, n)
                    copy.wait()
                    dst[...] = staged[...].astype(BF16)
                    if n + n_slots < count:
                        chunk(e, n + n_slots)[0].start()

    @pl.when((b + 1 < nu_ref[0]) & (e_next != e))
    def _():
        request_first(e_next)

    tm = x_ref.shape[0]
    rows_used = nu_ref[1 + b]

    def compute(rows):
        xb = x_ref[0:rows, :].astype(BF16)
        ffn = None
        for c0 in range(0, dff, tf):
            mid = _silu(_dot(xb, wg_ref[:, c0:c0 + tf])) * _dot(xb, wu_ref[:, c0:c0 + tf])
            part = _dot(mid.astype(BF16), wd_ref[c0:c0 + tf, :])
            ffn = part if ffn is None else ffn + part
        y_ref[0:rows, :] = ffn
        if rows < tm:
            y_ref[rows:tm, :] = jnp.zeros((tm - rows, y_ref.shape[1]), F32)

    step = tm // 4
    for rows in range(step, tm + 1, step):
        fits = (rows_used <= rows) if rows < tm else True
        needs = (rows_used > rows - step) if rows > step else True
        @pl.when((b < nu_ref[0]) & fits & needs)
        def _(rows=rows):
            compute(rows)

    @pl.when(b >= nu_ref[0])
    def _():
        y_ref[...] = jnp.zeros_like(y_ref)


def _ffn_moe(xs, wg, wu, wd, block_e, n_used, tm, tf):
    cap, d = xs.shape
    dff = wg.shape[2]
    rows_in, rows_out = 128, 512
    assert cap % tm == 0 and dff % tf == 0 and d % rows_in == 0 and dff % rows_out == 0
    assert min(2 * (d // rows_in), dff // rows_out) >= N_WSTAGE
    any_spec = pl.BlockSpec(memory_space=pl.ANY)
    return pl.pallas_call(
        functools.partial(_ffn_moe_kernel, tf),
        grid_spec=pltpu.PrefetchScalarGridSpec(
            num_scalar_prefetch=2,
            grid=(cap // tm,),
            in_specs=[pl.BlockSpec((tm, d), lambda b, be, nu: (jnp.minimum(b, nu[0] - 1), 0)),
                      any_spec, any_spec, any_spec],
            out_specs=pl.BlockSpec((tm, d), lambda b, be, nu: (b, 0)),
            scratch_shapes=[pltpu.VMEM((d, dff), BF16), pltpu.VMEM((d, dff), BF16), pltpu.VMEM((dff, d), BF16),
                            pltpu.VMEM((N_WSTAGE, rows_in, dff), F32), pltpu.VMEM((N_WSTAGE, rows_out, d), F32),
                            pltpu.SemaphoreType.DMA((2, N_WSTAGE))],
        ),
        out_shape=jax.ShapeDtypeStruct((cap, d), F32),
        compiler_params=_cparams(("arbitrary",)),
        name="ffn_moe",
    )(block_e, n_used, xs, wg, wu, wd)


def _combine_kernel(alpha, tm, dcur_ref, dnext_ref, h_ref, gate_ref, g_ref, b_ref, ys_ref, o_ref, ybuf, sems):
    i = pl.program_id(0)
    n = pl.num_programs(0)

    def issue(dref, slot):
        def body(j, carry):
            r0 = pl.multiple_of(j * SUBLANES, SUBLANES)
            for u in range(SUBLANES):
                for k in range(TOP_K):
                    src = dref[0, 0, j * SUBLANES + (k * tm + u)]
                    _row_copy(ys_ref, src, ybuf.at[slot, k], r0 + u, sems.at[slot]).start(priority=k)
            return carry

        lax.fori_loop(0, tm // SUBLANES, body, 0)

    @pl.when(i == 0)
    def _():
        issue(dcur_ref, 0)

    @pl.when(i + 1 < n)
    def _():
        issue(dnext_ref, (i + 1) % 2)

    slot = i % 2
    for k in range(TOP_K):
        _rows_wait(ys_ref, ybuf.at[slot, k], tm, sems.at[slot])

    gates = gate_ref[...]
    ffn = gates[:, 0:1] * ybuf[slot, 0] + gates[:, 1:2] * ybuf[slot, 1]
    o_ref[...] = _ln(alpha * h_ref[...] + ffn, g_ref[...], b_ref[...])


def _combine(h, dest_blk, gates, ys, g, b, alpha):
    t, d = h.shape
    nb = dest_blk.shape[0]
    tm = dest_blk.shape[2] // TOP_K
    dest_spec = lambda imap: pl.BlockSpec((1, 1, TOP_K * tm), imap, memory_space=pltpu.SMEM)
    return pl.pallas_call(
        functools.partial(_combine_kernel, alpha, tm),
        grid=(nb,),
        in_specs=[dest_spec(lambda i: (i, 0, 0)),
                  dest_spec(lambda i: (jnp.minimum(i + 1, nb - 1), 0, 0)),
                  pl.BlockSpec((tm, d), lambda i: (i, 0)),
                  pl.BlockSpec((tm, SUBLANES), lambda i: (i, 0)),
                  pl.BlockSpec((1, d), lambda i: (0, 0)),
                  pl.BlockSpec((1, d), lambda i: (0, 0)),
                  pl.BlockSpec(memory_space=pl.ANY)],
        out_specs=pl.BlockSpec((tm, d), lambda i: (i, 0)),
        out_shape=jax.ShapeDtypeStruct((t, d), F32),
        scratch_shapes=[pltpu.VMEM((2, TOP_K, tm, d), F32), pltpu.SemaphoreType.DMA((2,))],
        compiler_params=_cparams(("arbitrary",)),
        name="moe_combine",
    )(dest_blk, dest_blk, h, gates, g, b, ys)


def _ffn_routed(hs_list, first_routed, mw, g, b, alpha, tm_expert):
    routed = [first_routed[:2]]
    counts_in = first_routed[2]
    for h in hs_list[1:]:
        meta_i, gates, counts_in = _router(h, mw['w_router_cat'], counts_in)
        routed.append((meta_i, gates))
    counts = counts_in[:N_EXPERTS, 0]
    padded = (counts + tm_expert - 1) // tm_expert * tm_expert
    pad_end = jnp.cumsum(padded)
    pad_start = (pad_end - padded).astype(I32)
    n_assign = sum(h.shape[0] for h in hs_list) * TOP_K
    n_blocks = -(-n_assign // tm_expert) + N_EXPERTS
    cap = n_blocks * tm_expert
    n_used = (pad_end[-1] // tm_expert).astype(I32)
    blk_start = jnp.minimum(jnp.arange(n_blocks, dtype=I32), n_used - 1) * tm_expert
    block_e = jnp.minimum(jnp.sum(blk_start[:, None] >= pad_end[None, :], axis=1), N_EXPERTS - 1).astype(I32)
    zinfo = jnp.concatenate([pad_start, counts, padded, n_used.reshape(1)]).astype(I32)

    meta_all = jnp.concatenate([meta_i for meta_i, _ in routed], axis=1)
    dest = meta_all[TOP_K:2 * TOP_K, :]
    for e in range(N_EXPERTS):
        dest = dest + jnp.where(meta_all[0:TOP_K, :] == e, pad_start[e], 0)
    nb_all = dest.shape[1] // TM_MOVE
    dest_all = dest.reshape(TOP_K, nb_all, TM_MOVE).transpose(1, 0, 2).reshape(nb_all, 1, TOP_K * TM_MOVE)
    dests, b0 = [], 0
    for h in hs_list:
        assert h.shape[0] % TM_MOVE == 0
        dests.append(dest_all[b0:b0 + h.shape[0] // TM_MOVE])
        b0 += h.shape[0] // TM_MOVE
    xs = _dispatch(hs_list, dest_all, zinfo, n_blocks, tm_expert)
    filled_end = jnp.sum(jnp.where(block_e[:, None] == jnp.arange(N_EXPERTS, dtype=I32)[None, :],
                                   (pad_start + counts)[None, :], 0), axis=1)
    occupancy = jnp.clip(filled_end - jnp.arange(n_blocks, dtype=I32) * tm_expert, 0, tm_expert)
    used_info = jnp.concatenate([n_used.reshape(1), occupancy]).astype(I32)
    ys = _ffn_moe(xs, mw['w_e_gate'], mw['w_e_up'], mw['w_e_down'], block_e, used_info, tm_expert,
                  TF_EXPERT)
    return [_combine(h, dest_blk, gates, ys, g, b, alpha)
            for h, dest_blk, (_, gates) in zip(hs_list, dests, routed)]


def _prep_layer(i, p, n_step):
    dq = p['conv_a_w'].shape[2]
    hd = dq // N_HEADS_B
    tril = jnp.tril(jnp.ones((CHUNK, CHUNK), dtype=bool))
    w_s = p['w_s'][i]
    b_s = p['b_s'][i]
    wp = p['w_pool'][i]
    ng, gc = wp.shape[0], wp.shape[1]
    w_pool_bd = (jnp.eye(ng, dtype=F32)[:, None, :, None] * wp[:, :, None, :]).reshape(ng * gc, ng * gc)
    row = lambda a: a.reshape(1, -1)
    rows = lambda a: a.reshape(a.shape[0], 1, -1)
    return dict(
        ln_in_g=row(p['ln_in_g']), ln_in_b=row(p['ln_in_b']),
        layer=i, stacked=('w_in', 'w_out', 'conv_a_w', 'conv_c_w', 'conv_a_b', 'ln_a_g', 'ln_a_b', 'ln_v_g',
                          'ln_v_b', 'pool_scale', 'ln_mix_g', 'ln_mix_b'),
        w_in=p['w_in_bf'],
        conv_a_w=p['conv_a_w'], conv_a_b=rows(p['conv_a_b']),
        ln_a_g=rows(p['ln_a_g']), ln_a_b=rows(p['ln_a_b']),
        ln_v_g=rows(p['ln_v_g']), ln_v_b=rows(p['ln_v_b']),
        ws_tril=jnp.where(tril[None], w_s, 0).astype(BF16),
        bs_mat=jnp.repeat(b_s.T, hd, axis=1),
        ws_vec=jnp.repeat(jnp.where(tril[None], w_s, 0)[:, :n_step, :n_step].transpose(1, 2, 0)
                          .reshape(n_step * n_step, N_HEADS_B), hd, axis=1),
        bs_vec=jnp.repeat(b_s[:, :n_step].T, hd, axis=1),
        conv_c_w=p['conv_c_w'],
        w_pool_bd=w_pool_bd.astype(BF16), pool_scale=rows(p['pool_scale']),
        w_out=p['w_out_bf'],
        ln_mix_g=rows(p['ln_mix_g']), ln_mix_b=rows(p['ln_mix_b']),
    )


def kernel(x_prompt, x_sample, state_conv_a, state_conv_c, state_pool_d, ln_in_g, ln_in_b, w_in, conv_a_w,
           conv_a_b, ln_a_g, ln_a_b, ln_v_g, ln_v_b, w_s, b_s, conv_c_w, w_pool, pool_scale, w_out,
           ln_mix_g, ln_mix_b, w_ff_gate, w_ff_up, w_ff_down, w_router, w_e_gate, w_e_up, w_e_down,
           ln_ffn_g, ln_ffn_b):
    p = dict(ln_in_g=ln_in_g, ln_in_b=ln_in_b, w_in=w_in, conv_a_w=conv_a_w, conv_a_b=conv_a_b,
             ln_a_g=ln_a_g, ln_a_b=ln_a_b, ln_v_g=ln_v_g, ln_v_b=ln_v_b, w_s=w_s, b_s=b_s,
             conv_c_w=conv_c_w, w_pool=w_pool, pool_scale=pool_scale, w_out=w_out,
             ln_mix_g=ln_mix_g, ln_mix_b=ln_mix_b,
             w_in_bf=w_in.astype(BF16), w_out_bf=w_out.astype(BF16))
    depth = w_in.shape[0]
    bsz, seq, d = x_prompt.shape
    n_seq, n_step, _ = x_sample.shape
    dq = conv_a_w.shape[2]
    alpha = float((2.0 * depth) ** 0.25)

    hp = x_prompt
    hs = x_sample.transpose(1, 0, 2).reshape(n_step * n_seq, d)
    outs = {k: [] for k in ('pa', 'pc', 'pd', 'pv', 'sc', 'sv')}
    tm_all = lambda s: s.transpose(0, 2, 1, 3).reshape(depth, -1, dq)
    sa_all, sd_all = tm_all(state_conv_a), tm_all(state_pool_d)
    new_sa_sd = None
    for i in range(depth):
        lw = _prep_layer(i, p, n_step)
        j = i // 2
        routed = i % 2 == 1
        if routed:
            wr = jnp.pad(w_router[j], ((0, 0), (0, LANES - N_EXPERTS)))
            wr_hi = wr.astype(BF16)
            wr_lo = (wr - wr_hi.astype(F32)).astype(BF16)
            wr_cat = jnp.concatenate([wr_hi, wr_lo], axis=1)
        hp, pa, pc, pd, pv, *route_p = _mixer_prompt(hp, lw, i == 0, alpha, wr_cat if routed else None)
        tm_state = lambda s: s.transpose(1, 0, 2).reshape(-1, dq)
        hs, sa, sc, sd, sv = _mixer_sample(hs, sa_all, tm_state(state_conv_c[i]), sd_all, new_sa_sd, i, lw,
                                           i == 0, alpha, n_seq, n_step)
        new_sa_sd = (sa, sd)
        lg, lb = ln_ffn_g[i].reshape(1, -1), ln_ffn_b[i].reshape(1, -1)
        if not routed:
            hp, hs = _ffn_dense(hp.reshape(bsz * seq, d), hs, w_ff_gate[j], w_ff_up[j], w_ff_down[j], lg, lb,
                                alpha, TM_DENSE, TF_DENSE)
            hp = hp.reshape(bsz, seq, d)
        else:
            mw = dict(w_router_cat=wr_cat, w_e_gate=w_e_gate[j], w_e_up=w_e_up[j], w_e_down=w_e_down[j])
            hp, hs = _ffn_routed([hp.reshape(bsz * seq, d), hs], route_p, mw, lg, lb, alpha, TM_EXPERT)
            hp = hp.reshape(bsz, seq, d)
        back = lambda a: a.reshape(-1, n_seq, dq).transpose(1, 0, 2)
        for k, val in zip(('pa', 'pc', 'pd', 'pv', 'sc', 'sv'), (pa, pc, pd, pv, back(sc), back(sv))):
            outs[k].append(val)
    y_sample = hs.reshape(n_step, n_seq, d).transpose(1, 0, 2)
    st = lambda k: jnp.stack(outs[k])
    back_all = lambda a: a.reshape(depth, -1, n_seq, dq).transpose(0, 2, 1, 3)
    return (hp, y_sample, st('pa'), st('pc'), st('pd'), st('pv'), back_all(new_sa_sd[0]), st('sc'),
            back_all(new_sa_sd[1]), st('sv'))
```
